```python
import math
import jax, jax.numpy as jnp
from jax import lax
import numpy as np

D_MODEL = 1024
BATCH = 8
SEQ = 4096
DEPTH = 1

N_META = 16
D_SSM = D_MODEL // 2
D_POOL = D_MODEL - D_SSM
SSM_GROUP = 16
SSM_GROUPS = D_SSM // SSM_GROUP
SSM_STATE = 64
POOL_WINDOWS = (2, 4, 8, 16)
POOL_GROUPS = len(POOL_WINDOWS)
POOL_GROUP_DIM = D_POOL // POOL_GROUPS
D_FF = ((8 * D_MODEL // 3 + 255) // 256) * 256
STEP_MIN = 1e-3
STEP_MAX = 1e-1
EPS = 1e-6

kernel_name = "hymba_s5_multiscale_pool_hybrid"


def rmsnorm(x, g):
    xf = x.astype(jnp.float32)
    return xf * lax.rsqrt(jnp.mean(xf * xf, axis=-1, keepdims=True) + EPS) * g.astype(jnp.float32)


def _complex_affine_combine(e1, e2):
    a1r, a1i, b1r, b1i = e1
    a2r, a2i, b2r, b2i = e2
    ar = a2r * a1r - a2i * a1i
    ai = a2r * a1i + a2i * a1r
    br = a2r * b1r - a2i * b1i + b2r
    bi = a2r * b1i + a2i * b1r + b2i
    return (ar, ai, br, bi)


def s5_mixer(u, lam_re, lam_im, log_step, b_re, b_im, c_re, c_im, d, glu_w, glu_b):
    L = u.shape[1]
    lr = jnp.minimum(lam_re.astype(jnp.float32), -1e-4)
    li = lam_im.astype(jnp.float32)
    step = jnp.exp(log_step.astype(jnp.float32))[:, None]
    mag = jnp.exp(lr * step)
    ang = li * step
    abr = mag * jnp.cos(ang)
    abi = mag * jnp.sin(ang)
    nr = abr - 1.0
    ni = abi
    den = lr * lr + li * li
    cr = ((nr * lr + ni * li) / den)[..., None]
    ci = ((ni * lr - nr * li) / den)[..., None]
    br = b_re.astype(jnp.float32)
    bi = b_im.astype(jnp.float32)
    bbr = cr * br - ci * bi
    bbi = cr * bi + ci * br
    uf = u.astype(jnp.float32)
    bu_r = jnp.einsum('blgh,gph->blgp', uf, bbr)
    bu_i = jnp.einsum('blgh,gph->blgp', uf, bbi)
    a_r = jnp.broadcast_to(abr[None, None], (1, L) + abr.shape)
    a_i = jnp.broadcast_to(abi[None, None], (1, L) + abi.shape)
    _, _, sr, si = lax.associative_scan(_complex_affine_combine, (a_r, a_i, bu_r, bu_i), axis=1)
    y = (jnp.einsum('blgp,ghp->blgh', sr, c_re.astype(jnp.float32))
         - jnp.einsum('blgp,ghp->blgh', si, c_im.astype(jnp.float32))
         + d.astype(jnp.float32) * uf)
    g = jax.nn.gelu(y)
    gate = jnp.einsum('blgh,ghk->blgk', g, glu_w.astype(jnp.float32)) + glu_b.astype(jnp.float32)
    return g * jax.nn.sigmoid(gate)


def pool_mixer(v, pool_w, pool_scale):
    L = v.shape[1]
    vf = v.astype(jnp.float32)
    cs = jnp.cumsum(vf, axis=1)
    t = jnp.arange(1, L + 1, dtype=jnp.float32)
    outs = []
    for k, w in enumerate(POOL_WINDOWS):
        ck = cs[:, :, k]
        lower = jnp.pad(ck, ((0, 0), (w, 0), (0, 0)))[:, :L]
        cnt = jnp.minimum(t, float(w))[None, :, None]
        outs.append((ck - lower) / cnt - vf[:, :, k])
    p = jnp.stack(outs, axis=2)
    p = jnp.einsum('blkc,kcd->blkd', p, pool_w.astype(jnp.float32))
    return p * pool_scale.astype(jnp.float32)


def _fwd_setup_inputs(seed: int = 0) -> dict:
    key = jax.random.key(seed)
    ks = jax.random.split(key, 24)
    f32 = jnp.float32
    G, H, P = SSM_GROUPS, SSM_GROUP, SSM_STATE
    n = jnp.arange(P, dtype=f32)
    x = jax.random.normal(ks[0], (BATCH, SEQ, D_MODEL), f32)
    meta_tokens = jax.random.normal(ks[1], (N_META, D_MODEL), f32)
    norm1_g = 1.0 + 0.02 * jax.random.normal(ks[2], (DEPTH, D_MODEL), f32)
    w_in = jax.random.normal(ks[3], (DEPTH, D_MODEL, D_MODEL), f32) * D_MODEL ** -0.5
    ssm_lambda_re = -0.5 + 0.01 * jax.random.normal(ks[4], (DEPTH, G, P), f32)
    ssm_lambda_im = math.pi * n + 0.01 * jax.random.normal(ks[5], (DEPTH, G, P), f32)
    ssm_log_step = jax.random.uniform(ks[6], (DEPTH, G), f32, math.log(STEP_MIN), math.log(STEP_MAX))
    ssm_b_re = jax.random.normal(ks[7], (DEPTH, G, P, H), f32) * (2.0 * H) ** -0.5
    ssm_b_im = jax.random.normal(ks[8], (DEPTH, G, P, H), f32) * (2.0 * H) ** -0.5
    ssm_c_re = jax.random.normal(ks[9], (DEPTH, G, H, P), f32) * (2.0 * P) ** -0.5 * 4.0
    ssm_c_im = jax.random.normal(ks[10], (DEPTH, G, H, P), f32) * (2.0 * P) ** -0.5 * 4.0
    ssm_d = jax.random.normal(ks[11], (DEPTH, G, H), f32)
    ssm_glu_w = jax.random.normal(ks[12], (DEPTH, G, H, H), f32) * H ** -0.5
    ssm_glu_b = 0.02 * jax.random.normal(ks[13], (DEPTH, G, H), f32)
    ssm_norm_g = 1.0 + 0.02 * jax.random.normal(ks[14], (DEPTH, D_SSM), f32)
    pool_w = jax.random.normal(ks[15], (DEPTH, POOL_GROUPS, POOL_GROUP_DIM, POOL_GROUP_DIM), f32) * POOL_GROUP_DIM ** -0.5
    pool_scale = 1.0 + 0.1 * jax.random.normal(ks[16], (DEPTH, POOL_GROUPS, POOL_GROUP_DIM), f32)
    pool_norm_g = 1.0 + 0.02 * jax.random.normal(ks[17], (DEPTH, D_POOL), f32)
    w_out = jax.random.normal(ks[18], (DEPTH, D_MODEL, D_MODEL), f32) * D_MODEL ** -0.5
    norm2_g = 1.0 + 0.02 * jax.random.normal(ks[19], (DEPTH, D_MODEL), f32)
    w_gate = jax.random.normal(ks[20], (DEPTH, D_MODEL, D_FF), f32) * D_MODEL ** -0.5
    w_up = jax.random.normal(ks[21], (DEPTH, D_MODEL, D_FF), f32) * D_MODEL ** -0.5
    w_down = jax.random.normal(ks[22], (DEPTH, D_FF, D_MODEL), f32) * D_FF ** -0.5
    final_norm_g = 1.0 + 0.02 * jax.random.normal(ks[23], (D_MODEL,), f32)
    return {"x": x, "meta_tokens": meta_tokens, "norm1_g": norm1_g, "w_in": w_in,
            "ssm_lambda_re": ssm_lambda_re, "ssm_lambda_im": ssm_lambda_im,
            "ssm_log_step": ssm_log_step, "ssm_b_re": ssm_b_re, "ssm_b_im": ssm_b_im,
            "ssm_c_re": ssm_c_re, "ssm_c_im": ssm_c_im, "ssm_d": ssm_d,
            "ssm_glu_w": ssm_glu_w, "ssm_glu_b": ssm_glu_b, "ssm_norm_g": ssm_norm_g,
            "pool_w": pool_w, "pool_scale": pool_scale, "pool_norm_g": pool_norm_g,
            "w_out": w_out, "norm2_g": norm2_g, "w_gate": w_gate, "w_up": w_up,
            "w_down": w_down, "final_norm_g": final_norm_g}


def _fwd_reference(x, meta_tokens, norm1_g, w_in, ssm_lambda_re, ssm_lambda_im, ssm_log_step,
              ssm_b_re, ssm_b_im, ssm_c_re, ssm_c_im, ssm_d, ssm_glu_w, ssm_glu_b,
              ssm_norm_g, pool_w, pool_scale, pool_norm_g, w_out, norm2_g, w_gate, w_up,
              w_down, final_norm_g):
    B = x.shape[0]
    meta = jnp.broadcast_to(meta_tokens.astype(jnp.float32)[None], (B, N_META, D_MODEL))
    h = jnp.concatenate([meta, x.astype(jnp.float32)], axis=1)
    L = h.shape[1]
    for i in range(DEPTH):
        n1 = rmsnorm(h, norm1_g[i])
        proj = n1 @ w_in[i].astype(jnp.float32)
        u = proj[..., :D_SSM].reshape(B, L, SSM_GROUPS, SSM_GROUP)
        v = proj[..., D_SSM:].reshape(B, L, POOL_GROUPS, POOL_GROUP_DIM)
        ys = s5_mixer(u, ssm_lambda_re[i], ssm_lambda_im[i], ssm_log_step[i], ssm_b_re[i],
                      ssm_b_im[i], ssm_c_re[i], ssm_c_im[i], ssm_d[i], ssm_glu_w[i],
                      ssm_glu_b[i]).reshape(B, L, D_SSM)
        yp = pool_mixer(v, pool_w[i], pool_scale[i]).reshape(B, L, D_POOL)
        mixed = jnp.concatenate([rmsnorm(ys, ssm_norm_g[i]), rmsnorm(yp, pool_norm_g[i])], axis=-1)
        h = h + mixed @ w_out[i].astype(jnp.float32)
        n2 = rmsnorm(h, norm2_g[i])
        ff = jax.nn.silu(n2 @ w_gate[i].astype(jnp.float32)) * (n2 @ w_up[i].astype(jnp.float32))
        h = h + ff @ w_down[i].astype(jnp.float32)
    out = rmsnorm(h, final_norm_g)[:, N_META:]
    return out.astype(x.dtype)


import jax as _jax
import jax.numpy as _jnp

TWIN_FORMAT = 'train_step'
FWD_PARAMS = ['x', 'meta_tokens', 'norm1_g', 'w_in', 'ssm_lambda_re', 'ssm_lambda_im', 'ssm_log_step', 'ssm_b_re', 'ssm_b_im', 'ssm_c_re', 'ssm_c_im', 'ssm_d', 'ssm_glu_w', 'ssm_glu_b', 'ssm_norm_g', 'pool_w', 'pool_scale', 'pool_norm_g', 'w_out', 'norm2_g', 'w_gate', 'w_up', 'w_down', 'final_norm_g']
TWIN_WEIGHTS = ['meta_tokens', 'norm1_g', 'w_in', 'ssm_lambda_re', 'ssm_lambda_im', 'ssm_log_step', 'ssm_b_re', 'ssm_b_im', 'ssm_c_re', 'ssm_c_im', 'ssm_d', 'ssm_glu_w', 'ssm_glu_b', 'ssm_norm_g', 'pool_w', 'pool_scale', 'pool_norm_g', 'w_out', 'norm2_g', 'w_gate', 'w_up', 'w_down', 'final_norm_g']
TWIN_DIFF_INPUT = 'x'
TWIN_INPUTS = ['x', 'meta_tokens', 'norm1_g', 'w_in', 'ssm_lambda_re', 'ssm_lambda_im', 'ssm_log_step', 'ssm_b_re', 'ssm_b_im', 'ssm_c_re', 'ssm_c_im', 'ssm_d', 'ssm_glu_w', 'ssm_glu_b', 'ssm_norm_g', 'pool_w', 'pool_scale', 'pool_norm_g', 'w_out', 'norm2_g', 'w_gate', 'w_up', 'w_down', 'final_norm_g', 'loss_target', 'm_meta_tokens', 'm_norm1_g', 'm_w_in', 'm_ssm_lambda_re', 'm_ssm_lambda_im', 'm_ssm_log_step', 'm_ssm_b_re', 'm_ssm_b_im', 'm_ssm_c_re', 'm_ssm_c_im', 'm_ssm_d', 'm_ssm_glu_w', 'm_ssm_glu_b', 'm_ssm_norm_g', 'm_pool_w', 'm_pool_scale', 'm_pool_norm_g', 'm_w_out', 'm_norm2_g', 'm_w_gate', 'm_w_up', 'm_w_down', 'm_final_norm_g', 'v_meta_tokens', 'v_norm1_g', 'v_w_in', 'v_ssm_lambda_re', 'v_ssm_lambda_im', 'v_ssm_log_step', 'v_ssm_b_re', 'v_ssm_b_im', 'v_ssm_c_re', 'v_ssm_c_im', 'v_ssm_d', 'v_ssm_glu_w', 'v_ssm_glu_b', 'v_ssm_norm_g', 'v_pool_w', 'v_pool_scale', 'v_pool_norm_g', 'v_w_out', 'v_norm2_g', 'v_w_gate', 'v_w_up', 'v_w_down', 'v_final_norm_g']
TWIN_OUTPUTS = ['loss', 'grad_x', 'grad_meta_tokens', 'grad_norm1_g', 'grad_w_in', 'grad_ssm_lambda_re', 'grad_ssm_lambda_im', 'grad_ssm_log_step', 'grad_ssm_b_re', 'grad_ssm_b_im', 'grad_ssm_c_re', 'grad_ssm_c_im', 'grad_ssm_d', 'grad_ssm_glu_w', 'grad_ssm_glu_b', 'grad_ssm_norm_g', 'grad_pool_w', 'grad_pool_scale', 'grad_pool_norm_g', 'grad_w_out', 'grad_norm2_g', 'grad_w_gate', 'grad_w_up', 'grad_w_down', 'grad_final_norm_g', 'delta_meta_tokens', 'delta_norm1_g', 'delta_w_in', 'delta_ssm_lambda_re', 'delta_ssm_lambda_im', 'delta_ssm_log_step', 'delta_ssm_b_re', 'delta_ssm_b_im', 'delta_ssm_c_re', 'delta_ssm_c_im', 'delta_ssm_d', 'delta_ssm_glu_w', 'delta_ssm_glu_b', 'delta_ssm_norm_g', 'delta_pool_w', 'delta_pool_scale', 'delta_pool_norm_g', 'delta_w_out', 'delta_norm2_g', 'delta_w_gate', 'delta_w_up', 'delta_w_down', 'delta_final_norm_g', 'new_m_meta_tokens', 'new_m_norm1_g', 'new_m_w_in', 'new_m_ssm_lambda_re', 'new_m_ssm_lambda_im', 'new_m_ssm_log_step', 'new_m_ssm_b_re', 'new_m_ssm_b_im', 'new_m_ssm_c_re', 'new_m_ssm_c_im', 'new_m_ssm_d', 'new_m_ssm_glu_w', 'new_m_ssm_glu_b', 'new_m_ssm_norm_g', 'new_m_pool_w', 'new_m_pool_scale', 'new_m_pool_norm_g', 'new_m_w_out', 'new_m_norm2_g', 'new_m_w_gate', 'new_m_w_up', 'new_m_w_down', 'new_m_final_norm_g', 'new_v_meta_tokens', 'new_v_norm1_g', 'new_v_w_in', 'new_v_ssm_lambda_re', 'new_v_ssm_lambda_im', 'new_v_ssm_log_step', 'new_v_ssm_b_re', 'new_v_ssm_b_im', 'new_v_ssm_c_re', 'new_v_ssm_c_im', 'new_v_ssm_d', 'new_v_ssm_glu_w', 'new_v_ssm_glu_b', 'new_v_ssm_norm_g', 'new_v_pool_w', 'new_v_pool_scale', 'new_v_pool_norm_g', 'new_v_w_out', 'new_v_norm2_g', 'new_v_w_gate', 'new_v_w_up', 'new_v_w_down', 'new_v_final_norm_g']
TWIN_LEAF_KINDS = {'loss': 'loss', 'grad_x': 'grad_x', 'grad_meta_tokens': 'grad_w', 'grad_norm1_g': 'grad_w', 'grad_w_in': 'grad_w', 'grad_ssm_lambda_re': 'grad_w', 'grad_ssm_lambda_im': 'grad_w', 'grad_ssm_log_step': 'grad_w', 'grad_ssm_b_re': 'grad_w', 'grad_ssm_b_im': 'grad_w', 'grad_ssm_c_re': 'grad_w', 'grad_ssm_c_im': 'grad_w', 'grad_ssm_d': 'grad_w', 'grad_ssm_glu_w': 'grad_w', 'grad_ssm_glu_b': 'grad_w', 'grad_ssm_norm_g': 'grad_w', 'grad_pool_w': 'grad_w', 'grad_pool_scale': 'grad_w', 'grad_pool_norm_g': 'grad_w', 'grad_w_out': 'grad_w', 'grad_norm2_g': 'grad_w', 'grad_w_gate': 'grad_w', 'grad_w_up': 'grad_w', 'grad_w_down': 'grad_w', 'grad_final_norm_g': 'grad_w', 'delta_meta_tokens': 'delta_w', 'delta_norm1_g': 'delta_w', 'delta_w_in': 'delta_w', 'delta_ssm_lambda_re': 'delta_w', 'delta_ssm_lambda_im': 'delta_w', 'delta_ssm_log_step': 'delta_w', 'delta_ssm_b_re': 'delta_w', 'delta_ssm_b_im': 'delta_w', 'delta_ssm_c_re': 'delta_w', 'delta_ssm_c_im': 'delta_w', 'delta_ssm_d': 'delta_w', 'delta_ssm_glu_w': 'delta_w', 'delta_ssm_glu_b': 'delta_w', 'delta_ssm_norm_g': 'delta_w', 'delta_pool_w': 'delta_w', 'delta_pool_scale': 'delta_w', 'delta_pool_norm_g': 'delta_w', 'delta_w_out': 'delta_w', 'delta_norm2_g': 'delta_w', 'delta_w_gate': 'delta_w', 'delta_w_up': 'delta_w', 'delta_w_down': 'delta_w', 'delta_final_norm_g': 'delta_w', 'new_m_meta_tokens': 'new_m', 'new_m_norm1_g': 'new_m', 'new_m_w_in': 'new_m', 'new_m_ssm_lambda_re': 'new_m', 'new_m_ssm_lambda_im': 'new_m', 'new_m_ssm_log_step': 'new_m', 'new_m_ssm_b_re': 'new_m', 'new_m_ssm_b_im': 'new_m', 'new_m_ssm_c_re': 'new_m', 'new_m_ssm_c_im': 'new_m', 'new_m_ssm_d': 'new_m', 'new_m_ssm_glu_w': 'new_m', 'new_m_ssm_glu_b': 'new_m', 'new_m_ssm_norm_g': 'new_m', 'new_m_pool_w': 'new_m', 'new_m_pool_scale': 'new_m', 'new_m_pool_norm_g': 'new_m', 'new_m_w_out': 'new_m', 'new_m_norm2_g': 'new_m', 'new_m_w_gate': 'new_m', 'new_m_w_up': 'new_m', 'new_m_w_down': 'new_m', 'new_m_final_norm_g': 'new_m', 'new_v_meta_tokens': 'new_v', 'new_v_norm1_g': 'new_v', 'new_v_w_in': 'new_v', 'new_v_ssm_lambda_re': 'new_v', 'new_v_ssm_lambda_im': 'new_v', 'new_v_ssm_log_step': 'new_v', 'new_v_ssm_b_re': 'new_v', 'new_v_ssm_b_im': 'new_v', 'new_v_ssm_c_re': 'new_v', 'new_v_ssm_c_im': 'new_v', 'new_v_ssm_d': 'new_v', 'new_v_ssm_glu_w': 'new_v', 'new_v_ssm_glu_b': 'new_v', 'new_v_ssm_norm_g': 'new_v', 'new_v_pool_w': 'new_v', 'new_v_pool_scale': 'new_v', 'new_v_pool_norm_g': 'new_v', 'new_v_w_out': 'new_v', 'new_v_norm2_g': 'new_v', 'new_v_w_gate': 'new_v', 'new_v_w_up': 'new_v', 'new_v_w_down': 'new_v', 'new_v_final_norm_g': 'new_v'}


def _forward(args):
    return _fwd_reference(*[args[k] for k in FWD_PARAMS])


def _output_shape():
    out = _jax.eval_shape(lambda: _forward(_fwd_setup_inputs(0)))
    return out.shape, out.dtype

N_MICROBATCH = 1
ADAM_LR = 0.001
ADAM_B1 = 0.9
ADAM_B2 = 0.999
ADAM_EPS = 1e-08
ADAM_WD = 0.01
ADAM_STEP = 10
PER_EXAMPLE_BATCH_AXIS = {'x': 0, 'loss_target': 0}
SHARED_INPUTS = []
_WEIGHT_DTYPES = {'meta_tokens': _jnp.float32, 'norm1_g': _jnp.float32, 'w_in': _jnp.float32, 'ssm_lambda_re': _jnp.float32, 'ssm_lambda_im': _jnp.float32, 'ssm_log_step': _jnp.float32, 'ssm_b_re': _jnp.float32, 'ssm_b_im': _jnp.float32, 'ssm_c_re': _jnp.float32, 'ssm_c_im': _jnp.float32, 'ssm_d': _jnp.float32, 'ssm_glu_w': _jnp.float32, 'ssm_glu_b': _jnp.float32, 'ssm_norm_g': _jnp.float32, 'pool_w': _jnp.float32, 'pool_scale': _jnp.float32, 'pool_norm_g': _jnp.float32, 'w_out': _jnp.float32, 'norm2_g': _jnp.float32, 'w_gate': _jnp.float32, 'w_up': _jnp.float32, 'w_down': _jnp.float32, 'final_norm_g': _jnp.float32}
MOMENT_SCALE = {'meta_tokens': 1.666805e-03, 'norm1_g': 1.441837e-01, 'w_in': 1.410081e-01, 'ssm_lambda_re': 2.882431e-02, 'ssm_lambda_im': 2.570812e-02, 'ssm_log_step': 2.814852e+01, 'ssm_b_re': 1.921134e-02, 'ssm_b_im': 1.875512e-02, 'ssm_c_re': 9.666890e-03, 'ssm_c_im': 9.164007e-03, 'ssm_d': 1.438515e-01, 'ssm_glu_w': 6.468789e-02, 'ssm_glu_b': 6.244077e-02, 'ssm_norm_g': 1.559843e-01, 'pool_w': 1.374831e-01, 'pool_scale': 1.407157e-01, 'pool_norm_g': 1.415556e-01, 'w_out': 1.389699e-01, 'norm2_g': 1.023670e-01, 'w_gate': 4.272735e-02, 'w_up': 4.160799e-02, 'w_down': 6.865631e-02, 'final_norm_g': 3.210278e+01}


def _to_microbatches(a, axis):
    t = _jnp.moveaxis(a, axis, 0)
    t = t.reshape((N_MICROBATCH, t.shape[0] // N_MICROBATCH) + t.shape[1:])
    return _jnp.moveaxis(t, 1, axis + 1)


def setup_inputs(seed: int = 0) -> dict:
    inp = _fwd_setup_inputs(seed)
    key = _jax.random.fold_in(_jax.random.key(seed), 7919)
    shape, _ = _output_shape()
    out = dict(inp)
    out["loss_target"] = _jax.random.normal(_jax.random.fold_in(key, 0), shape, _jnp.float32)
    for i, name in enumerate(TWIN_WEIGHTS):
        w = inp[name].astype(_jnp.float32)
        if MOMENT_SCALE is None:
            s = _jnp.sqrt(_jnp.mean(_jnp.square(w)) + 1e-30)
        else:
            s = MOMENT_SCALE[name]
        km, kv = _jax.random.split(_jax.random.fold_in(key, i + 1))
        out[name] = w
        out["m_" + name] = s * _jax.random.normal(km, w.shape, _jnp.float32)
        out["v_" + name] = (s * s) * _jax.random.uniform(kv, w.shape, _jnp.float32, 0.5, 1.5)
    if N_MICROBATCH > 1:
        for name, axis in PER_EXAMPLE_BATCH_AXIS.items():
            out[name] = _to_microbatches(out[name], axis)
    return {'x': out['x'], 'meta_tokens': out['meta_tokens'], 'norm1_g': out['norm1_g'], 'w_in': out['w_in'], 'ssm_lambda_re': out['ssm_lambda_re'], 'ssm_lambda_im': out['ssm_lambda_im'], 'ssm_log_step': out['ssm_log_step'], 'ssm_b_re': out['ssm_b_re'], 'ssm_b_im': out['ssm_b_im'], 'ssm_c_re': out['ssm_c_re'], 'ssm_c_im': out['ssm_c_im'], 'ssm_d': out['ssm_d'], 'ssm_glu_w': out['ssm_glu_w'], 'ssm_glu_b': out['ssm_glu_b'], 'ssm_norm_g': out['ssm_norm_g'], 'pool_w': out['pool_w'], 'pool_scale': out['pool_scale'], 'pool_norm_g': out['pool_norm_g'], 'w_out': out['w_out'], 'norm2_g': out['norm2_g'], 'w_gate': out['w_gate'], 'w_up': out['w_up'], 'w_down': out['w_down'], 'final_norm_g': out['final_norm_g'], 'loss_target': out['loss_target'], 'm_meta_tokens': out['m_meta_tokens'], 'm_norm1_g': out['m_norm1_g'], 'm_w_in': out['m_w_in'], 'm_ssm_lambda_re': out['m_ssm_lambda_re'], 'm_ssm_lambda_im': out['m_ssm_lambda_im'], 'm_ssm_log_step': out['m_ssm_log_step'], 'm_ssm_b_re': out['m_ssm_b_re'], 'm_ssm_b_im': out['m_ssm_b_im'], 'm_ssm_c_re': out['m_ssm_c_re'], 'm_ssm_c_im': out['m_ssm_c_im'], 'm_ssm_d': out['m_ssm_d'], 'm_ssm_glu_w': out['m_ssm_glu_w'], 'm_ssm_glu_b': out['m_ssm_glu_b'], 'm_ssm_norm_g': out['m_ssm_norm_g'], 'm_pool_w': out['m_pool_w'], 'm_pool_scale': out['m_pool_scale'], 'm_pool_norm_g': out['m_pool_norm_g'], 'm_w_out': out['m_w_out'], 'm_norm2_g': out['m_norm2_g'], 'm_w_gate': out['m_w_gate'], 'm_w_up': out['m_w_up'], 'm_w_down': out['m_w_down'], 'm_final_norm_g': out['m_final_norm_g'], 'v_meta_tokens': out['v_meta_tokens'], 'v_norm1_g': out['v_norm1_g'], 'v_w_in': out['v_w_in'], 'v_ssm_lambda_re': out['v_ssm_lambda_re'], 'v_ssm_lambda_im': out['v_ssm_lambda_im'], 'v_ssm_log_step': out['v_ssm_log_step'], 'v_ssm_b_re': out['v_ssm_b_re'], 'v_ssm_b_im': out['v_ssm_b_im'], 'v_ssm_c_re': out['v_ssm_c_re'], 'v_ssm_c_im': out['v_ssm_c_im'], 'v_ssm_d': out['v_ssm_d'], 'v_ssm_glu_w': out['v_ssm_glu_w'], 'v_ssm_glu_b': out['v_ssm_glu_b'], 'v_ssm_norm_g': out['v_ssm_norm_g'], 'v_pool_w': out['v_pool_w'], 'v_pool_scale': out['v_pool_scale'], 'v_pool_norm_g': out['v_pool_norm_g'], 'v_w_out': out['v_w_out'], 'v_norm2_g': out['v_norm2_g'], 'v_w_gate': out['v_w_gate'], 'v_w_up': out['v_w_up'], 'v_w_down': out['v_w_down'], 'v_final_norm_g': out['v_final_norm_g']}


def _loss(weights, diff, rest, loss_target):
    with _jax.named_scope("forward"):
        args = {**rest, TWIN_DIFF_INPUT: diff, **{k: w.astype(_WEIGHT_DTYPES[k]) for k, w in weights.items()}}
        y = _forward(args)
    with _jax.named_scope("loss_head"):
        err = _jnp.square(y.astype(_jnp.float32) - loss_target)
        return 0.5 * _jnp.sum(_jnp.mean(err, axis=-1)) if err.ndim else 0.5 * err


def _adamw(w, g, m, v):
    m = ADAM_B1 * m + (1.0 - ADAM_B1) * g
    v = ADAM_B2 * v + (1.0 - ADAM_B2) * _jnp.square(g)
    m_hat = m / (1.0 - ADAM_B1 ** ADAM_STEP)
    v_hat = v / (1.0 - ADAM_B2 ** ADAM_STEP)
    delta = -ADAM_LR * (m_hat / (_jnp.sqrt(v_hat) + ADAM_EPS) + ADAM_WD * w)
    return delta, m, v


def reference(x, meta_tokens, norm1_g, w_in, ssm_lambda_re, ssm_lambda_im, ssm_log_step, ssm_b_re, ssm_b_im, ssm_c_re, ssm_c_im, ssm_d, ssm_glu_w, ssm_glu_b, ssm_norm_g, pool_w, pool_scale, pool_norm_g, w_out, norm2_g, w_gate, w_up, w_down, final_norm_g, loss_target, m_meta_tokens, m_norm1_g, m_w_in, m_ssm_lambda_re, m_ssm_lambda_im, m_ssm_log_step, m_ssm_b_re, m_ssm_b_im, m_ssm_c_re, m_ssm_c_im, m_ssm_d, m_ssm_glu_w, m_ssm_glu_b, m_ssm_norm_g, m_pool_w, m_pool_scale, m_pool_norm_g, m_w_out, m_norm2_g, m_w_gate, m_w_up, m_w_down, m_final_norm_g, v_meta_tokens, v_norm1_g, v_w_in, v_ssm_lambda_re, v_ssm_lambda_im, v_ssm_log_step, v_ssm_b_re, v_ssm_b_im, v_ssm_c_re, v_ssm_c_im, v_ssm_d, v_ssm_glu_w, v_ssm_glu_b, v_ssm_norm_g, v_pool_w, v_pool_scale, v_pool_norm_g, v_w_out, v_norm2_g, v_w_gate, v_w_up, v_w_down, v_final_norm_g):
    given = dict(x=x, meta_tokens=meta_tokens, norm1_g=norm1_g, w_in=w_in, ssm_lambda_re=ssm_lambda_re, ssm_lambda_im=ssm_lambda_im, ssm_log_step=ssm_log_step, ssm_b_re=ssm_b_re, ssm_b_im=ssm_b_im, ssm_c_re=ssm_c_re, ssm_c_im=ssm_c_im, ssm_d=ssm_d, ssm_glu_w=ssm_glu_w, ssm_glu_b=ssm_glu_b, ssm_norm_g=ssm_norm_g, pool_w=pool_w, pool_scale=pool_scale, pool_norm_g=pool_norm_g, w_out=w_out, norm2_g=norm2_g, w_gate=w_gate, w_up=w_up, w_down=w_down, final_norm_g=final_norm_g, loss_target=loss_target, m_meta_tokens=m_meta_tokens, m_norm1_g=m_norm1_g, m_w_in=m_w_in, m_ssm_lambda_re=m_ssm_lambda_re, m_ssm_lambda_im=m_ssm_lambda_im, m_ssm_log_step=m_ssm_log_step, m_ssm_b_re=m_ssm_b_re, m_ssm_b_im=m_ssm_b_im, m_ssm_c_re=m_ssm_c_re, m_ssm_c_im=m_ssm_c_im, m_ssm_d=m_ssm_d, m_ssm_glu_w=m_ssm_glu_w, m_ssm_glu_b=m_ssm_glu_b, m_ssm_norm_g=m_ssm_norm_g, m_pool_w=m_pool_w, m_pool_scale=m_pool_scale, m_pool_norm_g=m_pool_norm_g, m_w_out=m_w_out, m_norm2_g=m_norm2_g, m_w_gate=m_w_gate, m_w_up=m_w_up, m_w_down=m_w_down, m_final_norm_g=m_final_norm_g, v_meta_tokens=v_meta_tokens, v_norm1_g=v_norm1_g, v_w_in=v_w_in, v_ssm_lambda_re=v_ssm_lambda_re, v_ssm_lambda_im=v_ssm_lambda_im, v_ssm_log_step=v_ssm_log_step, v_ssm_b_re=v_ssm_b_re, v_ssm_b_im=v_ssm_b_im, v_ssm_c_re=v_ssm_c_re, v_ssm_c_im=v_ssm_c_im, v_ssm_d=v_ssm_d, v_ssm_glu_w=v_ssm_glu_w, v_ssm_glu_b=v_ssm_glu_b, v_ssm_norm_g=v_ssm_norm_g, v_pool_w=v_pool_w, v_pool_scale=v_pool_scale, v_pool_norm_g=v_pool_norm_g, v_w_out=v_w_out, v_norm2_g=v_norm2_g, v_w_gate=v_w_gate, v_w_up=v_w_up, v_w_down=v_w_down, v_final_norm_g=v_final_norm_g)
    weights = {n: given[n] for n in TWIN_WEIGHTS}
    shared = {n: given[n] for n in SHARED_INPUTS}
    per_example = {n: given[n] for n in ['x']}
    grad_fn = _jax.value_and_grad(_loss, argnums=(0, 1))

    def one_microbatch(ex, loss_target):
        ex = dict(ex)
        diff = ex.pop(TWIN_DIFF_INPUT)
        return grad_fn(weights, diff, {**shared, **ex}, loss_target)

    if N_MICROBATCH == 1:
        loss, (grad_w, grad_x) = one_microbatch(per_example, given["loss_target"])
    else:
        def body(carry, xs):
            loss_sum, grad_sum = carry
            l_k, (gw_k, gx_k) = one_microbatch(xs[0], xs[1])
            with _jax.named_scope("update"):
                return (loss_sum + l_k, _jax.tree.map(_jnp.add, grad_sum, gw_k)), gx_k

        init = (_jnp.zeros((), _jnp.float32), _jax.tree.map(_jnp.zeros_like, weights))
        (loss, grad_w), grad_x = _jax.lax.scan(body, init, (per_example, given["loss_target"]))
    with _jax.named_scope("update"):
        delta_w, new_m, new_v = {}, {}, {}
        for n in TWIN_WEIGHTS:
            delta_w[n], new_m[n], new_v[n] = _adamw(weights[n], grad_w[n], given["m_" + n], given["v_" + n])
    return (loss, grad_x, *[grad_w[n] for n in TWIN_WEIGHTS], *[delta_w[n] for n in TWIN_WEIGHTS],
            *[new_m[n] for n in TWIN_WEIGHTS], *[new_v[n] for n in TWIN_WEIGHTS])
```

```python
import functools

import jax
import jax.numpy as jnp
from jax import lax
from jax.experimental import pallas as pl
from jax.experimental.pallas import tpu as pltpu

F32 = jnp.float32
BF16 = jnp.bfloat16

N_DEV = 8
D_MODEL = 1024
D_SSM = 512
SSM_GROUP = 16
SSM_STATE = 64
SSM_GROUPS = 32
POOL_GROUPS = 4
POOL_DIM = 128
COL_U = 128
COL_S = 512
N_COL = D_SSM // COL_U
GROUPS_PER_COL = COL_U // SSM_GROUP
FF_SHARD = 352
FF_PAD = 384
TM = 256
HEAD = TM
SUBLANES = 8
POOL_HALO = 128
EPS = 1e-6
STEP_FLOOR = -1e-4
VMEM_LIMIT = 60 * 1024 * 1024

ADAM_LR = 0.001
ADAM_B1 = 0.9
ADAM_B2 = 0.999
ADAM_EPS = 1e-08
ADAM_WD = 0.01
ADAM_STEP = 10

MESH_ID = pl.DeviceIdType.MESH
ANY = pl.BlockSpec(memory_space=pl.ANY)

SMALL_NAMES = ("norm1_g", "ssm_lambda_re", "ssm_lambda_im", "ssm_log_step", "ssm_b_re", "ssm_b_im",
               "ssm_c_re", "ssm_c_im", "ssm_d", "ssm_glu_w", "ssm_glu_b", "ssm_norm_g", "pool_w",
               "pool_scale", "pool_norm_g", "norm2_g", "final_norm_g")
WEIGHT_NAMES = ("meta_tokens", "norm1_g", "w_in", "ssm_lambda_re", "ssm_lambda_im", "ssm_log_step",
                "ssm_b_re", "ssm_b_im", "ssm_c_re", "ssm_c_im", "ssm_d", "ssm_glu_w", "ssm_glu_b",
                "ssm_norm_g", "pool_w", "pool_scale", "pool_norm_g", "w_out", "norm2_g", "w_gate",
                "w_up", "w_down", "final_norm_g")
PACK_LANES = 128
PACK_UNIT = 8 * PACK_LANES


def _dot(a, b):
    return jnp.dot(a.astype(BF16), b.astype(BF16), preferred_element_type=F32)


def _dot_nt(a, b):
    return lax.dot_general(a.astype(BF16), b.astype(BF16), (((1,), (1,)), ((), ())), preferred_element_type=F32)


def _dot_tn(a, b):
    return lax.dot_general(a.astype(BF16), b.astype(BF16), (((0,), (0,)), ((), ())), preferred_element_type=F32)


def _sigmoid(x):
    return 1.0 / (1.0 + jnp.exp(-x))


def _rstd(x):
    return lax.rsqrt(jnp.mean(x * x, axis=-1, keepdims=True) + EPS)


def _rms_bwd(dy, xhat, r, g):
    dxh = dy * g
    dx = r * (dxh - xhat * jnp.mean(dxh * xhat, axis=-1, keepdims=True))
    return dx, jnp.sum(dy * xhat, axis=0, keepdims=True)


def _params(sem, vmem=None):
    return pltpu.CompilerParams(dimension_semantics=sem, vmem_limit_bytes=vmem)


def _const(shape):
    return pl.BlockSpec(shape, lambda *_: (0,) * len(shape))


def _xrow(i):
    return (jnp.maximum(i - 1, 0), 0)


def _all_to_all(arrays, scatter, name):
    n_arr = len(arrays)

    def body(*refs):
        ins, outs = refs[:n_arr], refs[n_arr:2 * n_arr]
        send_sems, recv_sems, local_sems = refs[2 * n_arr:]
        x, y, c = lax.axis_index("x"), lax.axis_index("y"), lax.axis_index("c")
        me = 4 * x + 2 * y + c
        copies = []
        for a in range(n_arr):
            own = ins[a].at[me] if scatter[a] else ins[a]
            local = pltpu.make_async_copy(own, outs[a].at[me], local_sems.at[a])
            local.start()
            copies.append(local)
            for k in range(1, N_DEV):
                px = 1 - x if k & 4 else x
                py = 1 - y if k & 2 else y
                pc = 1 - c if k & 1 else c
                peer = 4 * px + 2 * py + pc
                src = ins[a].at[peer] if scatter[a] else ins[a]
                cp = pltpu.make_async_remote_copy(
                    src_ref=src, dst_ref=outs[a].at[me], send_sem=send_sems.at[a, k - 1],
                    recv_sem=recv_sems.at[a, k - 1], device_id=(px, py, pc), device_id_type=MESH_ID)
                cp.start()
                copies.append(cp)
        for cp in copies:
            cp.wait()

    out_shape = []
    for arr, sc in zip(arrays, scatter):
        shape = arr.shape if sc else (N_DEV,) + arr.shape
        out_shape.append(jax.ShapeDtypeStruct(shape, arr.dtype))
    return pl.pallas_call(
        body, name=name, out_shape=out_shape,
        in_specs=[ANY] * n_arr, out_specs=[ANY] * n_arr,
        scratch_shapes=[pltpu.SemaphoreType.DMA((n_arr, N_DEV - 1)),
                        pltpu.SemaphoreType.DMA((n_arr, N_DEV - 1)),
                        pltpu.SemaphoreType.DMA((n_arr,))],
    )(*arrays)


def _adamw(slots, w, m, v, tile_rows, name):
    n, rows, cols = slots.shape
    bc1 = 1.0 - ADAM_B1 ** ADAM_STEP
    bc2 = 1.0 - ADAM_B2 ** ADAM_STEP

    def body(s_ref, w_ref, m_ref, v_ref, g_ref, d_ref, nm_ref, nv_ref):
        g = s_ref[0]
        for s in range(1, n):
            g = g + s_ref[s]
        nm = ADAM_B1 * m_ref[...] + (1.0 - ADAM_B1) * g
        nv = ADAM_B2 * v_ref[...] + (1.0 - ADAM_B2) * (g * g)
        g_ref[...] = g
        nm_ref[...] = nm
        nv_ref[...] = nv
        d_ref[...] = -ADAM_LR * ((nm / bc1) / (jnp.sqrt(nv / bc2) + ADAM_EPS) + ADAM_WD * w_ref[...])

    tile = pl.BlockSpec((tile_rows, cols), lambda i: (i, 0))
    return pl.pallas_call(
        body, name=name, grid=(rows // tile_rows,),
        in_specs=[pl.BlockSpec((n, tile_rows, cols), lambda i: (0, i, 0)), tile, tile, tile],
        out_specs=[tile] * 4, out_shape=[jax.ShapeDtypeStruct((rows, cols), F32)] * 4,
        compiler_params=_params(("parallel",), VMEM_LIMIT),
    )(slots, w, m, v)


def _disc_a(lam_re, lam_im, log_step):
    lr = jnp.minimum(lam_re, STEP_FLOOR)
    step = jnp.exp(log_step)
    mag = jnp.exp(lr * step)
    ang = lam_im * step
    abr = mag * jnp.cos(ang)
    abi = mag * jnp.sin(ang)
    nr = abr - 1.0
    den = lr * lr + lam_im * lam_im
    cr = (nr * lr + abi * lam_im) / den
    ci = (abi * lr - nr * lam_im) / den
    return abr, abi, cr, ci


def _disc_b(cr, ci, b_re, b_im):
    return cr * b_re - ci * b_im, cr * b_im + ci * b_re


def _s5_disc_a(lam_re, lam_im, log_step):
    def body(lr_ref, li_ref, ls_ref, *outs):
        for o, val in zip(outs, _disc_a(lr_ref[...], li_ref[...], ls_ref[...])):
            o[...] = val
    return pl.pallas_call(body, name="s5_disc_a", out_shape=[jax.ShapeDtypeStruct(lam_re.shape, F32)] * 4)(
        lam_re, lam_im, log_step)


def _s5_disc_a_bwd(lam_re, lam_im, log_step, cts):
    def body(lr_ref, li_ref, ls_ref, c0, c1, c2, c3, dlr_ref, dli_ref, dls_ref):
        _, vjp = jax.vjp(_disc_a, lr_ref[...], li_ref[...], ls_ref[...])
        dlr, dli, dls = vjp((c0[...], c1[...], c2[...], c3[...]))
        dlr_ref[...] = dlr
        dli_ref[...] = dli
        dls_ref[...] = dls
    return pl.pallas_call(
        body, name="s5_disc_a_bwd",
        out_shape=[jax.ShapeDtypeStruct(lam_re.shape, F32), jax.ShapeDtypeStruct(lam_re.shape, F32),
                   jax.ShapeDtypeStruct(log_step.shape, F32)])(lam_re, lam_im, log_step, *cts)


def _s5_disc_b(cr, ci, b_re, b_im):
    def body(cr_ref, ci_ref, br_ref, bi_ref, o_re, o_im):
        o_re[...], o_im[...] = _disc_b(cr_ref[...], ci_ref[...], br_ref[...], bi_ref[...])
    return pl.pallas_call(body, name="s5_disc_b", out_shape=[jax.ShapeDtypeStruct(b_re.shape, F32)] * 2)(
        cr, ci, b_re, b_im)


def _s5_disc_b_bwd(cr, ci, b_re, b_im, d_re, d_im):
    def body(cr_ref, ci_ref, br_ref, bi_ref, dr_ref, di_ref, dcr_ref, dci_ref, dbr_ref, dbi_ref):
        _, vjp = jax.vjp(_disc_b, cr_ref[...], ci_ref[...], br_ref[...], bi_ref[...])
        dcr_ref[...], dci_ref[...], dbr_ref[...], dbi_ref[...] = vjp((dr_ref[...], di_ref[...]))
    return pl.pallas_call(
        body, name="s5_disc_b_bwd",
        out_shape=[jax.ShapeDtypeStruct(cr.shape, F32)] * 2 + [jax.ShapeDtypeStruct(b_re.shape, F32)] * 2)(
            cr, ci, b_re, b_im, d_re, d_im)


def _cmul(ar, ai, br, bi):
    return ar * br - ai * bi, ar * bi + ai * br


def _cpow(ar, ai, n):
    rr, ri = jnp.ones_like(ar), jnp.zeros_like(ai)
    while n:
        if n & 1:
            rr, ri = _cmul(rr, ri, ar, ai)
        n >>= 1
        if n:
            ar, ai = _cmul(ar, ai, ar, ai)
    return rr, ri


def _tile_rows(i):
    return pl.ds(pl.multiple_of(i * SUBLANES, SUBLANES), SUBLANES)


def _segment_scan(z_re, z_im, ar, ai, lseg, reverse, visit=None):
    shape = (SUBLANES, z_re.shape[1])
    arb = jnp.broadcast_to(ar, shape)
    aib = jnp.broadcast_to(ai, shape)
    zero = jnp.zeros(shape, F32)
    row = lax.broadcasted_iota(jnp.int32, shape, 0)

    def tile_of(k):
        return lseg - 1 - k if reverse else k

    def advance(k, sr, si):
        rows = _tile_rows(tile_of(k))
        nr, ni = _cmul(arb, aib, sr, si)
        return rows, nr + z_re[rows, :], ni + z_im[rows, :]

    def first_pass(k, carry):
        _, nr, ni = advance(k, *carry)
        return nr, ni

    fr, fi = lax.fori_loop(0, lseg, first_pass, (zero, zero))
    pr, pi = _cpow(arb, aib, lseg)
    cr, ci = zero, zero
    for _ in range(SUBLANES - 1):
        tr, ti = _cmul(pr, pi, cr, ci)
        tr, ti = tr + fr, ti + fi
        if reverse:
            cr = jnp.where(row == SUBLANES - 1, 0.0, pltpu.roll(tr, SUBLANES - 1, 0))
            ci = jnp.where(row == SUBLANES - 1, 0.0, pltpu.roll(ti, SUBLANES - 1, 0))
        else:
            cr = jnp.where(row == 0, 0.0, pltpu.roll(tr, 1, 0))
            ci = jnp.where(row == 0, 0.0, pltpu.roll(ti, 1, 0))

    def second_pass(k, carry):
        sr, si, acc = carry
        rows, nr, ni = advance(k, sr, si)
        z_re[rows, :] = nr
        z_im[rows, :] = ni
        if visit is not None:
            acc = visit(tile_of(k), nr, ni, acc)
        return nr, ni, acc

    acc0 = (zero, zero) if visit is not None else 0
    return lax.fori_loop(0, lseg, second_pass, (cr, ci, acc0))[2]


def _gelu(y):
    c = 0.7978845608028654
    return 0.5 * y * (1.0 + jnp.tanh(c * (y + 0.044715 * y * y * y)))


def _gelu_grad(y):
    c = 0.7978845608028654
    th = jnp.tanh(c * (y + 0.044715 * y * y * y))
    return 0.5 * (1.0 + th) + 0.5 * y * (1.0 - th * th) * c * (1.0 + 3.0 * 0.044715 * y * y)


def _s5_specs(lp):
    col_u = pl.BlockSpec((lp, COL_U), lambda j: (0, j))
    row_u = pl.BlockSpec((1, COL_U), lambda j: (0, j))
    row_s = pl.BlockSpec((1, COL_S), lambda j: (0, j))
    b_mat = pl.BlockSpec((None, COL_U, COL_S), lambda j: (j, 0, 0))
    c_mat = pl.BlockSpec((None, COL_S, COL_U), lambda j: (j, 0, 0))
    g_mat = pl.BlockSpec((None, COL_U, COL_U), lambda j: (j, 0, 0))
    return col_u, row_u, row_s, b_mat, c_mat, g_mat


def _s5_fill_states(u_ref, bre_ref, bim_ref, ar_ref, ai_ref, s_re, s_im, lseg, n_chunks, chunk):
    def fill(cidx, carry):
        rows = pl.ds(pl.multiple_of(cidx * chunk, SUBLANES), chunk)
        ub = u_ref[rows, :].astype(BF16)
        s_re[rows, :] = jnp.dot(ub, bre_ref[...], preferred_element_type=F32)
        s_im[rows, :] = jnp.dot(ub, bim_ref[...], preferred_element_type=F32)
        return carry
    lax.fori_loop(0, n_chunks, fill, 0)
    _segment_scan(s_re, s_im, ar_ref[...], ai_ref[...], lseg, reverse=False)


def _s5_forward(u_p, ar, ai, bre_bd, bim_bd, cret_bd, cimt_bd, d_row, glu_bd, glub_row):
    lp = u_p.shape[0]
    lseg = lp // SUBLANES
    chunk, n_chunks = lseg, SUBLANES

    def body(u_ref, ar_ref, ai_ref, bre_ref, bim_ref, cret_ref, cimt_ref, d_ref, glu_ref, glub_ref,
             ys_ref, s_re, s_im):
        _s5_fill_states(u_ref, bre_ref, bim_ref, ar_ref, ai_ref, s_re, s_im, lseg, n_chunks, chunk)

        def emit(cidx, carry):
            rows = pl.ds(pl.multiple_of(cidx * chunk, SUBLANES), chunk)
            y = (_dot(s_re[rows, :], cret_ref[...]) - _dot(s_im[rows, :], cimt_ref[...])
                 + d_ref[...] * u_ref[rows, :])
            g = _gelu(y)
            gate = _dot(g, glu_ref[...]) + glub_ref[...]
            ys_ref[rows, :] = g * _sigmoid(gate)
            return carry
        lax.fori_loop(0, n_chunks, emit, 0)

    col_u, row_u, row_s, b_mat, c_mat, g_mat = _s5_specs(lp)
    return pl.pallas_call(
        body, name="s5_forward", grid=(N_COL,),
        in_specs=[col_u, row_s, row_s, b_mat, b_mat, c_mat, c_mat, row_u, g_mat, row_u],
        out_specs=col_u, out_shape=jax.ShapeDtypeStruct((lp, D_SSM), F32),
        scratch_shapes=[pltpu.VMEM((lp, COL_S), F32), pltpu.VMEM((lp, COL_S), F32)],
        compiler_params=_params(("arbitrary",), VMEM_LIMIT),
    )(u_p, ar, ai, bre_bd, bim_bd, cret_bd, cimt_bd, d_row, glu_bd, glub_row)


def _s5_backward(u_p, dys_p, ar, ai, bre_bd, bim_bd, cret_bd, cimt_bd, d_row, glu_bd, glub_row):
    lp = u_p.shape[0]
    lseg = lp // SUBLANES
    chunk, n_chunks = lseg, SUBLANES

    def body(u_ref, dys_ref, ar_ref, ai_ref, bre_ref, bim_ref, cret_ref, cimt_ref, d_ref, glu_ref, glub_ref,
             du_ref, dar_ref, dai_ref, dbre_ref, dbim_ref, dcre_ref, dcim_ref, dd_ref, dglu_ref, dglub_ref,
             s_re, s_im, q_re, q_im):
        _s5_fill_states(u_ref, bre_ref, bim_ref, ar_ref, ai_ref, s_re, s_im, lseg, n_chunks, chunk)
        for ref in (dcre_ref, dcim_ref, dd_ref, dglu_ref, dglub_ref, dbre_ref, dbim_ref):
            ref[...] = jnp.zeros(ref.shape, F32)

        def mixer_bwd(cidx, carry):
            rows = pl.ds(pl.multiple_of(cidx * chunk, SUBLANES), chunk)
            u = u_ref[rows, :]
            sr, si = s_re[rows, :], s_im[rows, :]
            y = _dot(sr, cret_ref[...]) - _dot(si, cimt_ref[...]) + d_ref[...] * u
            g = _gelu(y)
            sg = _sigmoid(_dot(g, glu_ref[...]) + glub_ref[...])
            dout = dys_ref[rows, :]
            dgate = dout * g * sg * (1.0 - sg)
            dy = (dout * sg + _dot_nt(dgate, glu_ref[...])) * _gelu_grad(y)
            dglu_ref[...] += _dot_tn(g, dgate)
            dglub_ref[...] += jnp.sum(dgate, axis=0, keepdims=True)
            dd_ref[...] += jnp.sum(dy * u, axis=0, keepdims=True)
            dcre_ref[...] += _dot_tn(dy, sr)
            dcim_ref[...] -= _dot_tn(dy, si)
            q_re[rows, :] = _dot_nt(dy, cret_ref[...])
            q_im[rows, :] = -_dot_nt(dy, cimt_ref[...])
            du_ref[rows, :] = d_ref[...] * dy
            return carry
        lax.fori_loop(0, n_chunks, mixer_bwd, 0)

        row = lax.broadcasted_iota(jnp.int32, (SUBLANES, COL_S), 0)

        def visit(i, qr, qi, acc):
            prev = _tile_rows(jnp.where(i == 0, lseg - 1, i - 1))
            pr, pi = s_re[prev, :], s_im[prev, :]
            first = i == 0
            pr = jnp.where(first, jnp.where(row == 0, 0.0, pltpu.roll(pr, 1, 0)), pr)
            pi = jnp.where(first, jnp.where(row == 0, 0.0, pltpu.roll(pi, 1, 0)), pi)
            return acc[0] + qr * pr + qi * pi, acc[1] + qi * pr - qr * pi

        dar, dai = _segment_scan(q_re, q_im, ar_ref[...], -ai_ref[...], lseg, reverse=True, visit=visit)
        dar_ref[...] = jnp.sum(dar, axis=0, keepdims=True)
        dai_ref[...] = jnp.sum(dai, axis=0, keepdims=True)

        def input_bwd(cidx, carry):
            rows = pl.ds(pl.multiple_of(cidx * chunk, SUBLANES), chunk)
            qr, qi = q_re[rows, :], q_im[rows, :]
            u = u_ref[rows, :]
            du_ref[rows, :] += _dot_nt(qr, bre_ref[...]) + _dot_nt(qi, bim_ref[...])
            dbre_ref[...] += _dot_tn(u, qr)
            dbim_ref[...] += _dot_tn(u, qi)
            return carry
        lax.fori_loop(0, n_chunks, input_bwd, 0)

    col_u, row_u, row_s, b_mat, c_mat, g_mat = _s5_specs(lp)
    return pl.pallas_call(
        body, name="s5_backward", grid=(N_COL,),
        in_specs=[col_u, col_u, row_s, row_s, b_mat, b_mat, c_mat, c_mat, row_u, g_mat, row_u],
        out_specs=[col_u, row_s, row_s, b_mat, b_mat, b_mat, b_mat, row_u, g_mat, row_u],
        out_shape=[jax.ShapeDtypeStruct((lp, D_SSM), F32),
                   jax.ShapeDtypeStruct((1, N_COL * COL_S), F32), jax.ShapeDtypeStruct((1, N_COL * COL_S), F32),
                   jax.ShapeDtypeStruct((N_COL, COL_U, COL_S), F32), jax.ShapeDtypeStruct((N_COL, COL_U, COL_S), F32),
                   jax.ShapeDtypeStruct((N_COL, COL_U, COL_S), F32), jax.ShapeDtypeStruct((N_COL, COL_U, COL_S), F32),
                   jax.ShapeDtypeStruct((1, D_SSM), F32),
                   jax.ShapeDtypeStruct((N_COL, COL_U, COL_U), F32), jax.ShapeDtypeStruct((1, D_SSM), F32)],
        scratch_shapes=[pltpu.VMEM((lp, COL_S), F32)] * 4,
        compiler_params=_params(("arbitrary",), VMEM_LIMIT),
    )(u_p, dys_p, ar, ai, bre_bd, bim_bd, cret_bd, cimt_bd, d_row, glu_bd, glub_row)


def _split3(x):
    hi = x.astype(BF16)
    r1 = x - hi.astype(F32)
    mid = r1.astype(BF16)
    lo = (r1 - mid.astype(F32)).astype(BF16)
    return hi, mid, lo


def _band_apply(band, x):
    hi, mid, lo = _split3(x)
    dot = functools.partial(jnp.dot, preferred_element_type=F32)
    return dot(band, hi) + dot(band, mid) + dot(band, lo)


def _pool_band(window, transposed):
    t = lax.broadcasted_iota(jnp.int32, (TM, TM + POOL_HALO), 0)
    c = lax.broadcasted_iota(jnp.int32, (TM, TM + POOL_HALO), 1)
    lag = c - t if transposed else t + POOL_HALO - c
    return jnp.where((lag >= 0) & (lag < window), 1.0, 0.0).astype(BF16)


def _pool_inv_count(tile, window, first_row):
    t = tile * TM + lax.broadcasted_iota(jnp.int32, (TM, 1), 0) - first_row
    return 1.0 / jnp.clip(t + 1, 1, window).astype(F32)


def _pool_specs(lp):
    col = pl.BlockSpec((lp, POOL_DIM), lambda k: (0, k))
    mat = pl.BlockSpec((None, POOL_DIM, POOL_DIM), lambda k: (k, 0, 0))
    row = pl.BlockSpec((None, 1, POOL_DIM), lambda k: (k, 0, 0))
    return col, mat, row


def _pool_forward(v, pool_w, pool_scale, first_row):
    lp = v.shape[0]
    n_tiles = lp // TM

    def body(v_ref, w_ref, sc_ref, yp_ref, vpad):
        window = jnp.left_shift(2, pl.program_id(0))
        vpad[pl.ds(0, POOL_HALO), :] = jnp.zeros((POOL_HALO, POOL_DIM), F32)
        vpad[pl.ds(POOL_HALO, lp), :] = v_ref[...]
        band = _pool_band(window, transposed=False)

        def tile(j, carry):
            start = pl.multiple_of(j * TM, TM)
            ext = vpad[pl.ds(start, TM + POOL_HALO), :]
            p = _band_apply(band, ext) * _pool_inv_count(j, window, first_row) - ext[POOL_HALO:, :]
            yp_ref[pl.ds(start, TM), :] = _dot(p, w_ref[...]) * sc_ref[...]
            return carry
        lax.fori_loop(0, n_tiles, tile, 0)

    col, mat, row = _pool_specs(lp)
    return pl.pallas_call(
        body, name="pool_forward", grid=(POOL_GROUPS,),
        in_specs=[col, mat, row], out_specs=col, out_shape=jax.ShapeDtypeStruct((lp, D_SSM), F32),
        scratch_shapes=[pltpu.VMEM((lp + POOL_HALO, POOL_DIM), F32)],
        compiler_params=_params(("arbitrary",), VMEM_LIMIT),
    )(v, pool_w, pool_scale)


def _pool_backward(v, dyp, pool_w, pool_scale, first_row):
    lp = v.shape[0]
    n_tiles = lp // TM

    def body(v_ref, dyp_ref, w_ref, sc_ref, dv_ref, dw_ref, dsc_ref, vpad, gpad):
        window = jnp.left_shift(2, pl.program_id(0))
        vpad[pl.ds(0, POOL_HALO), :] = jnp.zeros((POOL_HALO, POOL_DIM), F32)
        vpad[pl.ds(POOL_HALO, lp), :] = v_ref[...]
        gpad[pl.ds(lp, POOL_HALO), :] = jnp.zeros((POOL_HALO, POOL_DIM), F32)
        dw_ref[...] = jnp.zeros(dw_ref.shape, F32)
        dsc_ref[...] = jnp.zeros(dsc_ref.shape, F32)
        band = _pool_band(window, transposed=False)

        def linear_bwd(j, carry):
            start = pl.multiple_of(j * TM, TM)
            ext = vpad[pl.ds(start, TM + POOL_HALO), :]
            inv = _pool_inv_count(j, window, first_row)
            p = _band_apply(band, ext) * inv - ext[POOL_HALO:, :]
            z = _dot(p, w_ref[...])
            dyp_t = dyp_ref[pl.ds(start, TM), :]
            dz = dyp_t * sc_ref[...]
            dsc_ref[...] += jnp.sum(dyp_t * z, axis=0, keepdims=True)
            dw_ref[...] += _dot_tn(p, dz)
            dp = _dot_nt(dz, w_ref[...])
            gpad[pl.ds(start, TM), :] = dp * inv
            dv_ref[pl.ds(start, TM), :] = -dp
            return carry
        lax.fori_loop(0, n_tiles, linear_bwd, 0)
        band_t = _pool_band(window, transposed=True)

        def window_bwd(j, carry):
            start = pl.multiple_of(j * TM, TM)
            dv_ref[pl.ds(start, TM), :] += _band_apply(band_t, gpad[pl.ds(start, TM + POOL_HALO), :])
            return carry
        lax.fori_loop(0, n_tiles, window_bwd, 0)

    col, mat, row = _pool_specs(lp)
    return pl.pallas_call(
        body, name="pool_backward", grid=(POOL_GROUPS,),
        in_specs=[col, col, mat, row], out_specs=[col, mat, row],
        out_shape=[jax.ShapeDtypeStruct((lp, D_SSM), F32),
                   jax.ShapeDtypeStruct((POOL_GROUPS, POOL_DIM, POOL_DIM), F32),
                   jax.ShapeDtypeStruct((POOL_GROUPS, 1, POOL_DIM), F32)],
        scratch_shapes=[pltpu.VMEM((lp + POOL_HALO, POOL_DIM), F32)] * 2,
        compiler_params=_params(("arbitrary",), VMEM_LIMIT),
    )(v, dyp, pool_w, pool_scale)


def _row_specs():
    head = _const((HEAD, D_MODEL))
    xrow = pl.BlockSpec((TM, D_MODEL), _xrow)
    full = pl.BlockSpec((TM, D_MODEL), lambda i: (i, 0))
    half = pl.BlockSpec((TM, D_SSM), lambda i: (i, 0))
    return head, xrow, full, half


def _in_proj(head, x, g1, w_in):
    n_tiles = (HEAD + x.shape[0]) // TM
    lp = n_tiles * TM

    def body(head_ref, x_ref, g_ref, w_ref, u_ref, v_ref):
        h0 = jnp.where(pl.program_id(0) == 0, head_ref[...], x_ref[...])
        proj = _dot(h0 * _rstd(h0) * g_ref[...], w_ref[...])
        u_ref[...] = proj[:, :D_SSM]
        v_ref[...] = proj[:, D_SSM:]

    head_s, xrow, _, half = _row_specs()
    return pl.pallas_call(
        body, name="in_proj", grid=(n_tiles,),
        in_specs=[head_s, xrow, _const((1, D_MODEL)), _const((D_MODEL, D_MODEL))],
        out_specs=[half, half], out_shape=[jax.ShapeDtypeStruct((lp, D_SSM), F32)] * 2,
        compiler_params=_params(("parallel",), VMEM_LIMIT),
    )(head, x, g1, w_in)


def _out_proj(head, x, ys, yp, gs, gp, w_out):
    lp = ys.shape[0]

    def body(head_ref, x_ref, ys_ref, yp_ref, gs_ref, gp_ref, w_ref, h1_ref):
        h0 = jnp.where(pl.program_id(0) == 0, head_ref[...], x_ref[...])
        ys_t, yp_t = ys_ref[...], yp_ref[...]
        ms = ys_t * _rstd(ys_t) * gs_ref[...]
        mp = yp_t * _rstd(yp_t) * gp_ref[...]
        h1_ref[...] = h0 + _dot(ms, w_ref[pl.ds(0, D_SSM), :]) + _dot(mp, w_ref[pl.ds(D_SSM, D_SSM), :])

    head_s, xrow, full, half = _row_specs()
    return pl.pallas_call(
        body, name="out_proj", grid=(lp // TM,),
        in_specs=[head_s, xrow, half, half, _const((1, D_SSM)), _const((1, D_SSM)), _const((D_MODEL, D_MODEL))],
        out_specs=full, out_shape=jax.ShapeDtypeStruct((lp, D_MODEL), F32),
        compiler_params=_params(("parallel",), VMEM_LIMIT),
    )(head, x, ys, yp, gs, gp, w_out)


def _load_weights(c_hbm, wd_hbm, c_vmem, wd_vmem, sems):
    @pl.when(pl.program_id(0) == 0)
    def _():
        copies = [pltpu.make_async_copy(c_hbm, c_vmem, sems.at[0]),
                  pltpu.make_async_copy(wd_hbm, wd_vmem, sems.at[1])]
        for cp in copies:
            cp.start()
        for cp in copies:
            cp.wait()


def _ffn_scratch():
    return [pltpu.VMEM((N_DEV, D_MODEL, 2 * FF_PAD), BF16), pltpu.VMEM((N_DEV, FF_PAD, D_MODEL), BF16),
            pltpu.SemaphoreType.DMA((2,))]


def _ffn_forward(h1, g2, c_all, wd_all):
    lp = h1.shape[0]

    def body(h1_ref, g_ref, c_hbm, wd_hbm, ab_ref, n2_ref, h2_ref, c_vmem, wd_vmem, sems):
        _load_weights(c_hbm, wd_hbm, c_vmem, wd_vmem, sems)
        h1_t = h1_ref[...]
        n2 = (h1_t * _rstd(h1_t) * g_ref[...]).astype(BF16)
        n2_ref[...] = n2
        acc = h1_t
        for j in range(N_DEV):
            ab = jnp.dot(n2, c_vmem[j], preferred_element_type=F32)
            a, b = ab[:, :FF_PAD], ab[:, FF_PAD:]
            ab_ref[:, pl.ds(j * 2 * FF_PAD, 2 * FF_PAD)] = ab.astype(BF16)
            acc = acc + _dot(a * _sigmoid(a) * b, wd_vmem[j])
        h2_ref[...] = acc

    _, _, full, _ = _row_specs()
    wide = pl.BlockSpec((TM, N_DEV * 2 * FF_PAD), lambda i: (i, 0))
    return pl.pallas_call(
        body, name="ffn_forward", grid=(lp // TM,),
        in_specs=[full, _const((1, D_MODEL)), ANY, ANY], out_specs=[wide, full, full],
        out_shape=[jax.ShapeDtypeStruct((lp, N_DEV * 2 * FF_PAD), BF16),
                   jax.ShapeDtypeStruct((lp, D_MODEL), BF16), jax.ShapeDtypeStruct((lp, D_MODEL), F32)],
        scratch_shapes=_ffn_scratch(), compiler_params=_params(("arbitrary",), VMEM_LIMIT),
    )(h1, g2, c_all, wd_all)


def _ffn_backward(h2, target, h1, ab, gf, g2, c_all, wd_all):
    lp = h1.shape[0]

    def body(h2_ref, t_ref, h1_ref, ab_ref, gf_ref, g2_ref, c_hbm, wd_hbm,
             dh1_ref, dab_ref, dh2_ref, loss_ref, dgf_ref, dg2_ref, c_vmem, wd_vmem, sems):
        i = pl.program_id(0)
        _load_weights(c_hbm, wd_hbm, c_vmem, wd_vmem, sems)

        @pl.when(i == 0)
        def _():
            loss_ref[...] = jnp.zeros(loss_ref.shape, F32)
            dgf_ref[...] = jnp.zeros(dgf_ref.shape, F32)
            dg2_ref[...] = jnp.zeros(dg2_ref.shape, F32)

        h2_t = h2_ref[...]
        rf = _rstd(h2_t)
        xf = h2_t * rf
        diff = jnp.where(i == 0, 0.0, xf * gf_ref[...] - t_ref[...])
        loss_ref[...] += 0.5 * jnp.sum(diff * diff) / D_MODEL
        dh2, dgf = _rms_bwd(diff / D_MODEL, xf, rf, gf_ref[...])
        dgf_ref[...] += dgf
        dh2_b = dh2.astype(BF16)
        dh2_ref[...] = dh2_b

        dn2 = jnp.zeros((TM, D_MODEL), F32)
        for j in range(N_DEV):
            cols = pl.ds(j * 2 * FF_PAD, 2 * FF_PAD)
            dff = _dot_nt(dh2_b, wd_vmem[j])
            ab_t = ab_ref[:, cols].astype(F32)
            a, b = ab_t[:, :FF_PAD], ab_t[:, FF_PAD:]
            sg = _sigmoid(a)
            dab_ref[:, pl.ds(j * 2 * FF_PAD, FF_PAD)] = (dff * b * sg * (1.0 + a * (1.0 - sg))).astype(BF16)
            dab_ref[:, pl.ds(j * 2 * FF_PAD + FF_PAD, FF_PAD)] = (dff * a * sg).astype(BF16)
            dn2 = dn2 + _dot_nt(dab_ref[:, cols], c_vmem[j])

        h1_t = h1_ref[...]
        r2 = _rstd(h1_t)
        dx, dg2 = _rms_bwd(dn2, h1_t * r2, r2, g2_ref[...])
        dg2_ref[...] += dg2
        dh1_ref[...] = dh2 + dx

    _, xrow, full, _ = _row_specs()
    wide = pl.BlockSpec((TM, N_DEV * 2 * FF_PAD), lambda i: (i, 0))
    vec = _const((1, D_MODEL))
    return pl.pallas_call(
        body, name="ffn_backward", grid=(lp // TM,),
        in_specs=[full, xrow, full, wide, vec, vec, ANY, ANY],
        out_specs=[full, wide, full, _const((1, PACK_LANES)), vec, vec],
        out_shape=[jax.ShapeDtypeStruct((lp, D_MODEL), F32),
                   jax.ShapeDtypeStruct((lp, N_DEV * 2 * FF_PAD), BF16),
                   jax.ShapeDtypeStruct((lp, D_MODEL), BF16),
                   jax.ShapeDtypeStruct((1, PACK_LANES), F32),
                   jax.ShapeDtypeStruct((1, D_MODEL), F32), jax.ShapeDtypeStruct((1, D_MODEL), F32)],
        scratch_shapes=_ffn_scratch(), compiler_params=_params(("arbitrary",), VMEM_LIMIT),
    )(h2, target, h1, ab, gf, g2, c_all, wd_all)


def _ffn_wgrad(n2, dh2, ab, dab):
    lp = n2.shape[0]

    def body(n2_ref, dh2_ref, ab_ref, dab_ref, dc_ref, dwd_ref):
        @pl.when(pl.program_id(1) == 0)
        def _():
            dc_ref[...] = jnp.zeros(dc_ref.shape, F32)
            dwd_ref[...] = jnp.zeros(dwd_ref.shape, F32)

        ab_t = ab_ref[...].astype(F32)
        a, b = ab_t[:, :FF_PAD], ab_t[:, FF_PAD:]
        dc_ref[...] += _dot_tn(n2_ref[...], dab_ref[...])
        dwd_ref[...] += _dot_tn(a * _sigmoid(a) * b, dh2_ref[...])

    act = pl.BlockSpec((TM, D_MODEL), lambda j, i: (i, 0))
    shard = pl.BlockSpec((TM, 2 * FF_PAD), lambda j, i: (i, j))
    return pl.pallas_call(
        body, name="ffn_wgrad", grid=(N_DEV, lp // TM),
        in_specs=[act, act, shard, shard],
        out_specs=[pl.BlockSpec((None, D_MODEL, 2 * FF_PAD), lambda j, i: (j, 0, 0)),
                   pl.BlockSpec((None, FF_PAD, D_MODEL), lambda j, i: (j, 0, 0))],
        out_shape=[jax.ShapeDtypeStruct((N_DEV, D_MODEL, 2 * FF_PAD), F32),
                   jax.ShapeDtypeStruct((N_DEV, FF_PAD, D_MODEL), F32)],
        compiler_params=_params(("parallel", "arbitrary"), VMEM_LIMIT),
    )(n2, dh2, ab, dab)


def _out_proj_backward(dh1, ys, yp, gs, gp, w_out):
    lp = ys.shape[0]

    def body(dh1_ref, ys_ref, yp_ref, gs_ref, gp_ref, w_ref, dys_ref, dyp_ref, dgs_ref, dgp_ref, dw_ref):
        @pl.when(pl.program_id(0) == 0)
        def _():
            dgs_ref[...] = jnp.zeros(dgs_ref.shape, F32)
            dgp_ref[...] = jnp.zeros(dgp_ref.shape, F32)
            dw_ref[...] = jnp.zeros(dw_ref.shape, F32)

        dh1_b = dh1_ref[...].astype(BF16)
        dmix = _dot_nt(dh1_b, w_ref[...])
        for y_ref, g_ref, dy_ref, dg_ref, lo in ((ys_ref, gs_ref, dys_ref, dgs_ref, 0),
                                                 (yp_ref, gp_ref, dyp_ref, dgp_ref, D_SSM)):
            y_t = y_ref[...]
            r = _rstd(y_t)
            xhat = y_t * r
            dy, dg = _rms_bwd(dmix[:, lo:lo + D_SSM], xhat, r, g_ref[...])
            dy_ref[...] = dy
            dg_ref[...] += dg
            dw_ref[pl.ds(lo, D_SSM), :] += _dot_tn(xhat * g_ref[...], dh1_b)

    _, _, full, half = _row_specs()
    vec = _const((1, D_SSM))
    return pl.pallas_call(
        body, name="out_proj_backward", grid=(lp // TM,),
        in_specs=[full, half, half, vec, vec, _const((D_MODEL, D_MODEL))],
        out_specs=[half, half, vec, vec, _const((D_MODEL, D_MODEL))],
        out_shape=[jax.ShapeDtypeStruct((lp, D_SSM), F32)] * 2 + [jax.ShapeDtypeStruct((1, D_SSM), F32)] * 2
        + [jax.ShapeDtypeStruct((D_MODEL, D_MODEL), F32)],
        compiler_params=_params(("arbitrary",), VMEM_LIMIT),
    )(dh1, ys, yp, gs, gp, w_out)


def _in_proj_backward(head, x, du, dv, dh1, g1, w_in):
    lp = du.shape[0]

    def body(head_ref, x_ref, du_ref, dv_ref, dh1_ref, g_ref, w_ref, dx_ref, dhead_ref, dg_ref, dw_ref):
        i = pl.program_id(0)

        @pl.when(i == 0)
        def _():
            dg_ref[...] = jnp.zeros(dg_ref.shape, F32)
            dw_ref[...] = jnp.zeros(dw_ref.shape, F32)

        h0 = jnp.where(i == 0, head_ref[...], x_ref[...])
        r = _rstd(h0)
        xhat = h0 * r
        n1 = (xhat * g_ref[...]).astype(BF16)
        du_b, dv_b = du_ref[...].astype(BF16), dv_ref[...].astype(BF16)
        dn1 = _dot_nt(du_b, w_ref[:, pl.ds(0, D_SSM)]) + _dot_nt(dv_b, w_ref[:, pl.ds(D_SSM, D_SSM)])
        dx, dg = _rms_bwd(dn1, xhat, r, g_ref[...])
        dg_ref[...] += dg
        dh0 = dh1_ref[...] + dx
        dx_ref[...] = dh0

        @pl.when(i == 0)
        def _():
            dhead_ref[...] = dh0

        dw_ref[:, pl.ds(0, D_SSM)] += _dot_tn(n1, du_b)
        dw_ref[:, pl.ds(D_SSM, D_SSM)] += _dot_tn(n1, dv_b)

    head_s, xrow, full, half = _row_specs()
    vec = _const((1, D_MODEL))
    mat = _const((D_MODEL, D_MODEL))
    return pl.pallas_call(
        body, name="in_proj_backward", grid=(lp // TM,),
        in_specs=[head_s, xrow, half, half, full, vec, mat],
        out_specs=[xrow, head_s, vec, mat],
        out_shape=[jax.ShapeDtypeStruct(x.shape, F32), jax.ShapeDtypeStruct((HEAD, D_MODEL), F32),
                   jax.ShapeDtypeStruct((1, D_MODEL), F32), jax.ShapeDtypeStruct((D_MODEL, D_MODEL), F32)],
        compiler_params=_params(("arbitrary",), VMEM_LIMIT),
    )(head, x, du, dv, dh1, g1, w_in)


def _permute_rows(a):
    lp, n = a.shape
    return a.reshape(SUBLANES, lp // SUBLANES, n).transpose(1, 0, 2).reshape(lp, n)


def _unpermute_rows(a):
    lp, n = a.shape
    return a.reshape(lp // SUBLANES, SUBLANES, n).transpose(1, 0, 2).reshape(lp, n)


def _block_diag(blocks):
    _, r, c = blocks.shape
    b = blocks.reshape(N_COL, GROUPS_PER_COL, r, 1, c)
    eye = jnp.eye(GROUPS_PER_COL, dtype=blocks.dtype).reshape(1, GROUPS_PER_COL, 1, GROUPS_PER_COL, 1)
    return (b * eye).reshape(N_COL, GROUPS_PER_COL * r, GROUPS_PER_COL * c)


def _block_diag_extract(mats, r, c):
    m = mats.reshape(N_COL, GROUPS_PER_COL, r, GROUPS_PER_COL, c)
    eye = jnp.eye(GROUPS_PER_COL, dtype=mats.dtype).reshape(1, GROUPS_PER_COL, 1, GROUPS_PER_COL, 1)
    return jnp.sum(m * eye, axis=3).reshape(SSM_GROUPS, r, c)


def _pack(parts):
    rows = []
    for p in parts:
        flat = p.reshape(-1).astype(F32)
        pad = (-flat.shape[0]) % PACK_UNIT
        rows.append(jnp.pad(flat, (0, pad)).reshape(-1, PACK_LANES))
    return jnp.concatenate(rows, axis=0)


def _unpack(packed, shapes):
    out, row = [], 0
    for shape in shapes:
        size = 1
        for s in shape:
            size *= s
        n_rows = -(-size // PACK_UNIT) * 8
        out.append(packed[row:row + n_rows].reshape(-1)[:size].reshape(shape))
        row += n_rows
    return out


def _pad_cols(a):
    return jnp.pad(a, ((0, 0), (0, FF_PAD - FF_SHARD)))


def _pad_rows(a):
    return jnp.pad(a, ((0, FF_PAD - FF_SHARD), (0, 0)))


def _gate_up(gate, up):
    return jnp.concatenate([_pad_cols(gate), _pad_cols(up)], axis=1)


def kernel(x, meta_tokens, norm1_g, w_in, ssm_lambda_re, ssm_lambda_im, ssm_log_step, ssm_b_re, ssm_b_im, ssm_c_re, ssm_c_im, ssm_d, ssm_glu_w, ssm_glu_b, ssm_norm_g, pool_w, pool_scale, pool_norm_g, w_out, norm2_g, w_gate, w_up, w_down, final_norm_g, loss_target, m_meta_tokens, m_norm1_g, m_w_in, m_ssm_lambda_re, m_ssm_lambda_im, m_ssm_log_step, m_ssm_b_re, m_ssm_b_im, m_ssm_c_re, m_ssm_c_im, m_ssm_d, m_ssm_glu_w, m_ssm_glu_b, m_ssm_norm_g, m_pool_w, m_pool_scale, m_pool_norm_g, m_w_out, m_norm2_g, m_w_gate, m_w_up, m_w_down, m_final_norm_g, v_meta_tokens, v_norm1_g, v_w_in, v_ssm_lambda_re, v_ssm_lambda_im, v_ssm_log_step, v_ssm_b_re, v_ssm_b_im, v_ssm_c_re, v_ssm_c_im, v_ssm_d, v_ssm_glu_w, v_ssm_glu_b, v_ssm_norm_g, v_pool_w, v_pool_scale, v_pool_norm_g, v_w_out, v_norm2_g, v_w_gate, v_w_up, v_w_down, v_final_norm_g):
    given = dict(locals())
    weights = {n: given[n] for n in WEIGHT_NAMES}
    n_meta = meta_tokens.shape[0]
    me = 4 * lax.axis_index("x") + 2 * lax.axis_index("y") + lax.axis_index("c")

    rows_shard = jnp.concatenate([w_in[0], w_out[0], _pad_rows(w_down[0])], axis=0).astype(BF16)
    cols_shard = _gate_up(w_gate[0], w_up[0]).astype(BF16)
    rows_all, c_all, meta_all = _all_to_all([rows_shard, cols_shard, meta_tokens], [False] * 3, "weight_gather")
    shard_rows = w_in.shape[1]
    w_in_all = rows_all[:, :shard_rows].reshape(D_MODEL, D_MODEL)
    w_out_all = rows_all[:, shard_rows:2 * shard_rows].reshape(D_MODEL, D_MODEL)
    wd_all = rows_all[:, 2 * shard_rows:]
    meta_full = meta_all.transpose(1, 0, 2).reshape(n_meta, D_MODEL)

    xs = x[0]
    tgt = loss_target[0]
    head = jnp.concatenate([jnp.zeros((HEAD - n_meta, D_MODEL), F32), meta_full], axis=0)
    first_row = HEAD - n_meta
    g1, g2, gf = norm1_g, norm2_g, final_norm_g.reshape(1, D_MODEL)
    gs, gp = ssm_norm_g, pool_norm_g

    lam_re, lam_im = ssm_lambda_re[0], ssm_lambda_im[0]
    log_step = ssm_log_step[0].reshape(SSM_GROUPS, 1)
    b_re = ssm_b_re[0].reshape(SSM_GROUPS * SSM_STATE, SSM_GROUP)
    b_im = ssm_b_im[0].reshape(SSM_GROUPS * SSM_STATE, SSM_GROUP)
    abr, abi, zr, zi = _s5_disc_a(lam_re, lam_im, log_step)
    zr_col, zi_col = zr.reshape(-1, 1), zi.reshape(-1, 1)
    bbr, bbi = _s5_disc_b(zr_col, zi_col, b_re, b_im)
    to_bd = lambda b: _block_diag(b.reshape(SSM_GROUPS, SSM_STATE, SSM_GROUP).transpose(0, 2, 1)).astype(BF16)
    bre_bd, bim_bd = to_bd(bbr), to_bd(bbi)
    cret_bd = _block_diag(ssm_c_re[0].transpose(0, 2, 1)).astype(BF16)
    cimt_bd = _block_diag(ssm_c_im[0].transpose(0, 2, 1)).astype(BF16)
    glu_bd = _block_diag(ssm_glu_w[0]).astype(BF16)
    s5_consts = (abr.reshape(1, -1), abi.reshape(1, -1), bre_bd, bim_bd, cret_bd, cimt_bd,
                 ssm_d[0].reshape(1, D_SSM), glu_bd, ssm_glu_b[0].reshape(1, D_SSM))
    pool_sc = pool_scale[0].reshape(POOL_GROUPS, 1, POOL_DIM)

    u, v = _in_proj(head, xs, g1, w_in_all)
    u_p = _permute_rows(u)
    ys = _unpermute_rows(_s5_forward(u_p, *s5_consts))
    yp = _pool_forward(v, pool_w[0], pool_sc, first_row)
    h1 = _out_proj(head, xs, ys, yp, gs, gp, w_out_all)
    ab, n2, h2 = _ffn_forward(h1, g2, c_all, wd_all)

    dh1, dab, dh2, loss_part, d_gf, d_g2 = _ffn_backward(h2, tgt, h1, ab, gf, g2, c_all, wd_all)
    d_c, d_wd = _ffn_wgrad(n2, dh2, ab, dab)
    dys, dyp, d_gs, d_gp, d_wout = _out_proj_backward(dh1, ys, yp, gs, gp, w_out_all)
    dv, d_pool_w, d_pool_sc = _pool_backward(v, dyp, pool_w[0], pool_sc, first_row)
    (du_p, d_ar, d_ai, d_bre_bd, d_bim_bd, d_cre_bd, d_cim_bd, d_d, d_glu_bd, d_glub) = _s5_backward(
        u_p, _permute_rows(dys), *s5_consts)
    du = _unpermute_rows(du_p)
    d_x, d_head, d_g1, d_win = _in_proj_backward(head, xs, du, dv, dh1, g1, w_in_all)

    from_bd = lambda m: _block_diag_extract(m, SSM_GROUP, SSM_STATE).transpose(0, 2, 1)
    d_bbr = from_bd(d_bre_bd).reshape(SSM_GROUPS * SSM_STATE, SSM_GROUP)
    d_bbi = from_bd(d_bim_bd).reshape(SSM_GROUPS * SSM_STATE, SSM_GROUP)
    d_zr, d_zi, d_b_re, d_b_im = _s5_disc_b_bwd(zr_col, zi_col, b_re, b_im, d_bbr, d_bbi)
    d_lam_re, d_lam_im, d_log_step = _s5_disc_a_bwd(
        lam_re, lam_im, log_step,
        (d_ar.reshape(SSM_GROUPS, SSM_STATE), d_ai.reshape(SSM_GROUPS, SSM_STATE),
         d_zr.reshape(SSM_GROUPS, SSM_STATE), d_zi.reshape(SSM_GROUPS, SSM_STATE)))
    small_grads = {
        "norm1_g": d_g1, "ssm_lambda_re": d_lam_re, "ssm_lambda_im": d_lam_im, "ssm_log_step": d_log_step,
        "ssm_b_re": d_b_re, "ssm_b_im": d_b_im,
        "ssm_c_re": _block_diag_extract(d_cre_bd, SSM_GROUP, SSM_STATE),
        "ssm_c_im": _block_diag_extract(d_cim_bd, SSM_GROUP, SSM_STATE),
        "ssm_d": d_d, "ssm_glu_w": _block_diag_extract(d_glu_bd, SSM_GROUP, SSM_GROUP), "ssm_glu_b": d_glub,
        "ssm_norm_g": d_gs, "pool_w": d_pool_w, "pool_scale": d_pool_sc, "pool_norm_g": d_gp,
        "norm2_g": d_g2, "final_norm_g": d_gf,
    }

    extra = [d_head[first_row:], loss_part]
    grad_pack = _pack([small_grads[n] for n in SMALL_NAMES] + extra)
    tail = [jnp.zeros_like(e) for e in extra]
    pack_of = lambda prefix: _pack([given[prefix + n] for n in SMALL_NAMES] + tail)
    r_win, r_wout, r_wd, r_c, r_pack = _all_to_all(
        [d_win.reshape(N_DEV, shard_rows, D_MODEL), d_wout.reshape(N_DEV, shard_rows, D_MODEL), d_wd, d_c, grad_pack],
        [True, True, True, True, False], "grad_exchange")

    results = {}
    results["w_in"] = _adamw(r_win, w_in[0], m_w_in[0], v_w_in[0], shard_rows, "adamw_w_in")
    results["w_out"] = _adamw(r_wout, w_out[0], m_w_out[0], v_w_out[0], shard_rows, "adamw_w_out")
    res_wd = _adamw(r_wd, _pad_rows(w_down[0]), _pad_rows(m_w_down[0]), _pad_rows(v_w_down[0]), 128, "adamw_w_down")
    results["w_down"] = [r[:FF_SHARD] for r in res_wd]
    res_c = _adamw(r_c, _gate_up(w_gate[0], w_up[0]), _gate_up(m_w_gate[0], m_w_up[0]),
                   _gate_up(v_w_gate[0], v_w_up[0]), 128, "adamw_gate_up")
    results["w_gate"] = [r[:, :FF_SHARD] for r in res_c]
    results["w_up"] = [r[:, FF_PAD:FF_PAD + FF_SHARD] for r in res_c]
    res_small = _adamw(r_pack, pack_of(""), pack_of("m_"), pack_of("v_"), r_pack.shape[1], "adamw_small")
    shapes = [weights[n].shape for n in SMALL_NAMES] + [(n_meta, D_MODEL), (1, PACK_LANES)]
    unpacked = [_unpack(r, shapes) for r in res_small]
    for idx, n in enumerate(SMALL_NAMES):
        results[n] = [r[idx] for r in unpacked]
    g_meta_all, loss_row = unpacked[0][-2], unpacked[0][-1]
    shard_cols = meta_tokens.shape[1]
    g_meta = lax.dynamic_slice_in_dim(g_meta_all, me * shard_cols, shard_cols, axis=1)
    results["meta_tokens"] = _adamw(g_meta[None], meta_tokens, m_meta_tokens, v_meta_tokens, n_meta, "adamw_meta")

    out = [loss_row[0, 0], d_x[None]]
    for part in range(4):
        for n in WEIGHT_NAMES:
            out.append(results[n][part].reshape(weights[n].shape))
    return tuple(out)
```

```python
import functools

import jax
import jax.numpy as jnp
from jax import lax
from jax.experimental import pallas as pl
from jax.experimental.pallas import tpu as pltpu

F32 = jnp.float32
BF16 = jnp.bfloat16

N_DEV = 8
D_MODEL = 1024
D_SSM = 512
SSM_GROUP = 16
SSM_STATE = 64
SSM_GROUPS = 32
POOL_GROUPS = 4
POOL_DIM = 128
COL_U = 128
COL_S = 512
N_COL = D_SSM // COL_U
GROUPS_PER_COL = COL_U // SSM_GROUP
FF_SHARD = 352
FF_PAD = 384
TM = 256
HEAD = TM
SUBLANES = 8
POOL_HALO = 128
EPS = 1e-6
STEP_FLOOR = -1e-4
VMEM_LIMIT = 60 * 1024 * 1024

ADAM_LR = 0.001
ADAM_B1 = 0.9
ADAM_B2 = 0.999
ADAM_EPS = 1e-08
ADAM_WD = 0.01
ADAM_STEP = 10

MESH_ID = pl.DeviceIdType.MESH
ANY = pl.BlockSpec(memory_space=pl.ANY)

SMALL_NAMES = ("norm1_g", "ssm_lambda_re", "ssm_lambda_im", "ssm_log_step", "ssm_b_re", "ssm_b_im",
               "ssm_c_re", "ssm_c_im", "ssm_d", "ssm_glu_w", "ssm_glu_b", "ssm_norm_g", "pool_w",
               "pool_scale", "pool_norm_g", "norm2_g", "final_norm_g")
WEIGHT_NAMES = ("meta_tokens", "norm1_g", "w_in", "ssm_lambda_re", "ssm_lambda_im", "ssm_log_step",
                "ssm_b_re", "ssm_b_im", "ssm_c_re", "ssm_c_im", "ssm_d", "ssm_glu_w", "ssm_glu_b",
                "ssm_norm_g", "pool_w", "pool_scale", "pool_norm_g", "w_out", "norm2_g", "w_gate",
                "w_up", "w_down", "final_norm_g")
PACK_LANES = 128
PACK_UNIT = 8 * PACK_LANES


def _dot(a, b):
    return jnp.dot(a.astype(BF16), b.astype(BF16), preferred_element_type=F32)


def _dot_nt(a, b):
    return lax.dot_general(a.astype(BF16), b.astype(BF16), (((1,), (1,)), ((), ())), preferred_element_type=F32)


def _dot_tn(a, b):
    return lax.dot_general(a.astype(BF16), b.astype(BF16), (((0,), (0,)), ((), ())), preferred_element_type=F32)


def _sigmoid(x):
    return 1.0 / (1.0 + jnp.exp(-x))


def _rstd(x):
    return lax.rsqrt(jnp.mean(x * x, axis=-1, keepdims=True) + EPS)


def _rms_bwd(dy, xhat, r, g):
    dxh = dy * g
    dx = r * (dxh - xhat * jnp.mean(dxh * xhat, axis=-1, keepdims=True))
    return dx, jnp.sum(dy * xhat, axis=0, keepdims=True)


def _params(sem, vmem=None):
    return pltpu.CompilerParams(dimension_semantics=sem, vmem_limit_bytes=vmem)


def _const(shape):
    return pl.BlockSpec(shape, lambda *_: (0,) * len(shape))


def _xrow(i):
    return (jnp.maximum(i - 1, 0), 0)


def _all_to_all(arrays, scatter, name):
    n_arr = len(arrays)

    def body(*refs):
        ins, outs = refs[:n_arr], refs[n_arr:2 * n_arr]
        send_sems, recv_sems, local_sems = refs[2 * n_arr:]
        x, y, c = lax.axis_index("x"), lax.axis_index("y"), lax.axis_index("c")
        me = 4 * x + 2 * y + c
        copies = []
        for a in range(n_arr):
            own = ins[a].at[me] if scatter[a] else ins[a]
            local = pltpu.make_async_copy(own, outs[a].at[me], local_sems.at[a])
            local.start()
            copies.append(local)
            for k in range(1, N_DEV):
                px = 1 - x if k & 4 else x
                py = 1 - y if k & 2 else y
                pc = 1 - c if k & 1 else c
                peer = 4 * px + 2 * py + pc
                src = ins[a].at[peer] if scatter[a] else ins[a]
                cp = pltpu.make_async_remote_copy(
                    src_ref=src, dst_ref=outs[a].at[me], send_sem=send_sems.at[a, k - 1],
                    recv_sem=recv_sems.at[a, k - 1], device_id=(px, py, pc), device_id_type=MESH_ID)
                cp.start()
                copies.append(cp)
        for cp in copies:
            cp.wait()

    out_shape = []
    for arr, sc in zip(arrays, scatter):
        shape = arr.shape if sc else (N_DEV,) + arr.shape
        out_shape.append(jax.ShapeDtypeStruct(shape, arr.dtype))
    return pl.pallas_call(
        body, name=name, out_shape=out_shape,
        in_specs=[ANY] * n_arr, out_specs=[ANY] * n_arr,
        scratch_shapes=[pltpu.SemaphoreType.DMA((n_arr, N_DEV - 1)),
                        pltpu.SemaphoreType.DMA((n_arr, N_DEV - 1)),
                        pltpu.SemaphoreType.DMA((n_arr,))],
    )(*arrays)


HBM = pl.BlockSpec(memory_space=pltpu.HBM)
SEM = pl.BlockSpec(memory_space=pltpu.SEMAPHORE)
EFFECT = pltpu.SideEffectType.DATAFLOW_SIDE_EFFECTING


def _split_copy(srcs, lands, send_sems, recv_sems, scatter, a, k):
    x, y, c = lax.axis_index("x"), lax.axis_index("y"), lax.axis_index("c")
    me = 4 * x + 2 * y + c
    px = 1 - x if k & 4 else x
    py = 1 - y if k & 2 else y
    pc = 1 - c if k & 1 else c
    sem = a * (N_DEV - 1) + k - 1
    return pltpu.make_async_remote_copy(
        src_ref=srcs[a].at[4 * px + 2 * py + pc] if scatter[a] else srcs[a], dst_ref=lands[a].at[me],
        send_sem=send_sems.at[sem], recv_sem=recv_sems.at[sem], device_id=(px, py, pc), device_id_type=MESH_ID)


def _all_to_all_start(arrays, scatter, name):
    n_arr = len(arrays)
    n_sem = n_arr * (N_DEV - 1)

    def body(*refs):
        srcs, lands = refs[:n_arr], refs[n_arr:2 * n_arr]
        send_sems, recv_sems = refs[2 * n_arr], refs[2 * n_arr + 1]
        token = refs[-1]
        for a in range(n_arr):
            for k in range(1, N_DEV):
                _split_copy(srcs, lands, send_sems, recv_sems, scatter, a, k).start()
        token[...] = jnp.zeros(token.shape, token.dtype)

    land_shapes = [arr.shape if sc else (N_DEV,) + arr.shape for arr, sc in zip(arrays, scatter)]
    operands = [pltpu.with_memory_space_constraint(arr, pltpu.HBM) for arr in arrays]
    operands += [pltpu.with_memory_space_constraint(lax.empty(shape, arr.dtype), pltpu.HBM)
                 for shape, arr in zip(land_shapes, arrays)]
    out = pl.pallas_call(
        body, name=name,
        out_shape=(pltpu.SemaphoreType.DMA((n_sem,)), pltpu.SemaphoreType.DMA((n_sem,)),
                   *[pltpu.HBM(arr.shape, arr.dtype) for arr in arrays],
                   *[pltpu.HBM(shape, arr.dtype) for shape, arr in zip(land_shapes, arrays)],
                   jax.ShapeDtypeStruct((8, PACK_LANES), F32)),
        in_specs=[HBM] * (2 * n_arr),
        out_specs=(SEM, SEM, *[HBM] * (2 * n_arr), pl.BlockSpec(memory_space=pltpu.VMEM)),
        input_output_aliases={i: 2 + i for i in range(2 * n_arr)},
        compiler_params=pltpu.CompilerParams(has_side_effects=EFFECT),
    )(*operands)
    return out[:-1], out[-1]


def _all_to_all_wait(started, scatter, after, name):
    send_sems, recv_sems = started[0], started[1]
    n_arr = (len(started) - 2) // 2
    srcs_thru, lands_thru = started[2:2 + n_arr], started[2 + n_arr:]

    def body(*refs):
        srcs, lands = refs[:n_arr], refs[n_arr:2 * n_arr]
        send_r, recv_r = refs[2 * n_arr], refs[2 * n_arr + 1]
        for a in range(n_arr):
            for k in range(1, N_DEV):
                cp = _split_copy(srcs, lands, send_r, recv_r, scatter, a, k)
                cp.wait_send()
                cp.wait_recv()

    out = pl.pallas_call(
        body, name=name,
        out_shape=tuple(pltpu.HBM(t.shape, t.dtype) for t in (*srcs_thru, *lands_thru)),
        in_specs=[HBM] * (2 * n_arr) + [SEM, SEM, ANY], out_specs=tuple([HBM] * (2 * n_arr)),
        input_output_aliases={i: i for i in range(2 * n_arr)},
        compiler_params=pltpu.CompilerParams(has_side_effects=EFFECT),
    )(*srcs_thru, *lands_thru, send_sems, recv_sems, after)
    return out[:n_arr], out[n_arr:]


def _fill_own(lands, owns, scatter, name):
    n_arr = len(lands)

    def body(*refs):
        own, land_out, sems = refs[n_arr:2 * n_arr], refs[2 * n_arr:3 * n_arr], refs[-1]
        me = 4 * lax.axis_index("x") + 2 * lax.axis_index("y") + lax.axis_index("c")
        copies = [pltpu.make_async_copy(own[a].at[me] if scatter[a] else own[a], land_out[a].at[me], sems.at[a])
                  for a in range(n_arr)]
        for cp in copies:
            cp.start()
        for cp in copies:
            cp.wait()

    return pl.pallas_call(
        body, name=name, out_shape=[jax.ShapeDtypeStruct(t.shape, t.dtype) for t in lands],
        in_specs=[ANY] * (2 * n_arr), out_specs=[ANY] * n_arr,
        input_output_aliases={i: i for i in range(n_arr)},
        scratch_shapes=[pltpu.SemaphoreType.DMA((n_arr,))],
    )(*lands, *owns)


def _adamw_math(g, w, m, v):
    nm = ADAM_B1 * m + (1.0 - ADAM_B1) * g
    nv = ADAM_B2 * v + (1.0 - ADAM_B2) * (g * g)
    m_hat = nm / (1.0 - ADAM_B1 ** ADAM_STEP)
    v_hat = nv / (1.0 - ADAM_B2 ** ADAM_STEP)
    return -ADAM_LR * (m_hat / (jnp.sqrt(v_hat) + ADAM_EPS) + ADAM_WD * w), nm, nv


def _sum_slots(s_ref):
    g = s_ref[0].astype(F32)
    for s in range(1, s_ref.shape[0]):
        g = g + s_ref[s].astype(F32)
    return g


def _adamw(slots, w, m, v, tile_rows, name):
    n, rows, cols = slots.shape

    def body(s_ref, w_ref, m_ref, v_ref, g_ref, d_ref, nm_ref, nv_ref):
        g = _sum_slots(s_ref)
        g_ref[...] = g
        d_ref[...], nm_ref[...], nv_ref[...] = _adamw_math(g, w_ref[...], m_ref[...], v_ref[...])

    tile = pl.BlockSpec((tile_rows, cols), lambda i: (i, 0))
    return pl.pallas_call(
        body, name=name, grid=(rows // tile_rows,),
        in_specs=[pl.BlockSpec((n, tile_rows, cols), lambda i: (0, i, 0)), tile, tile, tile],
        out_specs=[tile] * 4, out_shape=[jax.ShapeDtypeStruct((rows, cols), F32)] * 4,
        compiler_params=_params(("parallel",), VMEM_LIMIT),
    )(slots, w, m, v)


def _reduce_slots(slot_arrays, name):
    def body(*refs):
        n_arr = len(refs) // 2
        for s_ref, o_ref in zip(refs[:n_arr], refs[n_arr:]):
            o_ref[...] = _sum_slots(s_ref)
    return pl.pallas_call(
        body, name=name, out_shape=[jax.ShapeDtypeStruct(s.shape[1:], F32) for s in slot_arrays],
        compiler_params=_params(None, VMEM_LIMIT))(*slot_arrays)


def _adamw_many(grads, ws, ms, vs, name):
    n = len(grads)

    def body(*refs):
        ins, outs = refs[:4 * n], refs[4 * n:]
        for i in range(n):
            g, w, m, v = (ins[j * n + i][...] for j in range(4))
            outs[i][...], outs[n + i][...], outs[2 * n + i][...] = _adamw_math(g, w, m, v)

    out = pl.pallas_call(
        body, name=name, out_shape=[jax.ShapeDtypeStruct(w.shape, F32) for w in ws] * 3,
        compiler_params=_params(None, VMEM_LIMIT))(*grads, *ws, *ms, *vs)
    return out[:n], out[n:2 * n], out[2 * n:]


def _disc_a(lam_re, lam_im, log_step):
    lr = jnp.minimum(lam_re, STEP_FLOOR)
    step = jnp.exp(log_step)
    mag = jnp.exp(lr * step)
    ang = lam_im * step
    abr = mag * jnp.cos(ang)
    abi = mag * jnp.sin(ang)
    nr = abr - 1.0
    den = lr * lr + lam_im * lam_im
    cr = (nr * lr + abi * lam_im) / den
    ci = (abi * lr - nr * lam_im) / den
    return abr, abi, cr, ci


def _disc_b(cr, ci, b_re, b_im):
    return cr * b_re - ci * b_im, cr * b_im + ci * b_re


def _s5_disc_a(lam_re, lam_im, log_step):
    def body(lr_ref, li_ref, ls_ref, *outs):
        for o, val in zip(outs, _disc_a(lr_ref[...], li_ref[...], ls_ref[...])):
            o[...] = val
    return pl.pallas_call(body, name="s5_disc_a", out_shape=[jax.ShapeDtypeStruct(lam_re.shape, F32)] * 4)(
        lam_re, lam_im, log_step)


def _s5_disc_a_bwd(lam_re, lam_im, log_step, cts):
    def body(lr_ref, li_ref, ls_ref, c0, c1, c2, c3, dlr_ref, dli_ref, dls_ref):
        _, vjp = jax.vjp(_disc_a, lr_ref[...], li_ref[...], ls_ref[...])
        dlr, dli, dls = vjp((c0[...], c1[...], c2[...], c3[...]))
        dlr_ref[...] = dlr
        dli_ref[...] = dli
        dls_ref[...] = dls
    return pl.pallas_call(
        body, name="s5_disc_a_bwd",
        out_shape=[jax.ShapeDtypeStruct(lam_re.shape, F32), jax.ShapeDtypeStruct(lam_re.shape, F32),
                   jax.ShapeDtypeStruct(log_step.shape, F32)])(lam_re, lam_im, log_step, *cts)


def _s5_disc_b(cr, ci, b_re, b_im):
    def body(cr_ref, ci_ref, br_ref, bi_ref, o_re, o_im):
        o_re[...], o_im[...] = _disc_b(cr_ref[...], ci_ref[...], br_ref[...], bi_ref[...])
    return pl.pallas_call(body, name="s5_disc_b", out_shape=[jax.ShapeDtypeStruct(b_re.shape, F32)] * 2)(
        cr, ci, b_re, b_im)


def _s5_disc_b_bwd(cr, ci, b_re, b_im, d_re, d_im):
    def body(cr_ref, ci_ref, br_ref, bi_ref, dr_ref, di_ref, dcr_ref, dci_ref, dbr_ref, dbi_ref):
        _, vjp = jax.vjp(_disc_b, cr_ref[...], ci_ref[...], br_ref[...], bi_ref[...])
        dcr_ref[...], dci_ref[...], dbr_ref[...], dbi_ref[...] = vjp((dr_ref[...], di_ref[...]))
    return pl.pallas_call(
        body, name="s5_disc_b_bwd",
        out_shape=[jax.ShapeDtypeStruct(cr.shape, F32)] * 2 + [jax.ShapeDtypeStruct(b_re.shape, F32)] * 2)(
            cr, ci, b_re, b_im, d_re, d_im)


def _cmul(ar, ai, br, bi):
    return ar * br - ai * bi, ar * bi + ai * br


def _cpow(ar, ai, n):
    rr, ri = jnp.ones_like(ar), jnp.zeros_like(ai)
    while n:
        if n & 1:
            rr, ri = _cmul(rr, ri, ar, ai)
        n >>= 1
        if n:
            ar, ai = _cmul(ar, ai, ar, ai)
    return rr, ri


def _tile_rows(i):
    return pl.ds(pl.multiple_of(i * SUBLANES, SUBLANES), SUBLANES)


def _segment_scan(z_re, z_im, ar, ai, lseg, reverse, visit=None):
    shape = (SUBLANES, z_re.shape[1])
    arb = jnp.broadcast_to(ar, shape)
    aib = jnp.broadcast_to(ai, shape)
    zero = jnp.zeros(shape, F32)
    row = lax.broadcasted_iota(jnp.int32, shape, 0)

    def tile_of(k):
        return lseg - 1 - k if reverse else k

    def advance(k, sr, si):
        rows = _tile_rows(tile_of(k))
        nr, ni = _cmul(arb, aib, sr, si)
        return rows, nr + z_re[rows, :], ni + z_im[rows, :]

    def first_pass(k, carry):
        _, nr, ni = advance(k, *carry)
        return nr, ni

    fr, fi = lax.fori_loop(0, lseg, first_pass, (zero, zero))
    pr, pi = _cpow(arb, aib, lseg)
    cr, ci = zero, zero
    for _ in range(SUBLANES - 1):
        tr, ti = _cmul(pr, pi, cr, ci)
        tr, ti = tr + fr, ti + fi
        if reverse:
            cr = jnp.where(row == SUBLANES - 1, 0.0, pltpu.roll(tr, SUBLANES - 1, 0))
            ci = jnp.where(row == SUBLANES - 1, 0.0, pltpu.roll(ti, SUBLANES - 1, 0))
        else:
            cr = jnp.where(row == 0, 0.0, pltpu.roll(tr, 1, 0))
            ci = jnp.where(row == 0, 0.0, pltpu.roll(ti, 1, 0))

    def second_pass(k, carry):
        sr, si, acc = carry
        rows, nr, ni = advance(k, sr, si)
        z_re[rows, :] = nr
        z_im[rows, :] = ni
        if visit is not None:
            acc = visit(tile_of(k), nr, ni, acc)
        return nr, ni, acc

    acc0 = (zero, zero) if visit is not None else 0
    return lax.fori_loop(0, lseg, second_pass, (cr, ci, acc0))[2]


def _gelu(y):
    c = 0.7978845608028654
    return 0.5 * y * (1.0 + jnp.tanh(c * (y + 0.044715 * y * y * y)))


def _gelu_grad(y):
    c = 0.7978845608028654
    th = jnp.tanh(c * (y + 0.044715 * y * y * y))
    return 0.5 * (1.0 + th) + 0.5 * y * (1.0 - th * th) * c * (1.0 + 3.0 * 0.044715 * y * y)


def _s5_specs(lp):
    col_u = pl.BlockSpec((lp, COL_U), lambda j: (0, j))
    row_u = pl.BlockSpec((1, COL_U), lambda j: (0, j))
    row_s = pl.BlockSpec((1, COL_S), lambda j: (0, j))
    b_mat = pl.BlockSpec((None, COL_U, COL_S), lambda j: (j, 0, 0))
    c_mat = pl.BlockSpec((None, COL_S, COL_U), lambda j: (j, 0, 0))
    g_mat = pl.BlockSpec((None, COL_U, COL_U), lambda j: (j, 0, 0))
    return col_u, row_u, row_s, b_mat, c_mat, g_mat


def _s5_fill_states(u_ref, bre_ref, bim_ref, ar_ref, ai_ref, s_re, s_im, lseg, n_chunks, chunk):
    def fill(cidx, carry):
        rows = pl.ds(pl.multiple_of(cidx * chunk, SUBLANES), chunk)
        ub = u_ref[rows, :].astype(BF16)
        s_re[rows, :] = jnp.dot(ub, bre_ref[...], preferred_element_type=F32)
        s_im[rows, :] = jnp.dot(ub, bim_ref[...], preferred_element_type=F32)
        return carry
    lax.fori_loop(0, n_chunks, fill, 0)
    _segment_scan(s_re, s_im, ar_ref[...], ai_ref[...], lseg, reverse=False)


def _s5_forward(u_p, ar, ai, bre_bd, bim_bd, cret_bd, cimt_bd, d_row, glu_bd, glub_row):
    lp = u_p.shape[0]
    lseg = lp // SUBLANES
    chunk, n_chunks = lseg, SUBLANES

    def body(u_ref, ar_ref, ai_ref, bre_ref, bim_ref, cret_ref, cimt_ref, d_ref, glu_ref, glub_ref,
             ys_ref, s_re, s_im):
        _s5_fill_states(u_ref, bre_ref, bim_ref, ar_ref, ai_ref, s_re, s_im, lseg, n_chunks, chunk)

        def emit(cidx, carry):
            rows = pl.ds(pl.multiple_of(cidx * chunk, SUBLANES), chunk)
            y = (_dot(s_re[rows, :], cret_ref[...]) - _dot(s_im[rows, :], cimt_ref[...])
                 + d_ref[...] * u_ref[rows, :])
            g = _gelu(y)
            gate = _dot(g, glu_ref[...]) + glub_ref[...]
            ys_ref[rows, :] = g * _sigmoid(gate)
            return carry
        lax.fori_loop(0, n_chunks, emit, 0)

    col_u, row_u, row_s, b_mat, c_mat, g_mat = _s5_specs(lp)
    return pl.pallas_call(
        body, name="s5_forward", grid=(N_COL,),
        in_specs=[col_u, row_s, row_s, b_mat, b_mat, c_mat, c_mat, row_u, g_mat, row_u],
        out_specs=col_u, out_shape=jax.ShapeDtypeStruct((lp, D_SSM), F32),
        scratch_shapes=[pltpu.VMEM((lp, COL_S), F32), pltpu.VMEM((lp, COL_S), F32)],
        compiler_params=_params(("arbitrary",), VMEM_LIMIT),
    )(u_p, ar, ai, bre_bd, bim_bd, cret_bd, cimt_bd, d_row, glu_bd, glub_row)


def _s5_backward(u_p, dys_p, ar, ai, bre_bd, bim_bd, cret_bd, cimt_bd, d_row, glu_bd, glub_row):
    lp = u_p.shape[0]
    lseg = lp // SUBLANES
    chunk, n_chunks = lseg, SUBLANES

    def body(u_ref, dys_ref, ar_ref, ai_ref, bre_ref, bim_ref, cret_ref, cimt_ref, d_ref, glu_ref, glub_ref,
             du_ref, dar_ref, dai_ref, dbre_ref, dbim_ref, dcre_ref, dcim_ref, dd_ref, dglu_ref, dglub_ref,
             s_re, s_im, q_re, q_im):
        _s5_fill_states(u_ref, bre_ref, bim_ref, ar_ref, ai_ref, s_re, s_im, lseg, n_chunks, chunk)
        for ref in (dcre_ref, dcim_ref, dd_ref, dglu_ref, dglub_ref, dbre_ref, dbim_ref):
            ref[...] = jnp.zeros(ref.shape, F32)

        def mixer_bwd(cidx, carry):
            rows = pl.ds(pl.multiple_of(cidx * chunk, SUBLANES), chunk)
            u = u_ref[rows, :]
            sr, si = s_re[rows, :], s_im[rows, :]
            y = _dot(sr, cret_ref[...]) - _dot(si, cimt_ref[...]) + d_ref[...] * u
            g = _gelu(y)
            sg = _sigmoid(_dot(g, glu_ref[...]) + glub_ref[...])
            dout = dys_ref[rows, :]
            dgate = dout * g * sg * (1.0 - sg)
            dy = (dout * sg + _dot_nt(dgate, glu_ref[...])) * _gelu_grad(y)
            dglu_ref[...] += _dot_tn(g, dgate)
            dglub_ref[...] += jnp.sum(dgate, axis=0, keepdims=True)
            dd_ref[...] += jnp.sum(dy * u, axis=0, keepdims=True)
            dcre_ref[...] += _dot_tn(dy, sr)
            dcim_ref[...] -= _dot_tn(dy, si)
            q_re[rows, :] = _dot_nt(dy, cret_ref[...])
            q_im[rows, :] = -_dot_nt(dy, cimt_ref[...])
            du_ref[rows, :] = d_ref[...] * dy
            return carry
        lax.fori_loop(0, n_chunks, mixer_bwd, 0)

        row = lax.broadcasted_iota(jnp.int32, (SUBLANES, COL_S), 0)

        def visit(i, qr, qi, acc):
            prev = _tile_rows(jnp.where(i == 0, lseg - 1, i - 1))
            pr, pi = s_re[prev, :], s_im[prev, :]
            first = i == 0
            pr = jnp.where(first, jnp.where(row == 0, 0.0, pltpu.roll(pr, 1, 0)), pr)
            pi = jnp.where(first, jnp.where(row == 0, 0.0, pltpu.roll(pi, 1, 0)), pi)
            return acc[0] + qr * pr + qi * pi, acc[1] + qi * pr - qr * pi

        dar, dai = _segment_scan(q_re, q_im, ar_ref[...], -ai_ref[...], lseg, reverse=True, visit=visit)
        dar_ref[...] = jnp.sum(dar, axis=0, keepdims=True)
        dai_ref[...] = jnp.sum(dai, axis=0, keepdims=True)

        def input_bwd(cidx, carry):
            rows = pl.ds(pl.multiple_of(cidx * chunk, SUBLANES), chunk)
            qr, qi = q_re[rows, :], q_im[rows, :]
            u = u_ref[rows, :]
            du_ref[rows, :] += _dot_nt(qr, bre_ref[...]) + _dot_nt(qi, bim_ref[...])
            dbre_ref[...] += _dot_tn(u, qr)
            dbim_ref[...] += _dot_tn(u, qi)
            return carry
        lax.fori_loop(0, n_chunks, input_bwd, 0)

    col_u, row_u, row_s, b_mat, c_mat, g_mat = _s5_specs(lp)
    return pl.pallas_call(
        body, name="s5_backward", grid=(N_COL,),
        in_specs=[col_u, col_u, row_s, row_s, b_mat, b_mat, c_mat, c_mat, row_u, g_mat, row_u],
        out_specs=[col_u, row_s, row_s, b_mat, b_mat, b_mat, b_mat, row_u, g_mat, row_u],
        out_shape=[jax.ShapeDtypeStruct((lp, D_SSM), F32),
                   jax.ShapeDtypeStruct((1, N_COL * COL_S), F32), jax.ShapeDtypeStruct((1, N_COL * COL_S), F32),
                   jax.ShapeDtypeStruct((N_COL, COL_U, COL_S), F32), jax.ShapeDtypeStruct((N_COL, COL_U, COL_S), F32),
                   jax.ShapeDtypeStruct((N_COL, COL_U, COL_S), F32), jax.ShapeDtypeStruct((N_COL, COL_U, COL_S), F32),
                   jax.ShapeDtypeStruct((1, D_SSM), F32),
                   jax.ShapeDtypeStruct((N_COL, COL_U, COL_U), F32), jax.ShapeDtypeStruct((1, D_SSM), F32)],
        scratch_shapes=[pltpu.VMEM((lp, COL_S), F32)] * 4,
        compiler_params=_params(("arbitrary",), VMEM_LIMIT),
    )(u_p, dys_p, ar, ai, bre_bd, bim_bd, cret_bd, cimt_bd, d_row, glu_bd, glub_row)


def _split3(x):
    hi = x.astype(BF16)
    r1 = x - hi.astype(F32)
    mid = r1.astype(BF16)
    lo = (r1 - mid.astype(F32)).astype(BF16)
    return hi, mid, lo


def _band_apply(band, x):
    hi, mid, lo = _split3(x)
    dot = functools.partial(jnp.dot, preferred_element_type=F32)
    return dot(band, hi) + dot(band, mid) + dot(band, lo)


def _pool_band(window, transposed):
    t = lax.broadcasted_iota(jnp.int32, (TM, TM + POOL_HALO), 0)
    c = lax.broadcasted_iota(jnp.int32, (TM, TM + POOL_HALO), 1)
    lag = c - t if transposed else t + POOL_HALO - c
    return jnp.where((lag >= 0) & (lag < window), 1.0, 0.0).astype(BF16)


def _pool_inv_count(tile, window, first_row):
    t = tile * TM + lax.broadcasted_iota(jnp.int32, (TM, 1), 0) - first_row
    return 1.0 / jnp.clip(t + 1, 1, window).astype(F32)


def _pool_specs(lp):
    col = pl.BlockSpec((lp, POOL_DIM), lambda k: (0, k))
    mat = pl.BlockSpec((None, POOL_DIM, POOL_DIM), lambda k: (k, 0, 0))
    row = pl.BlockSpec((None, 1, POOL_DIM), lambda k: (k, 0, 0))
    return col, mat, row


def _pool_forward(v, pool_w, pool_scale, first_row):
    lp = v.shape[0]
    n_tiles = lp // TM

    def body(v_ref, w_ref, sc_ref, yp_ref, vpad):
        window = jnp.left_shift(2, pl.program_id(0))
        vpad[pl.ds(0, POOL_HALO), :] = jnp.zeros((POOL_HALO, POOL_DIM), F32)
        vpad[pl.ds(POOL_HALO, lp), :] = v_ref[...]
        band = _pool_band(window, transposed=False)

        def tile(j, carry):
            start = pl.multiple_of(j * TM, TM)
            ext = vpad[pl.ds(start, TM + POOL_HALO), :]
            p = _band_apply(band, ext) * _pool_inv_count(j, window, first_row) - ext[POOL_HALO:, :]
            yp_ref[pl.ds(start, TM), :] = _dot(p, w_ref[...]) * sc_ref[...]
            return carry
        lax.fori_loop(0, n_tiles, tile, 0)

    col, mat, row = _pool_specs(lp)
    return pl.pallas_call(
        body, name="pool_forward", grid=(POOL_GROUPS,),
        in_specs=[col, mat, row], out_specs=col, out_shape=jax.ShapeDtypeStruct((lp, D_SSM), F32),
        scratch_shapes=[pltpu.VMEM((lp + POOL_HALO, POOL_DIM), F32)],
        compiler_params=_params(("arbitrary",), VMEM_LIMIT),
    )(v, pool_w, pool_scale)


def _pool_backward(v, dyp, pool_w, pool_scale, first_row):
    lp = v.shape[0]
    n_tiles = lp // TM

    def body(v_ref, dyp_ref, w_ref, sc_ref, dv_ref, dw_ref, dsc_ref, vpad, gpad):
        window = jnp.left_shift(2, pl.program_id(0))
        vpad[pl.ds(0, POOL_HALO), :] = jnp.zeros((POOL_HALO, POOL_DIM), F32)
        vpad[pl.ds(POOL_HALO, lp), :] = v_ref[...]
        gpad[pl.ds(lp, POOL_HALO), :] = jnp.zeros((POOL_HALO, POOL_DIM), F32)
        dw_ref[...] = jnp.zeros(dw_ref.shape, F32)
        dsc_ref[...] = jnp.zeros(dsc_ref.shape, F32)
        band = _pool_band(window, transposed=False)

        def linear_bwd(j, carry):
            start = pl.multiple_of(j * TM, TM)
            ext = vpad[pl.ds(start, TM + POOL_HALO), :]
            inv = _pool_inv_count(j, window, first_row)
            p = _band_apply(band, ext) * inv - ext[POOL_HALO:, :]
            z = _dot(p, w_ref[...])
            dyp_t = dyp_ref[pl.ds(start, TM), :]
            dz = dyp_t * sc_ref[...]
            dsc_ref[...] += jnp.sum(dyp_t * z, axis=0, keepdims=True)
            dw_ref[...] += _dot_tn(p, dz)
            dp = _dot_nt(dz, w_ref[...])
            gpad[pl.ds(start, TM), :] = dp * inv
            dv_ref[pl.ds(start, TM), :] = -dp
            return carry
        lax.fori_loop(0, n_tiles, linear_bwd, 0)
        band_t = _pool_band(window, transposed=True)

        def window_bwd(j, carry):
            start = pl.multiple_of(j * TM, TM)
            dv_ref[pl.ds(start, TM), :] += _band_apply(band_t, gpad[pl.ds(start, TM + POOL_HALO), :])
            return carry
        lax.fori_loop(0, n_tiles, window_bwd, 0)

    col, mat, row = _pool_specs(lp)
    return pl.pallas_call(
        body, name="pool_backward", grid=(POOL_GROUPS,),
        in_specs=[col, col, mat, row], out_specs=[col, mat, row],
        out_shape=[jax.ShapeDtypeStruct((lp, D_SSM), F32),
                   jax.ShapeDtypeStruct((POOL_GROUPS, POOL_DIM, POOL_DIM), F32),
                   jax.ShapeDtypeStruct((POOL_GROUPS, 1, POOL_DIM), F32)],
        scratch_shapes=[pltpu.VMEM((lp + POOL_HALO, POOL_DIM), F32)] * 2,
        compiler_params=_params(("arbitrary",), VMEM_LIMIT),
    )(v, dyp, pool_w, pool_scale)


def _row_specs():
    head = _const((HEAD, D_MODEL))
    xrow = pl.BlockSpec((TM, D_MODEL), _xrow)
    full = pl.BlockSpec((TM, D_MODEL), lambda i: (i, 0))
    half = pl.BlockSpec((TM, D_SSM), lambda i: (i, 0))
    return head, xrow, full, half


def _in_proj(head, x, g1, w_in):
    n_tiles = (HEAD + x.shape[0]) // TM
    lp = n_tiles * TM

    def body(head_ref, x_ref, g_ref, w_ref, u_ref, v_ref):
        h0 = jnp.where(pl.program_id(0) == 0, head_ref[...], x_ref[...])
        proj = _dot(h0 * _rstd(h0) * g_ref[...], w_ref[...])
        u_ref[...] = proj[:, :D_SSM]
        v_ref[...] = proj[:, D_SSM:]

    head_s, xrow, _, half = _row_specs()
    return pl.pallas_call(
        body, name="in_proj", grid=(n_tiles,),
        in_specs=[head_s, xrow, _const((1, D_MODEL)), _const((D_MODEL, D_MODEL))],
        out_specs=[half, half], out_shape=[jax.ShapeDtypeStruct((lp, D_SSM), F32)] * 2,
        compiler_params=_params(("parallel",), VMEM_LIMIT),
    )(head, x, g1, w_in)


def _out_proj(head, x, ys, yp, gs, gp, w_out):
    lp = ys.shape[0]

    def body(head_ref, x_ref, ys_ref, yp_ref, gs_ref, gp_ref, w_ref, h1_ref):
        h0 = jnp.where(pl.program_id(0) == 0, head_ref[...], x_ref[...])
        ys_t, yp_t = ys_ref[...], yp_ref[...]
        ms = ys_t * _rstd(ys_t) * gs_ref[...]
        mp = yp_t * _rstd(yp_t) * gp_ref[...]
        h1_ref[...] = h0 + _dot(ms, w_ref[pl.ds(0, D_SSM), :]) + _dot(mp, w_ref[pl.ds(D_SSM, D_SSM), :])

    head_s, xrow, full, half = _row_specs()
    return pl.pallas_call(
        body, name="out_proj", grid=(lp // TM,),
        in_specs=[head_s, xrow, half, half, _const((1, D_SSM)), _const((1, D_SSM)), _const((D_MODEL, D_MODEL))],
        out_specs=full, out_shape=jax.ShapeDtypeStruct((lp, D_MODEL), F32),
        compiler_params=_params(("parallel",), VMEM_LIMIT),
    )(head, x, ys, yp, gs, gp, w_out)


def _load_weights(c_hbm, wd_hbm, c_vmem, wd_vmem, sems):
    @pl.when(pl.program_id(0) == 0)
    def _():
        copies = [pltpu.make_async_copy(c_hbm, c_vmem, sems.at[0]),
                  pltpu.make_async_copy(wd_hbm, wd_vmem, sems.at[1])]
        for cp in copies:
            cp.start()
        for cp in copies:
            cp.wait()


def _ffn_scratch():
    return [pltpu.VMEM((N_DEV, D_MODEL, 2 * FF_PAD), BF16), pltpu.VMEM((N_DEV, FF_PAD, D_MODEL), BF16),
            pltpu.SemaphoreType.DMA((2,))]


def _ffn_forward(h1, g2, c_all, wd_all):
    lp = h1.shape[0]

    def body(h1_ref, g_ref, c_hbm, wd_hbm, ab_ref, n2_ref, h2_ref, c_vmem, wd_vmem, sems):
        _load_weights(c_hbm, wd_hbm, c_vmem, wd_vmem, sems)
        h1_t = h1_ref[...]
        n2 = (h1_t * _rstd(h1_t) * g_ref[...]).astype(BF16)
        n2_ref[...] = n2
        acc = h1_t
        for j in range(N_DEV):
            ab = jnp.dot(n2, c_vmem[j], preferred_element_type=F32)
            a, b = ab[:, :FF_PAD], ab[:, FF_PAD:]
            ab_ref[:, pl.ds(j * 2 * FF_PAD, 2 * FF_PAD)] = ab.astype(BF16)
            acc = acc + _dot(a * _sigmoid(a) * b, wd_vmem[j])
        h2_ref[...] = acc

    _, _, full, _ = _row_specs()
    wide = pl.BlockSpec((TM, N_DEV * 2 * FF_PAD), lambda i: (i, 0))
    return pl.pallas_call(
        body, name="ffn_forward", grid=(lp // TM,),
        in_specs=[full, _const((1, D_MODEL)), ANY, ANY], out_specs=[wide, full, full],
        out_shape=[jax.ShapeDtypeStruct((lp, N_DEV * 2 * FF_PAD), BF16),
                   jax.ShapeDtypeStruct((lp, D_MODEL), BF16), jax.ShapeDtypeStruct((lp, D_MODEL), F32)],
        scratch_shapes=_ffn_scratch(), compiler_params=_params(("arbitrary",), VMEM_LIMIT),
    )(h1, g2, c_all, wd_all)


def _ffn_backward(h2, target, h1, ab, gf, g2, c_all, wd_all):
    lp = h1.shape[0]

    def body(h2_ref, t_ref, h1_ref, ab_ref, gf_ref, g2_ref, c_hbm, wd_hbm,
             dh1_ref, dab_ref, dh2_ref, loss_ref, dgf_ref, dg2_ref, c_vmem, wd_vmem, sems):
        i = pl.program_id(0)
        _load_weights(c_hbm, wd_hbm, c_vmem, wd_vmem, sems)

        @pl.when(i == 0)
        def _():
            loss_ref[...] = jnp.zeros(loss_ref.shape, F32)
            dgf_ref[...] = jnp.zeros(dgf_ref.shape, F32)
            dg2_ref[...] = jnp.zeros(dg2_ref.shape, F32)

        h2_t = h2_ref[...]
        rf = _rstd(h2_t)
        xf = h2_t * rf
        diff = jnp.where(i == 0, 0.0, xf * gf_ref[...] - t_ref[...])
        loss_ref[...] += 0.5 * jnp.sum(diff * diff) / D_MODEL
        dh2, dgf = _rms_bwd(diff / D_MODEL, xf, rf, gf_ref[...])
        dgf_ref[...] += dgf
        dh2_b = dh2.astype(BF16)
        dh2_ref[...] = dh2_b

        dn2 = jnp.zeros((TM, D_MODEL), F32)
        for j in range(N_DEV):
            cols = pl.ds(j * 2 * FF_PAD, 2 * FF_PAD)
            dff = _dot_nt(dh2_b, wd_vmem[j])
            ab_t = ab_ref[:, cols].astype(F32)
            a, b = ab_t[:, :FF_PAD], ab_t[:, FF_PAD:]
            sg = _sigmoid(a)
            dab_ref[:, pl.ds(j * 2 * FF_PAD, FF_PAD)] = (dff * b * sg * (1.0 + a * (1.0 - sg))).astype(BF16)
            dab_ref[:, pl.ds(j * 2 * FF_PAD + FF_PAD, FF_PAD)] = (dff * a * sg).astype(BF16)
            dn2 = dn2 + _dot_nt(dab_ref[:, cols], c_vmem[j])

        h1_t = h1_ref[...]
        r2 = _rstd(h1_t)
        dx, dg2 = _rms_bwd(dn2, h1_t * r2, r2, g2_ref[...])
        dg2_ref[...] += dg2
        dh1_ref[...] = dh2 + dx

    _, xrow, full, _ = _row_specs()
    wide = pl.BlockSpec((TM, N_DEV * 2 * FF_PAD), lambda i: (i, 0))
    vec = _const((1, D_MODEL))
    return pl.pallas_call(
        body, name="ffn_backward", grid=(lp // TM,),
        in_specs=[full, xrow, full, wide, vec, vec, ANY, ANY],
        out_specs=[full, wide, full, _const((1, PACK_LANES)), vec, vec],
        out_shape=[jax.ShapeDtypeStruct((lp, D_MODEL), F32),
                   jax.ShapeDtypeStruct((lp, N_DEV * 2 * FF_PAD), BF16),
                   jax.ShapeDtypeStruct((lp, D_MODEL), BF16),
                   jax.ShapeDtypeStruct((1, PACK_LANES), F32),
                   jax.ShapeDtypeStruct((1, D_MODEL), F32), jax.ShapeDtypeStruct((1, D_MODEL), F32)],
        scratch_shapes=_ffn_scratch(), compiler_params=_params(("arbitrary",), VMEM_LIMIT),
    )(h2, target, h1, ab, gf, g2, c_all, wd_all)


def _ffn_wgrad(n2, dh2, ab, dab):
    lp = n2.shape[0]

    def body(n2_ref, dh2_ref, ab_ref, dab_ref, dc_ref, dwd_ref, dc_acc, dwd_acc):
        i = pl.program_id(1)

        @pl.when(i == 0)
        def _():
            dc_acc[...] = jnp.zeros(dc_acc.shape, F32)
            dwd_acc[...] = jnp.zeros(dwd_acc.shape, F32)

        ab_t = ab_ref[...].astype(F32)
        a, b = ab_t[:, :FF_PAD], ab_t[:, FF_PAD:]
        dc_acc[...] += _dot_tn(n2_ref[...], dab_ref[...])
        dwd_acc[...] += _dot_tn(a * _sigmoid(a) * b, dh2_ref[...])

        @pl.when(i == pl.num_programs(1) - 1)
        def _():
            dc_ref[...] = dc_acc[...].astype(BF16)
            dwd_ref[...] = dwd_acc[...].astype(BF16)

    act = pl.BlockSpec((TM, D_MODEL), lambda j, i: (i, 0))
    shard = pl.BlockSpec((TM, 2 * FF_PAD), lambda j, i: (i, j))
    return pl.pallas_call(
        body, name="ffn_wgrad", grid=(N_DEV, lp // TM),
        in_specs=[act, act, shard, shard],
        out_specs=[pl.BlockSpec((None, D_MODEL, 2 * FF_PAD), lambda j, i: (j, 0, 0)),
                   pl.BlockSpec((None, FF_PAD, D_MODEL), lambda j, i: (j, 0, 0))],
        out_shape=[jax.ShapeDtypeStruct((N_DEV, D_MODEL, 2 * FF_PAD), BF16),
                   jax.ShapeDtypeStruct((N_DEV, FF_PAD, D_MODEL), BF16)],
        scratch_shapes=[pltpu.VMEM((D_MODEL, 2 * FF_PAD), F32), pltpu.VMEM((FF_PAD, D_MODEL), F32)],
        compiler_params=_params(("parallel", "arbitrary"), VMEM_LIMIT),
    )(n2, dh2, ab, dab)


def _out_proj_backward(dh1, ys, yp, gs, gp, w_out):
    lp = ys.shape[0]

    def body(dh1_ref, ys_ref, yp_ref, gs_ref, gp_ref, w_ref, dys_ref, dyp_ref, dgs_ref, dgp_ref, dw_out, dw_ref):
        @pl.when(pl.program_id(0) == 0)
        def _():
            dgs_ref[...] = jnp.zeros(dgs_ref.shape, F32)
            dgp_ref[...] = jnp.zeros(dgp_ref.shape, F32)
            dw_ref[...] = jnp.zeros(dw_ref.shape, F32)

        dh1_b = dh1_ref[...].astype(BF16)
        dmix = _dot_nt(dh1_b, w_ref[...])
        for y_ref, g_ref, dy_ref, dg_ref, lo in ((ys_ref, gs_ref, dys_ref, dgs_ref, 0),
                                                 (yp_ref, gp_ref, dyp_ref, dgp_ref, D_SSM)):
            y_t = y_ref[...]
            r = _rstd(y_t)
            xhat = y_t * r
            dy, dg = _rms_bwd(dmix[:, lo:lo + D_SSM], xhat, r, g_ref[...])
            dy_ref[...] = dy
            dg_ref[...] += dg
            dw_ref[pl.ds(lo, D_SSM), :] += _dot_tn(xhat * g_ref[...], dh1_b)

        @pl.when(pl.program_id(0) == pl.num_programs(0) - 1)
        def _():
            dw_out[...] = dw_ref[...].astype(BF16)

    _, _, full, half = _row_specs()
    vec = _const((1, D_SSM))
    return pl.pallas_call(
        body, name="out_proj_backward", grid=(lp // TM,),
        in_specs=[full, half, half, vec, vec, _const((D_MODEL, D_MODEL))],
        out_specs=[half, half, vec, vec, _const((D_MODEL, D_MODEL))],
        out_shape=[jax.ShapeDtypeStruct((lp, D_SSM), F32)] * 2 + [jax.ShapeDtypeStruct((1, D_SSM), F32)] * 2
        + [jax.ShapeDtypeStruct((D_MODEL, D_MODEL), BF16)],
        scratch_shapes=[pltpu.VMEM((D_MODEL, D_MODEL), F32)],
        compiler_params=_params(("arbitrary",), VMEM_LIMIT),
    )(dh1, ys, yp, gs, gp, w_out)


def _in_proj_backward(head, x, du, dv, dh1, g1, w_in):
    lp = du.shape[0]

    def body(head_ref, x_ref, du_ref, dv_ref, dh1_ref, g_ref, w_ref, dx_ref, dhead_ref, dg_ref, dw_out, dw_ref):
        i = pl.program_id(0)

        @pl.when(i == 0)
        def _():
            dg_ref[...] = jnp.zeros(dg_ref.shape, F32)
            dw_ref[...] = jnp.zeros(dw_ref.shape, F32)

        h0 = jnp.where(i == 0, head_ref[...], x_ref[...])
        r = _rstd(h0)
        xhat = h0 * r
        n1 = (xhat * g_ref[...]).astype(BF16)
        du_b, dv_b = du_ref[...].astype(BF16), dv_ref[...].astype(BF16)
        dn1 = _dot_nt(du_b, w_ref[:, pl.ds(0, D_SSM)]) + _dot_nt(dv_b, w_ref[:, pl.ds(D_SSM, D_SSM)])
        dx, dg = _rms_bwd(dn1, xhat, r, g_ref[...])
        dg_ref[...] += dg
        dh0 = dh1_ref[...] + dx
        dx_ref[...] = dh0

        @pl.when(i == 0)
        def _():
            dhead_ref[...] = dh0

        dw_ref[:, pl.ds(0, D_SSM)] += _dot_tn(n1, du_b)
        dw_ref[:, pl.ds(D_SSM, D_SSM)] += _dot_tn(n1, dv_b)

        @pl.when(i == pl.num_programs(0) - 1)
        def _():
            dw_out[...] = dw_ref[...].astype(BF16)

    head_s, xrow, full, half = _row_specs()
    vec = _const((1, D_MODEL))
    mat = _const((D_MODEL, D_MODEL))
    return pl.pallas_call(
        body, name="in_proj_backward", grid=(lp // TM,),
        in_specs=[head_s, xrow, half, half, full, vec, mat],
        out_specs=[xrow, head_s, vec, mat],
        out_shape=[jax.ShapeDtypeStruct(x.shape, F32), jax.ShapeDtypeStruct((HEAD, D_MODEL), F32),
                   jax.ShapeDtypeStruct((1, D_MODEL), F32), jax.ShapeDtypeStruct((D_MODEL, D_MODEL), BF16)],
        scratch_shapes=[pltpu.VMEM((D_MODEL, D_MODEL), F32)],
        compiler_params=_params(("arbitrary",), VMEM_LIMIT),
    )(head, x, du, dv, dh1, g1, w_in)


def _permute_rows(a):
    lp, n = a.shape
    return a.reshape(SUBLANES, lp // SUBLANES, n).transpose(1, 0, 2).reshape(lp, n)


def _unpermute_rows(a):
    lp, n = a.shape
    return a.reshape(lp // SUBLANES, SUBLANES, n).transpose(1, 0, 2).reshape(lp, n)


def _block_diag(blocks):
    _, r, c = blocks.shape
    b = blocks.reshape(N_COL, GROUPS_PER_COL, r, 1, c)
    eye = jnp.eye(GROUPS_PER_COL, dtype=blocks.dtype).reshape(1, GROUPS_PER_COL, 1, GROUPS_PER_COL, 1)
    return (b * eye).reshape(N_COL, GROUPS_PER_COL * r, GROUPS_PER_COL * c)


def _block_diag_extract(mats, r, c):
    m = mats.reshape(N_COL, GROUPS_PER_COL, r, GROUPS_PER_COL, c)
    eye = jnp.eye(GROUPS_PER_COL, dtype=mats.dtype).reshape(1, GROUPS_PER_COL, 1, GROUPS_PER_COL, 1)
    return jnp.sum(m * eye, axis=3).reshape(SSM_GROUPS, r, c)


def _pack(parts, dtype):
    rows = []
    for p in parts:
        flat = p.reshape(-1).astype(dtype)
        pad = (-flat.shape[0]) % PACK_UNIT
        rows.append(jnp.pad(flat, (0, pad)).reshape(-1, PACK_LANES))
    n_rows = sum(r.shape[0] for r in rows)
    if n_rows % 16:
        rows.append(jnp.zeros((8, PACK_LANES), dtype))
    return jnp.concatenate(rows, axis=0)


def _as2d(a):
    return a.reshape(-1, a.shape[-1])


def _unpack(packed, shapes):
    out, row = [], 0
    for shape in shapes:
        size = 1
        for s in shape:
            size *= s
        n_rows = -(-size // PACK_UNIT) * 8
        out.append(packed[row:row + n_rows].reshape(-1)[:size].reshape(shape))
        row += n_rows
    return out


def _pad_cols(a):
    return jnp.pad(a, ((0, 0), (0, FF_PAD - FF_SHARD)))


def _pad_rows(a):
    return jnp.pad(a, ((0, FF_PAD - FF_SHARD), (0, 0)))


def _gate_up(gate, up):
    return jnp.concatenate([_pad_cols(gate), _pad_cols(up)], axis=1)


def kernel(x, meta_tokens, norm1_g, w_in, ssm_lambda_re, ssm_lambda_im, ssm_log_step, ssm_b_re, ssm_b_im, ssm_c_re, ssm_c_im, ssm_d, ssm_glu_w, ssm_glu_b, ssm_norm_g, pool_w, pool_scale, pool_norm_g, w_out, norm2_g, w_gate, w_up, w_down, final_norm_g, loss_target, m_meta_tokens, m_norm1_g, m_w_in, m_ssm_lambda_re, m_ssm_lambda_im, m_ssm_log_step, m_ssm_b_re, m_ssm_b_im, m_ssm_c_re, m_ssm_c_im, m_ssm_d, m_ssm_glu_w, m_ssm_glu_b, m_ssm_norm_g, m_pool_w, m_pool_scale, m_pool_norm_g, m_w_out, m_norm2_g, m_w_gate, m_w_up, m_w_down, m_final_norm_g, v_meta_tokens, v_norm1_g, v_w_in, v_ssm_lambda_re, v_ssm_lambda_im, v_ssm_log_step, v_ssm_b_re, v_ssm_b_im, v_ssm_c_re, v_ssm_c_im, v_ssm_d, v_ssm_glu_w, v_ssm_glu_b, v_ssm_norm_g, v_pool_w, v_pool_scale, v_pool_norm_g, v_w_out, v_norm2_g, v_w_gate, v_w_up, v_w_down, v_final_norm_g):
    given = dict(locals())
    weights = {n: given[n] for n in WEIGHT_NAMES}
    n_meta = meta_tokens.shape[0]
    me = 4 * lax.axis_index("x") + 2 * lax.axis_index("y") + lax.axis_index("c")

    shard_rows = w_in.shape[1]
    all_copies = [False, False, False]
    w_in_all, meta_all = _all_to_all([w_in[0].astype(BF16), meta_tokens], [False, False], "gather_w_in")
    w_in_all = w_in_all.reshape(D_MODEL, D_MODEL)
    meta_full = meta_all.transpose(1, 0, 2).reshape(n_meta, D_MODEL)
    gather, gather_token = _all_to_all_start(
        [w_out[0].astype(BF16), _pad_rows(w_down[0]).astype(BF16), _gate_up(w_gate[0], w_up[0]).astype(BF16)],
        all_copies, "gather_start")

    xs = x[0]
    tgt = loss_target[0]
    head = jnp.concatenate([jnp.zeros((HEAD - n_meta, D_MODEL), F32), meta_full], axis=0)
    first_row = HEAD - n_meta
    g1, g2, gf = norm1_g, norm2_g, final_norm_g.reshape(1, D_MODEL)
    gs, gp = ssm_norm_g, pool_norm_g

    lam_re, lam_im = ssm_lambda_re[0], ssm_lambda_im[0]
    log_step = ssm_log_step[0].reshape(SSM_GROUPS, 1)
    b_re = ssm_b_re[0].reshape(SSM_GROUPS * SSM_STATE, SSM_GROUP)
    b_im = ssm_b_im[0].reshape(SSM_GROUPS * SSM_STATE, SSM_GROUP)
    abr, abi, zr, zi = _s5_disc_a(lam_re, lam_im, log_step)
    zr_col, zi_col = zr.reshape(-1, 1), zi.reshape(-1, 1)
    bbr, bbi = _s5_disc_b(zr_col, zi_col, b_re, b_im)
    to_bd = lambda b: _block_diag(b.reshape(SSM_GROUPS, SSM_STATE, SSM_GROUP).transpose(0, 2, 1)).astype(BF16)
    bre_bd, bim_bd = to_bd(bbr), to_bd(bbi)
    cret_bd = _block_diag(ssm_c_re[0].transpose(0, 2, 1)).astype(BF16)
    cimt_bd = _block_diag(ssm_c_im[0].transpose(0, 2, 1)).astype(BF16)
    glu_bd = _block_diag(ssm_glu_w[0]).astype(BF16)
    s5_consts = (abr.reshape(1, -1), abi.reshape(1, -1), bre_bd, bim_bd, cret_bd, cimt_bd,
                 ssm_d[0].reshape(1, D_SSM), glu_bd, ssm_glu_b[0].reshape(1, D_SSM))
    pool_sc = pool_scale[0].reshape(POOL_GROUPS, 1, POOL_DIM)

    u, v = _in_proj(head, xs, g1 + gather_token[:1, :1], w_in_all)
    u_p = _permute_rows(u)
    ys = _unpermute_rows(_s5_forward(u_p, *s5_consts))
    yp = _pool_forward(v, pool_w[0], pool_sc, first_row)
    own_shards, gathered = _all_to_all_wait(gather, all_copies, yp, "gather_wait")
    w_out_all, wd_all, c_all = _fill_own(gathered, own_shards, all_copies, "gather_fill")
    w_out_all = w_out_all.reshape(D_MODEL, D_MODEL)
    h1 = _out_proj(head, xs, ys, yp, gs, gp, w_out_all)
    ab, n2, h2 = _ffn_forward(h1, g2, c_all, wd_all)

    dh1, dab, dh2, loss_part, d_gf, d_g2 = _ffn_backward(h2, tgt, h1, ab, gf, g2, c_all, wd_all)
    d_c, d_wd = _ffn_wgrad(n2, dh2, ab, dab)
    ffn_x, ffn_token = _all_to_all_start([d_c, d_wd], [True, True], "ffn_grad_start")
    dys, dyp, d_gs, d_gp, d_wout = _out_proj_backward(dh1, ys, yp, gs + ffn_token[:1, :1], gp, w_out_all)
    dv, d_pool_w, d_pool_sc = _pool_backward(v, dyp, pool_w[0], pool_sc, first_row)
    (du_p, d_ar, d_ai, d_bre_bd, d_bim_bd, d_cre_bd, d_cim_bd, d_d, d_glu_bd, d_glub) = _s5_backward(
        u_p, _permute_rows(dys), *s5_consts)
    du = _unpermute_rows(du_p)

    from_bd = lambda m: _block_diag_extract(m, SSM_GROUP, SSM_STATE).transpose(0, 2, 1)
    d_bbr = from_bd(d_bre_bd).reshape(SSM_GROUPS * SSM_STATE, SSM_GROUP)
    d_bbi = from_bd(d_bim_bd).reshape(SSM_GROUPS * SSM_STATE, SSM_GROUP)
    d_zr, d_zi, d_b_re, d_b_im = _s5_disc_b_bwd(zr_col, zi_col, b_re, b_im, d_bbr, d_bbi)
    d_lam_re, d_lam_im, d_log_step = _s5_disc_a_bwd(
        lam_re, lam_im, log_step,
        (d_ar.reshape(SSM_GROUPS, SSM_STATE), d_ai.reshape(SSM_GROUPS, SSM_STATE),
         d_zr.reshape(SSM_GROUPS, SSM_STATE), d_zi.reshape(SSM_GROUPS, SSM_STATE)))
    small_grads = {
        "ssm_lambda_re": d_lam_re, "ssm_lambda_im": d_lam_im, "ssm_log_step": d_log_step,
        "ssm_b_re": d_b_re, "ssm_b_im": d_b_im,
        "ssm_c_re": _block_diag_extract(d_cre_bd, SSM_GROUP, SSM_STATE),
        "ssm_c_im": _block_diag_extract(d_cim_bd, SSM_GROUP, SSM_STATE),
        "ssm_d": d_d, "ssm_glu_w": _block_diag_extract(d_glu_bd, SSM_GROUP, SSM_GROUP), "ssm_glu_b": d_glub,
        "ssm_norm_g": d_gs, "pool_w": d_pool_w, "pool_scale": d_pool_sc, "pool_norm_g": d_gp,
        "norm2_g": d_g2, "final_norm_g": d_gf,
    }

    early_names = SMALL_NAMES[1:-1]
    early_pack = _pack([small_grads[n] for n in early_names], BF16)
    early_x, early_token = _all_to_all_start(
        [d_wout.reshape(N_DEV, shard_rows, D_MODEL), early_pack], [True, False], "early_grad_start")
    d_x, d_head, d_g1, d_win = _in_proj_backward(head, xs, du, dv, dh1, g1 + early_token[:1, :1], w_in_all)
    ffn_src, ffn_land = _all_to_all_wait(ffn_x, [True, True], d_g1, "ffn_grad_wait")
    early_src, early_land = _all_to_all_wait(early_x, [True, False], d_g1, "early_grad_wait")
    late_pack = _pack([d_g1, d_gf, d_head[first_row:], loss_part], F32)
    r_win, r_late = _all_to_all([d_win.reshape(N_DEV, shard_rows, D_MODEL), late_pack], [True, False],
                                "late_grad_exchange")
    r_c, r_wd = _fill_own(ffn_land, ffn_src, [True, True], "ffn_grad_fill")
    r_wout, r_early = _fill_own(early_land, early_src, [True, False], "early_grad_fill")

    results = {}
    results["w_in"] = _adamw(r_win, w_in[0], m_w_in[0], v_w_in[0], shard_rows, "adamw_w_in")
    results["w_out"] = _adamw(r_wout, w_out[0], m_w_out[0], v_w_out[0], shard_rows, "adamw_w_out")
    res_wd = _adamw(r_wd, _pad_rows(w_down[0]), _pad_rows(m_w_down[0]), _pad_rows(v_w_down[0]), 128, "adamw_w_down")
    results["w_down"] = [r[:FF_SHARD] for r in res_wd]
    res_c = _adamw(r_c, _gate_up(w_gate[0], w_up[0]), _gate_up(m_w_gate[0], m_w_up[0]),
                   _gate_up(v_w_gate[0], v_w_up[0]), 128, "adamw_gate_up")
    results["w_gate"] = [r[:, :FF_SHARD] for r in res_c]
    results["w_up"] = [r[:, FF_PAD:FF_PAD + FF_SHARD] for r in res_c]

    sum_early, sum_late = _reduce_slots([r_early, r_late], "small_grad_sums")
    g_small = _unpack(sum_early, [weights[n].shape for n in early_names])
    g_norm1, g_final, g_meta_all, loss_row = _unpack(
        sum_late, [norm1_g.shape, final_norm_g.shape, (n_meta, D_MODEL), (1, PACK_LANES)])
    g_small = [g_norm1] + g_small + [g_final]
    small_2d = lambda prefix: [_as2d(given[prefix + n]) for n in SMALL_NAMES]
    res_small = _adamw_many([_as2d(g) for g in g_small], small_2d(""), small_2d("m_"), small_2d("v_"), "adamw_small")
    for idx, n in enumerate(SMALL_NAMES):
        results[n] = [g_small[idx]] + [part[idx] for part in res_small]
    shard_cols = meta_tokens.shape[1]
    g_meta = lax.dynamic_slice_in_dim(g_meta_all, me * shard_cols, shard_cols, axis=1)
    results["meta_tokens"] = _adamw(g_meta[None], meta_tokens, m_meta_tokens, v_meta_tokens, n_meta, "adamw_meta")

    out = [loss_row[0, 0], d_x[None]]
    for part in range(4):
        for n in WEIGHT_NAMES:
            out.append(results[n][part].reshape(weights[n].shape))
    return tuple(out)
```

```python
import functools

import jax
import jax.numpy as jnp
from jax import lax
from jax.experimental import pallas as pl
from jax.experimental.pallas import tpu as pltpu

F32 = jnp.float32
BF16 = jnp.bfloat16

N_DEV = 8
D_MODEL = 1024
D_SSM = 512
SSM_GROUP = 16
SSM_STATE = 64
SSM_GROUPS = 32
POOL_GROUPS = 4
POOL_DIM = 128
COL_U = 128
COL_S = 512
N_COL = D_SSM // COL_U
GROUPS_PER_COL = COL_U // SSM_GROUP
FF_SHARD = 352
FF_PAD = 384
TM = 256
HEAD = TM
SUBLANES = 8
POOL_HALO = 128
EPS = 1e-6
STEP_FLOOR = -1e-4
VMEM_LIMIT = 60 * 1024 * 1024

ADAM_LR = 0.001
ADAM_B1 = 0.9
ADAM_B2 = 0.999
ADAM_EPS = 1e-08
ADAM_WD = 0.01
ADAM_STEP = 10

MESH_ID = pl.DeviceIdType.MESH
ANY = pl.BlockSpec(memory_space=pl.ANY)

SMALL_NAMES = ("norm1_g", "ssm_lambda_re", "ssm_lambda_im", "ssm_log_step", "ssm_b_re", "ssm_b_im",
               "ssm_c_re", "ssm_c_im", "ssm_d", "ssm_glu_w", "ssm_glu_b", "ssm_norm_g", "pool_w",
               "pool_scale", "pool_norm_g", "norm2_g", "final_norm_g")
WEIGHT_NAMES = ("meta_tokens", "norm1_g", "w_in", "ssm_lambda_re", "ssm_lambda_im", "ssm_log_step",
                "ssm_b_re", "ssm_b_im", "ssm_c_re", "ssm_c_im", "ssm_d", "ssm_glu_w", "ssm_glu_b",
                "ssm_norm_g", "pool_w", "pool_scale", "pool_norm_g", "w_out", "norm2_g", "w_gate",
                "w_up", "w_down", "final_norm_g")
PACK_LANES = 128
PACK_UNIT = 8 * PACK_LANES


def _dot(a, b):
    return jnp.dot(a.astype(BF16), b.astype(BF16), preferred_element_type=F32)


def _dot_nt(a, b):
    return lax.dot_general(a.astype(BF16), b.astype(BF16), (((1,), (1,)), ((), ())), preferred_element_type=F32)


def _dot_tn(a, b):
    return lax.dot_general(a.astype(BF16), b.astype(BF16), (((0,), (0,)), ((), ())), preferred_element_type=F32)


def _sigmoid(x):
    return 1.0 / (1.0 + jnp.exp(-x))


def _rstd(x):
    return lax.rsqrt(jnp.mean(x * x, axis=-1, keepdims=True) + EPS)


def _rms_bwd(dy, xhat, r, g):
    dxh = dy * g
    dx = r * (dxh - xhat * jnp.mean(dxh * xhat, axis=-1, keepdims=True))
    return dx, jnp.sum(dy * xhat, axis=0, keepdims=True)


def _params(sem, vmem=None):
    return pltpu.CompilerParams(dimension_semantics=sem, vmem_limit_bytes=vmem)


def _const(shape):
    return pl.BlockSpec(shape, lambda *_: (0,) * len(shape))


def _xrow(i):
    return (jnp.maximum(i - 1, 0), 0)


HBM = pl.BlockSpec(memory_space=pltpu.HBM)
SEM = pl.BlockSpec(memory_space=pltpu.SEMAPHORE)
EFFECT = pltpu.SideEffectType.DATAFLOW_SIDE_EFFECTING
ALL_PEERS = tuple(range(1, N_DEV))
SIBLING = 1
CHIP_PEERS = (2, 4, 6)


def _me():
    return 4 * lax.axis_index("x") + 2 * lax.axis_index("y") + lax.axis_index("c")


def _peer(k):
    x, y, c = lax.axis_index("x"), lax.axis_index("y"), lax.axis_index("c")
    px = 1 - x if k & 4 else x
    py = 1 - y if k & 2 else y
    pc = 1 - c if k & 1 else c
    return (px, py, pc), 4 * px + 2 * py + pc


def _landing(arr, scatter):
    if scatter:
        own = lax.dynamic_index_in_dim(arr, _me(), 0, keepdims=False)
    else:
        own = arr
    return lax.dynamic_update_index_in_dim(lax.empty((N_DEV,) + own.shape, arr.dtype), own, _me(), 0)


def _push_copies(peers, scatter):
    n_arr = len(scatter)

    def make(refs, send_sems, recv_sems):
        copies = []
        for a in range(n_arr):
            for i, k in enumerate(peers):
                peer_id, peer = _peer(k)
                sem = a * len(peers) + i
                copies.append(pltpu.make_async_remote_copy(
                    src_ref=refs[a].at[peer] if scatter[a] else refs[a], dst_ref=refs[n_arr + a].at[_me()],
                    send_sem=send_sems.at[sem], recv_sem=recv_sems.at[sem],
                    device_id=peer_id, device_id_type=MESH_ID))
        return copies
    return make


def _forward_copies(n_arr):
    def make(refs, send_sems, recv_sems):
        copies = []
        sibling_id, _ = _peer(SIBLING)
        for a in range(n_arr):
            for i, k in enumerate(CHIP_PEERS):
                slot = refs[a].at[_peer(k)[1]]
                sem = a * len(CHIP_PEERS) + i
                copies.append(pltpu.make_async_remote_copy(
                    src_ref=slot, dst_ref=slot, send_sem=send_sems.at[sem], recv_sem=recv_sems.at[sem],
                    device_id=sibling_id, device_id_type=MESH_ID))
        return copies
    return make


def _exchange(arrays, scatter, name):
    n_arr = len(arrays)
    make = _push_copies(ALL_PEERS, scatter)
    n_sem = n_arr * len(ALL_PEERS)

    def body(*refs):
        send_sems, recv_sems = refs[-2:]
        token = refs[3 * n_arr]
        copies = make(refs[:n_arr] + refs[2 * n_arr:3 * n_arr], send_sems, recv_sems)
        for cp in copies:
            cp.start()
        token[...] = jnp.zeros(token.shape, token.dtype)
        for cp in copies:
            cp.wait()

    lands = [_landing(arr, sc) for arr, sc in zip(arrays, scatter)]
    out = pl.pallas_call(
        body, name=name,
        out_shape=[jax.ShapeDtypeStruct(t.shape, t.dtype) for t in lands] + [jax.ShapeDtypeStruct((8, PACK_LANES), F32)],
        in_specs=[ANY] * (2 * n_arr), out_specs=[ANY] * n_arr + [pl.BlockSpec(memory_space=pltpu.VMEM)],
        input_output_aliases={n_arr + i: i for i in range(n_arr)},
        scratch_shapes=[pltpu.SemaphoreType.DMA((n_sem,)), pltpu.SemaphoreType.DMA((n_sem,))],
    )(*arrays, *lands)
    return out[:n_arr], out[n_arr]


def _split_start(operands, make, n_sem, name):
    n_op = len(operands)

    def body(*refs):
        for cp in make(refs[:n_op], refs[n_op], refs[n_op + 1]):
            cp.start()
        refs[-1][...] = jnp.zeros(refs[-1].shape, F32)

    out = pl.pallas_call(
        body, name=name,
        out_shape=(pltpu.SemaphoreType.DMA((n_sem,)), pltpu.SemaphoreType.DMA((n_sem,)),
                   *[pltpu.HBM(t.shape, t.dtype) for t in operands], jax.ShapeDtypeStruct((8, PACK_LANES), F32)),
        in_specs=[HBM] * n_op, out_specs=(SEM, SEM, *[HBM] * n_op, pl.BlockSpec(memory_space=pltpu.VMEM)),
        input_output_aliases={i: 2 + i for i in range(n_op)},
        compiler_params=pltpu.CompilerParams(has_side_effects=EFFECT),
    )(*[pltpu.with_memory_space_constraint(t, pltpu.HBM) for t in operands])
    return out[:-1], out[-1]


def _split_wait(started, make, after, name):
    send_sems, recv_sems, thru = started[0], started[1], started[2:]
    n_op = len(thru)

    def body(*refs):
        for cp in make(refs[:n_op], refs[n_op], refs[n_op + 1]):
            cp.wait_send()
            cp.wait_recv()

    return pl.pallas_call(
        body, name=name, out_shape=tuple(pltpu.HBM(t.shape, t.dtype) for t in thru),
        in_specs=[HBM] * n_op + [SEM, SEM, ANY], out_specs=tuple([HBM] * n_op),
        input_output_aliases={i: i for i in range(n_op)},
        compiler_params=pltpu.CompilerParams(has_side_effects=EFFECT),
    )(*thru, send_sems, recv_sems, after)


def _adamw_math(g, w, m, v):
    nm = ADAM_B1 * m + (1.0 - ADAM_B1) * g
    nv = ADAM_B2 * v + (1.0 - ADAM_B2) * (g * g)
    m_hat = nm / (1.0 - ADAM_B1 ** ADAM_STEP)
    v_hat = nv / (1.0 - ADAM_B2 ** ADAM_STEP)
    return -ADAM_LR * (m_hat / (jnp.sqrt(v_hat) + ADAM_EPS) + ADAM_WD * w), nm, nv


def _sum_slots(s_ref):
    g = s_ref[0].astype(F32)
    for s in range(1, s_ref.shape[0]):
        g = g + s_ref[s].astype(F32)
    return g


def _adamw(slots, w, m, v, tile_rows, name):
    n, rows, cols = slots.shape

    def body(s_ref, w_ref, m_ref, v_ref, g_ref, d_ref, nm_ref, nv_ref):
        g = _sum_slots(s_ref)
        g_ref[...] = g
        d_ref[...], nm_ref[...], nv_ref[...] = _adamw_math(g, w_ref[...], m_ref[...], v_ref[...])

    tile = pl.BlockSpec((tile_rows, cols), lambda i: (i, 0))
    return pl.pallas_call(
        body, name=name, grid=(rows // tile_rows,),
        in_specs=[pl.BlockSpec((n, tile_rows, cols), lambda i: (0, i, 0)), tile, tile, tile],
        out_specs=[tile] * 4, out_shape=[jax.ShapeDtypeStruct((rows, cols), F32)] * 4,
        compiler_params=_params(("parallel",), VMEM_LIMIT),
    )(slots, w, m, v)


def _reduce_slots(slot_arrays, name):
    def body(*refs):
        n_arr = len(refs) // 2
        for s_ref, o_ref in zip(refs[:n_arr], refs[n_arr:]):
            o_ref[...] = _sum_slots(s_ref)
    return pl.pallas_call(
        body, name=name, out_shape=[jax.ShapeDtypeStruct(s.shape[1:], F32) for s in slot_arrays],
        compiler_params=_params(None, VMEM_LIMIT))(*slot_arrays)


def _adamw_many(grads, ws, ms, vs, name):
    n = len(grads)

    def body(*refs):
        ins, outs = refs[:4 * n], refs[4 * n:]
        for i in range(n):
            g, w, m, v = (ins[j * n + i][...] for j in range(4))
            outs[i][...], outs[n + i][...], outs[2 * n + i][...] = _adamw_math(g, w, m, v)

    out = pl.pallas_call(
        body, name=name, out_shape=[jax.ShapeDtypeStruct(w.shape, F32) for w in ws] * 3,
        compiler_params=_params(None, VMEM_LIMIT))(*grads, *ws, *ms, *vs)
    return out[:n], out[n:2 * n], out[2 * n:]


def _disc_a(lam_re, lam_im, log_step):
    lr = jnp.minimum(lam_re, STEP_FLOOR)
    step = jnp.exp(log_step)
    mag = jnp.exp(lr * step)
    ang = lam_im * step
    abr = mag * jnp.cos(ang)
    abi = mag * jnp.sin(ang)
    nr = abr - 1.0
    den = lr * lr + lam_im * lam_im
    cr = (nr * lr + abi * lam_im) / den
    ci = (abi * lr - nr * lam_im) / den
    return abr, abi, cr, ci


def _disc_b(cr, ci, b_re, b_im):
    return cr * b_re - ci * b_im, cr * b_im + ci * b_re


def _s5_disc_a(lam_re, lam_im, log_step):
    def body(lr_ref, li_ref, ls_ref, *outs):
        for o, val in zip(outs, _disc_a(lr_ref[...], li_ref[...], ls_ref[...])):
            o[...] = val
    return pl.pallas_call(body, name="s5_disc_a", out_shape=[jax.ShapeDtypeStruct(lam_re.shape, F32)] * 4)(
        lam_re, lam_im, log_step)


def _s5_disc_a_bwd(lam_re, lam_im, log_step, cts):
    def body(lr_ref, li_ref, ls_ref, c0, c1, c2, c3, dlr_ref, dli_ref, dls_ref):
        _, vjp = jax.vjp(_disc_a, lr_ref[...], li_ref[...], ls_ref[...])
        dlr, dli, dls = vjp((c0[...], c1[...], c2[...], c3[...]))
        dlr_ref[...] = dlr
        dli_ref[...] = dli
        dls_ref[...] = dls
    return pl.pallas_call(
        body, name="s5_disc_a_bwd",
        out_shape=[jax.ShapeDtypeStruct(lam_re.shape, F32), jax.ShapeDtypeStruct(lam_re.shape, F32),
                   jax.ShapeDtypeStruct(log_step.shape, F32)])(lam_re, lam_im, log_step, *cts)


def _s5_disc_b(cr, ci, b_re, b_im):
    def body(cr_ref, ci_ref, br_ref, bi_ref, o_re, o_im):
        o_re[...], o_im[...] = _disc_b(cr_ref[...], ci_ref[...], br_ref[...], bi_ref[...])
    return pl.pallas_call(body, name="s5_disc_b", out_shape=[jax.ShapeDtypeStruct(b_re.shape, F32)] * 2)(
        cr, ci, b_re, b_im)


def _s5_disc_b_bwd(cr, ci, b_re, b_im, d_re, d_im):
    def body(cr_ref, ci_ref, br_ref, bi_ref, dr_ref, di_ref, dcr_ref, dci_ref, dbr_ref, dbi_ref):
        _, vjp = jax.vjp(_disc_b, cr_ref[...], ci_ref[...], br_ref[...], bi_ref[...])
        dcr_ref[...], dci_ref[...], dbr_ref[...], dbi_ref[...] = vjp((dr_ref[...], di_ref[...]))
    return pl.pallas_call(
        body, name="s5_disc_b_bwd",
        out_shape=[jax.ShapeDtypeStruct(cr.shape, F32)] * 2 + [jax.ShapeDtypeStruct(b_re.shape, F32)] * 2)(
            cr, ci, b_re, b_im, d_re, d_im)


def _cmul(ar, ai, br, bi):
    return ar * br - ai * bi, ar * bi + ai * br


def _cpow(ar, ai, n):
    rr, ri = jnp.ones_like(ar), jnp.zeros_like(ai)
    while n:
        if n & 1:
            rr, ri = _cmul(rr, ri, ar, ai)
        n >>= 1
        if n:
            ar, ai = _cmul(ar, ai, ar, ai)
    return rr, ri


def _tile_rows(i):
    return pl.ds(pl.multiple_of(i * SUBLANES, SUBLANES), SUBLANES)


def _segment_scan(z_re, z_im, ar, ai, lseg, reverse, visit=None):
    shape = (SUBLANES, z_re.shape[1])
    arb = jnp.broadcast_to(ar, shape)
    aib = jnp.broadcast_to(ai, shape)
    zero = jnp.zeros(shape, F32)
    row = lax.broadcasted_iota(jnp.int32, shape, 0)

    def tile_of(k):
        return lseg - 1 - k if reverse else k

    def advance(k, sr, si):
        rows = _tile_rows(tile_of(k))
        nr, ni = _cmul(arb, aib, sr, si)
        return rows, nr + z_re[rows, :], ni + z_im[rows, :]

    def first_pass(k, carry):
        _, nr, ni = advance(k, *carry)
        return nr, ni

    fr, fi = lax.fori_loop(0, lseg, first_pass, (zero, zero))
    pr, pi = _cpow(arb, aib, lseg)
    cr, ci = zero, zero
    for _ in range(SUBLANES - 1):
        tr, ti = _cmul(pr, pi, cr, ci)
        tr, ti = tr + fr, ti + fi
        if reverse:
            cr = jnp.where(row == SUBLANES - 1, 0.0, pltpu.roll(tr, SUBLANES - 1, 0))
            ci = jnp.where(row == SUBLANES - 1, 0.0, pltpu.roll(ti, SUBLANES - 1, 0))
        else:
            cr = jnp.where(row == 0, 0.0, pltpu.roll(tr, 1, 0))
            ci = jnp.where(row == 0, 0.0, pltpu.roll(ti, 1, 0))

    def second_pass(k, carry):
        sr, si, acc = carry
        rows, nr, ni = advance(k, sr, si)
        z_re[rows, :] = nr
        z_im[rows, :] = ni
        if visit is not None:
            acc = visit(tile_of(k), nr, ni, acc)
        return nr, ni, acc

    acc0 = (zero, zero) if visit is not None else 0
    return lax.fori_loop(0, lseg, second_pass, (cr, ci, acc0))[2]


def _gelu(y):
    c = 0.7978845608028654
    return 0.5 * y * (1.0 + jnp.tanh(c * (y + 0.044715 * y * y * y)))


def _gelu_grad(y):
    c = 0.7978845608028654
    th = jnp.tanh(c * (y + 0.044715 * y * y * y))
    return 0.5 * (1.0 + th) + 0.5 * y * (1.0 - th * th) * c * (1.0 + 3.0 * 0.044715 * y * y)


def _s5_specs(lp):
    col_u = pl.BlockSpec((lp, COL_U), lambda j: (0, j))
    row_u = pl.BlockSpec((1, COL_U), lambda j: (0, j))
    row_s = pl.BlockSpec((1, COL_S), lambda j: (0, j))
    b_mat = pl.BlockSpec((None, COL_U, COL_S), lambda j: (j, 0, 0))
    c_mat = pl.BlockSpec((None, COL_S, COL_U), lambda j: (j, 0, 0))
    g_mat = pl.BlockSpec((None, COL_U, COL_U), lambda j: (j, 0, 0))
    return col_u, row_u, row_s, b_mat, c_mat, g_mat


def _s5_fill_states(u_ref, bre_ref, bim_ref, ar_ref, ai_ref, s_re, s_im, lseg, n_chunks, chunk):
    def fill(cidx, carry):
        rows = pl.ds(pl.multiple_of(cidx * chunk, SUBLANES), chunk)
        ub = u_ref[rows, :].astype(BF16)
        s_re[rows, :] = jnp.dot(ub, bre_ref[...], preferred_element_type=F32)
        s_im[rows, :] = jnp.dot(ub, bim_ref[...], preferred_element_type=F32)
        return carry
    lax.fori_loop(0, n_chunks, fill, 0)
    _segment_scan(s_re, s_im, ar_ref[...], ai_ref[...], lseg, reverse=False)


def _s5_forward(u_p, ar, ai, bre_bd, bim_bd, cret_bd, cimt_bd, d_row, glu_bd, glub_row):
    lp = u_p.shape[0]
    lseg = lp // SUBLANES
    chunk, n_chunks = lseg, SUBLANES

    def body(u_ref, ar_ref, ai_ref, bre_ref, bim_ref, cret_ref, cimt_ref, d_ref, glu_ref, glub_ref,
             ys_ref, s_re, s_im):
        _s5_fill_states(u_ref, bre_ref, bim_ref, ar_ref, ai_ref, s_re, s_im, lseg, n_chunks, chunk)

        def emit(cidx, carry):
            rows = pl.ds(pl.multiple_of(cidx * chunk, SUBLANES), chunk)
            y = (_dot(s_re[rows, :], cret_ref[...]) - _dot(s_im[rows, :], cimt_ref[...])
                 + d_ref[...] * u_ref[rows, :])
            g = _gelu(y)
            gate = _dot(g, glu_ref[...]) + glub_ref[...]
            ys_ref[rows, :] = g * _sigmoid(gate)
            return carry
        lax.fori_loop(0, n_chunks, emit, 0)

    col_u, row_u, row_s, b_mat, c_mat, g_mat = _s5_specs(lp)
    return pl.pallas_call(
        body, name="s5_forward", grid=(N_COL,),
        in_specs=[col_u, row_s, row_s, b_mat, b_mat, c_mat, c_mat, row_u, g_mat, row_u],
        out_specs=col_u, out_shape=jax.ShapeDtypeStruct((lp, D_SSM), F32),
        scratch_shapes=[pltpu.VMEM((lp, COL_S), F32), pltpu.VMEM((lp, COL_S), F32)],
        compiler_params=_params(("arbitrary",), VMEM_LIMIT),
    )(u_p, ar, ai, bre_bd, bim_bd, cret_bd, cimt_bd, d_row, glu_bd, glub_row)


def _s5_backward(u_p, dys_p, ar, ai, bre_bd, bim_bd, cret_bd, cimt_bd, d_row, glu_bd, glub_row):
    lp = u_p.shape[0]
    lseg = lp // SUBLANES
    chunk, n_chunks = lseg, SUBLANES

    def body(u_ref, dys_ref, ar_ref, ai_ref, bre_ref, bim_ref, cret_ref, cimt_ref, d_ref, glu_ref, glub_ref,
             du_ref, dar_ref, dai_ref, dbre_ref, dbim_ref, dcre_ref, dcim_ref, dd_ref, dglu_ref, dglub_ref,
             s_re, s_im, q_re, q_im):
        _s5_fill_states(u_ref, bre_ref, bim_ref, ar_ref, ai_ref, s_re, s_im, lseg, n_chunks, chunk)
        for ref in (dcre_ref, dcim_ref, dd_ref, dglu_ref, dglub_ref, dbre_ref, dbim_ref):
            ref[...] = jnp.zeros(ref.shape, F32)

        def mixer_bwd(cidx, carry):
            rows = pl.ds(pl.multiple_of(cidx * chunk, SUBLANES), chunk)
            u = u_ref[rows, :]
            sr, si = s_re[rows, :], s_im[rows, :]
            y = _dot(sr, cret_ref[...]) - _dot(si, cimt_ref[...]) + d_ref[...] * u
            g = _gelu(y)
            sg = _sigmoid(_dot(g, glu_ref[...]) + glub_ref[...])
            dout = dys_ref[rows, :]
            dgate = dout * g * sg * (1.0 - sg)
            dy = (dout * sg + _dot_nt(dgate, glu_ref[...])) * _gelu_grad(y)
            dglu_ref[...] += _dot_tn(g, dgate)
            dglub_ref[...] += jnp.sum(dgate, axis=0, keepdims=True)
            dd_ref[...] += jnp.sum(dy * u, axis=0, keepdims=True)
            dcre_ref[...] += _dot_tn(dy, sr)
            dcim_ref[...] -= _dot_tn(dy, si)
            q_re[rows, :] = _dot_nt(dy, cret_ref[...])
            q_im[rows, :] = -_dot_nt(dy, cimt_ref[...])
            du_ref[rows, :] = d_ref[...] * dy
            return carry
        lax.fori_loop(0, n_chunks, mixer_bwd, 0)

        row = lax.broadcasted_iota(jnp.int32, (SUBLANES, COL_S), 0)

        def visit(i, qr, qi, acc):
            prev = _tile_rows(jnp.where(i == 0, lseg - 1, i - 1))
            pr, pi = s_re[prev, :], s_im[prev, :]
            first = i == 0
            pr = jnp.where(first, jnp.where(row == 0, 0.0, pltpu.roll(pr, 1, 0)), pr)
            pi = jnp.where(first, jnp.where(row == 0, 0.0, pltpu.roll(pi, 1, 0)), pi)
            return acc[0] + qr * pr + qi * pi, acc[1] + qi * pr - qr * pi

        dar, dai = _segment_scan(q_re, q_im, ar_ref[...], -ai_ref[...], lseg, reverse=True, visit=visit)
        dar_ref[...] = jnp.sum(dar, axis=0, keepdims=True)
        dai_ref[...] = jnp.sum(dai, axis=0, keepdims=True)

        def input_bwd(cidx, carry):
            rows = pl.ds(pl.multiple_of(cidx * chunk, SUBLANES), chunk)
            qr, qi = q_re[rows, :], q_im[rows, :]
            u = u_ref[rows, :]
            du_ref[rows, :] += _dot_nt(qr, bre_ref[...]) + _dot_nt(qi, bim_ref[...])
            dbre_ref[...] += _dot_tn(u, qr)
            dbim_ref[...] += _dot_tn(u, qi)
            return carry
        lax.fori_loop(0, n_chunks, input_bwd, 0)

    col_u, row_u, row_s, b_mat, c_mat, g_mat = _s5_specs(lp)
    return pl.pallas_call(
        body, name="s5_backward", grid=(N_COL,),
        in_specs=[col_u, col_u, row_s, row_s, b_mat, b_mat, c_mat, c_mat, row_u, g_mat, row_u],
        out_specs=[col_u, row_s, row_s, b_mat, b_mat, b_mat, b_mat, row_u, g_mat, row_u],
        out_shape=[jax.ShapeDtypeStruct((lp, D_SSM), F32),
                   jax.ShapeDtypeStruct((1, N_COL * COL_S), F32), jax.ShapeDtypeStruct((1, N_COL * COL_S), F32),
                   jax.ShapeDtypeStruct((N_COL, COL_U, COL_S), F32), jax.ShapeDtypeStruct((N_COL, COL_U, COL_S), F32),
                   jax.ShapeDtypeStruct((N_COL, COL_U, COL_S), F32), jax.ShapeDtypeStruct((N_COL, COL_U, COL_S), F32),
                   jax.ShapeDtypeStruct((1, D_SSM), F32),
                   jax.ShapeDtypeStruct((N_COL, COL_U, COL_U), F32), jax.ShapeDtypeStruct((1, D_SSM), F32)],
        scratch_shapes=[pltpu.VMEM((lp, COL_S), F32)] * 4,
        compiler_params=_params(("arbitrary",), VMEM_LIMIT),
    )(u_p, dys_p, ar, ai, bre_bd, bim_bd, cret_bd, cimt_bd, d_row, glu_bd, glub_row)


def _split3(x):
    hi = x.astype(BF16)
    r1 = x - hi.astype(F32)
    mid = r1.astype(BF16)
    lo = (r1 - mid.astype(F32)).astype(BF16)
    return hi, mid, lo


def _band_apply(band, x):
    hi, mid, lo = _split3(x)
    dot = functools.partial(jnp.dot, preferred_element_type=F32)
    return dot(band, hi) + dot(band, mid) + dot(band, lo)


def _pool_band(window, transposed):
    t = lax.broadcasted_iota(jnp.int32, (TM, TM + POOL_HALO), 0)
    c = lax.broadcasted_iota(jnp.int32, (TM, TM + POOL_HALO), 1)
    lag = c - t if transposed else t + POOL_HALO - c
    return jnp.where((lag >= 0) & (lag < window), 1.0, 0.0).astype(BF16)


def _pool_inv_count(tile, window, first_row):
    t = tile * TM + lax.broadcasted_iota(jnp.int32, (TM, 1), 0) - first_row
    return 1.0 / jnp.clip(t + 1, 1, window).astype(F32)


def _pool_specs(lp):
    col = pl.BlockSpec((lp, POOL_DIM), lambda k: (0, k))
    mat = pl.BlockSpec((None, POOL_DIM, POOL_DIM), lambda k: (k, 0, 0))
    row = pl.BlockSpec((None, 1, POOL_DIM), lambda k: (k, 0, 0))
    return col, mat, row


def _pool_forward(v, pool_w, pool_scale, first_row):
    lp = v.shape[0]
    n_tiles = lp // TM

    def body(v_ref, w_ref, sc_ref, yp_ref, vpad):
        window = jnp.left_shift(2, pl.program_id(0))
        vpad[pl.ds(0, POOL_HALO), :] = jnp.zeros((POOL_HALO, POOL_DIM), F32)
        vpad[pl.ds(POOL_HALO, lp), :] = v_ref[...]
        band = _pool_band(window, transposed=False)

        def tile(j, carry):
            start = pl.multiple_of(j * TM, TM)
            ext = vpad[pl.ds(start, TM + POOL_HALO), :]
            p = _band_apply(band, ext) * _pool_inv_count(j, window, first_row) - ext[POOL_HALO:, :]
            yp_ref[pl.ds(start, TM), :] = _dot(p, w_ref[...]) * sc_ref[...]
            return carry
        lax.fori_loop(0, n_tiles, tile, 0)

    col, mat, row = _pool_specs(lp)
    return pl.pallas_call(
        body, name="pool_forward", grid=(POOL_GROUPS,),
        in_specs=[col, mat, row], out_specs=col, out_shape=jax.ShapeDtypeStruct((lp, D_SSM), F32),
        scratch_shapes=[pltpu.VMEM((lp + POOL_HALO, POOL_DIM), F32)],
        compiler_params=_params(("arbitrary",), VMEM_LIMIT),
    )(v, pool_w, pool_scale)


def _pool_backward(v, dyp, pool_w, pool_scale, first_row):
    lp = v.shape[0]
    n_tiles = lp // TM

    def body(v_ref, dyp_ref, w_ref, sc_ref, dv_ref, dw_ref, dsc_ref, vpad, gpad):
        window = jnp.left_shift(2, pl.program_id(0))
        vpad[pl.ds(0, POOL_HALO), :] = jnp.zeros((POOL_HALO, POOL_DIM), F32)
        vpad[pl.ds(POOL_HALO, lp), :] = v_ref[...]
        gpad[pl.ds(lp, POOL_HALO), :] = jnp.zeros((POOL_HALO, POOL_DIM), F32)
        dw_ref[...] = jnp.zeros(dw_ref.shape, F32)
        dsc_ref[...] = jnp.zeros(dsc_ref.shape, F32)
        band = _pool_band(window, transposed=False)

        def linear_bwd(j, carry):
            start = pl.multiple_of(j * TM, TM)
            ext = vpad[pl.ds(start, TM + POOL_HALO), :]
            inv = _pool_inv_count(j, window, first_row)
            p = _band_apply(band, ext) * inv - ext[POOL_HALO:, :]
            z = _dot(p, w_ref[...])
            dyp_t = dyp_ref[pl.ds(start, TM), :]
            dz = dyp_t * sc_ref[...]
            dsc_ref[...] += jnp.sum(dyp_t * z, axis=0, keepdims=True)
            dw_ref[...] += _dot_tn(p, dz)
            dp = _dot_nt(dz, w_ref[...])
            gpad[pl.ds(start, TM), :] = dp * inv
            dv_ref[pl.ds(start, TM), :] = -dp
            return carry
        lax.fori_loop(0, n_tiles, linear_bwd, 0)
        band_t = _pool_band(window, transposed=True)

        def window_bwd(j, carry):
            start = pl.multiple_of(j * TM, TM)
            dv_ref[pl.ds(start, TM), :] += _band_apply(band_t, gpad[pl.ds(start, TM + POOL_HALO), :])
            return carry
        lax.fori_loop(0, n_tiles, window_bwd, 0)

    col, mat, row = _pool_specs(lp)
    return pl.pallas_call(
        body, name="pool_backward", grid=(POOL_GROUPS,),
        in_specs=[col, col, mat, row], out_specs=[col, mat, row],
        out_shape=[jax.ShapeDtypeStruct((lp, D_SSM), F32),
                   jax.ShapeDtypeStruct((POOL_GROUPS, POOL_DIM, POOL_DIM), F32),
                   jax.ShapeDtypeStruct((POOL_GROUPS, 1, POOL_DIM), F32)],
        scratch_shapes=[pltpu.VMEM((lp + POOL_HALO, POOL_DIM), F32)] * 2,
        compiler_params=_params(("arbitrary",), VMEM_LIMIT),
    )(v, dyp, pool_w, pool_scale)


def _row_specs():
    head = _const((HEAD, D_MODEL))
    xrow = pl.BlockSpec((TM, D_MODEL), _xrow)
    full = pl.BlockSpec((TM, D_MODEL), lambda i: (i, 0))
    half = pl.BlockSpec((TM, D_SSM), lambda i: (i, 0))
    return head, xrow, full, half


def _in_proj(head, x, g1, w_in):
    n_tiles = (HEAD + x.shape[0]) // TM
    lp = n_tiles * TM

    def body(head_ref, x_ref, g_ref, w_ref, u_ref, v_ref):
        h0 = jnp.where(pl.program_id(0) == 0, head_ref[...], x_ref[...])
        proj = _dot(h0 * _rstd(h0) * g_ref[...], w_ref[...])
        u_ref[...] = proj[:, :D_SSM]
        v_ref[...] = proj[:, D_SSM:]

    head_s, xrow, _, half = _row_specs()
    return pl.pallas_call(
        body, name="in_proj", grid=(n_tiles,),
        in_specs=[head_s, xrow, _const((1, D_MODEL)), _const((D_MODEL, D_MODEL))],
        out_specs=[half, half], out_shape=[jax.ShapeDtypeStruct((lp, D_SSM), F32)] * 2,
        compiler_params=_params(("parallel",), VMEM_LIMIT),
    )(head, x, g1, w_in)


def _out_proj(head, x, ys, yp, gs, gp, w_out):
    lp = ys.shape[0]

    def body(head_ref, x_ref, ys_ref, yp_ref, gs_ref, gp_ref, w_ref, h1_ref):
        h0 = jnp.where(pl.program_id(0) == 0, head_ref[...], x_ref[...])
        ys_t, yp_t = ys_ref[...], yp_ref[...]
        ms = ys_t * _rstd(ys_t) * gs_ref[...]
        mp = yp_t * _rstd(yp_t) * gp_ref[...]
        h1_ref[...] = h0 + _dot(ms, w_ref[pl.ds(0, D_SSM), :]) + _dot(mp, w_ref[pl.ds(D_SSM, D_SSM), :])

    head_s, xrow, full, half = _row_specs()
    return pl.pallas_call(
        body, name="out_proj", grid=(lp // TM,),
        in_specs=[head_s, xrow, half, half, _const((1, D_SSM)), _const((1, D_SSM)), _const((D_MODEL, D_MODEL))],
        out_specs=full, out_shape=jax.ShapeDtypeStruct((lp, D_MODEL), F32),
        compiler_params=_params(("parallel",), VMEM_LIMIT),
    )(head, x, ys, yp, gs, gp, w_out)


def _load_weights(c_hbm, wd_hbm, c_vmem, wd_vmem, sems):
    @pl.when(pl.program_id(0) == 0)
    def _():
        copies = [pltpu.make_async_copy(c_hbm, c_vmem, sems.at[0]),
                  pltpu.make_async_copy(wd_hbm, wd_vmem, sems.at[1])]
        for cp in copies:
            cp.start()
        for cp in copies:
            cp.wait()


def _ffn_scratch():
    return [pltpu.VMEM((N_DEV, D_MODEL, 2 * FF_PAD), BF16), pltpu.VMEM((N_DEV, FF_PAD, D_MODEL), BF16),
            pltpu.SemaphoreType.DMA((2,))]


def _ffn_forward(h1, g2, c_all, wd_all):
    lp = h1.shape[0]

    def body(h1_ref, g_ref, c_hbm, wd_hbm, ab_ref, n2_ref, h2_ref, c_vmem, wd_vmem, sems):
        _load_weights(c_hbm, wd_hbm, c_vmem, wd_vmem, sems)
        h1_t = h1_ref[...]
        n2 = (h1_t * _rstd(h1_t) * g_ref[...]).astype(BF16)
        n2_ref[...] = n2
        acc = h1_t
        for j in range(N_DEV):
            ab = jnp.dot(n2, c_vmem[j], preferred_element_type=F32)
            a, b = ab[:, :FF_PAD], ab[:, FF_PAD:]
            ab_ref[:, pl.ds(j * 2 * FF_PAD, 2 * FF_PAD)] = ab.astype(BF16)
            acc = acc + _dot(a * _sigmoid(a) * b, wd_vmem[j])
        h2_ref[...] = acc

    _, _, full, _ = _row_specs()
    wide = pl.BlockSpec((TM, N_DEV * 2 * FF_PAD), lambda i: (i, 0))
    return pl.pallas_call(
        body, name="ffn_forward", grid=(lp // TM,),
        in_specs=[full, _const((1, D_MODEL)), ANY, ANY], out_specs=[wide, full, full],
        out_shape=[jax.ShapeDtypeStruct((lp, N_DEV * 2 * FF_PAD), BF16),
                   jax.ShapeDtypeStruct((lp, D_MODEL), BF16), jax.ShapeDtypeStruct((lp, D_MODEL), F32)],
        scratch_shapes=_ffn_scratch(), compiler_params=_params(("arbitrary",), VMEM_LIMIT),
    )(h1, g2, c_all, wd_all)


def _ffn_backward(h2, target, h1, ab, gf, g2, c_all, wd_all):
    lp = h1.shape[0]

    def body(h2_ref, t_ref, h1_ref, ab_ref, gf_ref, g2_ref, c_hbm, wd_hbm,
             dh1_ref, dab_ref, dh2_ref, loss_ref, dgf_ref, dg2_ref, c_vmem, wd_vmem, sems):
        i = pl.program_id(0)
        _load_weights(c_hbm, wd_hbm, c_vmem, wd_vmem, sems)

        @pl.when(i == 0)
        def _():
            loss_ref[...] = jnp.zeros(loss_ref.shape, F32)
            dgf_ref[...] = jnp.zeros(dgf_ref.shape, F32)
            dg2_ref[...] = jnp.zeros(dg2_ref.shape, F32)

        h2_t = h2_ref[...]
        rf = _rstd(h2_t)
        xf = h2_t * rf
        diff = jnp.where(i == 0, 0.0, xf * gf_ref[...] - t_ref[...])
        loss_ref[...] += 0.5 * jnp.sum(diff * diff) / D_MODEL
        dh2, dgf = _rms_bwd(diff / D_MODEL, xf, rf, gf_ref[...])
        dgf_ref[...] += dgf
        dh2_b = dh2.astype(BF16)
        dh2_ref[...] = dh2_b

        dn2 = jnp.zeros((TM, D_MODEL), F32)
        for j in range(N_DEV):
            cols = pl.ds(j * 2 * FF_PAD, 2 * FF_PAD)
            dff = _dot_nt(dh2_b, wd_vmem[j])
            ab_t = ab_ref[:, cols].astype(F32)
            a, b = ab_t[:, :FF_PAD], ab_t[:, FF_PAD:]
            sg = _sigmoid(a)
            dab_ref[:, pl.ds(j * 2 * FF_PAD, FF_PAD)] = (dff * b * sg * (1.0 + a * (1.0 - sg))).astype(BF16)
            dab_ref[:, pl.ds(j * 2 * FF_PAD + FF_PAD, FF_PAD)] = (dff * a * sg).astype(BF16)
            dn2 = dn2 + _dot_nt(dab_ref[:, cols], c_vmem[j])

        h1_t = h1_ref[...]
        r2 = _rstd(h1_t)
        dx, dg2 = _rms_bwd(dn2, h1_t * r2, r2, g2_ref[...])
        dg2_ref[...] += dg2
        dh1_ref[...] = dh2 + dx

    _, xrow, full, _ = _row_specs()
    wide = pl.BlockSpec((TM, N_DEV * 2 * FF_PAD), lambda i: (i, 0))
    vec = _const((1, D_MODEL))
    return pl.pallas_call(
        body, name="ffn_backward", grid=(lp // TM,),
        in_specs=[full, xrow, full, wide, vec, vec, ANY, ANY],
        out_specs=[full, wide, full, _const((1, PACK_LANES)), vec, vec],
        out_shape=[jax.ShapeDtypeStruct((lp, D_MODEL), F32),
                   jax.ShapeDtypeStruct((lp, N_DEV * 2 * FF_PAD), BF16),
                   jax.ShapeDtypeStruct((lp, D_MODEL), BF16),
                   jax.ShapeDtypeStruct((1, PACK_LANES), F32),
                   jax.ShapeDtypeStruct((1, D_MODEL), F32), jax.ShapeDtypeStruct((1, D_MODEL), F32)],
        scratch_shapes=_ffn_scratch(), compiler_params=_params(("arbitrary",), VMEM_LIMIT),
    )(h2, target, h1, ab, gf, g2, c_all, wd_all)


def _ffn_wgrad(n2, dh2, ab, dab):
    lp = n2.shape[0]

    def body(n2_ref, dh2_ref, ab_ref, dab_ref, dc_ref, dwd_ref, dc_acc, dwd_acc):
        i = pl.program_id(1)

        @pl.when(i == 0)
        def _():
            dc_acc[...] = jnp.zeros(dc_acc.shape, F32)
            dwd_acc[...] = jnp.zeros(dwd_acc.shape, F32)

        ab_t = ab_ref[...].astype(F32)
        a, b = ab_t[:, :FF_PAD], ab_t[:, FF_PAD:]
        dc_acc[...] += _dot_tn(n2_ref[...], dab_ref[...])
        dwd_acc[...] += _dot_tn(a * _sigmoid(a) * b, dh2_ref[...])

        @pl.when(i == pl.num_programs(1) - 1)
        def _():
            dc_ref[...] = dc_acc[...].astype(BF16)
            dwd_ref[...] = dwd_acc[...].astype(BF16)

    act = pl.BlockSpec((TM, D_MODEL), lambda j, i: (i, 0))
    shard = pl.BlockSpec((TM, 2 * FF_PAD), lambda j, i: (i, j))
    return pl.pallas_call(
        body, name="ffn_wgrad", grid=(N_DEV, lp // TM),
        in_specs=[act, act, shard, shard],
        out_specs=[pl.BlockSpec((None, D_MODEL, 2 * FF_PAD), lambda j, i: (j, 0, 0)),
                   pl.BlockSpec((None, FF_PAD, D_MODEL), lambda j, i: (j, 0, 0))],
        out_shape=[jax.ShapeDtypeStruct((N_DEV, D_MODEL, 2 * FF_PAD), BF16),
                   jax.ShapeDtypeStruct((N_DEV, FF_PAD, D_MODEL), BF16)],
        scratch_shapes=[pltpu.VMEM((D_MODEL, 2 * FF_PAD), F32), pltpu.VMEM((FF_PAD, D_MODEL), F32)],
        compiler_params=_params(("parallel", "arbitrary"), VMEM_LIMIT),
    )(n2, dh2, ab, dab)


def _out_proj_backward(dh1, ys, yp, gs, gp, w_out):
    lp = ys.shape[0]

    def body(dh1_ref, ys_ref, yp_ref, gs_ref, gp_ref, w_ref, dys_ref, dyp_ref, dgs_ref, dgp_ref, dw_out, dw_ref):
        @pl.when(pl.program_id(0) == 0)
        def _():
            dgs_ref[...] = jnp.zeros(dgs_ref.shape, F32)
            dgp_ref[...] = jnp.zeros(dgp_ref.shape, F32)
            dw_ref[...] = jnp.zeros(dw_ref.shape, F32)

        dh1_b = dh1_ref[...].astype(BF16)
        dmix = _dot_nt(dh1_b, w_ref[...])
        for y_ref, g_ref, dy_ref, dg_ref, lo in ((ys_ref, gs_ref, dys_ref, dgs_ref, 0),
                                                 (yp_ref, gp_ref, dyp_ref, dgp_ref, D_SSM)):
            y_t = y_ref[...]
            r = _rstd(y_t)
            xhat = y_t * r
            dy, dg = _rms_bwd(dmix[:, lo:lo + D_SSM], xhat, r, g_ref[...])
            dy_ref[...] = dy
            dg_ref[...] += dg
            dw_ref[pl.ds(lo, D_SSM), :] += _dot_tn(xhat * g_ref[...], dh1_b)

        @pl.when(pl.program_id(0) == pl.num_programs(0) - 1)
        def _():
            dw_out[...] = dw_ref[...].astype(BF16)

    _, _, full, half = _row_specs()
    vec = _const((1, D_SSM))
    return pl.pallas_call(
        body, name="out_proj_backward", grid=(lp // TM,),
        in_specs=[full, half, half, vec, vec, _const((D_MODEL, D_MODEL))],
        out_specs=[half, half, vec, vec, _const((D_MODEL, D_MODEL))],
        out_shape=[jax.ShapeDtypeStruct((lp, D_SSM), F32)] * 2 + [jax.ShapeDtypeStruct((1, D_SSM), F32)] * 2
        + [jax.ShapeDtypeStruct((D_MODEL, D_MODEL), BF16)],
        scratch_shapes=[pltpu.VMEM((D_MODEL, D_MODEL), F32)],
        compiler_params=_params(("arbitrary",), VMEM_LIMIT),
    )(dh1, ys, yp, gs, gp, w_out)


def _in_proj_backward(head, x, du, dv, dh1, g1, w_in):
    lp = du.shape[0]

    def body(head_ref, x_ref, du_ref, dv_ref, dh1_ref, g_ref, w_ref, dx_ref, dhead_ref, dg_ref, dw_out, dw_ref):
        i = pl.program_id(0)

        @pl.when(i == 0)
        def _():
            dg_ref[...] = jnp.zeros(dg_ref.shape, F32)
            dw_ref[...] = jnp.zeros(dw_ref.shape, F32)

        h0 = jnp.where(i == 0, head_ref[...], x_ref[...])
        r = _rstd(h0)
        xhat = h0 * r
        n1 = (xhat * g_ref[...]).astype(BF16)
        du_b, dv_b = du_ref[...].astype(BF16), dv_ref[...].astype(BF16)
        dn1 = _dot_nt(du_b, w_ref[:, pl.ds(0, D_SSM)]) + _dot_nt(dv_b, w_ref[:, pl.ds(D_SSM, D_SSM)])
        dx, dg = _rms_bwd(dn1, xhat, r, g_ref[...])
        dg_ref[...] += dg
        dh0 = dh1_ref[...] + dx
        dx_ref[...] = dh0

        @pl.when(i == 0)
        def _():
            dhead_ref[...] = dh0

        dw_ref[:, pl.ds(0, D_SSM)] += _dot_tn(n1, du_b)
        dw_ref[:, pl.ds(D_SSM, D_SSM)] += _dot_tn(n1, dv_b)

        @pl.when(i == pl.num_programs(0) - 1)
        def _():
            dw_out[...] = dw_ref[...].astype(BF16)

    head_s, xrow, full, half = _row_specs()
    vec = _const((1, D_MODEL))
    mat = _const((D_MODEL, D_MODEL))
    return pl.pallas_call(
        body, name="in_proj_backward", grid=(lp // TM,),
        in_specs=[head_s, xrow, half, half, full, vec, mat],
        out_specs=[xrow, head_s, vec, mat],
        out_shape=[jax.ShapeDtypeStruct(x.shape, F32), jax.ShapeDtypeStruct((HEAD, D_MODEL), F32),
                   jax.ShapeDtypeStruct((1, D_MODEL), F32), jax.ShapeDtypeStruct((D_MODEL, D_MODEL), BF16)],
        scratch_shapes=[pltpu.VMEM((D_MODEL, D_MODEL), F32)],
        compiler_params=_params(("arbitrary",), VMEM_LIMIT),
    )(head, x, du, dv, dh1, g1, w_in)


def _permute_rows(a):
    lp, n = a.shape
    return a.reshape(SUBLANES, lp // SUBLANES, n).transpose(1, 0, 2).reshape(lp, n)


def _unpermute_rows(a):
    lp, n = a.shape
    return a.reshape(lp // SUBLANES, SUBLANES, n).transpose(1, 0, 2).reshape(lp, n)


def _block_diag(blocks):
    _, r, c = blocks.shape
    b = blocks.reshape(N_COL, GROUPS_PER_COL, r, 1, c)
    eye = jnp.eye(GROUPS_PER_COL, dtype=blocks.dtype).reshape(1, GROUPS_PER_COL, 1, GROUPS_PER_COL, 1)
    return (b * eye).reshape(N_COL, GROUPS_PER_COL * r, GROUPS_PER_COL * c)


def _block_diag_extract(mats, r, c):
    m = mats.reshape(N_COL, GROUPS_PER_COL, r, GROUPS_PER_COL, c)
    eye = jnp.eye(GROUPS_PER_COL, dtype=mats.dtype).reshape(1, GROUPS_PER_COL, 1, GROUPS_PER_COL, 1)
    return jnp.sum(m * eye, axis=3).reshape(SSM_GROUPS, r, c)


def _pack(parts, dtype):
    rows = []
    for p in parts:
        flat = p.reshape(-1).astype(dtype)
        pad = (-flat.shape[0]) % PACK_UNIT
        rows.append(jnp.pad(flat, (0, pad)).reshape(-1, PACK_LANES))
    n_rows = sum(r.shape[0] for r in rows)
    if n_rows % 16:
        rows.append(jnp.zeros((8, PACK_LANES), dtype))
    return jnp.concatenate(rows, axis=0)


def _as2d(a):
    return a.reshape(-1, a.shape[-1])


def _unpack(packed, shapes):
    out, row = [], 0
    for shape in shapes:
        size = 1
        for s in shape:
            size *= s
        n_rows = -(-size // PACK_UNIT) * 8
        out.append(packed[row:row + n_rows].reshape(-1)[:size].reshape(shape))
        row += n_rows
    return out


def _pad_cols(a):
    return jnp.pad(a, ((0, 0), (0, FF_PAD - FF_SHARD)))


def _pad_rows(a):
    return jnp.pad(a, ((0, FF_PAD - FF_SHARD), (0, 0)))


def _gate_up(gate, up):
    return jnp.concatenate([_pad_cols(gate), _pad_cols(up)], axis=1)


def kernel(x, meta_tokens, norm1_g, w_in, ssm_lambda_re, ssm_lambda_im, ssm_log_step, ssm_b_re, ssm_b_im, ssm_c_re, ssm_c_im, ssm_d, ssm_glu_w, ssm_glu_b, ssm_norm_g, pool_w, pool_scale, pool_norm_g, w_out, norm2_g, w_gate, w_up, w_down, final_norm_g, loss_target, m_meta_tokens, m_norm1_g, m_w_in, m_ssm_lambda_re, m_ssm_lambda_im, m_ssm_log_step, m_ssm_b_re, m_ssm_b_im, m_ssm_c_re, m_ssm_c_im, m_ssm_d, m_ssm_glu_w, m_ssm_glu_b, m_ssm_norm_g, m_pool_w, m_pool_scale, m_pool_norm_g, m_w_out, m_norm2_g, m_w_gate, m_w_up, m_w_down, m_final_norm_g, v_meta_tokens, v_norm1_g, v_w_in, v_ssm_lambda_re, v_ssm_lambda_im, v_ssm_log_step, v_ssm_b_re, v_ssm_b_im, v_ssm_c_re, v_ssm_c_im, v_ssm_d, v_ssm_glu_w, v_ssm_glu_b, v_ssm_norm_g, v_pool_w, v_pool_scale, v_pool_norm_g, v_w_out, v_norm2_g, v_w_gate, v_w_up, v_w_down, v_final_norm_g):
    given = dict(locals())
    weights = {n: given[n] for n in WEIGHT_NAMES}
    n_meta = meta_tokens.shape[0]
    me = 4 * lax.axis_index("x") + 2 * lax.axis_index("y") + lax.axis_index("c")

    shard_rows = w_in.shape[1]
    (w_in_all, meta_all), first_token = _exchange([w_in[0].astype(BF16), meta_tokens], [False, False], "gather_w_in")
    w_in_all = w_in_all.reshape(D_MODEL, D_MODEL)
    meta_full = meta_all.transpose(1, 0, 2).reshape(n_meta, D_MODEL)
    broadcast3 = [False, False, False]
    shards = [(w_out[0] + first_token[:1, :1]).astype(BF16), _pad_rows(w_down[0]).astype(BF16),
              _gate_up(w_gate[0], w_up[0]).astype(BF16)]
    gather_make = _push_copies((SIBLING,) + CHIP_PEERS, broadcast3)
    gather, gather_token = _split_start(shards + [_landing(s, False) for s in shards], gather_make,
                                        3 * (1 + len(CHIP_PEERS)), "gather_start")

    xs = x[0]
    tgt = loss_target[0]
    head = jnp.concatenate([jnp.zeros((HEAD - n_meta, D_MODEL), F32), meta_full], axis=0)
    first_row = HEAD - n_meta
    g1, g2, gf = norm1_g, norm2_g, final_norm_g.reshape(1, D_MODEL)
    gs, gp = ssm_norm_g, pool_norm_g

    lam_re, lam_im = ssm_lambda_re[0], ssm_lambda_im[0]
    log_step = ssm_log_step[0].reshape(SSM_GROUPS, 1)
    b_re = ssm_b_re[0].reshape(SSM_GROUPS * SSM_STATE, SSM_GROUP)
    b_im = ssm_b_im[0].reshape(SSM_GROUPS * SSM_STATE, SSM_GROUP)
    abr, abi, zr, zi = _s5_disc_a(lam_re, lam_im, log_step)
    zr_col, zi_col = zr.reshape(-1, 1), zi.reshape(-1, 1)
    bbr, bbi = _s5_disc_b(zr_col, zi_col, b_re, b_im)
    to_bd = lambda b: _block_diag(b.reshape(SSM_GROUPS, SSM_STATE, SSM_GROUP).transpose(0, 2, 1)).astype(BF16)
    bre_bd, bim_bd = to_bd(bbr), to_bd(bbi)
    cret_bd = _block_diag(ssm_c_re[0].transpose(0, 2, 1)).astype(BF16)
    cimt_bd = _block_diag(ssm_c_im[0].transpose(0, 2, 1)).astype(BF16)
    glu_bd = _block_diag(ssm_glu_w[0]).astype(BF16)
    s5_consts = (abr.reshape(1, -1), abi.reshape(1, -1), bre_bd, bim_bd, cret_bd, cimt_bd,
                 ssm_d[0].reshape(1, D_SSM), glu_bd, ssm_glu_b[0].reshape(1, D_SSM))
    pool_sc = pool_scale[0].reshape(POOL_GROUPS, 1, POOL_DIM)

    u, v = _in_proj(head, xs, g1 + gather_token[:1, :1], w_in_all)
    u_p = _permute_rows(u)
    ys_p = _s5_forward(u_p, *s5_consts)
    landed = _split_wait(gather, gather_make, ys_p, "gather_wait")[3:]
    forward_make = _forward_copies(3)
    forward, forward_token = _split_start(list(landed), forward_make, 3 * len(CHIP_PEERS), "gather_forward_start")
    ys = _unpermute_rows(ys_p)
    yp = _pool_forward(v, pool_w[0], pool_sc + forward_token[:1, :1], first_row)
    w_out_all, wd_all, c_all = _split_wait(forward, forward_make, yp, "gather_forward_wait")
    w_out_all = w_out_all.reshape(D_MODEL, D_MODEL)
    h1 = _out_proj(head, xs, ys, yp, gs, gp, w_out_all)
    ab, n2, h2 = _ffn_forward(h1, g2, c_all, wd_all)

    dh1, dab, dh2, loss_part, d_gf, d_g2 = _ffn_backward(h2, tgt, h1, ab, gf, g2, c_all, wd_all)
    d_c, d_wd = _ffn_wgrad(n2, dh2, ab, dab)
    ffn_make = _push_copies(ALL_PEERS, [True, True])
    ffn_x, ffn_token = _split_start([d_c, d_wd, _landing(d_c, True), _landing(d_wd, True)], ffn_make,
                                    2 * len(ALL_PEERS), "ffn_grad_start")
    dys, dyp, d_gs, d_gp, d_wout = _out_proj_backward(dh1, ys, yp, gs + ffn_token[:1, :1], gp, w_out_all)
    dv, d_pool_w, d_pool_sc = _pool_backward(v, dyp, pool_w[0], pool_sc, first_row)
    (du_p, d_ar, d_ai, d_bre_bd, d_bim_bd, d_cre_bd, d_cim_bd, d_d, d_glu_bd, d_glub) = _s5_backward(
        u_p, _permute_rows(dys), *s5_consts)
    du = _unpermute_rows(du_p)

    from_bd = lambda m: _block_diag_extract(m, SSM_GROUP, SSM_STATE).transpose(0, 2, 1)
    d_bbr = from_bd(d_bre_bd).reshape(SSM_GROUPS * SSM_STATE, SSM_GROUP)
    d_bbi = from_bd(d_bim_bd).reshape(SSM_GROUPS * SSM_STATE, SSM_GROUP)
    d_zr, d_zi, d_b_re, d_b_im = _s5_disc_b_bwd(zr_col, zi_col, b_re, b_im, d_bbr, d_bbi)
    d_lam_re, d_lam_im, d_log_step = _s5_disc_a_bwd(
        lam_re, lam_im, log_step,
        (d_ar.reshape(SSM_GROUPS, SSM_STATE), d_ai.reshape(SSM_GROUPS, SSM_STATE),
         d_zr.reshape(SSM_GROUPS, SSM_STATE), d_zi.reshape(SSM_GROUPS, SSM_STATE)))
    small_grads = {
        "ssm_lambda_re": d_lam_re, "ssm_lambda_im": d_lam_im, "ssm_log_step": d_log_step,
        "ssm_b_re": d_b_re, "ssm_b_im": d_b_im,
        "ssm_c_re": _block_diag_extract(d_cre_bd, SSM_GROUP, SSM_STATE),
        "ssm_c_im": _block_diag_extract(d_cim_bd, SSM_GROUP, SSM_STATE),
        "ssm_d": d_d, "ssm_glu_w": _block_diag_extract(d_glu_bd, SSM_GROUP, SSM_GROUP), "ssm_glu_b": d_glub,
        "ssm_norm_g": d_gs, "pool_w": d_pool_w, "pool_scale": d_pool_sc, "pool_norm_g": d_gp,
        "norm2_g": d_g2, "final_norm_g": d_gf,
    }

    early_names = SMALL_NAMES[1:-1]
    early_pack = _pack([small_grads[n] for n in early_names], BF16)
    d_wout = d_wout.reshape(N_DEV, shard_rows, D_MODEL)
    early_make = _push_copies(ALL_PEERS, [True, False])
    early_x, early_token = _split_start([d_wout, early_pack, _landing(d_wout, True), _landing(early_pack, False)],
                                        early_make, 2 * len(ALL_PEERS), "early_grad_start")
    d_x, d_head, d_g1, d_win = _in_proj_backward(head, xs, du, dv, dh1, g1 + early_token[:1, :1], w_in_all)
    r_c, r_wd = _split_wait(ffn_x, ffn_make, d_g1, "ffn_grad_wait")[2:]
    r_wout, r_early = _split_wait(early_x, early_make, d_g1, "early_grad_wait")[2:]
    late_pack = _pack([d_g1, d_gf, d_head[first_row:], loss_part], F32)
    (r_win, r_late), _ = _exchange([d_win.reshape(N_DEV, shard_rows, D_MODEL), late_pack], [True, False],
                                   "late_grad_exchange")

    results = {}
    results["w_in"] = _adamw(r_win, w_in[0], m_w_in[0], v_w_in[0], shard_rows, "adamw_w_in")
    results["w_out"] = _adamw(r_wout, w_out[0], m_w_out[0], v_w_out[0], shard_rows, "adamw_w_out")
    res_wd = _adamw(r_wd, _pad_rows(w_down[0]), _pad_rows(m_w_down[0]), _pad_rows(v_w_down[0]), 128, "adamw_w_down")
    results["w_down"] = [r[:FF_SHARD] for r in res_wd]
    res_c = _adamw(r_c, _gate_up(w_gate[0], w_up[0]), _gate_up(m_w_gate[0], m_w_up[0]),
                   _gate_up(v_w_gate[0], v_w_up[0]), 128, "adamw_gate_up")
    results["w_gate"] = [r[:, :FF_SHARD] for r in res_c]
    results["w_up"] = [r[:, FF_PAD:FF_PAD + FF_SHARD] for r in res_c]

    sum_early, sum_late = _reduce_slots([r_early, r_late], "small_grad_sums")
    g_small = _unpack(sum_early, [weights[n].shape for n in early_names])
    g_norm1, g_final, g_meta_all, loss_row = _unpack(
        sum_late, [norm1_g.shape, final_norm_g.shape, (n_meta, D_MODEL), (1, PACK_LANES)])
    g_small = [g_norm1] + g_small + [g_final]
    small_2d = lambda prefix: [_as2d(given[prefix + n]) for n in SMALL_NAMES]
    res_small = _adamw_many([_as2d(g) for g in g_small], small_2d(""), small_2d("m_"), small_2d("v_"), "adamw_small")
    for idx, n in enumerate(SMALL_NAMES):
        results[n] = [g_small[idx]] + [part[idx] for part in res_small]
    shard_cols = meta_tokens.shape[1]
    g_meta = lax.dynamic_slice_in_dim(g_meta_all, me * shard_cols, shard_cols, axis=1)
    results["meta_tokens"] = _adamw(g_meta[None], meta_tokens, m_meta_tokens, v_meta_tokens, n_meta, "adamw_meta")

    out = [loss_row[0, 0], d_x[None]]
    for part in range(4):
        for n in WEIGHT_NAMES:
            out.append(results[n][part].reshape(weights[n].shape))
    return tuple(out)
```

```python
import functools

import jax
import jax.numpy as jnp
from jax import lax
from jax.experimental import pallas as pl
from jax.experimental.pallas import tpu as pltpu

F32 = jnp.float32
BF16 = jnp.bfloat16

N_DEV = 8
D_MODEL = 1024
D_SSM = 512
SSM_GROUP = 16
SSM_STATE = 64
SSM_GROUPS = 32
POOL_GROUPS = 4
POOL_DIM = 128
COL_U = 128
COL_S = 512
N_COL = D_SSM // COL_U
GROUPS_PER_COL = COL_U // SSM_GROUP
FF_SHARD = 352
FF_PAD = 384
TM = 256
WGRAD_STEPS = 2
HEAD = TM
SUBLANES = 8
SCAN_UNROLL = 4
POOL_HALO = 128
EPS = 1e-6
STEP_FLOOR = -1e-4
VMEM_LIMIT = 60 * 1024 * 1024

ADAM_LR = 0.001
ADAM_B1 = 0.9
ADAM_B2 = 0.999
ADAM_EPS = 1e-08
ADAM_WD = 0.01
ADAM_STEP = 10

MESH_ID = pl.DeviceIdType.MESH
ANY = pl.BlockSpec(memory_space=pl.ANY)

SMALL_NAMES = ("norm1_g", "ssm_lambda_re", "ssm_lambda_im", "ssm_log_step", "ssm_b_re", "ssm_b_im",
               "ssm_c_re", "ssm_c_im", "ssm_d", "ssm_glu_w", "ssm_glu_b", "ssm_norm_g", "pool_w",
               "pool_scale", "pool_norm_g", "norm2_g", "final_norm_g")
WEIGHT_NAMES = ("meta_tokens", "norm1_g", "w_in", "ssm_lambda_re", "ssm_lambda_im", "ssm_log_step",
                "ssm_b_re", "ssm_b_im", "ssm_c_re", "ssm_c_im", "ssm_d", "ssm_glu_w", "ssm_glu_b",
                "ssm_norm_g", "pool_w", "pool_scale", "pool_norm_g", "w_out", "norm2_g", "w_gate",
                "w_up", "w_down", "final_norm_g")
PACK_LANES = 128
PACK_UNIT = 8 * PACK_LANES


def _dot(a, b):
    return jnp.dot(a.astype(BF16), b.astype(BF16), preferred_element_type=F32)


def _dot_nt(a, b):
    return lax.dot_general(a.astype(BF16), b.astype(BF16), (((1,), (1,)), ((), ())), preferred_element_type=F32)


def _dot_tn(a, b):
    return lax.dot_general(a.astype(BF16), b.astype(BF16), (((0,), (0,)), ((), ())), preferred_element_type=F32)


def _sigmoid(x):
    return 1.0 / (1.0 + jnp.exp(-x))


def _rstd(x):
    return lax.rsqrt(jnp.mean(x * x, axis=-1, keepdims=True) + EPS)


def _rms_bwd(dy, xhat, r, g):
    dxh = dy * g
    dx = r * (dxh - xhat * jnp.mean(dxh * xhat, axis=-1, keepdims=True))
    return dx, jnp.sum(dy * xhat, axis=0, keepdims=True)


def _params(sem, vmem=None):
    return pltpu.CompilerParams(dimension_semantics=sem, vmem_limit_bytes=vmem)


def _const(shape):
    return pl.BlockSpec(shape, lambda *_: (0,) * len(shape))


def _xrow(i):
    return (jnp.maximum(i - 1, 0), 0)


HBM = pl.BlockSpec(memory_space=pltpu.HBM)
SEM = pl.BlockSpec(memory_space=pltpu.SEMAPHORE)
EFFECT = pltpu.SideEffectType.DATAFLOW_SIDE_EFFECTING
ALL_PEERS = tuple(range(1, N_DEV))
SIBLING = 1
CHIP_PEERS = (2, 4, 6)


def _me():
    return 4 * lax.axis_index("x") + 2 * lax.axis_index("y") + lax.axis_index("c")


def _peer(k):
    x, y, c = lax.axis_index("x"), lax.axis_index("y"), lax.axis_index("c")
    px = 1 - x if k & 4 else x
    py = 1 - y if k & 2 else y
    pc = 1 - c if k & 1 else c
    return (px, py, pc), 4 * px + 2 * py + pc


def _landing(arr, scatter):
    if scatter:
        own = lax.dynamic_index_in_dim(arr, _me(), 0, keepdims=False)
    else:
        own = arr
    return lax.dynamic_update_index_in_dim(lax.empty((N_DEV,) + own.shape, arr.dtype), own, _me(), 0)


def _push_copies(peers, scatter):
    n_arr = len(scatter)

    def make(refs, send_sems, recv_sems):
        copies = []
        for a in range(n_arr):
            for i, k in enumerate(peers):
                peer_id, peer = _peer(k)
                sem = a * len(peers) + i
                copies.append(pltpu.make_async_remote_copy(
                    src_ref=refs[a].at[peer] if scatter[a] else refs[a], dst_ref=refs[n_arr + a].at[_me()],
                    send_sem=send_sems.at[sem], recv_sem=recv_sems.at[sem],
                    device_id=peer_id, device_id_type=MESH_ID))
        return copies
    return make


def _forward_copies(n_arr):
    def make(refs, send_sems, recv_sems):
        copies = []
        sibling_id, _ = _peer(SIBLING)
        for a in range(n_arr):
            for i, k in enumerate(CHIP_PEERS):
                slot = refs[a].at[_peer(k)[1]]
                sem = a * len(CHIP_PEERS) + i
                copies.append(pltpu.make_async_remote_copy(
                    src_ref=slot, dst_ref=slot, send_sem=send_sems.at[sem], recv_sem=recv_sems.at[sem],
                    device_id=sibling_id, device_id_type=MESH_ID))
        return copies
    return make


def _split_start(operands, make, n_sem, name):
    n_op = len(operands)

    def body(*refs):
        for cp in make(refs[:n_op], refs[n_op], refs[n_op + 1]):
            cp.start()
        refs[-1][...] = jnp.zeros(refs[-1].shape, F32)

    out = pl.pallas_call(
        body, name=name,
        out_shape=(pltpu.SemaphoreType.DMA((n_sem,)), pltpu.SemaphoreType.DMA((n_sem,)),
                   *[pltpu.HBM(t.shape, t.dtype) for t in operands], jax.ShapeDtypeStruct((8, PACK_LANES), F32)),
        in_specs=[HBM] * n_op, out_specs=(SEM, SEM, *[HBM] * n_op, pl.BlockSpec(memory_space=pltpu.VMEM)),
        input_output_aliases={i: 2 + i for i in range(n_op)},
        compiler_params=pltpu.CompilerParams(has_side_effects=EFFECT),
    )(*[pltpu.with_memory_space_constraint(t, pltpu.HBM) for t in operands])
    return out[:-1], out[-1]


def _split_wait(started, make, after, name):
    send_sems, recv_sems, thru = started[0], started[1], started[2:]
    n_op = len(thru)

    def body(*refs):
        for cp in make(refs[:n_op], refs[n_op], refs[n_op + 1]):
            cp.wait_send()
            cp.wait_recv()
        refs[-1][...] = jnp.zeros(refs[-1].shape, F32)

    out = pl.pallas_call(
        body, name=name,
        out_shape=(*[pltpu.HBM(t.shape, t.dtype) for t in thru], jax.ShapeDtypeStruct((8, PACK_LANES), F32)),
        in_specs=[HBM] * n_op + [SEM, SEM, ANY], out_specs=(*[HBM] * n_op, pl.BlockSpec(memory_space=pltpu.VMEM)),
        input_output_aliases={i: i for i in range(n_op)},
        compiler_params=pltpu.CompilerParams(has_side_effects=EFFECT),
    )(*thru, send_sems, recv_sems, after)
    return out[:-1], out[-1]


def _adamw_math(g, w, m, v):
    nm = ADAM_B1 * m + (1.0 - ADAM_B1) * g
    nv = ADAM_B2 * v + (1.0 - ADAM_B2) * (g * g)
    m_hat = nm / (1.0 - ADAM_B1 ** ADAM_STEP)
    v_hat = nv / (1.0 - ADAM_B2 ** ADAM_STEP)
    return -ADAM_LR * (m_hat / (jnp.sqrt(v_hat) + ADAM_EPS) + ADAM_WD * w), nm, nv


def _sum_slots(s_ref):
    g = s_ref[0].astype(F32)
    for s in range(1, s_ref.shape[0]):
        g = g + s_ref[s].astype(F32)
    return g


def _adamw(slots, w, m, v, tile_rows, name):
    n, rows, cols = slots.shape

    def body(s_ref, w_ref, m_ref, v_ref, g_ref, d_ref, nm_ref, nv_ref):
        g = _sum_slots(s_ref)
        g_ref[...] = g
        d_ref[...], nm_ref[...], nv_ref[...] = _adamw_math(g, w_ref[...], m_ref[...], v_ref[...])

    tile = pl.BlockSpec((tile_rows, cols), lambda i: (i, 0))
    return pl.pallas_call(
        body, name=name, grid=(rows // tile_rows,),
        in_specs=[pl.BlockSpec((n, tile_rows, cols), lambda i: (0, i, 0)), tile, tile, tile],
        out_specs=[tile] * 4, out_shape=[jax.ShapeDtypeStruct((rows, cols), F32)] * 4,
        compiler_params=_params(("parallel",), VMEM_LIMIT),
    )(slots, w, m, v)


def _reduce_slots(slot_arrays, name):
    def body(*refs):
        n_arr = len(refs) // 2
        for s_ref, o_ref in zip(refs[:n_arr], refs[n_arr:]):
            o_ref[...] = _sum_slots(s_ref)
    return pl.pallas_call(
        body, name=name, out_shape=[jax.ShapeDtypeStruct(s.shape[1:], F32) for s in slot_arrays],
        compiler_params=_params(None, VMEM_LIMIT))(*slot_arrays)


def _adamw_many(grads, ws, ms, vs, name):
    n = len(grads)

    def body(*refs):
        ins, outs = refs[:4 * n], refs[4 * n:]
        for i in range(n):
            g, w, m, v = (ins[j * n + i][...] for j in range(4))
            outs[i][...], outs[n + i][...], outs[2 * n + i][...] = _adamw_math(g, w, m, v)

    out = pl.pallas_call(
        body, name=name, out_shape=[jax.ShapeDtypeStruct(w.shape, F32) for w in ws] * 3,
        compiler_params=_params(None, VMEM_LIMIT))(*grads, *ws, *ms, *vs)
    return out[:n], out[n:2 * n], out[2 * n:]


def _disc_a(lam_re, lam_im, log_step):
    lr = jnp.minimum(lam_re, STEP_FLOOR)
    step = jnp.exp(log_step)
    mag = jnp.exp(lr * step)
    ang = lam_im * step
    abr = mag * jnp.cos(ang)
    abi = mag * jnp.sin(ang)
    nr = abr - 1.0
    den = lr * lr + lam_im * lam_im
    cr = (nr * lr + abi * lam_im) / den
    ci = (abi * lr - nr * lam_im) / den
    return abr, abi, cr, ci


def _disc_b(cr, ci, b_re, b_im):
    return cr * b_re - ci * b_im, cr * b_im + ci * b_re


def _s5_disc_a(lam_re, lam_im, log_step):
    def body(lr_ref, li_ref, ls_ref, *outs):
        for o, val in zip(outs, _disc_a(lr_ref[...], li_ref[...], ls_ref[...])):
            o[...] = val
    return pl.pallas_call(body, name="s5_disc_a", out_shape=[jax.ShapeDtypeStruct(lam_re.shape, F32)] * 4)(
        lam_re, lam_im, log_step)


def _s5_disc_a_bwd(lam_re, lam_im, log_step, cts):
    def body(lr_ref, li_ref, ls_ref, c0, c1, c2, c3, dlr_ref, dli_ref, dls_ref):
        _, vjp = jax.vjp(_disc_a, lr_ref[...], li_ref[...], ls_ref[...])
        dlr, dli, dls = vjp((c0[...], c1[...], c2[...], c3[...]))
        dlr_ref[...] = dlr
        dli_ref[...] = dli
        dls_ref[...] = dls
    return pl.pallas_call(
        body, name="s5_disc_a_bwd",
        out_shape=[jax.ShapeDtypeStruct(lam_re.shape, F32), jax.ShapeDtypeStruct(lam_re.shape, F32),
                   jax.ShapeDtypeStruct(log_step.shape, F32)])(lam_re, lam_im, log_step, *cts)


def _s5_disc_b(cr, ci, b_re, b_im):
    def body(cr_ref, ci_ref, br_ref, bi_ref, o_re, o_im):
        o_re[...], o_im[...] = _disc_b(cr_ref[...], ci_ref[...], br_ref[...], bi_ref[...])
    return pl.pallas_call(body, name="s5_disc_b", out_shape=[jax.ShapeDtypeStruct(b_re.shape, F32)] * 2)(
        cr, ci, b_re, b_im)


def _s5_disc_b_bwd(cr, ci, b_re, b_im, d_re, d_im):
    def body(cr_ref, ci_ref, br_ref, bi_ref, dr_ref, di_ref, dcr_ref, dci_ref, dbr_ref, dbi_ref):
        _, vjp = jax.vjp(_disc_b, cr_ref[...], ci_ref[...], br_ref[...], bi_ref[...])
        dcr_ref[...], dci_ref[...], dbr_ref[...], dbi_ref[...] = vjp((dr_ref[...], di_ref[...]))
    return pl.pallas_call(
        body, name="s5_disc_b_bwd",
        out_shape=[jax.ShapeDtypeStruct(cr.shape, F32)] * 2 + [jax.ShapeDtypeStruct(b_re.shape, F32)] * 2)(
            cr, ci, b_re, b_im, d_re, d_im)


def _cmul(ar, ai, br, bi):
    return ar * br - ai * bi, ar * bi + ai * br


def _cpow(ar, ai, n):
    rr, ri = jnp.ones_like(ar), jnp.zeros_like(ai)
    while n:
        if n & 1:
            rr, ri = _cmul(rr, ri, ar, ai)
        n >>= 1
        if n:
            ar, ai = _cmul(ar, ai, ar, ai)
    return rr, ri


def _tile_rows(i):
    return pl.ds(pl.multiple_of(i * SUBLANES, SUBLANES), SUBLANES)


def _segment_scan(z_re, z_im, ar, ai, lseg, reverse, visit=None):
    shape = (SUBLANES, z_re.shape[1])
    arb = jnp.broadcast_to(ar, shape)
    aib = jnp.broadcast_to(ai, shape)
    zero = jnp.zeros(shape, F32)
    row = lax.broadcasted_iota(jnp.int32, shape, 0)

    def tile_of(k):
        return lseg - 1 - k if reverse else k

    def advance(k, sr, si):
        rows = _tile_rows(tile_of(k))
        nr, ni = _cmul(arb, aib, sr, si)
        return rows, nr + z_re[rows, :], ni + z_im[rows, :]

    def first_pass(k, carry):
        _, nr, ni = advance(k, *carry)
        return nr, ni

    fr, fi = lax.fori_loop(0, lseg, first_pass, (zero, zero), unroll=SCAN_UNROLL)
    pr, pi = _cpow(arb, aib, lseg)
    cr, ci = zero, zero
    for _ in range(SUBLANES - 1):
        tr, ti = _cmul(pr, pi, cr, ci)
        tr, ti = tr + fr, ti + fi
        if reverse:
            cr = jnp.where(row == SUBLANES - 1, 0.0, pltpu.roll(tr, SUBLANES - 1, 0))
            ci = jnp.where(row == SUBLANES - 1, 0.0, pltpu.roll(ti, SUBLANES - 1, 0))
        else:
            cr = jnp.where(row == 0, 0.0, pltpu.roll(tr, 1, 0))
            ci = jnp.where(row == 0, 0.0, pltpu.roll(ti, 1, 0))

    def second_pass(k, carry):
        sr, si, acc = carry
        rows, nr, ni = advance(k, sr, si)
        z_re[rows, :] = nr
        z_im[rows, :] = ni
        if visit is not None:
            acc = visit(tile_of(k), nr, ni, acc)
        return nr, ni, acc

    acc0 = (zero, zero) if visit is not None else 0
    return lax.fori_loop(0, lseg, second_pass, (cr, ci, acc0), unroll=SCAN_UNROLL)[2]


def _gelu(y):
    c = 0.7978845608028654
    return 0.5 * y * (1.0 + jnp.tanh(c * (y + 0.044715 * y * y * y)))


def _gelu_grad(y):
    c = 0.7978845608028654
    th = jnp.tanh(c * (y + 0.044715 * y * y * y))
    return 0.5 * (1.0 + th) + 0.5 * y * (1.0 - th * th) * c * (1.0 + 3.0 * 0.044715 * y * y)


def _s5_specs(lp):
    col_u = pl.BlockSpec((lp, COL_U), lambda j: (0, j))
    row_u = pl.BlockSpec((1, COL_U), lambda j: (0, j))
    row_s = pl.BlockSpec((1, COL_S), lambda j: (0, j))
    b_mat = pl.BlockSpec((None, COL_U, COL_S), lambda j: (j, 0, 0))
    c_mat = pl.BlockSpec((None, COL_S, COL_U), lambda j: (j, 0, 0))
    g_mat = pl.BlockSpec((None, COL_U, COL_U), lambda j: (j, 0, 0))
    return col_u, row_u, row_s, b_mat, c_mat, g_mat


def _s5_fill_states(u_ref, bre_ref, bim_ref, ar_ref, ai_ref, s_re, s_im, lseg, n_chunks, chunk):
    def fill(cidx, carry):
        rows = pl.ds(pl.multiple_of(cidx * chunk, SUBLANES), chunk)
        ub = u_ref[rows, :].astype(BF16)
        s_re[rows, :] = jnp.dot(ub, bre_ref[...], preferred_element_type=F32)
        s_im[rows, :] = jnp.dot(ub, bim_ref[...], preferred_element_type=F32)
        return carry
    lax.fori_loop(0, n_chunks, fill, 0)
    _segment_scan(s_re, s_im, ar_ref[...], ai_ref[...], lseg, reverse=False)


def _s5_forward(u_p, ar, ai, bre_bd, bim_bd, cret_bd, cimt_bd, d_row, glu_bd, glub_row):
    lp = u_p.shape[0]
    lseg = lp // SUBLANES
    chunk, n_chunks = lseg, SUBLANES

    def body(u_ref, ar_ref, ai_ref, bre_ref, bim_ref, cret_ref, cimt_ref, d_ref, glu_ref, glub_ref,
             ys_ref, s_re, s_im):
        _s5_fill_states(u_ref, bre_ref, bim_ref, ar_ref, ai_ref, s_re, s_im, lseg, n_chunks, chunk)

        def emit(cidx, carry):
            rows = pl.ds(pl.multiple_of(cidx * chunk, SUBLANES), chunk)
            y = (_dot(s_re[rows, :], cret_ref[...]) - _dot(s_im[rows, :], cimt_ref[...])
                 + d_ref[...] * u_ref[rows, :])
            g = _gelu(y)
            gate = _dot(g, glu_ref[...]) + glub_ref[...]
            ys_ref[rows, :] = g * _sigmoid(gate)
            return carry
        lax.fori_loop(0, n_chunks, emit, 0)

    col_u, row_u, row_s, b_mat, c_mat, g_mat = _s5_specs(lp)
    return pl.pallas_call(
        body, name="s5_forward", grid=(N_COL,),
        in_specs=[col_u, row_s, row_s, b_mat, b_mat, c_mat, c_mat, row_u, g_mat, row_u],
        out_specs=col_u, out_shape=jax.ShapeDtypeStruct((lp, D_SSM), F32),
        scratch_shapes=[pltpu.VMEM((lp, COL_S), F32), pltpu.VMEM((lp, COL_S), F32)],
        compiler_params=_params(("arbitrary",), VMEM_LIMIT),
    )(u_p, ar, ai, bre_bd, bim_bd, cret_bd, cimt_bd, d_row, glu_bd, glub_row)


def _s5_backward(u_p, dys_p, ar, ai, bre_bd, bim_bd, cret_bd, cimt_bd, d_row, glu_bd, glub_row):
    lp = u_p.shape[0]
    lseg = lp // SUBLANES
    chunk, n_chunks = lseg, SUBLANES

    def body(u_ref, dys_ref, ar_ref, ai_ref, bre_ref, bim_ref, cret_ref, cimt_ref, d_ref, glu_ref, glub_ref,
             du_ref, dar_ref, dai_ref, dbre_ref, dbim_ref, dcre_ref, dcim_ref, dd_ref, dglu_ref, dglub_ref,
             s_re, s_im, q_re, q_im):
        _s5_fill_states(u_ref, bre_ref, bim_ref, ar_ref, ai_ref, s_re, s_im, lseg, n_chunks, chunk)
        for ref in (dcre_ref, dcim_ref, dd_ref, dglu_ref, dglub_ref, dbre_ref, dbim_ref):
            ref[...] = jnp.zeros(ref.shape, F32)

        def mixer_bwd(cidx, carry):
            rows = pl.ds(pl.multiple_of(cidx * chunk, SUBLANES), chunk)
            u = u_ref[rows, :]
            sr, si = s_re[rows, :], s_im[rows, :]
            y = _dot(sr, cret_ref[...]) - _dot(si, cimt_ref[...]) + d_ref[...] * u
            g = _gelu(y)
            sg = _sigmoid(_dot(g, glu_ref[...]) + glub_ref[...])
            dout = dys_ref[rows, :]
            dgate = dout * g * sg * (1.0 - sg)
            dy = (dout * sg + _dot_nt(dgate, glu_ref[...])) * _gelu_grad(y)
            dglu_ref[...] += _dot_tn(g, dgate)
            dglub_ref[...] += jnp.sum(dgate, axis=0, keepdims=True)
            dd_ref[...] += jnp.sum(dy * u, axis=0, keepdims=True)
            dcre_ref[...] += _dot_tn(dy, sr)
            dcim_ref[...] -= _dot_tn(dy, si)
            q_re[rows, :] = _dot_nt(dy, cret_ref[...])
            q_im[rows, :] = -_dot_nt(dy, cimt_ref[...])
            du_ref[rows, :] = d_ref[...] * dy
            return carry
        lax.fori_loop(0, n_chunks, mixer_bwd, 0)

        row = lax.broadcasted_iota(jnp.int32, (SUBLANES, COL_S), 0)

        def visit(i, qr, qi, acc):
            prev = _tile_rows(jnp.where(i == 0, lseg - 1, i - 1))
            pr, pi = s_re[prev, :], s_im[prev, :]
            first = i == 0
            pr = jnp.where(first, jnp.where(row == 0, 0.0, pltpu.roll(pr, 1, 0)), pr)
            pi = jnp.where(first, jnp.where(row == 0, 0.0, pltpu.roll(pi, 1, 0)), pi)
            return acc[0] + qr * pr + qi * pi, acc[1] + qi * pr - qr * pi

        dar, dai = _segment_scan(q_re, q_im, ar_ref[...], -ai_ref[...], lseg, reverse=True, visit=visit)
        dar_ref[...] = jnp.sum(dar, axis=0, keepdims=True)
        dai_ref[...] = jnp.sum(dai, axis=0, keepdims=True)

        def input_bwd(cidx, carry):
            rows = pl.ds(pl.multiple_of(cidx * chunk, SUBLANES), chunk)
            qr, qi = q_re[rows, :], q_im[rows, :]
            u = u_ref[rows, :]
            du_ref[rows, :] += _dot_nt(qr, bre_ref[...]) + _dot_nt(qi, bim_ref[...])
            dbre_ref[...] += _dot_tn(u, qr)
            dbim_ref[...] += _dot_tn(u, qi)
            return carry
        lax.fori_loop(0, n_chunks, input_bwd, 0)

    col_u, row_u, row_s, b_mat, c_mat, g_mat = _s5_specs(lp)
    return pl.pallas_call(
        body, name="s5_backward", grid=(N_COL,),
        in_specs=[col_u, col_u, row_s, row_s, b_mat, b_mat, c_mat, c_mat, row_u, g_mat, row_u],
        out_specs=[col_u, row_s, row_s, b_mat, b_mat, b_mat, b_mat, row_u, g_mat, row_u],
        out_shape=[jax.ShapeDtypeStruct((lp, D_SSM), F32),
                   jax.ShapeDtypeStruct((1, N_COL * COL_S), F32), jax.ShapeDtypeStruct((1, N_COL * COL_S), F32),
                   jax.ShapeDtypeStruct((N_COL, COL_U, COL_S), F32), jax.ShapeDtypeStruct((N_COL, COL_U, COL_S), F32),
                   jax.ShapeDtypeStruct((N_COL, COL_U, COL_S), F32), jax.ShapeDtypeStruct((N_COL, COL_U, COL_S), F32),
                   jax.ShapeDtypeStruct((1, D_SSM), F32),
                   jax.ShapeDtypeStruct((N_COL, COL_U, COL_U), F32), jax.ShapeDtypeStruct((1, D_SSM), F32)],
        scratch_shapes=[pltpu.VMEM((lp, COL_S), F32)] * 4,
        compiler_params=_params(("arbitrary",), VMEM_LIMIT),
    )(u_p, dys_p, ar, ai, bre_bd, bim_bd, cret_bd, cimt_bd, d_row, glu_bd, glub_row)


def _band_apply(band, x):
    hi = x.astype(BF16)
    lo = (x - hi.astype(F32)).astype(BF16)
    dot = functools.partial(jnp.dot, preferred_element_type=F32)
    return dot(band, hi) + dot(band, lo)


def _pool_band(window, transposed):
    t = lax.broadcasted_iota(jnp.int32, (TM, TM + POOL_HALO), 0)
    c = lax.broadcasted_iota(jnp.int32, (TM, TM + POOL_HALO), 1)
    lag = c - t if transposed else t + POOL_HALO - c
    return jnp.where((lag >= 0) & (lag < window), 1.0, 0.0).astype(BF16)


def _pool_inv_count(tile, window, first_row):
    t = tile * TM + lax.broadcasted_iota(jnp.int32, (TM, 1), 0) - first_row
    return 1.0 / jnp.clip(t + 1, 1, window).astype(F32)


def _pool_specs(lp):
    col = pl.BlockSpec((lp, POOL_DIM), lambda k: (0, k))
    mat = pl.BlockSpec((None, POOL_DIM, POOL_DIM), lambda k: (k, 0, 0))
    row = pl.BlockSpec((None, 1, POOL_DIM), lambda k: (k, 0, 0))
    return col, mat, row


def _pool_forward(v, pool_w, pool_scale, first_row):
    lp = v.shape[0]
    n_tiles = lp // TM

    def body(v_ref, w_ref, sc_ref, yp_ref, vpad):
        window = jnp.left_shift(2, pl.program_id(0))
        vpad[pl.ds(0, POOL_HALO), :] = jnp.zeros((POOL_HALO, POOL_DIM), F32)
        vpad[pl.ds(POOL_HALO, lp), :] = v_ref[...]
        band = _pool_band(window, transposed=False)

        def tile(j, carry):
            start = pl.multiple_of(j * TM, TM)
            ext = vpad[pl.ds(start, TM + POOL_HALO), :]
            p = _band_apply(band, ext) * _pool_inv_count(j, window, first_row) - ext[POOL_HALO:, :]
            yp_ref[pl.ds(start, TM), :] = _dot(p, w_ref[...]) * sc_ref[...]
            return carry
        lax.fori_loop(0, n_tiles, tile, 0)

    col, mat, row = _pool_specs(lp)
    return pl.pallas_call(
        body, name="pool_forward", grid=(POOL_GROUPS,),
        in_specs=[col, mat, row], out_specs=col, out_shape=jax.ShapeDtypeStruct((lp, D_SSM), F32),
        scratch_shapes=[pltpu.VMEM((lp + POOL_HALO, POOL_DIM), F32)],
        compiler_params=_params(("arbitrary",), VMEM_LIMIT),
    )(v, pool_w, pool_scale)


def _pool_backward(v, dyp, pool_w, pool_scale, first_row):
    lp = v.shape[0]
    n_tiles = lp // TM

    def body(v_ref, dyp_ref, w_ref, sc_ref, dv_ref, dw_ref, dsc_ref, vpad, gpad):
        window = jnp.left_shift(2, pl.program_id(0))
        vpad[pl.ds(0, POOL_HALO), :] = jnp.zeros((POOL_HALO, POOL_DIM), F32)
        vpad[pl.ds(POOL_HALO, lp), :] = v_ref[...]
        gpad[pl.ds(lp, POOL_HALO), :] = jnp.zeros((POOL_HALO, POOL_DIM), F32)
        dw_ref[...] = jnp.zeros(dw_ref.shape, F32)
        dsc_ref[...] = jnp.zeros(dsc_ref.shape, F32)
        band = _pool_band(window, transposed=False)

        def linear_bwd(j, carry):
            start = pl.multiple_of(j * TM, TM)
            ext = vpad[pl.ds(start, TM + POOL_HALO), :]
            inv = _pool_inv_count(j, window, first_row)
            p = _band_apply(band, ext) * inv - ext[POOL_HALO:, :]
            z = _dot(p, w_ref[...])
            dyp_t = dyp_ref[pl.ds(start, TM), :]
            dz = dyp_t * sc_ref[...]
            dsc_ref[...] += jnp.sum(dyp_t * z, axis=0, keepdims=True)
            dw_ref[...] += _dot_tn(p, dz)
            dp = _dot_nt(dz, w_ref[...])
            gpad[pl.ds(start, TM), :] = dp * inv
            dv_ref[pl.ds(start, TM), :] = -dp
            return carry
        lax.fori_loop(0, n_tiles, linear_bwd, 0)
        band_t = _pool_band(window, transposed=True)

        def window_bwd(j, carry):
            start = pl.multiple_of(j * TM, TM)
            dv_ref[pl.ds(start, TM), :] += _band_apply(band_t, gpad[pl.ds(start, TM + POOL_HALO), :])
            return carry
        lax.fori_loop(0, n_tiles, window_bwd, 0)

    col, mat, row = _pool_specs(lp)
    return pl.pallas_call(
        body, name="pool_backward", grid=(POOL_GROUPS,),
        in_specs=[col, col, mat, row], out_specs=[col, mat, row],
        out_shape=[jax.ShapeDtypeStruct((lp, D_SSM), F32),
                   jax.ShapeDtypeStruct((POOL_GROUPS, POOL_DIM, POOL_DIM), F32),
                   jax.ShapeDtypeStruct((POOL_GROUPS, 1, POOL_DIM), F32)],
        scratch_shapes=[pltpu.VMEM((lp + POOL_HALO, POOL_DIM), F32)] * 2,
        compiler_params=_params(("arbitrary",), VMEM_LIMIT),
    )(v, dyp, pool_w, pool_scale)


def _row_specs():
    head = _const((HEAD, D_MODEL))
    xrow = pl.BlockSpec((TM, D_MODEL), _xrow)
    full = pl.BlockSpec((TM, D_MODEL), lambda i: (i, 0))
    half = pl.BlockSpec((TM, D_SSM), lambda i: (i, 0))
    return head, xrow, full, half


def _in_proj(head, x, g1, w_in):
    n_tiles = (HEAD + x.shape[0]) // TM
    lp = n_tiles * TM

    def body(head_ref, x_ref, g_ref, w_ref, u_ref, v_ref):
        h0 = jnp.where(pl.program_id(0) == 0, head_ref[...], x_ref[...])
        proj = _dot(h0 * _rstd(h0) * g_ref[...], w_ref[...])
        u_ref[...] = proj[:, :D_SSM]
        v_ref[...] = proj[:, D_SSM:]

    head_s, xrow, _, half = _row_specs()
    return pl.pallas_call(
        body, name="in_proj", grid=(n_tiles,),
        in_specs=[head_s, xrow, _const((1, D_MODEL)), _const((D_MODEL, D_MODEL))],
        out_specs=[half, half], out_shape=[jax.ShapeDtypeStruct((lp, D_SSM), F32)] * 2,
        compiler_params=_params(("parallel",), VMEM_LIMIT),
    )(head, x, g1, w_in)


def _out_proj(head, x, ys, yp, gs, gp, w_out):
    lp = ys.shape[0]

    def body(head_ref, x_ref, ys_ref, yp_ref, gs_ref, gp_ref, w_ref, h1_ref):
        h0 = jnp.where(pl.program_id(0) == 0, head_ref[...], x_ref[...])
        ys_t, yp_t = ys_ref[...], yp_ref[...]
        ms = ys_t * _rstd(ys_t) * gs_ref[...]
        mp = yp_t * _rstd(yp_t) * gp_ref[...]
        h1_ref[...] = h0 + _dot(ms, w_ref[pl.ds(0, D_SSM), :]) + _dot(mp, w_ref[pl.ds(D_SSM, D_SSM), :])

    head_s, xrow, full, half = _row_specs()
    return pl.pallas_call(
        body, name="out_proj", grid=(lp // TM,),
        in_specs=[head_s, xrow, half, half, _const((1, D_SSM)), _const((1, D_SSM)), _const((D_MODEL, D_MODEL))],
        out_specs=full, out_shape=jax.ShapeDtypeStruct((lp, D_MODEL), F32),
        compiler_params=_params(("parallel",), VMEM_LIMIT),
    )(head, x, ys, yp, gs, gp, w_out)


def _load_weights(c_hbm, wd_hbm, c_vmem, wd_vmem, sems):
    @pl.when(pl.program_id(0) == 0)
    def _():
        copies = [pltpu.make_async_copy(c_hbm, c_vmem, sems.at[0]),
                  pltpu.make_async_copy(wd_hbm, wd_vmem, sems.at[1])]
        for cp in copies:
            cp.start()
        for cp in copies:
            cp.wait()


def _ffn_scratch():
    return [pltpu.VMEM((N_DEV, D_MODEL, 2 * FF_PAD), BF16), pltpu.VMEM((N_DEV, FF_PAD, D_MODEL), BF16),
            pltpu.SemaphoreType.DMA((2,))]


def _ffn_forward(h1, g2, c_all, wd_all):
    lp = h1.shape[0]

    def body(h1_ref, g_ref, c_hbm, wd_hbm, ab_ref, n2_ref, h2_ref, c_vmem, wd_vmem, sems):
        _load_weights(c_hbm, wd_hbm, c_vmem, wd_vmem, sems)
        h1_t = h1_ref[...]
        n2 = (h1_t * _rstd(h1_t) * g_ref[...]).astype(BF16)
        n2_ref[...] = n2
        acc = h1_t
        for j in range(N_DEV):
            ab = jnp.dot(n2, c_vmem[j], preferred_element_type=F32)
            a, b = ab[:, :FF_PAD], ab[:, FF_PAD:]
            ab_ref[:, pl.ds(j * 2 * FF_PAD, 2 * FF_PAD)] = ab.astype(BF16)
            acc = acc + _dot(a * _sigmoid(a) * b, wd_vmem[j])
        h2_ref[...] = acc

    _, _, full, _ = _row_specs()
    wide = pl.BlockSpec((TM, N_DEV * 2 * FF_PAD), lambda i: (i, 0))
    return pl.pallas_call(
        body, name="ffn_forward", grid=(lp // TM,),
        in_specs=[full, _const((1, D_MODEL)), ANY, ANY], out_specs=[wide, full, full],
        out_shape=[jax.ShapeDtypeStruct((lp, N_DEV * 2 * FF_PAD), BF16),
                   jax.ShapeDtypeStruct((lp, D_MODEL), BF16), jax.ShapeDtypeStruct((lp, D_MODEL), F32)],
        scratch_shapes=_ffn_scratch(), compiler_params=_params(("arbitrary",), VMEM_LIMIT),
    )(h1, g2, c_all, wd_all)


def _ffn_backward(h2, target, h1, ab, gf, g2, c_all, wd_all):
    lp = h1.shape[0]

    def body(h2_ref, t_ref, h1_ref, ab_ref, gf_ref, g2_ref, c_hbm, wd_hbm,
             dh1_ref, dab_ref, dh2_ref, loss_ref, dgf_ref, dg2_ref, c_vmem, wd_vmem, sems):
        i = pl.program_id(0)
        _load_weights(c_hbm, wd_hbm, c_vmem, wd_vmem, sems)

        @pl.when(i == 0)
        def _():
            loss_ref[...] = jnp.zeros(loss_ref.shape, F32)
            dgf_ref[...] = jnp.zeros(dgf_ref.shape, F32)
            dg2_ref[...] = jnp.zeros(dg2_ref.shape, F32)

        h2_t = h2_ref[...]
        rf = _rstd(h2_t)
        xf = h2_t * rf
        diff = jnp.where(i == 0, 0.0, xf * gf_ref[...] - t_ref[...])
        loss_ref[...] += 0.5 * jnp.sum(diff * diff) / D_MODEL
        dh2, dgf = _rms_bwd(diff / D_MODEL, xf, rf, gf_ref[...])
        dgf_ref[...] += dgf
        dh2_b = dh2.astype(BF16)
        dh2_ref[...] = dh2_b

        dn2 = jnp.zeros((TM, D_MODEL), F32)
        for j in range(N_DEV):
            cols = pl.ds(j * 2 * FF_PAD, 2 * FF_PAD)
            dff = _dot_nt(dh2_b, wd_vmem[j])
            ab_t = ab_ref[:, cols].astype(F32)
            a, b = ab_t[:, :FF_PAD], ab_t[:, FF_PAD:]
            sg = _sigmoid(a)
            dab_ref[:, pl.ds(j * 2 * FF_PAD, FF_PAD)] = (dff * b * sg * (1.0 + a * (1.0 - sg))).astype(BF16)
            dab_ref[:, pl.ds(j * 2 * FF_PAD + FF_PAD, FF_PAD)] = (dff * a * sg).astype(BF16)
            dn2 = dn2 + _dot_nt(dab_ref[:, cols], c_vmem[j])

        h1_t = h1_ref[...]
        r2 = _rstd(h1_t)
        dx, dg2 = _rms_bwd(dn2, h1_t * r2, r2, g2_ref[...])
        dg2_ref[...] += dg2
        dh1_ref[...] = dh2 + dx

    _, xrow, full, _ = _row_specs()
    wide = pl.BlockSpec((TM, N_DEV * 2 * FF_PAD), lambda i: (i, 0))
    vec = _const((1, D_MODEL))
    return pl.pallas_call(
        body, name="ffn_backward", grid=(lp // TM,),
        in_specs=[full, xrow, full, wide, vec, vec, ANY, ANY],
        out_specs=[full, wide, full, _const((1, PACK_LANES)), vec, vec],
        out_shape=[jax.ShapeDtypeStruct((lp, D_MODEL), F32),
                   jax.ShapeDtypeStruct((lp, N_DEV * 2 * FF_PAD), BF16),
                   jax.ShapeDtypeStruct((lp, D_MODEL), BF16),
                   jax.ShapeDtypeStruct((1, PACK_LANES), F32),
                   jax.ShapeDtypeStruct((1, D_MODEL), F32), jax.ShapeDtypeStruct((1, D_MODEL), F32)],
        scratch_shapes=_ffn_scratch(), compiler_params=_params(("arbitrary",), VMEM_LIMIT),
    )(h2, target, h1, ab, gf, g2, c_all, wd_all)


def _ffn_wgrad(n2t, dh2, ab, dab):
    lp = dh2.shape[0]
    rows = lp // WGRAD_STEPS

    def body(n2t_ref, dh2_ref, ab_ref, dab_ref, dc_ref, dwd_ref, dc_acc, dwd_acc):
        i = pl.program_id(1)

        @pl.when(i == 0)
        def _():
            dc_acc[...] = jnp.zeros(dc_acc.shape, F32)
            dwd_acc[...] = jnp.zeros(dwd_acc.shape, F32)

        ab_t = ab_ref[...].astype(F32)
        a, b = ab_t[:, :FF_PAD], ab_t[:, FF_PAD:]
        dc_acc[...] += jnp.dot(n2t_ref[...], dab_ref[...], preferred_element_type=F32)
        dwd_acc[...] += _dot_tn(a * _sigmoid(a) * b, dh2_ref[...])

        @pl.when(i == pl.num_programs(1) - 1)
        def _():
            dc_ref[...] = dc_acc[...].astype(BF16)
            dwd_ref[...] = dwd_acc[...].astype(BF16)

    act_t = pl.BlockSpec((D_MODEL, rows), lambda j, i: (0, i))
    act = pl.BlockSpec((rows, D_MODEL), lambda j, i: (i, 0))
    shard = pl.BlockSpec((rows, 2 * FF_PAD), lambda j, i: (i, j))
    return pl.pallas_call(
        body, name="ffn_wgrad", grid=(N_DEV, WGRAD_STEPS),
        in_specs=[act_t, act, shard, shard],
        out_specs=[pl.BlockSpec((None, D_MODEL, 2 * FF_PAD), lambda j, i: (j, 0, 0)),
                   pl.BlockSpec((None, FF_PAD, D_MODEL), lambda j, i: (j, 0, 0))],
        out_shape=[jax.ShapeDtypeStruct((N_DEV, D_MODEL, 2 * FF_PAD), BF16),
                   jax.ShapeDtypeStruct((N_DEV, FF_PAD, D_MODEL), BF16)],
        scratch_shapes=[pltpu.VMEM((D_MODEL, 2 * FF_PAD), F32), pltpu.VMEM((FF_PAD, D_MODEL), F32)],
        compiler_params=_params(("parallel", "arbitrary"), VMEM_LIMIT),
    )(n2t, dh2, ab, dab)


def _out_proj_backward(dh1, ys, yp, gs, gp, w_out):
    lp = ys.shape[0]

    def body(dh1_ref, ys_ref, yp_ref, gs_ref, gp_ref, w_ref, dys_ref, dyp_ref, dgs_ref, dgp_ref, dw_out, dw_ref):
        @pl.when(pl.program_id(0) == 0)
        def _():
            dgs_ref[...] = jnp.zeros(dgs_ref.shape, F32)
            dgp_ref[...] = jnp.zeros(dgp_ref.shape, F32)
            dw_ref[...] = jnp.zeros(dw_ref.shape, F32)

        dh1_b = dh1_ref[...].astype(BF16)
        dmix = _dot_nt(dh1_b, w_ref[...])
        for y_ref, g_ref, dy_ref, dg_ref, lo in ((ys_ref, gs_ref, dys_ref, dgs_ref, 0),
                                                 (yp_ref, gp_ref, dyp_ref, dgp_ref, D_SSM)):
            y_t = y_ref[...]
            r = _rstd(y_t)
            xhat = y_t * r
            dy, dg = _rms_bwd(dmix[:, lo:lo + D_SSM], xhat, r, g_ref[...])
            dy_ref[...] = dy
            dg_ref[...] += dg
            dw_ref[pl.ds(lo, D_SSM), :] += _dot_tn(xhat * g_ref[...], dh1_b)

        @pl.when(pl.program_id(0) == pl.num_programs(0) - 1)
        def _():
            dw_out[...] = dw_ref[...].astype(BF16)

    _, _, full, half = _row_specs()
    vec = _const((1, D_SSM))
    return pl.pallas_call(
        body, name="out_proj_backward", grid=(lp // TM,),
        in_specs=[full, half, half, vec, vec, _const((D_MODEL, D_MODEL))],
        out_specs=[half, half, vec, vec, _const((D_MODEL, D_MODEL))],
        out_shape=[jax.ShapeDtypeStruct((lp, D_SSM), F32)] * 2 + [jax.ShapeDtypeStruct((1, D_SSM), F32)] * 2
        + [jax.ShapeDtypeStruct((D_MODEL, D_MODEL), BF16)],
        scratch_shapes=[pltpu.VMEM((D_MODEL, D_MODEL), F32)],
        compiler_params=_params(("arbitrary",), VMEM_LIMIT),
    )(dh1, ys, yp, gs, gp, w_out)


def _in_proj_backward(head, x, du, dv, dh1, g1, w_in):
    lp = du.shape[0]

    def body(head_ref, x_ref, du_ref, dv_ref, dh1_ref, g_ref, w_ref, dx_ref, dhead_ref, dg_ref, dw_out, dw_ref):
        i = pl.program_id(0)

        @pl.when(i == 0)
        def _():
            dg_ref[...] = jnp.zeros(dg_ref.shape, F32)
            dw_ref[...] = jnp.zeros(dw_ref.shape, F32)

        h0 = jnp.where(i == 0, head_ref[...], x_ref[...])
        r = _rstd(h0)
        xhat = h0 * r
        n1 = (xhat * g_ref[...]).astype(BF16)
        du_b, dv_b = du_ref[...].astype(BF16), dv_ref[...].astype(BF16)
        dn1 = _dot_nt(du_b, w_ref[:, pl.ds(0, D_SSM)]) + _dot_nt(dv_b, w_ref[:, pl.ds(D_SSM, D_SSM)])
        dx, dg = _rms_bwd(dn1, xhat, r, g_ref[...])
        dg_ref[...] += dg
        dh0 = dh1_ref[...] + dx
        dx_ref[...] = dh0

        @pl.when(i == 0)
        def _():
            dhead_ref[...] = dh0

        dw_ref[:, pl.ds(0, D_SSM)] += _dot_tn(n1, du_b)
        dw_ref[:, pl.ds(D_SSM, D_SSM)] += _dot_tn(n1, dv_b)

        @pl.when(i == pl.num_programs(0) - 1)
        def _():
            dw_out[...] = dw_ref[...].astype(BF16)

    head_s, xrow, full, half = _row_specs()
    vec = _const((1, D_MODEL))
    mat = _const((D_MODEL, D_MODEL))
    return pl.pallas_call(
        body, name="in_proj_backward", grid=(lp // TM,),
        in_specs=[head_s, xrow, half, half, full, vec, mat],
        out_specs=[xrow, head_s, vec, mat],
        out_shape=[jax.ShapeDtypeStruct(x.shape, F32), jax.ShapeDtypeStruct((HEAD, D_MODEL), F32),
                   jax.ShapeDtypeStruct((1, D_MODEL), F32), jax.ShapeDtypeStruct((D_MODEL, D_MODEL), BF16)],
        scratch_shapes=[pltpu.VMEM((D_MODEL, D_MODEL), F32)],
        compiler_params=_params(("arbitrary",), VMEM_LIMIT),
    )(head, x, du, dv, dh1, g1, w_in)


def _permute_rows(a):
    lp, n = a.shape
    return a.reshape(SUBLANES, lp // SUBLANES, n).transpose(1, 0, 2).reshape(lp, n)


def _unpermute_rows(a):
    lp, n = a.shape
    return a.reshape(lp // SUBLANES, SUBLANES, n).transpose(1, 0, 2).reshape(lp, n)


def _block_diag(blocks):
    _, r, c = blocks.shape
    b = blocks.reshape(N_COL, GROUPS_PER_COL, r, 1, c)
    eye = jnp.eye(GROUPS_PER_COL, dtype=blocks.dtype).reshape(1, GROUPS_PER_COL, 1, GROUPS_PER_COL, 1)
    return (b * eye).reshape(N_COL, GROUPS_PER_COL * r, GROUPS_PER_COL * c)


def _block_diag_extract(mats, r, c):
    m = mats.reshape(N_COL, GROUPS_PER_COL, r, GROUPS_PER_COL, c)
    eye = jnp.eye(GROUPS_PER_COL, dtype=mats.dtype).reshape(1, GROUPS_PER_COL, 1, GROUPS_PER_COL, 1)
    return jnp.sum(m * eye, axis=3).reshape(SSM_GROUPS, r, c)


def _pack(parts, dtype):
    rows = []
    for p in parts:
        flat = p.reshape(-1).astype(dtype)
        pad = (-flat.shape[0]) % PACK_UNIT
        rows.append(jnp.pad(flat, (0, pad)).reshape(-1, PACK_LANES))
    n_rows = sum(r.shape[0] for r in rows)
    if n_rows % 16:
        rows.append(jnp.zeros((8, PACK_LANES), dtype))
    return jnp.concatenate(rows, axis=0)


def _as2d(a):
    return a.reshape(-1, a.shape[-1])


def _unpack(packed, shapes):
    out, row = [], 0
    for shape in shapes:
        size = 1
        for s in shape:
            size *= s
        n_rows = -(-size // PACK_UNIT) * 8
        out.append(packed[row:row + n_rows].reshape(-1)[:size].reshape(shape))
        row += n_rows
    return out


def _pad_cols(a):
    return jnp.pad(a, ((0, 0), (0, FF_PAD - FF_SHARD)))


def _pad_rows(a):
    return jnp.pad(a, ((0, FF_PAD - FF_SHARD), (0, 0)))


def _gate_up(gate, up):
    return jnp.concatenate([_pad_cols(gate), _pad_cols(up)], axis=1)


def kernel(x, meta_tokens, norm1_g, w_in, ssm_lambda_re, ssm_lambda_im, ssm_log_step, ssm_b_re, ssm_b_im, ssm_c_re, ssm_c_im, ssm_d, ssm_glu_w, ssm_glu_b, ssm_norm_g, pool_w, pool_scale, pool_norm_g, w_out, norm2_g, w_gate, w_up, w_down, final_norm_g, loss_target, m_meta_tokens, m_norm1_g, m_w_in, m_ssm_lambda_re, m_ssm_lambda_im, m_ssm_log_step, m_ssm_b_re, m_ssm_b_im, m_ssm_c_re, m_ssm_c_im, m_ssm_d, m_ssm_glu_w, m_ssm_glu_b, m_ssm_norm_g, m_pool_w, m_pool_scale, m_pool_norm_g, m_w_out, m_norm2_g, m_w_gate, m_w_up, m_w_down, m_final_norm_g, v_meta_tokens, v_norm1_g, v_w_in, v_ssm_lambda_re, v_ssm_lambda_im, v_ssm_log_step, v_ssm_b_re, v_ssm_b_im, v_ssm_c_re, v_ssm_c_im, v_ssm_d, v_ssm_glu_w, v_ssm_glu_b, v_ssm_norm_g, v_pool_w, v_pool_scale, v_pool_norm_g, v_w_out, v_norm2_g, v_w_gate, v_w_up, v_w_down, v_final_norm_g):
    given = dict(locals())
    weights = {n: given[n] for n in WEIGHT_NAMES}
    n_meta = meta_tokens.shape[0]
    me = 4 * lax.axis_index("x") + 2 * lax.axis_index("y") + lax.axis_index("c")

    shard_rows = w_in.shape[1]
    first = [w_in[0].astype(BF16), meta_tokens]
    first_make = _push_copies(ALL_PEERS, [False, False])
    first_x, first_token = _split_start(first + [_landing(s, False) for s in first], first_make,
                                        2 * len(ALL_PEERS), "gather_w_in_start")

    xs = x[0]
    tgt = loss_target[0]
    first_row = HEAD - n_meta
    g1, g2, gf = norm1_g, norm2_g, final_norm_g.reshape(1, D_MODEL)
    gs, gp = ssm_norm_g, pool_norm_g

    lam_re, lam_im = ssm_lambda_re[0] + first_token[:1, :1], ssm_lambda_im[0]
    log_step = ssm_log_step[0].reshape(SSM_GROUPS, 1)
    b_re = ssm_b_re[0].reshape(SSM_GROUPS * SSM_STATE, SSM_GROUP)
    b_im = ssm_b_im[0].reshape(SSM_GROUPS * SSM_STATE, SSM_GROUP)
    abr, abi, zr, zi = _s5_disc_a(lam_re, lam_im, log_step)
    zr_col, zi_col = zr.reshape(-1, 1), zi.reshape(-1, 1)
    bbr, bbi = _s5_disc_b(zr_col, zi_col, b_re, b_im)
    to_bd = lambda b: _block_diag(b.reshape(SSM_GROUPS, SSM_STATE, SSM_GROUP).transpose(0, 2, 1)).astype(BF16)
    bre_bd, bim_bd = to_bd(bbr), to_bd(bbi)
    cret_bd = _block_diag(ssm_c_re[0].transpose(0, 2, 1)).astype(BF16)
    cimt_bd = _block_diag(ssm_c_im[0].transpose(0, 2, 1)).astype(BF16)
    glu_bd = _block_diag(ssm_glu_w[0]).astype(BF16)
    s5_consts = (abr.reshape(1, -1), abi.reshape(1, -1), bre_bd, bim_bd, cret_bd, cimt_bd,
                 ssm_d[0].reshape(1, D_SSM), glu_bd, ssm_glu_b[0].reshape(1, D_SSM))
    pool_sc = pool_scale[0].reshape(POOL_GROUPS, 1, POOL_DIM)

    (_, _, w_in_all, meta_all), first_done = _split_wait(first_x, first_make, bre_bd, "gather_w_in_wait")
    w_in_all = w_in_all.reshape(D_MODEL, D_MODEL)
    meta_full = meta_all.transpose(1, 0, 2).reshape(n_meta, D_MODEL)
    head = jnp.concatenate([jnp.zeros((HEAD - n_meta, D_MODEL), F32), meta_full], axis=0)
    shards = [(w_out[0] + first_done[:1, :1]).astype(BF16), _pad_rows(w_down[0]).astype(BF16),
              _gate_up(w_gate[0], w_up[0]).astype(BF16)]
    gather_make = _push_copies((SIBLING,) + CHIP_PEERS, [False, False, False])
    gather, gather_token = _split_start(shards + [_landing(s, False) for s in shards], gather_make,
                                        3 * (1 + len(CHIP_PEERS)), "gather_start")

    u, v = _in_proj(head, xs, g1 + gather_token[:1, :1], w_in_all)
    u_p = _permute_rows(u)
    ys_p = _s5_forward(u_p, *s5_consts)
    landed, _ = _split_wait(gather, gather_make, ys_p, "gather_wait")
    forward_make = _forward_copies(3)
    forward, forward_token = _split_start(list(landed[3:]), forward_make, 3 * len(CHIP_PEERS), "gather_forward_start")
    ys = _unpermute_rows(ys_p)
    yp = _pool_forward(v, pool_w[0], pool_sc + forward_token[:1, :1], first_row)
    (w_out_all, wd_all, c_all), _ = _split_wait(forward, forward_make, yp, "gather_forward_wait")
    w_out_all = w_out_all.reshape(D_MODEL, D_MODEL)
    h1 = _out_proj(head, xs, ys, yp, gs, gp, w_out_all)
    ab, n2, h2 = _ffn_forward(h1, g2, c_all, wd_all)

    dh1, dab, dh2, loss_part, d_gf, d_g2 = _ffn_backward(h2, tgt, h1, ab, gf, g2, c_all, wd_all)
    d_c, d_wd = _ffn_wgrad(n2.T, dh2, ab, dab)
    ffn_make = _push_copies(ALL_PEERS, [True, True])
    ffn_x, ffn_token = _split_start([d_c, d_wd, _landing(d_c, True), _landing(d_wd, True)], ffn_make,
                                    2 * len(ALL_PEERS), "ffn_grad_start")
    dys, dyp, d_gs, d_gp, d_wout = _out_proj_backward(dh1, ys, yp, gs + ffn_token[:1, :1], gp, w_out_all)
    dv, d_pool_w, d_pool_sc = _pool_backward(v, dyp, pool_w[0], pool_sc, first_row)
    (du_p, d_ar, d_ai, d_bre_bd, d_bim_bd, d_cre_bd, d_cim_bd, d_d, d_glu_bd, d_glub) = _s5_backward(
        u_p, _permute_rows(dys), *s5_consts)
    du = _unpermute_rows(du_p)

    from_bd = lambda m: _block_diag_extract(m, SSM_GROUP, SSM_STATE).transpose(0, 2, 1)
    d_bbr = from_bd(d_bre_bd).reshape(SSM_GROUPS * SSM_STATE, SSM_GROUP)
    d_bbi = from_bd(d_bim_bd).reshape(SSM_GROUPS * SSM_STATE, SSM_GROUP)
    d_zr, d_zi, d_b_re, d_b_im = _s5_disc_b_bwd(zr_col, zi_col, b_re, b_im, d_bbr, d_bbi)
    d_lam_re, d_lam_im, d_log_step = _s5_disc_a_bwd(
        lam_re, lam_im, log_step,
        (d_ar.reshape(SSM_GROUPS, SSM_STATE), d_ai.reshape(SSM_GROUPS, SSM_STATE),
         d_zr.reshape(SSM_GROUPS, SSM_STATE), d_zi.reshape(SSM_GROUPS, SSM_STATE)))
    small_grads = {
        "ssm_lambda_re": d_lam_re, "ssm_lambda_im": d_lam_im, "ssm_log_step": d_log_step,
        "ssm_b_re": d_b_re, "ssm_b_im": d_b_im,
        "ssm_c_re": _block_diag_extract(d_cre_bd, SSM_GROUP, SSM_STATE),
        "ssm_c_im": _block_diag_extract(d_cim_bd, SSM_GROUP, SSM_STATE),
        "ssm_d": d_d, "ssm_glu_w": _block_diag_extract(d_glu_bd, SSM_GROUP, SSM_GROUP), "ssm_glu_b": d_glub,
        "ssm_norm_g": d_gs, "pool_w": d_pool_w, "pool_scale": d_pool_sc, "pool_norm_g": d_gp,
        "norm2_g": d_g2, "final_norm_g": d_gf,
    }

    early_names = SMALL_NAMES[1:-1]
    early_pack = _pack([small_grads[n] for n in early_names], BF16)
    d_wout = d_wout.reshape(N_DEV, shard_rows, D_MODEL)
    early_make = _push_copies(ALL_PEERS, [True, False])
    early_x, early_token = _split_start([d_wout, early_pack, _landing(d_wout, True), _landing(early_pack, False)],
                                        early_make, 2 * len(ALL_PEERS), "early_grad_start")
    d_x, d_head, d_g1, d_win = _in_proj_backward(head, xs, du, dv, dh1, g1 + early_token[:1, :1], w_in_all)
    (_, _, r_c, r_wd), _ = _split_wait(ffn_x, ffn_make, d_g1, "ffn_grad_wait")
    (_, _, r_wout, r_early), _ = _split_wait(early_x, early_make, d_g1, "early_grad_wait")
    d_win = d_win.reshape(N_DEV, shard_rows, D_MODEL)
    late_pack = _pack([d_g1, d_gf, d_head[first_row:], loss_part], F32)
    late_make = _push_copies(ALL_PEERS, [True, False])
    late_x, late_token = _split_start([d_win, late_pack, _landing(d_win, True), _landing(late_pack, False)],
                                      late_make, 2 * len(ALL_PEERS), "late_grad_start")

    results = {}
    res_c = _adamw(r_c, _gate_up(w_gate[0], w_up[0]) + late_token[:1, :1], _gate_up(m_w_gate[0], m_w_up[0]),
                   _gate_up(v_w_gate[0], v_w_up[0]), 128, "adamw_gate_up")
    results["w_gate"] = [r[:, :FF_SHARD] for r in res_c]
    results["w_up"] = [r[:, FF_PAD:FF_PAD + FF_SHARD] for r in res_c]
    res_wd = _adamw(r_wd, _pad_rows(w_down[0]), _pad_rows(m_w_down[0]), _pad_rows(v_w_down[0]), 128, "adamw_w_down")
    results["w_down"] = [r[:FF_SHARD] for r in res_wd]
    results["w_out"] = _adamw(r_wout, w_out[0], m_w_out[0], v_w_out[0], shard_rows, "adamw_w_out")
    (_, _, r_win, r_late), _ = _split_wait(late_x, late_make, results["w_out"][1], "late_grad_wait")
    results["w_in"] = _adamw(r_win, w_in[0], m_w_in[0], v_w_in[0], shard_rows, "adamw_w_in")

    sum_early, sum_late = _reduce_slots([r_early, r_late], "small_grad_sums")
    g_small = _unpack(sum_early, [weights[n].shape for n in early_names])
    g_norm1, g_final, g_meta_all, loss_row = _unpack(
        sum_late, [norm1_g.shape, final_norm_g.shape, (n_meta, D_MODEL), (1, PACK_LANES)])
    g_small = [g_norm1] + g_small + [g_final]
    small_2d = lambda prefix: [_as2d(given[prefix + n]) for n in SMALL_NAMES]
    res_small = _adamw_many([_as2d(g) for g in g_small], small_2d(""), small_2d("m_"), small_2d("v_"), "adamw_small")
    for idx, n in enumerate(SMALL_NAMES):
        results[n] = [g_small[idx]] + [part[idx] for part in res_small]
    shard_cols = meta_tokens.shape[1]
    g_meta = lax.dynamic_slice_in_dim(g_meta_all, me * shard_cols, shard_cols, axis=1)
    results["meta_tokens"] = _adamw(g_meta[None], meta_tokens, m_meta_tokens, v_meta_tokens, n_meta, "adamw_meta")

    out = [loss_row[0, 0], d_x[None]]
    for part in range(4):
        for n in WEIGHT_NAMES:
            out.append(results[n][part].reshape(weights[n].shape))
    return tuple(out)
```

```python
import functools

import jax
import jax.numpy as jnp
from jax import lax
from jax.experimental import pallas as pl
from jax.experimental.pallas import tpu as pltpu

F32 = jnp.float32
BF16 = jnp.bfloat16

N_DEV = 8
D_MODEL = 1024
D_SSM = 512
SSM_GROUP = 16
SSM_STATE = 64
SSM_GROUPS = 32
POOL_GROUPS = 4
POOL_DIM = 128
COL_U = 128
COL_S = 512
N_COL = D_SSM // COL_U
GROUPS_PER_COL = COL_U // SSM_GROUP
FF_SHARD = 352
FF_PAD = 384
TM = 256
WGRAD_STEPS = 2
HEAD = TM
SUBLANES = 8
SCAN_UNROLL = 4
POOL_HALO = 128
EPS = 1e-6
STEP_FLOOR = -1e-4
VMEM_LIMIT = 60 * 1024 * 1024

ADAM_LR = 0.001
ADAM_B1 = 0.9
ADAM_B2 = 0.999
ADAM_EPS = 1e-08
ADAM_WD = 0.01
ADAM_STEP = 10

MESH_ID = pl.DeviceIdType.MESH
ANY = pl.BlockSpec(memory_space=pl.ANY)

SMALL_NAMES = ("norm1_g", "ssm_lambda_re", "ssm_lambda_im", "ssm_log_step", "ssm_b_re", "ssm_b_im",
               "ssm_c_re", "ssm_c_im", "ssm_d", "ssm_glu_w", "ssm_glu_b", "ssm_norm_g", "pool_w",
               "pool_scale", "pool_norm_g", "norm2_g", "final_norm_g")
WEIGHT_NAMES = ("meta_tokens", "norm1_g", "w_in", "ssm_lambda_re", "ssm_lambda_im", "ssm_log_step",
                "ssm_b_re", "ssm_b_im", "ssm_c_re", "ssm_c_im", "ssm_d", "ssm_glu_w", "ssm_glu_b",
                "ssm_norm_g", "pool_w", "pool_scale", "pool_norm_g", "w_out", "norm2_g", "w_gate",
                "w_up", "w_down", "final_norm_g")
PACK_LANES = 128
PACK_UNIT = 8 * PACK_LANES


def _dot(a, b):
    return jnp.dot(a.astype(BF16), b.astype(BF16), preferred_element_type=F32)


def _dot_nt(a, b):
    return lax.dot_general(a.astype(BF16), b.astype(BF16), (((1,), (1,)), ((), ())), preferred_element_type=F32)


def _dot_tn(a, b):
    return lax.dot_general(a.astype(BF16), b.astype(BF16), (((0,), (0,)), ((), ())), preferred_element_type=F32)


def _sigmoid(x):
    return 1.0 / (1.0 + jnp.exp(-x))


def _rstd(x):
    return lax.rsqrt(jnp.mean(x * x, axis=-1, keepdims=True) + EPS)


def _rms_bwd(dy, xhat, r, g):
    dxh = dy * g
    dx = r * (dxh - xhat * jnp.mean(dxh * xhat, axis=-1, keepdims=True))
    return dx, jnp.sum(dy * xhat, axis=0, keepdims=True)


def _params(sem, vmem=None):
    return pltpu.CompilerParams(dimension_semantics=sem, vmem_limit_bytes=vmem)


def _const(shape):
    return pl.BlockSpec(shape, lambda *_: (0,) * len(shape))


def _xrow(i):
    return (jnp.maximum(i - 1, 0), 0)


HBM = pl.BlockSpec(memory_space=pltpu.HBM)
SEM = pl.BlockSpec(memory_space=pltpu.SEMAPHORE)
EFFECT = pltpu.SideEffectType.DATAFLOW_SIDE_EFFECTING
ALL_PEERS = tuple(range(1, N_DEV))
SIBLING = 1
CHIP_PEERS = (2, 4, 6)


def _me():
    return 4 * lax.axis_index("x") + 2 * lax.axis_index("y") + lax.axis_index("c")


def _peer(k):
    x, y, c = lax.axis_index("x"), lax.axis_index("y"), lax.axis_index("c")
    px = 1 - x if k & 4 else x
    py = 1 - y if k & 2 else y
    pc = 1 - c if k & 1 else c
    return (px, py, pc), 4 * px + 2 * py + pc


def _landing(arr, scatter):
    if scatter:
        own = lax.dynamic_index_in_dim(arr, _me(), 0, keepdims=False)
    else:
        own = arr
    return lax.dynamic_update_index_in_dim(lax.empty((N_DEV,) + own.shape, arr.dtype), own, _me(), 0)


def _push_copies(peers, scatter):
    n_arr = len(scatter)

    def make(refs, send_sems, recv_sems):
        copies = []
        for a in range(n_arr):
            for i, k in enumerate(peers):
                peer_id, peer = _peer(k)
                sem = a * len(peers) + i
                copies.append(pltpu.make_async_remote_copy(
                    src_ref=refs[a].at[peer] if scatter[a] else refs[a], dst_ref=refs[n_arr + a].at[_me()],
                    send_sem=send_sems.at[sem], recv_sem=recv_sems.at[sem],
                    device_id=peer_id, device_id_type=MESH_ID))
        return copies
    return make


def _forward_copies(n_arr):
    def make(refs, send_sems, recv_sems):
        copies = []
        sibling_id, _ = _peer(SIBLING)
        for a in range(n_arr):
            for i, k in enumerate(CHIP_PEERS):
                slot = refs[a].at[_peer(k)[1]]
                sem = a * len(CHIP_PEERS) + i
                copies.append(pltpu.make_async_remote_copy(
                    src_ref=slot, dst_ref=slot, send_sem=send_sems.at[sem], recv_sem=recv_sems.at[sem],
                    device_id=sibling_id, device_id_type=MESH_ID))
        return copies
    return make


def _split_start(operands, make, n_sem, name):
    n_op = len(operands)

    def body(*refs):
        for cp in make(refs[:n_op], refs[n_op], refs[n_op + 1]):
            cp.start()
        refs[-1][...] = jnp.zeros(refs[-1].shape, F32)

    out = pl.pallas_call(
        body, name=name,
        out_shape=(pltpu.SemaphoreType.DMA((n_sem,)), pltpu.SemaphoreType.DMA((n_sem,)),
                   *[pltpu.HBM(t.shape, t.dtype) for t in operands], jax.ShapeDtypeStruct((8, PACK_LANES), F32)),
        in_specs=[HBM] * n_op, out_specs=(SEM, SEM, *[HBM] * n_op, pl.BlockSpec(memory_space=pltpu.VMEM)),
        input_output_aliases={i: 2 + i for i in range(n_op)},
        compiler_params=pltpu.CompilerParams(has_side_effects=EFFECT),
    )(*[pltpu.with_memory_space_constraint(t, pltpu.HBM) for t in operands])
    return out[:-1], out[-1]


def _split_wait(started, make, after, name):
    send_sems, recv_sems, thru = started[0], started[1], started[2:]
    n_op = len(thru)

    def body(*refs):
        for cp in make(refs[:n_op], refs[n_op], refs[n_op + 1]):
            cp.wait_send()
            cp.wait_recv()
        refs[-1][...] = jnp.zeros(refs[-1].shape, F32)

    out = pl.pallas_call(
        body, name=name,
        out_shape=(*[pltpu.HBM(t.shape, t.dtype) for t in thru], jax.ShapeDtypeStruct((8, PACK_LANES), F32)),
        in_specs=[HBM] * n_op + [SEM, SEM, ANY], out_specs=(*[HBM] * n_op, pl.BlockSpec(memory_space=pltpu.VMEM)),
        input_output_aliases={i: i for i in range(n_op)},
        compiler_params=pltpu.CompilerParams(has_side_effects=EFFECT),
    )(*thru, send_sems, recv_sems, after)
    return out[:-1], out[-1]


def _adamw_math(g, w, m, v):
    nm = ADAM_B1 * m + (1.0 - ADAM_B1) * g
    nv = ADAM_B2 * v + (1.0 - ADAM_B2) * (g * g)
    m_hat = nm / (1.0 - ADAM_B1 ** ADAM_STEP)
    v_hat = nv / (1.0 - ADAM_B2 ** ADAM_STEP)
    return -ADAM_LR * (m_hat / (jnp.sqrt(v_hat) + ADAM_EPS) + ADAM_WD * w), nm, nv


def _sum_slots(s_ref):
    g = s_ref[0].astype(F32)
    for s in range(1, s_ref.shape[0]):
        g = g + s_ref[s].astype(F32)
    return g


def _adamw(slots, w, m, v, tile_rows, name):
    n, rows, cols = slots.shape

    def body(s_ref, w_ref, m_ref, v_ref, g_ref, d_ref, nm_ref, nv_ref):
        g = _sum_slots(s_ref)
        g_ref[...] = g
        d_ref[...], nm_ref[...], nv_ref[...] = _adamw_math(g, w_ref[...], m_ref[...], v_ref[...])

    tile = pl.BlockSpec((tile_rows, cols), lambda i: (i, 0))
    return pl.pallas_call(
        body, name=name, grid=(rows // tile_rows,),
        in_specs=[pl.BlockSpec((n, tile_rows, cols), lambda i: (0, i, 0)), tile, tile, tile],
        out_specs=[tile] * 4, out_shape=[jax.ShapeDtypeStruct((rows, cols), F32)] * 4,
        compiler_params=_params(("parallel",), VMEM_LIMIT),
    )(slots, w, m, v)


def _reduce_slots(slot_arrays, name):
    def body(*refs):
        n_arr = len(refs) // 2
        for s_ref, o_ref in zip(refs[:n_arr], refs[n_arr:]):
            o_ref[...] = _sum_slots(s_ref)
    return pl.pallas_call(
        body, name=name, out_shape=[jax.ShapeDtypeStruct(s.shape[1:], F32) for s in slot_arrays],
        compiler_params=_params(None, VMEM_LIMIT))(*slot_arrays)


def _adamw_many(grads, ws, ms, vs, name):
    n = len(grads)

    def body(*refs):
        ins, outs = refs[:4 * n], refs[4 * n:]
        for i in range(n):
            g, w, m, v = (ins[j * n + i][...] for j in range(4))
            outs[i][...], outs[n + i][...], outs[2 * n + i][...] = _adamw_math(g, w, m, v)

    out = pl.pallas_call(
        body, name=name, out_shape=[jax.ShapeDtypeStruct(w.shape, F32) for w in ws] * 3,
        compiler_params=_params(None, VMEM_LIMIT))(*grads, *ws, *ms, *vs)
    return out[:n], out[n:2 * n], out[2 * n:]


def _disc_a(lam_re, lam_im, log_step):
    lr = jnp.minimum(lam_re, STEP_FLOOR)
    step = jnp.exp(log_step)
    mag = jnp.exp(lr * step)
    ang = lam_im * step
    abr = mag * jnp.cos(ang)
    abi = mag * jnp.sin(ang)
    nr = abr - 1.0
    den = lr * lr + lam_im * lam_im
    cr = (nr * lr + abi * lam_im) / den
    ci = (abi * lr - nr * lam_im) / den
    return abr, abi, cr, ci


def _disc_b(cr, ci, b_re, b_im):
    return cr * b_re - ci * b_im, cr * b_im + ci * b_re


def _s5_disc_a(lam_re, lam_im, log_step):
    def body(lr_ref, li_ref, ls_ref, *outs):
        for o, val in zip(outs, _disc_a(lr_ref[...], li_ref[...], ls_ref[...])):
            o[...] = val
    return pl.pallas_call(body, name="s5_disc_a", out_shape=[jax.ShapeDtypeStruct(lam_re.shape, F32)] * 4)(
        lam_re, lam_im, log_step)


def _s5_disc_a_bwd(lam_re, lam_im, log_step, cts):
    def body(lr_ref, li_ref, ls_ref, c0, c1, c2, c3, dlr_ref, dli_ref, dls_ref):
        _, vjp = jax.vjp(_disc_a, lr_ref[...], li_ref[...], ls_ref[...])
        dlr, dli, dls = vjp((c0[...], c1[...], c2[...], c3[...]))
        dlr_ref[...] = dlr
        dli_ref[...] = dli
        dls_ref[...] = dls
    return pl.pallas_call(
        body, name="s5_disc_a_bwd",
        out_shape=[jax.ShapeDtypeStruct(lam_re.shape, F32), jax.ShapeDtypeStruct(lam_re.shape, F32),
                   jax.ShapeDtypeStruct(log_step.shape, F32)])(lam_re, lam_im, log_step, *cts)


def _s5_disc_b(cr, ci, b_re, b_im):
    def body(cr_ref, ci_ref, br_ref, bi_ref, o_re, o_im):
        o_re[...], o_im[...] = _disc_b(cr_ref[...], ci_ref[...], br_ref[...], bi_ref[...])
    return pl.pallas_call(body, name="s5_disc_b", out_shape=[jax.ShapeDtypeStruct(b_re.shape, F32)] * 2)(
        cr, ci, b_re, b_im)


def _s5_disc_b_bwd(cr, ci, b_re, b_im, d_re, d_im):
    def body(cr_ref, ci_ref, br_ref, bi_ref, dr_ref, di_ref, dcr_ref, dci_ref, dbr_ref, dbi_ref):
        _, vjp = jax.vjp(_disc_b, cr_ref[...], ci_ref[...], br_ref[...], bi_ref[...])
        dcr_ref[...], dci_ref[...], dbr_ref[...], dbi_ref[...] = vjp((dr_ref[...], di_ref[...]))
    return pl.pallas_call(
        body, name="s5_disc_b_bwd",
        out_shape=[jax.ShapeDtypeStruct(cr.shape, F32)] * 2 + [jax.ShapeDtypeStruct(b_re.shape, F32)] * 2)(
            cr, ci, b_re, b_im, d_re, d_im)


def _cmul(ar, ai, br, bi):
    return ar * br - ai * bi, ar * bi + ai * br


def _cpow(ar, ai, n):
    rr, ri = jnp.ones_like(ar), jnp.zeros_like(ai)
    while n:
        if n & 1:
            rr, ri = _cmul(rr, ri, ar, ai)
        n >>= 1
        if n:
            ar, ai = _cmul(ar, ai, ar, ai)
    return rr, ri


def _tile_rows(i):
    if isinstance(i, int):
        return pl.ds(i * SUBLANES, SUBLANES)
    return pl.ds(pl.multiple_of(i * SUBLANES, SUBLANES), SUBLANES)


def _segment_scan(z_re, z_im, ar, ai, lseg, reverse, visit=None):
    shape = (SUBLANES, z_re.shape[1])
    half = lseg // 2
    arb = jnp.broadcast_to(ar, shape)
    aib = jnp.broadcast_to(ai, shape)
    zero = jnp.zeros(shape, F32)
    row = lax.broadcasted_iota(jnp.int32, shape, 0)

    def tiles(k):
        return (lseg - 1 - k, half - 1 - k) if reverse else (k, half + k)

    def advance(tile, sr, si):
        rows = _tile_rows(tile)
        nr, ni = _cmul(arb, aib, sr, si)
        return rows, nr + z_re[rows, :], ni + z_im[rows, :]

    def first_pass(k, carry):
        ta, tb = tiles(k)
        return advance(ta, carry[0], carry[1])[1:] + advance(tb, carry[2], carry[3])[1:]

    def unrolled(step):
        def body(it, carry):
            for j in range(SCAN_UNROLL):
                carry = step(it * SCAN_UNROLL + j, carry)
            return carry
        return body

    n_iter = half // SCAN_UNROLL
    fa_r, fa_i, fb_r, fb_i = lax.fori_loop(0, n_iter, unrolled(first_pass), (zero,) * 4)
    hr, hi = _cpow(arb, aib, half)
    pr, pi = _cmul(hr, hi, hr, hi)
    fr, fi = _cmul(hr, hi, fa_r, fa_i)
    fr, fi = fr + fb_r, fi + fb_i
    cr, ci = zero, zero
    for _ in range(SUBLANES - 1):
        tr, ti = _cmul(pr, pi, cr, ci)
        tr, ti = tr + fr, ti + fi
        if reverse:
            cr = jnp.where(row == SUBLANES - 1, 0.0, pltpu.roll(tr, SUBLANES - 1, 0))
            ci = jnp.where(row == SUBLANES - 1, 0.0, pltpu.roll(ti, SUBLANES - 1, 0))
        else:
            cr = jnp.where(row == 0, 0.0, pltpu.roll(tr, 1, 0))
            ci = jnp.where(row == 0, 0.0, pltpu.roll(ti, 1, 0))

    br, bi = _cmul(hr, hi, cr, ci)
    br, bi = br + fa_r, bi + fa_i

    def second_pass(k, carry, b_is_tile0=False):
        states, acc = list(carry[:4]), carry[4]
        for chain, tile in enumerate(tiles(k)):
            rows, nr, ni = advance(tile, states[2 * chain], states[2 * chain + 1])
            z_re[rows, :] = nr
            z_im[rows, :] = ni
            states[2 * chain], states[2 * chain + 1] = nr, ni
            if visit is not None:
                acc = visit(tile, nr, ni, acc, chain == 1 and b_is_tile0)
        return (*states, acc)

    acc0 = (zero, zero) if visit is not None else 0
    carry = lax.fori_loop(0, n_iter - 1, unrolled(second_pass), (cr, ci, br, bi, acc0))
    for k in range(half - SCAN_UNROLL, half - 1):
        carry = second_pass(k, carry)
    return second_pass(half - 1, carry, b_is_tile0=reverse)[4]


def _gelu(y):
    c = 0.7978845608028654
    return 0.5 * y * (1.0 + jnp.tanh(c * (y + 0.044715 * y * y * y)))


def _gelu_grad(y):
    c = 0.7978845608028654
    th = jnp.tanh(c * (y + 0.044715 * y * y * y))
    return 0.5 * (1.0 + th) + 0.5 * y * (1.0 - th * th) * c * (1.0 + 3.0 * 0.044715 * y * y)


def _s5_specs(lp):
    col_u = pl.BlockSpec((lp, COL_U), lambda j: (0, j))
    row_u = pl.BlockSpec((1, COL_U), lambda j: (0, j))
    row_s = pl.BlockSpec((1, COL_S), lambda j: (0, j))
    b_mat = pl.BlockSpec((None, COL_U, COL_S), lambda j: (j, 0, 0))
    c_mat = pl.BlockSpec((None, COL_S, COL_U), lambda j: (j, 0, 0))
    g_mat = pl.BlockSpec((None, COL_U, COL_U), lambda j: (j, 0, 0))
    return col_u, row_u, row_s, b_mat, c_mat, g_mat


def _s5_fill_states(u_ref, bre_ref, bim_ref, ar_ref, ai_ref, s_re, s_im, lseg, n_chunks, chunk):
    def fill(cidx, carry):
        rows = pl.ds(pl.multiple_of(cidx * chunk, SUBLANES), chunk)
        ub = u_ref[rows, :].astype(BF16)
        s_re[rows, :] = jnp.dot(ub, bre_ref[...], preferred_element_type=F32)
        s_im[rows, :] = jnp.dot(ub, bim_ref[...], preferred_element_type=F32)
        return carry
    lax.fori_loop(0, n_chunks, fill, 0)
    _segment_scan(s_re, s_im, ar_ref[...], ai_ref[...], lseg, reverse=False)


def _s5_forward(u_p, ar, ai, bre_bd, bim_bd, cret_bd, cimt_bd, d_row, glu_bd, glub_row):
    lp = u_p.shape[0]
    lseg = lp // SUBLANES
    chunk, n_chunks = lseg, SUBLANES

    def body(u_ref, ar_ref, ai_ref, bre_ref, bim_ref, cret_ref, cimt_ref, d_ref, glu_ref, glub_ref,
             ys_ref, s_re, s_im):
        _s5_fill_states(u_ref, bre_ref, bim_ref, ar_ref, ai_ref, s_re, s_im, lseg, n_chunks, chunk)

        def emit(cidx, carry):
            rows = pl.ds(pl.multiple_of(cidx * chunk, SUBLANES), chunk)
            y = (_dot(s_re[rows, :], cret_ref[...]) - _dot(s_im[rows, :], cimt_ref[...])
                 + d_ref[...] * u_ref[rows, :])
            g = _gelu(y)
            gate = _dot(g, glu_ref[...]) + glub_ref[...]
            ys_ref[rows, :] = g * _sigmoid(gate)
            return carry
        lax.fori_loop(0, n_chunks, emit, 0)

    col_u, row_u, row_s, b_mat, c_mat, g_mat = _s5_specs(lp)
    return pl.pallas_call(
        body, name="s5_forward", grid=(N_COL,),
        in_specs=[col_u, row_s, row_s, b_mat, b_mat, c_mat, c_mat, row_u, g_mat, row_u],
        out_specs=col_u, out_shape=jax.ShapeDtypeStruct((lp, D_SSM), F32),
        scratch_shapes=[pltpu.VMEM((lp, COL_S), F32), pltpu.VMEM((lp, COL_S), F32)],
        compiler_params=_params(("arbitrary",), VMEM_LIMIT),
    )(u_p, ar, ai, bre_bd, bim_bd, cret_bd, cimt_bd, d_row, glu_bd, glub_row)


def _s5_backward(u_p, dys_p, ar, ai, bre_bd, bim_bd, cret_bd, cimt_bd, d_row, glu_bd, glub_row):
    lp = u_p.shape[0]
    lseg = lp // SUBLANES
    chunk, n_chunks = lseg, SUBLANES

    def body(u_ref, dys_ref, ar_ref, ai_ref, bre_ref, bim_ref, cret_ref, cimt_ref, d_ref, glu_ref, glub_ref,
             du_ref, dar_ref, dai_ref, dbre_ref, dbim_ref, dcre_ref, dcim_ref, dd_ref, dglu_ref, dglub_ref,
             s_re, s_im, q_re, q_im):
        _s5_fill_states(u_ref, bre_ref, bim_ref, ar_ref, ai_ref, s_re, s_im, lseg, n_chunks, chunk)
        for ref in (dcre_ref, dcim_ref, dd_ref, dglu_ref, dglub_ref, dbre_ref, dbim_ref):
            ref[...] = jnp.zeros(ref.shape, F32)

        def mixer_bwd(cidx, carry):
            rows = pl.ds(pl.multiple_of(cidx * chunk, SUBLANES), chunk)
            u = u_ref[rows, :]
            sr, si = s_re[rows, :], s_im[rows, :]
            y = _dot(sr, cret_ref[...]) - _dot(si, cimt_ref[...]) + d_ref[...] * u
            g = _gelu(y)
            sg = _sigmoid(_dot(g, glu_ref[...]) + glub_ref[...])
            dout = dys_ref[rows, :]
            dgate = dout * g * sg * (1.0 - sg)
            dy = (dout * sg + _dot_nt(dgate, glu_ref[...])) * _gelu_grad(y)
            dglu_ref[...] += _dot_tn(g, dgate)
            dglub_ref[...] += jnp.sum(dgate, axis=0, keepdims=True)
            dd_ref[...] += jnp.sum(dy * u, axis=0, keepdims=True)
            dcre_ref[...] += _dot_tn(dy, sr)
            dcim_ref[...] -= _dot_tn(dy, si)
            q_re[rows, :] = _dot_nt(dy, cret_ref[...])
            q_im[rows, :] = -_dot_nt(dy, cimt_ref[...])
            du_ref[rows, :] = d_ref[...] * dy
            return carry
        lax.fori_loop(0, n_chunks, mixer_bwd, 0)

        row = lax.broadcasted_iota(jnp.int32, (SUBLANES, COL_S), 0)

        def visit(i, qr, qi, acc, is_tile0):
            if is_tile0:
                prev = _tile_rows(lseg - 1)
                pr = jnp.where(row == 0, 0.0, pltpu.roll(s_re[prev, :], 1, 0))
                pi = jnp.where(row == 0, 0.0, pltpu.roll(s_im[prev, :], 1, 0))
            else:
                prev = _tile_rows(i - 1)
                pr, pi = s_re[prev, :], s_im[prev, :]
            return acc[0] + qr * pr + qi * pi, acc[1] + qi * pr - qr * pi

        dar, dai = _segment_scan(q_re, q_im, ar_ref[...], -ai_ref[...], lseg, reverse=True, visit=visit)
        dar_ref[...] = jnp.sum(dar, axis=0, keepdims=True)
        dai_ref[...] = jnp.sum(dai, axis=0, keepdims=True)

        def input_bwd(cidx, carry):
            rows = pl.ds(pl.multiple_of(cidx * chunk, SUBLANES), chunk)
            qr, qi = q_re[rows, :], q_im[rows, :]
            u = u_ref[rows, :]
            du_ref[rows, :] += _dot_nt(qr, bre_ref[...]) + _dot_nt(qi, bim_ref[...])
            dbre_ref[...] += _dot_tn(u, qr)
            dbim_ref[...] += _dot_tn(u, qi)
            return carry
        lax.fori_loop(0, n_chunks, input_bwd, 0)

    col_u, row_u, row_s, b_mat, c_mat, g_mat = _s5_specs(lp)
    return pl.pallas_call(
        body, name="s5_backward", grid=(N_COL,),
        in_specs=[col_u, col_u, row_s, row_s, b_mat, b_mat, c_mat, c_mat, row_u, g_mat, row_u],
        out_specs=[col_u, row_s, row_s, b_mat, b_mat, b_mat, b_mat, row_u, g_mat, row_u],
        out_shape=[jax.ShapeDtypeStruct((lp, D_SSM), F32),
                   jax.ShapeDtypeStruct((1, N_COL * COL_S), F32), jax.ShapeDtypeStruct((1, N_COL * COL_S), F32),
                   jax.ShapeDtypeStruct((N_COL, COL_U, COL_S), F32), jax.ShapeDtypeStruct((N_COL, COL_U, COL_S), F32),
                   jax.ShapeDtypeStruct((N_COL, COL_U, COL_S), F32), jax.ShapeDtypeStruct((N_COL, COL_U, COL_S), F32),
                   jax.ShapeDtypeStruct((1, D_SSM), F32),
                   jax.ShapeDtypeStruct((N_COL, COL_U, COL_U), F32), jax.ShapeDtypeStruct((1, D_SSM), F32)],
        scratch_shapes=[pltpu.VMEM((lp, COL_S), F32)] * 4,
        compiler_params=_params(("arbitrary",), VMEM_LIMIT),
    )(u_p, dys_p, ar, ai, bre_bd, bim_bd, cret_bd, cimt_bd, d_row, glu_bd, glub_row)


def _band_apply(band, x):
    hi = x.astype(BF16)
    lo = (x - hi.astype(F32)).astype(BF16)
    dot = functools.partial(jnp.dot, preferred_element_type=F32)
    return dot(band, hi) + dot(band, lo)


def _pool_band(window, transposed):
    t = lax.broadcasted_iota(jnp.int32, (TM, TM + POOL_HALO), 0)
    c = lax.broadcasted_iota(jnp.int32, (TM, TM + POOL_HALO), 1)
    lag = c - t if transposed else t + POOL_HALO - c
    return jnp.where((lag >= 0) & (lag < window), 1.0, 0.0).astype(BF16)


def _pool_inv_count(tile, window, first_row):
    t = tile * TM + lax.broadcasted_iota(jnp.int32, (TM, 1), 0) - first_row
    return 1.0 / jnp.clip(t + 1, 1, window).astype(F32)


def _pool_specs(lp):
    col = pl.BlockSpec((lp, POOL_DIM), lambda k: (0, k))
    mat = pl.BlockSpec((None, POOL_DIM, POOL_DIM), lambda k: (k, 0, 0))
    row = pl.BlockSpec((None, 1, POOL_DIM), lambda k: (k, 0, 0))
    return col, mat, row


def _pool_forward(v, pool_w, pool_scale, first_row):
    lp = v.shape[0]
    n_tiles = lp // TM

    def body(v_ref, w_ref, sc_ref, yp_ref, vpad):
        window = jnp.left_shift(2, pl.program_id(0))
        vpad[pl.ds(0, POOL_HALO), :] = jnp.zeros((POOL_HALO, POOL_DIM), F32)
        vpad[pl.ds(POOL_HALO, lp), :] = v_ref[...]
        band = _pool_band(window, transposed=False)

        def tile(j, carry):
            start = pl.multiple_of(j * TM, TM)
            ext = vpad[pl.ds(start, TM + POOL_HALO), :]
            p = _band_apply(band, ext) * _pool_inv_count(j, window, first_row) - ext[POOL_HALO:, :]
            yp_ref[pl.ds(start, TM), :] = _dot(p, w_ref[...]) * sc_ref[...]
            return carry
        lax.fori_loop(0, n_tiles, tile, 0)

    col, mat, row = _pool_specs(lp)
    return pl.pallas_call(
        body, name="pool_forward", grid=(POOL_GROUPS,),
        in_specs=[col, mat, row], out_specs=col, out_shape=jax.ShapeDtypeStruct((lp, D_SSM), F32),
        scratch_shapes=[pltpu.VMEM((lp + POOL_HALO, POOL_DIM), F32)],
        compiler_params=_params(("arbitrary",), VMEM_LIMIT),
    )(v, pool_w, pool_scale)


def _pool_backward(v, dyp, pool_w, pool_scale, first_row):
    lp = v.shape[0]
    n_tiles = lp // TM

    def body(v_ref, dyp_ref, w_ref, sc_ref, dv_ref, dw_ref, dsc_ref, vpad, gpad):
        window = jnp.left_shift(2, pl.program_id(0))
        vpad[pl.ds(0, POOL_HALO), :] = jnp.zeros((POOL_HALO, POOL_DIM), F32)
        vpad[pl.ds(POOL_HALO, lp), :] = v_ref[...]
        gpad[pl.ds(lp, POOL_HALO), :] = jnp.zeros((POOL_HALO, POOL_DIM), F32)
        dw_ref[...] = jnp.zeros(dw_ref.shape, F32)
        dsc_ref[...] = jnp.zeros(dsc_ref.shape, F32)
        band = _pool_band(window, transposed=False)

        def linear_bwd(j, carry):
            start = pl.multiple_of(j * TM, TM)
            ext = vpad[pl.ds(start, TM + POOL_HALO), :]
            inv = _pool_inv_count(j, window, first_row)
            p = _band_apply(band, ext) * inv - ext[POOL_HALO:, :]
            z = _dot(p, w_ref[...])
            dyp_t = dyp_ref[pl.ds(start, TM), :]
            dz = dyp_t * sc_ref[...]
            dsc_ref[...] += jnp.sum(dyp_t * z, axis=0, keepdims=True)
            dw_ref[...] += _dot_tn(p, dz)
            dp = _dot_nt(dz, w_ref[...])
            gpad[pl.ds(start, TM), :] = dp * inv
            dv_ref[pl.ds(start, TM), :] = -dp
            return carry
        lax.fori_loop(0, n_tiles, linear_bwd, 0)
        band_t = _pool_band(window, transposed=True)

        def window_bwd(j, carry):
            start = pl.multiple_of(j * TM, TM)
            dv_ref[pl.ds(start, TM), :] += _band_apply(band_t, gpad[pl.ds(start, TM + POOL_HALO), :])
            return carry
        lax.fori_loop(0, n_tiles, window_bwd, 0)

    col, mat, row = _pool_specs(lp)
    return pl.pallas_call(
        body, name="pool_backward", grid=(POOL_GROUPS,),
        in_specs=[col, col, mat, row], out_specs=[col, mat, row],
        out_shape=[jax.ShapeDtypeStruct((lp, D_SSM), F32),
                   jax.ShapeDtypeStruct((POOL_GROUPS, POOL_DIM, POOL_DIM), F32),
                   jax.ShapeDtypeStruct((POOL_GROUPS, 1, POOL_DIM), F32)],
        scratch_shapes=[pltpu.VMEM((lp + POOL_HALO, POOL_DIM), F32)] * 2,
        compiler_params=_params(("arbitrary",), VMEM_LIMIT),
    )(v, dyp, pool_w, pool_scale)


def _row_specs():
    head = _const((HEAD, D_MODEL))
    xrow = pl.BlockSpec((TM, D_MODEL), _xrow)
    full = pl.BlockSpec((TM, D_MODEL), lambda i: (i, 0))
    half = pl.BlockSpec((TM, D_SSM), lambda i: (i, 0))
    return head, xrow, full, half


def _in_proj(head, x, g1, w_in):
    n_tiles = (HEAD + x.shape[0]) // TM
    lp = n_tiles * TM

    def body(head_ref, x_ref, g_ref, w_ref, u_ref, v_ref):
        h0 = jnp.where(pl.program_id(0) == 0, head_ref[...], x_ref[...])
        proj = _dot(h0 * _rstd(h0) * g_ref[...], w_ref[...])
        u_ref[...] = proj[:, :D_SSM]
        v_ref[...] = proj[:, D_SSM:]

    head_s, xrow, _, half = _row_specs()
    return pl.pallas_call(
        body, name="in_proj", grid=(n_tiles,),
        in_specs=[head_s, xrow, _const((1, D_MODEL)), _const((D_MODEL, D_MODEL))],
        out_specs=[half, half], out_shape=[jax.ShapeDtypeStruct((lp, D_SSM), F32)] * 2,
        compiler_params=_params(("parallel",), VMEM_LIMIT),
    )(head, x, g1, w_in)


def _out_proj(head, x, ys, yp, gs, gp, w_out):
    lp = ys.shape[0]

    def body(head_ref, x_ref, ys_ref, yp_ref, gs_ref, gp_ref, w_ref, h1_ref):
        h0 = jnp.where(pl.program_id(0) == 0, head_ref[...], x_ref[...])
        ys_t, yp_t = ys_ref[...], yp_ref[...]
        ms = ys_t * _rstd(ys_t) * gs_ref[...]
        mp = yp_t * _rstd(yp_t) * gp_ref[...]
        h1_ref[...] = h0 + _dot(ms, w_ref[pl.ds(0, D_SSM), :]) + _dot(mp, w_ref[pl.ds(D_SSM, D_SSM), :])

    head_s, xrow, full, half = _row_specs()
    return pl.pallas_call(
        body, name="out_proj", grid=(lp // TM,),
        in_specs=[head_s, xrow, half, half, _const((1, D_SSM)), _const((1, D_SSM)), _const((D_MODEL, D_MODEL))],
        out_specs=full, out_shape=jax.ShapeDtypeStruct((lp, D_MODEL), F32),
        compiler_params=_params(("parallel",), VMEM_LIMIT),
    )(head, x, ys, yp, gs, gp, w_out)


def _load_weights(c_hbm, wd_hbm, c_vmem, wd_vmem, sems):
    @pl.when(pl.program_id(0) == 0)
    def _():
        copies = [pltpu.make_async_copy(c_hbm, c_vmem, sems.at[0]),
                  pltpu.make_async_copy(wd_hbm, wd_vmem, sems.at[1])]
        for cp in copies:
            cp.start()
        for cp in copies:
            cp.wait()


def _ffn_scratch():
    return [pltpu.VMEM((N_DEV, D_MODEL, 2 * FF_PAD), BF16), pltpu.VMEM((N_DEV, FF_PAD, D_MODEL), BF16),
            pltpu.SemaphoreType.DMA((2,))]


def _ffn_forward(h1, g2, c_all, wd_all):
    lp = h1.shape[0]

    def body(h1_ref, g_ref, c_hbm, wd_hbm, ab_ref, n2t_ref, h2_ref, c_vmem, wd_vmem, sems):
        _load_weights(c_hbm, wd_hbm, c_vmem, wd_vmem, sems)
        h1_t = h1_ref[...]
        n2_f = h1_t * _rstd(h1_t) * g_ref[...]
        n2 = n2_f.astype(BF16)
        n2t_ref[...] = n2_f.T.astype(BF16)
        acc = h1_t
        for j in range(N_DEV):
            ab = jnp.dot(n2, c_vmem[j], preferred_element_type=F32)
            a, b = ab[:, :FF_PAD], ab[:, FF_PAD:]
            ab_ref[:, pl.ds(j * 2 * FF_PAD, 2 * FF_PAD)] = ab.astype(BF16)
            acc = acc + _dot(a * _sigmoid(a) * b, wd_vmem[j])
        h2_ref[...] = acc

    _, _, full, _ = _row_specs()
    wide = pl.BlockSpec((TM, N_DEV * 2 * FF_PAD), lambda i: (i, 0))
    return pl.pallas_call(
        body, name="ffn_forward", grid=(lp // TM,),
        in_specs=[full, _const((1, D_MODEL)), ANY, ANY],
        out_specs=[wide, pl.BlockSpec((D_MODEL, TM), lambda i: (0, i)), full],
        out_shape=[jax.ShapeDtypeStruct((lp, N_DEV * 2 * FF_PAD), BF16),
                   jax.ShapeDtypeStruct((D_MODEL, lp), BF16), jax.ShapeDtypeStruct((lp, D_MODEL), F32)],
        scratch_shapes=_ffn_scratch(), compiler_params=_params(("arbitrary",), VMEM_LIMIT),
    )(h1, g2, c_all, wd_all)


def _ffn_backward(h2, target, h1, ab, gf, g2, c_all, wd_all):
    lp = h1.shape[0]

    def body(h2_ref, t_ref, h1_ref, ab_ref, gf_ref, g2_ref, c_hbm, wd_hbm,
             dh1_ref, dab_ref, dh2_ref, loss_ref, dgf_ref, dg2_ref, c_vmem, wd_vmem, sems):
        i = pl.program_id(0)
        _load_weights(c_hbm, wd_hbm, c_vmem, wd_vmem, sems)

        @pl.when(i == 0)
        def _():
            loss_ref[...] = jnp.zeros(loss_ref.shape, F32)
            dgf_ref[...] = jnp.zeros(dgf_ref.shape, F32)
            dg2_ref[...] = jnp.zeros(dg2_ref.shape, F32)

        h2_t = h2_ref[...]
        rf = _rstd(h2_t)
        xf = h2_t * rf
        diff = jnp.where(i == 0, 0.0, xf * gf_ref[...] - t_ref[...])
        loss_ref[...] += 0.5 * jnp.sum(diff * diff) / D_MODEL
        dh2, dgf = _rms_bwd(diff / D_MODEL, xf, rf, gf_ref[...])
        dgf_ref[...] += dgf
        dh2_b = dh2.astype(BF16)
        dh2_ref[...] = dh2_b

        dn2 = jnp.zeros((TM, D_MODEL), F32)
        for j in range(N_DEV):
            cols = pl.ds(j * 2 * FF_PAD, 2 * FF_PAD)
            dff = _dot_nt(dh2_b, wd_vmem[j])
            ab_t = ab_ref[:, cols].astype(F32)
            a, b = ab_t[:, :FF_PAD], ab_t[:, FF_PAD:]
            sg = _sigmoid(a)
            dab_ref[:, pl.ds(j * 2 * FF_PAD, FF_PAD)] = (dff * b * sg * (1.0 + a * (1.0 - sg))).astype(BF16)
            dab_ref[:, pl.ds(j * 2 * FF_PAD + FF_PAD, FF_PAD)] = (dff * a * sg).astype(BF16)
            dn2 = dn2 + _dot_nt(dab_ref[:, cols], c_vmem[j])

        h1_t = h1_ref[...]
        r2 = _rstd(h1_t)
        dx, dg2 = _rms_bwd(dn2, h1_t * r2, r2, g2_ref[...])
        dg2_ref[...] += dg2
        dh1_ref[...] = dh2 + dx

    _, xrow, full, _ = _row_specs()
    wide = pl.BlockSpec((TM, N_DEV * 2 * FF_PAD), lambda i: (i, 0))
    vec = _const((1, D_MODEL))
    return pl.pallas_call(
        body, name="ffn_backward", grid=(lp // TM,),
        in_specs=[full, xrow, full, wide, vec, vec, ANY, ANY],
        out_specs=[full, wide, full, _const((1, PACK_LANES)), vec, vec],
        out_shape=[jax.ShapeDtypeStruct((lp, D_MODEL), F32),
                   jax.ShapeDtypeStruct((lp, N_DEV * 2 * FF_PAD), BF16),
                   jax.ShapeDtypeStruct((lp, D_MODEL), BF16),
                   jax.ShapeDtypeStruct((1, PACK_LANES), F32),
                   jax.ShapeDtypeStruct((1, D_MODEL), F32), jax.ShapeDtypeStruct((1, D_MODEL), F32)],
        scratch_shapes=_ffn_scratch(), compiler_params=_params(("arbitrary",), VMEM_LIMIT),
    )(h2, target, h1, ab, gf, g2, c_all, wd_all)


def _ffn_wgrad(n2t, dh2, ab, dab):
    lp = dh2.shape[0]
    rows = lp // WGRAD_STEPS

    def body(n2t_ref, dh2_ref, ab_ref, dab_ref, dc_ref, dwd_ref, dc_acc, dwd_acc):
        i = pl.program_id(1)

        @pl.when(i == 0)
        def _():
            dc_acc[...] = jnp.zeros(dc_acc.shape, F32)
            dwd_acc[...] = jnp.zeros(dwd_acc.shape, F32)

        ab_t = ab_ref[...].astype(F32)
        a, b = ab_t[:, :FF_PAD], ab_t[:, FF_PAD:]
        dc_acc[...] += jnp.dot(n2t_ref[...], dab_ref[...], preferred_element_type=F32)
        dwd_acc[...] += _dot_tn(a * _sigmoid(a) * b, dh2_ref[...])

        @pl.when(i == pl.num_programs(1) - 1)
        def _():
            dc_ref[...] = dc_acc[...].astype(BF16)
            dwd_ref[...] = dwd_acc[...].astype(BF16)

    act_t = pl.BlockSpec((D_MODEL, rows), lambda j, i: (0, i))
    act = pl.BlockSpec((rows, D_MODEL), lambda j, i: (i, 0))
    shard = pl.BlockSpec((rows, 2 * FF_PAD), lambda j, i: (i, j))
    return pl.pallas_call(
        body, name="ffn_wgrad", grid=(N_DEV, WGRAD_STEPS),
        in_specs=[act_t, act, shard, shard],
        out_specs=[pl.BlockSpec((None, D_MODEL, 2 * FF_PAD), lambda j, i: (j, 0, 0)),
                   pl.BlockSpec((None, FF_PAD, D_MODEL), lambda j, i: (j, 0, 0))],
        out_shape=[jax.ShapeDtypeStruct((N_DEV, D_MODEL, 2 * FF_PAD), BF16),
                   jax.ShapeDtypeStruct((N_DEV, FF_PAD, D_MODEL), BF16)],
        scratch_shapes=[pltpu.VMEM((D_MODEL, 2 * FF_PAD), F32), pltpu.VMEM((FF_PAD, D_MODEL), F32)],
        compiler_params=_params(("parallel", "arbitrary"), VMEM_LIMIT),
    )(n2t, dh2, ab, dab)


def _out_proj_backward(dh1, ys, yp, gs, gp, w_out):
    lp = ys.shape[0]

    def body(dh1_ref, ys_ref, yp_ref, gs_ref, gp_ref, w_ref, dys_ref, dyp_ref, dgs_ref, dgp_ref, dw_out, dw_ref):
        @pl.when(pl.program_id(0) == 0)
        def _():
            dgs_ref[...] = jnp.zeros(dgs_ref.shape, F32)
            dgp_ref[...] = jnp.zeros(dgp_ref.shape, F32)
            dw_ref[...] = jnp.zeros(dw_ref.shape, F32)

        dh1_b = dh1_ref[...].astype(BF16)
        dmix = _dot_nt(dh1_b, w_ref[...])
        for y_ref, g_ref, dy_ref, dg_ref, lo in ((ys_ref, gs_ref, dys_ref, dgs_ref, 0),
                                                 (yp_ref, gp_ref, dyp_ref, dgp_ref, D_SSM)):
            y_t = y_ref[...]
            r = _rstd(y_t)
            xhat = y_t * r
            dy, dg = _rms_bwd(dmix[:, lo:lo + D_SSM], xhat, r, g_ref[...])
            dy_ref[...] = dy
            dg_ref[...] += dg
            dw_ref[pl.ds(lo, D_SSM), :] += _dot_tn(xhat * g_ref[...], dh1_b)

        @pl.when(pl.program_id(0) == pl.num_programs(0) - 1)
        def _():
            dw_out[...] = dw_ref[...].astype(BF16)

    _, _, full, half = _row_specs()
    vec = _const((1, D_SSM))
    return pl.pallas_call(
        body, name="out_proj_backward", grid=(lp // TM,),
        in_specs=[full, half, half, vec, vec, _const((D_MODEL, D_MODEL))],
        out_specs=[half, half, vec, vec, _const((D_MODEL, D_MODEL))],
        out_shape=[jax.ShapeDtypeStruct((lp, D_SSM), F32)] * 2 + [jax.ShapeDtypeStruct((1, D_SSM), F32)] * 2
        + [jax.ShapeDtypeStruct((D_MODEL, D_MODEL), BF16)],
        scratch_shapes=[pltpu.VMEM((D_MODEL, D_MODEL), F32)],
        compiler_params=_params(("arbitrary",), VMEM_LIMIT),
    )(dh1, ys, yp, gs, gp, w_out)


def _in_proj_backward(head, x, du, dv, dh1, g1, w_in):
    lp = du.shape[0]

    def body(head_ref, x_ref, du_ref, dv_ref, dh1_ref, g_ref, w_ref, dx_ref, dhead_ref, dg_ref, dw_out, dw_ref):
        i = pl.program_id(0)

        @pl.when(i == 0)
        def _():
            dg_ref[...] = jnp.zeros(dg_ref.shape, F32)
            dw_ref[...] = jnp.zeros(dw_ref.shape, F32)

        h0 = jnp.where(i == 0, head_ref[...], x_ref[...])
        r = _rstd(h0)
        xhat = h0 * r
        n1 = (xhat * g_ref[...]).astype(BF16)
        du_b, dv_b = du_ref[...].astype(BF16), dv_ref[...].astype(BF16)
        dn1 = _dot_nt(du_b, w_ref[:, pl.ds(0, D_SSM)]) + _dot_nt(dv_b, w_ref[:, pl.ds(D_SSM, D_SSM)])
        dx, dg = _rms_bwd(dn1, xhat, r, g_ref[...])
        dg_ref[...] += dg
        dh0 = dh1_ref[...] + dx
        dx_ref[...] = dh0

        @pl.when(i == 0)
        def _():
            dhead_ref[...] = dh0

        dw_ref[:, pl.ds(0, D_SSM)] += _dot_tn(n1, du_b)
        dw_ref[:, pl.ds(D_SSM, D_SSM)] += _dot_tn(n1, dv_b)

        @pl.when(i == pl.num_programs(0) - 1)
        def _():
            dw_out[...] = dw_ref[...].astype(BF16)

    head_s, xrow, full, half = _row_specs()
    vec = _const((1, D_MODEL))
    mat = _const((D_MODEL, D_MODEL))
    return pl.pallas_call(
        body, name="in_proj_backward", grid=(lp // TM,),
        in_specs=[head_s, xrow, half, half, full, vec, mat],
        out_specs=[xrow, head_s, vec, mat],
        out_shape=[jax.ShapeDtypeStruct(x.shape, F32), jax.ShapeDtypeStruct((HEAD, D_MODEL), F32),
                   jax.ShapeDtypeStruct((1, D_MODEL), F32), jax.ShapeDtypeStruct((D_MODEL, D_MODEL), BF16)],
        scratch_shapes=[pltpu.VMEM((D_MODEL, D_MODEL), F32)],
        compiler_params=_params(("arbitrary",), VMEM_LIMIT),
    )(head, x, du, dv, dh1, g1, w_in)


def _permute_rows(a):
    lp, n = a.shape
    return a.reshape(SUBLANES, lp // SUBLANES, n).transpose(1, 0, 2).reshape(lp, n)


def _unpermute_rows(a):
    lp, n = a.shape
    return a.reshape(lp // SUBLANES, SUBLANES, n).transpose(1, 0, 2).reshape(lp, n)


def _block_diag(blocks):
    _, r, c = blocks.shape
    b = blocks.reshape(N_COL, GROUPS_PER_COL, r, 1, c)
    eye = jnp.eye(GROUPS_PER_COL, dtype=blocks.dtype).reshape(1, GROUPS_PER_COL, 1, GROUPS_PER_COL, 1)
    return (b * eye).reshape(N_COL, GROUPS_PER_COL * r, GROUPS_PER_COL * c)


def _block_diag_extract(mats, r, c):
    m = mats.reshape(N_COL, GROUPS_PER_COL, r, GROUPS_PER_COL, c)
    eye = jnp.eye(GROUPS_PER_COL, dtype=mats.dtype).reshape(1, GROUPS_PER_COL, 1, GROUPS_PER_COL, 1)
    return jnp.sum(m * eye, axis=3).reshape(SSM_GROUPS, r, c)


def _pack(parts, dtype):
    rows = []
    for p in parts:
        flat = p.reshape(-1).astype(dtype)
        pad = (-flat.shape[0]) % PACK_UNIT
        rows.append(jnp.pad(flat, (0, pad)).reshape(-1, PACK_LANES))
    n_rows = sum(r.shape[0] for r in rows)
    if n_rows % 16:
        rows.append(jnp.zeros((8, PACK_LANES), dtype))
    return jnp.concatenate(rows, axis=0)


def _as2d(a):
    return a.reshape(-1, a.shape[-1])


def _unpack(packed, shapes):
    out, row = [], 0
    for shape in shapes:
        size = 1
        for s in shape:
            size *= s
        n_rows = -(-size // PACK_UNIT) * 8
        out.append(packed[row:row + n_rows].reshape(-1)[:size].reshape(shape))
        row += n_rows
    return out


def _pad_cols(a):
    return jnp.pad(a, ((0, 0), (0, FF_PAD - FF_SHARD)))


def _pad_rows(a):
    return jnp.pad(a, ((0, FF_PAD - FF_SHARD), (0, 0)))


def _gate_up(gate, up):
    return jnp.concatenate([_pad_cols(gate), _pad_cols(up)], axis=1)


def kernel(x, meta_tokens, norm1_g, w_in, ssm_lambda_re, ssm_lambda_im, ssm_log_step, ssm_b_re, ssm_b_im, ssm_c_re, ssm_c_im, ssm_d, ssm_glu_w, ssm_glu_b, ssm_norm_g, pool_w, pool_scale, pool_norm_g, w_out, norm2_g, w_gate, w_up, w_down, final_norm_g, loss_target, m_meta_tokens, m_norm1_g, m_w_in, m_ssm_lambda_re, m_ssm_lambda_im, m_ssm_log_step, m_ssm_b_re, m_ssm_b_im, m_ssm_c_re, m_ssm_c_im, m_ssm_d, m_ssm_glu_w, m_ssm_glu_b, m_ssm_norm_g, m_pool_w, m_pool_scale, m_pool_norm_g, m_w_out, m_norm2_g, m_w_gate, m_w_up, m_w_down, m_final_norm_g, v_meta_tokens, v_norm1_g, v_w_in, v_ssm_lambda_re, v_ssm_lambda_im, v_ssm_log_step, v_ssm_b_re, v_ssm_b_im, v_ssm_c_re, v_ssm_c_im, v_ssm_d, v_ssm_glu_w, v_ssm_glu_b, v_ssm_norm_g, v_pool_w, v_pool_scale, v_pool_norm_g, v_w_out, v_norm2_g, v_w_gate, v_w_up, v_w_down, v_final_norm_g):
    given = dict(locals())
    weights = {n: given[n] for n in WEIGHT_NAMES}
    n_meta = meta_tokens.shape[0]
    me = 4 * lax.axis_index("x") + 2 * lax.axis_index("y") + lax.axis_index("c")

    shard_rows = w_in.shape[1]
    first = [w_in[0].astype(BF16), meta_tokens]
    first_make = _push_copies(ALL_PEERS, [False, False])
    first_x, first_token = _split_start(first + [_landing(s, False) for s in first], first_make,
                                        2 * len(ALL_PEERS), "gather_w_in_start")

    xs = x[0]
    tgt = loss_target[0]
    first_row = HEAD - n_meta
    g1, g2, gf = norm1_g, norm2_g, final_norm_g.reshape(1, D_MODEL)
    gs, gp = ssm_norm_g, pool_norm_g

    lam_re, lam_im = ssm_lambda_re[0] + first_token[:1, :1], ssm_lambda_im[0]
    log_step = ssm_log_step[0].reshape(SSM_GROUPS, 1)
    b_re = ssm_b_re[0].reshape(SSM_GROUPS * SSM_STATE, SSM_GROUP)
    b_im = ssm_b_im[0].reshape(SSM_GROUPS * SSM_STATE, SSM_GROUP)
    abr, abi, zr, zi = _s5_disc_a(lam_re, lam_im, log_step)
    zr_col, zi_col = zr.reshape(-1, 1), zi.reshape(-1, 1)
    bbr, bbi = _s5_disc_b(zr_col, zi_col, b_re, b_im)
    to_bd = lambda b: _block_diag(b.reshape(SSM_GROUPS, SSM_STATE, SSM_GROUP).transpose(0, 2, 1)).astype(BF16)
    bre_bd, bim_bd = to_bd(bbr), to_bd(bbi)
    cret_bd = _block_diag(ssm_c_re[0].transpose(0, 2, 1)).astype(BF16)
    cimt_bd = _block_diag(ssm_c_im[0].transpose(0, 2, 1)).astype(BF16)
    glu_bd = _block_diag(ssm_glu_w[0]).astype(BF16)
    s5_consts = (abr.reshape(1, -1), abi.reshape(1, -1), bre_bd, bim_bd, cret_bd, cimt_bd,
                 ssm_d[0].reshape(1, D_SSM), glu_bd, ssm_glu_b[0].reshape(1, D_SSM))
    pool_sc = pool_scale[0].reshape(POOL_GROUPS, 1, POOL_DIM)

    (_, _, w_in_all, meta_all), first_done = _split_wait(first_x, first_make, bre_bd, "gather_w_in_wait")
    w_in_all = w_in_all.reshape(D_MODEL, D_MODEL)
    meta_full = meta_all.transpose(1, 0, 2).reshape(n_meta, D_MODEL)
    head = jnp.concatenate([jnp.zeros((HEAD - n_meta, D_MODEL), F32), meta_full], axis=0)
    shards = [(w_out[0] + first_done[:1, :1]).astype(BF16), _pad_rows(w_down[0]).astype(BF16),
              _gate_up(w_gate[0], w_up[0]).astype(BF16)]
    gather_make = _push_copies((SIBLING,) + CHIP_PEERS, [False, False, False])
    gather, gather_token = _split_start(shards + [_landing(s, False) for s in shards], gather_make,
                                        3 * (1 + len(CHIP_PEERS)), "gather_start")

    u, v = _in_proj(head, xs, g1 + gather_token[:1, :1], w_in_all)
    u_p = _permute_rows(u)
    ys_p = _s5_forward(u_p, *s5_consts)
    landed, _ = _split_wait(gather, gather_make, ys_p, "gather_wait")
    forward_make = _forward_copies(3)
    forward, forward_token = _split_start(list(landed[3:]), forward_make, 3 * len(CHIP_PEERS), "gather_forward_start")
    ys = _unpermute_rows(ys_p)
    yp = _pool_forward(v, pool_w[0], pool_sc + forward_token[:1, :1], first_row)
    (w_out_all, wd_all, c_all), _ = _split_wait(forward, forward_make, yp, "gather_forward_wait")
    w_out_all = w_out_all.reshape(D_MODEL, D_MODEL)
    h1 = _out_proj(head, xs, ys, yp, gs, gp, w_out_all)
    ab, n2t, h2 = _ffn_forward(h1, g2, c_all, wd_all)

    dh1, dab, dh2, loss_part, d_gf, d_g2 = _ffn_backward(h2, tgt, h1, ab, gf, g2, c_all, wd_all)
    d_c, d_wd = _ffn_wgrad(n2t, dh2, ab, dab)
    ffn_make = _push_copies(ALL_PEERS, [True, True])
    ffn_x, ffn_token = _split_start([d_c, d_wd, _landing(d_c, True), _landing(d_wd, True)], ffn_make,
                                    2 * len(ALL_PEERS), "ffn_grad_start")
    dys, dyp, d_gs, d_gp, d_wout = _out_proj_backward(dh1, ys, yp, gs + ffn_token[:1, :1], gp, w_out_all)
    dv, d_pool_w, d_pool_sc = _pool_backward(v, dyp, pool_w[0], pool_sc, first_row)
    (du_p, d_ar, d_ai, d_bre_bd, d_bim_bd, d_cre_bd, d_cim_bd, d_d, d_glu_bd, d_glub) = _s5_backward(
        u_p, _permute_rows(dys), *s5_consts)
    du = _unpermute_rows(du_p)

    from_bd = lambda m: _block_diag_extract(m, SSM_GROUP, SSM_STATE).transpose(0, 2, 1)
    d_bbr = from_bd(d_bre_bd).reshape(SSM_GROUPS * SSM_STATE, SSM_GROUP)
    d_bbi = from_bd(d_bim_bd).reshape(SSM_GROUPS * SSM_STATE, SSM_GROUP)
    d_zr, d_zi, d_b_re, d_b_im = _s5_disc_b_bwd(zr_col, zi_col, b_re, b_im, d_bbr, d_bbi)
    d_lam_re, d_lam_im, d_log_step = _s5_disc_a_bwd(
        lam_re, lam_im, log_step,
        (d_ar.reshape(SSM_GROUPS, SSM_STATE), d_ai.reshape(SSM_GROUPS, SSM_STATE),
         d_zr.reshape(SSM_GROUPS, SSM_STATE), d_zi.reshape(SSM_GROUPS, SSM_STATE)))
    small_grads = {
        "ssm_lambda_re": d_lam_re, "ssm_lambda_im": d_lam_im, "ssm_log_step": d_log_step,
        "ssm_b_re": d_b_re, "ssm_b_im": d_b_im,
        "ssm_c_re": _block_diag_extract(d_cre_bd, SSM_GROUP, SSM_STATE),
        "ssm_c_im": _block_diag_extract(d_cim_bd, SSM_GROUP, SSM_STATE),
        "ssm_d": d_d, "ssm_glu_w": _block_diag_extract(d_glu_bd, SSM_GROUP, SSM_GROUP), "ssm_glu_b": d_glub,
        "ssm_norm_g": d_gs, "pool_w": d_pool_w, "pool_scale": d_pool_sc, "pool_norm_g": d_gp,
        "norm2_g": d_g2, "final_norm_g": d_gf,
    }

    early_names = SMALL_NAMES[1:-1]
    early_pack = _pack([small_grads[n] for n in early_names], BF16)
    d_wout = d_wout.reshape(N_DEV, shard_rows, D_MODEL)
    early_make = _push_copies(ALL_PEERS, [True, False])
    early_x, early_token = _split_start([d_wout, early_pack, _landing(d_wout, True), _landing(early_pack, False)],
                                        early_make, 2 * len(ALL_PEERS), "early_grad_start")
    d_x, d_head, d_g1, d_win = _in_proj_backward(head, xs, du, dv, dh1, g1 + early_token[:1, :1], w_in_all)
    (_, _, r_c, r_wd), _ = _split_wait(ffn_x, ffn_make, d_g1, "ffn_grad_wait")
    (_, _, r_wout, r_early), _ = _split_wait(early_x, early_make, d_g1, "early_grad_wait")
    d_win = d_win.reshape(N_DEV, shard_rows, D_MODEL)
    late_pack = _pack([d_g1, d_gf, d_head[first_row:], loss_part], F32)
    late_make = _push_copies(ALL_PEERS, [True, False])
    late_x, late_token = _split_start([d_win, late_pack, _landing(d_win, True), _landing(late_pack, False)],
                                      late_make, 2 * len(ALL_PEERS), "late_grad_start")

    results = {}
    res_c = _adamw(r_c, _gate_up(w_gate[0], w_up[0]) + late_token[:1, :1], _gate_up(m_w_gate[0], m_w_up[0]),
                   _gate_up(v_w_gate[0], v_w_up[0]), 128, "adamw_gate_up")
    results["w_gate"] = [r[:, :FF_SHARD] for r in res_c]
    results["w_up"] = [r[:, FF_PAD:FF_PAD + FF_SHARD] for r in res_c]
    res_wd = _adamw(r_wd, _pad_rows(w_down[0]), _pad_rows(m_w_down[0]), _pad_rows(v_w_down[0]), 128, "adamw_w_down")
    results["w_down"] = [r[:FF_SHARD] for r in res_wd]
    results["w_out"] = _adamw(r_wout, w_out[0], m_w_out[0], v_w_out[0], shard_rows, "adamw_w_out")
    (_, _, r_win, r_late), _ = _split_wait(late_x, late_make, res_c[1], "late_grad_wait")
    results["w_in"] = _adamw(r_win, w_in[0], m_w_in[0], v_w_in[0], shard_rows, "adamw_w_in")

    sum_early, sum_late = _reduce_slots([r_early, r_late], "small_grad_sums")
    g_small = _unpack(sum_early, [weights[n].shape for n in early_names])
    g_norm1, g_final, g_meta_all, loss_row = _unpack(
        sum_late, [norm1_g.shape, final_norm_g.shape, (n_meta, D_MODEL), (1, PACK_LANES)])
    g_small = [g_norm1] + g_small + [g_final]
    small_2d = lambda prefix: [_as2d(given[prefix + n]) for n in SMALL_NAMES]
    res_small = _adamw_many([_as2d(g) for g in g_small], small_2d(""), small_2d("m_"), small_2d("v_"), "adamw_small")
    for idx, n in enumerate(SMALL_NAMES):
        results[n] = [g_small[idx]] + [part[idx] for part in res_small]
    shard_cols = meta_tokens.shape[1]
    g_meta = lax.dynamic_slice_in_dim(g_meta_all, me * shard_cols, shard_cols, axis=1)
    results["meta_tokens"] = _adamw(g_meta[None], meta_tokens, m_meta_tokens, v_meta_tokens, n_meta, "adamw_meta")

    out = [loss_row[0, 0], d_x[None]]
    for part in range(4):
        for n in WEIGHT_NAMES:
            out.append(results[n][part].reshape(weights[n].shape))
    return tuple(out)
```

```python
import functools

import jax
import jax.numpy as jnp
from jax import lax
from jax.experimental import pallas as pl
from jax.experimental.pallas import tpu as pltpu

F32 = jnp.float32
BF16 = jnp.bfloat16

N_DEV = 8
D_MODEL = 1024
D_SSM = 512
SSM_GROUP = 16
SSM_STATE = 64
SSM_GROUPS = 32
POOL_GROUPS = 4
POOL_DIM = 128
COL_U = 128
COL_S = 512
N_COL = D_SSM // COL_U
GROUPS_PER_COL = COL_U // SSM_GROUP
FF_SHARD = 352
FF_PAD = 384
TM = 256
WGRAD_STEPS = 2
HEAD = TM
SUBLANES = 8
SCAN_UNROLL = 4
POOL_HALO = 128
EPS = 1e-6
STEP_FLOOR = -1e-4
VMEM_LIMIT = 60 * 1024 * 1024

ADAM_LR = 0.001
ADAM_B1 = 0.9
ADAM_B2 = 0.999
ADAM_EPS = 1e-08
ADAM_WD = 0.01
ADAM_STEP = 10

MESH_ID = pl.DeviceIdType.MESH
ANY = pl.BlockSpec(memory_space=pl.ANY)

SMALL_NAMES = ("norm1_g", "ssm_lambda_re", "ssm_lambda_im", "ssm_log_step", "ssm_b_re", "ssm_b_im",
               "ssm_c_re", "ssm_c_im", "ssm_d", "ssm_glu_w", "ssm_glu_b", "ssm_norm_g", "pool_w",
               "pool_scale", "pool_norm_g", "norm2_g", "final_norm_g")
WEIGHT_NAMES = ("meta_tokens", "norm1_g", "w_in", "ssm_lambda_re", "ssm_lambda_im", "ssm_log_step",
                "ssm_b_re", "ssm_b_im", "ssm_c_re", "ssm_c_im", "ssm_d", "ssm_glu_w", "ssm_glu_b",
                "ssm_norm_g", "pool_w", "pool_scale", "pool_norm_g", "w_out", "norm2_g", "w_gate",
                "w_up", "w_down", "final_norm_g")
PACK_LANES = 128
PACK_UNIT = 8 * PACK_LANES


def _dot(a, b):
    return jnp.dot(a.astype(BF16), b.astype(BF16), preferred_element_type=F32)


def _dot_nt(a, b):
    return lax.dot_general(a.astype(BF16), b.astype(BF16), (((1,), (1,)), ((), ())), preferred_element_type=F32)


def _dot_tn(a, b):
    return lax.dot_general(a.astype(BF16), b.astype(BF16), (((0,), (0,)), ((), ())), preferred_element_type=F32)


def _sigmoid(x):
    return 1.0 / (1.0 + jnp.exp(-x))


def _rstd(x):
    return lax.rsqrt(jnp.mean(x * x, axis=-1, keepdims=True) + EPS)


def _rms_bwd(dy, xhat, r, g):
    dxh = dy * g
    dx = r * (dxh - xhat * jnp.mean(dxh * xhat, axis=-1, keepdims=True))
    return dx, jnp.sum(dy * xhat, axis=0, keepdims=True)


def _params(sem, vmem=None):
    return pltpu.CompilerParams(dimension_semantics=sem, vmem_limit_bytes=vmem)


def _const(shape):
    return pl.BlockSpec(shape, lambda *_: (0,) * len(shape))


def _xrow(i):
    return (jnp.maximum(i - 1, 0), 0)


HBM = pl.BlockSpec(memory_space=pltpu.HBM)
SEM = pl.BlockSpec(memory_space=pltpu.SEMAPHORE)
EFFECT = pltpu.SideEffectType.DATAFLOW_SIDE_EFFECTING
ALL_PEERS = tuple(range(1, N_DEV))
SIBLING = 1
CHIP_PEERS = (2, 4, 6)


def _me():
    return 4 * lax.axis_index("x") + 2 * lax.axis_index("y") + lax.axis_index("c")


def _peer(k):
    x, y, c = lax.axis_index("x"), lax.axis_index("y"), lax.axis_index("c")
    px = 1 - x if k & 4 else x
    py = 1 - y if k & 2 else y
    pc = 1 - c if k & 1 else c
    return (px, py, pc), 4 * px + 2 * py + pc


def _landing(arr, scatter):
    if scatter:
        own = lax.dynamic_index_in_dim(arr, _me(), 0, keepdims=False)
    else:
        own = arr
    return lax.dynamic_update_index_in_dim(lax.empty((N_DEV,) + own.shape, arr.dtype), own, _me(), 0)


def _push_copies(peers, scatter):
    n_arr = len(scatter)

    def make(refs, send_sems, recv_sems):
        copies = []
        for a in range(n_arr):
            for i, k in enumerate(peers):
                peer_id, peer = _peer(k)
                sem = a * len(peers) + i
                copies.append(pltpu.make_async_remote_copy(
                    src_ref=refs[a].at[peer] if scatter[a] else refs[a], dst_ref=refs[n_arr + a].at[_me()],
                    send_sem=send_sems.at[sem], recv_sem=recv_sems.at[sem],
                    device_id=peer_id, device_id_type=MESH_ID))
        return copies
    return make


def _forward_copies(n_arr):
    def make(refs, send_sems, recv_sems):
        copies = []
        sibling_id, _ = _peer(SIBLING)
        for a in range(n_arr):
            for i, k in enumerate(CHIP_PEERS):
                slot = refs[a].at[_peer(k)[1]]
                sem = a * len(CHIP_PEERS) + i
                copies.append(pltpu.make_async_remote_copy(
                    src_ref=slot, dst_ref=slot, send_sem=send_sems.at[sem], recv_sem=recv_sems.at[sem],
                    device_id=sibling_id, device_id_type=MESH_ID))
        return copies
    return make


def _split_start(operands, make, n_sem, name):
    n_op = len(operands)

    def body(*refs):
        for cp in make(refs[:n_op], refs[n_op], refs[n_op + 1]):
            cp.start()
        refs[-1][...] = jnp.zeros(refs[-1].shape, F32)

    out = pl.pallas_call(
        body, name=name,
        out_shape=(pltpu.SemaphoreType.DMA((n_sem,)), pltpu.SemaphoreType.DMA((n_sem,)),
                   *[pltpu.HBM(t.shape, t.dtype) for t in operands], jax.ShapeDtypeStruct((8, PACK_LANES), F32)),
        in_specs=[HBM] * n_op, out_specs=(SEM, SEM, *[HBM] * n_op, pl.BlockSpec(memory_space=pltpu.VMEM)),
        input_output_aliases={i: 2 + i for i in range(n_op)},
        compiler_params=pltpu.CompilerParams(has_side_effects=EFFECT),
    )(*[pltpu.with_memory_space_constraint(t, pltpu.HBM) for t in operands])
    return out[:-1], out[-1]


def _split_wait(started, make, after, name):
    send_sems, recv_sems, thru = started[0], started[1], started[2:]
    n_op = len(thru)

    def body(*refs):
        for cp in make(refs[:n_op], refs[n_op], refs[n_op + 1]):
            cp.wait_send()
            cp.wait_recv()
        refs[-1][...] = jnp.zeros(refs[-1].shape, F32)

    out = pl.pallas_call(
        body, name=name,
        out_shape=(*[pltpu.HBM(t.shape, t.dtype) for t in thru], jax.ShapeDtypeStruct((8, PACK_LANES), F32)),
        in_specs=[HBM] * n_op + [SEM, SEM, ANY], out_specs=(*[HBM] * n_op, pl.BlockSpec(memory_space=pltpu.VMEM)),
        input_output_aliases={i: i for i in range(n_op)},
        compiler_params=pltpu.CompilerParams(has_side_effects=EFFECT),
    )(*thru, send_sems, recv_sems, after)
    return out[:-1], out[-1]


def _adamw_math(g, w, m, v):
    nm = ADAM_B1 * m + (1.0 - ADAM_B1) * g
    nv = ADAM_B2 * v + (1.0 - ADAM_B2) * (g * g)
    m_hat = nm / (1.0 - ADAM_B1 ** ADAM_STEP)
    v_hat = nv / (1.0 - ADAM_B2 ** ADAM_STEP)
    return -ADAM_LR * (m_hat / (jnp.sqrt(v_hat) + ADAM_EPS) + ADAM_WD * w), nm, nv


def _sum_slots(s_ref):
    g = s_ref[0].astype(F32)
    for s in range(1, s_ref.shape[0]):
        g = g + s_ref[s].astype(F32)
    return g


def _adamw(slots, w, m, v, tile_rows, name):
    n, rows, cols = slots.shape

    def body(s_ref, w_ref, m_ref, v_ref, g_ref, d_ref, nm_ref, nv_ref):
        g = _sum_slots(s_ref)
        g_ref[...] = g
        d_ref[...], nm_ref[...], nv_ref[...] = _adamw_math(g, w_ref[...], m_ref[...], v_ref[...])

    tile = pl.BlockSpec((tile_rows, cols), lambda i: (i, 0))
    return pl.pallas_call(
        body, name=name, grid=(rows // tile_rows,),
        in_specs=[pl.BlockSpec((n, tile_rows, cols), lambda i: (0, i, 0)), tile, tile, tile],
        out_specs=[tile] * 4, out_shape=[jax.ShapeDtypeStruct((rows, cols), F32)] * 4,
        compiler_params=_params(("parallel",), VMEM_LIMIT),
    )(slots, w, m, v)


def _adamw_transposed(slots, w_t, m_t, v_t, name):
    n, rows, cols = slots.shape
    lane_tile = 128

    def body(s_ref, w_ref, m_ref, v_ref, g_ref, d_ref, nm_ref, nv_ref):
        g = _sum_slots(s_ref).T
        g_ref[...] = g
        d_ref[...], nm_ref[...], nv_ref[...] = _adamw_math(g, w_ref[...], m_ref[...], v_ref[...])

    tile = pl.BlockSpec((lane_tile, rows), lambda i: (i, 0))
    return pl.pallas_call(
        body, name=name, grid=(cols // lane_tile,),
        in_specs=[pl.BlockSpec((n, rows, lane_tile), lambda i: (0, 0, i)), tile, tile, tile],
        out_specs=[tile] * 4, out_shape=[jax.ShapeDtypeStruct((cols, rows), F32)] * 4,
        compiler_params=_params(("parallel",), VMEM_LIMIT),
    )(slots, w_t, m_t, v_t)


def _reduce_slots(slot_arrays, name):
    def body(*refs):
        n_arr = len(refs) // 2
        for s_ref, o_ref in zip(refs[:n_arr], refs[n_arr:]):
            o_ref[...] = _sum_slots(s_ref)
    return pl.pallas_call(
        body, name=name, out_shape=[jax.ShapeDtypeStruct(s.shape[1:], F32) for s in slot_arrays],
        compiler_params=_params(None, VMEM_LIMIT))(*slot_arrays)


def _adamw_many(grads, ws, ms, vs, name):
    n = len(grads)

    def body(*refs):
        ins, outs = refs[:4 * n], refs[4 * n:]
        for i in range(n):
            g, w, m, v = (ins[j * n + i][...] for j in range(4))
            outs[i][...], outs[n + i][...], outs[2 * n + i][...] = _adamw_math(g, w, m, v)

    out = pl.pallas_call(
        body, name=name, out_shape=[jax.ShapeDtypeStruct(w.shape, F32) for w in ws] * 3,
        compiler_params=_params(None, VMEM_LIMIT))(*grads, *ws, *ms, *vs)
    return out[:n], out[n:2 * n], out[2 * n:]


def _disc_a(lam_re, lam_im, log_step):
    lr = jnp.minimum(lam_re, STEP_FLOOR)
    step = jnp.exp(log_step)
    mag = jnp.exp(lr * step)
    ang = lam_im * step
    abr = mag * jnp.cos(ang)
    abi = mag * jnp.sin(ang)
    nr = abr - 1.0
    den = lr * lr + lam_im * lam_im
    cr = (nr * lr + abi * lam_im) / den
    ci = (abi * lr - nr * lam_im) / den
    return abr, abi, cr, ci


def _disc_b(cr, ci, b_re, b_im):
    return cr * b_re - ci * b_im, cr * b_im + ci * b_re


def _s5_disc_a(lam_re, lam_im, log_step):
    def body(lr_ref, li_ref, ls_ref, *outs):
        for o, val in zip(outs, _disc_a(lr_ref[...], li_ref[...], ls_ref[...])):
            o[...] = val
    return pl.pallas_call(body, name="s5_disc_a", out_shape=[jax.ShapeDtypeStruct(lam_re.shape, F32)] * 4)(
        lam_re, lam_im, log_step)


def _s5_disc_a_bwd(lam_re, lam_im, log_step, cts):
    def body(lr_ref, li_ref, ls_ref, c0, c1, c2, c3, dlr_ref, dli_ref, dls_ref):
        _, vjp = jax.vjp(_disc_a, lr_ref[...], li_ref[...], ls_ref[...])
        dlr, dli, dls = vjp((c0[...], c1[...], c2[...], c3[...]))
        dlr_ref[...] = dlr
        dli_ref[...] = dli
        dls_ref[...] = dls
    return pl.pallas_call(
        body, name="s5_disc_a_bwd",
        out_shape=[jax.ShapeDtypeStruct(lam_re.shape, F32), jax.ShapeDtypeStruct(lam_re.shape, F32),
                   jax.ShapeDtypeStruct(log_step.shape, F32)])(lam_re, lam_im, log_step, *cts)


def _s5_disc_b(cr, ci, b_re, b_im):
    def body(cr_ref, ci_ref, br_ref, bi_ref, o_re, o_im):
        o_re[...], o_im[...] = _disc_b(cr_ref[...], ci_ref[...], br_ref[...], bi_ref[...])
    return pl.pallas_call(body, name="s5_disc_b", out_shape=[jax.ShapeDtypeStruct(b_re.shape, F32)] * 2)(
        cr, ci, b_re, b_im)


def _s5_disc_b_bwd(cr, ci, b_re, b_im, d_re, d_im):
    def body(cr_ref, ci_ref, br_ref, bi_ref, dr_ref, di_ref, dcr_ref, dci_ref, dbr_ref, dbi_ref):
        _, vjp = jax.vjp(_disc_b, cr_ref[...], ci_ref[...], br_ref[...], bi_ref[...])
        dcr_ref[...], dci_ref[...], dbr_ref[...], dbi_ref[...] = vjp((dr_ref[...], di_ref[...]))
    return pl.pallas_call(
        body, name="s5_disc_b_bwd",
        out_shape=[jax.ShapeDtypeStruct(cr.shape, F32)] * 2 + [jax.ShapeDtypeStruct(b_re.shape, F32)] * 2)(
            cr, ci, b_re, b_im, d_re, d_im)


def _cmul(ar, ai, br, bi):
    return ar * br - ai * bi, ar * bi + ai * br


def _cpow(ar, ai, n):
    rr, ri = jnp.ones_like(ar), jnp.zeros_like(ai)
    while n:
        if n & 1:
            rr, ri = _cmul(rr, ri, ar, ai)
        n >>= 1
        if n:
            ar, ai = _cmul(ar, ai, ar, ai)
    return rr, ri


def _tile_rows(i):
    if isinstance(i, int):
        return pl.ds(i * SUBLANES, SUBLANES)
    return pl.ds(pl.multiple_of(i * SUBLANES, SUBLANES), SUBLANES)


def _segment_scan(z_re, z_im, ar, ai, lseg, reverse, visit=None):
    shape = (SUBLANES, z_re.shape[1])
    half = lseg // 2
    arb = jnp.broadcast_to(ar, shape)
    aib = jnp.broadcast_to(ai, shape)
    zero = jnp.zeros(shape, F32)
    row = lax.broadcasted_iota(jnp.int32, shape, 0)

    def tiles(k):
        return (lseg - 1 - k, half - 1 - k) if reverse else (k, half + k)

    def advance(tile, sr, si):
        rows = _tile_rows(tile)
        nr, ni = _cmul(arb, aib, sr, si)
        return rows, nr + z_re[rows, :], ni + z_im[rows, :]

    def first_pass(k, carry):
        ta, tb = tiles(k)
        return advance(ta, carry[0], carry[1])[1:] + advance(tb, carry[2], carry[3])[1:]

    def unrolled(step):
        def body(it, carry):
            for j in range(SCAN_UNROLL):
                carry = step(it * SCAN_UNROLL + j, carry)
            return carry
        return body

    n_iter = half // SCAN_UNROLL
    fa_r, fa_i, fb_r, fb_i = lax.fori_loop(0, n_iter, unrolled(first_pass), (zero,) * 4)
    hr, hi = _cpow(arb, aib, half)
    pr, pi = _cmul(hr, hi, hr, hi)
    fr, fi = _cmul(hr, hi, fa_r, fa_i)
    fr, fi = fr + fb_r, fi + fb_i
    cr, ci = zero, zero
    for _ in range(SUBLANES - 1):
        tr, ti = _cmul(pr, pi, cr, ci)
        tr, ti = tr + fr, ti + fi
        if reverse:
            cr = jnp.where(row == SUBLANES - 1, 0.0, pltpu.roll(tr, SUBLANES - 1, 0))
            ci = jnp.where(row == SUBLANES - 1, 0.0, pltpu.roll(ti, SUBLANES - 1, 0))
        else:
            cr = jnp.where(row == 0, 0.0, pltpu.roll(tr, 1, 0))
            ci = jnp.where(row == 0, 0.0, pltpu.roll(ti, 1, 0))

    br, bi = _cmul(hr, hi, cr, ci)
    br, bi = br + fa_r, bi + fa_i

    def second_pass(k, carry, b_is_tile0=False):
        states, acc = list(carry[:4]), carry[4]
        for chain, tile in enumerate(tiles(k)):
            rows, nr, ni = advance(tile, states[2 * chain], states[2 * chain + 1])
            z_re[rows, :] = nr
            z_im[rows, :] = ni
            states[2 * chain], states[2 * chain + 1] = nr, ni
            if visit is not None:
                acc = visit(tile, nr, ni, acc, chain == 1 and b_is_tile0)
        return (*states, acc)

    acc0 = (zero, zero) if visit is not None else 0
    carry = lax.fori_loop(0, n_iter - 1, unrolled(second_pass), (cr, ci, br, bi, acc0))
    for k in range(half - SCAN_UNROLL, half - 1):
        carry = second_pass(k, carry)
    return second_pass(half - 1, carry, b_is_tile0=reverse)[4]


def _gelu(y):
    c = 0.7978845608028654
    return 0.5 * y * (1.0 + jnp.tanh(c * (y + 0.044715 * y * y * y)))


def _gelu_grad(y):
    c = 0.7978845608028654
    th = jnp.tanh(c * (y + 0.044715 * y * y * y))
    return 0.5 * (1.0 + th) + 0.5 * y * (1.0 - th * th) * c * (1.0 + 3.0 * 0.044715 * y * y)


def _s5_specs(lp):
    col_u = pl.BlockSpec((lp, COL_U), lambda j: (0, j))
    row_u = pl.BlockSpec((1, COL_U), lambda j: (0, j))
    row_s = pl.BlockSpec((1, COL_S), lambda j: (0, j))
    bc_blk = pl.BlockSpec((GROUPS_PER_COL, SSM_GROUP, SSM_STATE), lambda j: (j, 0, 0))
    glu_blk = pl.BlockSpec((GROUPS_PER_COL, SSM_GROUP, SSM_GROUP), lambda j: (j, 0, 0))
    return col_u, row_u, row_s, bc_blk, glu_blk


def _s5_block_diag_scratch():
    return ([pltpu.VMEM((COL_U, COL_S), BF16)] * 4 + [pltpu.VMEM((COL_U, COL_U), BF16)]
            + [pltpu.VMEM((COL_U, COL_S), F32)])


def _fill_block_diag(bd_ref, blocks_ref, stage):
    r, c = blocks_ref.shape[1:]
    stage[...] = jnp.zeros(stage.shape, F32)
    for gl in range(GROUPS_PER_COL):
        stage[pl.ds(gl * r, r), pl.ds(gl * c, c)] = blocks_ref[gl]
    bd_ref[...] = stage[:, :GROUPS_PER_COL * c].astype(BF16)


def _take_block_diag(out_ref, mat):
    r, c = out_ref.shape[1:]
    for gl in range(GROUPS_PER_COL):
        out_ref[gl] = mat[gl * r:(gl + 1) * r, gl * c:(gl + 1) * c]


def _s5_fill_states(u_ref, bre_ref, bim_ref, ar_ref, ai_ref, s_re, s_im, lseg, n_chunks, chunk):
    def fill(cidx, carry):
        rows = pl.ds(pl.multiple_of(cidx * chunk, SUBLANES), chunk)
        ub = u_ref[rows, :].astype(BF16)
        s_re[rows, :] = jnp.dot(ub, bre_ref[...], preferred_element_type=F32)
        s_im[rows, :] = jnp.dot(ub, bim_ref[...], preferred_element_type=F32)
        return carry
    lax.fori_loop(0, n_chunks, fill, 0)
    _segment_scan(s_re, s_im, ar_ref[...], ai_ref[...], lseg, reverse=False)


def _s5_forward(u_p, ar, ai, bbr, bbi, c_re, c_im, d_row, glu_w, glub_row):
    lp = u_p.shape[0]
    lseg = lp // SUBLANES
    chunk, n_chunks = lseg, SUBLANES

    def body(u_ref, ar_ref, ai_ref, bbr_ref, bbi_ref, cr_ref, ci_ref, d_ref, gw_ref, glub_ref,
             ys_ref, s_re, s_im, bre_ref, bim_ref, cre_ref, cim_ref, glu_ref, stage):
        for bd, blocks in ((bre_ref, bbr_ref), (bim_ref, bbi_ref), (cre_ref, cr_ref), (cim_ref, ci_ref),
                           (glu_ref, gw_ref)):
            _fill_block_diag(bd, blocks, stage)
        _s5_fill_states(u_ref, bre_ref, bim_ref, ar_ref, ai_ref, s_re, s_im, lseg, n_chunks, chunk)

        def emit(cidx, carry):
            rows = pl.ds(pl.multiple_of(cidx * chunk, SUBLANES), chunk)
            y = (_dot_nt(s_re[rows, :], cre_ref[...]) - _dot_nt(s_im[rows, :], cim_ref[...])
                 + d_ref[...] * u_ref[rows, :])
            g = _gelu(y)
            gate = _dot(g, glu_ref[...]) + glub_ref[...]
            ys_ref[rows, :] = g * _sigmoid(gate)
            return carry
        lax.fori_loop(0, n_chunks, emit, 0)

    col_u, row_u, row_s, bc_blk, glu_blk = _s5_specs(lp)
    return pl.pallas_call(
        body, name="s5_forward", grid=(N_COL,),
        in_specs=[col_u, row_s, row_s, bc_blk, bc_blk, bc_blk, bc_blk, row_u, glu_blk, row_u],
        out_specs=col_u, out_shape=jax.ShapeDtypeStruct((lp, D_SSM), F32),
        scratch_shapes=[pltpu.VMEM((lp, COL_S), F32), pltpu.VMEM((lp, COL_S), F32)] + _s5_block_diag_scratch(),
        compiler_params=_params(("arbitrary",), VMEM_LIMIT),
    )(u_p, ar, ai, bbr, bbi, c_re, c_im, d_row, glu_w, glub_row)


def _s5_backward(u_p, dys_p, ar, ai, bbr, bbi, c_re, c_im, d_row, glu_w, glub_row):
    lp = u_p.shape[0]
    lseg = lp // SUBLANES
    chunk, n_chunks = lseg, SUBLANES

    def body(u_ref, dys_ref, ar_ref, ai_ref, bbr_ref, bbi_ref, cr_ref, ci_ref, d_ref, gw_ref, glub_ref,
             du_ref, dar_ref, dai_ref, dbbr_ref, dbbi_ref, dcr_ref, dci_ref, dd_ref, dgw_ref, dglub_ref,
             s_re, s_im, q_re, q_im, bre_ref, bim_ref, cre_ref, cim_ref, glu_ref, stage,
             dbre_ref, dbim_ref, dcre_ref, dcim_ref, dglu_ref):
        for bd, blocks in ((bre_ref, bbr_ref), (bim_ref, bbi_ref), (cre_ref, cr_ref), (cim_ref, ci_ref),
                           (glu_ref, gw_ref)):
            _fill_block_diag(bd, blocks, stage)
        _s5_fill_states(u_ref, bre_ref, bim_ref, ar_ref, ai_ref, s_re, s_im, lseg, n_chunks, chunk)
        for ref in (dcre_ref, dcim_ref, dd_ref, dglu_ref, dglub_ref, dbre_ref, dbim_ref):
            ref[...] = jnp.zeros(ref.shape, F32)

        def mixer_bwd(cidx, carry):
            rows = pl.ds(pl.multiple_of(cidx * chunk, SUBLANES), chunk)
            u = u_ref[rows, :]
            sr, si = s_re[rows, :], s_im[rows, :]
            y = _dot_nt(sr, cre_ref[...]) - _dot_nt(si, cim_ref[...]) + d_ref[...] * u
            g = _gelu(y)
            sg = _sigmoid(_dot(g, glu_ref[...]) + glub_ref[...])
            dout = dys_ref[rows, :]
            dgate = dout * g * sg * (1.0 - sg)
            dy = (dout * sg + _dot_nt(dgate, glu_ref[...])) * _gelu_grad(y)
            dglu_ref[...] += _dot_tn(g, dgate)
            dglub_ref[...] += jnp.sum(dgate, axis=0, keepdims=True)
            dd_ref[...] += jnp.sum(dy * u, axis=0, keepdims=True)
            dcre_ref[...] += _dot_tn(dy, sr)
            dcim_ref[...] -= _dot_tn(dy, si)
            q_re[rows, :] = _dot(dy, cre_ref[...])
            q_im[rows, :] = -_dot(dy, cim_ref[...])
            du_ref[rows, :] = d_ref[...] * dy
            return carry
        lax.fori_loop(0, n_chunks, mixer_bwd, 0)

        row = lax.broadcasted_iota(jnp.int32, (SUBLANES, COL_S), 0)

        def visit(i, qr, qi, acc, is_tile0):
            if is_tile0:
                prev = _tile_rows(lseg - 1)
                pr = jnp.where(row == 0, 0.0, pltpu.roll(s_re[prev, :], 1, 0))
                pi = jnp.where(row == 0, 0.0, pltpu.roll(s_im[prev, :], 1, 0))
            else:
                prev = _tile_rows(i - 1)
                pr, pi = s_re[prev, :], s_im[prev, :]
            return acc[0] + qr * pr + qi * pi, acc[1] + qi * pr - qr * pi

        dar, dai = _segment_scan(q_re, q_im, ar_ref[...], -ai_ref[...], lseg, reverse=True, visit=visit)
        dar_ref[...] = jnp.sum(dar, axis=0, keepdims=True)
        dai_ref[...] = jnp.sum(dai, axis=0, keepdims=True)

        def input_bwd(cidx, carry):
            rows = pl.ds(pl.multiple_of(cidx * chunk, SUBLANES), chunk)
            qr, qi = q_re[rows, :], q_im[rows, :]
            u = u_ref[rows, :]
            du_ref[rows, :] += _dot_nt(qr, bre_ref[...]) + _dot_nt(qi, bim_ref[...])
            dbre_ref[...] += _dot_tn(u, qr)
            dbim_ref[...] += _dot_tn(u, qi)
            return carry
        lax.fori_loop(0, n_chunks, input_bwd, 0)
        for out, acc in ((dbbr_ref, dbre_ref), (dbbi_ref, dbim_ref), (dcr_ref, dcre_ref), (dci_ref, dcim_ref),
                         (dgw_ref, dglu_ref)):
            _take_block_diag(out, acc[...])

    col_u, row_u, row_s, bc_blk, glu_blk = _s5_specs(lp)
    group_mats = jax.ShapeDtypeStruct((SSM_GROUPS, SSM_GROUP, SSM_STATE), F32)
    return pl.pallas_call(
        body, name="s5_backward", grid=(N_COL,),
        in_specs=[col_u, col_u, row_s, row_s, bc_blk, bc_blk, bc_blk, bc_blk, row_u, glu_blk, row_u],
        out_specs=[col_u, row_s, row_s, bc_blk, bc_blk, bc_blk, bc_blk, row_u, glu_blk, row_u],
        out_shape=[jax.ShapeDtypeStruct((lp, D_SSM), F32),
                   jax.ShapeDtypeStruct((1, N_COL * COL_S), F32), jax.ShapeDtypeStruct((1, N_COL * COL_S), F32),
                   group_mats, group_mats, group_mats, group_mats, jax.ShapeDtypeStruct((1, D_SSM), F32),
                   jax.ShapeDtypeStruct((SSM_GROUPS, SSM_GROUP, SSM_GROUP), F32), jax.ShapeDtypeStruct((1, D_SSM), F32)],
        scratch_shapes=([pltpu.VMEM((lp, COL_S), F32)] * 4 + _s5_block_diag_scratch()
                        + [pltpu.VMEM((COL_U, COL_S), F32)] * 4 + [pltpu.VMEM((COL_U, COL_U), F32)]),
        compiler_params=_params(("arbitrary",), VMEM_LIMIT),
    )(u_p, dys_p, ar, ai, bbr, bbi, c_re, c_im, d_row, glu_w, glub_row)


def _band_apply(band, x):
    hi = x.astype(BF16)
    lo = (x - hi.astype(F32)).astype(BF16)
    dot = functools.partial(jnp.dot, preferred_element_type=F32)
    return dot(band, hi) + dot(band, lo)


def _pool_band(window, transposed):
    t = lax.broadcasted_iota(jnp.int32, (TM, TM + POOL_HALO), 0)
    c = lax.broadcasted_iota(jnp.int32, (TM, TM + POOL_HALO), 1)
    lag = c - t if transposed else t + POOL_HALO - c
    return jnp.where((lag >= 0) & (lag < window), 1.0, 0.0).astype(BF16)


def _pool_inv_count(tile, window, first_row):
    t = tile * TM + lax.broadcasted_iota(jnp.int32, (TM, 1), 0) - first_row
    return 1.0 / jnp.clip(t + 1, 1, window).astype(F32)


def _pool_specs(lp):
    col = pl.BlockSpec((lp, POOL_DIM), lambda k: (0, k))
    mat = pl.BlockSpec((None, POOL_DIM, POOL_DIM), lambda k: (k, 0, 0))
    row = pl.BlockSpec((None, 1, POOL_DIM), lambda k: (k, 0, 0))
    return col, mat, row


def _pool_forward(v, pool_w, pool_scale, first_row):
    lp = v.shape[0]
    n_tiles = lp // TM

    def body(v_ref, w_ref, sc_ref, yp_ref, vpad):
        window = jnp.left_shift(2, pl.program_id(0))
        vpad[pl.ds(0, POOL_HALO), :] = jnp.zeros((POOL_HALO, POOL_DIM), F32)
        vpad[pl.ds(POOL_HALO, lp), :] = v_ref[...]
        band = _pool_band(window, transposed=False)

        def tile(j, carry):
            start = pl.multiple_of(j * TM, TM)
            ext = vpad[pl.ds(start, TM + POOL_HALO), :]
            p = _band_apply(band, ext) * _pool_inv_count(j, window, first_row) - ext[POOL_HALO:, :]
            yp_ref[pl.ds(start, TM), :] = _dot(p, w_ref[...]) * sc_ref[...]
            return carry
        lax.fori_loop(0, n_tiles, tile, 0)

    col, mat, row = _pool_specs(lp)
    return pl.pallas_call(
        body, name="pool_forward", grid=(POOL_GROUPS,),
        in_specs=[col, mat, row], out_specs=col, out_shape=jax.ShapeDtypeStruct((lp, D_SSM), F32),
        scratch_shapes=[pltpu.VMEM((lp + POOL_HALO, POOL_DIM), F32)],
        compiler_params=_params(("arbitrary",), VMEM_LIMIT),
    )(v, pool_w, pool_scale)


def _pool_backward(v, dyp, pool_w, pool_scale, first_row):
    lp = v.shape[0]
    n_tiles = lp // TM

    def body(v_ref, dyp_ref, w_ref, sc_ref, dv_ref, dw_ref, dsc_ref, vpad, gpad):
        window = jnp.left_shift(2, pl.program_id(0))
        vpad[pl.ds(0, POOL_HALO), :] = jnp.zeros((POOL_HALO, POOL_DIM), F32)
        vpad[pl.ds(POOL_HALO, lp), :] = v_ref[...]
        gpad[pl.ds(lp, POOL_HALO), :] = jnp.zeros((POOL_HALO, POOL_DIM), F32)
        dw_ref[...] = jnp.zeros(dw_ref.shape, F32)
        dsc_ref[...] = jnp.zeros(dsc_ref.shape, F32)
        band = _pool_band(window, transposed=False)

        def linear_bwd(j, carry):
            start = pl.multiple_of(j * TM, TM)
            ext = vpad[pl.ds(start, TM + POOL_HALO), :]
            inv = _pool_inv_count(j, window, first_row)
            p = _band_apply(band, ext) * inv - ext[POOL_HALO:, :]
            z = _dot(p, w_ref[...])
            dyp_t = dyp_ref[pl.ds(start, TM), :]
            dz = dyp_t * sc_ref[...]
            dsc_ref[...] += jnp.sum(dyp_t * z, axis=0, keepdims=True)
            dw_ref[...] += _dot_tn(p, dz)
            dp = _dot_nt(dz, w_ref[...])
            gpad[pl.ds(start, TM), :] = dp * inv
            dv_ref[pl.ds(start, TM), :] = -dp
            return carry
        lax.fori_loop(0, n_tiles, linear_bwd, 0)
        band_t = _pool_band(window, transposed=True)

        def window_bwd(j, carry):
            start = pl.multiple_of(j * TM, TM)
            dv_ref[pl.ds(start, TM), :] += _band_apply(band_t, gpad[pl.ds(start, TM + POOL_HALO), :])
            return carry
        lax.fori_loop(0, n_tiles, window_bwd, 0)

    col, mat, row = _pool_specs(lp)
    return pl.pallas_call(
        body, name="pool_backward", grid=(POOL_GROUPS,),
        in_specs=[col, col, mat, row], out_specs=[col, mat, row],
        out_shape=[jax.ShapeDtypeStruct((lp, D_SSM), F32),
                   jax.ShapeDtypeStruct((POOL_GROUPS, POOL_DIM, POOL_DIM), F32),
                   jax.ShapeDtypeStruct((POOL_GROUPS, 1, POOL_DIM), F32)],
        scratch_shapes=[pltpu.VMEM((lp + POOL_HALO, POOL_DIM), F32)] * 2,
        compiler_params=_params(("arbitrary",), VMEM_LIMIT),
    )(v, dyp, pool_w, pool_scale)


def _row_specs():
    head = _const((HEAD, D_MODEL))
    xrow = pl.BlockSpec((TM, D_MODEL), _xrow)
    full = pl.BlockSpec((TM, D_MODEL), lambda i: (i, 0))
    half = pl.BlockSpec((TM, D_SSM), lambda i: (i, 0))
    return head, xrow, full, half


def _in_proj(head, x, g1, w_in):
    n_tiles = (HEAD + x.shape[0]) // TM
    lp = n_tiles * TM

    def body(head_ref, x_ref, g_ref, w_ref, u_ref, v_ref):
        h0 = jnp.where(pl.program_id(0) == 0, head_ref[...], x_ref[...])
        proj = _dot(h0 * _rstd(h0) * g_ref[...], w_ref[...])
        u_ref[...] = proj[:, :D_SSM]
        v_ref[...] = proj[:, D_SSM:]

    head_s, xrow, _, half = _row_specs()
    return pl.pallas_call(
        body, name="in_proj", grid=(n_tiles,),
        in_specs=[head_s, xrow, _const((1, D_MODEL)), _const((D_MODEL, D_MODEL))],
        out_specs=[half, half], out_shape=[jax.ShapeDtypeStruct((lp, D_SSM), F32)] * 2,
        compiler_params=_params(("parallel",), VMEM_LIMIT),
    )(head, x, g1, w_in)


def _out_proj(head, x, ys, yp, gs, gp, w_out):
    lp = ys.shape[0]

    def body(head_ref, x_ref, ys_ref, yp_ref, gs_ref, gp_ref, w_ref, h1_ref):
        h0 = jnp.where(pl.program_id(0) == 0, head_ref[...], x_ref[...])
        ys_t, yp_t = ys_ref[...], yp_ref[...]
        ms = ys_t * _rstd(ys_t) * gs_ref[...]
        mp = yp_t * _rstd(yp_t) * gp_ref[...]
        h1_ref[...] = h0 + _dot(ms, w_ref[pl.ds(0, D_SSM), :]) + _dot(mp, w_ref[pl.ds(D_SSM, D_SSM), :])

    head_s, xrow, full, half = _row_specs()
    return pl.pallas_call(
        body, name="out_proj", grid=(lp // TM,),
        in_specs=[head_s, xrow, half, half, _const((1, D_SSM)), _const((1, D_SSM)), _const((D_MODEL, D_MODEL))],
        out_specs=full, out_shape=jax.ShapeDtypeStruct((lp, D_MODEL), F32),
        compiler_params=_params(("parallel",), VMEM_LIMIT),
    )(head, x, ys, yp, gs, gp, w_out)


def _load_weights(c_hbm, wd_hbm, c_vmem, wd_vmem, sems):
    @pl.when(pl.program_id(0) == 0)
    def _():
        copies = [pltpu.make_async_copy(c_hbm, c_vmem, sems.at[0]),
                  pltpu.make_async_copy(wd_hbm, wd_vmem, sems.at[1])]
        for cp in copies:
            cp.start()
        for cp in copies:
            cp.wait()


def _ffn_scratch():
    return [pltpu.VMEM((N_DEV, D_MODEL, 2 * FF_PAD), BF16), pltpu.VMEM((N_DEV, FF_PAD, D_MODEL), BF16),
            pltpu.SemaphoreType.DMA((2,))]


def _ffn_forward(h1, g2, c_all, wd_all):
    lp = h1.shape[0]

    def body(h1_ref, g_ref, c_hbm, wd_hbm, ab_ref, n2t_ref, h2_ref, c_vmem, wd_vmem, sems):
        _load_weights(c_hbm, wd_hbm, c_vmem, wd_vmem, sems)
        h1_t = h1_ref[...]
        n2_f = h1_t * _rstd(h1_t) * g_ref[...]
        n2 = n2_f.astype(BF16)
        n2t_ref[...] = n2_f.T.astype(BF16)
        acc = h1_t
        for j in range(N_DEV):
            ab = jnp.dot(n2, c_vmem[j], preferred_element_type=F32)
            a, b = ab[:, :FF_PAD], ab[:, FF_PAD:]
            ab_ref[:, pl.ds(j * 2 * FF_PAD, 2 * FF_PAD)] = ab.astype(BF16)
            acc = acc + _dot(a * _sigmoid(a) * b, wd_vmem[j])
        h2_ref[...] = acc

    _, _, full, _ = _row_specs()
    wide = pl.BlockSpec((TM, N_DEV * 2 * FF_PAD), lambda i: (i, 0))
    return pl.pallas_call(
        body, name="ffn_forward", grid=(lp // TM,),
        in_specs=[full, _const((1, D_MODEL)), ANY, ANY],
        out_specs=[wide, pl.BlockSpec((D_MODEL, TM), lambda i: (0, i)), full],
        out_shape=[jax.ShapeDtypeStruct((lp, N_DEV * 2 * FF_PAD), BF16),
                   jax.ShapeDtypeStruct((D_MODEL, lp), BF16), jax.ShapeDtypeStruct((lp, D_MODEL), F32)],
        scratch_shapes=_ffn_scratch(), compiler_params=_params(("arbitrary",), VMEM_LIMIT),
    )(h1, g2, c_all, wd_all)


def _ffn_backward(h2, target, h1, ab, gf, g2, c_all, wd_all):
    lp = h1.shape[0]

    def body(h2_ref, t_ref, h1_ref, ab_ref, gf_ref, g2_ref, c_hbm, wd_hbm,
             dh1_ref, dab_ref, dh2_ref, loss_ref, dgf_ref, dg2_ref, c_vmem, wd_vmem, sems):
        i = pl.program_id(0)
        _load_weights(c_hbm, wd_hbm, c_vmem, wd_vmem, sems)

        @pl.when(i == 0)
        def _():
            loss_ref[...] = jnp.zeros(loss_ref.shape, F32)
            dgf_ref[...] = jnp.zeros(dgf_ref.shape, F32)
            dg2_ref[...] = jnp.zeros(dg2_ref.shape, F32)

        h2_t = h2_ref[...]
        rf = _rstd(h2_t)
        xf = h2_t * rf
        diff = jnp.where(i == 0, 0.0, xf * gf_ref[...] - t_ref[...])
        loss_ref[...] += 0.5 * jnp.sum(diff * diff) / D_MODEL
        dh2, dgf = _rms_bwd(diff / D_MODEL, xf, rf, gf_ref[...])
        dgf_ref[...] += dgf
        dh2_b = dh2.astype(BF16)
        dh2_ref[...] = dh2_b

        dn2 = jnp.zeros((TM, D_MODEL), F32)
        for j in range(N_DEV):
            cols = pl.ds(j * 2 * FF_PAD, 2 * FF_PAD)
            dff = _dot_nt(dh2_b, wd_vmem[j])
            ab_t = ab_ref[:, cols].astype(F32)
            a, b = ab_t[:, :FF_PAD], ab_t[:, FF_PAD:]
            sg = _sigmoid(a)
            dab_ref[:, pl.ds(j * 2 * FF_PAD, FF_PAD)] = (dff * b * sg * (1.0 + a * (1.0 - sg))).astype(BF16)
            dab_ref[:, pl.ds(j * 2 * FF_PAD + FF_PAD, FF_PAD)] = (dff * a * sg).astype(BF16)
            dn2 = dn2 + _dot_nt(dab_ref[:, cols], c_vmem[j])

        h1_t = h1_ref[...]
        r2 = _rstd(h1_t)
        dx, dg2 = _rms_bwd(dn2, h1_t * r2, r2, g2_ref[...])
        dg2_ref[...] += dg2
        dh1_ref[...] = dh2 + dx

    _, xrow, full, _ = _row_specs()
    wide = pl.BlockSpec((TM, N_DEV * 2 * FF_PAD), lambda i: (i, 0))
    vec = _const((1, D_MODEL))
    return pl.pallas_call(
        body, name="ffn_backward", grid=(lp // TM,),
        in_specs=[full, xrow, full, wide, vec, vec, ANY, ANY],
        out_specs=[full, wide, full, _const((1, PACK_LANES)), vec, vec],
        out_shape=[jax.ShapeDtypeStruct((lp, D_MODEL), F32),
                   jax.ShapeDtypeStruct((lp, N_DEV * 2 * FF_PAD), BF16),
                   jax.ShapeDtypeStruct((lp, D_MODEL), BF16),
                   jax.ShapeDtypeStruct((1, PACK_LANES), F32),
                   jax.ShapeDtypeStruct((1, D_MODEL), F32), jax.ShapeDtypeStruct((1, D_MODEL), F32)],
        scratch_shapes=_ffn_scratch(), compiler_params=_params(("arbitrary",), VMEM_LIMIT),
    )(h2, target, h1, ab, gf, g2, c_all, wd_all)


def _ffn_wgrad(n2t, dh2, ab, dab):
    lp = dh2.shape[0]
    rows = lp // WGRAD_STEPS

    def body(n2t_ref, dh2_ref, ab_ref, dab_ref, dc_ref, dwd_ref, dc_acc, dwd_acc):
        i = pl.program_id(1)

        @pl.when(i == 0)
        def _():
            dc_acc[...] = jnp.zeros(dc_acc.shape, F32)
            dwd_acc[...] = jnp.zeros(dwd_acc.shape, F32)

        ab_t = ab_ref[...].astype(F32)
        a, b = ab_t[:, :FF_PAD], ab_t[:, FF_PAD:]
        dc_acc[...] += jnp.dot(n2t_ref[...], dab_ref[...], preferred_element_type=F32)
        dwd_acc[...] += _dot_tn(a * _sigmoid(a) * b, dh2_ref[...])

        @pl.when(i == pl.num_programs(1) - 1)
        def _():
            dc_ref[...] = dc_acc[...].astype(BF16)
            dwd_ref[...] = dwd_acc[...].astype(BF16)

    act_t = pl.BlockSpec((D_MODEL, rows), lambda j, i: (0, i))
    act = pl.BlockSpec((rows, D_MODEL), lambda j, i: (i, 0))
    shard = pl.BlockSpec((rows, 2 * FF_PAD), lambda j, i: (i, j))
    return pl.pallas_call(
        body, name="ffn_wgrad", grid=(N_DEV, WGRAD_STEPS),
        in_specs=[act_t, act, shard, shard],
        out_specs=[pl.BlockSpec((None, D_MODEL, 2 * FF_PAD), lambda j, i: (j, 0, 0)),
                   pl.BlockSpec((None, FF_PAD, D_MODEL), lambda j, i: (j, 0, 0))],
        out_shape=[jax.ShapeDtypeStruct((N_DEV, D_MODEL, 2 * FF_PAD), BF16),
                   jax.ShapeDtypeStruct((N_DEV, FF_PAD, D_MODEL), BF16)],
        scratch_shapes=[pltpu.VMEM((D_MODEL, 2 * FF_PAD), F32), pltpu.VMEM((FF_PAD, D_MODEL), F32)],
        compiler_params=_params(("parallel", "arbitrary"), VMEM_LIMIT),
    )(n2t, dh2, ab, dab)


def _out_proj_backward(dh1, ys, yp, gs, gp, w_out):
    lp = ys.shape[0]

    def body(dh1_ref, ys_ref, yp_ref, gs_ref, gp_ref, w_ref, dys_ref, dyp_ref, dgs_ref, dgp_ref, dw_out, dw_ref):
        @pl.when(pl.program_id(0) == 0)
        def _():
            dgs_ref[...] = jnp.zeros(dgs_ref.shape, F32)
            dgp_ref[...] = jnp.zeros(dgp_ref.shape, F32)
            dw_ref[...] = jnp.zeros(dw_ref.shape, F32)

        dh1_b = dh1_ref[...].astype(BF16)
        dmix = _dot_nt(dh1_b, w_ref[...])
        for y_ref, g_ref, dy_ref, dg_ref, lo in ((ys_ref, gs_ref, dys_ref, dgs_ref, 0),
                                                 (yp_ref, gp_ref, dyp_ref, dgp_ref, D_SSM)):
            y_t = y_ref[...]
            r = _rstd(y_t)
            xhat = y_t * r
            dy, dg = _rms_bwd(dmix[:, lo:lo + D_SSM], xhat, r, g_ref[...])
            dy_ref[...] = dy
            dg_ref[...] += dg
            dw_ref[pl.ds(lo, D_SSM), :] += _dot_tn(xhat * g_ref[...], dh1_b)

        @pl.when(pl.program_id(0) == pl.num_programs(0) - 1)
        def _():
            dw_out[...] = dw_ref[...].astype(BF16)

    _, _, full, half = _row_specs()
    vec = _const((1, D_SSM))
    return pl.pallas_call(
        body, name="out_proj_backward", grid=(lp // TM,),
        in_specs=[full, half, half, vec, vec, _const((D_MODEL, D_MODEL))],
        out_specs=[half, half, vec, vec, _const((D_MODEL, D_MODEL))],
        out_shape=[jax.ShapeDtypeStruct((lp, D_SSM), F32)] * 2 + [jax.ShapeDtypeStruct((1, D_SSM), F32)] * 2
        + [jax.ShapeDtypeStruct((D_MODEL, D_MODEL), BF16)],
        scratch_shapes=[pltpu.VMEM((D_MODEL, D_MODEL), F32)],
        compiler_params=_params(("arbitrary",), VMEM_LIMIT),
    )(dh1, ys, yp, gs, gp, w_out)


def _in_proj_backward(head, x, du, dv, dh1, g1, w_in):
    lp = du.shape[0]

    def body(head_ref, x_ref, du_ref, dv_ref, dh1_ref, g_ref, w_ref, dx_ref, dhead_ref, dg_ref, dw_out, dw_ref):
        i = pl.program_id(0)

        @pl.when(i == 0)
        def _():
            dg_ref[...] = jnp.zeros(dg_ref.shape, F32)
            dw_ref[...] = jnp.zeros(dw_ref.shape, F32)

        h0 = jnp.where(i == 0, head_ref[...], x_ref[...])
        r = _rstd(h0)
        xhat = h0 * r
        n1 = (xhat * g_ref[...]).astype(BF16)
        du_b, dv_b = du_ref[...].astype(BF16), dv_ref[...].astype(BF16)
        dn1 = _dot_nt(du_b, w_ref[:, pl.ds(0, D_SSM)]) + _dot_nt(dv_b, w_ref[:, pl.ds(D_SSM, D_SSM)])
        dx, dg = _rms_bwd(dn1, xhat, r, g_ref[...])
        dg_ref[...] += dg
        dh0 = dh1_ref[...] + dx
        dx_ref[...] = dh0

        @pl.when(i == 0)
        def _():
            dhead_ref[...] = dh0

        dw_ref[:, pl.ds(0, D_SSM)] += _dot_tn(n1, du_b)
        dw_ref[:, pl.ds(D_SSM, D_SSM)] += _dot_tn(n1, dv_b)

        @pl.when(i == pl.num_programs(0) - 1)
        def _():
            dw_out[...] = dw_ref[...].astype(BF16)

    head_s, xrow, full, half = _row_specs()
    vec = _const((1, D_MODEL))
    mat = _const((D_MODEL, D_MODEL))
    return pl.pallas_call(
        body, name="in_proj_backward", grid=(lp // TM,),
        in_specs=[head_s, xrow, half, half, full, vec, mat],
        out_specs=[xrow, head_s, vec, mat],
        out_shape=[jax.ShapeDtypeStruct(x.shape, F32), jax.ShapeDtypeStruct((HEAD, D_MODEL), F32),
                   jax.ShapeDtypeStruct((1, D_MODEL), F32), jax.ShapeDtypeStruct((D_MODEL, D_MODEL), BF16)],
        scratch_shapes=[pltpu.VMEM((D_MODEL, D_MODEL), F32)],
        compiler_params=_params(("arbitrary",), VMEM_LIMIT),
    )(head, x, du, dv, dh1, g1, w_in)


def _permute_rows(a):
    lp, n = a.shape
    return a.reshape(SUBLANES, lp // SUBLANES, n).transpose(1, 0, 2).reshape(lp, n)


def _unpermute_rows(a):
    lp, n = a.shape
    return a.reshape(lp // SUBLANES, SUBLANES, n).transpose(1, 0, 2).reshape(lp, n)


def _pack(parts, dtype):
    rows = []
    for p in parts:
        flat = p.reshape(-1).astype(dtype)
        pad = (-flat.shape[0]) % PACK_UNIT
        rows.append(jnp.pad(flat, (0, pad)).reshape(-1, PACK_LANES))
    n_rows = sum(r.shape[0] for r in rows)
    if n_rows % 16:
        rows.append(jnp.zeros((8, PACK_LANES), dtype))
    return jnp.concatenate(rows, axis=0)


def _as2d(a):
    return a.reshape(-1, a.shape[-1])


def _unpack(packed, shapes):
    out, row = [], 0
    for shape in shapes:
        size = 1
        for s in shape:
            size *= s
        n_rows = -(-size // PACK_UNIT) * 8
        out.append(packed[row:row + n_rows].reshape(-1)[:size].reshape(shape))
        row += n_rows
    return out


def _pad_cols(a):
    return jnp.pad(a, ((0, 0), (0, FF_PAD - FF_SHARD)))


def _pad_rows(a):
    return jnp.pad(a, ((0, FF_PAD - FF_SHARD), (0, 0)))


def _gate_up(gate, up):
    return jnp.concatenate([_pad_cols(gate), _pad_cols(up)], axis=1)


def _gate_up_t(gate, up):
    return jnp.concatenate([_pad_rows(gate[0].T), _pad_rows(up[0].T)], axis=0)


def _to_view(name, a):
    if name in ("ssm_b_re", "ssm_b_im"):
        return a[0].transpose(0, 2, 1).reshape(-1, SSM_STATE)
    if name in ("ssm_d", "ssm_glu_b"):
        return a[0].T
    if name == "ssm_glu_w":
        return a[0].transpose(1, 2, 0).reshape(-1, SSM_GROUPS)
    return _as2d(a)


def _from_view(name, r, shape):
    if name in ("ssm_b_re", "ssm_b_im"):
        return r.reshape(SSM_GROUPS, SSM_GROUP, SSM_STATE).transpose(0, 2, 1).reshape(shape)
    if name in ("ssm_d", "ssm_glu_b"):
        return r.T.reshape(shape)
    if name == "ssm_glu_w":
        return r.reshape(SSM_GROUP, SSM_GROUP, SSM_GROUPS).transpose(2, 0, 1).reshape(shape)
    return r.reshape(shape)


def kernel(x, meta_tokens, norm1_g, w_in, ssm_lambda_re, ssm_lambda_im, ssm_log_step, ssm_b_re, ssm_b_im, ssm_c_re, ssm_c_im, ssm_d, ssm_glu_w, ssm_glu_b, ssm_norm_g, pool_w, pool_scale, pool_norm_g, w_out, norm2_g, w_gate, w_up, w_down, final_norm_g, loss_target, m_meta_tokens, m_norm1_g, m_w_in, m_ssm_lambda_re, m_ssm_lambda_im, m_ssm_log_step, m_ssm_b_re, m_ssm_b_im, m_ssm_c_re, m_ssm_c_im, m_ssm_d, m_ssm_glu_w, m_ssm_glu_b, m_ssm_norm_g, m_pool_w, m_pool_scale, m_pool_norm_g, m_w_out, m_norm2_g, m_w_gate, m_w_up, m_w_down, m_final_norm_g, v_meta_tokens, v_norm1_g, v_w_in, v_ssm_lambda_re, v_ssm_lambda_im, v_ssm_log_step, v_ssm_b_re, v_ssm_b_im, v_ssm_c_re, v_ssm_c_im, v_ssm_d, v_ssm_glu_w, v_ssm_glu_b, v_ssm_norm_g, v_pool_w, v_pool_scale, v_pool_norm_g, v_w_out, v_norm2_g, v_w_gate, v_w_up, v_w_down, v_final_norm_g):
    given = dict(locals())
    weights = {n: given[n] for n in WEIGHT_NAMES}
    n_meta = meta_tokens.shape[0]
    me = 4 * lax.axis_index("x") + 2 * lax.axis_index("y") + lax.axis_index("c")

    shard_rows = w_in.shape[1]
    first = [w_in[0].astype(BF16), meta_tokens]
    first_make = _push_copies(ALL_PEERS, [False, False])
    first_x, first_token = _split_start(first + [_landing(s, False) for s in first], first_make,
                                        2 * len(ALL_PEERS), "gather_w_in_start")

    xs = x[0]
    tgt = loss_target[0]
    first_row = HEAD - n_meta
    g1, g2, gf = norm1_g, norm2_g, final_norm_g.reshape(1, D_MODEL)
    gs, gp = ssm_norm_g, pool_norm_g

    lam_re, lam_im = ssm_lambda_re[0] + first_token[:1, :1], ssm_lambda_im[0]
    log_step = ssm_log_step[0].reshape(SSM_GROUPS, 1)
    b_re = ssm_b_re[0].transpose(0, 2, 1)
    b_im = ssm_b_im[0].transpose(0, 2, 1)
    abr, abi, zr, zi = _s5_disc_a(lam_re, lam_im, log_step)
    zr_col, zi_col = zr.reshape(SSM_GROUPS, 1, SSM_STATE), zi.reshape(SSM_GROUPS, 1, SSM_STATE)
    bbr, bbi = _s5_disc_b(zr_col, zi_col, b_re, b_im)
    s5_consts = (abr.reshape(1, -1), abi.reshape(1, -1), bbr, bbi, ssm_c_re[0], ssm_c_im[0],
                 ssm_d[0].reshape(1, D_SSM), ssm_glu_w[0], ssm_glu_b[0].reshape(1, D_SSM))
    pool_sc = pool_scale[0].reshape(POOL_GROUPS, 1, POOL_DIM)

    (_, _, w_in_all, meta_all), first_done = _split_wait(first_x, first_make, bbr, "gather_w_in_wait")
    w_in_all = w_in_all.reshape(D_MODEL, D_MODEL)
    meta_full = meta_all.transpose(1, 0, 2).reshape(n_meta, D_MODEL)
    head = jnp.concatenate([jnp.zeros((HEAD - n_meta, D_MODEL), F32), meta_full], axis=0)
    shards = [(w_out[0] + first_done[:1, :1]).astype(BF16), _pad_rows(w_down[0]).astype(BF16),
              _gate_up(w_gate[0], w_up[0]).astype(BF16)]
    gather_make = _push_copies((SIBLING,) + CHIP_PEERS, [False, False, False])
    gather, gather_token = _split_start(shards + [_landing(s, False) for s in shards], gather_make,
                                        3 * (1 + len(CHIP_PEERS)), "gather_start")

    u, v = _in_proj(head, xs, g1 + gather_token[:1, :1], w_in_all)
    u_p = _permute_rows(u)
    ys_p = _s5_forward(u_p, *s5_consts)
    landed, _ = _split_wait(gather, gather_make, ys_p, "gather_wait")
    forward_make = _forward_copies(3)
    forward, forward_token = _split_start(list(landed[3:]), forward_make, 3 * len(CHIP_PEERS), "gather_forward_start")
    ys = _unpermute_rows(ys_p)
    yp = _pool_forward(v, pool_w[0], pool_sc + forward_token[:1, :1], first_row)
    (w_out_all, wd_all, c_all), _ = _split_wait(forward, forward_make, yp, "gather_forward_wait")
    w_out_all = w_out_all.reshape(D_MODEL, D_MODEL)
    h1 = _out_proj(head, xs, ys, yp, gs, gp, w_out_all)
    ab, n2t, h2 = _ffn_forward(h1, g2, c_all, wd_all)

    dh1, dab, dh2, loss_part, d_gf, d_g2 = _ffn_backward(h2, tgt, h1, ab, gf, g2, c_all, wd_all)
    d_c, d_wd = _ffn_wgrad(n2t, dh2, ab, dab)
    ffn_make = _push_copies(ALL_PEERS, [True, True])
    ffn_x, ffn_token = _split_start([d_c, d_wd, _landing(d_c, True), _landing(d_wd, True)], ffn_make,
                                    2 * len(ALL_PEERS), "ffn_grad_start")
    dys, dyp, d_gs, d_gp, d_wout = _out_proj_backward(dh1, ys, yp, gs + ffn_token[:1, :1], gp, w_out_all)
    dv, d_pool_w, d_pool_sc = _pool_backward(v, dyp, pool_w[0], pool_sc, first_row)
    (du_p, d_ar, d_ai, d_bbr, d_bbi, d_c_re, d_c_im, d_d, d_glu, d_glub) = _s5_backward(
        u_p, _permute_rows(dys), *s5_consts)
    du = _unpermute_rows(du_p)

    d_zr, d_zi, d_b_re, d_b_im = _s5_disc_b_bwd(zr_col, zi_col, b_re, b_im, d_bbr, d_bbi)
    d_lam_re, d_lam_im, d_log_step = _s5_disc_a_bwd(
        lam_re, lam_im, log_step,
        (d_ar.reshape(SSM_GROUPS, SSM_STATE), d_ai.reshape(SSM_GROUPS, SSM_STATE),
         d_zr.reshape(SSM_GROUPS, SSM_STATE), d_zi.reshape(SSM_GROUPS, SSM_STATE)))
    groups_last = lambda row: row.reshape(SSM_GROUPS, SSM_GROUP).T
    small_grads = {
        "ssm_lambda_re": d_lam_re, "ssm_lambda_im": d_lam_im, "ssm_log_step": d_log_step.reshape(1, SSM_GROUPS),
        "ssm_b_re": d_b_re.reshape(-1, SSM_STATE), "ssm_b_im": d_b_im.reshape(-1, SSM_STATE),
        "ssm_c_re": d_c_re.reshape(-1, SSM_STATE), "ssm_c_im": d_c_im.reshape(-1, SSM_STATE),
        "ssm_d": groups_last(d_d), "ssm_glu_w": d_glu.transpose(1, 2, 0).reshape(-1, SSM_GROUPS),
        "ssm_glu_b": groups_last(d_glub),
        "ssm_norm_g": d_gs, "pool_w": d_pool_w.reshape(-1, POOL_DIM), "pool_scale": d_pool_sc.reshape(-1, POOL_DIM),
        "pool_norm_g": d_gp, "norm2_g": d_g2,
    }

    early_names = SMALL_NAMES[1:-1]
    early_pack = _pack([small_grads[n] for n in early_names], BF16)
    d_wout = d_wout.reshape(N_DEV, shard_rows, D_MODEL)
    early_make = _push_copies(ALL_PEERS, [True, False])
    early_x, early_token = _split_start([d_wout, early_pack, _landing(d_wout, True), _landing(early_pack, False)],
                                        early_make, 2 * len(ALL_PEERS), "early_grad_start")
    d_x, d_head, d_g1, d_win = _in_proj_backward(head, xs, du, dv, dh1, g1 + early_token[:1, :1], w_in_all)
    (_, _, r_c, r_wd), _ = _split_wait(ffn_x, ffn_make, d_g1, "ffn_grad_wait")
    (_, _, r_wout, r_early), _ = _split_wait(early_x, early_make, d_g1, "early_grad_wait")
    d_win = d_win.reshape(N_DEV, shard_rows, D_MODEL)
    late_pack = _pack([d_g1, d_gf, d_head[first_row:], loss_part], F32)
    late_make = _push_copies(ALL_PEERS, [True, False])
    late_x, late_token = _split_start([d_win, late_pack, _landing(d_win, True), _landing(late_pack, False)],
                                      late_make, 2 * len(ALL_PEERS), "late_grad_start")

    results = {}
    res_c = _adamw_transposed(r_c, _gate_up_t(w_gate, w_up) + late_token[:1, :1], _gate_up_t(m_w_gate, m_w_up),
                              _gate_up_t(v_w_gate, v_w_up), "adamw_gate_up")
    results["w_gate"] = [r[:FF_SHARD].T for r in res_c]
    results["w_up"] = [r[FF_PAD:FF_PAD + FF_SHARD].T for r in res_c]
    res_wd = _adamw(r_wd, _pad_rows(w_down[0]), _pad_rows(m_w_down[0]), _pad_rows(v_w_down[0]), 128, "adamw_w_down")
    results["w_down"] = [r[:FF_SHARD] for r in res_wd]
    results["w_out"] = _adamw(r_wout, w_out[0], m_w_out[0], v_w_out[0], shard_rows, "adamw_w_out")
    done = res_c[1][:1, :1] + res_wd[1][:1, :1] + results["w_out"][1][:1, :1]
    (_, _, r_win, r_late), _ = _split_wait(late_x, late_make, done, "late_grad_wait")
    results["w_in"] = _adamw(r_win, w_in[0], m_w_in[0], v_w_in[0], shard_rows, "adamw_w_in")

    sum_early, sum_late = _reduce_slots([r_early, r_late], "small_grad_sums")
    views = lambda prefix: [_to_view(n, given[prefix + n]) for n in SMALL_NAMES]
    w_views = views("")
    g_views = _unpack(sum_early, [w.shape for w in w_views[1:-1]])
    g_norm1, g_final, g_meta_all, loss_row = _unpack(
        sum_late, [norm1_g.shape, (1, D_MODEL), (n_meta, D_MODEL), (1, PACK_LANES)])
    g_views = [g_norm1] + g_views + [g_final]
    res_small = _adamw_many(g_views, w_views, views("m_"), views("v_"), "adamw_small")
    for idx, n in enumerate(SMALL_NAMES):
        results[n] = [_from_view(n, part[idx], weights[n].shape) for part in (g_views,) + tuple(res_small)]
    shard_cols = meta_tokens.shape[1]
    g_meta = lax.dynamic_slice_in_dim(g_meta_all, me * shard_cols, shard_cols, axis=1)
    results["meta_tokens"] = _adamw(g_meta[None], meta_tokens, m_meta_tokens, v_meta_tokens, n_meta, "adamw_meta")

    out = [loss_row[0, 0], d_x[None]]
    for part in range(4):
        for n in WEIGHT_NAMES:
            out.append(results[n][part].reshape(weights[n].shape))
    return tuple(out)
```

```python
import jax
import jax.numpy as jnp
from jax import lax
from jax.experimental import pallas as pl
from jax.experimental.pallas import tpu as pltpu

F32 = jnp.float32
BF16 = jnp.bfloat16

N_DEV = 8
D_MODEL = 1024
D_SSM = 512
SSM_GROUP = 16
SSM_STATE = 64
SSM_GROUPS = 32
POOL_GROUPS = 4
POOL_DIM = 128
COL_U = 128
COL_S = 512
N_COL = D_SSM // COL_U
GROUPS_PER_COL = COL_U // SSM_GROUP
FF_SHARD = 352
FF_PAD = 384
TM = 256
WGRAD_STEPS = 2
HEAD = TM
SUBLANES = 8
SCAN_UNROLL = 4
POOL_HALO = 16
EPS = 1e-6
STEP_FLOOR = -1e-4
VMEM_LIMIT = 60 * 1024 * 1024

ADAM_LR = 0.001
ADAM_B1 = 0.9
ADAM_B2 = 0.999
ADAM_EPS = 1e-08
ADAM_WD = 0.01
ADAM_STEP = 10

MESH_ID = pl.DeviceIdType.MESH
ANY = pl.BlockSpec(memory_space=pl.ANY)

SMALL_NAMES = ("norm1_g", "ssm_lambda_re", "ssm_lambda_im", "ssm_log_step", "ssm_b_re", "ssm_b_im",
               "ssm_c_re", "ssm_c_im", "ssm_d", "ssm_glu_w", "ssm_glu_b", "ssm_norm_g", "pool_w",
               "pool_scale", "pool_norm_g", "norm2_g", "final_norm_g")
WEIGHT_NAMES = ("meta_tokens", "norm1_g", "w_in", "ssm_lambda_re", "ssm_lambda_im", "ssm_log_step",
                "ssm_b_re", "ssm_b_im", "ssm_c_re", "ssm_c_im", "ssm_d", "ssm_glu_w", "ssm_glu_b",
                "ssm_norm_g", "pool_w", "pool_scale", "pool_norm_g", "w_out", "norm2_g", "w_gate",
                "w_up", "w_down", "final_norm_g")
PACK_LANES = 128
PACK_UNIT = 8 * PACK_LANES


def _dot(a, b):
    return jnp.dot(a.astype(BF16), b.astype(BF16), preferred_element_type=F32)


def _dot_nt(a, b):
    return lax.dot_general(a.astype(BF16), b.astype(BF16), (((1,), (1,)), ((), ())), preferred_element_type=F32)


def _dot_tn(a, b):
    return lax.dot_general(a.astype(BF16), b.astype(BF16), (((0,), (0,)), ((), ())), preferred_element_type=F32)


def _sigmoid(x):
    return 1.0 / (1.0 + jnp.exp(-x))


def _rstd(x):
    return lax.rsqrt(jnp.mean(x * x, axis=-1, keepdims=True) + EPS)


def _rms_bwd(dy, xhat, r, g):
    dxh = dy * g
    dx = r * (dxh - xhat * jnp.mean(dxh * xhat, axis=-1, keepdims=True))
    return dx, jnp.sum(dy * xhat, axis=0, keepdims=True)


def _params(sem, vmem=None):
    return pltpu.CompilerParams(dimension_semantics=sem, vmem_limit_bytes=vmem)


def _const(shape):
    return pl.BlockSpec(shape, lambda *_: (0,) * len(shape))


def _xrow(i):
    return (jnp.maximum(i - 1, 0), 0)


HBM = pl.BlockSpec(memory_space=pltpu.HBM)
SEM = pl.BlockSpec(memory_space=pltpu.SEMAPHORE)
EFFECT = pltpu.SideEffectType.DATAFLOW_SIDE_EFFECTING
ALL_PEERS = tuple(range(1, N_DEV))
SIBLING = 1
CHIP_PEERS = (2, 4, 6)


def _me():
    return 4 * lax.axis_index("x") + 2 * lax.axis_index("y") + lax.axis_index("c")


def _peer(k):
    x, y, c = lax.axis_index("x"), lax.axis_index("y"), lax.axis_index("c")
    px = 1 - x if k & 4 else x
    py = 1 - y if k & 2 else y
    pc = 1 - c if k & 1 else c
    return (px, py, pc), 4 * px + 2 * py + pc


def _landing(arr, scatter):
    if scatter:
        own = lax.dynamic_index_in_dim(arr, _me(), 0, keepdims=False)
    else:
        own = arr
    return lax.dynamic_update_index_in_dim(lax.empty((N_DEV,) + own.shape, arr.dtype), own, _me(), 0)


def _push_copies(peers, scatter):
    n_arr = len(scatter)

    def make(refs, send_sems, recv_sems):
        copies = []
        for a in range(n_arr):
            for i, k in enumerate(peers):
                peer_id, peer = _peer(k)
                sem = a * len(peers) + i
                copies.append(pltpu.make_async_remote_copy(
                    src_ref=refs[a].at[peer] if scatter[a] else refs[a], dst_ref=refs[n_arr + a].at[_me()],
                    send_sem=send_sems.at[sem], recv_sem=recv_sems.at[sem],
                    device_id=peer_id, device_id_type=MESH_ID))
        return copies
    return make


def _forward_copies(n_arr):
    def make(refs, send_sems, recv_sems):
        copies = []
        sibling_id, _ = _peer(SIBLING)
        for a in range(n_arr):
            for i, k in enumerate(CHIP_PEERS):
                slot = refs[a].at[_peer(k)[1]]
                sem = a * len(CHIP_PEERS) + i
                copies.append(pltpu.make_async_remote_copy(
                    src_ref=slot, dst_ref=slot, send_sem=send_sems.at[sem], recv_sem=recv_sems.at[sem],
                    device_id=sibling_id, device_id_type=MESH_ID))
        return copies
    return make


def _split_start(operands, make, n_sem, name):
    n_op = len(operands)

    def body(*refs):
        for cp in make(refs[:n_op], refs[n_op], refs[n_op + 1]):
            cp.start()
        refs[-1][...] = jnp.zeros(refs[-1].shape, F32)

    out = pl.pallas_call(
        body, name=name,
        out_shape=(pltpu.SemaphoreType.DMA((n_sem,)), pltpu.SemaphoreType.DMA((n_sem,)),
                   *[pltpu.HBM(t.shape, t.dtype) for t in operands], jax.ShapeDtypeStruct((8, PACK_LANES), F32)),
        in_specs=[HBM] * n_op, out_specs=(SEM, SEM, *[HBM] * n_op, pl.BlockSpec(memory_space=pltpu.VMEM)),
        input_output_aliases={i: 2 + i for i in range(n_op)},
        compiler_params=pltpu.CompilerParams(has_side_effects=EFFECT),
    )(*[pltpu.with_memory_space_constraint(t, pltpu.HBM) for t in operands])
    return out[:-1], out[-1]


def _split_wait(started, make, after, name):
    send_sems, recv_sems, thru = started[0], started[1], started[2:]
    n_op = len(thru)

    def body(*refs):
        for cp in make(refs[:n_op], refs[n_op], refs[n_op + 1]):
            cp.wait_send()
            cp.wait_recv()
        refs[-1][...] = jnp.zeros(refs[-1].shape, F32)

    out = pl.pallas_call(
        body, name=name,
        out_shape=(*[pltpu.HBM(t.shape, t.dtype) for t in thru], jax.ShapeDtypeStruct((8, PACK_LANES), F32)),
        in_specs=[HBM] * n_op + [SEM, SEM, ANY], out_specs=(*[HBM] * n_op, pl.BlockSpec(memory_space=pltpu.VMEM)),
        input_output_aliases={i: i for i in range(n_op)},
        compiler_params=pltpu.CompilerParams(has_side_effects=EFFECT),
    )(*thru, send_sems, recv_sems, after)
    return out[:-1], out[-1]


def _adamw_math(g, w, m, v):
    nm = ADAM_B1 * m + (1.0 - ADAM_B1) * g
    nv = ADAM_B2 * v + (1.0 - ADAM_B2) * (g * g)
    m_hat = nm / (1.0 - ADAM_B1 ** ADAM_STEP)
    v_hat = nv / (1.0 - ADAM_B2 ** ADAM_STEP)
    return -ADAM_LR * (m_hat / (jnp.sqrt(v_hat) + ADAM_EPS) + ADAM_WD * w), nm, nv


def _sum_slots(s_ref):
    g = s_ref[0].astype(F32)
    for s in range(1, s_ref.shape[0]):
        g = g + s_ref[s].astype(F32)
    return g


def _adamw(slots, w, m, v, tile_rows, name):
    n, rows, cols = slots.shape

    def body(s_ref, w_ref, m_ref, v_ref, g_ref, d_ref, nm_ref, nv_ref):
        g = _sum_slots(s_ref)
        g_ref[...] = g
        d_ref[...], nm_ref[...], nv_ref[...] = _adamw_math(g, w_ref[...], m_ref[...], v_ref[...])

    tile = pl.BlockSpec((tile_rows, cols), lambda i: (i, 0))
    return pl.pallas_call(
        body, name=name, grid=(rows // tile_rows,),
        in_specs=[pl.BlockSpec((n, tile_rows, cols), lambda i: (0, i, 0)), tile, tile, tile],
        out_specs=[tile] * 4, out_shape=[jax.ShapeDtypeStruct((rows, cols), F32)] * 4,
        compiler_params=_params(("parallel",), VMEM_LIMIT),
    )(slots, w, m, v)


def _adamw_transposed(slots, w_t, m_t, v_t, name):
    n, rows, cols = slots.shape
    lane_tile = 128

    def body(s_ref, w_ref, m_ref, v_ref, g_ref, d_ref, nm_ref, nv_ref):
        g = _sum_slots(s_ref).T
        g_ref[...] = g
        d_ref[...], nm_ref[...], nv_ref[...] = _adamw_math(g, w_ref[...], m_ref[...], v_ref[...])

    tile = pl.BlockSpec((lane_tile, rows), lambda i: (i, 0))
    return pl.pallas_call(
        body, name=name, grid=(cols // lane_tile,),
        in_specs=[pl.BlockSpec((n, rows, lane_tile), lambda i: (0, 0, i)), tile, tile, tile],
        out_specs=[tile] * 4, out_shape=[jax.ShapeDtypeStruct((cols, rows), F32)] * 4,
        compiler_params=_params(("parallel",), VMEM_LIMIT),
    )(slots, w_t, m_t, v_t)


def _reduce_slots(slot_arrays, name):
    def body(*refs):
        n_arr = len(refs) // 2
        for s_ref, o_ref in zip(refs[:n_arr], refs[n_arr:]):
            o_ref[...] = _sum_slots(s_ref)
    return pl.pallas_call(
        body, name=name, out_shape=[jax.ShapeDtypeStruct(s.shape[1:], F32) for s in slot_arrays],
        compiler_params=_params(None, VMEM_LIMIT))(*slot_arrays)


def _adamw_many(grads, ws, ms, vs, name):
    n = len(grads)

    def body(*refs):
        ins, outs = refs[:4 * n], refs[4 * n:]
        for i in range(n):
            g, w, m, v = (ins[j * n + i][...] for j in range(4))
            outs[i][...], outs[n + i][...], outs[2 * n + i][...] = _adamw_math(g, w, m, v)

    out = pl.pallas_call(
        body, name=name, out_shape=[jax.ShapeDtypeStruct(w.shape, F32) for w in ws] * 3,
        compiler_params=_params(None, VMEM_LIMIT))(*grads, *ws, *ms, *vs)
    return out[:n], out[n:2 * n], out[2 * n:]


def _disc_a(lam_re, lam_im, log_step):
    lr = jnp.minimum(lam_re, STEP_FLOOR)
    step = jnp.exp(log_step)
    mag = jnp.exp(lr * step)
    ang = lam_im * step
    abr = mag * jnp.cos(ang)
    abi = mag * jnp.sin(ang)
    nr = abr - 1.0
    den = lr * lr + lam_im * lam_im
    cr = (nr * lr + abi * lam_im) / den
    ci = (abi * lr - nr * lam_im) / den
    return abr, abi, cr, ci


def _disc_b(cr, ci, b_re, b_im):
    return cr * b_re - ci * b_im, cr * b_im + ci * b_re


def _s5_disc_a(lam_re, lam_im, log_step):
    def body(lr_ref, li_ref, ls_ref, *outs):
        for o, val in zip(outs, _disc_a(lr_ref[...], li_ref[...], ls_ref[...])):
            o[...] = val
    return pl.pallas_call(body, name="s5_disc_a", out_shape=[jax.ShapeDtypeStruct(lam_re.shape, F32)] * 4)(
        lam_re, lam_im, log_step)


def _s5_disc_a_bwd(lam_re, lam_im, log_step, cts):
    def body(lr_ref, li_ref, ls_ref, c0, c1, c2, c3, dlr_ref, dli_ref, dls_ref):
        _, vjp = jax.vjp(_disc_a, lr_ref[...], li_ref[...], ls_ref[...])
        dlr, dli, dls = vjp((c0[...], c1[...], c2[...], c3[...]))
        dlr_ref[...] = dlr
        dli_ref[...] = dli
        dls_ref[...] = dls
    return pl.pallas_call(
        body, name="s5_disc_a_bwd",
        out_shape=[jax.ShapeDtypeStruct(lam_re.shape, F32), jax.ShapeDtypeStruct(lam_re.shape, F32),
                   jax.ShapeDtypeStruct(log_step.shape, F32)])(lam_re, lam_im, log_step, *cts)


def _s5_disc_b(cr, ci, b_re, b_im):
    def body(cr_ref, ci_ref, br_ref, bi_ref, o_re, o_im):
        o_re[...], o_im[...] = _disc_b(cr_ref[...], ci_ref[...], br_ref[...], bi_ref[...])
    return pl.pallas_call(body, name="s5_disc_b", out_shape=[jax.ShapeDtypeStruct(b_re.shape, F32)] * 2)(
        cr, ci, b_re, b_im)


def _s5_disc_b_bwd(cr, ci, b_re, b_im, d_re, d_im):
    def body(cr_ref, ci_ref, br_ref, bi_ref, dr_ref, di_ref, dcr_ref, dci_ref, dbr_ref, dbi_ref):
        _, vjp = jax.vjp(_disc_b, cr_ref[...], ci_ref[...], br_ref[...], bi_ref[...])
        dcr_ref[...], dci_ref[...], dbr_ref[...], dbi_ref[...] = vjp((dr_ref[...], di_ref[...]))
    return pl.pallas_call(
        body, name="s5_disc_b_bwd",
        out_shape=[jax.ShapeDtypeStruct(cr.shape, F32)] * 2 + [jax.ShapeDtypeStruct(b_re.shape, F32)] * 2)(
            cr, ci, b_re, b_im, d_re, d_im)


def _cmul(ar, ai, br, bi):
    return ar * br - ai * bi, ar * bi + ai * br


def _cpow(ar, ai, n):
    rr, ri = jnp.ones_like(ar), jnp.zeros_like(ai)
    while n:
        if n & 1:
            rr, ri = _cmul(rr, ri, ar, ai)
        n >>= 1
        if n:
            ar, ai = _cmul(ar, ai, ar, ai)
    return rr, ri


def _tile_rows(i):
    if isinstance(i, int):
        return pl.ds(i * SUBLANES, SUBLANES)
    return pl.ds(pl.multiple_of(i * SUBLANES, SUBLANES), SUBLANES)


def _segment_scan(z_re, z_im, ar, ai, lseg, reverse, visit=None):
    shape = (SUBLANES, z_re.shape[1])
    half = lseg // 2
    arb = jnp.broadcast_to(ar, shape)
    aib = jnp.broadcast_to(ai, shape)
    zero = jnp.zeros(shape, F32)
    row = lax.broadcasted_iota(jnp.int32, shape, 0)

    def tiles(k):
        return (lseg - 1 - k, half - 1 - k) if reverse else (k, half + k)

    def advance(tile, sr, si):
        rows = _tile_rows(tile)
        nr, ni = _cmul(arb, aib, sr, si)
        return rows, nr + z_re[rows, :], ni + z_im[rows, :]

    def first_pass(k, carry):
        ta, tb = tiles(k)
        return advance(ta, carry[0], carry[1])[1:] + advance(tb, carry[2], carry[3])[1:]

    def unrolled(step):
        def body(it, carry):
            for j in range(SCAN_UNROLL):
                carry = step(it * SCAN_UNROLL + j, carry)
            return carry
        return body

    n_iter = half // SCAN_UNROLL
    fa_r, fa_i, fb_r, fb_i = lax.fori_loop(0, n_iter, unrolled(first_pass), (zero,) * 4)
    hr, hi = _cpow(arb, aib, half)
    pr, pi = _cmul(hr, hi, hr, hi)
    fr, fi = _cmul(hr, hi, fa_r, fa_i)
    fr, fi = fr + fb_r, fi + fb_i
    cr, ci = zero, zero
    for _ in range(SUBLANES - 1):
        tr, ti = _cmul(pr, pi, cr, ci)
        tr, ti = tr + fr, ti + fi
        if reverse:
            cr = jnp.where(row == SUBLANES - 1, 0.0, pltpu.roll(tr, SUBLANES - 1, 0))
            ci = jnp.where(row == SUBLANES - 1, 0.0, pltpu.roll(ti, SUBLANES - 1, 0))
        else:
            cr = jnp.where(row == 0, 0.0, pltpu.roll(tr, 1, 0))
            ci = jnp.where(row == 0, 0.0, pltpu.roll(ti, 1, 0))

    br, bi = _cmul(hr, hi, cr, ci)
    br, bi = br + fa_r, bi + fa_i

    def second_pass(k, carry, b_is_tile0=False):
        states, acc = list(carry[:4]), carry[4]
        for chain, tile in enumerate(tiles(k)):
            rows, nr, ni = advance(tile, states[2 * chain], states[2 * chain + 1])
            z_re[rows, :] = nr
            z_im[rows, :] = ni
            states[2 * chain], states[2 * chain + 1] = nr, ni
            if visit is not None:
                acc = visit(tile, nr, ni, acc, chain == 1 and b_is_tile0)
        return (*states, acc)

    acc0 = (zero, zero) if visit is not None else 0
    carry = lax.fori_loop(0, n_iter - 1, unrolled(second_pass), (cr, ci, br, bi, acc0))
    for k in range(half - SCAN_UNROLL, half - 1):
        carry = second_pass(k, carry)
    return second_pass(half - 1, carry, b_is_tile0=reverse)[4]


def _gelu(y):
    c = 0.7978845608028654
    return 0.5 * y * (1.0 + jnp.tanh(c * (y + 0.044715 * y * y * y)))


def _gelu_grad(y):
    c = 0.7978845608028654
    th = jnp.tanh(c * (y + 0.044715 * y * y * y))
    return 0.5 * (1.0 + th) + 0.5 * y * (1.0 - th * th) * c * (1.0 + 3.0 * 0.044715 * y * y)


def _s5_specs(lp):
    col_u = pl.BlockSpec((lp, COL_U), lambda j: (0, j))
    row_u = pl.BlockSpec((1, COL_U), lambda j: (0, j))
    row_s = pl.BlockSpec((1, COL_S), lambda j: (0, j))
    bc_blk = pl.BlockSpec((GROUPS_PER_COL, SSM_GROUP, SSM_STATE), lambda j: (j, 0, 0))
    glu_blk = pl.BlockSpec((GROUPS_PER_COL, SSM_GROUP, SSM_GROUP), lambda j: (j, 0, 0))
    return col_u, row_u, row_s, bc_blk, glu_blk


def _s5_block_diag_scratch():
    return ([pltpu.VMEM((COL_U, COL_S), BF16)] * 4 + [pltpu.VMEM((COL_U, COL_U), BF16)]
            + [pltpu.VMEM((COL_U, COL_S), F32)])


def _fill_block_diag(bd_ref, blocks_ref, stage):
    r, c = blocks_ref.shape[1:]
    stage[...] = jnp.zeros(stage.shape, F32)
    for gl in range(GROUPS_PER_COL):
        stage[pl.ds(gl * r, r), pl.ds(gl * c, c)] = blocks_ref[gl]
    bd_ref[...] = stage[:, :GROUPS_PER_COL * c].astype(BF16)


def _take_block_diag(out_ref, mat):
    r, c = out_ref.shape[1:]
    for gl in range(GROUPS_PER_COL):
        out_ref[gl] = mat[gl * r:(gl + 1) * r, gl * c:(gl + 1) * c]


def _s5_fill_states(u_ref, bre_ref, bim_ref, ar_ref, ai_ref, s_re, s_im, lseg, n_chunks, chunk):
    def fill(cidx, carry):
        rows = pl.ds(pl.multiple_of(cidx * chunk, SUBLANES), chunk)
        ub = u_ref[rows, :].astype(BF16)
        s_re[rows, :] = jnp.dot(ub, bre_ref[...], preferred_element_type=F32)
        s_im[rows, :] = jnp.dot(ub, bim_ref[...], preferred_element_type=F32)
        return carry
    lax.fori_loop(0, n_chunks, fill, 0)
    _segment_scan(s_re, s_im, ar_ref[...], ai_ref[...], lseg, reverse=False)


def _s5_forward(u_p, ar, ai, bbr, bbi, c_re, c_im, d_row, glu_w, glub_row):
    lp = u_p.shape[0]
    lseg = lp // SUBLANES
    chunk, n_chunks = lseg, SUBLANES

    def body(u_ref, ar_ref, ai_ref, bbr_ref, bbi_ref, cr_ref, ci_ref, d_ref, gw_ref, glub_ref,
             ys_ref, s_re, s_im, bre_ref, bim_ref, cre_ref, cim_ref, glu_ref, stage):
        for bd, blocks in ((bre_ref, bbr_ref), (bim_ref, bbi_ref), (cre_ref, cr_ref), (cim_ref, ci_ref),
                           (glu_ref, gw_ref)):
            _fill_block_diag(bd, blocks, stage)
        _s5_fill_states(u_ref, bre_ref, bim_ref, ar_ref, ai_ref, s_re, s_im, lseg, n_chunks, chunk)

        def emit(cidx, carry):
            rows = pl.ds(pl.multiple_of(cidx * chunk, SUBLANES), chunk)
            y = (_dot_nt(s_re[rows, :], cre_ref[...]) - _dot_nt(s_im[rows, :], cim_ref[...])
                 + d_ref[...] * u_ref[rows, :])
            g = _gelu(y)
            gate = _dot(g, glu_ref[...]) + glub_ref[...]
            ys_ref[rows, :] = g * _sigmoid(gate)
            return carry
        lax.fori_loop(0, n_chunks, emit, 0)

    col_u, row_u, row_s, bc_blk, glu_blk = _s5_specs(lp)
    return pl.pallas_call(
        body, name="s5_forward", grid=(N_COL,),
        in_specs=[col_u, row_s, row_s, bc_blk, bc_blk, bc_blk, bc_blk, row_u, glu_blk, row_u],
        out_specs=col_u, out_shape=jax.ShapeDtypeStruct((lp, D_SSM), F32),
        scratch_shapes=[pltpu.VMEM((lp, COL_S), F32), pltpu.VMEM((lp, COL_S), F32)] + _s5_block_diag_scratch(),
        compiler_params=_params(("arbitrary",), VMEM_LIMIT),
    )(u_p, ar, ai, bbr, bbi, c_re, c_im, d_row, glu_w, glub_row)


def _s5_backward(u_p, dys_p, ar, ai, bbr, bbi, c_re, c_im, d_row, glu_w, glub_row):
    lp = u_p.shape[0]
    lseg = lp // SUBLANES
    chunk, n_chunks = lseg, SUBLANES

    def body(u_ref, dys_ref, ar_ref, ai_ref, bbr_ref, bbi_ref, cr_ref, ci_ref, d_ref, gw_ref, glub_ref,
             du_ref, dar_ref, dai_ref, dbbr_ref, dbbi_ref, dcr_ref, dci_ref, dd_ref, dgw_ref, dglub_ref,
             s_re, s_im, q_re, q_im, bre_ref, bim_ref, cre_ref, cim_ref, glu_ref, stage,
             dbre_ref, dbim_ref, dcre_ref, dcim_ref, dglu_ref):
        for bd, blocks in ((bre_ref, bbr_ref), (bim_ref, bbi_ref), (cre_ref, cr_ref), (cim_ref, ci_ref),
                           (glu_ref, gw_ref)):
            _fill_block_diag(bd, blocks, stage)
        _s5_fill_states(u_ref, bre_ref, bim_ref, ar_ref, ai_ref, s_re, s_im, lseg, n_chunks, chunk)
        for ref in (dcre_ref, dcim_ref, dd_ref, dglu_ref, dglub_ref, dbre_ref, dbim_ref):
            ref[...] = jnp.zeros(ref.shape, F32)

        def mixer_bwd(cidx, carry):
            rows = pl.ds(pl.multiple_of(cidx * chunk, SUBLANES), chunk)
            u = u_ref[rows, :]
            sr, si = s_re[rows, :], s_im[rows, :]
            y = _dot_nt(sr, cre_ref[...]) - _dot_nt(si, cim_ref[...]) + d_ref[...] * u
            g = _gelu(y)
            sg = _sigmoid(_dot(g, glu_ref[...]) + glub_ref[...])
            dout = dys_ref[rows, :]
            dgate = dout * g * sg * (1.0 - sg)
            dy = (dout * sg + _dot_nt(dgate, glu_ref[...])) * _gelu_grad(y)
            dglu_ref[...] += _dot_tn(g, dgate)
            dglub_ref[...] += jnp.sum(dgate, axis=0, keepdims=True)
            dd_ref[...] += jnp.sum(dy * u, axis=0, keepdims=True)
            dcre_ref[...] += _dot_tn(dy, sr)
            dcim_ref[...] -= _dot_tn(dy, si)
            q_re[rows, :] = _dot(dy, cre_ref[...])
            q_im[rows, :] = -_dot(dy, cim_ref[...])
            du_ref[rows, :] = d_ref[...] * dy
            return carry
        lax.fori_loop(0, n_chunks, mixer_bwd, 0)

        row = lax.broadcasted_iota(jnp.int32, (SUBLANES, COL_S), 0)

        def visit(i, qr, qi, acc, is_tile0):
            if is_tile0:
                prev = _tile_rows(lseg - 1)
                pr = jnp.where(row == 0, 0.0, pltpu.roll(s_re[prev, :], 1, 0))
                pi = jnp.where(row == 0, 0.0, pltpu.roll(s_im[prev, :], 1, 0))
            else:
                prev = _tile_rows(i - 1)
                pr, pi = s_re[prev, :], s_im[prev, :]
            return acc[0] + qr * pr + qi * pi, acc[1] + qi * pr - qr * pi

        dar, dai = _segment_scan(q_re, q_im, ar_ref[...], -ai_ref[...], lseg, reverse=True, visit=visit)
        dar_ref[...] = jnp.sum(dar, axis=0, keepdims=True)
        dai_ref[...] = jnp.sum(dai, axis=0, keepdims=True)

        def input_bwd(cidx, carry):
            rows = pl.ds(pl.multiple_of(cidx * chunk, SUBLANES), chunk)
            qr, qi = q_re[rows, :], q_im[rows, :]
            u = u_ref[rows, :]
            du_ref[rows, :] += _dot_nt(qr, bre_ref[...]) + _dot_nt(qi, bim_ref[...])
            dbre_ref[...] += _dot_tn(u, qr)
            dbim_ref[...] += _dot_tn(u, qi)
            return carry
        lax.fori_loop(0, n_chunks, input_bwd, 0)
        for out, acc in ((dbbr_ref, dbre_ref), (dbbi_ref, dbim_ref), (dcr_ref, dcre_ref), (dci_ref, dcim_ref),
                         (dgw_ref, dglu_ref)):
            _take_block_diag(out, acc[...])

    col_u, row_u, row_s, bc_blk, glu_blk = _s5_specs(lp)
    group_mats = jax.ShapeDtypeStruct((SSM_GROUPS, SSM_GROUP, SSM_STATE), F32)
    return pl.pallas_call(
        body, name="s5_backward", grid=(N_COL,),
        in_specs=[col_u, col_u, row_s, row_s, bc_blk, bc_blk, bc_blk, bc_blk, row_u, glu_blk, row_u],
        out_specs=[col_u, row_s, row_s, bc_blk, bc_blk, bc_blk, bc_blk, row_u, glu_blk, row_u],
        out_shape=[jax.ShapeDtypeStruct((lp, D_SSM), F32),
                   jax.ShapeDtypeStruct((1, N_COL * COL_S), F32), jax.ShapeDtypeStruct((1, N_COL * COL_S), F32),
                   group_mats, group_mats, group_mats, group_mats, jax.ShapeDtypeStruct((1, D_SSM), F32),
                   jax.ShapeDtypeStruct((SSM_GROUPS, SSM_GROUP, SSM_GROUP), F32), jax.ShapeDtypeStruct((1, D_SSM), F32)],
        scratch_shapes=([pltpu.VMEM((lp, COL_S), F32)] * 4 + _s5_block_diag_scratch()
                        + [pltpu.VMEM((COL_U, COL_S), F32)] * 4 + [pltpu.VMEM((COL_U, COL_U), F32)]),
        compiler_params=_params(("arbitrary",), VMEM_LIMIT),
    )(u_p, dys_p, ar, ai, bbr, bbi, c_re, c_im, d_row, glu_w, glub_row)


def _window_sum(ext, group, leading):
    n = ext.shape[0]
    s = ext
    for j in range(POOL_GROUPS):
        shift = n - (1 << j) if leading else 1 << j
        s = jnp.where(j <= group, s + pltpu.roll(s, shift, 0), s)
    return s


def _pool_inv_count(tile, window, first_row):
    t = tile * TM + lax.broadcasted_iota(jnp.int32, (TM, 1), 0) - first_row
    return 1.0 / jnp.clip(t + 1, 1, window).astype(F32)


def _pool_specs(lp):
    col = pl.BlockSpec((lp, POOL_DIM), lambda k: (0, k))
    mat = pl.BlockSpec((None, POOL_DIM, POOL_DIM), lambda k: (k, 0, 0))
    row = pl.BlockSpec((None, 1, POOL_DIM), lambda k: (k, 0, 0))
    return col, mat, row


def _pool_forward(v, pool_w, pool_scale, first_row):
    lp = v.shape[0]
    n_tiles = lp // TM

    def body(v_ref, w_ref, sc_ref, yp_ref, vpad):
        group = pl.program_id(0)
        window = jnp.left_shift(2, group)
        vpad[pl.ds(0, POOL_HALO), :] = jnp.zeros((POOL_HALO, POOL_DIM), F32)
        vpad[pl.ds(POOL_HALO, lp), :] = v_ref[...]

        def tile(j, carry):
            start = pl.multiple_of(j * TM, TM)
            ext = vpad[pl.ds(start, TM + POOL_HALO), :]
            sums = _window_sum(ext, group, leading=False)[POOL_HALO:, :]
            p = sums * _pool_inv_count(j, window, first_row) - ext[POOL_HALO:, :]
            yp_ref[pl.ds(start, TM), :] = _dot(p, w_ref[...]) * sc_ref[...]
            return carry
        lax.fori_loop(0, n_tiles, tile, 0)

    col, mat, row = _pool_specs(lp)
    return pl.pallas_call(
        body, name="pool_forward", grid=(POOL_GROUPS,),
        in_specs=[col, mat, row], out_specs=col, out_shape=jax.ShapeDtypeStruct((lp, D_SSM), F32),
        scratch_shapes=[pltpu.VMEM((lp + POOL_HALO, POOL_DIM), F32)],
        compiler_params=_params(("arbitrary",), VMEM_LIMIT),
    )(v, pool_w, pool_scale)


def _pool_backward(v, dyp, pool_w, pool_scale, first_row):
    lp = v.shape[0]
    n_tiles = lp // TM

    def body(v_ref, dyp_ref, w_ref, sc_ref, dv_ref, dw_ref, dsc_ref, vpad, gpad):
        group = pl.program_id(0)
        window = jnp.left_shift(2, group)
        vpad[pl.ds(0, POOL_HALO), :] = jnp.zeros((POOL_HALO, POOL_DIM), F32)
        vpad[pl.ds(POOL_HALO, lp), :] = v_ref[...]
        gpad[pl.ds(lp, POOL_HALO), :] = jnp.zeros((POOL_HALO, POOL_DIM), F32)
        dw_ref[...] = jnp.zeros(dw_ref.shape, F32)
        dsc_ref[...] = jnp.zeros(dsc_ref.shape, F32)

        def linear_bwd(j, carry):
            start = pl.multiple_of(j * TM, TM)
            ext = vpad[pl.ds(start, TM + POOL_HALO), :]
            inv = _pool_inv_count(j, window, first_row)
            p = _window_sum(ext, group, leading=False)[POOL_HALO:, :] * inv - ext[POOL_HALO:, :]
            z = _dot(p, w_ref[...])
            dyp_t = dyp_ref[pl.ds(start, TM), :]
            dz = dyp_t * sc_ref[...]
            dsc_ref[...] += jnp.sum(dyp_t * z, axis=0, keepdims=True)
            dw_ref[...] += _dot_tn(p, dz)
            dp = _dot_nt(dz, w_ref[...])
            gpad[pl.ds(start, TM), :] = dp * inv
            dv_ref[pl.ds(start, TM), :] = -dp
            return carry
        lax.fori_loop(0, n_tiles, linear_bwd, 0)

        def window_bwd(j, carry):
            start = pl.multiple_of(j * TM, TM)
            ext = gpad[pl.ds(start, TM + POOL_HALO), :]
            dv_ref[pl.ds(start, TM), :] += _window_sum(ext, group, leading=True)[:TM, :]
            return carry
        lax.fori_loop(0, n_tiles, window_bwd, 0)

    col, mat, row = _pool_specs(lp)
    return pl.pallas_call(
        body, name="pool_backward", grid=(POOL_GROUPS,),
        in_specs=[col, col, mat, row], out_specs=[col, mat, row],
        out_shape=[jax.ShapeDtypeStruct((lp, D_SSM), F32),
                   jax.ShapeDtypeStruct((POOL_GROUPS, POOL_DIM, POOL_DIM), F32),
                   jax.ShapeDtypeStruct((POOL_GROUPS, 1, POOL_DIM), F32)],
        scratch_shapes=[pltpu.VMEM((lp + POOL_HALO, POOL_DIM), F32)] * 2,
        compiler_params=_params(("arbitrary",), VMEM_LIMIT),
    )(v, dyp, pool_w, pool_scale)


def _row_specs():
    head = _const((HEAD, D_MODEL))
    xrow = pl.BlockSpec((TM, D_MODEL), _xrow)
    full = pl.BlockSpec((TM, D_MODEL), lambda i: (i, 0))
    half = pl.BlockSpec((TM, D_SSM), lambda i: (i, 0))
    return head, xrow, full, half


def _in_proj(head, x, g1, w_in):
    n_tiles = (HEAD + x.shape[0]) // TM
    lp = n_tiles * TM

    def body(head_ref, x_ref, g_ref, w_ref, u_ref, v_ref):
        h0 = jnp.where(pl.program_id(0) == 0, head_ref[...], x_ref[...])
        proj = _dot(h0 * _rstd(h0) * g_ref[...], w_ref[...])
        u_ref[...] = proj[:, :D_SSM]
        v_ref[...] = proj[:, D_SSM:]

    head_s, xrow, _, half = _row_specs()
    return pl.pallas_call(
        body, name="in_proj", grid=(n_tiles,),
        in_specs=[head_s, xrow, _const((1, D_MODEL)), _const((D_MODEL, D_MODEL))],
        out_specs=[half, half], out_shape=[jax.ShapeDtypeStruct((lp, D_SSM), F32)] * 2,
        compiler_params=_params(("parallel",), VMEM_LIMIT),
    )(head, x, g1, w_in)


def _out_proj(head, x, ys, yp, gs, gp, w_out):
    lp = ys.shape[0]

    def body(head_ref, x_ref, ys_ref, yp_ref, gs_ref, gp_ref, w_ref, h1_ref):
        h0 = jnp.where(pl.program_id(0) == 0, head_ref[...], x_ref[...])
        ys_t, yp_t = ys_ref[...], yp_ref[...]
        ms = ys_t * _rstd(ys_t) * gs_ref[...]
        mp = yp_t * _rstd(yp_t) * gp_ref[...]
        h1_ref[...] = h0 + _dot(ms, w_ref[pl.ds(0, D_SSM), :]) + _dot(mp, w_ref[pl.ds(D_SSM, D_SSM), :])

    head_s, xrow, full, half = _row_specs()
    return pl.pallas_call(
        body, name="out_proj", grid=(lp // TM,),
        in_specs=[head_s, xrow, half, half, _const((1, D_SSM)), _const((1, D_SSM)), _const((D_MODEL, D_MODEL))],
        out_specs=full, out_shape=jax.ShapeDtypeStruct((lp, D_MODEL), F32),
        compiler_params=_params(("parallel",), VMEM_LIMIT),
    )(head, x, ys, yp, gs, gp, w_out)


def _load_weights(c_hbm, wd_hbm, c_vmem, wd_vmem, sems):
    @pl.when(pl.program_id(0) == 0)
    def _():
        copies = [pltpu.make_async_copy(c_hbm, c_vmem, sems.at[0]),
                  pltpu.make_async_copy(wd_hbm, wd_vmem, sems.at[1])]
        for cp in copies:
            cp.start()
        for cp in copies:
            cp.wait()


def _ffn_scratch():
    return [pltpu.VMEM((N_DEV, D_MODEL, 2 * FF_PAD), BF16), pltpu.VMEM((N_DEV, FF_PAD, D_MODEL), BF16),
            pltpu.SemaphoreType.DMA((2,))]


def _ffn_forward(h1, g2, c_all, wd_all):
    lp = h1.shape[0]

    def body(h1_ref, g_ref, c_hbm, wd_hbm, ab_ref, n2t_ref, h2_ref, c_vmem, wd_vmem, sems):
        _load_weights(c_hbm, wd_hbm, c_vmem, wd_vmem, sems)
        h1_t = h1_ref[...]
        n2_f = h1_t * _rstd(h1_t) * g_ref[...]
        n2 = n2_f.astype(BF16)
        n2t_ref[...] = n2_f.T.astype(BF16)
        acc = h1_t
        for j in range(N_DEV):
            ab = jnp.dot(n2, c_vmem[j], preferred_element_type=F32)
            a, b = ab[:, :FF_PAD], ab[:, FF_PAD:]
            ab_ref[:, pl.ds(j * 2 * FF_PAD, 2 * FF_PAD)] = ab.astype(BF16)
            acc = acc + _dot(a * _sigmoid(a) * b, wd_vmem[j])
        h2_ref[...] = acc

    _, _, full, _ = _row_specs()
    wide = pl.BlockSpec((TM, N_DEV * 2 * FF_PAD), lambda i: (i, 0))
    return pl.pallas_call(
        body, name="ffn_forward", grid=(lp // TM,),
        in_specs=[full, _const((1, D_MODEL)), ANY, ANY],
        out_specs=[wide, pl.BlockSpec((D_MODEL, TM), lambda i: (0, i)), full],
        out_shape=[jax.ShapeDtypeStruct((lp, N_DEV * 2 * FF_PAD), BF16),
                   jax.ShapeDtypeStruct((D_MODEL, lp), BF16), jax.ShapeDtypeStruct((lp, D_MODEL), F32)],
        scratch_shapes=_ffn_scratch(), compiler_params=_params(("arbitrary",), VMEM_LIMIT),
    )(h1, g2, c_all, wd_all)


def _ffn_backward(h2, target, h1, ab, gf, g2, c_all, wd_all):
    lp = h1.shape[0]

    def body(h2_ref, t_ref, h1_ref, ab_ref, gf_ref, g2_ref, c_hbm, wd_hbm,
             dh1_ref, dab_ref, dh2_ref, loss_ref, dgf_ref, dg2_ref, c_vmem, wd_vmem, sems):
        i = pl.program_id(0)
        _load_weights(c_hbm, wd_hbm, c_vmem, wd_vmem, sems)

        @pl.when(i == 0)
        def _():
            loss_ref[...] = jnp.zeros(loss_ref.shape, F32)
            dgf_ref[...] = jnp.zeros(dgf_ref.shape, F32)
            dg2_ref[...] = jnp.zeros(dg2_ref.shape, F32)

        h2_t = h2_ref[...]
        rf = _rstd(h2_t)
        xf = h2_t * rf
        diff = jnp.where(i == 0, 0.0, xf * gf_ref[...] - t_ref[...])
        loss_ref[...] += 0.5 * jnp.sum(diff * diff) / D_MODEL
        dh2, dgf = _rms_bwd(diff / D_MODEL, xf, rf, gf_ref[...])
        dgf_ref[...] += dgf
        dh2_b = dh2.astype(BF16)
        dh2_ref[...] = dh2_b

        dn2 = jnp.zeros((TM, D_MODEL), F32)
        for j in range(N_DEV):
            cols = pl.ds(j * 2 * FF_PAD, 2 * FF_PAD)
            dff = _dot_nt(dh2_b, wd_vmem[j])
            ab_t = ab_ref[:, cols].astype(F32)
            a, b = ab_t[:, :FF_PAD], ab_t[:, FF_PAD:]
            sg = _sigmoid(a)
            dab_ref[:, pl.ds(j * 2 * FF_PAD, FF_PAD)] = (dff * b * sg * (1.0 + a * (1.0 - sg))).astype(BF16)
            dab_ref[:, pl.ds(j * 2 * FF_PAD + FF_PAD, FF_PAD)] = (dff * a * sg).astype(BF16)
            dn2 = dn2 + _dot_nt(dab_ref[:, cols], c_vmem[j])

        h1_t = h1_ref[...]
        r2 = _rstd(h1_t)
        dx, dg2 = _rms_bwd(dn2, h1_t * r2, r2, g2_ref[...])
        dg2_ref[...] += dg2
        dh1_ref[...] = dh2 + dx

    _, xrow, full, _ = _row_specs()
    wide = pl.BlockSpec((TM, N_DEV * 2 * FF_PAD), lambda i: (i, 0))
    vec = _const((1, D_MODEL))
    return pl.pallas_call(
        body, name="ffn_backward", grid=(lp // TM,),
        in_specs=[full, xrow, full, wide, vec, vec, ANY, ANY],
        out_specs=[full, wide, full, _const((1, PACK_LANES)), vec, vec],
        out_shape=[jax.ShapeDtypeStruct((lp, D_MODEL), F32),
                   jax.ShapeDtypeStruct((lp, N_DEV * 2 * FF_PAD), BF16),
                   jax.ShapeDtypeStruct((lp, D_MODEL), BF16),
                   jax.ShapeDtypeStruct((1, PACK_LANES), F32),
                   jax.ShapeDtypeStruct((1, D_MODEL), F32), jax.ShapeDtypeStruct((1, D_MODEL), F32)],
        scratch_shapes=_ffn_scratch(), compiler_params=_params(("arbitrary",), VMEM_LIMIT),
    )(h2, target, h1, ab, gf, g2, c_all, wd_all)


def _ffn_wgrad(n2t, dh2, ab, dab):
    lp = dh2.shape[0]
    rows = lp // WGRAD_STEPS

    def body(n2t_ref, dh2_ref, ab_ref, dab_ref, dc_ref, dwd_ref, dc_acc, dwd_acc):
        i = pl.program_id(1)

        @pl.when(i == 0)
        def _():
            dc_acc[...] = jnp.zeros(dc_acc.shape, F32)
            dwd_acc[...] = jnp.zeros(dwd_acc.shape, F32)

        ab_t = ab_ref[...].astype(F32)
        a, b = ab_t[:, :FF_PAD], ab_t[:, FF_PAD:]
        dc_acc[...] += jnp.dot(n2t_ref[...], dab_ref[...], preferred_element_type=F32)
        dwd_acc[...] += _dot_tn(a * _sigmoid(a) * b, dh2_ref[...])

        @pl.when(i == pl.num_programs(1) - 1)
        def _():
            dc_ref[...] = dc_acc[...].astype(BF16)
            dwd_ref[...] = dwd_acc[...].astype(BF16)

    act_t = pl.BlockSpec((D_MODEL, rows), lambda j, i: (0, i))
    act = pl.BlockSpec((rows, D_MODEL), lambda j, i: (i, 0))
    shard = pl.BlockSpec((rows, 2 * FF_PAD), lambda j, i: (i, j))
    return pl.pallas_call(
        body, name="ffn_wgrad", grid=(N_DEV, WGRAD_STEPS),
        in_specs=[act_t, act, shard, shard],
        out_specs=[pl.BlockSpec((None, D_MODEL, 2 * FF_PAD), lambda j, i: (j, 0, 0)),
                   pl.BlockSpec((None, FF_PAD, D_MODEL), lambda j, i: (j, 0, 0))],
        out_shape=[jax.ShapeDtypeStruct((N_DEV, D_MODEL, 2 * FF_PAD), BF16),
                   jax.ShapeDtypeStruct((N_DEV, FF_PAD, D_MODEL), BF16)],
        scratch_shapes=[pltpu.VMEM((D_MODEL, 2 * FF_PAD), F32), pltpu.VMEM((FF_PAD, D_MODEL), F32)],
        compiler_params=_params(("parallel", "arbitrary"), VMEM_LIMIT),
    )(n2t, dh2, ab, dab)


def _out_proj_backward(dh1, ys, yp, gs, gp, w_out):
    lp = ys.shape[0]

    def body(dh1_ref, ys_ref, yp_ref, gs_ref, gp_ref, w_ref, dys_ref, dyp_ref, dgs_ref, dgp_ref, dw_out, dw_ref):
        @pl.when(pl.program_id(0) == 0)
        def _():
            dgs_ref[...] = jnp.zeros(dgs_ref.shape, F32)
            dgp_ref[...] = jnp.zeros(dgp_ref.shape, F32)
            dw_ref[...] = jnp.zeros(dw_ref.shape, F32)

        dh1_b = dh1_ref[...].astype(BF16)
        dmix = _dot_nt(dh1_b, w_ref[...])
        for y_ref, g_ref, dy_ref, dg_ref, lo in ((ys_ref, gs_ref, dys_ref, dgs_ref, 0),
                                                 (yp_ref, gp_ref, dyp_ref, dgp_ref, D_SSM)):
            y_t = y_ref[...]
            r = _rstd(y_t)
            xhat = y_t * r
            dy, dg = _rms_bwd(dmix[:, lo:lo + D_SSM], xhat, r, g_ref[...])
            dy_ref[...] = dy
            dg_ref[...] += dg
            dw_ref[pl.ds(lo, D_SSM), :] += _dot_tn(xhat * g_ref[...], dh1_b)

        @pl.when(pl.program_id(0) == pl.num_programs(0) - 1)
        def _():
            dw_out[...] = dw_ref[...].astype(BF16)

    _, _, full, half = _row_specs()
    vec = _const((1, D_SSM))
    return pl.pallas_call(
        body, name="out_proj_backward", grid=(lp // TM,),
        in_specs=[full, half, half, vec, vec, _const((D_MODEL, D_MODEL))],
        out_specs=[half, half, vec, vec, _const((D_MODEL, D_MODEL))],
        out_shape=[jax.ShapeDtypeStruct((lp, D_SSM), F32)] * 2 + [jax.ShapeDtypeStruct((1, D_SSM), F32)] * 2
        + [jax.ShapeDtypeStruct((D_MODEL, D_MODEL), BF16)],
        scratch_shapes=[pltpu.VMEM((D_MODEL, D_MODEL), F32)],
        compiler_params=_params(("arbitrary",), VMEM_LIMIT),
    )(dh1, ys, yp, gs, gp, w_out)


def _in_proj_backward(head, x, du, dv, dh1, g1, w_in):
    lp = du.shape[0]

    def body(head_ref, x_ref, du_ref, dv_ref, dh1_ref, g_ref, w_ref, dx_ref, dhead_ref, dg_ref, dw_out, dw_ref):
        i = pl.program_id(0)

        @pl.when(i == 0)
        def _():
            dg_ref[...] = jnp.zeros(dg_ref.shape, F32)
            dw_ref[...] = jnp.zeros(dw_ref.shape, F32)

        h0 = jnp.where(i == 0, head_ref[...], x_ref[...])
        r = _rstd(h0)
        xhat = h0 * r
        n1 = (xhat * g_ref[...]).astype(BF16)
        du_b, dv_b = du_ref[...].astype(BF16), dv_ref[...].astype(BF16)
        dn1 = _dot_nt(du_b, w_ref[:, pl.ds(0, D_SSM)]) + _dot_nt(dv_b, w_ref[:, pl.ds(D_SSM, D_SSM)])
        dx, dg = _rms_bwd(dn1, xhat, r, g_ref[...])
        dg_ref[...] += dg
        dh0 = dh1_ref[...] + dx
        dx_ref[...] = dh0

        @pl.when(i == 0)
        def _():
            dhead_ref[...] = dh0

        dw_ref[:, pl.ds(0, D_SSM)] += _dot_tn(n1, du_b)
        dw_ref[:, pl.ds(D_SSM, D_SSM)] += _dot_tn(n1, dv_b)

        @pl.when(i == pl.num_programs(0) - 1)
        def _():
            dw_out[...] = dw_ref[...].astype(BF16)

    head_s, xrow, full, half = _row_specs()
    vec = _const((1, D_MODEL))
    mat = _const((D_MODEL, D_MODEL))
    return pl.pallas_call(
        body, name="in_proj_backward", grid=(lp // TM,),
        in_specs=[head_s, xrow, half, half, full, vec, mat],
        out_specs=[xrow, head_s, vec, mat],
        out_shape=[jax.ShapeDtypeStruct(x.shape, F32), jax.ShapeDtypeStruct((HEAD, D_MODEL), F32),
                   jax.ShapeDtypeStruct((1, D_MODEL), F32), jax.ShapeDtypeStruct((D_MODEL, D_MODEL), BF16)],
        scratch_shapes=[pltpu.VMEM((D_MODEL, D_MODEL), F32)],
        compiler_params=_params(("arbitrary",), VMEM_LIMIT),
    )(head, x, du, dv, dh1, g1, w_in)


def _permute_rows(a):
    lp, n = a.shape
    return a.reshape(SUBLANES, lp // SUBLANES, n).transpose(1, 0, 2).reshape(lp, n)


def _unpermute_rows(a):
    lp, n = a.shape
    return a.reshape(lp // SUBLANES, SUBLANES, n).transpose(1, 0, 2).reshape(lp, n)


def _pack(parts, dtype):
    rows = []
    for p in parts:
        flat = p.reshape(-1).astype(dtype)
        pad = (-flat.shape[0]) % PACK_UNIT
        rows.append(jnp.pad(flat, (0, pad)).reshape(-1, PACK_LANES))
    n_rows = sum(r.shape[0] for r in rows)
    if n_rows % 16:
        rows.append(jnp.zeros((8, PACK_LANES), dtype))
    return jnp.concatenate(rows, axis=0)


def _as2d(a):
    return a.reshape(-1, a.shape[-1])


def _unpack(packed, shapes):
    out, row = [], 0
    for shape in shapes:
        size = 1
        for s in shape:
            size *= s
        n_rows = -(-size // PACK_UNIT) * 8
        out.append(packed[row:row + n_rows].reshape(-1)[:size].reshape(shape))
        row += n_rows
    return out


def _pad_cols(a):
    return jnp.pad(a, ((0, 0), (0, FF_PAD - FF_SHARD)))


def _pad_rows(a):
    return jnp.pad(a, ((0, FF_PAD - FF_SHARD), (0, 0)))


def _gate_up(gate, up):
    return jnp.concatenate([_pad_cols(gate), _pad_cols(up)], axis=1)


def _gate_up_t(gate, up):
    return jnp.concatenate([_pad_rows(gate[0].T), _pad_rows(up[0].T)], axis=0)


def _to_view(name, a):
    if name in ("ssm_b_re", "ssm_b_im"):
        return a[0].transpose(0, 2, 1).reshape(-1, SSM_STATE)
    if name in ("ssm_d", "ssm_glu_b"):
        return a[0].T
    if name == "ssm_glu_w":
        return a[0].transpose(1, 2, 0).reshape(-1, SSM_GROUPS)
    return _as2d(a)


def _from_view(name, r, shape):
    if name in ("ssm_b_re", "ssm_b_im"):
        return r.reshape(SSM_GROUPS, SSM_GROUP, SSM_STATE).transpose(0, 2, 1).reshape(shape)
    if name in ("ssm_d", "ssm_glu_b"):
        return r.T.reshape(shape)
    if name == "ssm_glu_w":
        return r.reshape(SSM_GROUP, SSM_GROUP, SSM_GROUPS).transpose(2, 0, 1).reshape(shape)
    return r.reshape(shape)


def kernel(x, meta_tokens, norm1_g, w_in, ssm_lambda_re, ssm_lambda_im, ssm_log_step, ssm_b_re, ssm_b_im, ssm_c_re, ssm_c_im, ssm_d, ssm_glu_w, ssm_glu_b, ssm_norm_g, pool_w, pool_scale, pool_norm_g, w_out, norm2_g, w_gate, w_up, w_down, final_norm_g, loss_target, m_meta_tokens, m_norm1_g, m_w_in, m_ssm_lambda_re, m_ssm_lambda_im, m_ssm_log_step, m_ssm_b_re, m_ssm_b_im, m_ssm_c_re, m_ssm_c_im, m_ssm_d, m_ssm_glu_w, m_ssm_glu_b, m_ssm_norm_g, m_pool_w, m_pool_scale, m_pool_norm_g, m_w_out, m_norm2_g, m_w_gate, m_w_up, m_w_down, m_final_norm_g, v_meta_tokens, v_norm1_g, v_w_in, v_ssm_lambda_re, v_ssm_lambda_im, v_ssm_log_step, v_ssm_b_re, v_ssm_b_im, v_ssm_c_re, v_ssm_c_im, v_ssm_d, v_ssm_glu_w, v_ssm_glu_b, v_ssm_norm_g, v_pool_w, v_pool_scale, v_pool_norm_g, v_w_out, v_norm2_g, v_w_gate, v_w_up, v_w_down, v_final_norm_g):
    given = dict(locals())
    weights = {n: given[n] for n in WEIGHT_NAMES}
    n_meta = meta_tokens.shape[0]
    me = 4 * lax.axis_index("x") + 2 * lax.axis_index("y") + lax.axis_index("c")

    shard_rows = w_in.shape[1]
    first = [w_in[0].astype(BF16), meta_tokens]
    first_make = _push_copies(ALL_PEERS, [False, False])
    first_x, first_token = _split_start(first + [_landing(s, False) for s in first], first_make,
                                        2 * len(ALL_PEERS), "gather_w_in_start")

    xs = x[0]
    tgt = loss_target[0]
    first_row = HEAD - n_meta
    g1, g2, gf = norm1_g, norm2_g, final_norm_g.reshape(1, D_MODEL)
    gs, gp = ssm_norm_g, pool_norm_g

    lam_re, lam_im = ssm_lambda_re[0] + first_token[:1, :1], ssm_lambda_im[0]
    log_step = ssm_log_step[0].reshape(SSM_GROUPS, 1)
    b_re = ssm_b_re[0].transpose(0, 2, 1)
    b_im = ssm_b_im[0].transpose(0, 2, 1)
    abr, abi, zr, zi = _s5_disc_a(lam_re, lam_im, log_step)
    zr_col, zi_col = zr.reshape(SSM_GROUPS, 1, SSM_STATE), zi.reshape(SSM_GROUPS, 1, SSM_STATE)
    bbr, bbi = _s5_disc_b(zr_col, zi_col, b_re, b_im)
    s5_consts = (abr.reshape(1, -1), abi.reshape(1, -1), bbr, bbi, ssm_c_re[0], ssm_c_im[0],
                 ssm_d[0].reshape(1, D_SSM), ssm_glu_w[0], ssm_glu_b[0].reshape(1, D_SSM))
    pool_sc = pool_scale[0].reshape(POOL_GROUPS, 1, POOL_DIM)

    (_, _, w_in_all, meta_all), first_done = _split_wait(first_x, first_make, bbr, "gather_w_in_wait")
    w_in_all = w_in_all.reshape(D_MODEL, D_MODEL)
    meta_full = meta_all.transpose(1, 0, 2).reshape(n_meta, D_MODEL)
    head = jnp.concatenate([jnp.zeros((HEAD - n_meta, D_MODEL), F32), meta_full], axis=0)
    shards = [(w_out[0] + first_done[:1, :1]).astype(BF16), _pad_rows(w_down[0]).astype(BF16),
              _gate_up(w_gate[0], w_up[0]).astype(BF16)]
    gather_make = _push_copies((SIBLING,) + CHIP_PEERS, [False, False, False])
    gather, gather_token = _split_start(shards + [_landing(s, False) for s in shards], gather_make,
                                        3 * (1 + len(CHIP_PEERS)), "gather_start")

    u, v = _in_proj(head, xs, g1 + gather_token[:1, :1], w_in_all)
    u_p = _permute_rows(u)
    ys_p = _s5_forward(u_p, *s5_consts)
    landed, _ = _split_wait(gather, gather_make, ys_p, "gather_wait")
    forward_make = _forward_copies(3)
    forward, forward_token = _split_start(list(landed[3:]), forward_make, 3 * len(CHIP_PEERS), "gather_forward_start")
    ys = _unpermute_rows(ys_p)
    yp = _pool_forward(v, pool_w[0], pool_sc + forward_token[:1, :1], first_row)
    (w_out_all, wd_all, c_all), _ = _split_wait(forward, forward_make, yp, "gather_forward_wait")
    w_out_all = w_out_all.reshape(D_MODEL, D_MODEL)
    h1 = _out_proj(head, xs, ys, yp, gs, gp, w_out_all)
    ab, n2t, h2 = _ffn_forward(h1, g2, c_all, wd_all)

    dh1, dab, dh2, loss_part, d_gf, d_g2 = _ffn_backward(h2, tgt, h1, ab, gf, g2, c_all, wd_all)
    d_c, d_wd = _ffn_wgrad(n2t, dh2, ab, dab)
    ffn_make = _push_copies(ALL_PEERS, [True, True])
    ffn_x, ffn_token = _split_start([d_c, d_wd, _landing(d_c, True), _landing(d_wd, True)], ffn_make,
                                    2 * len(ALL_PEERS), "ffn_grad_start")
    dys, dyp, d_gs, d_gp, d_wout = _out_proj_backward(dh1, ys, yp, gs + ffn_token[:1, :1], gp, w_out_all)
    dv, d_pool_w, d_pool_sc = _pool_backward(v, dyp, pool_w[0], pool_sc, first_row)
    (du_p, d_ar, d_ai, d_bbr, d_bbi, d_c_re, d_c_im, d_d, d_glu, d_glub) = _s5_backward(
        u_p, _permute_rows(dys), *s5_consts)
    du = _unpermute_rows(du_p)

    d_zr, d_zi, d_b_re, d_b_im = _s5_disc_b_bwd(zr_col, zi_col, b_re, b_im, d_bbr, d_bbi)
    d_lam_re, d_lam_im, d_log_step = _s5_disc_a_bwd(
        lam_re, lam_im, log_step,
        (d_ar.reshape(SSM_GROUPS, SSM_STATE), d_ai.reshape(SSM_GROUPS, SSM_STATE),
         d_zr.reshape(SSM_GROUPS, SSM_STATE), d_zi.reshape(SSM_GROUPS, SSM_STATE)))
    groups_last = lambda row: row.reshape(SSM_GROUPS, SSM_GROUP).T
    small_grads = {
        "ssm_lambda_re": d_lam_re, "ssm_lambda_im": d_lam_im, "ssm_log_step": d_log_step.reshape(1, SSM_GROUPS),
        "ssm_b_re": d_b_re.reshape(-1, SSM_STATE), "ssm_b_im": d_b_im.reshape(-1, SSM_STATE),
        "ssm_c_re": d_c_re.reshape(-1, SSM_STATE), "ssm_c_im": d_c_im.reshape(-1, SSM_STATE),
        "ssm_d": groups_last(d_d), "ssm_glu_w": d_glu.transpose(1, 2, 0).reshape(-1, SSM_GROUPS),
        "ssm_glu_b": groups_last(d_glub),
        "ssm_norm_g": d_gs, "pool_w": d_pool_w.reshape(-1, POOL_DIM), "pool_scale": d_pool_sc.reshape(-1, POOL_DIM),
        "pool_norm_g": d_gp, "norm2_g": d_g2,
    }

    early_names = SMALL_NAMES[1:-1]
    early_pack = _pack([small_grads[n] for n in early_names], BF16)
    d_wout = d_wout.reshape(N_DEV, shard_rows, D_MODEL)
    early_make = _push_copies(ALL_PEERS, [True, False])
    early_x, early_token = _split_start([d_wout, early_pack, _landing(d_wout, True), _landing(early_pack, False)],
                                        early_make, 2 * len(ALL_PEERS), "early_grad_start")
    d_x, d_head, d_g1, d_win = _in_proj_backward(head, xs, du, dv, dh1, g1 + early_token[:1, :1], w_in_all)
    (_, _, r_c, r_wd), _ = _split_wait(ffn_x, ffn_make, d_g1, "ffn_grad_wait")
    (_, _, r_wout, r_early), _ = _split_wait(early_x, early_make, d_g1, "early_grad_wait")
    d_win = d_win.reshape(N_DEV, shard_rows, D_MODEL)
    late_pack = _pack([d_g1, d_gf, d_head[first_row:], loss_part], F32)
    late_make = _push_copies(ALL_PEERS, [True, False])
    late_x, late_token = _split_start([d_win, late_pack, _landing(d_win, True), _landing(late_pack, False)],
                                      late_make, 2 * len(ALL_PEERS), "late_grad_start")

    results = {}
    res_c = _adamw_transposed(r_c, _gate_up_t(w_gate, w_up) + late_token[:1, :1], _gate_up_t(m_w_gate, m_w_up),
                              _gate_up_t(v_w_gate, v_w_up), "adamw_gate_up")
    results["w_gate"] = [r[:FF_SHARD].T for r in res_c]
    results["w_up"] = [r[FF_PAD:FF_PAD + FF_SHARD].T for r in res_c]
    res_wd = _adamw(r_wd, _pad_rows(w_down[0]), _pad_rows(m_w_down[0]), _pad_rows(v_w_down[0]), 128, "adamw_w_down")
    results["w_down"] = [r[:FF_SHARD] for r in res_wd]
    results["w_out"] = _adamw(r_wout, w_out[0], m_w_out[0], v_w_out[0], shard_rows, "adamw_w_out")
    done = res_c[1][:1, :1] + res_wd[1][:1, :1] + results["w_out"][1][:1, :1]
    (_, _, r_win, r_late), _ = _split_wait(late_x, late_make, done, "late_grad_wait")
    results["w_in"] = _adamw(r_win, w_in[0], m_w_in[0], v_w_in[0], shard_rows, "adamw_w_in")

    sum_early, sum_late = _reduce_slots([r_early, r_late], "small_grad_sums")
    views = lambda prefix: [_to_view(n, given[prefix + n]) for n in SMALL_NAMES]
    w_views = views("")
    g_views = _unpack(sum_early, [w.shape for w in w_views[1:-1]])
    g_norm1, g_final, g_meta_all, loss_row = _unpack(
        sum_late, [norm1_g.shape, (1, D_MODEL), (n_meta, D_MODEL), (1, PACK_LANES)])
    g_views = [g_norm1] + g_views + [g_final]
    res_small = _adamw_many(g_views, w_views, views("m_"), views("v_"), "adamw_small")
    for idx, n in enumerate(SMALL_NAMES):
        results[n] = [_from_view(n, part[idx], weights[n].shape) for part in (g_views,) + tuple(res_small)]
    shard_cols = meta_tokens.shape[1]
    g_meta = lax.dynamic_slice_in_dim(g_meta_all, me * shard_cols, shard_cols, axis=1)
    results["meta_tokens"] = _adamw(g_meta[None], meta_tokens, m_meta_tokens, v_meta_tokens, n_meta, "adamw_meta")

    out = [loss_row[0, 0], d_x[None]]
    for part in range(4):
        for n in WEIGHT_NAMES:
            out.append(results[n][part].reshape(weights[n].shape))
    return tuple(out)
```

```python
import jax
import jax.numpy as jnp
from jax import lax
from jax.experimental import pallas as pl
from jax.experimental.pallas import tpu as pltpu

F32 = jnp.float32
BF16 = jnp.bfloat16

N_DEV = 8
D_MODEL = 1024
D_SSM = 512
SSM_GROUP = 16
SSM_STATE = 64
SSM_GROUPS = 32
POOL_GROUPS = 4
POOL_DIM = 128
COL_U = 128
COL_S = 512
N_COL = D_SSM // COL_U
GROUPS_PER_COL = COL_U // SSM_GROUP
D_FF = 2816
FF_SHARD = D_FF // N_DEV
FF_TILE = D_FF // 2
TM = 256
WGRAD_STEPS = 2
HEAD = TM
SUBLANES = 8
SCAN_UNROLL = 4
POOL_HALO = 16
EPS = 1e-6
STEP_FLOOR = -1e-4
VMEM_LIMIT = 60 * 1024 * 1024

ADAM_LR = 0.001
ADAM_B1 = 0.9
ADAM_B2 = 0.999
ADAM_EPS = 1e-08
ADAM_WD = 0.01
ADAM_STEP = 10

MESH_ID = pl.DeviceIdType.MESH
ANY = pl.BlockSpec(memory_space=pl.ANY)

SMALL_NAMES = ("norm1_g", "ssm_lambda_re", "ssm_lambda_im", "ssm_log_step", "ssm_b_re", "ssm_b_im",
               "ssm_c_re", "ssm_c_im", "ssm_d", "ssm_glu_w", "ssm_glu_b", "ssm_norm_g", "pool_w",
               "pool_scale", "pool_norm_g", "norm2_g", "final_norm_g")
WEIGHT_NAMES = ("meta_tokens", "norm1_g", "w_in", "ssm_lambda_re", "ssm_lambda_im", "ssm_log_step",
                "ssm_b_re", "ssm_b_im", "ssm_c_re", "ssm_c_im", "ssm_d", "ssm_glu_w", "ssm_glu_b",
                "ssm_norm_g", "pool_w", "pool_scale", "pool_norm_g", "w_out", "norm2_g", "w_gate",
                "w_up", "w_down", "final_norm_g")
PACK_LANES = 128
PACK_UNIT = 8 * PACK_LANES


def _dot(a, b):
    return jnp.dot(a.astype(BF16), b.astype(BF16), preferred_element_type=F32)


def _dot_nt(a, b):
    return lax.dot_general(a.astype(BF16), b.astype(BF16), (((1,), (1,)), ((), ())), preferred_element_type=F32)


def _dot_tn(a, b):
    return lax.dot_general(a.astype(BF16), b.astype(BF16), (((0,), (0,)), ((), ())), preferred_element_type=F32)


def _sigmoid(x):
    return 1.0 / (1.0 + jnp.exp(-x))


def _rstd(x):
    return lax.rsqrt(jnp.mean(x * x, axis=-1, keepdims=True) + EPS)


def _rms_bwd(dy, xhat, r, g):
    dxh = dy * g
    dx = r * (dxh - xhat * jnp.mean(dxh * xhat, axis=-1, keepdims=True))
    return dx, jnp.sum(dy * xhat, axis=0, keepdims=True)


def _params(sem, vmem=None):
    return pltpu.CompilerParams(dimension_semantics=sem, vmem_limit_bytes=vmem)


def _const(shape):
    return pl.BlockSpec(shape, lambda *_: (0,) * len(shape))


def _xrow(i):
    return (jnp.maximum(i - 1, 0), 0)


HBM = pl.BlockSpec(memory_space=pltpu.HBM)
SEM = pl.BlockSpec(memory_space=pltpu.SEMAPHORE)
EFFECT = pltpu.SideEffectType.DATAFLOW_SIDE_EFFECTING
ALL_PEERS = tuple(range(1, N_DEV))
SIBLING = 1
CHIP_PEERS = (2, 4, 6)


def _me():
    return 4 * lax.axis_index("x") + 2 * lax.axis_index("y") + lax.axis_index("c")


def _peer(k):
    x, y, c = lax.axis_index("x"), lax.axis_index("y"), lax.axis_index("c")
    px = 1 - x if k & 4 else x
    py = 1 - y if k & 2 else y
    pc = 1 - c if k & 1 else c
    return (px, py, pc), 4 * px + 2 * py + pc


def _landing(arr, scatter):
    if scatter:
        own = lax.dynamic_index_in_dim(arr, _me(), 0, keepdims=False)
    else:
        own = arr
    return lax.dynamic_update_index_in_dim(lax.empty((N_DEV,) + own.shape, arr.dtype), own, _me(), 0)


def _push_copies(peers, scatter):
    n_arr = len(scatter)

    def make(refs, send_sems, recv_sems):
        copies = []
        for a in range(n_arr):
            for i, k in enumerate(peers):
                peer_id, peer = _peer(k)
                sem = a * len(peers) + i
                copies.append(pltpu.make_async_remote_copy(
                    src_ref=refs[a].at[peer] if scatter[a] else refs[a], dst_ref=refs[n_arr + a].at[_me()],
                    send_sem=send_sems.at[sem], recv_sem=recv_sems.at[sem],
                    device_id=peer_id, device_id_type=MESH_ID))
        return copies
    return make


def _forward_copies(n_arr):
    def make(refs, send_sems, recv_sems):
        copies = []
        sibling_id, _ = _peer(SIBLING)
        for a in range(n_arr):
            for i, k in enumerate(CHIP_PEERS):
                slot = refs[a].at[_peer(k)[1]]
                sem = a * len(CHIP_PEERS) + i
                copies.append(pltpu.make_async_remote_copy(
                    src_ref=slot, dst_ref=slot, send_sem=send_sems.at[sem], recv_sem=recv_sems.at[sem],
                    device_id=sibling_id, device_id_type=MESH_ID))
        return copies
    return make


def _split_start(operands, make, n_sem, name):
    n_op = len(operands)

    def body(*refs):
        for cp in make(refs[:n_op], refs[n_op], refs[n_op + 1]):
            cp.start()
        refs[-1][...] = jnp.zeros(refs[-1].shape, F32)

    out = pl.pallas_call(
        body, name=name,
        out_shape=(pltpu.SemaphoreType.DMA((n_sem,)), pltpu.SemaphoreType.DMA((n_sem,)),
                   *[pltpu.HBM(t.shape, t.dtype) for t in operands], jax.ShapeDtypeStruct((8, PACK_LANES), F32)),
        in_specs=[HBM] * n_op, out_specs=(SEM, SEM, *[HBM] * n_op, pl.BlockSpec(memory_space=pltpu.VMEM)),
        input_output_aliases={i: 2 + i for i in range(n_op)},
        compiler_params=pltpu.CompilerParams(has_side_effects=EFFECT),
    )(*[pltpu.with_memory_space_constraint(t, pltpu.HBM) for t in operands])
    return out[:-1], out[-1]


def _split_wait(started, make, after, name):
    send_sems, recv_sems, thru = started[0], started[1], started[2:]
    n_op = len(thru)

    def body(*refs):
        for cp in make(refs[:n_op], refs[n_op], refs[n_op + 1]):
            cp.wait_send()
            cp.wait_recv()
        refs[-1][...] = jnp.zeros(refs[-1].shape, F32)

    out = pl.pallas_call(
        body, name=name,
        out_shape=(*[pltpu.HBM(t.shape, t.dtype) for t in thru], jax.ShapeDtypeStruct((8, PACK_LANES), F32)),
        in_specs=[HBM] * n_op + [SEM, SEM, ANY], out_specs=(*[HBM] * n_op, pl.BlockSpec(memory_space=pltpu.VMEM)),
        input_output_aliases={i: i for i in range(n_op)},
        compiler_params=pltpu.CompilerParams(has_side_effects=EFFECT),
    )(*thru, send_sems, recv_sems, after)
    return out[:-1], out[-1]


def _adamw_math(g, w, m, v):
    nm = ADAM_B1 * m + (1.0 - ADAM_B1) * g
    nv = ADAM_B2 * v + (1.0 - ADAM_B2) * (g * g)
    m_hat = nm / (1.0 - ADAM_B1 ** ADAM_STEP)
    v_hat = nv / (1.0 - ADAM_B2 ** ADAM_STEP)
    return -ADAM_LR * (m_hat / (jnp.sqrt(v_hat) + ADAM_EPS) + ADAM_WD * w), nm, nv


def _sum_slots(s_ref):
    g = s_ref[0].astype(F32)
    for s in range(1, s_ref.shape[0]):
        g = g + s_ref[s].astype(F32)
    return g


def _adamw(slots, w, m, v, tile_rows, name):
    n, rows, cols = slots.shape

    def body(s_ref, w_ref, m_ref, v_ref, g_ref, d_ref, nm_ref, nv_ref):
        g = _sum_slots(s_ref)
        g_ref[...] = g
        d_ref[...], nm_ref[...], nv_ref[...] = _adamw_math(g, w_ref[...], m_ref[...], v_ref[...])

    tile = pl.BlockSpec((tile_rows, cols), lambda i: (i, 0))
    return pl.pallas_call(
        body, name=name, grid=(rows // tile_rows,),
        in_specs=[pl.BlockSpec((n, tile_rows, cols), lambda i: (0, i, 0)), tile, tile, tile],
        out_specs=[tile] * 4, out_shape=[jax.ShapeDtypeStruct((rows, cols), F32)] * 4,
        compiler_params=_params(("parallel",), VMEM_LIMIT),
    )(slots, w, m, v)


def _adamw_part(slots, part, w, m, v, name):
    n, _, rows, cols = slots.shape
    tile_cols = 256

    def body(s_ref, w_ref, m_ref, v_ref, g_ref, d_ref, nm_ref, nv_ref):
        g = _sum_slots(s_ref)
        g_ref[...] = g
        d_ref[...], nm_ref[...], nv_ref[...] = _adamw_math(g, w_ref[...], m_ref[...], v_ref[...])

    tile = pl.BlockSpec((rows, tile_cols), lambda i: (0, i))
    return pl.pallas_call(
        body, name=name, grid=(cols // tile_cols,),
        in_specs=[pl.BlockSpec((n, None, rows, tile_cols), lambda i: (0, part, 0, i)), tile, tile, tile],
        out_specs=[tile] * 4, out_shape=[jax.ShapeDtypeStruct((rows, cols), F32)] * 4,
        compiler_params=_params(("parallel",), VMEM_LIMIT),
    )(slots, w, m, v)


def _reduce_slots(slot_arrays, name):
    def body(*refs):
        n_arr = len(refs) // 2
        for s_ref, o_ref in zip(refs[:n_arr], refs[n_arr:]):
            o_ref[...] = _sum_slots(s_ref)
    return pl.pallas_call(
        body, name=name, out_shape=[jax.ShapeDtypeStruct(s.shape[1:], F32) for s in slot_arrays],
        compiler_params=_params(None, VMEM_LIMIT))(*slot_arrays)


def _adamw_many(grads, ws, ms, vs, name):
    n = len(grads)

    def body(*refs):
        ins, outs = refs[:4 * n], refs[4 * n:]
        for i in range(n):
            g, w, m, v = (ins[j * n + i][...] for j in range(4))
            outs[i][...], outs[n + i][...], outs[2 * n + i][...] = _adamw_math(g, w, m, v)

    out = pl.pallas_call(
        body, name=name, out_shape=[jax.ShapeDtypeStruct(w.shape, F32) for w in ws] * 3,
        compiler_params=_params(None, VMEM_LIMIT))(*grads, *ws, *ms, *vs)
    return out[:n], out[n:2 * n], out[2 * n:]


def _disc_a(lam_re, lam_im, log_step):
    lr = jnp.minimum(lam_re, STEP_FLOOR)
    step = jnp.exp(log_step)
    mag = jnp.exp(lr * step)
    ang = lam_im * step
    abr = mag * jnp.cos(ang)
    abi = mag * jnp.sin(ang)
    nr = abr - 1.0
    den = lr * lr + lam_im * lam_im
    cr = (nr * lr + abi * lam_im) / den
    ci = (abi * lr - nr * lam_im) / den
    return abr, abi, cr, ci


def _disc_b(cr, ci, b_re, b_im):
    return cr * b_re - ci * b_im, cr * b_im + ci * b_re


def _s5_disc_a(lam_re, lam_im, log_step):
    def body(lr_ref, li_ref, ls_ref, *outs):
        for o, val in zip(outs, _disc_a(lr_ref[...], li_ref[...], ls_ref[...])):
            o[...] = val
    return pl.pallas_call(body, name="s5_disc_a", out_shape=[jax.ShapeDtypeStruct(lam_re.shape, F32)] * 4)(
        lam_re, lam_im, log_step)


def _s5_disc_a_bwd(lam_re, lam_im, log_step, cts):
    def body(lr_ref, li_ref, ls_ref, c0, c1, c2, c3, dlr_ref, dli_ref, dls_ref):
        _, vjp = jax.vjp(_disc_a, lr_ref[...], li_ref[...], ls_ref[...])
        dlr, dli, dls = vjp((c0[...], c1[...], c2[...], c3[...]))
        dlr_ref[...] = dlr
        dli_ref[...] = dli
        dls_ref[...] = dls
    return pl.pallas_call(
        body, name="s5_disc_a_bwd",
        out_shape=[jax.ShapeDtypeStruct(lam_re.shape, F32), jax.ShapeDtypeStruct(lam_re.shape, F32),
                   jax.ShapeDtypeStruct(log_step.shape, F32)])(lam_re, lam_im, log_step, *cts)


def _s5_disc_b(cr, ci, b_re, b_im):
    def body(cr_ref, ci_ref, br_ref, bi_ref, o_re, o_im):
        o_re[...], o_im[...] = _disc_b(cr_ref[...], ci_ref[...], br_ref[...], bi_ref[...])
    return pl.pallas_call(body, name="s5_disc_b", out_shape=[jax.ShapeDtypeStruct(b_re.shape, F32)] * 2)(
        cr, ci, b_re, b_im)


def _s5_disc_b_bwd(cr, ci, b_re, b_im, d_re, d_im):
    def body(cr_ref, ci_ref, br_ref, bi_ref, dr_ref, di_ref, dcr_ref, dci_ref, dbr_ref, dbi_ref):
        _, vjp = jax.vjp(_disc_b, cr_ref[...], ci_ref[...], br_ref[...], bi_ref[...])
        dcr_ref[...], dci_ref[...], dbr_ref[...], dbi_ref[...] = vjp((dr_ref[...], di_ref[...]))
    return pl.pallas_call(
        body, name="s5_disc_b_bwd",
        out_shape=[jax.ShapeDtypeStruct(cr.shape, F32)] * 2 + [jax.ShapeDtypeStruct(b_re.shape, F32)] * 2)(
            cr, ci, b_re, b_im, d_re, d_im)


def _cmul(ar, ai, br, bi):
    return ar * br - ai * bi, ar * bi + ai * br


def _cpow(ar, ai, n):
    rr, ri = jnp.ones_like(ar), jnp.zeros_like(ai)
    while n:
        if n & 1:
            rr, ri = _cmul(rr, ri, ar, ai)
        n >>= 1
        if n:
            ar, ai = _cmul(ar, ai, ar, ai)
    return rr, ri


def _tile_rows(i):
    if isinstance(i, int):
        return pl.ds(i * SUBLANES, SUBLANES)
    return pl.ds(pl.multiple_of(i * SUBLANES, SUBLANES), SUBLANES)


def _segment_scan(z_re, z_im, ar, ai, lseg, reverse, visit=None):
    shape = (SUBLANES, z_re.shape[1])
    half = lseg // 2
    arb = jnp.broadcast_to(ar, shape)
    aib = jnp.broadcast_to(ai, shape)
    zero = jnp.zeros(shape, F32)
    row = lax.broadcasted_iota(jnp.int32, shape, 0)

    def tiles(k):
        return (lseg - 1 - k, half - 1 - k) if reverse else (k, half + k)

    def advance(tile, sr, si):
        rows = _tile_rows(tile)
        nr, ni = _cmul(arb, aib, sr, si)
        return rows, nr + z_re[rows, :], ni + z_im[rows, :]

    def first_pass(k, carry):
        ta, tb = tiles(k)
        return advance(ta, carry[0], carry[1])[1:] + advance(tb, carry[2], carry[3])[1:]

    def unrolled(step):
        def body(it, carry):
            for j in range(SCAN_UNROLL):
                carry = step(it * SCAN_UNROLL + j, carry)
            return carry
        return body

    n_iter = half // SCAN_UNROLL
    fa_r, fa_i, fb_r, fb_i = lax.fori_loop(0, n_iter, unrolled(first_pass), (zero,) * 4)
    hr, hi = _cpow(arb, aib, half)
    pr, pi = _cmul(hr, hi, hr, hi)
    fr, fi = _cmul(hr, hi, fa_r, fa_i)
    fr, fi = fr + fb_r, fi + fb_i
    cr, ci = zero, zero
    for _ in range(SUBLANES - 1):
        tr, ti = _cmul(pr, pi, cr, ci)
        tr, ti = tr + fr, ti + fi
        if reverse:
            cr = jnp.where(row == SUBLANES - 1, 0.0, pltpu.roll(tr, SUBLANES - 1, 0))
            ci = jnp.where(row == SUBLANES - 1, 0.0, pltpu.roll(ti, SUBLANES - 1, 0))
        else:
            cr = jnp.where(row == 0, 0.0, pltpu.roll(tr, 1, 0))
            ci = jnp.where(row == 0, 0.0, pltpu.roll(ti, 1, 0))

    br, bi = _cmul(hr, hi, cr, ci)
    br, bi = br + fa_r, bi + fa_i

    def second_pass(k, carry, b_is_tile0=False):
        states, acc = list(carry[:4]), carry[4]
        for chain, tile in enumerate(tiles(k)):
            rows, nr, ni = advance(tile, states[2 * chain], states[2 * chain + 1])
            z_re[rows, :] = nr
            z_im[rows, :] = ni
            states[2 * chain], states[2 * chain + 1] = nr, ni
            if visit is not None:
                acc = visit(tile, nr, ni, acc, chain == 1 and b_is_tile0)
        return (*states, acc)

    acc0 = (zero, zero) if visit is not None else 0
    carry = lax.fori_loop(0, n_iter - 1, unrolled(second_pass), (cr, ci, br, bi, acc0))
    for k in range(half - SCAN_UNROLL, half - 1):
        carry = second_pass(k, carry)
    return second_pass(half - 1, carry, b_is_tile0=reverse)[4]


def _gelu(y):
    c = 0.7978845608028654
    return 0.5 * y * (1.0 + jnp.tanh(c * (y + 0.044715 * y * y * y)))


def _gelu_grad(y):
    c = 0.7978845608028654
    th = jnp.tanh(c * (y + 0.044715 * y * y * y))
    return 0.5 * (1.0 + th) + 0.5 * y * (1.0 - th * th) * c * (1.0 + 3.0 * 0.044715 * y * y)


def _s5_specs(lp):
    col_u = pl.BlockSpec((lp, COL_U), lambda j: (0, j))
    row_u = pl.BlockSpec((1, COL_U), lambda j: (0, j))
    row_s = pl.BlockSpec((1, COL_S), lambda j: (0, j))
    bc_blk = pl.BlockSpec((GROUPS_PER_COL, SSM_GROUP, SSM_STATE), lambda j: (j, 0, 0))
    glu_blk = pl.BlockSpec((GROUPS_PER_COL, SSM_GROUP, SSM_GROUP), lambda j: (j, 0, 0))
    return col_u, row_u, row_s, bc_blk, glu_blk


def _s5_block_diag_scratch():
    return ([pltpu.VMEM((COL_U, COL_S), BF16)] * 4 + [pltpu.VMEM((COL_U, COL_U), BF16)]
            + [pltpu.VMEM((COL_U, COL_S), F32)])


def _fill_block_diag(bd_ref, blocks_ref, stage):
    r, c = blocks_ref.shape[1:]
    stage[...] = jnp.zeros(stage.shape, F32)
    for gl in range(GROUPS_PER_COL):
        stage[pl.ds(gl * r, r), pl.ds(gl * c, c)] = blocks_ref[gl]
    bd_ref[...] = stage[:, :GROUPS_PER_COL * c].astype(BF16)


def _take_block_diag(out_ref, mat):
    r, c = out_ref.shape[1:]
    for gl in range(GROUPS_PER_COL):
        out_ref[gl] = mat[gl * r:(gl + 1) * r, gl * c:(gl + 1) * c]


def _s5_fill_states(u_ref, bre_ref, bim_ref, ar_ref, ai_ref, s_re, s_im, lseg, n_chunks, chunk):
    def fill(cidx, carry):
        rows = pl.ds(pl.multiple_of(cidx * chunk, SUBLANES), chunk)
        ub = u_ref[rows, :].astype(BF16)
        s_re[rows, :] = jnp.dot(ub, bre_ref[...], preferred_element_type=F32)
        s_im[rows, :] = jnp.dot(ub, bim_ref[...], preferred_element_type=F32)
        return carry
    lax.fori_loop(0, n_chunks, fill, 0)
    _segment_scan(s_re, s_im, ar_ref[...], ai_ref[...], lseg, reverse=False)


def _s5_forward(u_p, ar, ai, bbr, bbi, c_re, c_im, d_row, glu_w, glub_row):
    lp = u_p.shape[0]
    lseg = lp // SUBLANES
    chunk, n_chunks = lseg, SUBLANES

    def body(u_ref, ar_ref, ai_ref, bbr_ref, bbi_ref, cr_ref, ci_ref, d_ref, gw_ref, glub_ref,
             ys_ref, s_re, s_im, bre_ref, bim_ref, cre_ref, cim_ref, glu_ref, stage):
        for bd, blocks in ((bre_ref, bbr_ref), (bim_ref, bbi_ref), (cre_ref, cr_ref), (cim_ref, ci_ref),
                           (glu_ref, gw_ref)):
            _fill_block_diag(bd, blocks, stage)
        _s5_fill_states(u_ref, bre_ref, bim_ref, ar_ref, ai_ref, s_re, s_im, lseg, n_chunks, chunk)

        def emit(cidx, carry):
            rows = pl.ds(pl.multiple_of(cidx * chunk, SUBLANES), chunk)
            y = (_dot_nt(s_re[rows, :], cre_ref[...]) - _dot_nt(s_im[rows, :], cim_ref[...])
                 + d_ref[...] * u_ref[rows, :])
            g = _gelu(y)
            gate = _dot(g, glu_ref[...]) + glub_ref[...]
            ys_ref[rows, :] = g * _sigmoid(gate)
            return carry
        lax.fori_loop(0, n_chunks, emit, 0)

    col_u, row_u, row_s, bc_blk, glu_blk = _s5_specs(lp)
    return pl.pallas_call(
        body, name="s5_forward", grid=(N_COL,),
        in_specs=[col_u, row_s, row_s, bc_blk, bc_blk, bc_blk, bc_blk, row_u, glu_blk, row_u],
        out_specs=col_u, out_shape=jax.ShapeDtypeStruct((lp, D_SSM), F32),
        scratch_shapes=[pltpu.VMEM((lp, COL_S), F32), pltpu.VMEM((lp, COL_S), F32)] + _s5_block_diag_scratch(),
        compiler_params=_params(("arbitrary",), VMEM_LIMIT),
    )(u_p, ar, ai, bbr, bbi, c_re, c_im, d_row, glu_w, glub_row)


def _s5_backward(u_p, dys_p, ar, ai, bbr, bbi, c_re, c_im, d_row, glu_w, glub_row):
    lp = u_p.shape[0]
    lseg = lp // SUBLANES
    chunk, n_chunks = lseg, SUBLANES

    def body(u_ref, dys_ref, ar_ref, ai_ref, bbr_ref, bbi_ref, cr_ref, ci_ref, d_ref, gw_ref, glub_ref,
             du_ref, dar_ref, dai_ref, dbbr_ref, dbbi_ref, dcr_ref, dci_ref, dd_ref, dgw_ref, dglub_ref,
             s_re, s_im, q_re, q_im, bre_ref, bim_ref, cre_ref, cim_ref, glu_ref, stage,
             dbre_ref, dbim_ref, dcre_ref, dcim_ref, dglu_ref):
        for bd, blocks in ((bre_ref, bbr_ref), (bim_ref, bbi_ref), (cre_ref, cr_ref), (cim_ref, ci_ref),
                           (glu_ref, gw_ref)):
            _fill_block_diag(bd, blocks, stage)
        _s5_fill_states(u_ref, bre_ref, bim_ref, ar_ref, ai_ref, s_re, s_im, lseg, n_chunks, chunk)
        for ref in (dcre_ref, dcim_ref, dd_ref, dglu_ref, dglub_ref, dbre_ref, dbim_ref):
            ref[...] = jnp.zeros(ref.shape, F32)

        def mixer_bwd(cidx, carry):
            rows = pl.ds(pl.multiple_of(cidx * chunk, SUBLANES), chunk)
            u = u_ref[rows, :]
            sr, si = s_re[rows, :], s_im[rows, :]
            y = _dot_nt(sr, cre_ref[...]) - _dot_nt(si, cim_ref[...]) + d_ref[...] * u
            g = _gelu(y)
            sg = _sigmoid(_dot(g, glu_ref[...]) + glub_ref[...])
            dout = dys_ref[rows, :]
            dgate = dout * g * sg * (1.0 - sg)
            dy = (dout * sg + _dot_nt(dgate, glu_ref[...])) * _gelu_grad(y)
            dglu_ref[...] += _dot_tn(g, dgate)
            dglub_ref[...] += jnp.sum(dgate, axis=0, keepdims=True)
            dd_ref[...] += jnp.sum(dy * u, axis=0, keepdims=True)
            dcre_ref[...] += _dot_tn(dy, sr)
            dcim_ref[...] -= _dot_tn(dy, si)
            q_re[rows, :] = _dot(dy, cre_ref[...])
            q_im[rows, :] = -_dot(dy, cim_ref[...])
            du_ref[rows, :] = d_ref[...] * dy
            return carry
        lax.fori_loop(0, n_chunks, mixer_bwd, 0)

        row = lax.broadcasted_iota(jnp.int32, (SUBLANES, COL_S), 0)

        def visit(i, qr, qi, acc, is_tile0):
            if is_tile0:
                prev = _tile_rows(lseg - 1)
                pr = jnp.where(row == 0, 0.0, pltpu.roll(s_re[prev, :], 1, 0))
                pi = jnp.where(row == 0, 0.0, pltpu.roll(s_im[prev, :], 1, 0))
            else:
                prev = _tile_rows(i - 1)
                pr, pi = s_re[prev, :], s_im[prev, :]
            return acc[0] + qr * pr + qi * pi, acc[1] + qi * pr - qr * pi

        dar, dai = _segment_scan(q_re, q_im, ar_ref[...], -ai_ref[...], lseg, reverse=True, visit=visit)
        dar_ref[...] = jnp.sum(dar, axis=0, keepdims=True)
        dai_ref[...] = jnp.sum(dai, axis=0, keepdims=True)

        def input_bwd(cidx, carry):
            rows = pl.ds(pl.multiple_of(cidx * chunk, SUBLANES), chunk)
            qr, qi = q_re[rows, :], q_im[rows, :]
            u = u_ref[rows, :]
            du_ref[rows, :] += _dot_nt(qr, bre_ref[...]) + _dot_nt(qi, bim_ref[...])
            dbre_ref[...] += _dot_tn(u, qr)
            dbim_ref[...] += _dot_tn(u, qi)
            return carry
        lax.fori_loop(0, n_chunks, input_bwd, 0)
        for out, acc in ((dbbr_ref, dbre_ref), (dbbi_ref, dbim_ref), (dcr_ref, dcre_ref), (dci_ref, dcim_ref),
                         (dgw_ref, dglu_ref)):
            _take_block_diag(out, acc[...])

    col_u, row_u, row_s, bc_blk, glu_blk = _s5_specs(lp)
    group_mats = jax.ShapeDtypeStruct((SSM_GROUPS, SSM_GROUP, SSM_STATE), F32)
    return pl.pallas_call(
        body, name="s5_backward", grid=(N_COL,),
        in_specs=[col_u, col_u, row_s, row_s, bc_blk, bc_blk, bc_blk, bc_blk, row_u, glu_blk, row_u],
        out_specs=[col_u, row_s, row_s, bc_blk, bc_blk, bc_blk, bc_blk, row_u, glu_blk, row_u],
        out_shape=[jax.ShapeDtypeStruct((lp, D_SSM), F32),
                   jax.ShapeDtypeStruct((1, N_COL * COL_S), F32), jax.ShapeDtypeStruct((1, N_COL * COL_S), F32),
                   group_mats, group_mats, group_mats, group_mats, jax.ShapeDtypeStruct((1, D_SSM), F32),
                   jax.ShapeDtypeStruct((SSM_GROUPS, SSM_GROUP, SSM_GROUP), F32), jax.ShapeDtypeStruct((1, D_SSM), F32)],
        scratch_shapes=([pltpu.VMEM((lp, COL_S), F32)] * 4 + _s5_block_diag_scratch()
                        + [pltpu.VMEM((COL_U, COL_S), F32)] * 4 + [pltpu.VMEM((COL_U, COL_U), F32)]),
        compiler_params=_params(("arbitrary",), VMEM_LIMIT),
    )(u_p, dys_p, ar, ai, bbr, bbi, c_re, c_im, d_row, glu_w, glub_row)


def _window_sum(ext, group, leading):
    n = ext.shape[0]
    s = ext
    for j in range(POOL_GROUPS):
        shift = n - (1 << j) if leading else 1 << j
        s = jnp.where(j <= group, s + pltpu.roll(s, shift, 0), s)
    return s


def _pool_inv_count(tile, window, first_row):
    t = tile * TM + lax.broadcasted_iota(jnp.int32, (TM, 1), 0) - first_row
    return 1.0 / jnp.clip(t + 1, 1, window).astype(F32)


def _pool_specs(lp):
    col = pl.BlockSpec((lp, POOL_DIM), lambda k: (0, k))
    mat = pl.BlockSpec((None, POOL_DIM, POOL_DIM), lambda k: (k, 0, 0))
    row = pl.BlockSpec((None, 1, POOL_DIM), lambda k: (k, 0, 0))
    return col, mat, row


def _pool_forward(v, pool_w, pool_scale, first_row):
    lp = v.shape[0]
    n_tiles = lp // TM

    def body(v_ref, w_ref, sc_ref, yp_ref, vpad):
        group = pl.program_id(0)
        window = jnp.left_shift(2, group)
        vpad[pl.ds(0, POOL_HALO), :] = jnp.zeros((POOL_HALO, POOL_DIM), F32)
        vpad[pl.ds(POOL_HALO, lp), :] = v_ref[...]

        def tile(j, carry):
            start = pl.multiple_of(j * TM, TM)
            ext = vpad[pl.ds(start, TM + POOL_HALO), :]
            sums = _window_sum(ext, group, leading=False)[POOL_HALO:, :]
            p = sums * _pool_inv_count(j, window, first_row) - ext[POOL_HALO:, :]
            yp_ref[pl.ds(start, TM), :] = _dot(p, w_ref[...]) * sc_ref[...]
            return carry
        lax.fori_loop(0, n_tiles, tile, 0)

    col, mat, row = _pool_specs(lp)
    return pl.pallas_call(
        body, name="pool_forward", grid=(POOL_GROUPS,),
        in_specs=[col, mat, row], out_specs=col, out_shape=jax.ShapeDtypeStruct((lp, D_SSM), F32),
        scratch_shapes=[pltpu.VMEM((lp + POOL_HALO, POOL_DIM), F32)],
        compiler_params=_params(("arbitrary",), VMEM_LIMIT),
    )(v, pool_w, pool_scale)


def _pool_backward(v, dyp, pool_w, pool_scale, first_row):
    lp = v.shape[0]
    n_tiles = lp // TM

    def body(v_ref, dyp_ref, w_ref, sc_ref, dv_ref, dw_ref, dsc_ref, vpad, gpad):
        group = pl.program_id(0)
        window = jnp.left_shift(2, group)
        vpad[pl.ds(0, POOL_HALO), :] = jnp.zeros((POOL_HALO, POOL_DIM), F32)
        vpad[pl.ds(POOL_HALO, lp), :] = v_ref[...]
        gpad[pl.ds(lp, POOL_HALO), :] = jnp.zeros((POOL_HALO, POOL_DIM), F32)
        dw_ref[...] = jnp.zeros(dw_ref.shape, F32)
        dsc_ref[...] = jnp.zeros(dsc_ref.shape, F32)

        def linear_bwd(j, carry):
            start = pl.multiple_of(j * TM, TM)
            ext = vpad[pl.ds(start, TM + POOL_HALO), :]
            inv = _pool_inv_count(j, window, first_row)
            p = _window_sum(ext, group, leading=False)[POOL_HALO:, :] * inv - ext[POOL_HALO:, :]
            z = _dot(p, w_ref[...])
            dyp_t = dyp_ref[pl.ds(start, TM), :]
            dz = dyp_t * sc_ref[...]
            dsc_ref[...] += jnp.sum(dyp_t * z, axis=0, keepdims=True)
            dw_ref[...] += _dot_tn(p, dz)
            dp = _dot_nt(dz, w_ref[...])
            gpad[pl.ds(start, TM), :] = dp * inv
            dv_ref[pl.ds(start, TM), :] = -dp
            return carry
        lax.fori_loop(0, n_tiles, linear_bwd, 0)

        def window_bwd(j, carry):
            start = pl.multiple_of(j * TM, TM)
            ext = gpad[pl.ds(start, TM + POOL_HALO), :]
            dv_ref[pl.ds(start, TM), :] += _window_sum(ext, group, leading=True)[:TM, :]
            return carry
        lax.fori_loop(0, n_tiles, window_bwd, 0)

    col, mat, row = _pool_specs(lp)
    return pl.pallas_call(
        body, name="pool_backward", grid=(POOL_GROUPS,),
        in_specs=[col, col, mat, row], out_specs=[col, mat, row],
        out_shape=[jax.ShapeDtypeStruct((lp, D_SSM), F32),
                   jax.ShapeDtypeStruct((POOL_GROUPS, POOL_DIM, POOL_DIM), F32),
                   jax.ShapeDtypeStruct((POOL_GROUPS, 1, POOL_DIM), F32)],
        scratch_shapes=[pltpu.VMEM((lp + POOL_HALO, POOL_DIM), F32)] * 2,
        compiler_params=_params(("arbitrary",), VMEM_LIMIT),
    )(v, dyp, pool_w, pool_scale)


def _row_specs():
    head = _const((HEAD, D_MODEL))
    xrow = pl.BlockSpec((TM, D_MODEL), _xrow)
    full = pl.BlockSpec((TM, D_MODEL), lambda i: (i, 0))
    half = pl.BlockSpec((TM, D_SSM), lambda i: (i, 0))
    return head, xrow, full, half


def _in_proj(head, x, g1, w_in):
    n_tiles = (HEAD + x.shape[0]) // TM
    lp = n_tiles * TM

    def body(head_ref, x_ref, g_ref, w_ref, u_ref, v_ref):
        h0 = jnp.where(pl.program_id(0) == 0, head_ref[...], x_ref[...])
        proj = _dot(h0 * _rstd(h0) * g_ref[...], w_ref[...])
        u_ref[...] = proj[:, :D_SSM]
        v_ref[...] = proj[:, D_SSM:]

    head_s, xrow, _, half = _row_specs()
    return pl.pallas_call(
        body, name="in_proj", grid=(n_tiles,),
        in_specs=[head_s, xrow, _const((1, D_MODEL)), _const((D_MODEL, D_MODEL))],
        out_specs=[half, half], out_shape=[jax.ShapeDtypeStruct((lp, D_SSM), F32)] * 2,
        compiler_params=_params(("parallel",), VMEM_LIMIT),
    )(head, x, g1, w_in)


def _out_proj(head, x, ys, yp, gs, gp, w_out):
    lp = ys.shape[0]

    def body(head_ref, x_ref, ys_ref, yp_ref, gs_ref, gp_ref, w_ref, h1_ref):
        h0 = jnp.where(pl.program_id(0) == 0, head_ref[...], x_ref[...])
        ys_t, yp_t = ys_ref[...], yp_ref[...]
        ms = ys_t * _rstd(ys_t) * gs_ref[...]
        mp = yp_t * _rstd(yp_t) * gp_ref[...]
        h1_ref[...] = h0 + _dot(ms, w_ref[pl.ds(0, D_SSM), :]) + _dot(mp, w_ref[pl.ds(D_SSM, D_SSM), :])

    head_s, xrow, full, half = _row_specs()
    return pl.pallas_call(
        body, name="out_proj", grid=(lp // TM,),
        in_specs=[head_s, xrow, half, half, _const((1, D_SSM)), _const((1, D_SSM)), _const((D_MODEL, D_MODEL))],
        out_specs=full, out_shape=jax.ShapeDtypeStruct((lp, D_MODEL), F32),
        compiler_params=_params(("parallel",), VMEM_LIMIT),
    )(head, x, ys, yp, gs, gp, w_out)


def _load_weights(hbm_refs, vmem_refs, sems):
    @pl.when(pl.program_id(0) == 0)
    def _():
        copies = [pltpu.make_async_copy(h, v, sems.at[n]) for n, (h, v) in enumerate(zip(hbm_refs, vmem_refs))]
        for cp in copies:
            cp.start()
        for cp in copies:
            cp.wait()


def _ffn_scratch():
    return [pltpu.VMEM((D_FF, D_MODEL), BF16)] * 3 + [pltpu.SemaphoreType.DMA((3,))]


def _ff_tile(t):
    return pl.ds(t * FF_TILE, FF_TILE)


def _ffn_forward(h1, g2, wg_t, wu_t, wd):
    lp = h1.shape[0]

    def body(h1_ref, g_ref, wg_hbm, wu_hbm, wd_hbm, ab_ref, n2_ref, h2_ref, wg, wu, wdn, sems):
        _load_weights((wg_hbm, wu_hbm, wd_hbm), (wg, wu, wdn), sems)
        h1_t = h1_ref[...]
        n2 = (h1_t * _rstd(h1_t) * g_ref[...]).astype(BF16)
        n2_ref[...] = n2
        acc = h1_t
        for t in range(D_FF // FF_TILE):
            a = _dot_nt(n2, wg[_ff_tile(t), :])
            b = _dot_nt(n2, wu[_ff_tile(t), :])
            ab_ref[:, _ff_tile(t)] = a.astype(BF16)
            ab_ref[:, pl.ds(D_FF + t * FF_TILE, FF_TILE)] = b.astype(BF16)
            acc = acc + _dot(a * _sigmoid(a) * b, wdn[_ff_tile(t), :])
        h2_ref[...] = acc

    _, _, full, _ = _row_specs()
    wide = pl.BlockSpec((TM, 2 * D_FF), lambda i: (i, 0))
    half_width = pl.BlockSpec((TM, D_MODEL), lambda i: (i, 0))
    return pl.pallas_call(
        body, name="ffn_forward", grid=(lp // TM,),
        in_specs=[full, _const((1, D_MODEL)), ANY, ANY, ANY], out_specs=[wide, half_width, full],
        out_shape=[jax.ShapeDtypeStruct((lp, 2 * D_FF), BF16), jax.ShapeDtypeStruct((lp, D_MODEL), BF16),
                   jax.ShapeDtypeStruct((lp, D_MODEL), F32)],
        scratch_shapes=_ffn_scratch(), compiler_params=_params(("arbitrary",), VMEM_LIMIT),
    )(h1, g2, wg_t, wu_t, wd)


def _ffn_backward(h2, target, h1, ab, gf, g2, wg_t, wu_t, wd):
    lp = h1.shape[0]

    def body(h2_ref, t_ref, h1_ref, ab_ref, gf_ref, g2_ref, wg_hbm, wu_hbm, wd_hbm,
             dh1_ref, xt_ref, dh2_ref, loss_ref, dgf_ref, dg2_ref, wg, wu, wdn, sems):
        i = pl.program_id(0)
        _load_weights((wg_hbm, wu_hbm, wd_hbm), (wg, wu, wdn), sems)

        @pl.when(i == 0)
        def _():
            loss_ref[...] = jnp.zeros(loss_ref.shape, F32)
            dgf_ref[...] = jnp.zeros(dgf_ref.shape, F32)
            dg2_ref[...] = jnp.zeros(dg2_ref.shape, F32)

        h2_t = h2_ref[...]
        rf = _rstd(h2_t)
        xf = h2_t * rf
        diff = jnp.where(i == 0, 0.0, xf * gf_ref[...] - t_ref[...])
        loss_ref[...] += 0.5 * jnp.sum(diff * diff) / D_MODEL
        dh2, dgf = _rms_bwd(diff / D_MODEL, xf, rf, gf_ref[...])
        dgf_ref[...] += dgf
        dh2_b = dh2.astype(BF16)
        dh2_ref[...] = dh2_b

        dn2 = jnp.zeros((TM, D_MODEL), F32)
        for t in range(D_FF // FF_TILE):
            dff = _dot_nt(dh2_b, wdn[_ff_tile(t), :])
            a = ab_ref[:, _ff_tile(t)].astype(F32)
            b = ab_ref[:, pl.ds(D_FF + t * FF_TILE, FF_TILE)].astype(F32)
            sg = _sigmoid(a)
            silu = a * sg
            da = dff * b * sg * (1.0 + a * (1.0 - sg))
            db = dff * silu
            for part, val in enumerate((da, db, silu * b)):
                xt_ref[pl.ds(part * D_FF + t * FF_TILE, FF_TILE), :] = val.T.astype(BF16)
            dn2 = dn2 + _dot(da, wg[_ff_tile(t), :]) + _dot(db, wu[_ff_tile(t), :])

        h1_t = h1_ref[...]
        r2 = _rstd(h1_t)
        dx, dg2 = _rms_bwd(dn2, h1_t * r2, r2, g2_ref[...])
        dg2_ref[...] += dg2
        dh1_ref[...] = dh2 + dx

    _, xrow, full, _ = _row_specs()
    wide = pl.BlockSpec((TM, 2 * D_FF), lambda i: (i, 0))
    half_width = pl.BlockSpec((TM, D_MODEL), lambda i: (i, 0))
    vec = _const((1, D_MODEL))
    return pl.pallas_call(
        body, name="ffn_backward", grid=(lp // TM,),
        in_specs=[full, xrow, full, wide, vec, vec, ANY, ANY, ANY],
        out_specs=[full, pl.BlockSpec((3 * D_FF, TM), lambda i: (0, i)), half_width, _const((1, PACK_LANES)), vec, vec],
        out_shape=[jax.ShapeDtypeStruct((lp, D_MODEL), F32),
                   jax.ShapeDtypeStruct((3 * D_FF, lp), BF16),
                   jax.ShapeDtypeStruct((lp, D_MODEL), BF16),
                   jax.ShapeDtypeStruct((1, PACK_LANES), F32),
                   jax.ShapeDtypeStruct((1, D_MODEL), F32), jax.ShapeDtypeStruct((1, D_MODEL), F32)],
        scratch_shapes=_ffn_scratch(), compiler_params=_params(("arbitrary",), VMEM_LIMIT),
    )(h2, target, h1, ab, gf, g2, wg_t, wu_t, wd)


def _ffn_wgrad(xt, n2, dh2):
    lp = n2.shape[0]
    rows = lp // WGRAD_STEPS
    n_tiles = 3 * D_FF // FF_TILE
    shards_per_tile = FF_TILE // FF_SHARD
    gate_up_tiles = 2 * D_FF // FF_TILE

    def body(xt_ref, n2_ref, dh2_ref, out_ref, acc):
        q, k = pl.program_id(0), pl.program_id(1)

        @pl.when(k == 0)
        def _():
            acc[...] = jnp.zeros(acc.shape, F32)

        @pl.when(q < gate_up_tiles)
        def _():
            acc[...] += jnp.dot(xt_ref[...], n2_ref[...], preferred_element_type=F32)

        @pl.when(q >= gate_up_tiles)
        def _():
            acc[...] += jnp.dot(xt_ref[...], dh2_ref[...], preferred_element_type=F32)

        @pl.when(k == pl.num_programs(1) - 1)
        def _():
            for s in range(shards_per_tile):
                out_ref[s] = acc[pl.ds(s * FF_SHARD, FF_SHARD), :].astype(BF16)

    tiles_per_matrix = D_FF // FF_TILE
    return pl.pallas_call(
        body, name="ffn_wgrad", grid=(n_tiles, WGRAD_STEPS),
        in_specs=[pl.BlockSpec((FF_TILE, rows), lambda q, k: (q, k)),
                  pl.BlockSpec((rows, D_MODEL), lambda q, k: (jnp.where(q < gate_up_tiles, k, 0), 0)),
                  pl.BlockSpec((rows, D_MODEL), lambda q, k: (jnp.where(q < gate_up_tiles, 0, k), 0))],
        out_specs=pl.BlockSpec((shards_per_tile, None, FF_SHARD, D_MODEL),
                               lambda q, k: (q % tiles_per_matrix, q // tiles_per_matrix, 0, 0)),
        out_shape=jax.ShapeDtypeStruct((N_DEV, 3, FF_SHARD, D_MODEL), BF16),
        scratch_shapes=[pltpu.VMEM((FF_TILE, D_MODEL), F32)],
        compiler_params=_params(("parallel", "arbitrary"), VMEM_LIMIT),
    )(xt, n2, dh2)


def _out_proj_backward(dh1, ys, yp, gs, gp, w_out):
    lp = ys.shape[0]

    def body(dh1_ref, ys_ref, yp_ref, gs_ref, gp_ref, w_ref, dys_ref, dyp_ref, dgs_ref, dgp_ref, dw_out, dw_ref):
        @pl.when(pl.program_id(0) == 0)
        def _():
            dgs_ref[...] = jnp.zeros(dgs_ref.shape, F32)
            dgp_ref[...] = jnp.zeros(dgp_ref.shape, F32)
            dw_ref[...] = jnp.zeros(dw_ref.shape, F32)

        dh1_b = dh1_ref[...].astype(BF16)
        dmix = _dot_nt(dh1_b, w_ref[...])
        for y_ref, g_ref, dy_ref, dg_ref, lo in ((ys_ref, gs_ref, dys_ref, dgs_ref, 0),
                                                 (yp_ref, gp_ref, dyp_ref, dgp_ref, D_SSM)):
            y_t = y_ref[...]
            r = _rstd(y_t)
            xhat = y_t * r
            dy, dg = _rms_bwd(dmix[:, lo:lo + D_SSM], xhat, r, g_ref[...])
            dy_ref[...] = dy
            dg_ref[...] += dg
            dw_ref[pl.ds(lo, D_SSM), :] += _dot_tn(xhat * g_ref[...], dh1_b)

        @pl.when(pl.program_id(0) == pl.num_programs(0) - 1)
        def _():
            dw_out[...] = dw_ref[...].astype(BF16)

    _, _, full, half = _row_specs()
    vec = _const((1, D_SSM))
    return pl.pallas_call(
        body, name="out_proj_backward", grid=(lp // TM,),
        in_specs=[full, half, half, vec, vec, _const((D_MODEL, D_MODEL))],
        out_specs=[half, half, vec, vec, _const((D_MODEL, D_MODEL))],
        out_shape=[jax.ShapeDtypeStruct((lp, D_SSM), F32)] * 2 + [jax.ShapeDtypeStruct((1, D_SSM), F32)] * 2
        + [jax.ShapeDtypeStruct((D_MODEL, D_MODEL), BF16)],
        scratch_shapes=[pltpu.VMEM((D_MODEL, D_MODEL), F32)],
        compiler_params=_params(("arbitrary",), VMEM_LIMIT),
    )(dh1, ys, yp, gs, gp, w_out)


def _in_proj_backward(head, x, du, dv, dh1, g1, w_in):
    lp = du.shape[0]

    def body(head_ref, x_ref, du_ref, dv_ref, dh1_ref, g_ref, w_ref, dx_ref, dhead_ref, dg_ref, dw_out, dw_ref):
        i = pl.program_id(0)

        @pl.when(i == 0)
        def _():
            dg_ref[...] = jnp.zeros(dg_ref.shape, F32)
            dw_ref[...] = jnp.zeros(dw_ref.shape, F32)

        h0 = jnp.where(i == 0, head_ref[...], x_ref[...])
        r = _rstd(h0)
        xhat = h0 * r
        n1 = (xhat * g_ref[...]).astype(BF16)
        du_b, dv_b = du_ref[...].astype(BF16), dv_ref[...].astype(BF16)
        dn1 = _dot_nt(du_b, w_ref[:, pl.ds(0, D_SSM)]) + _dot_nt(dv_b, w_ref[:, pl.ds(D_SSM, D_SSM)])
        dx, dg = _rms_bwd(dn1, xhat, r, g_ref[...])
        dg_ref[...] += dg
        dh0 = dh1_ref[...] + dx
        dx_ref[...] = dh0

        @pl.when(i == 0)
        def _():
            dhead_ref[...] = dh0

        dw_ref[:, pl.ds(0, D_SSM)] += _dot_tn(n1, du_b)
        dw_ref[:, pl.ds(D_SSM, D_SSM)] += _dot_tn(n1, dv_b)

        @pl.when(i == pl.num_programs(0) - 1)
        def _():
            dw_out[...] = dw_ref[...].astype(BF16)

    head_s, xrow, full, half = _row_specs()
    vec = _const((1, D_MODEL))
    mat = _const((D_MODEL, D_MODEL))
    return pl.pallas_call(
        body, name="in_proj_backward", grid=(lp // TM,),
        in_specs=[head_s, xrow, half, half, full, vec, mat],
        out_specs=[xrow, head_s, vec, mat],
        out_shape=[jax.ShapeDtypeStruct(x.shape, F32), jax.ShapeDtypeStruct((HEAD, D_MODEL), F32),
                   jax.ShapeDtypeStruct((1, D_MODEL), F32), jax.ShapeDtypeStruct((D_MODEL, D_MODEL), BF16)],
        scratch_shapes=[pltpu.VMEM((D_MODEL, D_MODEL), F32)],
        compiler_params=_params(("arbitrary",), VMEM_LIMIT),
    )(head, x, du, dv, dh1, g1, w_in)


def _permute_rows(a):
    lp, n = a.shape
    return a.reshape(SUBLANES, lp // SUBLANES, n).transpose(1, 0, 2).reshape(lp, n)


def _unpermute_rows(a):
    lp, n = a.shape
    return a.reshape(lp // SUBLANES, SUBLANES, n).transpose(1, 0, 2).reshape(lp, n)


def _pack(parts, dtype):
    rows = []
    for p in parts:
        flat = p.reshape(-1).astype(dtype)
        pad = (-flat.shape[0]) % PACK_UNIT
        rows.append(jnp.pad(flat, (0, pad)).reshape(-1, PACK_LANES))
    n_rows = sum(r.shape[0] for r in rows)
    if n_rows % 16:
        rows.append(jnp.zeros((8, PACK_LANES), dtype))
    return jnp.concatenate(rows, axis=0)


def _as2d(a):
    return a.reshape(-1, a.shape[-1])


def _unpack(packed, shapes):
    out, row = [], 0
    for shape in shapes:
        size = 1
        for s in shape:
            size *= s
        n_rows = -(-size // PACK_UNIT) * 8
        out.append(packed[row:row + n_rows].reshape(-1)[:size].reshape(shape))
        row += n_rows
    return out


def _to_view(name, a):
    if name in ("ssm_b_re", "ssm_b_im"):
        return a[0].transpose(0, 2, 1).reshape(-1, SSM_STATE)
    if name in ("ssm_d", "ssm_glu_b"):
        return a[0].T
    if name == "ssm_glu_w":
        return a[0].transpose(1, 2, 0).reshape(-1, SSM_GROUPS)
    return _as2d(a)


def _from_view(name, r, shape):
    if name in ("ssm_b_re", "ssm_b_im"):
        return r.reshape(SSM_GROUPS, SSM_GROUP, SSM_STATE).transpose(0, 2, 1).reshape(shape)
    if name in ("ssm_d", "ssm_glu_b"):
        return r.T.reshape(shape)
    if name == "ssm_glu_w":
        return r.reshape(SSM_GROUP, SSM_GROUP, SSM_GROUPS).transpose(2, 0, 1).reshape(shape)
    return r.reshape(shape)


def kernel(x, meta_tokens, norm1_g, w_in, ssm_lambda_re, ssm_lambda_im, ssm_log_step, ssm_b_re, ssm_b_im, ssm_c_re, ssm_c_im, ssm_d, ssm_glu_w, ssm_glu_b, ssm_norm_g, pool_w, pool_scale, pool_norm_g, w_out, norm2_g, w_gate, w_up, w_down, final_norm_g, loss_target, m_meta_tokens, m_norm1_g, m_w_in, m_ssm_lambda_re, m_ssm_lambda_im, m_ssm_log_step, m_ssm_b_re, m_ssm_b_im, m_ssm_c_re, m_ssm_c_im, m_ssm_d, m_ssm_glu_w, m_ssm_glu_b, m_ssm_norm_g, m_pool_w, m_pool_scale, m_pool_norm_g, m_w_out, m_norm2_g, m_w_gate, m_w_up, m_w_down, m_final_norm_g, v_meta_tokens, v_norm1_g, v_w_in, v_ssm_lambda_re, v_ssm_lambda_im, v_ssm_log_step, v_ssm_b_re, v_ssm_b_im, v_ssm_c_re, v_ssm_c_im, v_ssm_d, v_ssm_glu_w, v_ssm_glu_b, v_ssm_norm_g, v_pool_w, v_pool_scale, v_pool_norm_g, v_w_out, v_norm2_g, v_w_gate, v_w_up, v_w_down, v_final_norm_g):
    given = dict(locals())
    weights = {n: given[n] for n in WEIGHT_NAMES}
    n_meta = meta_tokens.shape[0]
    me = 4 * lax.axis_index("x") + 2 * lax.axis_index("y") + lax.axis_index("c")

    shard_rows = w_in.shape[1]
    first = [w_in[0].astype(BF16), meta_tokens]
    first_make = _push_copies(ALL_PEERS, [False, False])
    first_x, first_token = _split_start(first + [_landing(s, False) for s in first], first_make,
                                        2 * len(ALL_PEERS), "gather_w_in_start")

    xs = x[0]
    tgt = loss_target[0]
    first_row = HEAD - n_meta
    g1, g2, gf = norm1_g, norm2_g, final_norm_g.reshape(1, D_MODEL)
    gs, gp = ssm_norm_g, pool_norm_g

    lam_re, lam_im = ssm_lambda_re[0] + first_token[:1, :1], ssm_lambda_im[0]
    log_step = ssm_log_step[0].reshape(SSM_GROUPS, 1)
    b_re = ssm_b_re[0].transpose(0, 2, 1)
    b_im = ssm_b_im[0].transpose(0, 2, 1)
    abr, abi, zr, zi = _s5_disc_a(lam_re, lam_im, log_step)
    zr_col, zi_col = zr.reshape(SSM_GROUPS, 1, SSM_STATE), zi.reshape(SSM_GROUPS, 1, SSM_STATE)
    bbr, bbi = _s5_disc_b(zr_col, zi_col, b_re, b_im)
    s5_consts = (abr.reshape(1, -1), abi.reshape(1, -1), bbr, bbi, ssm_c_re[0], ssm_c_im[0],
                 ssm_d[0].reshape(1, D_SSM), ssm_glu_w[0], ssm_glu_b[0].reshape(1, D_SSM))
    pool_sc = pool_scale[0].reshape(POOL_GROUPS, 1, POOL_DIM)

    (_, _, w_in_all, meta_all), first_done = _split_wait(first_x, first_make, bbr, "gather_w_in_wait")
    w_in_all = w_in_all.reshape(D_MODEL, D_MODEL)
    meta_full = meta_all.transpose(1, 0, 2).reshape(n_meta, D_MODEL)
    head = jnp.concatenate([jnp.zeros((HEAD - n_meta, D_MODEL), F32), meta_full], axis=0)
    shards = [(w_out[0] + first_done[:1, :1]).astype(BF16), w_gate[0].T.astype(BF16), w_up[0].T.astype(BF16),
              w_down[0].astype(BF16)]
    n_big = len(shards)
    gather_make = _push_copies((SIBLING,) + CHIP_PEERS, [False] * n_big)
    gather, gather_token = _split_start(shards + [_landing(s, False) for s in shards], gather_make,
                                        n_big * (1 + len(CHIP_PEERS)), "gather_start")

    u, v = _in_proj(head, xs, g1 + gather_token[:1, :1], w_in_all)
    u_p = _permute_rows(u)
    ys_p = _s5_forward(u_p, *s5_consts)
    landed, _ = _split_wait(gather, gather_make, ys_p, "gather_wait")
    forward_make = _forward_copies(n_big)
    forward, forward_token = _split_start(list(landed[n_big:]), forward_make, n_big * len(CHIP_PEERS),
                                          "gather_forward_start")
    ys = _unpermute_rows(ys_p)
    yp = _pool_forward(v, pool_w[0], pool_sc + forward_token[:1, :1], first_row)
    (w_out_all, wg_t, wu_t, wd_all), _ = _split_wait(forward, forward_make, yp, "gather_forward_wait")
    w_out_all = w_out_all.reshape(D_MODEL, D_MODEL)
    ffn_weights = [w.reshape(D_FF, D_MODEL) for w in (wg_t, wu_t, wd_all)]
    h1 = _out_proj(head, xs, ys, yp, gs, gp, w_out_all)
    ab, n2, h2 = _ffn_forward(h1, g2, *ffn_weights)

    dh1, xt, dh2, loss_part, d_gf, d_g2 = _ffn_backward(h2, tgt, h1, ab, gf, g2, *ffn_weights)
    d_ffn = _ffn_wgrad(xt, n2, dh2)
    ffn_make = _push_copies(ALL_PEERS, [True])
    ffn_x, ffn_token = _split_start([d_ffn, _landing(d_ffn, True)], ffn_make, len(ALL_PEERS), "ffn_grad_start")
    dys, dyp, d_gs, d_gp, d_wout = _out_proj_backward(dh1, ys, yp, gs + ffn_token[:1, :1], gp, w_out_all)
    dv, d_pool_w, d_pool_sc = _pool_backward(v, dyp, pool_w[0], pool_sc, first_row)
    (du_p, d_ar, d_ai, d_bbr, d_bbi, d_c_re, d_c_im, d_d, d_glu, d_glub) = _s5_backward(
        u_p, _permute_rows(dys), *s5_consts)
    du = _unpermute_rows(du_p)

    d_zr, d_zi, d_b_re, d_b_im = _s5_disc_b_bwd(zr_col, zi_col, b_re, b_im, d_bbr, d_bbi)
    d_lam_re, d_lam_im, d_log_step = _s5_disc_a_bwd(
        lam_re, lam_im, log_step,
        (d_ar.reshape(SSM_GROUPS, SSM_STATE), d_ai.reshape(SSM_GROUPS, SSM_STATE),
         d_zr.reshape(SSM_GROUPS, SSM_STATE), d_zi.reshape(SSM_GROUPS, SSM_STATE)))
    groups_last = lambda row: row.reshape(SSM_GROUPS, SSM_GROUP).T
    small_grads = {
        "ssm_lambda_re": d_lam_re, "ssm_lambda_im": d_lam_im, "ssm_log_step": d_log_step.reshape(1, SSM_GROUPS),
        "ssm_b_re": d_b_re.reshape(-1, SSM_STATE), "ssm_b_im": d_b_im.reshape(-1, SSM_STATE),
        "ssm_c_re": d_c_re.reshape(-1, SSM_STATE), "ssm_c_im": d_c_im.reshape(-1, SSM_STATE),
        "ssm_d": groups_last(d_d), "ssm_glu_w": d_glu.transpose(1, 2, 0).reshape(-1, SSM_GROUPS),
        "ssm_glu_b": groups_last(d_glub),
        "ssm_norm_g": d_gs, "pool_w": d_pool_w.reshape(-1, POOL_DIM), "pool_scale": d_pool_sc.reshape(-1, POOL_DIM),
        "pool_norm_g": d_gp, "norm2_g": d_g2,
    }

    early_names = SMALL_NAMES[1:-1]
    early_pack = _pack([small_grads[n] for n in early_names], BF16)
    d_wout = d_wout.reshape(N_DEV, shard_rows, D_MODEL)
    early_make = _push_copies(ALL_PEERS, [True, False])
    early_x, early_token = _split_start([d_wout, early_pack, _landing(d_wout, True), _landing(early_pack, False)],
                                        early_make, 2 * len(ALL_PEERS), "early_grad_start")
    d_x, d_head, d_g1, d_win = _in_proj_backward(head, xs, du, dv, dh1, g1 + early_token[:1, :1], w_in_all)
    (_, r_ffn), _ = _split_wait(ffn_x, ffn_make, d_g1, "ffn_grad_wait")
    (_, _, r_wout, r_early), _ = _split_wait(early_x, early_make, d_g1, "early_grad_wait")
    d_win = d_win.reshape(N_DEV, shard_rows, D_MODEL)
    late_pack = _pack([d_g1, d_gf, d_head[first_row:], loss_part], F32)
    late_make = _push_copies(ALL_PEERS, [True, False])
    late_x, late_token = _split_start([d_win, late_pack, _landing(d_win, True), _landing(late_pack, False)],
                                      late_make, 2 * len(ALL_PEERS), "late_grad_start")

    results = {}
    res_gate = _adamw_part(r_ffn, 0, w_gate[0].T + late_token[:1, :1], m_w_gate[0].T, v_w_gate[0].T, "adamw_w_gate")
    res_up = _adamw_part(r_ffn, 1, w_up[0].T, m_w_up[0].T, v_w_up[0].T, "adamw_w_up")
    results["w_gate"] = [r.T for r in res_gate]
    results["w_up"] = [r.T for r in res_up]
    results["w_down"] = _adamw_part(r_ffn, 2, w_down[0], m_w_down[0], v_w_down[0], "adamw_w_down")
    results["w_out"] = _adamw(r_wout, w_out[0], m_w_out[0], v_w_out[0], shard_rows, "adamw_w_out")
    done = sum(res[1][:1, :1] for res in (res_gate, res_up, results["w_down"], results["w_out"]))
    (_, _, r_win, r_late), _ = _split_wait(late_x, late_make, done, "late_grad_wait")
    results["w_in"] = _adamw(r_win, w_in[0], m_w_in[0], v_w_in[0], shard_rows, "adamw_w_in")

    sum_early, sum_late = _reduce_slots([r_early, r_late], "small_grad_sums")
    views = lambda prefix: [_to_view(n, given[prefix + n]) for n in SMALL_NAMES]
    w_views = views("")
    g_views = _unpack(sum_early, [w.shape for w in w_views[1:-1]])
    g_norm1, g_final, g_meta_all, loss_row = _unpack(
        sum_late, [norm1_g.shape, (1, D_MODEL), (n_meta, D_MODEL), (1, PACK_LANES)])
    g_views = [g_norm1] + g_views + [g_final]
    res_small = _adamw_many(g_views, w_views, views("m_"), views("v_"), "adamw_small")
    for idx, n in enumerate(SMALL_NAMES):
        results[n] = [_from_view(n, part[idx], weights[n].shape) for part in (g_views,) + tuple(res_small)]
    shard_cols = meta_tokens.shape[1]
    g_meta = lax.dynamic_slice_in_dim(g_meta_all, me * shard_cols, shard_cols, axis=1)
    results["meta_tokens"] = _adamw(g_meta[None], meta_tokens, m_meta_tokens, v_meta_tokens, n_meta, "adamw_meta")

    out = [loss_row[0, 0], d_x[None]]
    for part in range(4):
        for n in WEIGHT_NAMES:
            out.append(results[n][part].reshape(weights[n].shape))
    return tuple(out)
```

```python
import jax
import jax.numpy as jnp
from jax import lax
from jax.experimental import pallas as pl
from jax.experimental.pallas import tpu as pltpu

F32 = jnp.float32
BF16 = jnp.bfloat16

N_DEV = 8
D_MODEL = 1024
D_SSM = 512
SSM_GROUP = 16
SSM_STATE = 64
SSM_GROUPS = 32
POOL_GROUPS = 4
POOL_DIM = 128
COL_U = 128
COL_S = 512
N_COL = D_SSM // COL_U
GROUPS_PER_COL = COL_U // SSM_GROUP
D_FF = 2816
FF_SHARD = D_FF // N_DEV
FF_TILE = D_FF // 2
TM = 256
WGRAD_GROUP = 4
WGRAD_STEPS = 2
HEAD = TM
SUBLANES = 8
SCAN_UNROLL = 4
POOL_HALO = 16
EPS = 1e-6
STEP_FLOOR = -1e-4
VMEM_LIMIT = 60 * 1024 * 1024

ADAM_LR = 0.001
ADAM_B1 = 0.9
ADAM_B2 = 0.999
ADAM_EPS = 1e-08
ADAM_WD = 0.01
ADAM_STEP = 10

MESH_ID = pl.DeviceIdType.MESH
ANY = pl.BlockSpec(memory_space=pl.ANY)

SMALL_NAMES = ("norm1_g", "ssm_lambda_re", "ssm_lambda_im", "ssm_log_step", "ssm_b_re", "ssm_b_im",
               "ssm_c_re", "ssm_c_im", "ssm_d", "ssm_glu_w", "ssm_glu_b", "ssm_norm_g", "pool_w",
               "pool_scale", "pool_norm_g", "norm2_g", "final_norm_g")
WEIGHT_NAMES = ("meta_tokens", "norm1_g", "w_in", "ssm_lambda_re", "ssm_lambda_im", "ssm_log_step",
                "ssm_b_re", "ssm_b_im", "ssm_c_re", "ssm_c_im", "ssm_d", "ssm_glu_w", "ssm_glu_b",
                "ssm_norm_g", "pool_w", "pool_scale", "pool_norm_g", "w_out", "norm2_g", "w_gate",
                "w_up", "w_down", "final_norm_g")
LANES = 128
PACK_LANES = LANES
PACK_UNIT = 8 * PACK_LANES


def _dot(a, b):
    return jnp.dot(a.astype(BF16), b.astype(BF16), preferred_element_type=F32)


def _dot_nt(a, b):
    return lax.dot_general(a.astype(BF16), b.astype(BF16), (((1,), (1,)), ((), ())), preferred_element_type=F32)


def _dot_tn(a, b):
    return lax.dot_general(a.astype(BF16), b.astype(BF16), (((0,), (0,)), ((), ())), preferred_element_type=F32)


def _sigmoid(x):
    return 1.0 / (1.0 + jnp.exp(-x))


def _rstd(x):
    return lax.rsqrt(jnp.mean(x * x, axis=-1, keepdims=True) + EPS)


def _rms_bwd(dy, xhat, r, g):
    dxh = dy * g
    dx = r * (dxh - xhat * jnp.mean(dxh * xhat, axis=-1, keepdims=True))
    return dx, jnp.sum(dy * xhat, axis=0, keepdims=True)


def _params(sem, vmem=None):
    return pltpu.CompilerParams(dimension_semantics=sem, vmem_limit_bytes=vmem)


def _const(shape):
    return pl.BlockSpec(shape, lambda *_: (0,) * len(shape))


def _xrow(i):
    return (jnp.maximum(i - 1, 0), 0)


HBM = pl.BlockSpec(memory_space=pltpu.HBM)
SEM = pl.BlockSpec(memory_space=pltpu.SEMAPHORE)
EFFECT = pltpu.SideEffectType.DATAFLOW_SIDE_EFFECTING
ALL_PEERS = tuple(range(1, N_DEV))
SIBLING = 1
CHIP_PEERS = (2, 4, 6)


def _me():
    return 4 * lax.axis_index("x") + 2 * lax.axis_index("y") + lax.axis_index("c")


def _peer(k):
    x, y, c = lax.axis_index("x"), lax.axis_index("y"), lax.axis_index("c")
    px = 1 - x if k & 4 else x
    py = 1 - y if k & 2 else y
    pc = 1 - c if k & 1 else c
    return (px, py, pc), 4 * px + 2 * py + pc


def _landing(arr, scatter):
    if scatter:
        own = lax.dynamic_index_in_dim(arr, _me(), 0, keepdims=False)
    else:
        own = arr
    return lax.dynamic_update_index_in_dim(lax.empty((N_DEV,) + own.shape, arr.dtype), own, _me(), 0)


def _push_copies(peers, scatter):
    n_arr = len(scatter)

    def make(refs, send_sems, recv_sems):
        copies = []
        for a in range(n_arr):
            for i, k in enumerate(peers):
                peer_id, peer = _peer(k)
                sem = a * len(peers) + i
                copies.append(pltpu.make_async_remote_copy(
                    src_ref=refs[a].at[peer] if scatter[a] else refs[a], dst_ref=refs[n_arr + a].at[_me()],
                    send_sem=send_sems.at[sem], recv_sem=recv_sems.at[sem],
                    device_id=peer_id, device_id_type=MESH_ID))
        return copies
    return make


def _forward_copies(n_arr):
    def make(refs, send_sems, recv_sems):
        copies = []
        sibling_id, _ = _peer(SIBLING)
        for a in range(n_arr):
            for i, k in enumerate(CHIP_PEERS):
                slot = refs[a].at[_peer(k)[1]]
                sem = a * len(CHIP_PEERS) + i
                copies.append(pltpu.make_async_remote_copy(
                    src_ref=slot, dst_ref=slot, send_sem=send_sems.at[sem], recv_sem=recv_sems.at[sem],
                    device_id=sibling_id, device_id_type=MESH_ID))
        return copies
    return make


def _split_start(operands, make, n_sem, name):
    n_op = len(operands)

    def body(*refs):
        for cp in make(refs[:n_op], refs[n_op], refs[n_op + 1]):
            cp.start()
        refs[-1][...] = jnp.zeros(refs[-1].shape, F32)

    out = pl.pallas_call(
        body, name=name,
        out_shape=(pltpu.SemaphoreType.DMA((n_sem,)), pltpu.SemaphoreType.DMA((n_sem,)),
                   *[pltpu.HBM(t.shape, t.dtype) for t in operands], jax.ShapeDtypeStruct((8, PACK_LANES), F32)),
        in_specs=[HBM] * n_op, out_specs=(SEM, SEM, *[HBM] * n_op, pl.BlockSpec(memory_space=pltpu.VMEM)),
        input_output_aliases={i: 2 + i for i in range(n_op)},
        compiler_params=pltpu.CompilerParams(has_side_effects=EFFECT),
    )(*[pltpu.with_memory_space_constraint(t, pltpu.HBM) for t in operands])
    return out[:-1], out[-1]


def _split_wait(started, make, after, name):
    send_sems, recv_sems, thru = started[0], started[1], started[2:]
    n_op = len(thru)

    def body(*refs):
        for cp in make(refs[:n_op], refs[n_op], refs[n_op + 1]):
            cp.wait_send()
            cp.wait_recv()
        refs[-1][...] = jnp.zeros(refs[-1].shape, F32)

    out = pl.pallas_call(
        body, name=name,
        out_shape=(*[pltpu.HBM(t.shape, t.dtype) for t in thru], jax.ShapeDtypeStruct((8, PACK_LANES), F32)),
        in_specs=[HBM] * n_op + [SEM, SEM, ANY], out_specs=(*[HBM] * n_op, pl.BlockSpec(memory_space=pltpu.VMEM)),
        input_output_aliases={i: i for i in range(n_op)},
        compiler_params=pltpu.CompilerParams(has_side_effects=EFFECT),
    )(*thru, send_sems, recv_sems, after)
    return out[:-1], out[-1]


def _adamw_math(g, w, m, v):
    nm = ADAM_B1 * m + (1.0 - ADAM_B1) * g
    nv = ADAM_B2 * v + (1.0 - ADAM_B2) * (g * g)
    m_hat = nm / (1.0 - ADAM_B1 ** ADAM_STEP)
    v_hat = nv / (1.0 - ADAM_B2 ** ADAM_STEP)
    return -ADAM_LR * (m_hat / (jnp.sqrt(v_hat) + ADAM_EPS) + ADAM_WD * w), nm, nv


def _sum_slots(s_ref):
    g = s_ref[0].astype(F32)
    for s in range(1, s_ref.shape[0]):
        g = g + s_ref[s].astype(F32)
    return g


def _adamw(slots, w, m, v, tile_rows, name):
    n, rows, cols = slots.shape

    def body(s_ref, w_ref, m_ref, v_ref, g_ref, d_ref, nm_ref, nv_ref):
        g = _sum_slots(s_ref)
        g_ref[...] = g
        d_ref[...], nm_ref[...], nv_ref[...] = _adamw_math(g, w_ref[...], m_ref[...], v_ref[...])

    tile = pl.BlockSpec((tile_rows, cols), lambda i: (i, 0))
    return pl.pallas_call(
        body, name=name, grid=(rows // tile_rows,),
        in_specs=[pl.BlockSpec((n, tile_rows, cols), lambda i: (0, i, 0)), tile, tile, tile],
        out_specs=[tile] * 4, out_shape=[jax.ShapeDtypeStruct((rows, cols), F32)] * 4,
        compiler_params=_params(("parallel",), VMEM_LIMIT),
    )(slots, w, m, v)


def _adamw_part(slots, part, w, m, v, name):
    n, _, rows, cols = slots.shape
    tile_cols = 256

    def body(s_ref, w_ref, m_ref, v_ref, g_ref, d_ref, nm_ref, nv_ref):
        g = _sum_slots(s_ref)
        g_ref[...] = g
        d_ref[...], nm_ref[...], nv_ref[...] = _adamw_math(g, w_ref[...], m_ref[...], v_ref[...])

    tile = pl.BlockSpec((rows, tile_cols), lambda i: (0, i))
    return pl.pallas_call(
        body, name=name, grid=(cols // tile_cols,),
        in_specs=[pl.BlockSpec((n, None, rows, tile_cols), lambda i: (0, part, 0, i)), tile, tile, tile],
        out_specs=[tile] * 4, out_shape=[jax.ShapeDtypeStruct((rows, cols), F32)] * 4,
        compiler_params=_params(("parallel",), VMEM_LIMIT),
    )(slots, w, m, v)


def _reduce_slots(slot_arrays, name):
    def body(*refs):
        n_arr = len(refs) // 2
        for s_ref, o_ref in zip(refs[:n_arr], refs[n_arr:]):
            o_ref[...] = _sum_slots(s_ref)
    return pl.pallas_call(
        body, name=name, out_shape=[jax.ShapeDtypeStruct(s.shape[1:], F32) for s in slot_arrays],
        compiler_params=_params(None, VMEM_LIMIT))(*slot_arrays)


def _adamw_many(grads, ws, ms, vs, name):
    n = len(grads)

    def body(*refs):
        ins, outs = refs[:4 * n], refs[4 * n:]
        for i in range(n):
            g, w, m, v = (ins[j * n + i][...] for j in range(4))
            outs[i][...], outs[n + i][...], outs[2 * n + i][...] = _adamw_math(g, w, m, v)

    out = pl.pallas_call(
        body, name=name, out_shape=[jax.ShapeDtypeStruct(w.shape, F32) for w in ws] * 3,
        compiler_params=_params(None, VMEM_LIMIT))(*grads, *ws, *ms, *vs)
    return out[:n], out[n:2 * n], out[2 * n:]


def _disc_a(lam_re, lam_im, log_step):
    lr = jnp.minimum(lam_re, STEP_FLOOR)
    step = jnp.exp(log_step)
    mag = jnp.exp(lr * step)
    ang = lam_im * step
    abr = mag * jnp.cos(ang)
    abi = mag * jnp.sin(ang)
    nr = abr - 1.0
    den = lr * lr + lam_im * lam_im
    cr = (nr * lr + abi * lam_im) / den
    ci = (abi * lr - nr * lam_im) / den
    return abr, abi, cr, ci


def _disc_b(cr, ci, b_re, b_im):
    return cr * b_re - ci * b_im, cr * b_im + ci * b_re


def _s5_disc_a(lam_re, lam_im, log_step):
    def body(lr_ref, li_ref, ls_ref, *outs):
        for o, val in zip(outs, _disc_a(lr_ref[...], li_ref[...], ls_ref[...])):
            o[...] = val
    return pl.pallas_call(body, name="s5_disc_a", out_shape=[jax.ShapeDtypeStruct(lam_re.shape, F32)] * 4)(
        lam_re, lam_im, log_step)


def _s5_disc_a_bwd(lam_re, lam_im, log_step, cts):
    def body(lr_ref, li_ref, ls_ref, c0, c1, c2, c3, dlr_ref, dli_ref, dls_ref):
        _, vjp = jax.vjp(_disc_a, lr_ref[...], li_ref[...], ls_ref[...])
        dlr, dli, dls = vjp((c0[...], c1[...], c2[...], c3[...]))
        dlr_ref[...] = dlr
        dli_ref[...] = dli
        dls_ref[...] = dls
    return pl.pallas_call(
        body, name="s5_disc_a_bwd",
        out_shape=[jax.ShapeDtypeStruct(lam_re.shape, F32), jax.ShapeDtypeStruct(lam_re.shape, F32),
                   jax.ShapeDtypeStruct(log_step.shape, F32)])(lam_re, lam_im, log_step, *cts)


def _s5_disc_b(cr, ci, b_re, b_im):
    def body(cr_ref, ci_ref, br_ref, bi_ref, o_re, o_im):
        o_re[...], o_im[...] = _disc_b(cr_ref[...], ci_ref[...], br_ref[...], bi_ref[...])
    return pl.pallas_call(body, name="s5_disc_b", out_shape=[jax.ShapeDtypeStruct(b_re.shape, F32)] * 2)(
        cr, ci, b_re, b_im)


def _s5_disc_b_bwd(cr, ci, b_re, b_im, d_re, d_im):
    def body(cr_ref, ci_ref, br_ref, bi_ref, dr_ref, di_ref, dcr_ref, dci_ref, dbr_ref, dbi_ref):
        _, vjp = jax.vjp(_disc_b, cr_ref[...], ci_ref[...], br_ref[...], bi_ref[...])
        dcr_ref[...], dci_ref[...], dbr_ref[...], dbi_ref[...] = vjp((dr_ref[...], di_ref[...]))
    return pl.pallas_call(
        body, name="s5_disc_b_bwd",
        out_shape=[jax.ShapeDtypeStruct(cr.shape, F32)] * 2 + [jax.ShapeDtypeStruct(b_re.shape, F32)] * 2)(
            cr, ci, b_re, b_im, d_re, d_im)


def _cmul(ar, ai, br, bi):
    return ar * br - ai * bi, ar * bi + ai * br


def _cpow(ar, ai, n):
    rr, ri = jnp.ones_like(ar), jnp.zeros_like(ai)
    while n:
        if n & 1:
            rr, ri = _cmul(rr, ri, ar, ai)
        n >>= 1
        if n:
            ar, ai = _cmul(ar, ai, ar, ai)
    return rr, ri


def _tile_rows(i):
    if isinstance(i, int):
        return pl.ds(i * SUBLANES, SUBLANES)
    return pl.ds(pl.multiple_of(i * SUBLANES, SUBLANES), SUBLANES)


def _segment_scan(z_re, z_im, ar, ai, lseg, reverse, visit=None):
    shape = (SUBLANES, z_re.shape[1])
    half = lseg // 2
    arb = jnp.broadcast_to(ar, shape)
    aib = jnp.broadcast_to(ai, shape)
    zero = jnp.zeros(shape, F32)
    row = lax.broadcasted_iota(jnp.int32, shape, 0)

    def tiles(k):
        return (lseg - 1 - k, half - 1 - k) if reverse else (k, half + k)

    def advance(tile, sr, si):
        rows = _tile_rows(tile)
        nr, ni = _cmul(arb, aib, sr, si)
        return rows, nr + z_re[rows, :], ni + z_im[rows, :]

    def first_pass(k, carry):
        ta, tb = tiles(k)
        return advance(ta, carry[0], carry[1])[1:] + advance(tb, carry[2], carry[3])[1:]

    def unrolled(step):
        def body(it, carry):
            for j in range(SCAN_UNROLL):
                carry = step(it * SCAN_UNROLL + j, carry)
            return carry
        return body

    n_iter = half // SCAN_UNROLL
    fa_r, fa_i, fb_r, fb_i = lax.fori_loop(0, n_iter, unrolled(first_pass), (zero,) * 4)
    hr, hi = _cpow(arb, aib, half)
    pr, pi = _cmul(hr, hi, hr, hi)
    fr, fi = _cmul(hr, hi, fa_r, fa_i)
    fr, fi = fr + fb_r, fi + fb_i
    cr, ci = zero, zero
    for _ in range(SUBLANES - 1):
        tr, ti = _cmul(pr, pi, cr, ci)
        tr, ti = tr + fr, ti + fi
        if reverse:
            cr = jnp.where(row == SUBLANES - 1, 0.0, pltpu.roll(tr, SUBLANES - 1, 0))
            ci = jnp.where(row == SUBLANES - 1, 0.0, pltpu.roll(ti, SUBLANES - 1, 0))
        else:
            cr = jnp.where(row == 0, 0.0, pltpu.roll(tr, 1, 0))
            ci = jnp.where(row == 0, 0.0, pltpu.roll(ti, 1, 0))

    br, bi = _cmul(hr, hi, cr, ci)
    br, bi = br + fa_r, bi + fa_i

    def second_pass(k, carry, b_is_tile0=False):
        states, acc = list(carry[:4]), carry[4]
        for chain, tile in enumerate(tiles(k)):
            rows, nr, ni = advance(tile, states[2 * chain], states[2 * chain + 1])
            z_re[rows, :] = nr
            z_im[rows, :] = ni
            states[2 * chain], states[2 * chain + 1] = nr, ni
            if visit is not None:
                acc = visit(tile, nr, ni, acc, chain == 1 and b_is_tile0)
        return (*states, acc)

    acc0 = (zero, zero) if visit is not None else 0
    carry = lax.fori_loop(0, n_iter - 1, unrolled(second_pass), (cr, ci, br, bi, acc0))
    for k in range(half - SCAN_UNROLL, half - 1):
        carry = second_pass(k, carry)
    return second_pass(half - 1, carry, b_is_tile0=reverse)[4]


def _gelu(y):
    c = 0.7978845608028654
    return 0.5 * y * (1.0 + jnp.tanh(c * (y + 0.044715 * y * y * y)))


def _gelu_grad(y):
    c = 0.7978845608028654
    th = jnp.tanh(c * (y + 0.044715 * y * y * y))
    return 0.5 * (1.0 + th) + 0.5 * y * (1.0 - th * th) * c * (1.0 + 3.0 * 0.044715 * y * y)


def _s5_specs(lp):
    col_u = pl.BlockSpec((lp, COL_U), lambda j: (0, j))
    row_u = pl.BlockSpec((1, COL_U), lambda j: (0, j))
    row_s = pl.BlockSpec((1, COL_S), lambda j: (0, j))
    bc_blk = pl.BlockSpec((GROUPS_PER_COL, SSM_GROUP, SSM_STATE), lambda j: (j, 0, 0))
    glu_blk = pl.BlockSpec((GROUPS_PER_COL, SSM_GROUP, SSM_GROUP), lambda j: (j, 0, 0))
    return col_u, row_u, row_s, bc_blk, glu_blk


def _s5_block_diag_scratch():
    return ([pltpu.VMEM((COL_U, COL_S), BF16)] * 4 + [pltpu.VMEM((COL_U, COL_U), BF16)]
            + [pltpu.VMEM((COL_U, COL_S), F32)])


def _fill_block_diag(bd_ref, blocks_ref, stage):
    r, c = blocks_ref.shape[1:]
    stage[...] = jnp.zeros(stage.shape, F32)
    for gl in range(GROUPS_PER_COL):
        stage[pl.ds(gl * r, r), pl.ds(gl * c, c)] = blocks_ref[gl]
    bd_ref[...] = stage[:, :GROUPS_PER_COL * c].astype(BF16)


def _take_block_diag(out_ref, mat):
    r, c = out_ref.shape[1:]
    for gl in range(GROUPS_PER_COL):
        out_ref[gl] = mat[gl * r:(gl + 1) * r, gl * c:(gl + 1) * c]


def _s5_fill_states(u_ref, bre_ref, bim_ref, ar_ref, ai_ref, s_re, s_im, lseg, n_chunks, chunk):
    def fill(cidx, carry):
        rows = pl.ds(pl.multiple_of(cidx * chunk, SUBLANES), chunk)
        ub = u_ref[rows, :].astype(BF16)
        s_re[rows, :] = jnp.dot(ub, bre_ref[...], preferred_element_type=F32)
        s_im[rows, :] = jnp.dot(ub, bim_ref[...], preferred_element_type=F32)
        return carry
    lax.fori_loop(0, n_chunks, fill, 0)
    _segment_scan(s_re, s_im, ar_ref[...], ai_ref[...], lseg, reverse=False)


def _s5_forward(u_p, ar, ai, bbr, bbi, c_re, c_im, d_row, glu_w, glub_row):
    lp = u_p.shape[0]
    lseg = lp // SUBLANES
    chunk, n_chunks = lseg, SUBLANES

    def body(u_ref, ar_ref, ai_ref, bbr_ref, bbi_ref, cr_ref, ci_ref, d_ref, gw_ref, glub_ref,
             ys_ref, s_re, s_im, bre_ref, bim_ref, cre_ref, cim_ref, glu_ref, stage):
        for bd, blocks in ((bre_ref, bbr_ref), (bim_ref, bbi_ref), (cre_ref, cr_ref), (cim_ref, ci_ref),
                           (glu_ref, gw_ref)):
            _fill_block_diag(bd, blocks, stage)
        _s5_fill_states(u_ref, bre_ref, bim_ref, ar_ref, ai_ref, s_re, s_im, lseg, n_chunks, chunk)

        def emit(cidx, carry):
            rows = pl.ds(pl.multiple_of(cidx * chunk, SUBLANES), chunk)
            y = (_dot_nt(s_re[rows, :], cre_ref[...]) - _dot_nt(s_im[rows, :], cim_ref[...])
                 + d_ref[...] * u_ref[rows, :])
            g = _gelu(y)
            gate = _dot(g, glu_ref[...]) + glub_ref[...]
            ys_ref[rows, :] = g * _sigmoid(gate)
            return carry
        lax.fori_loop(0, n_chunks, emit, 0)

    col_u, row_u, row_s, bc_blk, glu_blk = _s5_specs(lp)
    return pl.pallas_call(
        body, name="s5_forward", grid=(N_COL,),
        in_specs=[col_u, row_s, row_s, bc_blk, bc_blk, bc_blk, bc_blk, row_u, glu_blk, row_u],
        out_specs=col_u, out_shape=jax.ShapeDtypeStruct((lp, D_SSM), F32),
        scratch_shapes=[pltpu.VMEM((lp, COL_S), F32), pltpu.VMEM((lp, COL_S), F32)] + _s5_block_diag_scratch(),
        compiler_params=_params(("arbitrary",), VMEM_LIMIT),
    )(u_p, ar, ai, bbr, bbi, c_re, c_im, d_row, glu_w, glub_row)


def _s5_backward(u_p, dys_p, ar, ai, bbr, bbi, c_re, c_im, d_row, glu_w, glub_row):
    lp = u_p.shape[0]
    lseg = lp // SUBLANES
    chunk, n_chunks = lseg, SUBLANES

    def body(u_ref, dys_ref, ar_ref, ai_ref, bbr_ref, bbi_ref, cr_ref, ci_ref, d_ref, gw_ref, glub_ref,
             du_ref, dar_ref, dai_ref, dbbr_ref, dbbi_ref, dcr_ref, dci_ref, dd_ref, dgw_ref, dglub_ref,
             s_re, s_im, q_re, q_im, bre_ref, bim_ref, cre_ref, cim_ref, glu_ref, stage,
             dbre_ref, dbim_ref, dcre_ref, dcim_ref, dglu_ref):
        for bd, blocks in ((bre_ref, bbr_ref), (bim_ref, bbi_ref), (cre_ref, cr_ref), (cim_ref, ci_ref),
                           (glu_ref, gw_ref)):
            _fill_block_diag(bd, blocks, stage)
        _s5_fill_states(u_ref, bre_ref, bim_ref, ar_ref, ai_ref, s_re, s_im, lseg, n_chunks, chunk)
        for ref in (dcre_ref, dcim_ref, dd_ref, dglu_ref, dglub_ref, dbre_ref, dbim_ref):
            ref[...] = jnp.zeros(ref.shape, F32)

        def mixer_bwd(cidx, carry):
            rows = pl.ds(pl.multiple_of(cidx * chunk, SUBLANES), chunk)
            u = u_ref[rows, :]
            sr, si = s_re[rows, :], s_im[rows, :]
            y = _dot_nt(sr, cre_ref[...]) - _dot_nt(si, cim_ref[...]) + d_ref[...] * u
            g = _gelu(y)
            sg = _sigmoid(_dot(g, glu_ref[...]) + glub_ref[...])
            dout = dys_ref[rows, :]
            dgate = dout * g * sg * (1.0 - sg)
            dy = (dout * sg + _dot_nt(dgate, glu_ref[...])) * _gelu_grad(y)
            dglu_ref[...] += _dot_tn(g, dgate)
            dglub_ref[...] += jnp.sum(dgate, axis=0, keepdims=True)
            dd_ref[...] += jnp.sum(dy * u, axis=0, keepdims=True)
            dcre_ref[...] += _dot_tn(dy, sr)
            dcim_ref[...] -= _dot_tn(dy, si)
            q_re[rows, :] = _dot(dy, cre_ref[...])
            q_im[rows, :] = -_dot(dy, cim_ref[...])
            du_ref[rows, :] = d_ref[...] * dy
            return carry
        lax.fori_loop(0, n_chunks, mixer_bwd, 0)

        row = lax.broadcasted_iota(jnp.int32, (SUBLANES, COL_S), 0)

        def visit(i, qr, qi, acc, is_tile0):
            if is_tile0:
                prev = _tile_rows(lseg - 1)
                pr = jnp.where(row == 0, 0.0, pltpu.roll(s_re[prev, :], 1, 0))
                pi = jnp.where(row == 0, 0.0, pltpu.roll(s_im[prev, :], 1, 0))
            else:
                prev = _tile_rows(i - 1)
                pr, pi = s_re[prev, :], s_im[prev, :]
            return acc[0] + qr * pr + qi * pi, acc[1] + qi * pr - qr * pi

        dar, dai = _segment_scan(q_re, q_im, ar_ref[...], -ai_ref[...], lseg, reverse=True, visit=visit)
        dar_ref[...] = jnp.sum(dar, axis=0, keepdims=True)
        dai_ref[...] = jnp.sum(dai, axis=0, keepdims=True)

        def input_bwd(cidx, carry):
            rows = pl.ds(pl.multiple_of(cidx * chunk, SUBLANES), chunk)
            qr, qi = q_re[rows, :], q_im[rows, :]
            u = u_ref[rows, :]
            du_ref[rows, :] += _dot_nt(qr, bre_ref[...]) + _dot_nt(qi, bim_ref[...])
            dbre_ref[...] += _dot_tn(u, qr)
            dbim_ref[...] += _dot_tn(u, qi)
            return carry
        lax.fori_loop(0, n_chunks, input_bwd, 0)
        for out, acc in ((dbbr_ref, dbre_ref), (dbbi_ref, dbim_ref), (dcr_ref, dcre_ref), (dci_ref, dcim_ref),
                         (dgw_ref, dglu_ref)):
            _take_block_diag(out, acc[...])

    col_u, row_u, row_s, bc_blk, glu_blk = _s5_specs(lp)
    group_mats = jax.ShapeDtypeStruct((SSM_GROUPS, SSM_GROUP, SSM_STATE), F32)
    return pl.pallas_call(
        body, name="s5_backward", grid=(N_COL,),
        in_specs=[col_u, col_u, row_s, row_s, bc_blk, bc_blk, bc_blk, bc_blk, row_u, glu_blk, row_u],
        out_specs=[col_u, row_s, row_s, bc_blk, bc_blk, bc_blk, bc_blk, row_u, glu_blk, row_u],
        out_shape=[jax.ShapeDtypeStruct((lp, D_SSM), F32),
                   jax.ShapeDtypeStruct((1, N_COL * COL_S), F32), jax.ShapeDtypeStruct((1, N_COL * COL_S), F32),
                   group_mats, group_mats, group_mats, group_mats, jax.ShapeDtypeStruct((1, D_SSM), F32),
                   jax.ShapeDtypeStruct((SSM_GROUPS, SSM_GROUP, SSM_GROUP), F32), jax.ShapeDtypeStruct((1, D_SSM), F32)],
        scratch_shapes=([pltpu.VMEM((lp, COL_S), F32)] * 4 + _s5_block_diag_scratch()
                        + [pltpu.VMEM((COL_U, COL_S), F32)] * 4 + [pltpu.VMEM((COL_U, COL_U), F32)]),
        compiler_params=_params(("arbitrary",), VMEM_LIMIT),
    )(u_p, dys_p, ar, ai, bbr, bbi, c_re, c_im, d_row, glu_w, glub_row)


def _window_sum(ext, group, leading):
    n = ext.shape[0]
    s = ext
    for j in range(POOL_GROUPS):
        shift = n - (1 << j) if leading else 1 << j
        s = jnp.where(j <= group, s + pltpu.roll(s, shift, 0), s)
    return s


def _pool_inv_count(tile, window, first_row):
    t = tile * TM + lax.broadcasted_iota(jnp.int32, (TM, 1), 0) - first_row
    return 1.0 / jnp.clip(t + 1, 1, window).astype(F32)


def _pool_specs(lp):
    col = pl.BlockSpec((lp, POOL_DIM), lambda k: (0, k))
    mat = pl.BlockSpec((None, POOL_DIM, POOL_DIM), lambda k: (k, 0, 0))
    row = pl.BlockSpec((None, 1, POOL_DIM), lambda k: (k, 0, 0))
    return col, mat, row


def _pool_forward(v, pool_w, pool_scale, first_row):
    lp = v.shape[0]
    n_tiles = lp // TM

    def body(v_ref, w_ref, sc_ref, yp_ref, vpad):
        group = pl.program_id(0)
        window = jnp.left_shift(2, group)
        vpad[pl.ds(0, POOL_HALO), :] = jnp.zeros((POOL_HALO, POOL_DIM), F32)
        vpad[pl.ds(POOL_HALO, lp), :] = v_ref[...]

        def tile(j, carry):
            start = pl.multiple_of(j * TM, TM)
            ext = vpad[pl.ds(start, TM + POOL_HALO), :]
            sums = _window_sum(ext, group, leading=False)[POOL_HALO:, :]
            p = sums * _pool_inv_count(j, window, first_row) - ext[POOL_HALO:, :]
            yp_ref[pl.ds(start, TM), :] = _dot(p, w_ref[...]) * sc_ref[...]
            return carry
        lax.fori_loop(0, n_tiles, tile, 0)

    col, mat, row = _pool_specs(lp)
    return pl.pallas_call(
        body, name="pool_forward", grid=(POOL_GROUPS,),
        in_specs=[col, mat, row], out_specs=col, out_shape=jax.ShapeDtypeStruct((lp, D_SSM), F32),
        scratch_shapes=[pltpu.VMEM((lp + POOL_HALO, POOL_DIM), F32)],
        compiler_params=_params(("arbitrary",), VMEM_LIMIT),
    )(v, pool_w, pool_scale)


def _pool_backward(v, dyp, pool_w, pool_scale, first_row):
    lp = v.shape[0]
    n_tiles = lp // TM

    def body(v_ref, dyp_ref, w_ref, sc_ref, dv_ref, dw_ref, dsc_ref, vpad, gpad):
        group = pl.program_id(0)
        window = jnp.left_shift(2, group)
        vpad[pl.ds(0, POOL_HALO), :] = jnp.zeros((POOL_HALO, POOL_DIM), F32)
        vpad[pl.ds(POOL_HALO, lp), :] = v_ref[...]
        gpad[pl.ds(lp, POOL_HALO), :] = jnp.zeros((POOL_HALO, POOL_DIM), F32)
        dw_ref[...] = jnp.zeros(dw_ref.shape, F32)
        dsc_ref[...] = jnp.zeros(dsc_ref.shape, F32)

        def linear_bwd(j, carry):
            start = pl.multiple_of(j * TM, TM)
            ext = vpad[pl.ds(start, TM + POOL_HALO), :]
            inv = _pool_inv_count(j, window, first_row)
            p = _window_sum(ext, group, leading=False)[POOL_HALO:, :] * inv - ext[POOL_HALO:, :]
            z = _dot(p, w_ref[...])
            dyp_t = dyp_ref[pl.ds(start, TM), :]
            dz = dyp_t * sc_ref[...]
            dsc_ref[...] += jnp.sum(dyp_t * z, axis=0, keepdims=True)
            dw_ref[...] += _dot_tn(p, dz)
            dp = _dot_nt(dz, w_ref[...])
            gpad[pl.ds(start, TM), :] = dp * inv
            dv_ref[pl.ds(start, TM), :] = -dp
            return carry
        lax.fori_loop(0, n_tiles, linear_bwd, 0)

        def window_bwd(j, carry):
            start = pl.multiple_of(j * TM, TM)
            ext = gpad[pl.ds(start, TM + POOL_HALO), :]
            dv_ref[pl.ds(start, TM), :] += _window_sum(ext, group, leading=True)[:TM, :]
            return carry
        lax.fori_loop(0, n_tiles, window_bwd, 0)

    col, mat, row = _pool_specs(lp)
    return pl.pallas_call(
        body, name="pool_backward", grid=(POOL_GROUPS,),
        in_specs=[col, col, mat, row], out_specs=[col, mat, row],
        out_shape=[jax.ShapeDtypeStruct((lp, D_SSM), F32),
                   jax.ShapeDtypeStruct((POOL_GROUPS, POOL_DIM, POOL_DIM), F32),
                   jax.ShapeDtypeStruct((POOL_GROUPS, 1, POOL_DIM), F32)],
        scratch_shapes=[pltpu.VMEM((lp + POOL_HALO, POOL_DIM), F32)] * 2,
        compiler_params=_params(("arbitrary",), VMEM_LIMIT),
    )(v, dyp, pool_w, pool_scale)


def _wgrad_stacks():
    return [pltpu.VMEM((WGRAD_GROUP * TM, D_MODEL), BF16)] * 2


def _stack_rows(stack_ref, cols, value, i):
    slot = i % WGRAD_GROUP

    @pl.when((slot == 0) & (i + WGRAD_GROUP > pl.num_programs(0)))
    def _():
        stack_ref[:, cols] = jnp.zeros((stack_ref.shape[0], cols.size), stack_ref.dtype)

    stack_ref[pl.ds(pl.multiple_of(slot * TM, TM), TM), cols] = value


def _group_done(i):
    return (i % WGRAD_GROUP == WGRAD_GROUP - 1) | (i == pl.num_programs(0) - 1)


def _row_specs():
    head = _const((HEAD, D_MODEL))
    xrow = pl.BlockSpec((TM, D_MODEL), _xrow)
    full = pl.BlockSpec((TM, D_MODEL), lambda i: (i, 0))
    half = pl.BlockSpec((TM, D_SSM), lambda i: (i, 0))
    return head, xrow, full, half


def _in_proj(head, x, g1, w_in):
    n_tiles = (HEAD + x.shape[0]) // TM
    lp = n_tiles * TM

    def body(head_ref, x_ref, g_ref, w_ref, u_ref, v_ref):
        h0 = jnp.where(pl.program_id(0) == 0, head_ref[...], x_ref[...])
        proj = _dot(h0 * _rstd(h0) * g_ref[...], w_ref[...])
        u_ref[...] = proj[:, :D_SSM]
        v_ref[...] = proj[:, D_SSM:]

    head_s, xrow, _, half = _row_specs()
    return pl.pallas_call(
        body, name="in_proj", grid=(n_tiles,),
        in_specs=[head_s, xrow, _const((1, D_MODEL)), _const((D_MODEL, D_MODEL))],
        out_specs=[half, half], out_shape=[jax.ShapeDtypeStruct((lp, D_SSM), F32)] * 2,
        compiler_params=_params(("parallel",), VMEM_LIMIT),
    )(head, x, g1, w_in)


def _out_proj(head, x, ys, yp, gs, gp, w_out):
    lp = ys.shape[0]

    def body(head_ref, x_ref, ys_ref, yp_ref, gs_ref, gp_ref, w_ref, h1_ref):
        h0 = jnp.where(pl.program_id(0) == 0, head_ref[...], x_ref[...])
        ys_t, yp_t = ys_ref[...], yp_ref[...]
        ms = ys_t * _rstd(ys_t) * gs_ref[...]
        mp = yp_t * _rstd(yp_t) * gp_ref[...]
        h1_ref[...] = h0 + _dot(ms, w_ref[pl.ds(0, D_SSM), :]) + _dot(mp, w_ref[pl.ds(D_SSM, D_SSM), :])

    head_s, xrow, full, half = _row_specs()
    return pl.pallas_call(
        body, name="out_proj", grid=(lp // TM,),
        in_specs=[head_s, xrow, half, half, _const((1, D_SSM)), _const((1, D_SSM)), _const((D_MODEL, D_MODEL))],
        out_specs=full, out_shape=jax.ShapeDtypeStruct((lp, D_MODEL), F32),
        compiler_params=_params(("parallel",), VMEM_LIMIT),
    )(head, x, ys, yp, gs, gp, w_out)


def _load_weights(hbm_refs, vmem_refs, sems):
    @pl.when(pl.program_id(0) == 0)
    def _():
        copies = [pltpu.make_async_copy(h, v, sems.at[n]) for n, (h, v) in enumerate(zip(hbm_refs, vmem_refs))]
        for cp in copies:
            cp.start()
        for cp in copies:
            cp.wait()


def _ffn_scratch():
    return [pltpu.VMEM((D_FF, D_MODEL), BF16)] * 3 + [pltpu.SemaphoreType.DMA((3,))]


def _ff_tile(t):
    return pl.ds(t * FF_TILE, FF_TILE)


def _ffn_forward(h1, g2, wg_t, wu_t, wd):
    lp = h1.shape[0]

    def body(h1_ref, g_ref, wg_hbm, wu_hbm, wd_hbm, ab_ref, n2_ref, h2_ref, wg, wu, wdn, sems):
        _load_weights((wg_hbm, wu_hbm, wd_hbm), (wg, wu, wdn), sems)
        h1_t = h1_ref[...]
        n2 = (h1_t * _rstd(h1_t) * g_ref[...]).astype(BF16)
        n2_ref[...] = n2
        acc = h1_t
        for t in range(D_FF // FF_TILE):
            a = _dot_nt(n2, wg[_ff_tile(t), :])
            b = _dot_nt(n2, wu[_ff_tile(t), :])
            ab_ref[:, _ff_tile(t)] = a.astype(BF16)
            ab_ref[:, pl.ds(D_FF + t * FF_TILE, FF_TILE)] = b.astype(BF16)
            acc = acc + _dot(a * _sigmoid(a) * b, wdn[_ff_tile(t), :])
        h2_ref[...] = acc

    _, _, full, _ = _row_specs()
    wide = pl.BlockSpec((TM, 2 * D_FF), lambda i: (i, 0))
    half_width = pl.BlockSpec((TM, D_MODEL), lambda i: (i, 0))
    return pl.pallas_call(
        body, name="ffn_forward", grid=(lp // TM,),
        in_specs=[full, _const((1, D_MODEL)), ANY, ANY, ANY], out_specs=[wide, half_width, full],
        out_shape=[jax.ShapeDtypeStruct((lp, 2 * D_FF), BF16), jax.ShapeDtypeStruct((lp, D_MODEL), BF16),
                   jax.ShapeDtypeStruct((lp, D_MODEL), F32)],
        scratch_shapes=_ffn_scratch(), compiler_params=_params(("arbitrary",), VMEM_LIMIT),
    )(h1, g2, wg_t, wu_t, wd)


def _ffn_backward(h2, target, h1, ab, gf, g2, wg_t, wu_t, wd):
    lp = h1.shape[0]

    def body(h2_ref, t_ref, h1_ref, ab_ref, gf_ref, g2_ref, wg_hbm, wu_hbm, wd_hbm,
             dh1_ref, xt_ref, dh2_ref, loss_ref, dgf_ref, dg2_ref, wg, wu, wdn, sems):
        i = pl.program_id(0)
        _load_weights((wg_hbm, wu_hbm, wd_hbm), (wg, wu, wdn), sems)

        @pl.when(i == 0)
        def _():
            loss_ref[...] = jnp.zeros(loss_ref.shape, F32)
            dgf_ref[...] = jnp.zeros(dgf_ref.shape, F32)
            dg2_ref[...] = jnp.zeros(dg2_ref.shape, F32)

        h2_t = h2_ref[...]
        rf = _rstd(h2_t)
        xf = h2_t * rf
        diff = jnp.where(i == 0, 0.0, xf * gf_ref[...] - t_ref[...])
        loss_ref[...] += 0.5 * jnp.sum(diff * diff) / D_MODEL
        dh2, dgf = _rms_bwd(diff / D_MODEL, xf, rf, gf_ref[...])
        dgf_ref[...] += dgf
        dh2_b = dh2.astype(BF16)
        dh2_ref[...] = dh2_b

        dn2 = jnp.zeros((TM, D_MODEL), F32)
        for t in range(D_FF // FF_TILE):
            dff = _dot_nt(dh2_b, wdn[_ff_tile(t), :])
            a = ab_ref[:, _ff_tile(t)].astype(F32)
            b = ab_ref[:, pl.ds(D_FF + t * FF_TILE, FF_TILE)].astype(F32)
            sg = _sigmoid(a)
            silu = a * sg
            da = dff * b * sg * (1.0 + a * (1.0 - sg))
            db = dff * silu
            for part, val in enumerate((da, db, silu * b)):
                xt_ref[pl.ds(part * D_FF + t * FF_TILE, FF_TILE), :] = val.T.astype(BF16)
            dn2 = dn2 + _dot(da, wg[_ff_tile(t), :]) + _dot(db, wu[_ff_tile(t), :])

        h1_t = h1_ref[...]
        r2 = _rstd(h1_t)
        dx, dg2 = _rms_bwd(dn2, h1_t * r2, r2, g2_ref[...])
        dg2_ref[...] += dg2
        dh1_ref[...] = dh2 + dx

    _, xrow, full, _ = _row_specs()
    wide = pl.BlockSpec((TM, 2 * D_FF), lambda i: (i, 0))
    half_width = pl.BlockSpec((TM, D_MODEL), lambda i: (i, 0))
    vec = _const((1, D_MODEL))
    return pl.pallas_call(
        body, name="ffn_backward", grid=(lp // TM,),
        in_specs=[full, xrow, full, wide, vec, vec, ANY, ANY, ANY],
        out_specs=[full, pl.BlockSpec((3 * D_FF, TM), lambda i: (0, i)), half_width, _const((1, PACK_LANES)), vec, vec],
        out_shape=[jax.ShapeDtypeStruct((lp, D_MODEL), F32),
                   jax.ShapeDtypeStruct((3 * D_FF, lp), BF16),
                   jax.ShapeDtypeStruct((lp, D_MODEL), BF16),
                   jax.ShapeDtypeStruct((1, PACK_LANES), F32),
                   jax.ShapeDtypeStruct((1, D_MODEL), F32), jax.ShapeDtypeStruct((1, D_MODEL), F32)],
        scratch_shapes=_ffn_scratch(), compiler_params=_params(("arbitrary",), VMEM_LIMIT),
    )(h2, target, h1, ab, gf, g2, wg_t, wu_t, wd)


def _ffn_wgrad(xt, n2, dh2):
    lp = n2.shape[0]
    rows = lp // WGRAD_STEPS
    n_tiles = 3 * D_FF // FF_TILE
    shards_per_tile = FF_TILE // FF_SHARD
    gate_up_tiles = 2 * D_FF // FF_TILE

    def body(xt_ref, n2_ref, dh2_ref, out_ref, acc):
        q, k = pl.program_id(0), pl.program_id(1)

        @pl.when(k == 0)
        def _():
            acc[...] = jnp.zeros(acc.shape, F32)

        @pl.when(q < gate_up_tiles)
        def _():
            acc[...] += jnp.dot(xt_ref[...], n2_ref[...], preferred_element_type=F32)

        @pl.when(q >= gate_up_tiles)
        def _():
            acc[...] += jnp.dot(xt_ref[...], dh2_ref[...], preferred_element_type=F32)

        @pl.when(k == pl.num_programs(1) - 1)
        def _():
            for s in range(shards_per_tile):
                out_ref[s] = acc[pl.ds(s * FF_SHARD, FF_SHARD), :].astype(BF16)

    tiles_per_matrix = D_FF // FF_TILE
    return pl.pallas_call(
        body, name="ffn_wgrad", grid=(n_tiles, WGRAD_STEPS),
        in_specs=[pl.BlockSpec((FF_TILE, rows), lambda q, k: (q, k)),
                  pl.BlockSpec((rows, D_MODEL), lambda q, k: (jnp.where(q < gate_up_tiles, k, 0), 0)),
                  pl.BlockSpec((rows, D_MODEL), lambda q, k: (jnp.where(q < gate_up_tiles, 0, k), 0))],
        out_specs=pl.BlockSpec((shards_per_tile, None, FF_SHARD, D_MODEL),
                               lambda q, k: (q % tiles_per_matrix, q // tiles_per_matrix, 0, 0)),
        out_shape=jax.ShapeDtypeStruct((N_DEV, 3, FF_SHARD, D_MODEL), BF16),
        scratch_shapes=[pltpu.VMEM((FF_TILE, D_MODEL), F32)],
        compiler_params=_params(("parallel", "arbitrary"), VMEM_LIMIT),
    )(xt, n2, dh2)


def _out_proj_backward(dh1, ys, yp, gs, gp, w_out):
    lp = ys.shape[0]

    def body(dh1_ref, ys_ref, yp_ref, gs_ref, gp_ref, w_ref, dys_ref, dyp_ref, dgs_ref, dgp_ref, dw_out, dw_ref,
             mix_stack, dh1_stack):
        i = pl.program_id(0)

        @pl.when(i == 0)
        def _():
            dgs_ref[...] = jnp.zeros(dgs_ref.shape, F32)
            dgp_ref[...] = jnp.zeros(dgp_ref.shape, F32)
            dw_ref[...] = jnp.zeros(dw_ref.shape, F32)

        dh1_b = dh1_ref[...].astype(BF16)
        _stack_rows(dh1_stack, pl.ds(0, D_MODEL), dh1_b, i)
        dmix = _dot_nt(dh1_b, w_ref[...])
        for y_ref, g_ref, dy_ref, dg_ref, lo in ((ys_ref, gs_ref, dys_ref, dgs_ref, 0),
                                                 (yp_ref, gp_ref, dyp_ref, dgp_ref, D_SSM)):
            y_t = y_ref[...]
            r = _rstd(y_t)
            xhat = y_t * r
            dy, dg = _rms_bwd(dmix[:, lo:lo + D_SSM], xhat, r, g_ref[...])
            dy_ref[...] = dy
            dg_ref[...] += dg
            _stack_rows(mix_stack, pl.ds(lo, D_SSM), (xhat * g_ref[...]).astype(BF16), i)

        @pl.when(_group_done(i))
        def _():
            dw_ref[...] += _dot_tn(mix_stack[...], dh1_stack[...])

        @pl.when(i == pl.num_programs(0) - 1)
        def _():
            dw_out[...] = dw_ref[...].astype(BF16)

    _, _, full, half = _row_specs()
    vec = _const((1, D_SSM))
    return pl.pallas_call(
        body, name="out_proj_backward", grid=(lp // TM,),
        in_specs=[full, half, half, vec, vec, _const((D_MODEL, D_MODEL))],
        out_specs=[half, half, vec, vec, _const((D_MODEL, D_MODEL))],
        out_shape=[jax.ShapeDtypeStruct((lp, D_SSM), F32)] * 2 + [jax.ShapeDtypeStruct((1, D_SSM), F32)] * 2
        + [jax.ShapeDtypeStruct((D_MODEL, D_MODEL), BF16)],
        scratch_shapes=[pltpu.VMEM((D_MODEL, D_MODEL), F32)] + _wgrad_stacks(),
        compiler_params=_params(("arbitrary",), VMEM_LIMIT),
    )(dh1, ys, yp, gs, gp, w_out)


def _in_proj_backward(head, x, du, dv, dh1, g1, w_in):
    lp = du.shape[0]

    def body(head_ref, x_ref, du_ref, dv_ref, dh1_ref, g_ref, w_ref, dx_ref, dhead_ref, dg_ref, dw_out, dw_ref,
             n1_stack, dp_stack):
        i = pl.program_id(0)

        @pl.when(i == 0)
        def _():
            dg_ref[...] = jnp.zeros(dg_ref.shape, F32)
            dw_ref[...] = jnp.zeros(dw_ref.shape, F32)

        h0 = jnp.where(i == 0, head_ref[...], x_ref[...])
        r = _rstd(h0)
        xhat = h0 * r
        n1 = (xhat * g_ref[...]).astype(BF16)
        du_b, dv_b = du_ref[...].astype(BF16), dv_ref[...].astype(BF16)
        dn1 = _dot_nt(du_b, w_ref[:, pl.ds(0, D_SSM)]) + _dot_nt(dv_b, w_ref[:, pl.ds(D_SSM, D_SSM)])
        dx, dg = _rms_bwd(dn1, xhat, r, g_ref[...])
        dg_ref[...] += dg
        dh0 = dh1_ref[...] + dx
        dx_ref[...] = dh0

        @pl.when(i == 0)
        def _():
            dhead_ref[...] = dh0

        _stack_rows(n1_stack, pl.ds(0, D_MODEL), n1, i)
        _stack_rows(dp_stack, pl.ds(0, D_SSM), du_b, i)
        _stack_rows(dp_stack, pl.ds(D_SSM, D_SSM), dv_b, i)

        @pl.when(_group_done(i))
        def _():
            dw_ref[...] += _dot_tn(n1_stack[...], dp_stack[...])

        @pl.when(i == pl.num_programs(0) - 1)
        def _():
            dw_out[...] = dw_ref[...].astype(BF16)

    head_s, xrow, full, half = _row_specs()
    vec = _const((1, D_MODEL))
    mat = _const((D_MODEL, D_MODEL))
    return pl.pallas_call(
        body, name="in_proj_backward", grid=(lp // TM,),
        in_specs=[head_s, xrow, half, half, full, vec, mat],
        out_specs=[xrow, head_s, vec, mat],
        out_shape=[jax.ShapeDtypeStruct(x.shape, F32), jax.ShapeDtypeStruct((HEAD, D_MODEL), F32),
                   jax.ShapeDtypeStruct((1, D_MODEL), F32), jax.ShapeDtypeStruct((D_MODEL, D_MODEL), BF16)],
        scratch_shapes=[pltpu.VMEM((D_MODEL, D_MODEL), F32)] + _wgrad_stacks(),
        compiler_params=_params(("arbitrary",), VMEM_LIMIT),
    )(head, x, du, dv, dh1, g1, w_in)


def _permute_rows(a):
    lp, n = a.shape
    return a.reshape(SUBLANES, lp // SUBLANES, n).transpose(1, 0, 2).reshape(lp, n)


def _unpermute_rows(a):
    lp, n = a.shape
    return a.reshape(lp // SUBLANES, SUBLANES, n).transpose(1, 0, 2).reshape(lp, n)


def _pack(parts, dtype):
    rows = []
    for p in parts:
        flat = p.reshape(-1).astype(dtype)
        pad = (-flat.shape[0]) % PACK_UNIT
        rows.append(jnp.pad(flat, (0, pad)).reshape(-1, PACK_LANES))
    n_rows = sum(r.shape[0] for r in rows)
    if n_rows % 16:
        rows.append(jnp.zeros((8, PACK_LANES), dtype))
    return jnp.concatenate(rows, axis=0)


def _as2d(a):
    return a.reshape(-1, a.shape[-1])


def _unpack(packed, shapes):
    out, row = [], 0
    for shape in shapes:
        size = 1
        for s in shape:
            size *= s
        n_rows = -(-size // PACK_UNIT) * 8
        out.append(packed[row:row + n_rows].reshape(-1)[:size].reshape(shape))
        row += n_rows
    return out


def _to_view(name, a):
    if name in ("ssm_b_re", "ssm_b_im"):
        return a[0].transpose(0, 2, 1).reshape(-1, SSM_STATE)
    if name in ("ssm_d", "ssm_glu_b"):
        return a[0].T
    if name == "ssm_glu_w":
        return a[0].transpose(1, 2, 0).reshape(-1, SSM_GROUPS)
    return _as2d(a)


def _from_view(name, r, shape):
    if name in ("ssm_b_re", "ssm_b_im"):
        return r.reshape(SSM_GROUPS, SSM_GROUP, SSM_STATE).transpose(0, 2, 1).reshape(shape)
    if name in ("ssm_d", "ssm_glu_b"):
        return r.T.reshape(shape)
    if name == "ssm_glu_w":
        return r.reshape(SSM_GROUP, SSM_GROUP, SSM_GROUPS).transpose(2, 0, 1).reshape(shape)
    return r.reshape(shape)


def kernel(x, meta_tokens, norm1_g, w_in, ssm_lambda_re, ssm_lambda_im, ssm_log_step, ssm_b_re, ssm_b_im, ssm_c_re, ssm_c_im, ssm_d, ssm_glu_w, ssm_glu_b, ssm_norm_g, pool_w, pool_scale, pool_norm_g, w_out, norm2_g, w_gate, w_up, w_down, final_norm_g, loss_target, m_meta_tokens, m_norm1_g, m_w_in, m_ssm_lambda_re, m_ssm_lambda_im, m_ssm_log_step, m_ssm_b_re, m_ssm_b_im, m_ssm_c_re, m_ssm_c_im, m_ssm_d, m_ssm_glu_w, m_ssm_glu_b, m_ssm_norm_g, m_pool_w, m_pool_scale, m_pool_norm_g, m_w_out, m_norm2_g, m_w_gate, m_w_up, m_w_down, m_final_norm_g, v_meta_tokens, v_norm1_g, v_w_in, v_ssm_lambda_re, v_ssm_lambda_im, v_ssm_log_step, v_ssm_b_re, v_ssm_b_im, v_ssm_c_re, v_ssm_c_im, v_ssm_d, v_ssm_glu_w, v_ssm_glu_b, v_ssm_norm_g, v_pool_w, v_pool_scale, v_pool_norm_g, v_w_out, v_norm2_g, v_w_gate, v_w_up, v_w_down, v_final_norm_g):
    given = dict(locals())
    weights = {n: given[n] for n in WEIGHT_NAMES}
    n_meta = meta_tokens.shape[0]
    me = 4 * lax.axis_index("x") + 2 * lax.axis_index("y") + lax.axis_index("c")

    shard_rows = w_in.shape[1]
    first = [w_in[0].astype(BF16), meta_tokens]
    first_make = _push_copies(ALL_PEERS, [False, False])
    first_x, first_token = _split_start(first + [_landing(s, False) for s in first], first_make,
                                        2 * len(ALL_PEERS), "gather_w_in_start")

    xs = x[0]
    tgt = loss_target[0]
    first_row = HEAD - n_meta
    g1, g2, gf = norm1_g, norm2_g, final_norm_g.reshape(1, D_MODEL)
    gs, gp = ssm_norm_g, pool_norm_g

    lam_re, lam_im = ssm_lambda_re[0] + first_token[:1, :1], ssm_lambda_im[0]
    log_step = ssm_log_step[0].reshape(SSM_GROUPS, 1)
    b_re = ssm_b_re[0].transpose(0, 2, 1)
    b_im = ssm_b_im[0].transpose(0, 2, 1)
    abr, abi, zr, zi = _s5_disc_a(lam_re, lam_im, log_step)
    zr_col, zi_col = zr.reshape(SSM_GROUPS, 1, SSM_STATE), zi.reshape(SSM_GROUPS, 1, SSM_STATE)
    bbr, bbi = _s5_disc_b(zr_col, zi_col, b_re, b_im)
    s5_consts = (abr.reshape(1, -1), abi.reshape(1, -1), bbr, bbi, ssm_c_re[0], ssm_c_im[0],
                 ssm_d[0].reshape(1, D_SSM), ssm_glu_w[0], ssm_glu_b[0].reshape(1, D_SSM))
    pool_sc = pool_scale[0].reshape(POOL_GROUPS, 1, POOL_DIM)

    (_, _, w_in_all, meta_all), first_done = _split_wait(first_x, first_make, bbr, "gather_w_in_wait")
    w_in_all = w_in_all.reshape(D_MODEL, D_MODEL)
    meta_full = meta_all.transpose(1, 0, 2).reshape(n_meta, D_MODEL)
    head = jnp.concatenate([jnp.zeros((HEAD - n_meta, D_MODEL), F32), meta_full], axis=0)
    shards = [(w_out[0] + first_done[:1, :1]).astype(BF16), w_gate[0].T.astype(BF16), w_up[0].T.astype(BF16),
              w_down[0].astype(BF16)]
    n_big = len(shards)
    gather_make = _push_copies((SIBLING,) + CHIP_PEERS, [False] * n_big)
    gather, gather_token = _split_start(shards + [_landing(s, False) for s in shards], gather_make,
                                        n_big * (1 + len(CHIP_PEERS)), "gather_start")

    u, v = _in_proj(head, xs, g1 + gather_token[:1, :1], w_in_all)
    u_p = _permute_rows(u)
    ys_p = _s5_forward(u_p, *s5_consts)
    landed, _ = _split_wait(gather, gather_make, ys_p, "gather_wait")
    forward_make = _forward_copies(n_big)
    forward, forward_token = _split_start(list(landed[n_big:]), forward_make, n_big * len(CHIP_PEERS),
                                          "gather_forward_start")
    ys = _unpermute_rows(ys_p)
    yp = _pool_forward(v, pool_w[0], pool_sc + forward_token[:1, :1], first_row)
    (w_out_all, wg_t, wu_t, wd_all), _ = _split_wait(forward, forward_make, yp, "gather_forward_wait")
    w_out_all = w_out_all.reshape(D_MODEL, D_MODEL)
    ffn_weights = [w.reshape(D_FF, D_MODEL) for w in (wg_t, wu_t, wd_all)]
    h1 = _out_proj(head, xs, ys, yp, gs, gp, w_out_all)
    ab, n2, h2 = _ffn_forward(h1, g2, *ffn_weights)

    dh1, xt, dh2, loss_part, d_gf, d_g2 = _ffn_backward(h2, tgt, h1, ab, gf, g2, *ffn_weights)
    d_ffn = _ffn_wgrad(xt, n2, dh2)
    ffn_make = _push_copies(ALL_PEERS, [True])
    ffn_x, ffn_token = _split_start([d_ffn, _landing(d_ffn, True)], ffn_make, len(ALL_PEERS), "ffn_grad_start")
    dys, dyp, d_gs, d_gp, d_wout = _out_proj_backward(dh1, ys, yp, gs + ffn_token[:1, :1], gp, w_out_all)
    dv, d_pool_w, d_pool_sc = _pool_backward(v, dyp, pool_w[0], pool_sc, first_row)
    (du_p, d_ar, d_ai, d_bbr, d_bbi, d_c_re, d_c_im, d_d, d_glu, d_glub) = _s5_backward(
        u_p, _permute_rows(dys), *s5_consts)
    du = _unpermute_rows(du_p)

    d_zr, d_zi, d_b_re, d_b_im = _s5_disc_b_bwd(zr_col, zi_col, b_re, b_im, d_bbr, d_bbi)
    d_lam_re, d_lam_im, d_log_step = _s5_disc_a_bwd(
        lam_re, lam_im, log_step,
        (d_ar.reshape(SSM_GROUPS, SSM_STATE), d_ai.reshape(SSM_GROUPS, SSM_STATE),
         d_zr.reshape(SSM_GROUPS, SSM_STATE), d_zi.reshape(SSM_GROUPS, SSM_STATE)))
    groups_last = lambda row: row.reshape(SSM_GROUPS, SSM_GROUP).T
    small_grads = {
        "ssm_lambda_re": d_lam_re, "ssm_lambda_im": d_lam_im, "ssm_log_step": d_log_step.reshape(1, SSM_GROUPS),
        "ssm_b_re": d_b_re.reshape(-1, SSM_STATE), "ssm_b_im": d_b_im.reshape(-1, SSM_STATE),
        "ssm_c_re": d_c_re.reshape(-1, SSM_STATE), "ssm_c_im": d_c_im.reshape(-1, SSM_STATE),
        "ssm_d": groups_last(d_d), "ssm_glu_w": d_glu.transpose(1, 2, 0).reshape(-1, SSM_GROUPS),
        "ssm_glu_b": groups_last(d_glub),
        "ssm_norm_g": d_gs, "pool_w": d_pool_w.reshape(-1, POOL_DIM), "pool_scale": d_pool_sc.reshape(-1, POOL_DIM),
        "pool_norm_g": d_gp, "norm2_g": d_g2,
    }

    early_names = SMALL_NAMES[1:-1]
    early_pack = _pack([small_grads[n] for n in early_names], BF16)
    d_wout = d_wout.reshape(N_DEV, shard_rows, D_MODEL)
    early_make = _push_copies(ALL_PEERS, [True, False])
    early_x, early_token = _split_start([d_wout, early_pack, _landing(d_wout, True), _landing(early_pack, False)],
                                        early_make, 2 * len(ALL_PEERS), "early_grad_start")
    d_x, d_head, d_g1, d_win = _in_proj_backward(head, xs, du, dv, dh1, g1 + early_token[:1, :1], w_in_all)
    (_, r_ffn), _ = _split_wait(ffn_x, ffn_make, d_g1, "ffn_grad_wait")
    (_, _, r_wout, r_early), _ = _split_wait(early_x, early_make, d_g1, "early_grad_wait")
    d_win = d_win.reshape(N_DEV, shard_rows, D_MODEL)
    late_pack = _pack([d_g1, d_gf, d_head[first_row:], loss_part], F32)
    late_make = _push_copies(ALL_PEERS, [True, False])
    late_x, late_token = _split_start([d_win, late_pack, _landing(d_win, True), _landing(late_pack, False)],
                                      late_make, 2 * len(ALL_PEERS), "late_grad_start")

    results = {}
    res_gate = _adamw_part(r_ffn, 0, w_gate[0].T + late_token[:1, :1], m_w_gate[0].T, v_w_gate[0].T, "adamw_w_gate")
    res_up = _adamw_part(r_ffn, 1, w_up[0].T, m_w_up[0].T, v_w_up[0].T, "adamw_w_up")
    results["w_gate"] = [r.T for r in res_gate]
    results["w_up"] = [r.T for r in res_up]
    results["w_down"] = _adamw_part(r_ffn, 2, w_down[0], m_w_down[0], v_w_down[0], "adamw_w_down")
    results["w_out"] = _adamw(r_wout, w_out[0], m_w_out[0], v_w_out[0], shard_rows, "adamw_w_out")
    done = sum(res[1][:1, :1] for res in (res_gate, res_up, results["w_down"], results["w_out"]))
    (_, _, r_win, r_late), _ = _split_wait(late_x, late_make, done, "late_grad_wait")
    results["w_in"] = _adamw(r_win, w_in[0], m_w_in[0], v_w_in[0], shard_rows, "adamw_w_in")

    sum_early, sum_late = _reduce_slots([r_early, r_late], "small_grad_sums")
    views = lambda prefix: [_to_view(n, given[prefix + n]) for n in SMALL_NAMES]
    w_views = views("")
    g_views = _unpack(sum_early, [w.shape for w in w_views[1:-1]])
    g_norm1, g_final, g_meta_all, loss_row = _unpack(
        sum_late, [norm1_g.shape, (1, D_MODEL), (n_meta, D_MODEL), (1, PACK_LANES)])
    g_views = [g_norm1] + g_views + [g_final]
    res_small = _adamw_many(g_views, w_views, views("m_"), views("v_"), "adamw_small")
    for idx, n in enumerate(SMALL_NAMES):
        results[n] = [_from_view(n, part[idx], weights[n].shape) for part in (g_views,) + tuple(res_small)]
    shard_cols = meta_tokens.shape[1]
    g_meta = lax.dynamic_slice_in_dim(g_meta_all, me * shard_cols, shard_cols, axis=1)
    results["meta_tokens"] = _adamw(g_meta[None], meta_tokens, m_meta_tokens, v_meta_tokens, n_meta, "adamw_meta")

    out = [loss_row[0, 0], d_x[None]]
    for part in range(4):
        for n in WEIGHT_NAMES:
            out.append(results[n][part].reshape(weights[n].shape))
    return tuple(out)
```

```python
import jax
import jax.numpy as jnp
from jax import lax
from jax.experimental import pallas as pl
from jax.experimental.pallas import tpu as pltpu

F32 = jnp.float32
BF16 = jnp.bfloat16

N_DEV = 8
D_MODEL = 1024
D_SSM = 512
SSM_GROUP = 16
SSM_STATE = 64
SSM_GROUPS = 32
POOL_GROUPS = 4
POOL_DIM = 128
COL_U = 128
COL_S = 512
N_COL = D_SSM // COL_U
GROUPS_PER_COL = COL_U // SSM_GROUP
D_FF = 2816
FF_SHARD = D_FF // N_DEV
FF_TILE = D_FF // 2
TM = 256
WGRAD_STEPS = 2
HEAD = TM
SUBLANES = 8
SCAN_UNROLL = 4
POOL_HALO = 16
EPS = 1e-6
STEP_FLOOR = -1e-4
VMEM_LIMIT = 60 * 1024 * 1024

ADAM_LR = 0.001
ADAM_B1 = 0.9
ADAM_B2 = 0.999
ADAM_EPS = 1e-08
ADAM_WD = 0.01
ADAM_STEP = 10

MESH_ID = pl.DeviceIdType.MESH
ANY = pl.BlockSpec(memory_space=pl.ANY)

SMALL_NAMES = ("norm1_g", "ssm_lambda_re", "ssm_lambda_im", "ssm_log_step", "ssm_b_re", "ssm_b_im",
               "ssm_c_re", "ssm_c_im", "ssm_d", "ssm_glu_w", "ssm_glu_b", "ssm_norm_g", "pool_w",
               "pool_scale", "pool_norm_g", "norm2_g", "final_norm_g")
WEIGHT_NAMES = ("meta_tokens", "norm1_g", "w_in", "ssm_lambda_re", "ssm_lambda_im", "ssm_log_step",
                "ssm_b_re", "ssm_b_im", "ssm_c_re", "ssm_c_im", "ssm_d", "ssm_glu_w", "ssm_glu_b",
                "ssm_norm_g", "pool_w", "pool_scale", "pool_norm_g", "w_out", "norm2_g", "w_gate",
                "w_up", "w_down", "final_norm_g")
LANES = 128
PACK_LANES = LANES
PACK_UNIT = 8 * PACK_LANES


def _dot(a, b):
    return jnp.dot(a.astype(BF16), b.astype(BF16), preferred_element_type=F32)


def _dot_nt(a, b):
    return lax.dot_general(a.astype(BF16), b.astype(BF16), (((1,), (1,)), ((), ())), preferred_element_type=F32)


def _dot_tn(a, b):
    return lax.dot_general(a.astype(BF16), b.astype(BF16), (((0,), (0,)), ((), ())), preferred_element_type=F32)


def _sigmoid(x):
    return 1.0 / (1.0 + jnp.exp(-x))


def _rstd(x):
    return lax.rsqrt(jnp.mean(x * x, axis=-1, keepdims=True) + EPS)


def _rms_bwd(dy, xhat, r, g):
    dxh = dy * g
    dx = r * (dxh - xhat * jnp.mean(dxh * xhat, axis=-1, keepdims=True))
    return dx, jnp.sum(dy * xhat, axis=0, keepdims=True)


def _params(sem, vmem=None):
    return pltpu.CompilerParams(dimension_semantics=sem, vmem_limit_bytes=vmem)


def _const(shape):
    return pl.BlockSpec(shape, lambda *_: (0,) * len(shape))


def _xrow(i):
    return (jnp.maximum(i - 1, 0), 0)


HBM = pl.BlockSpec(memory_space=pltpu.HBM)
SEM = pl.BlockSpec(memory_space=pltpu.SEMAPHORE)
EFFECT = pltpu.SideEffectType.DATAFLOW_SIDE_EFFECTING
ALL_PEERS = tuple(range(1, N_DEV))
SIBLING = 1
CHIP_PEERS = (2, 4, 6)


def _me():
    return 4 * lax.axis_index("x") + 2 * lax.axis_index("y") + lax.axis_index("c")


def _peer(k):
    x, y, c = lax.axis_index("x"), lax.axis_index("y"), lax.axis_index("c")
    px = 1 - x if k & 4 else x
    py = 1 - y if k & 2 else y
    pc = 1 - c if k & 1 else c
    return (px, py, pc), 4 * px + 2 * py + pc


def _landing(arr, scatter):
    if scatter:
        own = lax.dynamic_index_in_dim(arr, _me(), 0, keepdims=False)
    else:
        own = arr
    return lax.dynamic_update_index_in_dim(lax.empty((N_DEV,) + own.shape, arr.dtype), own, _me(), 0)


def _push_copies(peers, scatter):
    n_arr = len(scatter)

    def make(refs, send_sems, recv_sems):
        copies = []
        for a in range(n_arr):
            for i, k in enumerate(peers):
                peer_id, peer = _peer(k)
                sem = a * len(peers) + i
                copies.append(pltpu.make_async_remote_copy(
                    src_ref=refs[a].at[peer] if scatter[a] else refs[a], dst_ref=refs[n_arr + a].at[_me()],
                    send_sem=send_sems.at[sem], recv_sem=recv_sems.at[sem],
                    device_id=peer_id, device_id_type=MESH_ID))
        return copies
    return make


def _forward_copies(n_arr):
    def make(refs, send_sems, recv_sems):
        copies = []
        sibling_id, _ = _peer(SIBLING)
        for a in range(n_arr):
            for i, k in enumerate(CHIP_PEERS):
                slot = refs[a].at[_peer(k)[1]]
                sem = a * len(CHIP_PEERS) + i
                copies.append(pltpu.make_async_remote_copy(
                    src_ref=slot, dst_ref=slot, send_sem=send_sems.at[sem], recv_sem=recv_sems.at[sem],
                    device_id=sibling_id, device_id_type=MESH_ID))
        return copies
    return make


def _split_start(operands, make, n_sem, name):
    n_op = len(operands)

    def body(*refs):
        for cp in make(refs[:n_op], refs[n_op], refs[n_op + 1]):
            cp.start()
        refs[-1][...] = jnp.zeros(refs[-1].shape, F32)

    out = pl.pallas_call(
        body, name=name,
        out_shape=(pltpu.SemaphoreType.DMA((n_sem,)), pltpu.SemaphoreType.DMA((n_sem,)),
                   *[pltpu.HBM(t.shape, t.dtype) for t in operands], jax.ShapeDtypeStruct((8, PACK_LANES), F32)),
        in_specs=[HBM] * n_op, out_specs=(SEM, SEM, *[HBM] * n_op, pl.BlockSpec(memory_space=pltpu.VMEM)),
        input_output_aliases={i: 2 + i for i in range(n_op)},
        compiler_params=pltpu.CompilerParams(has_side_effects=EFFECT),
    )(*[pltpu.with_memory_space_constraint(t, pltpu.HBM) for t in operands])
    return out[:-1], out[-1]


def _split_wait(started, make, after, name):
    send_sems, recv_sems, thru = started[0], started[1], started[2:]
    n_op = len(thru)

    def body(*refs):
        for cp in make(refs[:n_op], refs[n_op], refs[n_op + 1]):
            cp.wait_send()
            cp.wait_recv()
        refs[-1][...] = jnp.zeros(refs[-1].shape, F32)

    out = pl.pallas_call(
        body, name=name,
        out_shape=(*[pltpu.HBM(t.shape, t.dtype) for t in thru], jax.ShapeDtypeStruct((8, PACK_LANES), F32)),
        in_specs=[HBM] * n_op + [SEM, SEM, ANY], out_specs=(*[HBM] * n_op, pl.BlockSpec(memory_space=pltpu.VMEM)),
        input_output_aliases={i: i for i in range(n_op)},
        compiler_params=pltpu.CompilerParams(has_side_effects=EFFECT),
    )(*thru, send_sems, recv_sems, after)
    return out[:-1], out[-1]


def _adamw_math(g, w, m, v):
    nm = ADAM_B1 * m + (1.0 - ADAM_B1) * g
    nv = ADAM_B2 * v + (1.0 - ADAM_B2) * (g * g)
    m_hat = nm / (1.0 - ADAM_B1 ** ADAM_STEP)
    v_hat = nv / (1.0 - ADAM_B2 ** ADAM_STEP)
    return -ADAM_LR * (m_hat / (jnp.sqrt(v_hat) + ADAM_EPS) + ADAM_WD * w), nm, nv


def _sum_slots(s_ref):
    g = s_ref[0].astype(F32)
    for s in range(1, s_ref.shape[0]):
        g = g + s_ref[s].astype(F32)
    return g


def _adamw(slots, w, m, v, tile_rows, name):
    n, rows, cols = slots.shape

    def body(s_ref, w_ref, m_ref, v_ref, g_ref, d_ref, nm_ref, nv_ref):
        g = _sum_slots(s_ref)
        g_ref[...] = g
        d_ref[...], nm_ref[...], nv_ref[...] = _adamw_math(g, w_ref[...], m_ref[...], v_ref[...])

    tile = pl.BlockSpec((tile_rows, cols), lambda i: (i, 0))
    return pl.pallas_call(
        body, name=name, grid=(rows // tile_rows,),
        in_specs=[pl.BlockSpec((n, tile_rows, cols), lambda i: (0, i, 0)), tile, tile, tile],
        out_specs=[tile] * 4, out_shape=[jax.ShapeDtypeStruct((rows, cols), F32)] * 4,
        compiler_params=_params(("parallel",), VMEM_LIMIT),
    )(slots, w, m, v)


def _adamw_part(slots, part, w, m, v, name):
    n, _, rows, cols = slots.shape
    tile_cols = 256

    def body(s_ref, w_ref, m_ref, v_ref, g_ref, d_ref, nm_ref, nv_ref):
        g = _sum_slots(s_ref)
        g_ref[...] = g
        d_ref[...], nm_ref[...], nv_ref[...] = _adamw_math(g, w_ref[...], m_ref[...], v_ref[...])

    tile = pl.BlockSpec((rows, tile_cols), lambda i: (0, i))
    return pl.pallas_call(
        body, name=name, grid=(cols // tile_cols,),
        in_specs=[pl.BlockSpec((n, None, rows, tile_cols), lambda i: (0, part, 0, i)), tile, tile, tile],
        out_specs=[tile] * 4, out_shape=[jax.ShapeDtypeStruct((rows, cols), F32)] * 4,
        compiler_params=_params(("parallel",), VMEM_LIMIT),
    )(slots, w, m, v)


def _reduce_slots(slot_arrays, name):
    def body(*refs):
        n_arr = len(refs) // 2
        for s_ref, o_ref in zip(refs[:n_arr], refs[n_arr:]):
            o_ref[...] = _sum_slots(s_ref)
    return pl.pallas_call(
        body, name=name, out_shape=[jax.ShapeDtypeStruct(s.shape[1:], F32) for s in slot_arrays],
        compiler_params=_params(None, VMEM_LIMIT))(*slot_arrays)


def _adamw_many(grads, ws, ms, vs, name):
    n = len(grads)

    def body(*refs):
        ins, outs = refs[:4 * n], refs[4 * n:]
        for i in range(n):
            g, w, m, v = (ins[j * n + i][...] for j in range(4))
            outs[i][...], outs[n + i][...], outs[2 * n + i][...] = _adamw_math(g, w, m, v)

    out = pl.pallas_call(
        body, name=name, out_shape=[jax.ShapeDtypeStruct(w.shape, F32) for w in ws] * 3,
        compiler_params=_params(None, VMEM_LIMIT))(*grads, *ws, *ms, *vs)
    return out[:n], out[n:2 * n], out[2 * n:]


def _disc_a(lam_re, lam_im, log_step):
    lr = jnp.minimum(lam_re, STEP_FLOOR)
    step = jnp.exp(log_step)
    mag = jnp.exp(lr * step)
    ang = lam_im * step
    abr = mag * jnp.cos(ang)
    abi = mag * jnp.sin(ang)
    nr = abr - 1.0
    den = lr * lr + lam_im * lam_im
    cr = (nr * lr + abi * lam_im) / den
    ci = (abi * lr - nr * lam_im) / den
    return abr, abi, cr, ci


def _disc_b(cr, ci, b_re, b_im):
    return cr * b_re - ci * b_im, cr * b_im + ci * b_re


def _s5_disc_a(lam_re, lam_im, log_step):
    def body(lr_ref, li_ref, ls_ref, *outs):
        for o, val in zip(outs, _disc_a(lr_ref[...], li_ref[...], ls_ref[...])):
            o[...] = val
    return pl.pallas_call(body, name="s5_disc_a", out_shape=[jax.ShapeDtypeStruct(lam_re.shape, F32)] * 4)(
        lam_re, lam_im, log_step)


def _s5_disc_a_bwd(lam_re, lam_im, log_step, cts):
    def body(lr_ref, li_ref, ls_ref, c0, c1, c2, c3, dlr_ref, dli_ref, dls_ref):
        _, vjp = jax.vjp(_disc_a, lr_ref[...], li_ref[...], ls_ref[...])
        dlr, dli, dls = vjp((c0[...], c1[...], c2[...], c3[...]))
        dlr_ref[...] = dlr
        dli_ref[...] = dli
        dls_ref[...] = dls
    return pl.pallas_call(
        body, name="s5_disc_a_bwd",
        out_shape=[jax.ShapeDtypeStruct(lam_re.shape, F32), jax.ShapeDtypeStruct(lam_re.shape, F32),
                   jax.ShapeDtypeStruct(log_step.shape, F32)])(lam_re, lam_im, log_step, *cts)


def _s5_disc_b(cr, ci, b_re, b_im):
    def body(cr_ref, ci_ref, br_ref, bi_ref, o_re, o_im):
        o_re[...], o_im[...] = _disc_b(cr_ref[...], ci_ref[...], br_ref[...], bi_ref[...])
    return pl.pallas_call(body, name="s5_disc_b", out_shape=[jax.ShapeDtypeStruct(b_re.shape, F32)] * 2)(
        cr, ci, b_re, b_im)


def _s5_disc_b_bwd(cr, ci, b_re, b_im, d_re, d_im):
    def body(cr_ref, ci_ref, br_ref, bi_ref, dr_ref, di_ref, dcr_ref, dci_ref, dbr_ref, dbi_ref):
        _, vjp = jax.vjp(_disc_b, cr_ref[...], ci_ref[...], br_ref[...], bi_ref[...])
        dcr_ref[...], dci_ref[...], dbr_ref[...], dbi_ref[...] = vjp((dr_ref[...], di_ref[...]))
    return pl.pallas_call(
        body, name="s5_disc_b_bwd",
        out_shape=[jax.ShapeDtypeStruct(cr.shape, F32)] * 2 + [jax.ShapeDtypeStruct(b_re.shape, F32)] * 2)(
            cr, ci, b_re, b_im, d_re, d_im)


def _cmul(ar, ai, br, bi):
    return ar * br - ai * bi, ar * bi + ai * br


def _cpow(ar, ai, n):
    rr, ri = jnp.ones_like(ar), jnp.zeros_like(ai)
    while n:
        if n & 1:
            rr, ri = _cmul(rr, ri, ar, ai)
        n >>= 1
        if n:
            ar, ai = _cmul(ar, ai, ar, ai)
    return rr, ri


def _tile_rows(i):
    if isinstance(i, int):
        return pl.ds(i * SUBLANES, SUBLANES)
    return pl.ds(pl.multiple_of(i * SUBLANES, SUBLANES), SUBLANES)


def _segment_scan(z_re, z_im, ar, ai, lseg, reverse, visit=None):
    shape = (SUBLANES, z_re.shape[1])
    half = lseg // 2
    arb = jnp.broadcast_to(ar, shape)
    aib = jnp.broadcast_to(ai, shape)
    zero = jnp.zeros(shape, F32)
    row = lax.broadcasted_iota(jnp.int32, shape, 0)

    def tiles(k):
        return (lseg - 1 - k, half - 1 - k) if reverse else (k, half + k)

    def advance(tile, sr, si):
        rows = _tile_rows(tile)
        nr, ni = _cmul(arb, aib, sr, si)
        return rows, nr + z_re[rows, :], ni + z_im[rows, :]

    def first_pass(k, carry):
        ta, tb = tiles(k)
        return advance(ta, carry[0], carry[1])[1:] + advance(tb, carry[2], carry[3])[1:]

    def unrolled(step):
        def body(it, carry):
            for j in range(SCAN_UNROLL):
                carry = step(it * SCAN_UNROLL + j, carry)
            return carry
        return body

    n_iter = half // SCAN_UNROLL
    fa_r, fa_i, fb_r, fb_i = lax.fori_loop(0, n_iter, unrolled(first_pass), (zero,) * 4)
    hr, hi = _cpow(arb, aib, half)
    pr, pi = _cmul(hr, hi, hr, hi)
    fr, fi = _cmul(hr, hi, fa_r, fa_i)
    fr, fi = fr + fb_r, fi + fb_i
    cr, ci = zero, zero
    for _ in range(SUBLANES - 1):
        tr, ti = _cmul(pr, pi, cr, ci)
        tr, ti = tr + fr, ti + fi
        if reverse:
            cr = jnp.where(row == SUBLANES - 1, 0.0, pltpu.roll(tr, SUBLANES - 1, 0))
            ci = jnp.where(row == SUBLANES - 1, 0.0, pltpu.roll(ti, SUBLANES - 1, 0))
        else:
            cr = jnp.where(row == 0, 0.0, pltpu.roll(tr, 1, 0))
            ci = jnp.where(row == 0, 0.0, pltpu.roll(ti, 1, 0))

    br, bi = _cmul(hr, hi, cr, ci)
    br, bi = br + fa_r, bi + fa_i

    def second_pass(k, carry, b_is_tile0=False):
        states, acc = list(carry[:4]), carry[4]
        for chain, tile in enumerate(tiles(k)):
            rows, nr, ni = advance(tile, states[2 * chain], states[2 * chain + 1])
            z_re[rows, :] = nr
            z_im[rows, :] = ni
            states[2 * chain], states[2 * chain + 1] = nr, ni
            if visit is not None:
                acc = visit(tile, nr, ni, acc, chain == 1 and b_is_tile0)
        return (*states, acc)

    acc0 = (zero, zero) if visit is not None else 0
    carry = lax.fori_loop(0, n_iter - 1, unrolled(second_pass), (cr, ci, br, bi, acc0))
    for k in range(half - SCAN_UNROLL, half - 1):
        carry = second_pass(k, carry)
    return second_pass(half - 1, carry, b_is_tile0=reverse)[4]


def _gelu(y):
    c = 0.7978845608028654
    return 0.5 * y * (1.0 + jnp.tanh(c * (y + 0.044715 * y * y * y)))


def _gelu_grad(y):
    c = 0.7978845608028654
    th = jnp.tanh(c * (y + 0.044715 * y * y * y))
    return 0.5 * (1.0 + th) + 0.5 * y * (1.0 - th * th) * c * (1.0 + 3.0 * 0.044715 * y * y)


def _s5_specs(lp):
    col_u = pl.BlockSpec((lp, COL_U), lambda j: (0, j))
    row_u = pl.BlockSpec((1, COL_U), lambda j: (0, j))
    row_s = pl.BlockSpec((1, COL_S), lambda j: (0, j))
    bc_blk = pl.BlockSpec((GROUPS_PER_COL, SSM_GROUP, SSM_STATE), lambda j: (j, 0, 0))
    glu_blk = pl.BlockSpec((GROUPS_PER_COL, SSM_GROUP, SSM_GROUP), lambda j: (j, 0, 0))
    return col_u, row_u, row_s, bc_blk, glu_blk


def _s5_block_diag_scratch():
    return ([pltpu.VMEM((COL_U, COL_S), BF16)] * 4 + [pltpu.VMEM((COL_U, COL_U), BF16)]
            + [pltpu.VMEM((COL_U, COL_S), F32)])


def _fill_block_diag(bd_ref, blocks_ref, stage):
    r, c = blocks_ref.shape[1:]
    stage[...] = jnp.zeros(stage.shape, F32)
    for gl in range(GROUPS_PER_COL):
        stage[pl.ds(gl * r, r), pl.ds(gl * c, c)] = blocks_ref[gl]
    bd_ref[...] = stage[:, :GROUPS_PER_COL * c].astype(BF16)


def _take_block_diag(out_ref, mat):
    r, c = out_ref.shape[1:]
    for gl in range(GROUPS_PER_COL):
        out_ref[gl] = mat[gl * r:(gl + 1) * r, gl * c:(gl + 1) * c]


def _s5_fill_states(u_ref, bre_ref, bim_ref, ar_ref, ai_ref, s_re, s_im, lseg, n_chunks, chunk):
    def fill(cidx, carry):
        rows = pl.ds(pl.multiple_of(cidx * chunk, SUBLANES), chunk)
        ub = u_ref[rows, :].astype(BF16)
        s_re[rows, :] = jnp.dot(ub, bre_ref[...], preferred_element_type=F32)
        s_im[rows, :] = jnp.dot(ub, bim_ref[...], preferred_element_type=F32)
        return carry
    lax.fori_loop(0, n_chunks, fill, 0)
    _segment_scan(s_re, s_im, ar_ref[...], ai_ref[...], lseg, reverse=False)


def _s5_forward(u_p, ar, ai, bbr, bbi, c_re, c_im, d_row, glu_w, glub_row):
    lp = u_p.shape[0]
    lseg = lp // SUBLANES
    chunk, n_chunks = 4 * lseg, SUBLANES // 4

    def body(u_ref, ar_ref, ai_ref, bbr_ref, bbi_ref, cr_ref, ci_ref, d_ref, gw_ref, glub_ref,
             ys_ref, s_re, s_im, bre_ref, bim_ref, cre_ref, cim_ref, glu_ref, stage):
        for bd, blocks in ((bre_ref, bbr_ref), (bim_ref, bbi_ref), (cre_ref, cr_ref), (cim_ref, ci_ref),
                           (glu_ref, gw_ref)):
            _fill_block_diag(bd, blocks, stage)
        _s5_fill_states(u_ref, bre_ref, bim_ref, ar_ref, ai_ref, s_re, s_im, lseg, n_chunks, chunk)

        def emit(cidx, carry):
            rows = pl.ds(pl.multiple_of(cidx * chunk, SUBLANES), chunk)
            y = (_dot_nt(s_re[rows, :], cre_ref[...]) - _dot_nt(s_im[rows, :], cim_ref[...])
                 + d_ref[...] * u_ref[rows, :])
            g = _gelu(y)
            gate = _dot(g, glu_ref[...]) + glub_ref[...]
            ys_ref[rows, :] = g * _sigmoid(gate)
            return carry
        lax.fori_loop(0, n_chunks, emit, 0)

    col_u, row_u, row_s, bc_blk, glu_blk = _s5_specs(lp)
    return pl.pallas_call(
        body, name="s5_forward", grid=(N_COL,),
        in_specs=[col_u, row_s, row_s, bc_blk, bc_blk, bc_blk, bc_blk, row_u, glu_blk, row_u],
        out_specs=col_u, out_shape=jax.ShapeDtypeStruct((lp, D_SSM), F32),
        scratch_shapes=[pltpu.VMEM((lp, COL_S), F32), pltpu.VMEM((lp, COL_S), F32)] + _s5_block_diag_scratch(),
        compiler_params=_params(("arbitrary",), VMEM_LIMIT),
    )(u_p, ar, ai, bbr, bbi, c_re, c_im, d_row, glu_w, glub_row)


def _s5_backward(u_p, dys_p, ar, ai, bbr, bbi, c_re, c_im, d_row, glu_w, glub_row):
    lp = u_p.shape[0]
    lseg = lp // SUBLANES
    chunk, n_chunks = 4 * lseg, SUBLANES // 4

    def body(u_ref, dys_ref, ar_ref, ai_ref, bbr_ref, bbi_ref, cr_ref, ci_ref, d_ref, gw_ref, glub_ref,
             du_ref, dar_ref, dai_ref, dbbr_ref, dbbi_ref, dcr_ref, dci_ref, dd_ref, dgw_ref, dglub_ref,
             s_re, s_im, q_re, q_im, bre_ref, bim_ref, cre_ref, cim_ref, glu_ref, stage,
             dbre_ref, dbim_ref, dcre_ref, dcim_ref, dglu_ref):
        for bd, blocks in ((bre_ref, bbr_ref), (bim_ref, bbi_ref), (cre_ref, cr_ref), (cim_ref, ci_ref),
                           (glu_ref, gw_ref)):
            _fill_block_diag(bd, blocks, stage)
        _s5_fill_states(u_ref, bre_ref, bim_ref, ar_ref, ai_ref, s_re, s_im, lseg, n_chunks, chunk)
        for ref in (dcre_ref, dcim_ref, dd_ref, dglu_ref, dglub_ref, dbre_ref, dbim_ref):
            ref[...] = jnp.zeros(ref.shape, F32)

        def mixer_bwd(cidx, carry):
            rows = pl.ds(pl.multiple_of(cidx * chunk, SUBLANES), chunk)
            u = u_ref[rows, :]
            sr, si = s_re[rows, :], s_im[rows, :]
            y = _dot_nt(sr, cre_ref[...]) - _dot_nt(si, cim_ref[...]) + d_ref[...] * u
            g = _gelu(y)
            sg = _sigmoid(_dot(g, glu_ref[...]) + glub_ref[...])
            dout = dys_ref[rows, :]
            dgate = dout * g * sg * (1.0 - sg)
            dy = (dout * sg + _dot_nt(dgate, glu_ref[...])) * _gelu_grad(y)
            dglu_ref[...] += _dot_tn(g, dgate)
            dglub_ref[...] += jnp.sum(dgate, axis=0, keepdims=True)
            dd_ref[...] += jnp.sum(dy * u, axis=0, keepdims=True)
            dcre_ref[...] += _dot_tn(dy, sr)
            dcim_ref[...] -= _dot_tn(dy, si)
            q_re[rows, :] = _dot(dy, cre_ref[...])
            q_im[rows, :] = -_dot(dy, cim_ref[...])
            du_ref[rows, :] = d_ref[...] * dy
            return carry
        lax.fori_loop(0, n_chunks, mixer_bwd, 0)

        row = lax.broadcasted_iota(jnp.int32, (SUBLANES, COL_S), 0)

        def visit(i, qr, qi, acc, is_tile0):
            if is_tile0:
                prev = _tile_rows(lseg - 1)
                pr = jnp.where(row == 0, 0.0, pltpu.roll(s_re[prev, :], 1, 0))
                pi = jnp.where(row == 0, 0.0, pltpu.roll(s_im[prev, :], 1, 0))
            else:
                prev = _tile_rows(i - 1)
                pr, pi = s_re[prev, :], s_im[prev, :]
            return acc[0] + qr * pr + qi * pi, acc[1] + qi * pr - qr * pi

        dar, dai = _segment_scan(q_re, q_im, ar_ref[...], -ai_ref[...], lseg, reverse=True, visit=visit)
        dar_ref[...] = jnp.sum(dar, axis=0, keepdims=True)
        dai_ref[...] = jnp.sum(dai, axis=0, keepdims=True)

        def input_bwd(cidx, carry):
            rows = pl.ds(pl.multiple_of(cidx * chunk, SUBLANES), chunk)
            qr, qi = q_re[rows, :], q_im[rows, :]
            u = u_ref[rows, :]
            du_ref[rows, :] += _dot_nt(qr, bre_ref[...]) + _dot_nt(qi, bim_ref[...])
            dbre_ref[...] += _dot_tn(u, qr)
            dbim_ref[...] += _dot_tn(u, qi)
            return carry
        lax.fori_loop(0, n_chunks, input_bwd, 0)
        for out, acc in ((dbbr_ref, dbre_ref), (dbbi_ref, dbim_ref), (dcr_ref, dcre_ref), (dci_ref, dcim_ref),
                         (dgw_ref, dglu_ref)):
            _take_block_diag(out, acc[...])

    col_u, row_u, row_s, bc_blk, glu_blk = _s5_specs(lp)
    group_mats = jax.ShapeDtypeStruct((SSM_GROUPS, SSM_GROUP, SSM_STATE), F32)
    return pl.pallas_call(
        body, name="s5_backward", grid=(N_COL,),
        in_specs=[col_u, col_u, row_s, row_s, bc_blk, bc_blk, bc_blk, bc_blk, row_u, glu_blk, row_u],
        out_specs=[col_u, row_s, row_s, bc_blk, bc_blk, bc_blk, bc_blk, row_u, glu_blk, row_u],
        out_shape=[jax.ShapeDtypeStruct((lp, D_SSM), F32),
                   jax.ShapeDtypeStruct((1, N_COL * COL_S), F32), jax.ShapeDtypeStruct((1, N_COL * COL_S), F32),
                   group_mats, group_mats, group_mats, group_mats, jax.ShapeDtypeStruct((1, D_SSM), F32),
                   jax.ShapeDtypeStruct((SSM_GROUPS, SSM_GROUP, SSM_GROUP), F32), jax.ShapeDtypeStruct((1, D_SSM), F32)],
        scratch_shapes=([pltpu.VMEM((lp, COL_S), F32)] * 4 + _s5_block_diag_scratch()
                        + [pltpu.VMEM((COL_U, COL_S), F32)] * 4 + [pltpu.VMEM((COL_U, COL_U), F32)]),
        compiler_params=_params(("arbitrary",), VMEM_LIMIT),
    )(u_p, dys_p, ar, ai, bbr, bbi, c_re, c_im, d_row, glu_w, glub_row)


def _window_sum(ext, group, leading):
    n = ext.shape[0]
    s = ext
    for j in range(POOL_GROUPS):
        shift = n - (1 << j) if leading else 1 << j
        s = jnp.where(j <= group, s + pltpu.roll(s, shift, 0), s)
    return s


def _pool_inv_count(tile, window, first_row):
    t = tile * TM + lax.broadcasted_iota(jnp.int32, (TM, 1), 0) - first_row
    return 1.0 / jnp.clip(t + 1, 1, window).astype(F32)


def _pool_specs(lp):
    col = pl.BlockSpec((lp, POOL_DIM), lambda k: (0, k))
    mat = pl.BlockSpec((None, POOL_DIM, POOL_DIM), lambda k: (k, 0, 0))
    row = pl.BlockSpec((None, 1, POOL_DIM), lambda k: (k, 0, 0))
    return col, mat, row


def _pool_forward(v, pool_w, pool_scale, first_row):
    lp = v.shape[0]
    n_tiles = lp // TM

    def body(v_ref, w_ref, sc_ref, yp_ref, vpad):
        group = pl.program_id(0)
        window = jnp.left_shift(2, group)
        vpad[pl.ds(0, POOL_HALO), :] = jnp.zeros((POOL_HALO, POOL_DIM), F32)
        vpad[pl.ds(POOL_HALO, lp), :] = v_ref[...]

        def tile(j, carry):
            start = pl.multiple_of(j * TM, TM)
            ext = vpad[pl.ds(start, TM + POOL_HALO), :]
            sums = _window_sum(ext, group, leading=False)[POOL_HALO:, :]
            p = sums * _pool_inv_count(j, window, first_row) - ext[POOL_HALO:, :]
            yp_ref[pl.ds(start, TM), :] = _dot(p, w_ref[...]) * sc_ref[...]
            return carry
        lax.fori_loop(0, n_tiles, tile, 0)

    col, mat, row = _pool_specs(lp)
    return pl.pallas_call(
        body, name="pool_forward", grid=(POOL_GROUPS,),
        in_specs=[col, mat, row], out_specs=col, out_shape=jax.ShapeDtypeStruct((lp, D_SSM), F32),
        scratch_shapes=[pltpu.VMEM((lp + POOL_HALO, POOL_DIM), F32)],
        compiler_params=_params(("arbitrary",), VMEM_LIMIT),
    )(v, pool_w, pool_scale)


def _pool_backward(v, dyp, pool_w, pool_scale, first_row):
    lp = v.shape[0]
    n_tiles = lp // TM

    def body(v_ref, dyp_ref, w_ref, sc_ref, dv_ref, dw_ref, dsc_ref, vpad, gpad):
        group = pl.program_id(0)
        window = jnp.left_shift(2, group)
        vpad[pl.ds(0, POOL_HALO), :] = jnp.zeros((POOL_HALO, POOL_DIM), F32)
        vpad[pl.ds(POOL_HALO, lp), :] = v_ref[...]
        gpad[pl.ds(lp, POOL_HALO), :] = jnp.zeros((POOL_HALO, POOL_DIM), F32)
        dw_ref[...] = jnp.zeros(dw_ref.shape, F32)
        dsc_ref[...] = jnp.zeros(dsc_ref.shape, F32)

        def linear_bwd(j, carry):
            start = pl.multiple_of(j * TM, TM)
            ext = vpad[pl.ds(start, TM + POOL_HALO), :]
            inv = _pool_inv_count(j, window, first_row)
            p = _window_sum(ext, group, leading=False)[POOL_HALO:, :] * inv - ext[POOL_HALO:, :]
            z = _dot(p, w_ref[...])
            dyp_t = dyp_ref[pl.ds(start, TM), :]
            dz = dyp_t * sc_ref[...]
            dsc_ref[...] += jnp.sum(dyp_t * z, axis=0, keepdims=True)
            dw_ref[...] += _dot_tn(p, dz)
            dp = _dot_nt(dz, w_ref[...])
            gpad[pl.ds(start, TM), :] = dp * inv
            dv_ref[pl.ds(start, TM), :] = -dp
            return carry
        lax.fori_loop(0, n_tiles, linear_bwd, 0)

        def window_bwd(j, carry):
            start = pl.multiple_of(j * TM, TM)
            ext = gpad[pl.ds(start, TM + POOL_HALO), :]
            dv_ref[pl.ds(start, TM), :] += _window_sum(ext, group, leading=True)[:TM, :]
            return carry
        lax.fori_loop(0, n_tiles, window_bwd, 0)

    col, mat, row = _pool_specs(lp)
    return pl.pallas_call(
        body, name="pool_backward", grid=(POOL_GROUPS,),
        in_specs=[col, col, mat, row], out_specs=[col, mat, row],
        out_shape=[jax.ShapeDtypeStruct((lp, D_SSM), F32),
                   jax.ShapeDtypeStruct((POOL_GROUPS, POOL_DIM, POOL_DIM), F32),
                   jax.ShapeDtypeStruct((POOL_GROUPS, 1, POOL_DIM), F32)],
        scratch_shapes=[pltpu.VMEM((lp + POOL_HALO, POOL_DIM), F32)] * 2,
        compiler_params=_params(("arbitrary",), VMEM_LIMIT),
    )(v, dyp, pool_w, pool_scale)


def _row_specs():
    head = _const((HEAD, D_MODEL))
    xrow = pl.BlockSpec((TM, D_MODEL), _xrow)
    full = pl.BlockSpec((TM, D_MODEL), lambda i: (i, 0))
    half = pl.BlockSpec((TM, D_SSM), lambda i: (i, 0))
    return head, xrow, full, half


def _in_proj(head, x, g1, w_in):
    n_tiles = (HEAD + x.shape[0]) // TM
    lp = n_tiles * TM

    def body(head_ref, x_ref, g_ref, w_ref, u_ref, v_ref):
        h0 = jnp.where(pl.program_id(0) == 0, head_ref[...], x_ref[...])
        proj = _dot(h0 * _rstd(h0) * g_ref[...], w_ref[...])
        u_ref[...] = proj[:, :D_SSM]
        v_ref[...] = proj[:, D_SSM:]

    head_s, xrow, _, half = _row_specs()
    return pl.pallas_call(
        body, name="in_proj", grid=(n_tiles,),
        in_specs=[head_s, xrow, _const((1, D_MODEL)), _const((D_MODEL, D_MODEL))],
        out_specs=[half, half], out_shape=[jax.ShapeDtypeStruct((lp, D_SSM), F32)] * 2,
        compiler_params=_params(("parallel",), VMEM_LIMIT),
    )(head, x, g1, w_in)


def _out_proj(head, x, ys, yp, gs, gp, w_out):
    lp = ys.shape[0]

    def body(head_ref, x_ref, ys_ref, yp_ref, gs_ref, gp_ref, w_ref, h1_ref):
        h0 = jnp.where(pl.program_id(0) == 0, head_ref[...], x_ref[...])
        ys_t, yp_t = ys_ref[...], yp_ref[...]
        ms = ys_t * _rstd(ys_t) * gs_ref[...]
        mp = yp_t * _rstd(yp_t) * gp_ref[...]
        h1_ref[...] = h0 + _dot(ms, w_ref[pl.ds(0, D_SSM), :]) + _dot(mp, w_ref[pl.ds(D_SSM, D_SSM), :])

    head_s, xrow, full, half = _row_specs()
    return pl.pallas_call(
        body, name="out_proj", grid=(lp // TM,),
        in_specs=[head_s, xrow, half, half, _const((1, D_SSM)), _const((1, D_SSM)), _const((D_MODEL, D_MODEL))],
        out_specs=full, out_shape=jax.ShapeDtypeStruct((lp, D_MODEL), F32),
        compiler_params=_params(("parallel",), VMEM_LIMIT),
    )(head, x, ys, yp, gs, gp, w_out)


def _load_weights(hbm_refs, vmem_refs, sems):
    @pl.when(pl.program_id(0) == 0)
    def _():
        copies = [pltpu.make_async_copy(h, v, sems.at[n]) for n, (h, v) in enumerate(zip(hbm_refs, vmem_refs))]
        for cp in copies:
            cp.start()
        for cp in copies:
            cp.wait()


def _ffn_scratch():
    return [pltpu.VMEM((D_FF, D_MODEL), BF16)] * 3 + [pltpu.SemaphoreType.DMA((3,))]


def _ff_tile(t):
    return pl.ds(t * FF_TILE, FF_TILE)


def _ffn_forward(h1, g2, wg_t, wu_t, wd):
    lp = h1.shape[0]

    def body(h1_ref, g_ref, wg_hbm, wu_hbm, wd_hbm, ab_ref, n2_ref, h2_ref, wg, wu, wdn, sems):
        _load_weights((wg_hbm, wu_hbm, wd_hbm), (wg, wu, wdn), sems)
        h1_t = h1_ref[...]
        n2 = (h1_t * _rstd(h1_t) * g_ref[...]).astype(BF16)
        n2_ref[...] = n2
        acc = h1_t
        for t in range(D_FF // FF_TILE):
            a = _dot_nt(n2, wg[_ff_tile(t), :])
            b = _dot_nt(n2, wu[_ff_tile(t), :])
            ab_ref[:, _ff_tile(t)] = a.astype(BF16)
            ab_ref[:, pl.ds(D_FF + t * FF_TILE, FF_TILE)] = b.astype(BF16)
            acc = acc + _dot(a * _sigmoid(a) * b, wdn[_ff_tile(t), :])
        h2_ref[...] = acc

    _, _, full, _ = _row_specs()
    wide = pl.BlockSpec((TM, 2 * D_FF), lambda i: (i, 0))
    half_width = pl.BlockSpec((TM, D_MODEL), lambda i: (i, 0))
    return pl.pallas_call(
        body, name="ffn_forward", grid=(lp // TM,),
        in_specs=[full, _const((1, D_MODEL)), ANY, ANY, ANY], out_specs=[wide, half_width, full],
        out_shape=[jax.ShapeDtypeStruct((lp, 2 * D_FF), BF16), jax.ShapeDtypeStruct((lp, D_MODEL), BF16),
                   jax.ShapeDtypeStruct((lp, D_MODEL), F32)],
        scratch_shapes=_ffn_scratch(), compiler_params=_params(("arbitrary",), VMEM_LIMIT),
    )(h1, g2, wg_t, wu_t, wd)


def _ffn_backward(h2, target, h1, ab, gf, g2, wg_t, wu_t, wd):
    lp = h1.shape[0]

    def body(h2_ref, t_ref, h1_ref, ab_ref, gf_ref, g2_ref, wg_hbm, wu_hbm, wd_hbm,
             dh1_ref, xt_ref, dh2_ref, loss_ref, dgf_ref, dg2_ref, wg, wu, wdn, sems):
        i = pl.program_id(0)
        _load_weights((wg_hbm, wu_hbm, wd_hbm), (wg, wu, wdn), sems)

        @pl.when(i == 0)
        def _():
            loss_ref[...] = jnp.zeros(loss_ref.shape, F32)
            dgf_ref[...] = jnp.zeros(dgf_ref.shape, F32)
            dg2_ref[...] = jnp.zeros(dg2_ref.shape, F32)

        h2_t = h2_ref[...]
        rf = _rstd(h2_t)
        xf = h2_t * rf
        diff = jnp.where(i == 0, 0.0, xf * gf_ref[...] - t_ref[...])
        loss_ref[...] += 0.5 * jnp.sum(diff * diff) / D_MODEL
        dh2, dgf = _rms_bwd(diff / D_MODEL, xf, rf, gf_ref[...])
        dgf_ref[...] += dgf
        dh2_b = dh2.astype(BF16)
        dh2_ref[...] = dh2_b

        dn2 = jnp.zeros((TM, D_MODEL), F32)
        for t in range(D_FF // FF_TILE):
            dff = _dot_nt(dh2_b, wdn[_ff_tile(t), :])
            a = ab_ref[:, _ff_tile(t)].astype(F32)
            b = ab_ref[:, pl.ds(D_FF + t * FF_TILE, FF_TILE)].astype(F32)
            sg = _sigmoid(a)
            silu = a * sg
            da = dff * b * sg * (1.0 + a * (1.0 - sg))
            db = dff * silu
            for part, val in enumerate((da, db, silu * b)):
                xt_ref[pl.ds(part * D_FF + t * FF_TILE, FF_TILE), :] = val.T.astype(BF16)
            dn2 = dn2 + _dot(da, wg[_ff_tile(t), :]) + _dot(db, wu[_ff_tile(t), :])

        h1_t = h1_ref[...]
        r2 = _rstd(h1_t)
        dx, dg2 = _rms_bwd(dn2, h1_t * r2, r2, g2_ref[...])
        dg2_ref[...] += dg2
        dh1_ref[...] = dh2 + dx

    _, xrow, full, _ = _row_specs()
    wide = pl.BlockSpec((TM, 2 * D_FF), lambda i: (i, 0))
    half_width = pl.BlockSpec((TM, D_MODEL), lambda i: (i, 0))
    vec = _const((1, D_MODEL))
    return pl.pallas_call(
        body, name="ffn_backward", grid=(lp // TM,),
        in_specs=[full, xrow, full, wide, vec, vec, ANY, ANY, ANY],
        out_specs=[full, pl.BlockSpec((3 * D_FF, TM), lambda i: (0, i)), half_width, _const((1, PACK_LANES)), vec, vec],
        out_shape=[jax.ShapeDtypeStruct((lp, D_MODEL), F32),
                   jax.ShapeDtypeStruct((3 * D_FF, lp), BF16),
                   jax.ShapeDtypeStruct((lp, D_MODEL), BF16),
                   jax.ShapeDtypeStruct((1, PACK_LANES), F32),
                   jax.ShapeDtypeStruct((1, D_MODEL), F32), jax.ShapeDtypeStruct((1, D_MODEL), F32)],
        scratch_shapes=_ffn_scratch(), compiler_params=_params(("arbitrary",), VMEM_LIMIT),
    )(h2, target, h1, ab, gf, g2, wg_t, wu_t, wd)


def _ffn_wgrad(xt, n2, dh2):
    lp = n2.shape[0]
    rows = lp // WGRAD_STEPS
    n_tiles = 3 * D_FF // FF_TILE
    shards_per_tile = FF_TILE // FF_SHARD
    gate_up_tiles = 2 * D_FF // FF_TILE

    def body(xt_ref, n2_ref, dh2_ref, out_ref, acc):
        q, k = pl.program_id(0), pl.program_id(1)

        @pl.when(k == 0)
        def _():
            acc[...] = jnp.zeros(acc.shape, F32)

        @pl.when(q < gate_up_tiles)
        def _():
            acc[...] += jnp.dot(xt_ref[...], n2_ref[...], preferred_element_type=F32)

        @pl.when(q >= gate_up_tiles)
        def _():
            acc[...] += jnp.dot(xt_ref[...], dh2_ref[...], preferred_element_type=F32)

        @pl.when(k == pl.num_programs(1) - 1)
        def _():
            for s in range(shards_per_tile):
                out_ref[s] = acc[pl.ds(s * FF_SHARD, FF_SHARD), :].astype(BF16)

    tiles_per_matrix = D_FF // FF_TILE
    return pl.pallas_call(
        body, name="ffn_wgrad", grid=(n_tiles, WGRAD_STEPS),
        in_specs=[pl.BlockSpec((FF_TILE, rows), lambda q, k: (q, k)),
                  pl.BlockSpec((rows, D_MODEL), lambda q, k: (jnp.where(q < gate_up_tiles, k, 0), 0)),
                  pl.BlockSpec((rows, D_MODEL), lambda q, k: (jnp.where(q < gate_up_tiles, 0, k), 0))],
        out_specs=pl.BlockSpec((shards_per_tile, None, FF_SHARD, D_MODEL),
                               lambda q, k: (q % tiles_per_matrix, q // tiles_per_matrix, 0, 0)),
        out_shape=jax.ShapeDtypeStruct((N_DEV, 3, FF_SHARD, D_MODEL), BF16),
        scratch_shapes=[pltpu.VMEM((FF_TILE, D_MODEL), F32)],
        compiler_params=_params(("parallel", "arbitrary"), VMEM_LIMIT),
    )(xt, n2, dh2)


def _out_proj_backward(dh1, ys, yp, gs, gp, w_out):
    lp = ys.shape[0]

    def body(dh1_ref, ys_ref, yp_ref, gs_ref, gp_ref, w_ref, dys_ref, dyp_ref, dgs_ref, dgp_ref, dw_out, dw_ref):
        @pl.when(pl.program_id(0) == 0)
        def _():
            dgs_ref[...] = jnp.zeros(dgs_ref.shape, F32)
            dgp_ref[...] = jnp.zeros(dgp_ref.shape, F32)
            dw_ref[...] = jnp.zeros(dw_ref.shape, F32)

        dh1_b = dh1_ref[...].astype(BF16)
        dmix = _dot_nt(dh1_b, w_ref[...])
        for y_ref, g_ref, dy_ref, dg_ref, lo in ((ys_ref, gs_ref, dys_ref, dgs_ref, 0),
                                                 (yp_ref, gp_ref, dyp_ref, dgp_ref, D_SSM)):
            y_t = y_ref[...]
            r = _rstd(y_t)
            xhat = y_t * r
            dy, dg = _rms_bwd(dmix[:, lo:lo + D_SSM], xhat, r, g_ref[...])
            dy_ref[...] = dy
            dg_ref[...] += dg
            dw_ref[pl.ds(lo, D_SSM), :] += _dot_tn(xhat * g_ref[...], dh1_b)

        @pl.when(pl.program_id(0) == pl.num_programs(0) - 1)
        def _():
            dw_out[...] = dw_ref[...].astype(BF16)

    _, _, full, half = _row_specs()
    vec = _const((1, D_SSM))
    return pl.pallas_call(
        body, name="out_proj_backward", grid=(lp // TM,),
        in_specs=[full, half, half, vec, vec, _const((D_MODEL, D_MODEL))],
        out_specs=[half, half, vec, vec, _const((D_MODEL, D_MODEL))],
        out_shape=[jax.ShapeDtypeStruct((lp, D_SSM), F32)] * 2 + [jax.ShapeDtypeStruct((1, D_SSM), F32)] * 2
        + [jax.ShapeDtypeStruct((D_MODEL, D_MODEL), BF16)],
        scratch_shapes=[pltpu.VMEM((D_MODEL, D_MODEL), F32)],
        compiler_params=_params(("arbitrary",), VMEM_LIMIT),
    )(dh1, ys, yp, gs, gp, w_out)


def _in_proj_backward(head, x, du, dv, dh1, g1, w_in):
    lp = du.shape[0]

    def body(head_ref, x_ref, du_ref, dv_ref, dh1_ref, g_ref, w_ref, dx_ref, dhead_ref, dg_ref, dw_out, dw_ref):
        i = pl.program_id(0)

        @pl.when(i == 0)
        def _():
            dg_ref[...] = jnp.zeros(dg_ref.shape, F32)
            dw_ref[...] = jnp.zeros(dw_ref.shape, F32)

        h0 = jnp.where(i == 0, head_ref[...], x_ref[...])
        r = _rstd(h0)
        xhat = h0 * r
        n1 = (xhat * g_ref[...]).astype(BF16)
        du_b, dv_b = du_ref[...].astype(BF16), dv_ref[...].astype(BF16)
        dn1 = _dot_nt(du_b, w_ref[:, pl.ds(0, D_SSM)]) + _dot_nt(dv_b, w_ref[:, pl.ds(D_SSM, D_SSM)])
        dx, dg = _rms_bwd(dn1, xhat, r, g_ref[...])
        dg_ref[...] += dg
        dh0 = dh1_ref[...] + dx
        dx_ref[...] = dh0

        @pl.when(i == 0)
        def _():
            dhead_ref[...] = dh0

        dw_ref[:, pl.ds(0, D_SSM)] += _dot_tn(n1, du_b)
        dw_ref[:, pl.ds(D_SSM, D_SSM)] += _dot_tn(n1, dv_b)

        @pl.when(i == pl.num_programs(0) - 1)
        def _():
            dw_out[...] = dw_ref[...].astype(BF16)

    head_s, xrow, full, half = _row_specs()
    vec = _const((1, D_MODEL))
    mat = _const((D_MODEL, D_MODEL))
    return pl.pallas_call(
        body, name="in_proj_backward", grid=(lp // TM,),
        in_specs=[head_s, xrow, half, half, full, vec, mat],
        out_specs=[xrow, head_s, vec, mat],
        out_shape=[jax.ShapeDtypeStruct(x.shape, F32), jax.ShapeDtypeStruct((HEAD, D_MODEL), F32),
                   jax.ShapeDtypeStruct((1, D_MODEL), F32), jax.ShapeDtypeStruct((D_MODEL, D_MODEL), BF16)],
        scratch_shapes=[pltpu.VMEM((D_MODEL, D_MODEL), F32)],
        compiler_params=_params(("arbitrary",), VMEM_LIMIT),
    )(head, x, du, dv, dh1, g1, w_in)


def _permute_rows(a):
    lp, n = a.shape
    return a.reshape(SUBLANES, lp // SUBLANES, n).transpose(1, 0, 2).reshape(lp, n)


def _unpermute_rows(a):
    lp, n = a.shape
    return a.reshape(lp // SUBLANES, SUBLANES, n).transpose(1, 0, 2).reshape(lp, n)


def _pack(parts, dtype):
    rows = []
    for p in parts:
        flat = p.reshape(-1).astype(dtype)
        pad = (-flat.shape[0]) % PACK_UNIT
        rows.append(jnp.pad(flat, (0, pad)).reshape(-1, PACK_LANES))
    n_rows = sum(r.shape[0] for r in rows)
    if n_rows % 16:
        rows.append(jnp.zeros((8, PACK_LANES), dtype))
    return jnp.concatenate(rows, axis=0)


def _as2d(a):
    return a.reshape(-1, a.shape[-1])


def _unpack(packed, shapes):
    out, row = [], 0
    for shape in shapes:
        size = 1
        for s in shape:
            size *= s
        n_rows = -(-size // PACK_UNIT) * 8
        out.append(packed[row:row + n_rows].reshape(-1)[:size].reshape(shape))
        row += n_rows
    return out


def _to_view(name, a):
    if name in ("ssm_b_re", "ssm_b_im"):
        return a[0].transpose(0, 2, 1).reshape(-1, SSM_STATE)
    if name in ("ssm_d", "ssm_glu_b"):
        return a[0].T
    if name == "ssm_glu_w":
        return a[0].transpose(1, 2, 0).reshape(-1, SSM_GROUPS)
    return _as2d(a)


def _from_view(name, r, shape):
    if name in ("ssm_b_re", "ssm_b_im"):
        return r.reshape(SSM_GROUPS, SSM_GROUP, SSM_STATE).transpose(0, 2, 1).reshape(shape)
    if name in ("ssm_d", "ssm_glu_b"):
        return r.T.reshape(shape)
    if name == "ssm_glu_w":
        return r.reshape(SSM_GROUP, SSM_GROUP, SSM_GROUPS).transpose(2, 0, 1).reshape(shape)
    return r.reshape(shape)


def kernel(x, meta_tokens, norm1_g, w_in, ssm_lambda_re, ssm_lambda_im, ssm_log_step, ssm_b_re, ssm_b_im, ssm_c_re, ssm_c_im, ssm_d, ssm_glu_w, ssm_glu_b, ssm_norm_g, pool_w, pool_scale, pool_norm_g, w_out, norm2_g, w_gate, w_up, w_down, final_norm_g, loss_target, m_meta_tokens, m_norm1_g, m_w_in, m_ssm_lambda_re, m_ssm_lambda_im, m_ssm_log_step, m_ssm_b_re, m_ssm_b_im, m_ssm_c_re, m_ssm_c_im, m_ssm_d, m_ssm_glu_w, m_ssm_glu_b, m_ssm_norm_g, m_pool_w, m_pool_scale, m_pool_norm_g, m_w_out, m_norm2_g, m_w_gate, m_w_up, m_w_down, m_final_norm_g, v_meta_tokens, v_norm1_g, v_w_in, v_ssm_lambda_re, v_ssm_lambda_im, v_ssm_log_step, v_ssm_b_re, v_ssm_b_im, v_ssm_c_re, v_ssm_c_im, v_ssm_d, v_ssm_glu_w, v_ssm_glu_b, v_ssm_norm_g, v_pool_w, v_pool_scale, v_pool_norm_g, v_w_out, v_norm2_g, v_w_gate, v_w_up, v_w_down, v_final_norm_g):
    given = dict(locals())
    weights = {n: given[n] for n in WEIGHT_NAMES}
    n_meta = meta_tokens.shape[0]
    me = 4 * lax.axis_index("x") + 2 * lax.axis_index("y") + lax.axis_index("c")

    shard_rows = w_in.shape[1]
    first = [w_in[0].astype(BF16), meta_tokens]
    first_make = _push_copies(ALL_PEERS, [False, False])
    first_x, first_token = _split_start(first + [_landing(s, False) for s in first], first_make,
                                        2 * len(ALL_PEERS), "gather_w_in_start")

    xs = x[0]
    tgt = loss_target[0]
    first_row = HEAD - n_meta
    g1, g2, gf = norm1_g, norm2_g, final_norm_g.reshape(1, D_MODEL)
    gs, gp = ssm_norm_g, pool_norm_g

    lam_re, lam_im = ssm_lambda_re[0] + first_token[:1, :1], ssm_lambda_im[0]
    log_step = ssm_log_step[0].reshape(SSM_GROUPS, 1)
    b_re = ssm_b_re[0].transpose(0, 2, 1)
    b_im = ssm_b_im[0].transpose(0, 2, 1)
    abr, abi, zr, zi = _s5_disc_a(lam_re, lam_im, log_step)
    zr_col, zi_col = zr.reshape(SSM_GROUPS, 1, SSM_STATE), zi.reshape(SSM_GROUPS, 1, SSM_STATE)
    bbr, bbi = _s5_disc_b(zr_col, zi_col, b_re, b_im)
    s5_consts = (abr.reshape(1, -1), abi.reshape(1, -1), bbr, bbi, ssm_c_re[0], ssm_c_im[0],
                 ssm_d[0].reshape(1, D_SSM), ssm_glu_w[0], ssm_glu_b[0].reshape(1, D_SSM))
    pool_sc = pool_scale[0].reshape(POOL_GROUPS, 1, POOL_DIM)

    (_, _, w_in_all, meta_all), first_done = _split_wait(first_x, first_make, bbr, "gather_w_in_wait")
    w_in_all = w_in_all.reshape(D_MODEL, D_MODEL)
    meta_full = meta_all.transpose(1, 0, 2).reshape(n_meta, D_MODEL)
    head = jnp.concatenate([jnp.zeros((HEAD - n_meta, D_MODEL), F32), meta_full], axis=0)
    shards = [(w_out[0] + first_done[:1, :1]).astype(BF16), w_gate[0].T.astype(BF16), w_up[0].T.astype(BF16),
              w_down[0].astype(BF16)]
    n_big = len(shards)
    gather_make = _push_copies((SIBLING,) + CHIP_PEERS, [False] * n_big)
    gather, gather_token = _split_start(shards + [_landing(s, False) for s in shards], gather_make,
                                        n_big * (1 + len(CHIP_PEERS)), "gather_start")

    u, v = _in_proj(head, xs, g1 + gather_token[:1, :1], w_in_all)
    u_p = _permute_rows(u)
    ys_p = _s5_forward(u_p, *s5_consts)
    landed, _ = _split_wait(gather, gather_make, ys_p, "gather_wait")
    forward_make = _forward_copies(n_big)
    forward, forward_token = _split_start(list(landed[n_big:]), forward_make, n_big * len(CHIP_PEERS),
                                          "gather_forward_start")
    ys = _unpermute_rows(ys_p)
    yp = _pool_forward(v, pool_w[0], pool_sc + forward_token[:1, :1], first_row)
    (w_out_all, wg_t, wu_t, wd_all), _ = _split_wait(forward, forward_make, yp, "gather_forward_wait")
    w_out_all = w_out_all.reshape(D_MODEL, D_MODEL)
    ffn_weights = [w.reshape(D_FF, D_MODEL) for w in (wg_t, wu_t, wd_all)]
    h1 = _out_proj(head, xs, ys, yp, gs, gp, w_out_all)
    ab, n2, h2 = _ffn_forward(h1, g2, *ffn_weights)

    dh1, xt, dh2, loss_part, d_gf, d_g2 = _ffn_backward(h2, tgt, h1, ab, gf, g2, *ffn_weights)
    d_ffn = _ffn_wgrad(xt, n2, dh2)
    ffn_make = _push_copies(ALL_PEERS, [True])
    ffn_x, ffn_token = _split_start([d_ffn, _landing(d_ffn, True)], ffn_make, len(ALL_PEERS), "ffn_grad_start")
    dys, dyp, d_gs, d_gp, d_wout = _out_proj_backward(dh1, ys, yp, gs + ffn_token[:1, :1], gp, w_out_all)
    dv, d_pool_w, d_pool_sc = _pool_backward(v, dyp, pool_w[0], pool_sc, first_row)
    (du_p, d_ar, d_ai, d_bbr, d_bbi, d_c_re, d_c_im, d_d, d_glu, d_glub) = _s5_backward(
        u_p, _permute_rows(dys), *s5_consts)
    du = _unpermute_rows(du_p)

    d_zr, d_zi, d_b_re, d_b_im = _s5_disc_b_bwd(zr_col, zi_col, b_re, b_im, d_bbr, d_bbi)
    d_lam_re, d_lam_im, d_log_step = _s5_disc_a_bwd(
        lam_re, lam_im, log_step,
        (d_ar.reshape(SSM_GROUPS, SSM_STATE), d_ai.reshape(SSM_GROUPS, SSM_STATE),
         d_zr.reshape(SSM_GROUPS, SSM_STATE), d_zi.reshape(SSM_GROUPS, SSM_STATE)))
    groups_last = lambda row: row.reshape(SSM_GROUPS, SSM_GROUP).T
    small_grads = {
        "ssm_lambda_re": d_lam_re, "ssm_lambda_im": d_lam_im, "ssm_log_step": d_log_step.reshape(1, SSM_GROUPS),
        "ssm_b_re": d_b_re.reshape(-1, SSM_STATE), "ssm_b_im": d_b_im.reshape(-1, SSM_STATE),
        "ssm_c_re": d_c_re.reshape(-1, SSM_STATE), "ssm_c_im": d_c_im.reshape(-1, SSM_STATE),
        "ssm_d": groups_last(d_d), "ssm_glu_w": d_glu.transpose(1, 2, 0).reshape(-1, SSM_GROUPS),
        "ssm_glu_b": groups_last(d_glub),
        "ssm_norm_g": d_gs, "pool_w": d_pool_w.reshape(-1, POOL_DIM), "pool_scale": d_pool_sc.reshape(-1, POOL_DIM),
        "pool_norm_g": d_gp, "norm2_g": d_g2,
    }

    early_names = SMALL_NAMES[1:-1]
    early_pack = _pack([small_grads[n] for n in early_names], BF16)
    d_wout = d_wout.reshape(N_DEV, shard_rows, D_MODEL)
    early_make = _push_copies(ALL_PEERS, [True, False])
    early_x, early_token = _split_start([d_wout, early_pack, _landing(d_wout, True), _landing(early_pack, False)],
                                        early_make, 2 * len(ALL_PEERS), "early_grad_start")
    d_x, d_head, d_g1, d_win = _in_proj_backward(head, xs, du, dv, dh1, g1 + early_token[:1, :1], w_in_all)
    (_, r_ffn), _ = _split_wait(ffn_x, ffn_make, d_g1, "ffn_grad_wait")
    (_, _, r_wout, r_early), _ = _split_wait(early_x, early_make, d_g1, "early_grad_wait")
    d_win = d_win.reshape(N_DEV, shard_rows, D_MODEL)
    late_pack = _pack([d_g1, d_gf, d_head[first_row:], loss_part], F32)
    late_make = _push_copies(ALL_PEERS, [True, False])
    late_x, late_token = _split_start([d_win, late_pack, _landing(d_win, True), _landing(late_pack, False)],
                                      late_make, 2 * len(ALL_PEERS), "late_grad_start")

    results = {}
    res_gate = _adamw_part(r_ffn, 0, w_gate[0].T + late_token[:1, :1], m_w_gate[0].T, v_w_gate[0].T, "adamw_w_gate")
    res_up = _adamw_part(r_ffn, 1, w_up[0].T, m_w_up[0].T, v_w_up[0].T, "adamw_w_up")
    results["w_gate"] = [r.T for r in res_gate]
    results["w_up"] = [r.T for r in res_up]
    results["w_down"] = _adamw_part(r_ffn, 2, w_down[0], m_w_down[0], v_w_down[0], "adamw_w_down")
    results["w_out"] = _adamw(r_wout, w_out[0], m_w_out[0], v_w_out[0], shard_rows, "adamw_w_out")
    done = sum(res[1][:1, :1] for res in (res_gate, res_up, results["w_down"], results["w_out"]))
    (_, _, r_win, r_late), _ = _split_wait(late_x, late_make, done, "late_grad_wait")
    results["w_in"] = _adamw(r_win, w_in[0], m_w_in[0], v_w_in[0], shard_rows, "adamw_w_in")

    sum_early, sum_late = _reduce_slots([r_early, r_late], "small_grad_sums")
    views = lambda prefix: [_to_view(n, given[prefix + n]) for n in SMALL_NAMES]
    w_views = views("")
    g_views = _unpack(sum_early, [w.shape for w in w_views[1:-1]])
    g_norm1, g_final, g_meta_all, loss_row = _unpack(
        sum_late, [norm1_g.shape, (1, D_MODEL), (n_meta, D_MODEL), (1, PACK_LANES)])
    g_views = [g_norm1] + g_views + [g_final]
    res_small = _adamw_many(g_views, w_views, views("m_"), views("v_"), "adamw_small")
    for idx, n in enumerate(SMALL_NAMES):
        results[n] = [_from_view(n, part[idx], weights[n].shape) for part in (g_views,) + tuple(res_small)]
    shard_cols = meta_tokens.shape[1]
    g_meta = lax.dynamic_slice_in_dim(g_meta_all, me * shard_cols, shard_cols, axis=1)
    results["meta_tokens"] = _adamw(g_meta[None], meta_tokens, m_meta_tokens, v_meta_tokens, n_meta, "adamw_meta")

    out = [loss_row[0, 0], d_x[None]]
    for part in range(4):
        for n in WEIGHT_NAMES:
            out.append(results[n][part].reshape(weights[n].shape))
    return tuple(out)
```

```python
import jax
import jax.numpy as jnp
from jax import lax
from jax.experimental import pallas as pl
from jax.experimental.pallas import tpu as pltpu

F32 = jnp.float32
BF16 = jnp.bfloat16

N_DEV = 8
D_MODEL = 1024
D_SSM = 512
SSM_GROUP = 16
SSM_STATE = 64
SSM_GROUPS = 32
POOL_GROUPS = 4
POOL_DIM = 128
COL_U = 128
COL_S = 512
N_COL = D_SSM // COL_U
GROUPS_PER_COL = COL_U // SSM_GROUP
D_FF = 2816
FF_SHARD = D_FF // N_DEV
FF_TILE = D_FF // 2
TM = 256
WGRAD_STEPS = 2
HEAD = TM
SUBLANES = 8
SCAN_UNROLL = 4
POOL_HALO = 16
EPS = 1e-6
STEP_FLOOR = -1e-4
VMEM_LIMIT = 60 * 1024 * 1024

ADAM_LR = 0.001
ADAM_B1 = 0.9
ADAM_B2 = 0.999
ADAM_EPS = 1e-08
ADAM_WD = 0.01
ADAM_STEP = 10

MESH_ID = pl.DeviceIdType.MESH
ANY = pl.BlockSpec(memory_space=pl.ANY)

SMALL_NAMES = ("norm1_g", "ssm_lambda_re", "ssm_lambda_im", "ssm_log_step", "ssm_b_re", "ssm_b_im",
               "ssm_c_re", "ssm_c_im", "ssm_d", "ssm_glu_w", "ssm_glu_b", "ssm_norm_g", "pool_w",
               "pool_scale", "pool_norm_g", "norm2_g", "final_norm_g")
WEIGHT_NAMES = ("meta_tokens", "norm1_g", "w_in", "ssm_lambda_re", "ssm_lambda_im", "ssm_log_step",
                "ssm_b_re", "ssm_b_im", "ssm_c_re", "ssm_c_im", "ssm_d", "ssm_glu_w", "ssm_glu_b",
                "ssm_norm_g", "pool_w", "pool_scale", "pool_norm_g", "w_out", "norm2_g", "w_gate",
                "w_up", "w_down", "final_norm_g")
LANES = 128
PACK_LANES = LANES
PACK_UNIT = 8 * PACK_LANES


def _dot(a, b):
    return jnp.dot(a.astype(BF16), b.astype(BF16), preferred_element_type=F32)


def _dot_nt(a, b):
    return lax.dot_general(a.astype(BF16), b.astype(BF16), (((1,), (1,)), ((), ())), preferred_element_type=F32)


def _dot_tn(a, b):
    return lax.dot_general(a.astype(BF16), b.astype(BF16), (((0,), (0,)), ((), ())), preferred_element_type=F32)


def _sigmoid(x):
    return 1.0 / (1.0 + jnp.exp(-x))


def _rstd(x):
    return lax.rsqrt(jnp.mean(x * x, axis=-1, keepdims=True) + EPS)


def _rms_bwd(dy, xhat, r, g):
    dxh = dy * g
    dx = r * (dxh - xhat * jnp.mean(dxh * xhat, axis=-1, keepdims=True))
    return dx, jnp.sum(dy * xhat, axis=0, keepdims=True)


def _params(sem, vmem=None):
    return pltpu.CompilerParams(dimension_semantics=sem, vmem_limit_bytes=vmem)


def _const(shape):
    return pl.BlockSpec(shape, lambda *_: (0,) * len(shape))


def _xrow(i):
    return (jnp.maximum(i - 1, 0), 0)


HBM = pl.BlockSpec(memory_space=pltpu.HBM)
SEM = pl.BlockSpec(memory_space=pltpu.SEMAPHORE)
EFFECT = pltpu.SideEffectType.DATAFLOW_SIDE_EFFECTING
ALL_PEERS = tuple(range(1, N_DEV))
SIBLING = 1
CHIP_PEERS = (2, 4, 6)


def _me():
    return 4 * lax.axis_index("x") + 2 * lax.axis_index("y") + lax.axis_index("c")


def _peer(k):
    x, y, c = lax.axis_index("x"), lax.axis_index("y"), lax.axis_index("c")
    px = 1 - x if k & 4 else x
    py = 1 - y if k & 2 else y
    pc = 1 - c if k & 1 else c
    return (px, py, pc), 4 * px + 2 * py + pc


def _landing(arr, scatter):
    if scatter:
        own = lax.dynamic_index_in_dim(arr, _me(), 0, keepdims=False)
    else:
        own = arr
    return lax.dynamic_update_index_in_dim(lax.empty((N_DEV,) + own.shape, arr.dtype), own, _me(), 0)


def _push_copies(peers, scatter):
    n_arr = len(scatter)

    def make(refs, send_sems, recv_sems):
        copies = []
        for a in range(n_arr):
            for i, k in enumerate(peers):
                peer_id, peer = _peer(k)
                sem = a * len(peers) + i
                copies.append(pltpu.make_async_remote_copy(
                    src_ref=refs[a].at[peer] if scatter[a] else refs[a], dst_ref=refs[n_arr + a].at[_me()],
                    send_sem=send_sems.at[sem], recv_sem=recv_sems.at[sem],
                    device_id=peer_id, device_id_type=MESH_ID))
        return copies
    return make


def _forward_copies(n_arr):
    def make(refs, send_sems, recv_sems):
        copies = []
        sibling_id, _ = _peer(SIBLING)
        for a in range(n_arr):
            for i, k in enumerate(CHIP_PEERS):
                slot = refs[a].at[_peer(k)[1]]
                sem = a * len(CHIP_PEERS) + i
                copies.append(pltpu.make_async_remote_copy(
                    src_ref=slot, dst_ref=slot, send_sem=send_sems.at[sem], recv_sem=recv_sems.at[sem],
                    device_id=sibling_id, device_id_type=MESH_ID))
        return copies
    return make


def _split_start(operands, make, n_sem, name):
    n_op = len(operands)

    def body(*refs):
        for cp in make(refs[:n_op], refs[n_op], refs[n_op + 1]):
            cp.start()
        refs[-1][...] = jnp.zeros(refs[-1].shape, F32)

    out = pl.pallas_call(
        body, name=name,
        out_shape=(pltpu.SemaphoreType.DMA((n_sem,)), pltpu.SemaphoreType.DMA((n_sem,)),
                   *[pltpu.HBM(t.shape, t.dtype) for t in operands], jax.ShapeDtypeStruct((8, PACK_LANES), F32)),
        in_specs=[HBM] * n_op, out_specs=(SEM, SEM, *[HBM] * n_op, pl.BlockSpec(memory_space=pltpu.VMEM)),
        input_output_aliases={i: 2 + i for i in range(n_op)},
        compiler_params=pltpu.CompilerParams(has_side_effects=EFFECT),
    )(*[pltpu.with_memory_space_constraint(t, pltpu.HBM) for t in operands])
    return out[:-1], out[-1]


def _split_wait(started, make, after, name):
    send_sems, recv_sems, thru = started[0], started[1], started[2:]
    n_op = len(thru)

    def body(*refs):
        for cp in make(refs[:n_op], refs[n_op], refs[n_op + 1]):
            cp.wait_send()
            cp.wait_recv()
        refs[-1][...] = jnp.zeros(refs[-1].shape, F32)

    out = pl.pallas_call(
        body, name=name,
        out_shape=(*[pltpu.HBM(t.shape, t.dtype) for t in thru], jax.ShapeDtypeStruct((8, PACK_LANES), F32)),
        in_specs=[HBM] * n_op + [SEM, SEM, ANY], out_specs=(*[HBM] * n_op, pl.BlockSpec(memory_space=pltpu.VMEM)),
        input_output_aliases={i: i for i in range(n_op)},
        compiler_params=pltpu.CompilerParams(has_side_effects=EFFECT),
    )(*thru, send_sems, recv_sems, after)
    return out[:-1], out[-1]


def _adamw_math(g, w, m, v):
    nm = ADAM_B1 * m + (1.0 - ADAM_B1) * g
    nv = ADAM_B2 * v + (1.0 - ADAM_B2) * (g * g)
    m_hat = nm / (1.0 - ADAM_B1 ** ADAM_STEP)
    v_hat = nv / (1.0 - ADAM_B2 ** ADAM_STEP)
    return -ADAM_LR * (m_hat / (jnp.sqrt(v_hat) + ADAM_EPS) + ADAM_WD * w), nm, nv


def _sum_slots(s_ref):
    g = s_ref[0].astype(F32)
    for s in range(1, s_ref.shape[0]):
        g = g + s_ref[s].astype(F32)
    return g


def _adamw(slots, w, m, v, tile_rows, name):
    n, rows, cols = slots.shape

    def body(s_ref, w_ref, m_ref, v_ref, g_ref, d_ref, nm_ref, nv_ref):
        g = _sum_slots(s_ref)
        g_ref[...] = g
        d_ref[...], nm_ref[...], nv_ref[...] = _adamw_math(g, w_ref[...], m_ref[...], v_ref[...])

    tile = pl.BlockSpec((tile_rows, cols), lambda i: (i, 0))
    return pl.pallas_call(
        body, name=name, grid=(rows // tile_rows,),
        in_specs=[pl.BlockSpec((n, tile_rows, cols), lambda i: (0, i, 0)), tile, tile, tile],
        out_specs=[tile] * 4, out_shape=[jax.ShapeDtypeStruct((rows, cols), F32)] * 4,
        compiler_params=_params(("parallel",), VMEM_LIMIT),
    )(slots, w, m, v)


def _adamw_part(slots, part, w, m, v, name):
    n, _, rows, cols = slots.shape
    tile_cols = 256

    def body(s_ref, w_ref, m_ref, v_ref, g_ref, d_ref, nm_ref, nv_ref):
        g = _sum_slots(s_ref)
        g_ref[...] = g
        d_ref[...], nm_ref[...], nv_ref[...] = _adamw_math(g, w_ref[...], m_ref[...], v_ref[...])

    tile = pl.BlockSpec((rows, tile_cols), lambda i: (0, i))
    return pl.pallas_call(
        body, name=name, grid=(cols // tile_cols,),
        in_specs=[pl.BlockSpec((n, None, rows, tile_cols), lambda i: (0, part, 0, i)), tile, tile, tile],
        out_specs=[tile] * 4, out_shape=[jax.ShapeDtypeStruct((rows, cols), F32)] * 4,
        compiler_params=_params(("parallel",), VMEM_LIMIT),
    )(slots, w, m, v)


def _reduce_slots(slot_arrays, name):
    def body(*refs):
        n_arr = len(refs) // 2
        for s_ref, o_ref in zip(refs[:n_arr], refs[n_arr:]):
            o_ref[...] = _sum_slots(s_ref)
    return pl.pallas_call(
        body, name=name, out_shape=[jax.ShapeDtypeStruct(s.shape[1:], F32) for s in slot_arrays],
        compiler_params=_params(None, VMEM_LIMIT))(*slot_arrays)


def _adamw_many(grads, ws, ms, vs, name):
    n = len(grads)

    def body(*refs):
        ins, outs = refs[:4 * n], refs[4 * n:]
        for i in range(n):
            g, w, m, v = (ins[j * n + i][...] for j in range(4))
            outs[i][...], outs[n + i][...], outs[2 * n + i][...] = _adamw_math(g, w, m, v)

    out = pl.pallas_call(
        body, name=name, out_shape=[jax.ShapeDtypeStruct(w.shape, F32) for w in ws] * 3,
        compiler_params=_params(None, VMEM_LIMIT))(*grads, *ws, *ms, *vs)
    return out[:n], out[n:2 * n], out[2 * n:]


def _disc_a(lam_re, lam_im, log_step):
    lr = jnp.minimum(lam_re, STEP_FLOOR)
    step = jnp.exp(log_step)
    mag = jnp.exp(lr * step)
    ang = lam_im * step
    abr = mag * jnp.cos(ang)
    abi = mag * jnp.sin(ang)
    nr = abr - 1.0
    den = lr * lr + lam_im * lam_im
    cr = (nr * lr + abi * lam_im) / den
    ci = (abi * lr - nr * lam_im) / den
    return abr, abi, cr, ci


def _disc_b(cr, ci, b_re, b_im):
    return cr * b_re - ci * b_im, cr * b_im + ci * b_re


def _s5_disc_a(lam_re, lam_im, log_step):
    def body(lr_ref, li_ref, ls_ref, *outs):
        for o, val in zip(outs, _disc_a(lr_ref[...], li_ref[...], ls_ref[...])):
            o[...] = val
    return pl.pallas_call(body, name="s5_disc_a", out_shape=[jax.ShapeDtypeStruct(lam_re.shape, F32)] * 4)(
        lam_re, lam_im, log_step)


def _s5_disc_a_bwd(lam_re, lam_im, log_step, cts):
    def body(lr_ref, li_ref, ls_ref, c0, c1, c2, c3, dlr_ref, dli_ref, dls_ref):
        _, vjp = jax.vjp(_disc_a, lr_ref[...], li_ref[...], ls_ref[...])
        dlr, dli, dls = vjp((c0[...], c1[...], c2[...], c3[...]))
        dlr_ref[...] = dlr
        dli_ref[...] = dli
        dls_ref[...] = dls
    return pl.pallas_call(
        body, name="s5_disc_a_bwd",
        out_shape=[jax.ShapeDtypeStruct(lam_re.shape, F32), jax.ShapeDtypeStruct(lam_re.shape, F32),
                   jax.ShapeDtypeStruct(log_step.shape, F32)])(lam_re, lam_im, log_step, *cts)


def _s5_disc_b(cr, ci, b_re, b_im):
    def body(cr_ref, ci_ref, br_ref, bi_ref, o_re, o_im):
        o_re[...], o_im[...] = _disc_b(cr_ref[...], ci_ref[...], br_ref[...], bi_ref[...])
    return pl.pallas_call(body, name="s5_disc_b", out_shape=[jax.ShapeDtypeStruct(b_re.shape, F32)] * 2)(
        cr, ci, b_re, b_im)


def _s5_disc_b_bwd(cr, ci, b_re, b_im, d_re, d_im):
    def body(cr_ref, ci_ref, br_ref, bi_ref, dr_ref, di_ref, dcr_ref, dci_ref, dbr_ref, dbi_ref):
        _, vjp = jax.vjp(_disc_b, cr_ref[...], ci_ref[...], br_ref[...], bi_ref[...])
        dcr_ref[...], dci_ref[...], dbr_ref[...], dbi_ref[...] = vjp((dr_ref[...], di_ref[...]))
    return pl.pallas_call(
        body, name="s5_disc_b_bwd",
        out_shape=[jax.ShapeDtypeStruct(cr.shape, F32)] * 2 + [jax.ShapeDtypeStruct(b_re.shape, F32)] * 2)(
            cr, ci, b_re, b_im, d_re, d_im)


def _cmul(ar, ai, br, bi):
    return ar * br - ai * bi, ar * bi + ai * br


def _cpow(ar, ai, n):
    rr, ri = jnp.ones_like(ar), jnp.zeros_like(ai)
    while n:
        if n & 1:
            rr, ri = _cmul(rr, ri, ar, ai)
        n >>= 1
        if n:
            ar, ai = _cmul(ar, ai, ar, ai)
    return rr, ri


def _tile_rows(i):
    if isinstance(i, int):
        return pl.ds(i * SUBLANES, SUBLANES)
    return pl.ds(pl.multiple_of(i * SUBLANES, SUBLANES), SUBLANES)


def _segment_scan(z_re, z_im, ar, ai, lseg, reverse, visit=None):
    shape = (SUBLANES, z_re.shape[1])
    half = lseg // 2
    arb = jnp.broadcast_to(ar, shape)
    aib = jnp.broadcast_to(ai, shape)
    zero = jnp.zeros(shape, F32)
    row = lax.broadcasted_iota(jnp.int32, shape, 0)

    def tiles(k):
        return (lseg - 1 - k, half - 1 - k) if reverse else (k, half + k)

    def advance(tile, sr, si):
        rows = _tile_rows(tile)
        nr, ni = _cmul(arb, aib, sr, si)
        return rows, nr + z_re[rows, :], ni + z_im[rows, :]

    def first_pass(k, carry):
        ta, tb = tiles(k)
        return advance(ta, carry[0], carry[1])[1:] + advance(tb, carry[2], carry[3])[1:]

    def unrolled(step):
        def body(it, carry):
            for j in range(SCAN_UNROLL):
                carry = step(it * SCAN_UNROLL + j, carry)
            return carry
        return body

    n_iter = half // SCAN_UNROLL
    fa_r, fa_i, fb_r, fb_i = lax.fori_loop(0, n_iter, unrolled(first_pass), (zero,) * 4)
    hr, hi = _cpow(arb, aib, half)
    pr, pi = _cmul(hr, hi, hr, hi)
    fr, fi = _cmul(hr, hi, fa_r, fa_i)
    fr, fi = fr + fb_r, fi + fb_i
    cr, ci = zero, zero
    for _ in range(SUBLANES - 1):
        tr, ti = _cmul(pr, pi, cr, ci)
        tr, ti = tr + fr, ti + fi
        if reverse:
            cr = jnp.where(row == SUBLANES - 1, 0.0, pltpu.roll(tr, SUBLANES - 1, 0))
            ci = jnp.where(row == SUBLANES - 1, 0.0, pltpu.roll(ti, SUBLANES - 1, 0))
        else:
            cr = jnp.where(row == 0, 0.0, pltpu.roll(tr, 1, 0))
            ci = jnp.where(row == 0, 0.0, pltpu.roll(ti, 1, 0))

    br, bi = _cmul(hr, hi, cr, ci)
    br, bi = br + fa_r, bi + fa_i

    def second_pass(k, carry, b_is_tile0=False):
        states, acc = list(carry[:4]), carry[4]
        for chain, tile in enumerate(tiles(k)):
            rows, nr, ni = advance(tile, states[2 * chain], states[2 * chain + 1])
            z_re[rows, :] = nr
            z_im[rows, :] = ni
            states[2 * chain], states[2 * chain + 1] = nr, ni
            if visit is not None:
                acc = visit(tile, nr, ni, acc, chain == 1 and b_is_tile0)
        return (*states, acc)

    acc0 = (zero, zero) if visit is not None else 0
    carry = lax.fori_loop(0, n_iter - 1, unrolled(second_pass), (cr, ci, br, bi, acc0))
    for k in range(half - SCAN_UNROLL, half - 1):
        carry = second_pass(k, carry)
    return second_pass(half - 1, carry, b_is_tile0=reverse)[4]


def _gelu(y):
    c = 0.7978845608028654
    return 0.5 * y * (1.0 + jnp.tanh(c * (y + 0.044715 * y * y * y)))


def _gelu_grad(y):
    c = 0.7978845608028654
    th = jnp.tanh(c * (y + 0.044715 * y * y * y))
    return 0.5 * (1.0 + th) + 0.5 * y * (1.0 - th * th) * c * (1.0 + 3.0 * 0.044715 * y * y)


def _s5_specs(lp):
    col_u = pl.BlockSpec((lp, COL_U), lambda j: (0, j))
    row_u = pl.BlockSpec((1, COL_U), lambda j: (0, j))
    row_s = pl.BlockSpec((1, COL_S), lambda j: (0, j))
    bc_blk = pl.BlockSpec((GROUPS_PER_COL, SSM_GROUP, SSM_STATE), lambda j: (j, 0, 0))
    glu_blk = pl.BlockSpec((GROUPS_PER_COL, SSM_GROUP, SSM_GROUP), lambda j: (j, 0, 0))
    return col_u, row_u, row_s, bc_blk, glu_blk


def _s5_block_diag_scratch():
    return ([pltpu.VMEM((COL_U, COL_S), BF16)] * 4 + [pltpu.VMEM((COL_U, COL_U), BF16)]
            + [pltpu.VMEM((COL_U, COL_S), F32)])


def _fill_block_diag(bd_ref, blocks_ref, stage):
    r, c = blocks_ref.shape[1:]
    stage[...] = jnp.zeros(stage.shape, F32)
    for gl in range(GROUPS_PER_COL):
        stage[pl.ds(gl * r, r), pl.ds(gl * c, c)] = blocks_ref[gl]
    bd_ref[...] = stage[:, :GROUPS_PER_COL * c].astype(BF16)


def _take_block_diag(out_ref, mat):
    r, c = out_ref.shape[1:]
    for gl in range(GROUPS_PER_COL):
        out_ref[gl] = mat[gl * r:(gl + 1) * r, gl * c:(gl + 1) * c]


def _s5_fill_states(u_ref, bre_ref, bim_ref, ar_ref, ai_ref, s_re, s_im, lseg, n_chunks, chunk):
    def fill(cidx, carry):
        rows = pl.ds(pl.multiple_of(cidx * chunk, SUBLANES), chunk)
        ub = u_ref[rows, :].astype(BF16)
        s_re[rows, :] = jnp.dot(ub, bre_ref[...], preferred_element_type=F32)
        s_im[rows, :] = jnp.dot(ub, bim_ref[...], preferred_element_type=F32)
        return carry
    lax.fori_loop(0, n_chunks, fill, 0)
    _segment_scan(s_re, s_im, ar_ref[...], ai_ref[...], lseg, reverse=False)


def _s5_forward(u_p, ar, ai, bbr, bbi, c_re, c_im, d_row, glu_w, glub_row):
    lp = u_p.shape[0]
    lseg = lp // SUBLANES
    chunk, n_chunks = 4 * lseg, SUBLANES // 4

    def body(u_ref, ar_ref, ai_ref, bbr_ref, bbi_ref, cr_ref, ci_ref, d_ref, gw_ref, glub_ref,
             ys_ref, s_re, s_im, bre_ref, bim_ref, cre_ref, cim_ref, glu_ref, stage):
        for bd, blocks in ((bre_ref, bbr_ref), (bim_ref, bbi_ref), (cre_ref, cr_ref), (cim_ref, ci_ref),
                           (glu_ref, gw_ref)):
            _fill_block_diag(bd, blocks, stage)
        _s5_fill_states(u_ref, bre_ref, bim_ref, ar_ref, ai_ref, s_re, s_im, lseg, n_chunks, chunk)

        def emit(cidx, carry):
            rows = pl.ds(pl.multiple_of(cidx * chunk, SUBLANES), chunk)
            y = (_dot_nt(s_re[rows, :], cre_ref[...]) - _dot_nt(s_im[rows, :], cim_ref[...])
                 + d_ref[...] * u_ref[rows, :])
            g = _gelu(y)
            gate = _dot(g, glu_ref[...]) + glub_ref[...]
            ys_ref[rows, :] = g * _sigmoid(gate)
            return carry
        lax.fori_loop(0, n_chunks, emit, 0)

    col_u, row_u, row_s, bc_blk, glu_blk = _s5_specs(lp)
    return pl.pallas_call(
        body, name="s5_forward", grid=(N_COL,),
        in_specs=[col_u, row_s, row_s, bc_blk, bc_blk, bc_blk, bc_blk, row_u, glu_blk, row_u],
        out_specs=col_u, out_shape=jax.ShapeDtypeStruct((lp, D_SSM), F32),
        scratch_shapes=[pltpu.VMEM((lp, COL_S), F32), pltpu.VMEM((lp, COL_S), F32)] + _s5_block_diag_scratch(),
        compiler_params=_params(("arbitrary",), VMEM_LIMIT),
    )(u_p, ar, ai, bbr, bbi, c_re, c_im, d_row, glu_w, glub_row)


def _s5_backward(u_p, dys_p, ar, ai, bbr, bbi, c_re, c_im, d_row, glu_w, glub_row):
    lp = u_p.shape[0]
    lseg = lp // SUBLANES
    chunk, n_chunks = 4 * lseg, SUBLANES // 4

    def body(u_ref, dys_ref, ar_ref, ai_ref, bbr_ref, bbi_ref, cr_ref, ci_ref, d_ref, gw_ref, glub_ref,
             du_ref, dar_ref, dai_ref, dbbr_ref, dbbi_ref, dcr_ref, dci_ref, dd_ref, dgw_ref, dglub_ref,
             s_re, s_im, q_re, q_im, bre_ref, bim_ref, cre_ref, cim_ref, glu_ref, stage,
             dbre_ref, dbim_ref, dcre_ref, dcim_ref, dglu_ref):
        for bd, blocks in ((bre_ref, bbr_ref), (bim_ref, bbi_ref), (cre_ref, cr_ref), (cim_ref, ci_ref),
                           (glu_ref, gw_ref)):
            _fill_block_diag(bd, blocks, stage)
        _s5_fill_states(u_ref, bre_ref, bim_ref, ar_ref, ai_ref, s_re, s_im, lseg, n_chunks, chunk)
        for ref in (dcre_ref, dcim_ref, dd_ref, dglu_ref, dglub_ref, dbre_ref, dbim_ref):
            ref[...] = jnp.zeros(ref.shape, F32)

        def mixer_bwd(cidx, carry):
            rows = pl.ds(pl.multiple_of(cidx * chunk, SUBLANES), chunk)
            u = u_ref[rows, :]
            sr, si = s_re[rows, :], s_im[rows, :]
            y = _dot_nt(sr, cre_ref[...]) - _dot_nt(si, cim_ref[...]) + d_ref[...] * u
            g = _gelu(y)
            sg = _sigmoid(_dot(g, glu_ref[...]) + glub_ref[...])
            dout = dys_ref[rows, :]
            dgate = dout * g * sg * (1.0 - sg)
            dy = (dout * sg + _dot_nt(dgate, glu_ref[...])) * _gelu_grad(y)
            dglu_ref[...] += _dot_tn(g, dgate)
            dglub_ref[...] += jnp.sum(dgate, axis=0, keepdims=True)
            dd_ref[...] += jnp.sum(dy * u, axis=0, keepdims=True)
            dcre_ref[...] += _dot_tn(dy, sr)
            dcim_ref[...] -= _dot_tn(dy, si)
            q_re[rows, :] = _dot(dy, cre_ref[...])
            q_im[rows, :] = -_dot(dy, cim_ref[...])
            du_ref[rows, :] = d_ref[...] * dy
            return carry
        lax.fori_loop(0, n_chunks, mixer_bwd, 0)

        row = lax.broadcasted_iota(jnp.int32, (SUBLANES, COL_S), 0)

        def visit(i, qr, qi, acc, is_tile0):
            if is_tile0:
                prev = _tile_rows(lseg - 1)
                pr = jnp.where(row == 0, 0.0, pltpu.roll(s_re[prev, :], 1, 0))
                pi = jnp.where(row == 0, 0.0, pltpu.roll(s_im[prev, :], 1, 0))
            else:
                prev = _tile_rows(i - 1)
                pr, pi = s_re[prev, :], s_im[prev, :]
            return acc[0] + qr * pr + qi * pi, acc[1] + qi * pr - qr * pi

        dar, dai = _segment_scan(q_re, q_im, ar_ref[...], -ai_ref[...], lseg, reverse=True, visit=visit)
        dar_ref[...] = jnp.sum(dar, axis=0, keepdims=True)
        dai_ref[...] = jnp.sum(dai, axis=0, keepdims=True)

        def input_bwd(cidx, carry):
            rows = pl.ds(pl.multiple_of(cidx * chunk, SUBLANES), chunk)
            qr, qi = q_re[rows, :], q_im[rows, :]
            u = u_ref[rows, :]
            du_ref[rows, :] += _dot_nt(qr, bre_ref[...]) + _dot_nt(qi, bim_ref[...])
            dbre_ref[...] += _dot_tn(u, qr)
            dbim_ref[...] += _dot_tn(u, qi)
            return carry
        lax.fori_loop(0, n_chunks, input_bwd, 0)
        for out, acc in ((dbbr_ref, dbre_ref), (dbbi_ref, dbim_ref), (dcr_ref, dcre_ref), (dci_ref, dcim_ref),
                         (dgw_ref, dglu_ref)):
            _take_block_diag(out, acc[...])

    col_u, row_u, row_s, bc_blk, glu_blk = _s5_specs(lp)
    group_mats = jax.ShapeDtypeStruct((SSM_GROUPS, SSM_GROUP, SSM_STATE), F32)
    return pl.pallas_call(
        body, name="s5_backward", grid=(N_COL,),
        in_specs=[col_u, col_u, row_s, row_s, bc_blk, bc_blk, bc_blk, bc_blk, row_u, glu_blk, row_u],
        out_specs=[col_u, row_s, row_s, bc_blk, bc_blk, bc_blk, bc_blk, row_u, glu_blk, row_u],
        out_shape=[jax.ShapeDtypeStruct((lp, D_SSM), F32),
                   jax.ShapeDtypeStruct((1, N_COL * COL_S), F32), jax.ShapeDtypeStruct((1, N_COL * COL_S), F32),
                   group_mats, group_mats, group_mats, group_mats, jax.ShapeDtypeStruct((1, D_SSM), F32),
                   jax.ShapeDtypeStruct((SSM_GROUPS, SSM_GROUP, SSM_GROUP), F32), jax.ShapeDtypeStruct((1, D_SSM), F32)],
        scratch_shapes=([pltpu.VMEM((lp, COL_S), F32)] * 4 + _s5_block_diag_scratch()
                        + [pltpu.VMEM((COL_U, COL_S), F32)] * 4 + [pltpu.VMEM((COL_U, COL_U), F32)]),
        compiler_params=_params(("arbitrary",), VMEM_LIMIT),
    )(u_p, dys_p, ar, ai, bbr, bbi, c_re, c_im, d_row, glu_w, glub_row)


def _window_sum(ext, group, leading):
    n = ext.shape[0]
    s = ext
    for j in range(POOL_GROUPS):
        shift = n - (1 << j) if leading else 1 << j
        s = jnp.where(j <= group, s + pltpu.roll(s, shift, 0), s)
    return s


def _pool_inv_count(tile, window, first_row):
    t = tile * TM + lax.broadcasted_iota(jnp.int32, (TM, 1), 0) - first_row
    return 1.0 / jnp.clip(t + 1, 1, window).astype(F32)


def _pool_specs(lp):
    col = pl.BlockSpec((lp, POOL_DIM), lambda k: (0, k))
    mat = pl.BlockSpec((None, POOL_DIM, POOL_DIM), lambda k: (k, 0, 0))
    row = pl.BlockSpec((None, 1, POOL_DIM), lambda k: (k, 0, 0))
    return col, mat, row


def _pool_forward(v, pool_w, pool_scale, first_row):
    lp = v.shape[0]
    n_tiles = lp // TM

    def body(v_ref, w_ref, sc_ref, yp_ref, vpad):
        group = pl.program_id(0)
        window = jnp.left_shift(2, group)
        vpad[pl.ds(0, POOL_HALO), :] = jnp.zeros((POOL_HALO, POOL_DIM), F32)
        vpad[pl.ds(POOL_HALO, lp), :] = v_ref[...]

        def tile(j, carry):
            start = pl.multiple_of(j * TM, TM)
            ext = vpad[pl.ds(start, TM + POOL_HALO), :]
            sums = _window_sum(ext, group, leading=False)[POOL_HALO:, :]
            p = sums * _pool_inv_count(j, window, first_row) - ext[POOL_HALO:, :]
            yp_ref[pl.ds(start, TM), :] = _dot(p, w_ref[...]) * sc_ref[...]
            return carry
        lax.fori_loop(0, n_tiles, tile, 0)

    col, mat, row = _pool_specs(lp)
    return pl.pallas_call(
        body, name="pool_forward", grid=(POOL_GROUPS,),
        in_specs=[col, mat, row], out_specs=col, out_shape=jax.ShapeDtypeStruct((lp, D_SSM), F32),
        scratch_shapes=[pltpu.VMEM((lp + POOL_HALO, POOL_DIM), F32)],
        compiler_params=_params(("arbitrary",), VMEM_LIMIT),
    )(v, pool_w, pool_scale)


def _pool_backward(v, dyp, pool_w, pool_scale, first_row):
    lp = v.shape[0]
    n_tiles = lp // TM

    def body(v_ref, dyp_ref, w_ref, sc_ref, dv_ref, dw_ref, dsc_ref, vpad, gpad):
        group = pl.program_id(0)
        window = jnp.left_shift(2, group)
        vpad[pl.ds(0, POOL_HALO), :] = jnp.zeros((POOL_HALO, POOL_DIM), F32)
        vpad[pl.ds(POOL_HALO, lp), :] = v_ref[...]
        gpad[pl.ds(lp, POOL_HALO), :] = jnp.zeros((POOL_HALO, POOL_DIM), F32)
        dw_ref[...] = jnp.zeros(dw_ref.shape, F32)
        dsc_ref[...] = jnp.zeros(dsc_ref.shape, F32)

        def linear_bwd(j, carry):
            start = pl.multiple_of(j * TM, TM)
            ext = vpad[pl.ds(start, TM + POOL_HALO), :]
            inv = _pool_inv_count(j, window, first_row)
            p = _window_sum(ext, group, leading=False)[POOL_HALO:, :] * inv - ext[POOL_HALO:, :]
            z = _dot(p, w_ref[...])
            dyp_t = dyp_ref[pl.ds(start, TM), :]
            dz = dyp_t * sc_ref[...]
            dsc_ref[...] += jnp.sum(dyp_t * z, axis=0, keepdims=True)
            dw_ref[...] += _dot_tn(p, dz)
            dp = _dot_nt(dz, w_ref[...])
            gpad[pl.ds(start, TM), :] = dp * inv
            dv_ref[pl.ds(start, TM), :] = -dp
            return carry
        lax.fori_loop(0, n_tiles, linear_bwd, 0)

        def window_bwd(j, carry):
            start = pl.multiple_of(j * TM, TM)
            ext = gpad[pl.ds(start, TM + POOL_HALO), :]
            dv_ref[pl.ds(start, TM), :] += _window_sum(ext, group, leading=True)[:TM, :]
            return carry
        lax.fori_loop(0, n_tiles, window_bwd, 0)

    col, mat, row = _pool_specs(lp)
    return pl.pallas_call(
        body, name="pool_backward", grid=(POOL_GROUPS,),
        in_specs=[col, col, mat, row], out_specs=[col, mat, row],
        out_shape=[jax.ShapeDtypeStruct((lp, D_SSM), F32),
                   jax.ShapeDtypeStruct((POOL_GROUPS, POOL_DIM, POOL_DIM), F32),
                   jax.ShapeDtypeStruct((POOL_GROUPS, 1, POOL_DIM), F32)],
        scratch_shapes=[pltpu.VMEM((lp + POOL_HALO, POOL_DIM), F32)] * 2,
        compiler_params=_params(("arbitrary",), VMEM_LIMIT),
    )(v, dyp, pool_w, pool_scale)


def _row_specs():
    head = _const((HEAD, D_MODEL))
    xrow = pl.BlockSpec((TM, D_MODEL), _xrow)
    full = pl.BlockSpec((TM, D_MODEL), lambda i: (i, 0))
    half = pl.BlockSpec((TM, D_SSM), lambda i: (i, 0))
    return head, xrow, full, half


def _in_proj(head, x, g1, w_in):
    n_tiles = (HEAD + x.shape[0]) // TM
    lp = n_tiles * TM

    def body(head_ref, x_ref, g_ref, w_ref, u_ref, v_ref, n1t_ref):
        h0 = jnp.where(pl.program_id(0) == 0, head_ref[...], x_ref[...])
        n1 = h0 * _rstd(h0) * g_ref[...]
        n1t_ref[...] = n1.T.astype(BF16)
        proj = _dot(n1, w_ref[...])
        u_ref[...] = proj[:, :D_SSM]
        v_ref[...] = proj[:, D_SSM:]

    head_s, xrow, _, half = _row_specs()
    return pl.pallas_call(
        body, name="in_proj", grid=(n_tiles,),
        in_specs=[head_s, xrow, _const((1, D_MODEL)), _const((D_MODEL, D_MODEL))],
        out_specs=[half, half, pl.BlockSpec((D_MODEL, TM), lambda i: (0, i))],
        out_shape=[jax.ShapeDtypeStruct((lp, D_SSM), F32)] * 2 + [jax.ShapeDtypeStruct((D_MODEL, lp), BF16)],
        compiler_params=_params(("parallel",), VMEM_LIMIT),
    )(head, x, g1, w_in)


def _proj_wgrad(a_t, bs, name):
    lp = a_t.shape[1]
    rows = lp // WGRAD_STEPS
    widths = [b.shape[1] for b in bs]

    def body(*refs):
        a_ref, b_refs, out_ref, acc = refs[0], refs[1:1 + len(bs)], refs[-2], refs[-1]
        k = pl.program_id(0)

        @pl.when(k == 0)
        def _():
            acc[...] = jnp.zeros(acc.shape, F32)

        lo = 0
        for b_ref, width in zip(b_refs, widths):
            acc[:, pl.ds(lo, width)] += jnp.dot(a_ref[...], b_ref[...].astype(BF16), preferred_element_type=F32)
            lo += width

        @pl.when(k == pl.num_programs(0) - 1)
        def _():
            out_ref[...] = acc[...].astype(BF16)

    total = sum(widths)
    return pl.pallas_call(
        body, name=name, grid=(WGRAD_STEPS,),
        in_specs=[pl.BlockSpec((D_MODEL, rows), lambda k: (0, k))]
        + [pl.BlockSpec((rows, width), lambda k: (k, 0)) for width in widths],
        out_specs=_const((D_MODEL, total)), out_shape=jax.ShapeDtypeStruct((D_MODEL, total), BF16),
        scratch_shapes=[pltpu.VMEM((D_MODEL, total), F32)],
        compiler_params=_params(("arbitrary",), VMEM_LIMIT),
    )(a_t, *bs)


def _out_proj(head, x, ys, yp, gs, gp, w_out):
    lp = ys.shape[0]

    def body(head_ref, x_ref, ys_ref, yp_ref, gs_ref, gp_ref, w_ref, h1_ref, mixt_ref):
        h0 = jnp.where(pl.program_id(0) == 0, head_ref[...], x_ref[...])
        ys_t, yp_t = ys_ref[...], yp_ref[...]
        ms = ys_t * _rstd(ys_t) * gs_ref[...]
        mp = yp_t * _rstd(yp_t) * gp_ref[...]
        mixt_ref[pl.ds(0, D_SSM), :] = ms.T.astype(BF16)
        mixt_ref[pl.ds(D_SSM, D_SSM), :] = mp.T.astype(BF16)
        h1_ref[...] = h0 + _dot(ms, w_ref[pl.ds(0, D_SSM), :]) + _dot(mp, w_ref[pl.ds(D_SSM, D_SSM), :])

    head_s, xrow, full, half = _row_specs()
    return pl.pallas_call(
        body, name="out_proj", grid=(lp // TM,),
        in_specs=[head_s, xrow, half, half, _const((1, D_SSM)), _const((1, D_SSM)), _const((D_MODEL, D_MODEL))],
        out_specs=[full, pl.BlockSpec((D_MODEL, TM), lambda i: (0, i))],
        out_shape=[jax.ShapeDtypeStruct((lp, D_MODEL), F32), jax.ShapeDtypeStruct((D_MODEL, lp), BF16)],
        compiler_params=_params(("parallel",), VMEM_LIMIT),
    )(head, x, ys, yp, gs, gp, w_out)


def _load_weights(hbm_refs, vmem_refs, sems):
    @pl.when(pl.program_id(0) == 0)
    def _():
        copies = [pltpu.make_async_copy(h, v, sems.at[n]) for n, (h, v) in enumerate(zip(hbm_refs, vmem_refs))]
        for cp in copies:
            cp.start()
        for cp in copies:
            cp.wait()


def _ffn_scratch():
    return [pltpu.VMEM((D_FF, D_MODEL), BF16)] * 3 + [pltpu.SemaphoreType.DMA((3,))]


def _ff_tile(t):
    return pl.ds(t * FF_TILE, FF_TILE)


def _ffn_forward(h1, g2, wg_t, wu_t, wd):
    lp = h1.shape[0]

    def body(h1_ref, g_ref, wg_hbm, wu_hbm, wd_hbm, ab_ref, n2_ref, h2_ref, wg, wu, wdn, sems):
        _load_weights((wg_hbm, wu_hbm, wd_hbm), (wg, wu, wdn), sems)
        h1_t = h1_ref[...]
        n2 = (h1_t * _rstd(h1_t) * g_ref[...]).astype(BF16)
        n2_ref[...] = n2
        acc = h1_t
        for t in range(D_FF // FF_TILE):
            a = _dot_nt(n2, wg[_ff_tile(t), :])
            b = _dot_nt(n2, wu[_ff_tile(t), :])
            ab_ref[:, _ff_tile(t)] = a.astype(BF16)
            ab_ref[:, pl.ds(D_FF + t * FF_TILE, FF_TILE)] = b.astype(BF16)
            acc = acc + _dot(a * _sigmoid(a) * b, wdn[_ff_tile(t), :])
        h2_ref[...] = acc

    _, _, full, _ = _row_specs()
    wide = pl.BlockSpec((TM, 2 * D_FF), lambda i: (i, 0))
    half_width = pl.BlockSpec((TM, D_MODEL), lambda i: (i, 0))
    return pl.pallas_call(
        body, name="ffn_forward", grid=(lp // TM,),
        in_specs=[full, _const((1, D_MODEL)), ANY, ANY, ANY], out_specs=[wide, half_width, full],
        out_shape=[jax.ShapeDtypeStruct((lp, 2 * D_FF), BF16), jax.ShapeDtypeStruct((lp, D_MODEL), BF16),
                   jax.ShapeDtypeStruct((lp, D_MODEL), F32)],
        scratch_shapes=_ffn_scratch(), compiler_params=_params(("arbitrary",), VMEM_LIMIT),
    )(h1, g2, wg_t, wu_t, wd)


def _ffn_backward(h2, target, h1, ab, gf, g2, wg_t, wu_t, wd):
    lp = h1.shape[0]

    def body(h2_ref, t_ref, h1_ref, ab_ref, gf_ref, g2_ref, wg_hbm, wu_hbm, wd_hbm,
             dh1_ref, xt_ref, dh2_ref, loss_ref, dgf_ref, dg2_ref, wg, wu, wdn, sems):
        i = pl.program_id(0)
        _load_weights((wg_hbm, wu_hbm, wd_hbm), (wg, wu, wdn), sems)

        @pl.when(i == 0)
        def _():
            loss_ref[...] = jnp.zeros(loss_ref.shape, F32)
            dgf_ref[...] = jnp.zeros(dgf_ref.shape, F32)
            dg2_ref[...] = jnp.zeros(dg2_ref.shape, F32)

        h2_t = h2_ref[...]
        rf = _rstd(h2_t)
        xf = h2_t * rf
        diff = jnp.where(i == 0, 0.0, xf * gf_ref[...] - t_ref[...])
        loss_ref[...] += 0.5 * jnp.sum(diff * diff) / D_MODEL
        dh2, dgf = _rms_bwd(diff / D_MODEL, xf, rf, gf_ref[...])
        dgf_ref[...] += dgf
        dh2_b = dh2.astype(BF16)
        dh2_ref[...] = dh2_b

        dn2 = jnp.zeros((TM, D_MODEL), F32)
        for t in range(D_FF // FF_TILE):
            dff = _dot_nt(dh2_b, wdn[_ff_tile(t), :])
            a = ab_ref[:, _ff_tile(t)].astype(F32)
            b = ab_ref[:, pl.ds(D_FF + t * FF_TILE, FF_TILE)].astype(F32)
            sg = _sigmoid(a)
            silu = a * sg
            da = dff * b * sg * (1.0 + a * (1.0 - sg))
            db = dff * silu
            for part, val in enumerate((da, db, silu * b)):
                xt_ref[pl.ds(part * D_FF + t * FF_TILE, FF_TILE), :] = val.T.astype(BF16)
            dn2 = dn2 + _dot(da, wg[_ff_tile(t), :]) + _dot(db, wu[_ff_tile(t), :])

        h1_t = h1_ref[...]
        r2 = _rstd(h1_t)
        dx, dg2 = _rms_bwd(dn2, h1_t * r2, r2, g2_ref[...])
        dg2_ref[...] += dg2
        dh1_ref[...] = dh2 + dx

    _, xrow, full, _ = _row_specs()
    wide = pl.BlockSpec((TM, 2 * D_FF), lambda i: (i, 0))
    half_width = pl.BlockSpec((TM, D_MODEL), lambda i: (i, 0))
    vec = _const((1, D_MODEL))
    return pl.pallas_call(
        body, name="ffn_backward", grid=(lp // TM,),
        in_specs=[full, xrow, full, wide, vec, vec, ANY, ANY, ANY],
        out_specs=[full, pl.BlockSpec((3 * D_FF, TM), lambda i: (0, i)), half_width, _const((1, PACK_LANES)), vec, vec],
        out_shape=[jax.ShapeDtypeStruct((lp, D_MODEL), F32),
                   jax.ShapeDtypeStruct((3 * D_FF, lp), BF16),
                   jax.ShapeDtypeStruct((lp, D_MODEL), BF16),
                   jax.ShapeDtypeStruct((1, PACK_LANES), F32),
                   jax.ShapeDtypeStruct((1, D_MODEL), F32), jax.ShapeDtypeStruct((1, D_MODEL), F32)],
        scratch_shapes=_ffn_scratch(), compiler_params=_params(("arbitrary",), VMEM_LIMIT),
    )(h2, target, h1, ab, gf, g2, wg_t, wu_t, wd)


def _ffn_wgrad(xt, n2, dh2):
    lp = n2.shape[0]
    rows = lp // WGRAD_STEPS
    n_tiles = 3 * D_FF // FF_TILE
    shards_per_tile = FF_TILE // FF_SHARD
    gate_up_tiles = 2 * D_FF // FF_TILE

    def body(xt_ref, n2_ref, dh2_ref, out_ref, acc):
        q, k = pl.program_id(0), pl.program_id(1)

        @pl.when(k == 0)
        def _():
            acc[...] = jnp.zeros(acc.shape, F32)

        @pl.when(q < gate_up_tiles)
        def _():
            acc[...] += jnp.dot(xt_ref[...], n2_ref[...], preferred_element_type=F32)

        @pl.when(q >= gate_up_tiles)
        def _():
            acc[...] += jnp.dot(xt_ref[...], dh2_ref[...], preferred_element_type=F32)

        @pl.when(k == pl.num_programs(1) - 1)
        def _():
            for s in range(shards_per_tile):
                out_ref[s] = acc[pl.ds(s * FF_SHARD, FF_SHARD), :].astype(BF16)

    tiles_per_matrix = D_FF // FF_TILE
    return pl.pallas_call(
        body, name="ffn_wgrad", grid=(n_tiles, WGRAD_STEPS),
        in_specs=[pl.BlockSpec((FF_TILE, rows), lambda q, k: (q, k)),
                  pl.BlockSpec((rows, D_MODEL), lambda q, k: (jnp.where(q < gate_up_tiles, k, 0), 0)),
                  pl.BlockSpec((rows, D_MODEL), lambda q, k: (jnp.where(q < gate_up_tiles, 0, k), 0))],
        out_specs=pl.BlockSpec((shards_per_tile, None, FF_SHARD, D_MODEL),
                               lambda q, k: (q % tiles_per_matrix, q // tiles_per_matrix, 0, 0)),
        out_shape=jax.ShapeDtypeStruct((N_DEV, 3, FF_SHARD, D_MODEL), BF16),
        scratch_shapes=[pltpu.VMEM((FF_TILE, D_MODEL), F32)],
        compiler_params=_params(("parallel", "arbitrary"), VMEM_LIMIT),
    )(xt, n2, dh2)


def _out_proj_backward(dh1, ys, yp, gs, gp, w_out):
    lp = ys.shape[0]

    def body(dh1_ref, ys_ref, yp_ref, gs_ref, gp_ref, w_ref, dys_ref, dyp_ref, dgs_ref, dgp_ref):
        @pl.when(pl.program_id(0) == 0)
        def _():
            dgs_ref[...] = jnp.zeros(dgs_ref.shape, F32)
            dgp_ref[...] = jnp.zeros(dgp_ref.shape, F32)

        dmix = _dot_nt(dh1_ref[...], w_ref[...])
        for y_ref, g_ref, dy_ref, dg_ref, lo in ((ys_ref, gs_ref, dys_ref, dgs_ref, 0),
                                                 (yp_ref, gp_ref, dyp_ref, dgp_ref, D_SSM)):
            y_t = y_ref[...]
            r = _rstd(y_t)
            xhat = y_t * r
            dy, dg = _rms_bwd(dmix[:, lo:lo + D_SSM], xhat, r, g_ref[...])
            dy_ref[...] = dy
            dg_ref[...] += dg

    _, _, full, half = _row_specs()
    vec = _const((1, D_SSM))
    return pl.pallas_call(
        body, name="out_proj_backward", grid=(lp // TM,),
        in_specs=[full, half, half, vec, vec, _const((D_MODEL, D_MODEL))],
        out_specs=[half, half, vec, vec],
        out_shape=[jax.ShapeDtypeStruct((lp, D_SSM), F32)] * 2 + [jax.ShapeDtypeStruct((1, D_SSM), F32)] * 2,
        compiler_params=_params(("arbitrary",), VMEM_LIMIT),
    )(dh1, ys, yp, gs, gp, w_out)


def _in_proj_backward(head, x, du, dv, dh1, g1, w_in):
    lp = du.shape[0]

    def body(head_ref, x_ref, du_ref, dv_ref, dh1_ref, g_ref, w_ref, dx_ref, dhead_ref, dg_ref):
        i = pl.program_id(0)

        @pl.when(i == 0)
        def _():
            dg_ref[...] = jnp.zeros(dg_ref.shape, F32)

        h0 = jnp.where(i == 0, head_ref[...], x_ref[...])
        r = _rstd(h0)
        xhat = h0 * r
        dn1 = (_dot_nt(du_ref[...], w_ref[:, pl.ds(0, D_SSM)])
               + _dot_nt(dv_ref[...], w_ref[:, pl.ds(D_SSM, D_SSM)]))
        dx, dg = _rms_bwd(dn1, xhat, r, g_ref[...])
        dg_ref[...] += dg
        dh0 = dh1_ref[...] + dx
        dx_ref[...] = dh0

        @pl.when(i == 0)
        def _():
            dhead_ref[...] = dh0

    head_s, xrow, full, half = _row_specs()
    vec = _const((1, D_MODEL))
    return pl.pallas_call(
        body, name="in_proj_backward", grid=(lp // TM,),
        in_specs=[head_s, xrow, half, half, full, vec, _const((D_MODEL, D_MODEL))],
        out_specs=[xrow, head_s, vec],
        out_shape=[jax.ShapeDtypeStruct(x.shape, F32), jax.ShapeDtypeStruct((HEAD, D_MODEL), F32),
                   jax.ShapeDtypeStruct((1, D_MODEL), F32)],
        compiler_params=_params(("arbitrary",), VMEM_LIMIT),
    )(head, x, du, dv, dh1, g1, w_in)


def _permute_rows(a):
    lp, n = a.shape
    return a.reshape(SUBLANES, lp // SUBLANES, n).transpose(1, 0, 2).reshape(lp, n)


def _unpermute_rows(a):
    lp, n = a.shape
    return a.reshape(lp // SUBLANES, SUBLANES, n).transpose(1, 0, 2).reshape(lp, n)


def _pack(parts, dtype):
    rows = []
    for p in parts:
        flat = p.reshape(-1).astype(dtype)
        pad = (-flat.shape[0]) % PACK_UNIT
        rows.append(jnp.pad(flat, (0, pad)).reshape(-1, PACK_LANES))
    n_rows = sum(r.shape[0] for r in rows)
    if n_rows % 16:
        rows.append(jnp.zeros((8, PACK_LANES), dtype))
    return jnp.concatenate(rows, axis=0)


def _as2d(a):
    return a.reshape(-1, a.shape[-1])


def _unpack(packed, shapes):
    out, row = [], 0
    for shape in shapes:
        size = 1
        for s in shape:
            size *= s
        n_rows = -(-size // PACK_UNIT) * 8
        out.append(packed[row:row + n_rows].reshape(-1)[:size].reshape(shape))
        row += n_rows
    return out


def _to_view(name, a):
    if name in ("ssm_b_re", "ssm_b_im"):
        return a[0].transpose(0, 2, 1).reshape(-1, SSM_STATE)
    if name in ("ssm_d", "ssm_glu_b"):
        return a[0].T
    if name == "ssm_glu_w":
        return a[0].transpose(1, 2, 0).reshape(-1, SSM_GROUPS)
    return _as2d(a)


def _from_view(name, r, shape):
    if name in ("ssm_b_re", "ssm_b_im"):
        return r.reshape(SSM_GROUPS, SSM_GROUP, SSM_STATE).transpose(0, 2, 1).reshape(shape)
    if name in ("ssm_d", "ssm_glu_b"):
        return r.T.reshape(shape)
    if name == "ssm_glu_w":
        return r.reshape(SSM_GROUP, SSM_GROUP, SSM_GROUPS).transpose(2, 0, 1).reshape(shape)
    return r.reshape(shape)


def kernel(x, meta_tokens, norm1_g, w_in, ssm_lambda_re, ssm_lambda_im, ssm_log_step, ssm_b_re, ssm_b_im, ssm_c_re, ssm_c_im, ssm_d, ssm_glu_w, ssm_glu_b, ssm_norm_g, pool_w, pool_scale, pool_norm_g, w_out, norm2_g, w_gate, w_up, w_down, final_norm_g, loss_target, m_meta_tokens, m_norm1_g, m_w_in, m_ssm_lambda_re, m_ssm_lambda_im, m_ssm_log_step, m_ssm_b_re, m_ssm_b_im, m_ssm_c_re, m_ssm_c_im, m_ssm_d, m_ssm_glu_w, m_ssm_glu_b, m_ssm_norm_g, m_pool_w, m_pool_scale, m_pool_norm_g, m_w_out, m_norm2_g, m_w_gate, m_w_up, m_w_down, m_final_norm_g, v_meta_tokens, v_norm1_g, v_w_in, v_ssm_lambda_re, v_ssm_lambda_im, v_ssm_log_step, v_ssm_b_re, v_ssm_b_im, v_ssm_c_re, v_ssm_c_im, v_ssm_d, v_ssm_glu_w, v_ssm_glu_b, v_ssm_norm_g, v_pool_w, v_pool_scale, v_pool_norm_g, v_w_out, v_norm2_g, v_w_gate, v_w_up, v_w_down, v_final_norm_g):
    given = dict(locals())
    weights = {n: given[n] for n in WEIGHT_NAMES}
    n_meta = meta_tokens.shape[0]
    me = 4 * lax.axis_index("x") + 2 * lax.axis_index("y") + lax.axis_index("c")

    shard_rows = w_in.shape[1]
    first = [w_in[0].astype(BF16), meta_tokens]
    first_make = _push_copies(ALL_PEERS, [False, False])
    first_x, first_token = _split_start(first + [_landing(s, False) for s in first], first_make,
                                        2 * len(ALL_PEERS), "gather_w_in_start")

    xs = x[0]
    tgt = loss_target[0]
    first_row = HEAD - n_meta
    g1, g2, gf = norm1_g, norm2_g, final_norm_g.reshape(1, D_MODEL)
    gs, gp = ssm_norm_g, pool_norm_g

    lam_re, lam_im = ssm_lambda_re[0] + first_token[:1, :1], ssm_lambda_im[0]
    log_step = ssm_log_step[0].reshape(SSM_GROUPS, 1)
    b_re = ssm_b_re[0].transpose(0, 2, 1)
    b_im = ssm_b_im[0].transpose(0, 2, 1)
    abr, abi, zr, zi = _s5_disc_a(lam_re, lam_im, log_step)
    zr_col, zi_col = zr.reshape(SSM_GROUPS, 1, SSM_STATE), zi.reshape(SSM_GROUPS, 1, SSM_STATE)
    bbr, bbi = _s5_disc_b(zr_col, zi_col, b_re, b_im)
    s5_consts = (abr.reshape(1, -1), abi.reshape(1, -1), bbr, bbi, ssm_c_re[0], ssm_c_im[0],
                 ssm_d[0].reshape(1, D_SSM), ssm_glu_w[0], ssm_glu_b[0].reshape(1, D_SSM))
    pool_sc = pool_scale[0].reshape(POOL_GROUPS, 1, POOL_DIM)

    (_, _, w_in_all, meta_all), first_done = _split_wait(first_x, first_make, bbr, "gather_w_in_wait")
    w_in_all = w_in_all.reshape(D_MODEL, D_MODEL)
    meta_full = meta_all.transpose(1, 0, 2).reshape(n_meta, D_MODEL)
    head = jnp.concatenate([jnp.zeros((HEAD - n_meta, D_MODEL), F32), meta_full], axis=0)
    shards = [(w_out[0] + first_done[:1, :1]).astype(BF16), w_gate[0].T.astype(BF16), w_up[0].T.astype(BF16),
              w_down[0].astype(BF16)]
    n_big = len(shards)
    gather_make = _push_copies((SIBLING,) + CHIP_PEERS, [False] * n_big)
    gather, gather_token = _split_start(shards + [_landing(s, False) for s in shards], gather_make,
                                        n_big * (1 + len(CHIP_PEERS)), "gather_start")

    u, v, n1_t = _in_proj(head, xs, g1 + gather_token[:1, :1], w_in_all)
    u_p = _permute_rows(u)
    ys_p = _s5_forward(u_p, *s5_consts)
    landed, _ = _split_wait(gather, gather_make, ys_p, "gather_wait")
    forward_make = _forward_copies(n_big)
    forward, forward_token = _split_start(list(landed[n_big:]), forward_make, n_big * len(CHIP_PEERS),
                                          "gather_forward_start")
    ys = _unpermute_rows(ys_p)
    yp = _pool_forward(v, pool_w[0], pool_sc + forward_token[:1, :1], first_row)
    (w_out_all, wg_t, wu_t, wd_all), _ = _split_wait(forward, forward_make, yp, "gather_forward_wait")
    w_out_all = w_out_all.reshape(D_MODEL, D_MODEL)
    ffn_weights = [w.reshape(D_FF, D_MODEL) for w in (wg_t, wu_t, wd_all)]
    h1, mixed_t = _out_proj(head, xs, ys, yp, gs, gp, w_out_all)
    ab, n2, h2 = _ffn_forward(h1, g2, *ffn_weights)

    dh1, xt, dh2, loss_part, d_gf, d_g2 = _ffn_backward(h2, tgt, h1, ab, gf, g2, *ffn_weights)
    d_ffn = _ffn_wgrad(xt, n2, dh2)
    ffn_make = _push_copies(ALL_PEERS, [True])
    ffn_x, ffn_token = _split_start([d_ffn, _landing(d_ffn, True)], ffn_make, len(ALL_PEERS), "ffn_grad_start")
    dys, dyp, d_gs, d_gp = _out_proj_backward(dh1, ys, yp, gs + ffn_token[:1, :1], gp, w_out_all)
    d_wout = _proj_wgrad(mixed_t, [dh1], "out_proj_wgrad")
    dv, d_pool_w, d_pool_sc = _pool_backward(v, dyp, pool_w[0], pool_sc, first_row)
    (du_p, d_ar, d_ai, d_bbr, d_bbi, d_c_re, d_c_im, d_d, d_glu, d_glub) = _s5_backward(
        u_p, _permute_rows(dys), *s5_consts)
    du = _unpermute_rows(du_p)

    d_zr, d_zi, d_b_re, d_b_im = _s5_disc_b_bwd(zr_col, zi_col, b_re, b_im, d_bbr, d_bbi)
    d_lam_re, d_lam_im, d_log_step = _s5_disc_a_bwd(
        lam_re, lam_im, log_step,
        (d_ar.reshape(SSM_GROUPS, SSM_STATE), d_ai.reshape(SSM_GROUPS, SSM_STATE),
         d_zr.reshape(SSM_GROUPS, SSM_STATE), d_zi.reshape(SSM_GROUPS, SSM_STATE)))
    groups_last = lambda row: row.reshape(SSM_GROUPS, SSM_GROUP).T
    small_grads = {
        "ssm_lambda_re": d_lam_re, "ssm_lambda_im": d_lam_im, "ssm_log_step": d_log_step.reshape(1, SSM_GROUPS),
        "ssm_b_re": d_b_re.reshape(-1, SSM_STATE), "ssm_b_im": d_b_im.reshape(-1, SSM_STATE),
        "ssm_c_re": d_c_re.reshape(-1, SSM_STATE), "ssm_c_im": d_c_im.reshape(-1, SSM_STATE),
        "ssm_d": groups_last(d_d), "ssm_glu_w": d_glu.transpose(1, 2, 0).reshape(-1, SSM_GROUPS),
        "ssm_glu_b": groups_last(d_glub),
        "ssm_norm_g": d_gs, "pool_w": d_pool_w.reshape(-1, POOL_DIM), "pool_scale": d_pool_sc.reshape(-1, POOL_DIM),
        "pool_norm_g": d_gp, "norm2_g": d_g2,
    }

    early_names = SMALL_NAMES[1:-1]
    early_pack = _pack([small_grads[n] for n in early_names], BF16)
    d_wout = d_wout.reshape(N_DEV, shard_rows, D_MODEL)
    early_make = _push_copies(ALL_PEERS, [True, False])
    early_x, early_token = _split_start([d_wout, early_pack, _landing(d_wout, True), _landing(early_pack, False)],
                                        early_make, 2 * len(ALL_PEERS), "early_grad_start")
    d_x, d_head, d_g1 = _in_proj_backward(head, xs, du, dv, dh1, g1 + early_token[:1, :1], w_in_all)
    d_win = _proj_wgrad(n1_t, [du, dv], "in_proj_wgrad")
    (_, r_ffn), _ = _split_wait(ffn_x, ffn_make, d_g1, "ffn_grad_wait")
    (_, _, r_wout, r_early), _ = _split_wait(early_x, early_make, d_g1, "early_grad_wait")
    d_win = d_win.reshape(N_DEV, shard_rows, D_MODEL)
    late_pack = _pack([d_g1, d_gf, d_head[first_row:], loss_part], F32)
    late_make = _push_copies(ALL_PEERS, [True, False])
    late_x, late_token = _split_start([d_win, late_pack, _landing(d_win, True), _landing(late_pack, False)],
                                      late_make, 2 * len(ALL_PEERS), "late_grad_start")

    results = {}
    res_gate = _adamw_part(r_ffn, 0, w_gate[0].T + late_token[:1, :1], m_w_gate[0].T, v_w_gate[0].T, "adamw_w_gate")
    res_up = _adamw_part(r_ffn, 1, w_up[0].T, m_w_up[0].T, v_w_up[0].T, "adamw_w_up")
    results["w_gate"] = [r.T for r in res_gate]
    results["w_up"] = [r.T for r in res_up]
    results["w_down"] = _adamw_part(r_ffn, 2, w_down[0], m_w_down[0], v_w_down[0], "adamw_w_down")
    results["w_out"] = _adamw(r_wout, w_out[0], m_w_out[0], v_w_out[0], shard_rows, "adamw_w_out")
    done = sum(res[1][:1, :1] for res in (res_gate, res_up, results["w_down"], results["w_out"]))
    (_, _, r_win, r_late), _ = _split_wait(late_x, late_make, done, "late_grad_wait")
    results["w_in"] = _adamw(r_win, w_in[0], m_w_in[0], v_w_in[0], shard_rows, "adamw_w_in")

    sum_early, sum_late = _reduce_slots([r_early, r_late], "small_grad_sums")
    views = lambda prefix: [_to_view(n, given[prefix + n]) for n in SMALL_NAMES]
    w_views = views("")
    g_views = _unpack(sum_early, [w.shape for w in w_views[1:-1]])
    g_norm1, g_final, g_meta_all, loss_row = _unpack(
        sum_late, [norm1_g.shape, (1, D_MODEL), (n_meta, D_MODEL), (1, PACK_LANES)])
    g_views = [g_norm1] + g_views + [g_final]
    res_small = _adamw_many(g_views, w_views, views("m_"), views("v_"), "adamw_small")
    for idx, n in enumerate(SMALL_NAMES):
        results[n] = [_from_view(n, part[idx], weights[n].shape) for part in (g_views,) + tuple(res_small)]
    shard_cols = meta_tokens.shape[1]
    g_meta = lax.dynamic_slice_in_dim(g_meta_all, me * shard_cols, shard_cols, axis=1)
    results["meta_tokens"] = _adamw(g_meta[None], meta_tokens, m_meta_tokens, v_meta_tokens, n_meta, "adamw_meta")

    out = [loss_row[0, 0], d_x[None]]
    for part in range(4):
        for n in WEIGHT_NAMES:
            out.append(results[n][part].reshape(weights[n].shape))
    return tuple(out)
```

```python
import jax
import jax.numpy as jnp
from jax import lax
from jax.experimental import pallas as pl
from jax.experimental.pallas import tpu as pltpu

F32 = jnp.float32
BF16 = jnp.bfloat16

N_DEV = 8
D_MODEL = 1024
D_SSM = 512
SSM_GROUP = 16
SSM_STATE = 64
SSM_GROUPS = 32
POOL_GROUPS = 4
POOL_DIM = 128
COL_U = 128
COL_S = 512
N_COL = D_SSM // COL_U
GROUPS_PER_COL = COL_U // SSM_GROUP
D_FF = 2816
FF_SHARD = D_FF // N_DEV
FF_TILE = D_FF // 2
TM = 256
WGRAD_STEPS = 2
HEAD = TM
SUBLANES = 8
SCAN_UNROLL = 4
POOL_HALO = 16
EPS = 1e-6
STEP_FLOOR = -1e-4
VMEM_LIMIT = 60 * 1024 * 1024

ADAM_LR = 0.001
ADAM_B1 = 0.9
ADAM_B2 = 0.999
ADAM_EPS = 1e-08
ADAM_WD = 0.01
ADAM_STEP = 10

MESH_ID = pl.DeviceIdType.MESH
ANY = pl.BlockSpec(memory_space=pl.ANY)

SMALL_NAMES = ("norm1_g", "ssm_lambda_re", "ssm_lambda_im", "ssm_log_step", "ssm_b_re", "ssm_b_im",
               "ssm_c_re", "ssm_c_im", "ssm_d", "ssm_glu_w", "ssm_glu_b", "ssm_norm_g", "pool_w",
               "pool_scale", "pool_norm_g", "norm2_g", "final_norm_g")
WEIGHT_NAMES = ("meta_tokens", "norm1_g", "w_in", "ssm_lambda_re", "ssm_lambda_im", "ssm_log_step",
                "ssm_b_re", "ssm_b_im", "ssm_c_re", "ssm_c_im", "ssm_d", "ssm_glu_w", "ssm_glu_b",
                "ssm_norm_g", "pool_w", "pool_scale", "pool_norm_g", "w_out", "norm2_g", "w_gate",
                "w_up", "w_down", "final_norm_g")
LANES = 128
PACK_LANES = LANES
PACK_UNIT = 8 * PACK_LANES


def _dot(a, b):
    return jnp.dot(a.astype(BF16), b.astype(BF16), preferred_element_type=F32)


def _dot_nt(a, b):
    return lax.dot_general(a.astype(BF16), b.astype(BF16), (((1,), (1,)), ((), ())), preferred_element_type=F32)


def _dot_tn(a, b):
    return lax.dot_general(a.astype(BF16), b.astype(BF16), (((0,), (0,)), ((), ())), preferred_element_type=F32)


def _sigmoid(x):
    return 1.0 / (1.0 + jnp.exp(-x))


def _rstd(x):
    return lax.rsqrt(jnp.mean(x * x, axis=-1, keepdims=True) + EPS)


def _rms_bwd(dy, xhat, r, g):
    dxh = dy * g
    dx = r * (dxh - xhat * jnp.mean(dxh * xhat, axis=-1, keepdims=True))
    return dx, jnp.sum(dy * xhat, axis=0, keepdims=True)


def _params(sem, vmem=None):
    return pltpu.CompilerParams(dimension_semantics=sem, vmem_limit_bytes=vmem)


def _const(shape):
    return pl.BlockSpec(shape, lambda *_: (0,) * len(shape))


def _xrow(i):
    return (jnp.maximum(i - 1, 0), 0)


HBM = pl.BlockSpec(memory_space=pltpu.HBM)
SEM = pl.BlockSpec(memory_space=pltpu.SEMAPHORE)
EFFECT = pltpu.SideEffectType.DATAFLOW_SIDE_EFFECTING
ALL_PEERS = tuple(range(1, N_DEV))
SIBLING = 1
CHIP_PEERS = (2, 4, 6)


def _me():
    return 4 * lax.axis_index("x") + 2 * lax.axis_index("y") + lax.axis_index("c")


def _peer(k):
    x, y, c = lax.axis_index("x"), lax.axis_index("y"), lax.axis_index("c")
    px = 1 - x if k & 4 else x
    py = 1 - y if k & 2 else y
    pc = 1 - c if k & 1 else c
    return (px, py, pc), 4 * px + 2 * py + pc


def _landing(arr, scatter):
    if scatter:
        own = lax.dynamic_index_in_dim(arr, _me(), 0, keepdims=False)
    else:
        own = arr
    return lax.dynamic_update_index_in_dim(lax.empty((N_DEV,) + own.shape, arr.dtype), own, _me(), 0)


def _push_copies(peers, scatter):
    n_arr = len(scatter)

    def make(refs, send_sems, recv_sems):
        copies = []
        for a in range(n_arr):
            for i, k in enumerate(peers):
                peer_id, peer = _peer(k)
                sem = a * len(peers) + i
                copies.append(pltpu.make_async_remote_copy(
                    src_ref=refs[a].at[peer] if scatter[a] else refs[a], dst_ref=refs[n_arr + a].at[_me()],
                    send_sem=send_sems.at[sem], recv_sem=recv_sems.at[sem],
                    device_id=peer_id, device_id_type=MESH_ID))
        return copies
    return make


def _forward_copies(n_arr):
    def make(refs, send_sems, recv_sems):
        copies = []
        sibling_id, _ = _peer(SIBLING)
        for a in range(n_arr):
            for i, k in enumerate(CHIP_PEERS):
                slot = refs[a].at[_peer(k)[1]]
                sem = a * len(CHIP_PEERS) + i
                copies.append(pltpu.make_async_remote_copy(
                    src_ref=slot, dst_ref=slot, send_sem=send_sems.at[sem], recv_sem=recv_sems.at[sem],
                    device_id=sibling_id, device_id_type=MESH_ID))
        return copies
    return make


def _split_start(operands, make, n_sem, name):
    n_op = len(operands)

    def body(*refs):
        for cp in make(refs[:n_op], refs[n_op], refs[n_op + 1]):
            cp.start()
        refs[-1][...] = jnp.zeros(refs[-1].shape, F32)

    out = pl.pallas_call(
        body, name=name,
        out_shape=(pltpu.SemaphoreType.DMA((n_sem,)), pltpu.SemaphoreType.DMA((n_sem,)),
                   *[pltpu.HBM(t.shape, t.dtype) for t in operands], jax.ShapeDtypeStruct((8, PACK_LANES), F32)),
        in_specs=[HBM] * n_op, out_specs=(SEM, SEM, *[HBM] * n_op, pl.BlockSpec(memory_space=pltpu.VMEM)),
        input_output_aliases={i: 2 + i for i in range(n_op)},
        compiler_params=pltpu.CompilerParams(has_side_effects=EFFECT),
    )(*[pltpu.with_memory_space_constraint(t, pltpu.HBM) for t in operands])
    return out[:-1], out[-1]


def _split_wait(started, make, after, name):
    send_sems, recv_sems, thru = started[0], started[1], started[2:]
    n_op = len(thru)

    def body(*refs):
        for cp in make(refs[:n_op], refs[n_op], refs[n_op + 1]):
            cp.wait_send()
            cp.wait_recv()
        refs[-1][...] = jnp.zeros(refs[-1].shape, F32)

    out = pl.pallas_call(
        body, name=name,
        out_shape=(*[pltpu.HBM(t.shape, t.dtype) for t in thru], jax.ShapeDtypeStruct((8, PACK_LANES), F32)),
        in_specs=[HBM] * n_op + [SEM, SEM, ANY], out_specs=(*[HBM] * n_op, pl.BlockSpec(memory_space=pltpu.VMEM)),
        input_output_aliases={i: i for i in range(n_op)},
        compiler_params=pltpu.CompilerParams(has_side_effects=EFFECT),
    )(*thru, send_sems, recv_sems, after)
    return out[:-1], out[-1]


def _adamw_math(g, w, m, v):
    nm = ADAM_B1 * m + (1.0 - ADAM_B1) * g
    nv = ADAM_B2 * v + (1.0 - ADAM_B2) * (g * g)
    m_hat = nm / (1.0 - ADAM_B1 ** ADAM_STEP)
    v_hat = nv / (1.0 - ADAM_B2 ** ADAM_STEP)
    return -ADAM_LR * (m_hat / (jnp.sqrt(v_hat) + ADAM_EPS) + ADAM_WD * w), nm, nv


def _sum_slots(s_ref):
    g = s_ref[0].astype(F32)
    for s in range(1, s_ref.shape[0]):
        g = g + s_ref[s].astype(F32)
    return g


def _adamw(slots, w, m, v, tile_rows, name):
    n, rows, cols = slots.shape

    def body(s_ref, w_ref, m_ref, v_ref, g_ref, d_ref, nm_ref, nv_ref):
        g = _sum_slots(s_ref)
        g_ref[...] = g
        d_ref[...], nm_ref[...], nv_ref[...] = _adamw_math(g, w_ref[...], m_ref[...], v_ref[...])

    tile = pl.BlockSpec((tile_rows, cols), lambda i: (i, 0))
    return pl.pallas_call(
        body, name=name, grid=(rows // tile_rows,),
        in_specs=[pl.BlockSpec((n, tile_rows, cols), lambda i: (0, i, 0)), tile, tile, tile],
        out_specs=[tile] * 4, out_shape=[jax.ShapeDtypeStruct((rows, cols), F32)] * 4,
        compiler_params=_params(("parallel",), VMEM_LIMIT),
    )(slots, w, m, v)


def _adamw_part(slots, part, w, m, v, name):
    n, _, rows, cols = slots.shape
    tile_cols = 256

    def body(s_ref, w_ref, m_ref, v_ref, g_ref, d_ref, nm_ref, nv_ref):
        g = _sum_slots(s_ref)
        g_ref[...] = g
        d_ref[...], nm_ref[...], nv_ref[...] = _adamw_math(g, w_ref[...], m_ref[...], v_ref[...])

    tile = pl.BlockSpec((rows, tile_cols), lambda i: (0, i))
    return pl.pallas_call(
        body, name=name, grid=(cols // tile_cols,),
        in_specs=[pl.BlockSpec((n, None, rows, tile_cols), lambda i: (0, part, 0, i)), tile, tile, tile],
        out_specs=[tile] * 4, out_shape=[jax.ShapeDtypeStruct((rows, cols), F32)] * 4,
        compiler_params=_params(("parallel",), VMEM_LIMIT),
    )(slots, w, m, v)


def _reduce_slots(slot_arrays, name):
    def body(*refs):
        n_arr = len(refs) // 2
        for s_ref, o_ref in zip(refs[:n_arr], refs[n_arr:]):
            o_ref[...] = _sum_slots(s_ref)
    return pl.pallas_call(
        body, name=name, out_shape=[jax.ShapeDtypeStruct(s.shape[1:], F32) for s in slot_arrays],
        compiler_params=_params(None, VMEM_LIMIT))(*slot_arrays)


def _adamw_many(grads, ws, ms, vs, name):
    n = len(grads)

    def body(*refs):
        ins, outs = refs[:4 * n], refs[4 * n:]
        for i in range(n):
            g, w, m, v = (ins[j * n + i][...] for j in range(4))
            outs[i][...], outs[n + i][...], outs[2 * n + i][...] = _adamw_math(g, w, m, v)

    out = pl.pallas_call(
        body, name=name, out_shape=[jax.ShapeDtypeStruct(w.shape, F32) for w in ws] * 3,
        compiler_params=_params(None, VMEM_LIMIT))(*grads, *ws, *ms, *vs)
    return out[:n], out[n:2 * n], out[2 * n:]


def _disc_a(lam_re, lam_im, log_step):
    lr = jnp.minimum(lam_re, STEP_FLOOR)
    step = jnp.exp(log_step)
    mag = jnp.exp(lr * step)
    ang = lam_im * step
    abr = mag * jnp.cos(ang)
    abi = mag * jnp.sin(ang)
    nr = abr - 1.0
    den = lr * lr + lam_im * lam_im
    cr = (nr * lr + abi * lam_im) / den
    ci = (abi * lr - nr * lam_im) / den
    return abr, abi, cr, ci


def _disc_b(cr, ci, b_re, b_im):
    return cr * b_re - ci * b_im, cr * b_im + ci * b_re


def _s5_disc_a(lam_re, lam_im, log_step):
    def body(lr_ref, li_ref, ls_ref, *outs):
        for o, val in zip(outs, _disc_a(lr_ref[...], li_ref[...], ls_ref[...])):
            o[...] = val
    return pl.pallas_call(body, name="s5_disc_a", out_shape=[jax.ShapeDtypeStruct(lam_re.shape, F32)] * 4)(
        lam_re, lam_im, log_step)


def _s5_disc_a_bwd(lam_re, lam_im, log_step, cts):
    def body(lr_ref, li_ref, ls_ref, c0, c1, c2, c3, dlr_ref, dli_ref, dls_ref):
        _, vjp = jax.vjp(_disc_a, lr_ref[...], li_ref[...], ls_ref[...])
        dlr, dli, dls = vjp((c0[...], c1[...], c2[...], c3[...]))
        dlr_ref[...] = dlr
        dli_ref[...] = dli
        dls_ref[...] = dls
    return pl.pallas_call(
        body, name="s5_disc_a_bwd",
        out_shape=[jax.ShapeDtypeStruct(lam_re.shape, F32), jax.ShapeDtypeStruct(lam_re.shape, F32),
                   jax.ShapeDtypeStruct(log_step.shape, F32)])(lam_re, lam_im, log_step, *cts)


def _s5_disc_b(cr, ci, b_re, b_im):
    def body(cr_ref, ci_ref, br_ref, bi_ref, o_re, o_im):
        o_re[...], o_im[...] = _disc_b(cr_ref[...], ci_ref[...], br_ref[...], bi_ref[...])
    return pl.pallas_call(body, name="s5_disc_b", out_shape=[jax.ShapeDtypeStruct(b_re.shape, F32)] * 2)(
        cr, ci, b_re, b_im)


def _s5_disc_b_bwd(cr, ci, b_re, b_im, d_re, d_im):
    def body(cr_ref, ci_ref, br_ref, bi_ref, dr_ref, di_ref, dcr_ref, dci_ref, dbr_ref, dbi_ref):
        _, vjp = jax.vjp(_disc_b, cr_ref[...], ci_ref[...], br_ref[...], bi_ref[...])
        dcr_ref[...], dci_ref[...], dbr_ref[...], dbi_ref[...] = vjp((dr_ref[...], di_ref[...]))
    return pl.pallas_call(
        body, name="s5_disc_b_bwd",
        out_shape=[jax.ShapeDtypeStruct(cr.shape, F32)] * 2 + [jax.ShapeDtypeStruct(b_re.shape, F32)] * 2)(
            cr, ci, b_re, b_im, d_re, d_im)


def _cmul(ar, ai, br, bi):
    return ar * br - ai * bi, ar * bi + ai * br


def _cpow(ar, ai, n):
    rr, ri = jnp.ones_like(ar), jnp.zeros_like(ai)
    while n:
        if n & 1:
            rr, ri = _cmul(rr, ri, ar, ai)
        n >>= 1
        if n:
            ar, ai = _cmul(ar, ai, ar, ai)
    return rr, ri


def _tile_rows(i):
    if isinstance(i, int):
        return pl.ds(i * SUBLANES, SUBLANES)
    return pl.ds(pl.multiple_of(i * SUBLANES, SUBLANES), SUBLANES)


def _segment_scan(z_re, z_im, ar, ai, lseg, reverse, visit=None):
    shape = (SUBLANES, z_re.shape[1])
    half = lseg // 2
    arb = jnp.broadcast_to(ar, shape)
    aib = jnp.broadcast_to(ai, shape)
    zero = jnp.zeros(shape, F32)
    row = lax.broadcasted_iota(jnp.int32, shape, 0)

    def tiles(k):
        return (lseg - 1 - k, half - 1 - k) if reverse else (k, half + k)

    def advance(tile, sr, si):
        rows = _tile_rows(tile)
        nr, ni = _cmul(arb, aib, sr, si)
        return rows, nr + z_re[rows, :], ni + z_im[rows, :]

    def first_pass(k, carry):
        ta, tb = tiles(k)
        return advance(ta, carry[0], carry[1])[1:] + advance(tb, carry[2], carry[3])[1:]

    def unrolled(step):
        def body(it, carry):
            for j in range(SCAN_UNROLL):
                carry = step(it * SCAN_UNROLL + j, carry)
            return carry
        return body

    n_iter = half // SCAN_UNROLL
    fa_r, fa_i, fb_r, fb_i = lax.fori_loop(0, n_iter, unrolled(first_pass), (zero,) * 4)
    hr, hi = _cpow(arb, aib, half)
    pr, pi = _cmul(hr, hi, hr, hi)
    fr, fi = _cmul(hr, hi, fa_r, fa_i)
    fr, fi = fr + fb_r, fi + fb_i
    cr, ci = zero, zero
    for _ in range(SUBLANES - 1):
        tr, ti = _cmul(pr, pi, cr, ci)
        tr, ti = tr + fr, ti + fi
        if reverse:
            cr = jnp.where(row == SUBLANES - 1, 0.0, pltpu.roll(tr, SUBLANES - 1, 0))
            ci = jnp.where(row == SUBLANES - 1, 0.0, pltpu.roll(ti, SUBLANES - 1, 0))
        else:
            cr = jnp.where(row == 0, 0.0, pltpu.roll(tr, 1, 0))
            ci = jnp.where(row == 0, 0.0, pltpu.roll(ti, 1, 0))

    br, bi = _cmul(hr, hi, cr, ci)
    br, bi = br + fa_r, bi + fa_i

    def second_pass(k, carry, b_is_tile0=False):
        states, acc = list(carry[:4]), carry[4]
        for chain, tile in enumerate(tiles(k)):
            rows, nr, ni = advance(tile, states[2 * chain], states[2 * chain + 1])
            z_re[rows, :] = nr
            z_im[rows, :] = ni
            states[2 * chain], states[2 * chain + 1] = nr, ni
            if visit is not None:
                acc = visit(tile, nr, ni, acc, chain == 1 and b_is_tile0)
        return (*states, acc)

    acc0 = (zero, zero) if visit is not None else 0
    carry = lax.fori_loop(0, n_iter - 1, unrolled(second_pass), (cr, ci, br, bi, acc0))
    for k in range(half - SCAN_UNROLL, half - 1):
        carry = second_pass(k, carry)
    return second_pass(half - 1, carry, b_is_tile0=reverse)[4]


def _gelu(y):
    c = 0.7978845608028654
    return 0.5 * y * (1.0 + jnp.tanh(c * (y + 0.044715 * y * y * y)))


def _gelu_grad(y):
    c = 0.7978845608028654
    th = jnp.tanh(c * (y + 0.044715 * y * y * y))
    return 0.5 * (1.0 + th) + 0.5 * y * (1.0 - th * th) * c * (1.0 + 3.0 * 0.044715 * y * y)


def _s5_specs(lp):
    col_u = pl.BlockSpec((lp, COL_U), lambda j: (0, j))
    row_u = pl.BlockSpec((1, COL_U), lambda j: (0, j))
    row_s = pl.BlockSpec((1, COL_S), lambda j: (0, j))
    bc_blk = pl.BlockSpec((GROUPS_PER_COL, SSM_GROUP, SSM_STATE), lambda j: (j, 0, 0))
    glu_blk = pl.BlockSpec((GROUPS_PER_COL, SSM_GROUP, SSM_GROUP), lambda j: (j, 0, 0))
    return col_u, row_u, row_s, bc_blk, glu_blk


def _s5_block_diag_scratch():
    return ([pltpu.VMEM((COL_U, COL_S), BF16)] * 4 + [pltpu.VMEM((COL_U, COL_U), BF16)]
            + [pltpu.VMEM((COL_U, COL_S), F32)])


def _fill_block_diag(bd_ref, blocks_ref, stage):
    r, c = blocks_ref.shape[1:]
    stage[...] = jnp.zeros(stage.shape, F32)
    for gl in range(GROUPS_PER_COL):
        stage[pl.ds(gl * r, r), pl.ds(gl * c, c)] = blocks_ref[gl]
    bd_ref[...] = stage[:, :GROUPS_PER_COL * c].astype(BF16)


def _take_block_diag(out_ref, mat):
    r, c = out_ref.shape[1:]
    for gl in range(GROUPS_PER_COL):
        out_ref[gl] = mat[gl * r:(gl + 1) * r, gl * c:(gl + 1) * c]


def _s5_fill_states(u_ref, bre_ref, bim_ref, ar_ref, ai_ref, s_re, s_im, lseg, n_chunks, chunk):
    def fill(cidx, carry):
        rows = pl.ds(pl.multiple_of(cidx * chunk, SUBLANES), chunk)
        ub = u_ref[rows, :].astype(BF16)
        s_re[rows, :] = jnp.dot(ub, bre_ref[...], preferred_element_type=F32)
        s_im[rows, :] = jnp.dot(ub, bim_ref[...], preferred_element_type=F32)
        return carry
    lax.fori_loop(0, n_chunks, fill, 0)
    _segment_scan(s_re, s_im, ar_ref[...], ai_ref[...], lseg, reverse=False)


def _s5_forward(u_p, ar, ai, bbr, bbi, c_re, c_im, d_row, glu_w, glub_row):
    lp = u_p.shape[0]
    lseg = lp // SUBLANES
    chunk, n_chunks = 4 * lseg, SUBLANES // 4

    def body(u_ref, ar_ref, ai_ref, bbr_ref, bbi_ref, cr_ref, ci_ref, d_ref, gw_ref, glub_ref,
             ys_ref, s_re, s_im, bre_ref, bim_ref, cre_ref, cim_ref, glu_ref, stage):
        for bd, blocks in ((bre_ref, bbr_ref), (bim_ref, bbi_ref), (cre_ref, cr_ref), (cim_ref, ci_ref),
                           (glu_ref, gw_ref)):
            _fill_block_diag(bd, blocks, stage)
        _s5_fill_states(u_ref, bre_ref, bim_ref, ar_ref, ai_ref, s_re, s_im, lseg, n_chunks, chunk)

        def emit(cidx, carry):
            rows = pl.ds(pl.multiple_of(cidx * chunk, SUBLANES), chunk)
            y = (_dot_nt(s_re[rows, :], cre_ref[...]) - _dot_nt(s_im[rows, :], cim_ref[...])
                 + d_ref[...] * u_ref[rows, :])
            g = _gelu(y)
            gate = _dot(g, glu_ref[...]) + glub_ref[...]
            ys_ref[rows, :] = g * _sigmoid(gate)
            return carry
        lax.fori_loop(0, n_chunks, emit, 0)

    col_u, row_u, row_s, bc_blk, glu_blk = _s5_specs(lp)
    return pl.pallas_call(
        body, name="s5_forward", grid=(N_COL,),
        in_specs=[col_u, row_s, row_s, bc_blk, bc_blk, bc_blk, bc_blk, row_u, glu_blk, row_u],
        out_specs=col_u, out_shape=jax.ShapeDtypeStruct((lp, D_SSM), F32),
        scratch_shapes=[pltpu.VMEM((lp, COL_S), F32), pltpu.VMEM((lp, COL_S), F32)] + _s5_block_diag_scratch(),
        compiler_params=_params(("arbitrary",), VMEM_LIMIT),
    )(u_p, ar, ai, bbr, bbi, c_re, c_im, d_row, glu_w, glub_row)


def _s5_backward(u_p, dys_p, ar, ai, bbr, bbi, c_re, c_im, d_row, glu_w, glub_row):
    lp = u_p.shape[0]
    lseg = lp // SUBLANES
    chunk, n_chunks = 4 * lseg, SUBLANES // 4

    def body(u_ref, dys_ref, ar_ref, ai_ref, bbr_ref, bbi_ref, cr_ref, ci_ref, d_ref, gw_ref, glub_ref,
             du_ref, dar_ref, dai_ref, dbbr_ref, dbbi_ref, dcr_ref, dci_ref, dd_ref, dgw_ref, dglub_ref,
             s_re, s_im, q_re, q_im, bre_ref, bim_ref, cre_ref, cim_ref, glu_ref, stage,
             dbre_ref, dbim_ref, dcre_ref, dcim_ref, dglu_ref):
        for bd, blocks in ((bre_ref, bbr_ref), (bim_ref, bbi_ref), (cre_ref, cr_ref), (cim_ref, ci_ref),
                           (glu_ref, gw_ref)):
            _fill_block_diag(bd, blocks, stage)
        _s5_fill_states(u_ref, bre_ref, bim_ref, ar_ref, ai_ref, s_re, s_im, lseg, n_chunks, chunk)
        for ref in (dcre_ref, dcim_ref, dd_ref, dglu_ref, dglub_ref, dbre_ref, dbim_ref):
            ref[...] = jnp.zeros(ref.shape, F32)

        def mixer_bwd(cidx, carry):
            rows = pl.ds(pl.multiple_of(cidx * chunk, SUBLANES), chunk)
            u = u_ref[rows, :]
            sr, si = s_re[rows, :], s_im[rows, :]
            y = _dot_nt(sr, cre_ref[...]) - _dot_nt(si, cim_ref[...]) + d_ref[...] * u
            g = _gelu(y)
            sg = _sigmoid(_dot(g, glu_ref[...]) + glub_ref[...])
            dout = dys_ref[rows, :]
            dgate = dout * g * sg * (1.0 - sg)
            dy = (dout * sg + _dot_nt(dgate, glu_ref[...])) * _gelu_grad(y)
            dglu_ref[...] += _dot_tn(g, dgate)
            dglub_ref[...] += jnp.sum(dgate, axis=0, keepdims=True)
            dd_ref[...] += jnp.sum(dy * u, axis=0, keepdims=True)
            dcre_ref[...] += _dot_tn(dy, sr)
            dcim_ref[...] -= _dot_tn(dy, si)
            q_re[rows, :] = _dot(dy, cre_ref[...])
            q_im[rows, :] = -_dot(dy, cim_ref[...])
            du_ref[rows, :] = d_ref[...] * dy
            return carry
        lax.fori_loop(0, n_chunks, mixer_bwd, 0)

        row = lax.broadcasted_iota(jnp.int32, (SUBLANES, COL_S), 0)

        def visit(i, qr, qi, acc, is_tile0):
            if is_tile0:
                prev = _tile_rows(lseg - 1)
                pr = jnp.where(row == 0, 0.0, pltpu.roll(s_re[prev, :], 1, 0))
                pi = jnp.where(row == 0, 0.0, pltpu.roll(s_im[prev, :], 1, 0))
            else:
                prev = _tile_rows(i - 1)
                pr, pi = s_re[prev, :], s_im[prev, :]
            return acc[0] + qr * pr + qi * pi, acc[1] + qi * pr - qr * pi

        dar, dai = _segment_scan(q_re, q_im, ar_ref[...], -ai_ref[...], lseg, reverse=True, visit=visit)
        dar_ref[...] = jnp.sum(dar, axis=0, keepdims=True)
        dai_ref[...] = jnp.sum(dai, axis=0, keepdims=True)

        def input_bwd(cidx, carry):
            rows = pl.ds(pl.multiple_of(cidx * chunk, SUBLANES), chunk)
            qr, qi = q_re[rows, :], q_im[rows, :]
            u = u_ref[rows, :]
            du_ref[rows, :] += _dot_nt(qr, bre_ref[...]) + _dot_nt(qi, bim_ref[...])
            dbre_ref[...] += _dot_tn(u, qr)
            dbim_ref[...] += _dot_tn(u, qi)
            return carry
        lax.fori_loop(0, n_chunks, input_bwd, 0)
        for out, acc in ((dbbr_ref, dbre_ref), (dbbi_ref, dbim_ref), (dcr_ref, dcre_ref), (dci_ref, dcim_ref),
                         (dgw_ref, dglu_ref)):
            _take_block_diag(out, acc[...])

    col_u, row_u, row_s, bc_blk, glu_blk = _s5_specs(lp)
    group_mats = jax.ShapeDtypeStruct((SSM_GROUPS, SSM_GROUP, SSM_STATE), F32)
    return pl.pallas_call(
        body, name="s5_backward", grid=(N_COL,),
        in_specs=[col_u, col_u, row_s, row_s, bc_blk, bc_blk, bc_blk, bc_blk, row_u, glu_blk, row_u],
        out_specs=[col_u, row_s, row_s, bc_blk, bc_blk, bc_blk, bc_blk, row_u, glu_blk, row_u],
        out_shape=[jax.ShapeDtypeStruct((lp, D_SSM), F32),
                   jax.ShapeDtypeStruct((1, N_COL * COL_S), F32), jax.ShapeDtypeStruct((1, N_COL * COL_S), F32),
                   group_mats, group_mats, group_mats, group_mats, jax.ShapeDtypeStruct((1, D_SSM), F32),
                   jax.ShapeDtypeStruct((SSM_GROUPS, SSM_GROUP, SSM_GROUP), F32), jax.ShapeDtypeStruct((1, D_SSM), F32)],
        scratch_shapes=([pltpu.VMEM((lp, COL_S), F32)] * 4 + _s5_block_diag_scratch()
                        + [pltpu.VMEM((COL_U, COL_S), F32)] * 4 + [pltpu.VMEM((COL_U, COL_U), F32)]),
        compiler_params=_params(("arbitrary",), VMEM_LIMIT),
    )(u_p, dys_p, ar, ai, bbr, bbi, c_re, c_im, d_row, glu_w, glub_row)


def _window_sum(ext, group, leading):
    n = ext.shape[0]
    s = ext
    for j in range(POOL_GROUPS):
        shift = n - (1 << j) if leading else 1 << j
        s = jnp.where(j <= group, s + pltpu.roll(s, shift, 0), s)
    return s


def _pool_inv_count(tile, window, first_row):
    t = tile * TM + lax.broadcasted_iota(jnp.int32, (TM, 1), 0) - first_row
    return 1.0 / jnp.clip(t + 1, 1, window).astype(F32)


def _pool_specs(lp):
    col = pl.BlockSpec((lp, POOL_DIM), lambda k: (0, k))
    mat = pl.BlockSpec((None, POOL_DIM, POOL_DIM), lambda k: (k, 0, 0))
    row = pl.BlockSpec((None, 1, POOL_DIM), lambda k: (k, 0, 0))
    return col, mat, row


def _pool_forward(v, pool_w, pool_scale, first_row):
    lp = v.shape[0]
    n_tiles = lp // TM

    def body(v_ref, w_ref, sc_ref, yp_ref, vpad):
        group = pl.program_id(0)
        window = jnp.left_shift(2, group)
        vpad[pl.ds(0, POOL_HALO), :] = jnp.zeros((POOL_HALO, POOL_DIM), F32)
        vpad[pl.ds(POOL_HALO, lp), :] = v_ref[...]

        def tile(j, carry):
            start = pl.multiple_of(j * TM, TM)
            ext = vpad[pl.ds(start, TM + POOL_HALO), :]
            sums = _window_sum(ext, group, leading=False)[POOL_HALO:, :]
            p = sums * _pool_inv_count(j, window, first_row) - ext[POOL_HALO:, :]
            yp_ref[pl.ds(start, TM), :] = _dot(p, w_ref[...]) * sc_ref[...]
            return carry
        lax.fori_loop(0, n_tiles, tile, 0)

    col, mat, row = _pool_specs(lp)
    return pl.pallas_call(
        body, name="pool_forward", grid=(POOL_GROUPS,),
        in_specs=[col, mat, row], out_specs=col, out_shape=jax.ShapeDtypeStruct((lp, D_SSM), F32),
        scratch_shapes=[pltpu.VMEM((lp + POOL_HALO, POOL_DIM), F32)],
        compiler_params=_params(("arbitrary",), VMEM_LIMIT),
    )(v, pool_w, pool_scale)


def _pool_backward(v, dyp, pool_w, pool_scale, first_row):
    lp = v.shape[0]
    n_tiles = lp // TM

    def body(v_ref, dyp_ref, w_ref, sc_ref, dv_ref, dw_ref, dsc_ref, vpad, gpad):
        group = pl.program_id(0)
        window = jnp.left_shift(2, group)
        vpad[pl.ds(0, POOL_HALO), :] = jnp.zeros((POOL_HALO, POOL_DIM), F32)
        vpad[pl.ds(POOL_HALO, lp), :] = v_ref[...]
        gpad[pl.ds(lp, POOL_HALO), :] = jnp.zeros((POOL_HALO, POOL_DIM), F32)
        dw_ref[...] = jnp.zeros(dw_ref.shape, F32)
        dsc_ref[...] = jnp.zeros(dsc_ref.shape, F32)

        def linear_bwd(j, carry):
            start = pl.multiple_of(j * TM, TM)
            ext = vpad[pl.ds(start, TM + POOL_HALO), :]
            inv = _pool_inv_count(j, window, first_row)
            p = _window_sum(ext, group, leading=False)[POOL_HALO:, :] * inv - ext[POOL_HALO:, :]
            z = _dot(p, w_ref[...])
            dyp_t = dyp_ref[pl.ds(start, TM), :]
            dz = dyp_t * sc_ref[...]
            dsc_ref[...] += jnp.sum(dyp_t * z, axis=0, keepdims=True)
            dw_ref[...] += _dot_tn(p, dz)
            dp = _dot_nt(dz, w_ref[...])
            gpad[pl.ds(start, TM), :] = dp * inv
            dv_ref[pl.ds(start, TM), :] = -dp
            return carry
        lax.fori_loop(0, n_tiles, linear_bwd, 0)

        def window_bwd(j, carry):
            start = pl.multiple_of(j * TM, TM)
            ext = gpad[pl.ds(start, TM + POOL_HALO), :]
            dv_ref[pl.ds(start, TM), :] += _window_sum(ext, group, leading=True)[:TM, :]
            return carry
        lax.fori_loop(0, n_tiles, window_bwd, 0)

    col, mat, row = _pool_specs(lp)
    return pl.pallas_call(
        body, name="pool_backward", grid=(POOL_GROUPS,),
        in_specs=[col, col, mat, row], out_specs=[col, mat, row],
        out_shape=[jax.ShapeDtypeStruct((lp, D_SSM), F32),
                   jax.ShapeDtypeStruct((POOL_GROUPS, POOL_DIM, POOL_DIM), F32),
                   jax.ShapeDtypeStruct((POOL_GROUPS, 1, POOL_DIM), F32)],
        scratch_shapes=[pltpu.VMEM((lp + POOL_HALO, POOL_DIM), F32)] * 2,
        compiler_params=_params(("arbitrary",), VMEM_LIMIT),
    )(v, dyp, pool_w, pool_scale)


def _row_specs():
    head = _const((HEAD, D_MODEL))
    xrow = pl.BlockSpec((TM, D_MODEL), _xrow)
    full = pl.BlockSpec((TM, D_MODEL), lambda i: (i, 0))
    half = pl.BlockSpec((TM, D_SSM), lambda i: (i, 0))
    return head, xrow, full, half


def _in_proj(head, x, g1, w_in):
    n_tiles = (HEAD + x.shape[0]) // TM
    lp = n_tiles * TM

    def body(head_ref, x_ref, g_ref, w_ref, u_ref, v_ref):
        h0 = jnp.where(pl.program_id(0) == 0, head_ref[...], x_ref[...])
        proj = _dot(h0 * _rstd(h0) * g_ref[...], w_ref[...])
        u_ref[...] = proj[:, :D_SSM]
        v_ref[...] = proj[:, D_SSM:]

    head_s, xrow, _, half = _row_specs()
    return pl.pallas_call(
        body, name="in_proj", grid=(n_tiles,),
        in_specs=[head_s, xrow, _const((1, D_MODEL)), _const((D_MODEL, D_MODEL))],
        out_specs=[half, half], out_shape=[jax.ShapeDtypeStruct((lp, D_SSM), F32)] * 2,
        compiler_params=_params(("parallel",), VMEM_LIMIT),
    )(head, x, g1, w_in)


def _out_proj(head, x, ys, yp, gs, gp, w_out):
    lp = ys.shape[0]

    def body(head_ref, x_ref, ys_ref, yp_ref, gs_ref, gp_ref, w_ref, h1_ref):
        h0 = jnp.where(pl.program_id(0) == 0, head_ref[...], x_ref[...])
        ys_t, yp_t = ys_ref[...], yp_ref[...]
        ms = ys_t * _rstd(ys_t) * gs_ref[...]
        mp = yp_t * _rstd(yp_t) * gp_ref[...]
        h1_ref[...] = h0 + _dot(ms, w_ref[pl.ds(0, D_SSM), :]) + _dot(mp, w_ref[pl.ds(D_SSM, D_SSM), :])

    head_s, xrow, full, half = _row_specs()
    return pl.pallas_call(
        body, name="out_proj", grid=(lp // TM,),
        in_specs=[head_s, xrow, half, half, _const((1, D_SSM)), _const((1, D_SSM)), _const((D_MODEL, D_MODEL))],
        out_specs=full, out_shape=jax.ShapeDtypeStruct((lp, D_MODEL), F32),
        compiler_params=_params(("parallel",), VMEM_LIMIT),
    )(head, x, ys, yp, gs, gp, w_out)


def _load_weights(hbm_refs, vmem_refs, sems):
    @pl.when(pl.program_id(0) == 0)
    def _():
        copies = [pltpu.make_async_copy(h, v, sems.at[n]) for n, (h, v) in enumerate(zip(hbm_refs, vmem_refs))]
        for cp in copies:
            cp.start()
        for cp in copies:
            cp.wait()


def _ffn_scratch():
    return [pltpu.VMEM((D_FF, D_MODEL), BF16)] * 3 + [pltpu.SemaphoreType.DMA((3,))]


def _ff_tile(t):
    return pl.ds(t * FF_TILE, FF_TILE)


def _ffn_forward(h1, g2, wg_t, wu_t, wd):
    lp = h1.shape[0]

    def body(h1_ref, g_ref, wg_hbm, wu_hbm, wd_hbm, ab_ref, n2_ref, h2_ref, wg, wu, wdn, sems):
        _load_weights((wg_hbm, wu_hbm, wd_hbm), (wg, wu, wdn), sems)
        h1_t = h1_ref[...]
        n2 = (h1_t * _rstd(h1_t) * g_ref[...]).astype(BF16)
        n2_ref[...] = n2
        acc = h1_t
        for t in range(D_FF // FF_TILE):
            a = _dot_nt(n2, wg[_ff_tile(t), :])
            b = _dot_nt(n2, wu[_ff_tile(t), :])
            ab_ref[:, _ff_tile(t)] = a.astype(BF16)
            ab_ref[:, pl.ds(D_FF + t * FF_TILE, FF_TILE)] = b.astype(BF16)
            acc = acc + _dot(a * _sigmoid(a) * b, wdn[_ff_tile(t), :])
        h2_ref[...] = acc

    _, _, full, _ = _row_specs()
    wide = pl.BlockSpec((TM, 2 * D_FF), lambda i: (i, 0))
    half_width = pl.BlockSpec((TM, D_MODEL), lambda i: (i, 0))
    return pl.pallas_call(
        body, name="ffn_forward", grid=(lp // TM,),
        in_specs=[full, _const((1, D_MODEL)), ANY, ANY, ANY], out_specs=[wide, half_width, full],
        out_shape=[jax.ShapeDtypeStruct((lp, 2 * D_FF), BF16), jax.ShapeDtypeStruct((lp, D_MODEL), BF16),
                   jax.ShapeDtypeStruct((lp, D_MODEL), F32)],
        scratch_shapes=_ffn_scratch(), compiler_params=_params(("arbitrary",), VMEM_LIMIT),
    )(h1, g2, wg_t, wu_t, wd)


def _ffn_backward(h2, target, h1, ab, gf, g2, wg_t, wu_t, wd):
    lp = h1.shape[0]

    def body(h2_ref, t_ref, h1_ref, ab_ref, gf_ref, g2_ref, wg_hbm, wu_hbm, wd_hbm,
             dh1_ref, xt_ref, dh2_ref, loss_ref, dgf_ref, dg2_ref, wg, wu, wdn, sems):
        i = pl.program_id(0)
        _load_weights((wg_hbm, wu_hbm, wd_hbm), (wg, wu, wdn), sems)

        @pl.when(i == 0)
        def _():
            loss_ref[...] = jnp.zeros(loss_ref.shape, F32)
            dgf_ref[...] = jnp.zeros(dgf_ref.shape, F32)
            dg2_ref[...] = jnp.zeros(dg2_ref.shape, F32)

        h2_t = h2_ref[...]
        rf = _rstd(h2_t)
        xf = h2_t * rf
        diff = jnp.where(i == 0, 0.0, xf * gf_ref[...] - t_ref[...])
        loss_ref[...] += 0.5 * jnp.sum(diff * diff) / D_MODEL
        dh2, dgf = _rms_bwd(diff / D_MODEL, xf, rf, gf_ref[...])
        dgf_ref[...] += dgf
        dh2_b = dh2.astype(BF16)
        dh2_ref[...] = dh2_b

        dn2 = jnp.zeros((TM, D_MODEL), F32)
        for t in range(D_FF // FF_TILE):
            dff = _dot_nt(dh2_b, wdn[_ff_tile(t), :])
            a = ab_ref[:, _ff_tile(t)].astype(F32)
            b = ab_ref[:, pl.ds(D_FF + t * FF_TILE, FF_TILE)].astype(F32)
            sg = _sigmoid(a)
            silu = a * sg
            da = dff * b * sg * (1.0 + a * (1.0 - sg))
            db = dff * silu
            for part, val in enumerate((da, db, silu * b)):
                xt_ref[pl.ds(part * D_FF + t * FF_TILE, FF_TILE), :] = val.T.astype(BF16)
            dn2 = dn2 + _dot(da, wg[_ff_tile(t), :]) + _dot(db, wu[_ff_tile(t), :])

        h1_t = h1_ref[...]
        r2 = _rstd(h1_t)
        dx, dg2 = _rms_bwd(dn2, h1_t * r2, r2, g2_ref[...])
        dg2_ref[...] += dg2
        dh1_ref[...] = dh2 + dx

    _, xrow, full, _ = _row_specs()
    wide = pl.BlockSpec((TM, 2 * D_FF), lambda i: (i, 0))
    half_width = pl.BlockSpec((TM, D_MODEL), lambda i: (i, 0))
    vec = _const((1, D_MODEL))
    return pl.pallas_call(
        body, name="ffn_backward", grid=(lp // TM,),
        in_specs=[full, xrow, full, wide, vec, vec, ANY, ANY, ANY],
        out_specs=[full, pl.BlockSpec((3 * D_FF, TM), lambda i: (0, i)), half_width, _const((1, PACK_LANES)), vec, vec],
        out_shape=[jax.ShapeDtypeStruct((lp, D_MODEL), F32),
                   jax.ShapeDtypeStruct((3 * D_FF, lp), BF16),
                   jax.ShapeDtypeStruct((lp, D_MODEL), BF16),
                   jax.ShapeDtypeStruct((1, PACK_LANES), F32),
                   jax.ShapeDtypeStruct((1, D_MODEL), F32), jax.ShapeDtypeStruct((1, D_MODEL), F32)],
        scratch_shapes=_ffn_scratch(), compiler_params=_params(("arbitrary",), VMEM_LIMIT),
    )(h2, target, h1, ab, gf, g2, wg_t, wu_t, wd)


def _ffn_wgrad(xt, n2, dh2):
    lp = n2.shape[0]
    rows = lp // WGRAD_STEPS
    n_tiles = 3 * D_FF // FF_TILE
    shards_per_tile = FF_TILE // FF_SHARD
    gate_up_tiles = 2 * D_FF // FF_TILE

    def body(xt_ref, n2_ref, dh2_ref, out_ref, acc):
        q, k = pl.program_id(0), pl.program_id(1)

        @pl.when(k == 0)
        def _():
            acc[...] = jnp.zeros(acc.shape, F32)

        @pl.when(q < gate_up_tiles)
        def _():
            acc[...] += jnp.dot(xt_ref[...], n2_ref[...], preferred_element_type=F32)

        @pl.when(q >= gate_up_tiles)
        def _():
            acc[...] += jnp.dot(xt_ref[...], dh2_ref[...], preferred_element_type=F32)

        @pl.when(k == pl.num_programs(1) - 1)
        def _():
            for s in range(shards_per_tile):
                out_ref[s] = acc[pl.ds(s * FF_SHARD, FF_SHARD), :].astype(BF16)

    tiles_per_matrix = D_FF // FF_TILE
    return pl.pallas_call(
        body, name="ffn_wgrad", grid=(n_tiles, WGRAD_STEPS),
        in_specs=[pl.BlockSpec((FF_TILE, rows), lambda q, k: (q, k)),
                  pl.BlockSpec((rows, D_MODEL), lambda q, k: (jnp.where(q < gate_up_tiles, k, 0), 0)),
                  pl.BlockSpec((rows, D_MODEL), lambda q, k: (jnp.where(q < gate_up_tiles, 0, k), 0))],
        out_specs=pl.BlockSpec((shards_per_tile, None, FF_SHARD, D_MODEL),
                               lambda q, k: (q % tiles_per_matrix, q // tiles_per_matrix, 0, 0)),
        out_shape=jax.ShapeDtypeStruct((N_DEV, 3, FF_SHARD, D_MODEL), BF16),
        scratch_shapes=[pltpu.VMEM((FF_TILE, D_MODEL), F32)],
        compiler_params=_params(("parallel", "arbitrary"), VMEM_LIMIT),
    )(xt, n2, dh2)


def _out_proj_backward(dh1, ys, yp, gs, gp, w_out):
    lp = ys.shape[0]

    def body(dh1_ref, ys_ref, yp_ref, gs_ref, gp_ref, w_ref, dys_ref, dyp_ref, dgs_ref, dgp_ref, dw_out, dw_ref):
        @pl.when(pl.program_id(0) == 0)
        def _():
            dgs_ref[...] = jnp.zeros(dgs_ref.shape, F32)
            dgp_ref[...] = jnp.zeros(dgp_ref.shape, F32)
            dw_ref[...] = jnp.zeros(dw_ref.shape, F32)

        dh1_b = dh1_ref[...].astype(BF16)
        dmix = _dot_nt(dh1_b, w_ref[...])
        for y_ref, g_ref, dy_ref, dg_ref, lo in ((ys_ref, gs_ref, dys_ref, dgs_ref, 0),
                                                 (yp_ref, gp_ref, dyp_ref, dgp_ref, D_SSM)):
            y_t = y_ref[...]
            r = _rstd(y_t)
            xhat = y_t * r
            dy, dg = _rms_bwd(dmix[:, lo:lo + D_SSM], xhat, r, g_ref[...])
            dy_ref[...] = dy
            dg_ref[...] += dg
            dw_ref[pl.ds(lo, D_SSM), :] += _dot_tn(xhat * g_ref[...], dh1_b)

        @pl.when(pl.program_id(0) == pl.num_programs(0) - 1)
        def _():
            dw_out[...] = dw_ref[...].astype(BF16)

    _, _, full, half = _row_specs()
    vec = _const((1, D_SSM))
    return pl.pallas_call(
        body, name="out_proj_backward", grid=(lp // TM,),
        in_specs=[full, half, half, vec, vec, _const((D_MODEL, D_MODEL))],
        out_specs=[half, half, vec, vec, _const((D_MODEL, D_MODEL))],
        out_shape=[jax.ShapeDtypeStruct((lp, D_SSM), F32)] * 2 + [jax.ShapeDtypeStruct((1, D_SSM), F32)] * 2
        + [jax.ShapeDtypeStruct((D_MODEL, D_MODEL), BF16)],
        scratch_shapes=[pltpu.VMEM((D_MODEL, D_MODEL), F32)],
        compiler_params=_params(("arbitrary",), VMEM_LIMIT),
    )(dh1, ys, yp, gs, gp, w_out)


def _in_proj_backward(head, x, du, dv, dh1, g1, w_in):
    lp = du.shape[0]

    def body(head_ref, x_ref, du_ref, dv_ref, dh1_ref, g_ref, w_ref, dx_ref, dhead_ref, dg_ref, dw_out, dw_ref):
        i = pl.program_id(0)

        @pl.when(i == 0)
        def _():
            dg_ref[...] = jnp.zeros(dg_ref.shape, F32)
            dw_ref[...] = jnp.zeros(dw_ref.shape, F32)

        h0 = jnp.where(i == 0, head_ref[...], x_ref[...])
        r = _rstd(h0)
        xhat = h0 * r
        n1 = (xhat * g_ref[...]).astype(BF16)
        du_b, dv_b = du_ref[...].astype(BF16), dv_ref[...].astype(BF16)
        dn1 = _dot_nt(du_b, w_ref[:, pl.ds(0, D_SSM)]) + _dot_nt(dv_b, w_ref[:, pl.ds(D_SSM, D_SSM)])
        dx, dg = _rms_bwd(dn1, xhat, r, g_ref[...])
        dg_ref[...] += dg
        dh0 = dh1_ref[...] + dx
        dx_ref[...] = dh0

        @pl.when(i == 0)
        def _():
            dhead_ref[...] = dh0

        dw_ref[:, pl.ds(0, D_SSM)] += _dot_tn(n1, du_b)
        dw_ref[:, pl.ds(D_SSM, D_SSM)] += _dot_tn(n1, dv_b)

        @pl.when(i == pl.num_programs(0) - 1)
        def _():
            dw_out[...] = dw_ref[...].astype(BF16)

    head_s, xrow, full, half = _row_specs()
    vec = _const((1, D_MODEL))
    mat = _const((D_MODEL, D_MODEL))
    return pl.pallas_call(
        body, name="in_proj_backward", grid=(lp // TM,),
        in_specs=[head_s, xrow, half, half, full, vec, mat],
        out_specs=[xrow, head_s, vec, mat],
        out_shape=[jax.ShapeDtypeStruct(x.shape, F32), jax.ShapeDtypeStruct((HEAD, D_MODEL), F32),
                   jax.ShapeDtypeStruct((1, D_MODEL), F32), jax.ShapeDtypeStruct((D_MODEL, D_MODEL), BF16)],
        scratch_shapes=[pltpu.VMEM((D_MODEL, D_MODEL), F32)],
        compiler_params=_params(("arbitrary",), VMEM_LIMIT),
    )(head, x, du, dv, dh1, g1, w_in)


def _permute_rows(a):
    lp, n = a.shape
    return a.reshape(SUBLANES, lp // SUBLANES, n).transpose(1, 0, 2).reshape(lp, n)


def _unpermute_rows(a):
    lp, n = a.shape
    return a.reshape(lp // SUBLANES, SUBLANES, n).transpose(1, 0, 2).reshape(lp, n)


def _pack(parts, dtype):
    rows = []
    for p in parts:
        flat = p.reshape(-1).astype(dtype)
        pad = (-flat.shape[0]) % PACK_UNIT
        rows.append(jnp.pad(flat, (0, pad)).reshape(-1, PACK_LANES))
    n_rows = sum(r.shape[0] for r in rows)
    if n_rows % 16:
        rows.append(jnp.zeros((8, PACK_LANES), dtype))
    return jnp.concatenate(rows, axis=0)


def _as2d(a):
    return a.reshape(-1, a.shape[-1])


def _unpack(packed, shapes):
    out, row = [], 0
    for shape in shapes:
        size = 1
        for s in shape:
            size *= s
        n_rows = -(-size // PACK_UNIT) * 8
        out.append(packed[row:row + n_rows].reshape(-1)[:size].reshape(shape))
        row += n_rows
    return out


def _to_view(name, a):
    if name in ("ssm_b_re", "ssm_b_im"):
        return a[0].transpose(0, 2, 1).reshape(-1, SSM_STATE)
    if name in ("ssm_d", "ssm_glu_b"):
        return a[0].T
    if name == "ssm_glu_w":
        return a[0].transpose(1, 2, 0).reshape(-1, SSM_GROUPS)
    return _as2d(a)


def _from_view(name, r, shape):
    if name in ("ssm_b_re", "ssm_b_im"):
        return r.reshape(SSM_GROUPS, SSM_GROUP, SSM_STATE).transpose(0, 2, 1).reshape(shape)
    if name in ("ssm_d", "ssm_glu_b"):
        return r.T.reshape(shape)
    if name == "ssm_glu_w":
        return r.reshape(SSM_GROUP, SSM_GROUP, SSM_GROUPS).transpose(2, 0, 1).reshape(shape)
    return r.reshape(shape)


def kernel(x, meta_tokens, norm1_g, w_in, ssm_lambda_re, ssm_lambda_im, ssm_log_step, ssm_b_re, ssm_b_im, ssm_c_re, ssm_c_im, ssm_d, ssm_glu_w, ssm_glu_b, ssm_norm_g, pool_w, pool_scale, pool_norm_g, w_out, norm2_g, w_gate, w_up, w_down, final_norm_g, loss_target, m_meta_tokens, m_norm1_g, m_w_in, m_ssm_lambda_re, m_ssm_lambda_im, m_ssm_log_step, m_ssm_b_re, m_ssm_b_im, m_ssm_c_re, m_ssm_c_im, m_ssm_d, m_ssm_glu_w, m_ssm_glu_b, m_ssm_norm_g, m_pool_w, m_pool_scale, m_pool_norm_g, m_w_out, m_norm2_g, m_w_gate, m_w_up, m_w_down, m_final_norm_g, v_meta_tokens, v_norm1_g, v_w_in, v_ssm_lambda_re, v_ssm_lambda_im, v_ssm_log_step, v_ssm_b_re, v_ssm_b_im, v_ssm_c_re, v_ssm_c_im, v_ssm_d, v_ssm_glu_w, v_ssm_glu_b, v_ssm_norm_g, v_pool_w, v_pool_scale, v_pool_norm_g, v_w_out, v_norm2_g, v_w_gate, v_w_up, v_w_down, v_final_norm_g):
    given = dict(locals())
    weights = {n: given[n] for n in WEIGHT_NAMES}
    n_meta = meta_tokens.shape[0]
    me = 4 * lax.axis_index("x") + 2 * lax.axis_index("y") + lax.axis_index("c")

    shard_rows = w_in.shape[1]
    first = [w_in[0].astype(BF16), meta_tokens]
    first_make = _push_copies((SIBLING,) + CHIP_PEERS, [False, False])
    first_x, first_token = _split_start(first + [_landing(s, False) for s in first], first_make,
                                        2 * (1 + len(CHIP_PEERS)), "gather_w_in_start")

    xs = x[0]
    tgt = loss_target[0]
    first_row = HEAD - n_meta
    g1, g2, gf = norm1_g, norm2_g, final_norm_g.reshape(1, D_MODEL)
    gs, gp = ssm_norm_g, pool_norm_g

    lam_re, lam_im = ssm_lambda_re[0] + first_token[:1, :1], ssm_lambda_im[0]
    log_step = ssm_log_step[0].reshape(SSM_GROUPS, 1)
    b_re = ssm_b_re[0].transpose(0, 2, 1)
    b_im = ssm_b_im[0].transpose(0, 2, 1)
    abr, abi, zr, zi = _s5_disc_a(lam_re, lam_im, log_step)
    zr_col, zi_col = zr.reshape(SSM_GROUPS, 1, SSM_STATE), zi.reshape(SSM_GROUPS, 1, SSM_STATE)
    bbr, bbi = _s5_disc_b(zr_col, zi_col, b_re, b_im)
    s5_consts = (abr.reshape(1, -1), abi.reshape(1, -1), bbr, bbi, ssm_c_re[0], ssm_c_im[0],
                 ssm_d[0].reshape(1, D_SSM), ssm_glu_w[0], ssm_glu_b[0].reshape(1, D_SSM))
    pool_sc = pool_scale[0].reshape(POOL_GROUPS, 1, POOL_DIM)

    first_landed, _ = _split_wait(first_x, first_make, bbr, "gather_w_in_wait")
    first_fwd_make = _forward_copies(2)
    first_fwd, _ = _split_start(list(first_landed[2:]), first_fwd_make, 2 * len(CHIP_PEERS), "gather_w_in_forward_start")
    (w_in_all, meta_all), first_done = _split_wait(first_fwd, first_fwd_make, bbr, "gather_w_in_forward_wait")
    w_in_all = w_in_all.reshape(D_MODEL, D_MODEL)
    meta_full = meta_all.transpose(1, 0, 2).reshape(n_meta, D_MODEL)
    head = jnp.concatenate([jnp.zeros((HEAD - n_meta, D_MODEL), F32), meta_full], axis=0)
    shards = [(w_out[0] + first_done[:1, :1]).astype(BF16), w_gate[0].T.astype(BF16), w_up[0].T.astype(BF16),
              w_down[0].astype(BF16)]
    n_big = len(shards)
    gather_make = _push_copies((SIBLING,) + CHIP_PEERS, [False] * n_big)
    gather, gather_token = _split_start(shards + [_landing(s, False) for s in shards], gather_make,
                                        n_big * (1 + len(CHIP_PEERS)), "gather_start")

    u, v = _in_proj(head, xs, g1 + gather_token[:1, :1], w_in_all)
    u_p = _permute_rows(u)
    ys_p = _s5_forward(u_p, *s5_consts)
    landed, _ = _split_wait(gather, gather_make, ys_p, "gather_wait")
    forward_make = _forward_copies(n_big)
    forward, forward_token = _split_start(list(landed[n_big:]), forward_make, n_big * len(CHIP_PEERS),
                                          "gather_forward_start")
    ys = _unpermute_rows(ys_p)
    yp = _pool_forward(v, pool_w[0], pool_sc + forward_token[:1, :1], first_row)
    (w_out_all, wg_t, wu_t, wd_all), _ = _split_wait(forward, forward_make, yp, "gather_forward_wait")
    w_out_all = w_out_all.reshape(D_MODEL, D_MODEL)
    ffn_weights = [w.reshape(D_FF, D_MODEL) for w in (wg_t, wu_t, wd_all)]
    h1 = _out_proj(head, xs, ys, yp, gs, gp, w_out_all)
    ab, n2, h2 = _ffn_forward(h1, g2, *ffn_weights)

    dh1, xt, dh2, loss_part, d_gf, d_g2 = _ffn_backward(h2, tgt, h1, ab, gf, g2, *ffn_weights)
    d_ffn = _ffn_wgrad(xt, n2, dh2)
    ffn_make = _push_copies(ALL_PEERS, [True])
    ffn_x, ffn_token = _split_start([d_ffn, _landing(d_ffn, True)], ffn_make, len(ALL_PEERS), "ffn_grad_start")
    dys, dyp, d_gs, d_gp, d_wout = _out_proj_backward(dh1, ys, yp, gs + ffn_token[:1, :1], gp, w_out_all)
    dv, d_pool_w, d_pool_sc = _pool_backward(v, dyp, pool_w[0], pool_sc, first_row)
    (du_p, d_ar, d_ai, d_bbr, d_bbi, d_c_re, d_c_im, d_d, d_glu, d_glub) = _s5_backward(
        u_p, _permute_rows(dys), *s5_consts)
    du = _unpermute_rows(du_p)

    d_zr, d_zi, d_b_re, d_b_im = _s5_disc_b_bwd(zr_col, zi_col, b_re, b_im, d_bbr, d_bbi)
    d_lam_re, d_lam_im, d_log_step = _s5_disc_a_bwd(
        lam_re, lam_im, log_step,
        (d_ar.reshape(SSM_GROUPS, SSM_STATE), d_ai.reshape(SSM_GROUPS, SSM_STATE),
         d_zr.reshape(SSM_GROUPS, SSM_STATE), d_zi.reshape(SSM_GROUPS, SSM_STATE)))
    groups_last = lambda row: row.reshape(SSM_GROUPS, SSM_GROUP).T
    small_grads = {
        "ssm_lambda_re": d_lam_re, "ssm_lambda_im": d_lam_im, "ssm_log_step": d_log_step.reshape(1, SSM_GROUPS),
        "ssm_b_re": d_b_re.reshape(-1, SSM_STATE), "ssm_b_im": d_b_im.reshape(-1, SSM_STATE),
        "ssm_c_re": d_c_re.reshape(-1, SSM_STATE), "ssm_c_im": d_c_im.reshape(-1, SSM_STATE),
        "ssm_d": groups_last(d_d), "ssm_glu_w": d_glu.transpose(1, 2, 0).reshape(-1, SSM_GROUPS),
        "ssm_glu_b": groups_last(d_glub),
        "ssm_norm_g": d_gs, "pool_w": d_pool_w.reshape(-1, POOL_DIM), "pool_scale": d_pool_sc.reshape(-1, POOL_DIM),
        "pool_norm_g": d_gp, "norm2_g": d_g2,
    }

    early_names = SMALL_NAMES[1:-1]
    early_pack = _pack([small_grads[n] for n in early_names], BF16)
    d_wout = d_wout.reshape(N_DEV, shard_rows, D_MODEL)
    early_make = _push_copies(ALL_PEERS, [True, False])
    early_x, early_token = _split_start([d_wout, early_pack, _landing(d_wout, True), _landing(early_pack, False)],
                                        early_make, 2 * len(ALL_PEERS), "early_grad_start")
    d_x, d_head, d_g1, d_win = _in_proj_backward(head, xs, du, dv, dh1, g1 + early_token[:1, :1], w_in_all)
    (_, r_ffn), _ = _split_wait(ffn_x, ffn_make, d_g1, "ffn_grad_wait")
    (_, _, r_wout, r_early), _ = _split_wait(early_x, early_make, d_g1, "early_grad_wait")
    d_win = d_win.reshape(N_DEV, shard_rows, D_MODEL)
    late_pack = _pack([d_g1, d_gf, d_head[first_row:], loss_part], F32)
    late_make = _push_copies(ALL_PEERS, [True, False])
    late_x, late_token = _split_start([d_win, late_pack, _landing(d_win, True), _landing(late_pack, False)],
                                      late_make, 2 * len(ALL_PEERS), "late_grad_start")

    results = {}
    res_gate = _adamw_part(r_ffn, 0, w_gate[0].T + late_token[:1, :1], m_w_gate[0].T, v_w_gate[0].T, "adamw_w_gate")
    res_up = _adamw_part(r_ffn, 1, w_up[0].T, m_w_up[0].T, v_w_up[0].T, "adamw_w_up")
    results["w_gate"] = [r.T for r in res_gate]
    results["w_up"] = [r.T for r in res_up]
    results["w_down"] = _adamw_part(r_ffn, 2, w_down[0], m_w_down[0], v_w_down[0], "adamw_w_down")
    results["w_out"] = _adamw(r_wout, w_out[0], m_w_out[0], v_w_out[0], shard_rows, "adamw_w_out")
    done = sum(res[1][:1, :1] for res in (res_gate, res_up, results["w_down"], results["w_out"]))
    (_, _, r_win, r_late), _ = _split_wait(late_x, late_make, done, "late_grad_wait")
    results["w_in"] = _adamw(r_win, w_in[0], m_w_in[0], v_w_in[0], shard_rows, "adamw_w_in")

    sum_early, sum_late = _reduce_slots([r_early, r_late], "small_grad_sums")
    views = lambda prefix: [_to_view(n, given[prefix + n]) for n in SMALL_NAMES]
    w_views = views("")
    g_views = _unpack(sum_early, [w.shape for w in w_views[1:-1]])
    g_norm1, g_final, g_meta_all, loss_row = _unpack(
        sum_late, [norm1_g.shape, (1, D_MODEL), (n_meta, D_MODEL), (1, PACK_LANES)])
    g_views = [g_norm1] + g_views + [g_final]
    res_small = _adamw_many(g_views, w_views, views("m_"), views("v_"), "adamw_small")
    for idx, n in enumerate(SMALL_NAMES):
        results[n] = [_from_view(n, part[idx], weights[n].shape) for part in (g_views,) + tuple(res_small)]
    shard_cols = meta_tokens.shape[1]
    g_meta = lax.dynamic_slice_in_dim(g_meta_all, me * shard_cols, shard_cols, axis=1)
    results["meta_tokens"] = _adamw(g_meta[None], meta_tokens, m_meta_tokens, v_meta_tokens, n_meta, "adamw_meta")

    out = [loss_row[0, 0], d_x[None]]
    for part in range(4):
        for n in WEIGHT_NAMES:
            out.append(results[n][part].reshape(weights[n].shape))
    return tuple(out)
```

```python
import jax
import jax.numpy as jnp
from jax import lax
from jax.experimental import pallas as pl
from jax.experimental.pallas import tpu as pltpu

F32 = jnp.float32
BF16 = jnp.bfloat16

N_DEV = 8
D_MODEL = 1024
D_SSM = 512
SSM_GROUP = 16
SSM_STATE = 64
SSM_GROUPS = 32
POOL_GROUPS = 4
POOL_DIM = 128
COL_U = 128
COL_S = 512
N_COL = D_SSM // COL_U
GROUPS_PER_COL = COL_U // SSM_GROUP
D_FF = 2816
FF_SHARD = D_FF // N_DEV
FF_TILE = D_FF // 2
TM = 256
WGRAD_STEPS = 2
HEAD = TM
SUBLANES = 8
SCAN_UNROLL = 4
POOL_HALO = 16
EPS = 1e-6
STEP_FLOOR = -1e-4
VMEM_LIMIT = 60 * 1024 * 1024

ADAM_LR = 0.001
ADAM_B1 = 0.9
ADAM_B2 = 0.999
ADAM_EPS = 1e-08
ADAM_WD = 0.01
ADAM_STEP = 10

MESH_ID = pl.DeviceIdType.MESH
ANY = pl.BlockSpec(memory_space=pl.ANY)

SMALL_NAMES = ("norm1_g", "ssm_lambda_re", "ssm_lambda_im", "ssm_log_step", "ssm_b_re", "ssm_b_im",
               "ssm_c_re", "ssm_c_im", "ssm_d", "ssm_glu_w", "ssm_glu_b", "ssm_norm_g", "pool_w",
               "pool_scale", "pool_norm_g", "norm2_g", "final_norm_g")
WEIGHT_NAMES = ("meta_tokens", "norm1_g", "w_in", "ssm_lambda_re", "ssm_lambda_im", "ssm_log_step",
                "ssm_b_re", "ssm_b_im", "ssm_c_re", "ssm_c_im", "ssm_d", "ssm_glu_w", "ssm_glu_b",
                "ssm_norm_g", "pool_w", "pool_scale", "pool_norm_g", "w_out", "norm2_g", "w_gate",
                "w_up", "w_down", "final_norm_g")
LANES = 128
PACK_LANES = LANES
PACK_UNIT = 8 * PACK_LANES


def _dot(a, b):
    return jnp.dot(a.astype(BF16), b.astype(BF16), preferred_element_type=F32)


def _dot_nt(a, b):
    return lax.dot_general(a.astype(BF16), b.astype(BF16), (((1,), (1,)), ((), ())), preferred_element_type=F32)


def _dot_tn(a, b):
    return lax.dot_general(a.astype(BF16), b.astype(BF16), (((0,), (0,)), ((), ())), preferred_element_type=F32)


def _sigmoid(x):
    return 1.0 / (1.0 + jnp.exp(-x))


def _rstd(x):
    return lax.rsqrt(jnp.mean(x * x, axis=-1, keepdims=True) + EPS)


def _rms_bwd(dy, xhat, r, g):
    dxh = dy * g
    dx = r * (dxh - xhat * jnp.mean(dxh * xhat, axis=-1, keepdims=True))
    return dx, jnp.sum(dy * xhat, axis=0, keepdims=True)


def _params(sem, vmem=None):
    return pltpu.CompilerParams(dimension_semantics=sem, vmem_limit_bytes=vmem)


def _const(shape):
    return pl.BlockSpec(shape, lambda *_: (0,) * len(shape))


def _xrow(i):
    return (jnp.maximum(i - 1, 0), 0)


HBM = pl.BlockSpec(memory_space=pltpu.HBM)
SEM = pl.BlockSpec(memory_space=pltpu.SEMAPHORE)
EFFECT = pltpu.SideEffectType.DATAFLOW_SIDE_EFFECTING
ALL_PEERS = tuple(range(1, N_DEV))
SIBLING = 1
CHIP_PEERS = (2, 4, 6)


def _me():
    return 4 * lax.axis_index("x") + 2 * lax.axis_index("y") + lax.axis_index("c")


def _peer(k):
    x, y, c = lax.axis_index("x"), lax.axis_index("y"), lax.axis_index("c")
    px = 1 - x if k & 4 else x
    py = 1 - y if k & 2 else y
    pc = 1 - c if k & 1 else c
    return (px, py, pc), 4 * px + 2 * py + pc


def _landing(arr, scatter):
    if scatter:
        own = lax.dynamic_index_in_dim(arr, _me(), 0, keepdims=False)
    else:
        own = arr
    return lax.dynamic_update_index_in_dim(lax.empty((N_DEV,) + own.shape, arr.dtype), own, _me(), 0)


def _push_copies(peers, scatter):
    n_arr = len(scatter)

    def make(refs, send_sems, recv_sems):
        copies = []
        for a in range(n_arr):
            for i, k in enumerate(peers):
                peer_id, peer = _peer(k)
                sem = a * len(peers) + i
                copies.append(pltpu.make_async_remote_copy(
                    src_ref=refs[a].at[peer] if scatter[a] else refs[a], dst_ref=refs[n_arr + a].at[_me()],
                    send_sem=send_sems.at[sem], recv_sem=recv_sems.at[sem],
                    device_id=peer_id, device_id_type=MESH_ID))
        return copies
    return make


def _forward_copies(n_arr):
    def make(refs, send_sems, recv_sems):
        copies = []
        sibling_id, _ = _peer(SIBLING)
        for a in range(n_arr):
            for i, k in enumerate(CHIP_PEERS):
                slot = refs[a].at[_peer(k)[1]]
                sem = a * len(CHIP_PEERS) + i
                copies.append(pltpu.make_async_remote_copy(
                    src_ref=slot, dst_ref=slot, send_sem=send_sems.at[sem], recv_sem=recv_sems.at[sem],
                    device_id=sibling_id, device_id_type=MESH_ID))
        return copies
    return make


def _split_start(operands, make, n_sem, name):
    n_op = len(operands)

    def body(*refs):
        for cp in make(refs[:n_op], refs[n_op], refs[n_op + 1]):
            cp.start()
        refs[-1][...] = jnp.zeros(refs[-1].shape, F32)

    out = pl.pallas_call(
        body, name=name,
        out_shape=(pltpu.SemaphoreType.DMA((n_sem,)), pltpu.SemaphoreType.DMA((n_sem,)),
                   *[pltpu.HBM(t.shape, t.dtype) for t in operands], jax.ShapeDtypeStruct((8, PACK_LANES), F32)),
        in_specs=[HBM] * n_op, out_specs=(SEM, SEM, *[HBM] * n_op, pl.BlockSpec(memory_space=pltpu.VMEM)),
        input_output_aliases={i: 2 + i for i in range(n_op)},
        compiler_params=pltpu.CompilerParams(has_side_effects=EFFECT),
    )(*[pltpu.with_memory_space_constraint(t, pltpu.HBM) for t in operands])
    return out[:-1], out[-1]


def _split_wait(started, make, after, name):
    send_sems, recv_sems, thru = started[0], started[1], started[2:]
    n_op = len(thru)

    def body(*refs):
        for cp in make(refs[:n_op], refs[n_op], refs[n_op + 1]):
            cp.wait_send()
            cp.wait_recv()
        refs[-1][...] = jnp.zeros(refs[-1].shape, F32)

    out = pl.pallas_call(
        body, name=name,
        out_shape=(*[pltpu.HBM(t.shape, t.dtype) for t in thru], jax.ShapeDtypeStruct((8, PACK_LANES), F32)),
        in_specs=[HBM] * n_op + [SEM, SEM, ANY], out_specs=(*[HBM] * n_op, pl.BlockSpec(memory_space=pltpu.VMEM)),
        input_output_aliases={i: i for i in range(n_op)},
        compiler_params=pltpu.CompilerParams(has_side_effects=EFFECT),
    )(*thru, send_sems, recv_sems, after)
    return out[:-1], out[-1]


def _adamw_math(g, w, m, v):
    nm = ADAM_B1 * m + (1.0 - ADAM_B1) * g
    nv = ADAM_B2 * v + (1.0 - ADAM_B2) * (g * g)
    m_hat = nm / (1.0 - ADAM_B1 ** ADAM_STEP)
    v_hat = nv / (1.0 - ADAM_B2 ** ADAM_STEP)
    return -ADAM_LR * (m_hat / (jnp.sqrt(v_hat) + ADAM_EPS) + ADAM_WD * w), nm, nv


def _sum_slots(s_ref):
    g = s_ref[0].astype(F32)
    for s in range(1, s_ref.shape[0]):
        g = g + s_ref[s].astype(F32)
    return g


def _adamw(slots, w, m, v, tile_rows, name):
    n, rows, cols = slots.shape

    def body(s_ref, w_ref, m_ref, v_ref, g_ref, d_ref, nm_ref, nv_ref):
        g = _sum_slots(s_ref)
        g_ref[...] = g
        d_ref[...], nm_ref[...], nv_ref[...] = _adamw_math(g, w_ref[...], m_ref[...], v_ref[...])

    tile = pl.BlockSpec((tile_rows, cols), lambda i: (i, 0))
    return pl.pallas_call(
        body, name=name, grid=(rows // tile_rows,),
        in_specs=[pl.BlockSpec((n, tile_rows, cols), lambda i: (0, i, 0)), tile, tile, tile],
        out_specs=[tile] * 4, out_shape=[jax.ShapeDtypeStruct((rows, cols), F32)] * 4,
        compiler_params=_params(("parallel",), VMEM_LIMIT),
    )(slots, w, m, v)


def _adamw_part(slots, part, w, m, v, name):
    n, _, rows, cols = slots.shape
    tile_cols = 256

    def body(s_ref, w_ref, m_ref, v_ref, g_ref, d_ref, nm_ref, nv_ref):
        g = _sum_slots(s_ref)
        g_ref[...] = g
        d_ref[...], nm_ref[...], nv_ref[...] = _adamw_math(g, w_ref[...], m_ref[...], v_ref[...])

    tile = pl.BlockSpec((rows, tile_cols), lambda i: (0, i))
    return pl.pallas_call(
        body, name=name, grid=(cols // tile_cols,),
        in_specs=[pl.BlockSpec((n, None, rows, tile_cols), lambda i: (0, part, 0, i)), tile, tile, tile],
        out_specs=[tile] * 4, out_shape=[jax.ShapeDtypeStruct((rows, cols), F32)] * 4,
        compiler_params=_params(("parallel",), VMEM_LIMIT),
    )(slots, w, m, v)


def _reduce_slots(slot_arrays, name):
    def body(*refs):
        n_arr = len(refs) // 2
        for s_ref, o_ref in zip(refs[:n_arr], refs[n_arr:]):
            o_ref[...] = _sum_slots(s_ref)
    return pl.pallas_call(
        body, name=name, out_shape=[jax.ShapeDtypeStruct(s.shape[1:], F32) for s in slot_arrays],
        compiler_params=_params(None, VMEM_LIMIT))(*slot_arrays)


def _adamw_many(grads, ws, ms, vs, name):
    n = len(grads)

    def body(*refs):
        ins, outs = refs[:4 * n], refs[4 * n:]
        for i in range(n):
            g, w, m, v = (ins[j * n + i][...] for j in range(4))
            outs[i][...], outs[n + i][...], outs[2 * n + i][...] = _adamw_math(g, w, m, v)

    out = pl.pallas_call(
        body, name=name, out_shape=[jax.ShapeDtypeStruct(w.shape, F32) for w in ws] * 3,
        compiler_params=_params(None, VMEM_LIMIT))(*grads, *ws, *ms, *vs)
    return out[:n], out[n:2 * n], out[2 * n:]


def _disc_a(lam_re, lam_im, log_step):
    lr = jnp.minimum(lam_re, STEP_FLOOR)
    step = jnp.exp(log_step)
    mag = jnp.exp(lr * step)
    ang = lam_im * step
    abr = mag * jnp.cos(ang)
    abi = mag * jnp.sin(ang)
    nr = abr - 1.0
    den = lr * lr + lam_im * lam_im
    cr = (nr * lr + abi * lam_im) / den
    ci = (abi * lr - nr * lam_im) / den
    return abr, abi, cr, ci


def _disc_b(cr, ci, b_re, b_im):
    return cr * b_re - ci * b_im, cr * b_im + ci * b_re


def _s5_disc_a(lam_re, lam_im, log_step):
    def body(lr_ref, li_ref, ls_ref, *outs):
        for o, val in zip(outs, _disc_a(lr_ref[...], li_ref[...], ls_ref[...])):
            o[...] = val
    return pl.pallas_call(body, name="s5_disc_a", out_shape=[jax.ShapeDtypeStruct(lam_re.shape, F32)] * 4)(
        lam_re, lam_im, log_step)


def _s5_disc_a_bwd(lam_re, lam_im, log_step, cts):
    def body(lr_ref, li_ref, ls_ref, c0, c1, c2, c3, dlr_ref, dli_ref, dls_ref):
        _, vjp = jax.vjp(_disc_a, lr_ref[...], li_ref[...], ls_ref[...])
        dlr, dli, dls = vjp((c0[...], c1[...], c2[...], c3[...]))
        dlr_ref[...] = dlr
        dli_ref[...] = dli
        dls_ref[...] = dls
    return pl.pallas_call(
        body, name="s5_disc_a_bwd",
        out_shape=[jax.ShapeDtypeStruct(lam_re.shape, F32), jax.ShapeDtypeStruct(lam_re.shape, F32),
                   jax.ShapeDtypeStruct(log_step.shape, F32)])(lam_re, lam_im, log_step, *cts)


def _s5_disc_b(cr, ci, b_re, b_im):
    def body(cr_ref, ci_ref, br_ref, bi_ref, o_re, o_im):
        o_re[...], o_im[...] = _disc_b(cr_ref[...], ci_ref[...], br_ref[...], bi_ref[...])
    return pl.pallas_call(body, name="s5_disc_b", out_shape=[jax.ShapeDtypeStruct(b_re.shape, F32)] * 2)(
        cr, ci, b_re, b_im)


def _s5_disc_b_bwd(cr, ci, b_re, b_im, d_re, d_im):
    def body(cr_ref, ci_ref, br_ref, bi_ref, dr_ref, di_ref, dcr_ref, dci_ref, dbr_ref, dbi_ref):
        _, vjp = jax.vjp(_disc_b, cr_ref[...], ci_ref[...], br_ref[...], bi_ref[...])
        dcr_ref[...], dci_ref[...], dbr_ref[...], dbi_ref[...] = vjp((dr_ref[...], di_ref[...]))
    return pl.pallas_call(
        body, name="s5_disc_b_bwd",
        out_shape=[jax.ShapeDtypeStruct(cr.shape, F32)] * 2 + [jax.ShapeDtypeStruct(b_re.shape, F32)] * 2)(
            cr, ci, b_re, b_im, d_re, d_im)


def _cmul(ar, ai, br, bi):
    return ar * br - ai * bi, ar * bi + ai * br


def _cpow(ar, ai, n):
    rr, ri = jnp.ones_like(ar), jnp.zeros_like(ai)
    while n:
        if n & 1:
            rr, ri = _cmul(rr, ri, ar, ai)
        n >>= 1
        if n:
            ar, ai = _cmul(ar, ai, ar, ai)
    return rr, ri


def _tile_rows(i):
    if isinstance(i, int):
        return pl.ds(i * SUBLANES, SUBLANES)
    return pl.ds(pl.multiple_of(i * SUBLANES, SUBLANES), SUBLANES)


def _segment_scan(z_re, z_im, ar, ai, lseg, reverse, visit=None):
    shape = (SUBLANES, z_re.shape[1])
    half = lseg // 2
    arb = jnp.broadcast_to(ar, shape)
    aib = jnp.broadcast_to(ai, shape)
    zero = jnp.zeros(shape, F32)
    row = lax.broadcasted_iota(jnp.int32, shape, 0)

    def tiles(k):
        return (lseg - 1 - k, half - 1 - k) if reverse else (k, half + k)

    def advance(tile, sr, si):
        rows = _tile_rows(tile)
        nr, ni = _cmul(arb, aib, sr, si)
        return rows, nr + z_re[rows, :], ni + z_im[rows, :]

    def first_pass(k, carry):
        ta, tb = tiles(k)
        return advance(ta, carry[0], carry[1])[1:] + advance(tb, carry[2], carry[3])[1:]

    def unrolled(step):
        def body(it, carry):
            for j in range(SCAN_UNROLL):
                carry = step(it * SCAN_UNROLL + j, carry)
            return carry
        return body

    n_iter = half // SCAN_UNROLL
    fa_r, fa_i, fb_r, fb_i = lax.fori_loop(0, n_iter, unrolled(first_pass), (zero,) * 4)
    hr, hi = _cpow(arb, aib, half)
    pr, pi = _cmul(hr, hi, hr, hi)
    fr, fi = _cmul(hr, hi, fa_r, fa_i)
    fr, fi = fr + fb_r, fi + fb_i
    cr, ci = zero, zero
    for _ in range(SUBLANES - 1):
        tr, ti = _cmul(pr, pi, cr, ci)
        tr, ti = tr + fr, ti + fi
        if reverse:
            cr = jnp.where(row == SUBLANES - 1, 0.0, pltpu.roll(tr, SUBLANES - 1, 0))
            ci = jnp.where(row == SUBLANES - 1, 0.0, pltpu.roll(ti, SUBLANES - 1, 0))
        else:
            cr = jnp.where(row == 0, 0.0, pltpu.roll(tr, 1, 0))
            ci = jnp.where(row == 0, 0.0, pltpu.roll(ti, 1, 0))

    br, bi = _cmul(hr, hi, cr, ci)
    br, bi = br + fa_r, bi + fa_i

    def second_pass(k, carry, b_is_tile0=False):
        states, acc = list(carry[:4]), carry[4]
        for chain, tile in enumerate(tiles(k)):
            rows, nr, ni = advance(tile, states[2 * chain], states[2 * chain + 1])
            z_re[rows, :] = nr
            z_im[rows, :] = ni
            states[2 * chain], states[2 * chain + 1] = nr, ni
            if visit is not None:
                acc = visit(tile, nr, ni, acc, chain == 1 and b_is_tile0)
        return (*states, acc)

    acc0 = (zero, zero) if visit is not None else 0
    carry = lax.fori_loop(0, n_iter - 1, unrolled(second_pass), (cr, ci, br, bi, acc0))
    for k in range(half - SCAN_UNROLL, half - 1):
        carry = second_pass(k, carry)
    return second_pass(half - 1, carry, b_is_tile0=reverse)[4]


def _gelu(y):
    c = 0.7978845608028654
    return 0.5 * y * (1.0 + jnp.tanh(c * (y + 0.044715 * y * y * y)))


def _gelu_grad(y):
    c = 0.7978845608028654
    th = jnp.tanh(c * (y + 0.044715 * y * y * y))
    return 0.5 * (1.0 + th) + 0.5 * y * (1.0 - th * th) * c * (1.0 + 3.0 * 0.044715 * y * y)


def _s5_specs(lp):
    col_u = pl.BlockSpec((lp, COL_U), lambda j: (0, j))
    row_u = pl.BlockSpec((1, COL_U), lambda j: (0, j))
    row_s = pl.BlockSpec((1, COL_S), lambda j: (0, j))
    bc_blk = pl.BlockSpec((GROUPS_PER_COL, SSM_GROUP, SSM_STATE), lambda j: (j, 0, 0))
    glu_blk = pl.BlockSpec((GROUPS_PER_COL, SSM_GROUP, SSM_GROUP), lambda j: (j, 0, 0))
    return col_u, row_u, row_s, bc_blk, glu_blk


def _s5_block_diag_scratch():
    return ([pltpu.VMEM((COL_U, COL_S), BF16)] * 4 + [pltpu.VMEM((COL_U, COL_U), BF16)]
            + [pltpu.VMEM((COL_U, COL_S), F32)])


def _fill_block_diag(bd_ref, blocks_ref, stage):
    r, c = blocks_ref.shape[1:]
    stage[...] = jnp.zeros(stage.shape, F32)
    for gl in range(GROUPS_PER_COL):
        stage[pl.ds(gl * r, r), pl.ds(gl * c, c)] = blocks_ref[gl]
    bd_ref[...] = stage[:, :GROUPS_PER_COL * c].astype(BF16)


def _take_block_diag(out_ref, mat):
    r, c = out_ref.shape[1:]
    for gl in range(GROUPS_PER_COL):
        out_ref[gl] = mat[gl * r:(gl + 1) * r, gl * c:(gl + 1) * c]


def _s5_fill_states(u_ref, bre_ref, bim_ref, ar_ref, ai_ref, s_re, s_im, lseg, n_chunks, chunk):
    def fill(cidx, carry):
        rows = pl.ds(pl.multiple_of(cidx * chunk, SUBLANES), chunk)
        ub = u_ref[rows, :].astype(BF16)
        s_re[rows, :] = jnp.dot(ub, bre_ref[...], preferred_element_type=F32)
        s_im[rows, :] = jnp.dot(ub, bim_ref[...], preferred_element_type=F32)
        return carry
    lax.fori_loop(0, n_chunks, fill, 0)
    _segment_scan(s_re, s_im, ar_ref[...], ai_ref[...], lseg, reverse=False)


def _s5_forward(u_p, ar, ai, bbr, bbi, c_re, c_im, d_row, glu_w, glub_row):
    lp = u_p.shape[0]
    lseg = lp // SUBLANES
    chunk, n_chunks = 4 * lseg, SUBLANES // 4

    def body(u_ref, ar_ref, ai_ref, bbr_ref, bbi_ref, cr_ref, ci_ref, d_ref, gw_ref, glub_ref,
             ys_ref, s_re, s_im, bre_ref, bim_ref, cre_ref, cim_ref, glu_ref, stage):
        for bd, blocks in ((bre_ref, bbr_ref), (bim_ref, bbi_ref), (cre_ref, cr_ref), (cim_ref, ci_ref),
                           (glu_ref, gw_ref)):
            _fill_block_diag(bd, blocks, stage)
        _s5_fill_states(u_ref, bre_ref, bim_ref, ar_ref, ai_ref, s_re, s_im, lseg, n_chunks, chunk)

        def emit(cidx, carry):
            rows = pl.ds(pl.multiple_of(cidx * chunk, SUBLANES), chunk)
            y = (_dot_nt(s_re[rows, :], cre_ref[...]) - _dot_nt(s_im[rows, :], cim_ref[...])
                 + d_ref[...] * u_ref[rows, :])
            g = _gelu(y)
            gate = _dot(g, glu_ref[...]) + glub_ref[...]
            ys_ref[rows, :] = g * _sigmoid(gate)
            return carry
        lax.fori_loop(0, n_chunks, emit, 0)

    col_u, row_u, row_s, bc_blk, glu_blk = _s5_specs(lp)
    return pl.pallas_call(
        body, name="s5_forward", grid=(N_COL,),
        in_specs=[col_u, row_s, row_s, bc_blk, bc_blk, bc_blk, bc_blk, row_u, glu_blk, row_u],
        out_specs=col_u, out_shape=jax.ShapeDtypeStruct((lp, D_SSM), F32),
        scratch_shapes=[pltpu.VMEM((lp, COL_S), F32), pltpu.VMEM((lp, COL_S), F32)] + _s5_block_diag_scratch(),
        compiler_params=_params(("arbitrary",), VMEM_LIMIT),
    )(u_p, ar, ai, bbr, bbi, c_re, c_im, d_row, glu_w, glub_row)


def _s5_backward(u_p, dys_p, ar, ai, bbr, bbi, c_re, c_im, d_row, glu_w, glub_row):
    lp = u_p.shape[0]
    lseg = lp // SUBLANES
    chunk, n_chunks = 4 * lseg, SUBLANES // 4

    def body(u_ref, dys_ref, ar_ref, ai_ref, bbr_ref, bbi_ref, cr_ref, ci_ref, d_ref, gw_ref, glub_ref,
             du_ref, dar_ref, dai_ref, dbbr_ref, dbbi_ref, dcr_ref, dci_ref, dd_ref, dgw_ref, dglub_ref,
             s_re, s_im, q_re, q_im, bre_ref, bim_ref, cre_ref, cim_ref, glu_ref, stage,
             dbre_ref, dbim_ref, dcre_ref, dcim_ref, dglu_ref):
        for bd, blocks in ((bre_ref, bbr_ref), (bim_ref, bbi_ref), (cre_ref, cr_ref), (cim_ref, ci_ref),
                           (glu_ref, gw_ref)):
            _fill_block_diag(bd, blocks, stage)
        _s5_fill_states(u_ref, bre_ref, bim_ref, ar_ref, ai_ref, s_re, s_im, lseg, n_chunks, chunk)
        for ref in (dcre_ref, dcim_ref, dd_ref, dglu_ref, dglub_ref, dbre_ref, dbim_ref):
            ref[...] = jnp.zeros(ref.shape, F32)

        def mixer_bwd(cidx, carry):
            rows = pl.ds(pl.multiple_of(cidx * chunk, SUBLANES), chunk)
            u = u_ref[rows, :]
            sr, si = s_re[rows, :], s_im[rows, :]
            y = _dot_nt(sr, cre_ref[...]) - _dot_nt(si, cim_ref[...]) + d_ref[...] * u
            g = _gelu(y)
            sg = _sigmoid(_dot(g, glu_ref[...]) + glub_ref[...])
            dout = dys_ref[rows, :]
            dgate = dout * g * sg * (1.0 - sg)
            dy = (dout * sg + _dot_nt(dgate, glu_ref[...])) * _gelu_grad(y)
            dglu_ref[...] += _dot_tn(g, dgate)
            dglub_ref[...] += jnp.sum(dgate, axis=0, keepdims=True)
            dd_ref[...] += jnp.sum(dy * u, axis=0, keepdims=True)
            dcre_ref[...] += _dot_tn(dy, sr)
            dcim_ref[...] -= _dot_tn(dy, si)
            q_re[rows, :] = _dot(dy, cre_ref[...])
            q_im[rows, :] = -_dot(dy, cim_ref[...])
            du_ref[rows, :] = d_ref[...] * dy
            return carry
        lax.fori_loop(0, n_chunks, mixer_bwd, 0)

        row = lax.broadcasted_iota(jnp.int32, (SUBLANES, COL_S), 0)

        def visit(i, qr, qi, acc, is_tile0):
            if is_tile0:
                prev = _tile_rows(lseg - 1)
                pr = jnp.where(row == 0, 0.0, pltpu.roll(s_re[prev, :], 1, 0))
                pi = jnp.where(row == 0, 0.0, pltpu.roll(s_im[prev, :], 1, 0))
            else:
                prev = _tile_rows(i - 1)
                pr, pi = s_re[prev, :], s_im[prev, :]
            return acc[0] + qr * pr + qi * pi, acc[1] + qi * pr - qr * pi

        dar, dai = _segment_scan(q_re, q_im, ar_ref[...], -ai_ref[...], lseg, reverse=True, visit=visit)
        dar_ref[...] = jnp.sum(dar, axis=0, keepdims=True)
        dai_ref[...] = jnp.sum(dai, axis=0, keepdims=True)

        def input_bwd(cidx, carry):
            rows = pl.ds(pl.multiple_of(cidx * chunk, SUBLANES), chunk)
            qr, qi = q_re[rows, :], q_im[rows, :]
            u = u_ref[rows, :]
            du_ref[rows, :] += _dot_nt(qr, bre_ref[...]) + _dot_nt(qi, bim_ref[...])
            dbre_ref[...] += _dot_tn(u, qr)
            dbim_ref[...] += _dot_tn(u, qi)
            return carry
        lax.fori_loop(0, n_chunks, input_bwd, 0)
        for out, acc in ((dbbr_ref, dbre_ref), (dbbi_ref, dbim_ref), (dcr_ref, dcre_ref), (dci_ref, dcim_ref),
                         (dgw_ref, dglu_ref)):
            _take_block_diag(out, acc[...])

    col_u, row_u, row_s, bc_blk, glu_blk = _s5_specs(lp)
    group_mats = jax.ShapeDtypeStruct((SSM_GROUPS, SSM_GROUP, SSM_STATE), F32)
    return pl.pallas_call(
        body, name="s5_backward", grid=(N_COL,),
        in_specs=[col_u, col_u, row_s, row_s, bc_blk, bc_blk, bc_blk, bc_blk, row_u, glu_blk, row_u],
        out_specs=[col_u, row_s, row_s, bc_blk, bc_blk, bc_blk, bc_blk, row_u, glu_blk, row_u],
        out_shape=[jax.ShapeDtypeStruct((lp, D_SSM), F32),
                   jax.ShapeDtypeStruct((1, N_COL * COL_S), F32), jax.ShapeDtypeStruct((1, N_COL * COL_S), F32),
                   group_mats, group_mats, group_mats, group_mats, jax.ShapeDtypeStruct((1, D_SSM), F32),
                   jax.ShapeDtypeStruct((SSM_GROUPS, SSM_GROUP, SSM_GROUP), F32), jax.ShapeDtypeStruct((1, D_SSM), F32)],
        scratch_shapes=([pltpu.VMEM((lp, COL_S), F32)] * 4 + _s5_block_diag_scratch()
                        + [pltpu.VMEM((COL_U, COL_S), F32)] * 4 + [pltpu.VMEM((COL_U, COL_U), F32)]),
        compiler_params=_params(("arbitrary",), VMEM_LIMIT),
    )(u_p, dys_p, ar, ai, bbr, bbi, c_re, c_im, d_row, glu_w, glub_row)


def _window_sum(ext, group, leading):
    n = ext.shape[0]
    s = ext
    for j in range(POOL_GROUPS):
        shift = n - (1 << j) if leading else 1 << j
        s = jnp.where(j <= group, s + pltpu.roll(s, shift, 0), s)
    return s


def _pool_inv_count(tile, window, first_row):
    t = tile * TM + lax.broadcasted_iota(jnp.int32, (TM, 1), 0) - first_row
    return 1.0 / jnp.clip(t + 1, 1, window).astype(F32)


def _pool_specs(lp):
    col = pl.BlockSpec((lp, POOL_DIM), lambda k: (0, k))
    mat = pl.BlockSpec((None, POOL_DIM, POOL_DIM), lambda k: (k, 0, 0))
    row = pl.BlockSpec((None, 1, POOL_DIM), lambda k: (k, 0, 0))
    return col, mat, row


def _pool_forward(v, pool_w, pool_scale, first_row):
    lp = v.shape[0]
    n_tiles = lp // TM

    def body(v_ref, w_ref, sc_ref, yp_ref, vpad):
        group = pl.program_id(0)
        window = jnp.left_shift(2, group)
        vpad[pl.ds(0, POOL_HALO), :] = jnp.zeros((POOL_HALO, POOL_DIM), F32)
        vpad[pl.ds(POOL_HALO, lp), :] = v_ref[...]

        def tile(j, carry):
            start = pl.multiple_of(j * TM, TM)
            ext = vpad[pl.ds(start, TM + POOL_HALO), :]
            sums = _window_sum(ext, group, leading=False)[POOL_HALO:, :]
            p = sums * _pool_inv_count(j, window, first_row) - ext[POOL_HALO:, :]
            yp_ref[pl.ds(start, TM), :] = _dot(p, w_ref[...]) * sc_ref[...]
            return carry
        lax.fori_loop(0, n_tiles, tile, 0)

    col, mat, row = _pool_specs(lp)
    return pl.pallas_call(
        body, name="pool_forward", grid=(POOL_GROUPS,),
        in_specs=[col, mat, row], out_specs=col, out_shape=jax.ShapeDtypeStruct((lp, D_SSM), F32),
        scratch_shapes=[pltpu.VMEM((lp + POOL_HALO, POOL_DIM), F32)],
        compiler_params=_params(("arbitrary",), VMEM_LIMIT),
    )(v, pool_w, pool_scale)


def _pool_backward(v, dyp, pool_w, pool_scale, first_row):
    lp = v.shape[0]
    n_tiles = lp // TM

    def body(v_ref, dyp_ref, w_ref, sc_ref, dv_ref, dw_ref, dsc_ref, vpad, gpad):
        group = pl.program_id(0)
        window = jnp.left_shift(2, group)
        vpad[pl.ds(0, POOL_HALO), :] = jnp.zeros((POOL_HALO, POOL_DIM), F32)
        vpad[pl.ds(POOL_HALO, lp), :] = v_ref[...]
        gpad[pl.ds(lp, POOL_HALO), :] = jnp.zeros((POOL_HALO, POOL_DIM), F32)
        dw_ref[...] = jnp.zeros(dw_ref.shape, F32)
        dsc_ref[...] = jnp.zeros(dsc_ref.shape, F32)

        def linear_bwd(j, carry):
            start = pl.multiple_of(j * TM, TM)
            ext = vpad[pl.ds(start, TM + POOL_HALO), :]
            inv = _pool_inv_count(j, window, first_row)
            p = _window_sum(ext, group, leading=False)[POOL_HALO:, :] * inv - ext[POOL_HALO:, :]
            z = _dot(p, w_ref[...])
            dyp_t = dyp_ref[pl.ds(start, TM), :]
            dz = dyp_t * sc_ref[...]
            dsc_ref[...] += jnp.sum(dyp_t * z, axis=0, keepdims=True)
            dw_ref[...] += _dot_tn(p, dz)
            dp = _dot_nt(dz, w_ref[...])
            gpad[pl.ds(start, TM), :] = dp * inv
            dv_ref[pl.ds(start, TM), :] = -dp
            return carry
        lax.fori_loop(0, n_tiles, linear_bwd, 0)

        def window_bwd(j, carry):
            start = pl.multiple_of(j * TM, TM)
            ext = gpad[pl.ds(start, TM + POOL_HALO), :]
            dv_ref[pl.ds(start, TM), :] += _window_sum(ext, group, leading=True)[:TM, :]
            return carry
        lax.fori_loop(0, n_tiles, window_bwd, 0)

    col, mat, row = _pool_specs(lp)
    return pl.pallas_call(
        body, name="pool_backward", grid=(POOL_GROUPS,),
        in_specs=[col, col, mat, row], out_specs=[col, mat, row],
        out_shape=[jax.ShapeDtypeStruct((lp, D_SSM), F32),
                   jax.ShapeDtypeStruct((POOL_GROUPS, POOL_DIM, POOL_DIM), F32),
                   jax.ShapeDtypeStruct((POOL_GROUPS, 1, POOL_DIM), F32)],
        scratch_shapes=[pltpu.VMEM((lp + POOL_HALO, POOL_DIM), F32)] * 2,
        compiler_params=_params(("arbitrary",), VMEM_LIMIT),
    )(v, dyp, pool_w, pool_scale)


def _row_specs():
    head = _const((HEAD, D_MODEL))
    xrow = pl.BlockSpec((TM, D_MODEL), _xrow)
    full = pl.BlockSpec((TM, D_MODEL), lambda i: (i, 0))
    half = pl.BlockSpec((TM, D_SSM), lambda i: (i, 0))
    return head, xrow, full, half


def _in_proj(head, x, g1, w_in):
    n_tiles = (HEAD + x.shape[0]) // TM
    lp = n_tiles * TM

    def body(head_ref, x_ref, g_ref, w_ref, u_ref, v_ref):
        h0 = jnp.where(pl.program_id(0) == 0, head_ref[...], x_ref[...])
        proj = _dot(h0 * _rstd(h0) * g_ref[...], w_ref[...])
        u_ref[...] = proj[:, :D_SSM]
        v_ref[...] = proj[:, D_SSM:]

    head_s, xrow, _, half = _row_specs()
    return pl.pallas_call(
        body, name="in_proj", grid=(n_tiles,),
        in_specs=[head_s, xrow, _const((1, D_MODEL)), _const((D_MODEL, D_MODEL))],
        out_specs=[half, half], out_shape=[jax.ShapeDtypeStruct((lp, D_SSM), F32)] * 2,
        compiler_params=_params(("parallel",), VMEM_LIMIT),
    )(head, x, g1, w_in)


def _out_proj(head, x, ys, yp, gs, gp, w_out):
    lp = ys.shape[0]

    def body(head_ref, x_ref, ys_ref, yp_ref, gs_ref, gp_ref, w_ref, h1_ref):
        h0 = jnp.where(pl.program_id(0) == 0, head_ref[...], x_ref[...])
        ys_t, yp_t = ys_ref[...], yp_ref[...]
        ms = ys_t * _rstd(ys_t) * gs_ref[...]
        mp = yp_t * _rstd(yp_t) * gp_ref[...]
        h1_ref[...] = h0 + _dot(ms, w_ref[pl.ds(0, D_SSM), :]) + _dot(mp, w_ref[pl.ds(D_SSM, D_SSM), :])

    head_s, xrow, full, half = _row_specs()
    return pl.pallas_call(
        body, name="out_proj", grid=(lp // TM,),
        in_specs=[head_s, xrow, half, half, _const((1, D_SSM)), _const((1, D_SSM)), _const((D_MODEL, D_MODEL))],
        out_specs=full, out_shape=jax.ShapeDtypeStruct((lp, D_MODEL), F32),
        compiler_params=_params(("parallel",), VMEM_LIMIT),
    )(head, x, ys, yp, gs, gp, w_out)


def _load_weights(hbm_refs, vmem_refs, sems):
    @pl.when(pl.program_id(0) == 0)
    def _():
        copies = [pltpu.make_async_copy(h, v, sems.at[n]) for n, (h, v) in enumerate(zip(hbm_refs, vmem_refs))]
        for cp in copies:
            cp.start()
        for cp in copies:
            cp.wait()


def _ffn_scratch():
    return [pltpu.VMEM((D_FF, D_MODEL), BF16)] * 3 + [pltpu.SemaphoreType.DMA((3,))]


def _ff_tile(t):
    return pl.ds(t * FF_TILE, FF_TILE)


def _ffn_forward(h1, g2, wg_t, wu_t, wd):
    lp = h1.shape[0]

    def body(h1_ref, g_ref, wg_hbm, wu_hbm, wd_hbm, ab_ref, n2_ref, h2_ref, wg, wu, wdn, sems):
        _load_weights((wg_hbm, wu_hbm, wd_hbm), (wg, wu, wdn), sems)
        h1_t = h1_ref[...]
        n2 = (h1_t * _rstd(h1_t) * g_ref[...]).astype(BF16)
        n2_ref[...] = n2
        acc = h1_t
        for t in range(D_FF // FF_TILE):
            a = _dot_nt(n2, wg[_ff_tile(t), :])
            b = _dot_nt(n2, wu[_ff_tile(t), :])
            ab_ref[:, _ff_tile(t)] = a.astype(BF16)
            ab_ref[:, pl.ds(D_FF + t * FF_TILE, FF_TILE)] = b.astype(BF16)
            acc = acc + _dot(a * _sigmoid(a) * b, wdn[_ff_tile(t), :])
        h2_ref[...] = acc

    _, _, full, _ = _row_specs()
    wide = pl.BlockSpec((TM, 2 * D_FF), lambda i: (i, 0))
    half_width = pl.BlockSpec((TM, D_MODEL), lambda i: (i, 0))
    return pl.pallas_call(
        body, name="ffn_forward", grid=(lp // TM,),
        in_specs=[full, _const((1, D_MODEL)), ANY, ANY, ANY], out_specs=[wide, half_width, full],
        out_shape=[jax.ShapeDtypeStruct((lp, 2 * D_FF), BF16), jax.ShapeDtypeStruct((lp, D_MODEL), BF16),
                   jax.ShapeDtypeStruct((lp, D_MODEL), F32)],
        scratch_shapes=_ffn_scratch(), compiler_params=_params(("arbitrary",), VMEM_LIMIT),
    )(h1, g2, wg_t, wu_t, wd)


def _ffn_backward(h2, target, h1, ab, gf, g2, wg_t, wu_t, wd):
    lp = h1.shape[0]

    def body(h2_ref, t_ref, h1_ref, ab_ref, gf_ref, g2_ref, wg_hbm, wu_hbm, wd_hbm,
             dh1_ref, xt_ref, dh2_ref, loss_ref, dgf_ref, dg2_ref, wg, wu, wdn, sems):
        i = pl.program_id(0)
        _load_weights((wg_hbm, wu_hbm, wd_hbm), (wg, wu, wdn), sems)

        @pl.when(i == 0)
        def _():
            loss_ref[...] = jnp.zeros(loss_ref.shape, F32)
            dgf_ref[...] = jnp.zeros(dgf_ref.shape, F32)
            dg2_ref[...] = jnp.zeros(dg2_ref.shape, F32)

        h2_t = h2_ref[...]
        rf = _rstd(h2_t)
        xf = h2_t * rf
        diff = jnp.where(i == 0, 0.0, xf * gf_ref[...] - t_ref[...])
        loss_ref[...] += 0.5 * jnp.sum(diff * diff) / D_MODEL
        dh2, dgf = _rms_bwd(diff / D_MODEL, xf, rf, gf_ref[...])
        dgf_ref[...] += dgf
        dh2_b = dh2.astype(BF16)
        dh2_ref[...] = dh2_b

        dn2 = jnp.zeros((TM, D_MODEL), F32)
        for t in range(D_FF // FF_TILE):
            dff = _dot_nt(dh2_b, wdn[_ff_tile(t), :])
            a = ab_ref[:, _ff_tile(t)].astype(F32)
            b = ab_ref[:, pl.ds(D_FF + t * FF_TILE, FF_TILE)].astype(F32)
            sg = _sigmoid(a)
            silu = a * sg
            da = dff * b * sg * (1.0 + a * (1.0 - sg))
            db = dff * silu
            for part, val in enumerate((da, db, silu * b)):
                xt_ref[pl.ds(part * D_FF + t * FF_TILE, FF_TILE), :] = val.T.astype(BF16)
            dn2 = dn2 + _dot(da, wg[_ff_tile(t), :]) + _dot(db, wu[_ff_tile(t), :])

        h1_t = h1_ref[...]
        r2 = _rstd(h1_t)
        dx, dg2 = _rms_bwd(dn2, h1_t * r2, r2, g2_ref[...])
        dg2_ref[...] += dg2
        dh1_ref[...] = dh2 + dx

    _, xrow, full, _ = _row_specs()
    wide = pl.BlockSpec((TM, 2 * D_FF), lambda i: (i, 0))
    half_width = pl.BlockSpec((TM, D_MODEL), lambda i: (i, 0))
    vec = _const((1, D_MODEL))
    return pl.pallas_call(
        body, name="ffn_backward", grid=(lp // TM,),
        in_specs=[full, xrow, full, wide, vec, vec, ANY, ANY, ANY],
        out_specs=[full, pl.BlockSpec((3 * D_FF, TM), lambda i: (0, i)), half_width, _const((1, PACK_LANES)), vec, vec],
        out_shape=[jax.ShapeDtypeStruct((lp, D_MODEL), F32),
                   jax.ShapeDtypeStruct((3 * D_FF, lp), BF16),
                   jax.ShapeDtypeStruct((lp, D_MODEL), BF16),
                   jax.ShapeDtypeStruct((1, PACK_LANES), F32),
                   jax.ShapeDtypeStruct((1, D_MODEL), F32), jax.ShapeDtypeStruct((1, D_MODEL), F32)],
        scratch_shapes=_ffn_scratch(), compiler_params=_params(("arbitrary",), VMEM_LIMIT),
    )(h2, target, h1, ab, gf, g2, wg_t, wu_t, wd)


def _ffn_wgrad(xt, n2, dh2):
    lp = n2.shape[0]
    rows = lp // WGRAD_STEPS
    n_tiles = 3 * D_FF // FF_TILE
    shards_per_tile = FF_TILE // FF_SHARD
    gate_up_tiles = 2 * D_FF // FF_TILE

    def body(xt_ref, n2_ref, dh2_ref, out_ref, acc):
        q, k = pl.program_id(0), pl.program_id(1)

        @pl.when(k == 0)
        def _():
            acc[...] = jnp.zeros(acc.shape, F32)

        @pl.when(q < gate_up_tiles)
        def _():
            acc[...] += jnp.dot(xt_ref[...], n2_ref[...], preferred_element_type=F32)

        @pl.when(q >= gate_up_tiles)
        def _():
            acc[...] += jnp.dot(xt_ref[...], dh2_ref[...], preferred_element_type=F32)

        @pl.when(k == pl.num_programs(1) - 1)
        def _():
            for s in range(shards_per_tile):
                out_ref[s] = acc[pl.ds(s * FF_SHARD, FF_SHARD), :].astype(BF16)

    tiles_per_matrix = D_FF // FF_TILE
    return pl.pallas_call(
        body, name="ffn_wgrad", grid=(n_tiles, WGRAD_STEPS),
        in_specs=[pl.BlockSpec((FF_TILE, rows), lambda q, k: (q, k)),
                  pl.BlockSpec((rows, D_MODEL), lambda q, k: (jnp.where(q < gate_up_tiles, k, 0), 0)),
                  pl.BlockSpec((rows, D_MODEL), lambda q, k: (jnp.where(q < gate_up_tiles, 0, k), 0))],
        out_specs=pl.BlockSpec((shards_per_tile, None, FF_SHARD, D_MODEL),
                               lambda q, k: (q % tiles_per_matrix, q // tiles_per_matrix, 0, 0)),
        out_shape=jax.ShapeDtypeStruct((N_DEV, 3, FF_SHARD, D_MODEL), BF16),
        scratch_shapes=[pltpu.VMEM((FF_TILE, D_MODEL), F32)],
        compiler_params=_params(("parallel", "arbitrary"), VMEM_LIMIT),
    )(xt, n2, dh2)


def _out_proj_backward(dh1, ys, yp, gs, gp, w_out):
    lp = ys.shape[0]

    def body(dh1_ref, ys_ref, yp_ref, gs_ref, gp_ref, w_ref, dys_ref, dyp_ref, dgs_ref, dgp_ref, dw_out, dw_ref):
        @pl.when(pl.program_id(0) == 0)
        def _():
            dgs_ref[...] = jnp.zeros(dgs_ref.shape, F32)
            dgp_ref[...] = jnp.zeros(dgp_ref.shape, F32)
            dw_ref[...] = jnp.zeros(dw_ref.shape, F32)

        dh1_b = dh1_ref[...].astype(BF16)
        dmix = _dot_nt(dh1_b, w_ref[...])
        for y_ref, g_ref, dy_ref, dg_ref, lo in ((ys_ref, gs_ref, dys_ref, dgs_ref, 0),
                                                 (yp_ref, gp_ref, dyp_ref, dgp_ref, D_SSM)):
            y_t = y_ref[...]
            r = _rstd(y_t)
            xhat = y_t * r
            dy, dg = _rms_bwd(dmix[:, lo:lo + D_SSM], xhat, r, g_ref[...])
            dy_ref[...] = dy
            dg_ref[...] += dg
            dw_ref[pl.ds(lo, D_SSM), :] += _dot_tn(xhat * g_ref[...], dh1_b)

        @pl.when(pl.program_id(0) == pl.num_programs(0) - 1)
        def _():
            dw_out[...] = dw_ref[...].astype(BF16)

    _, _, full, half = _row_specs()
    vec = _const((1, D_SSM))
    return pl.pallas_call(
        body, name="out_proj_backward", grid=(lp // TM,),
        in_specs=[full, half, half, vec, vec, _const((D_MODEL, D_MODEL))],
        out_specs=[half, half, vec, vec, _const((D_MODEL, D_MODEL))],
        out_shape=[jax.ShapeDtypeStruct((lp, D_SSM), F32)] * 2 + [jax.ShapeDtypeStruct((1, D_SSM), F32)] * 2
        + [jax.ShapeDtypeStruct((D_MODEL, D_MODEL), BF16)],
        scratch_shapes=[pltpu.VMEM((D_MODEL, D_MODEL), F32)],
        compiler_params=_params(("arbitrary",), VMEM_LIMIT),
    )(dh1, ys, yp, gs, gp, w_out)


def _in_proj_backward(head, x, du, dv, dh1, g1, w_in):
    lp = du.shape[0]

    def body(head_ref, x_ref, du_ref, dv_ref, dh1_ref, g_ref, w_ref, dx_ref, dhead_ref, dg_ref, dw_out, dw_ref):
        i = pl.program_id(0)

        @pl.when(i == 0)
        def _():
            dg_ref[...] = jnp.zeros(dg_ref.shape, F32)
            dw_ref[...] = jnp.zeros(dw_ref.shape, F32)

        h0 = jnp.where(i == 0, head_ref[...], x_ref[...])
        r = _rstd(h0)
        xhat = h0 * r
        n1 = (xhat * g_ref[...]).astype(BF16)
        du_b, dv_b = du_ref[...].astype(BF16), dv_ref[...].astype(BF16)
        dn1 = _dot_nt(du_b, w_ref[:, pl.ds(0, D_SSM)]) + _dot_nt(dv_b, w_ref[:, pl.ds(D_SSM, D_SSM)])
        dx, dg = _rms_bwd(dn1, xhat, r, g_ref[...])
        dg_ref[...] += dg
        dh0 = dh1_ref[...] + dx
        dx_ref[...] = dh0

        @pl.when(i == 0)
        def _():
            dhead_ref[...] = dh0

        dw_ref[:, pl.ds(0, D_SSM)] += _dot_tn(n1, du_b)
        dw_ref[:, pl.ds(D_SSM, D_SSM)] += _dot_tn(n1, dv_b)

        @pl.when(i == pl.num_programs(0) - 1)
        def _():
            dw_out[...] = dw_ref[...].astype(BF16)

    head_s, xrow, full, half = _row_specs()
    vec = _const((1, D_MODEL))
    mat = _const((D_MODEL, D_MODEL))
    return pl.pallas_call(
        body, name="in_proj_backward", grid=(lp // TM,),
        in_specs=[head_s, xrow, half, half, full, vec, mat],
        out_specs=[xrow, head_s, vec, mat],
        out_shape=[jax.ShapeDtypeStruct(x.shape, F32), jax.ShapeDtypeStruct((HEAD, D_MODEL), F32),
                   jax.ShapeDtypeStruct((1, D_MODEL), F32), jax.ShapeDtypeStruct((D_MODEL, D_MODEL), BF16)],
        scratch_shapes=[pltpu.VMEM((D_MODEL, D_MODEL), F32)],
        compiler_params=_params(("arbitrary",), VMEM_LIMIT),
    )(head, x, du, dv, dh1, g1, w_in)


def _permute_rows(a):
    lp, n = a.shape
    return a.reshape(SUBLANES, lp // SUBLANES, n).transpose(1, 0, 2).reshape(lp, n)


def _unpermute_rows(a):
    lp, n = a.shape
    return a.reshape(lp // SUBLANES, SUBLANES, n).transpose(1, 0, 2).reshape(lp, n)


def _pack(parts, dtype):
    rows = []
    for p in parts:
        flat = p.reshape(-1).astype(dtype)
        pad = (-flat.shape[0]) % PACK_UNIT
        rows.append(jnp.pad(flat, (0, pad)).reshape(-1, PACK_LANES))
    n_rows = sum(r.shape[0] for r in rows)
    if n_rows % 16:
        rows.append(jnp.zeros((8, PACK_LANES), dtype))
    return jnp.concatenate(rows, axis=0)


def _as2d(a):
    return a.reshape(-1, a.shape[-1])


def _unpack(packed, shapes):
    out, row = [], 0
    for shape in shapes:
        size = 1
        for s in shape:
            size *= s
        n_rows = -(-size // PACK_UNIT) * 8
        out.append(packed[row:row + n_rows].reshape(-1)[:size].reshape(shape))
        row += n_rows
    return out


def _to_view(name, a):
    if name in ("ssm_b_re", "ssm_b_im"):
        return a[0].transpose(0, 2, 1).reshape(-1, SSM_STATE)
    if name in ("ssm_d", "ssm_glu_b"):
        return a[0].T
    if name == "ssm_glu_w":
        return a[0].transpose(1, 2, 0).reshape(-1, SSM_GROUPS)
    return _as2d(a)


def _from_view(name, r, shape):
    if name in ("ssm_b_re", "ssm_b_im"):
        return r.reshape(SSM_GROUPS, SSM_GROUP, SSM_STATE).transpose(0, 2, 1).reshape(shape)
    if name in ("ssm_d", "ssm_glu_b"):
        return r.T.reshape(shape)
    if name == "ssm_glu_w":
        return r.reshape(SSM_GROUP, SSM_GROUP, SSM_GROUPS).transpose(2, 0, 1).reshape(shape)
    return r.reshape(shape)


def kernel(x, meta_tokens, norm1_g, w_in, ssm_lambda_re, ssm_lambda_im, ssm_log_step, ssm_b_re, ssm_b_im, ssm_c_re, ssm_c_im, ssm_d, ssm_glu_w, ssm_glu_b, ssm_norm_g, pool_w, pool_scale, pool_norm_g, w_out, norm2_g, w_gate, w_up, w_down, final_norm_g, loss_target, m_meta_tokens, m_norm1_g, m_w_in, m_ssm_lambda_re, m_ssm_lambda_im, m_ssm_log_step, m_ssm_b_re, m_ssm_b_im, m_ssm_c_re, m_ssm_c_im, m_ssm_d, m_ssm_glu_w, m_ssm_glu_b, m_ssm_norm_g, m_pool_w, m_pool_scale, m_pool_norm_g, m_w_out, m_norm2_g, m_w_gate, m_w_up, m_w_down, m_final_norm_g, v_meta_tokens, v_norm1_g, v_w_in, v_ssm_lambda_re, v_ssm_lambda_im, v_ssm_log_step, v_ssm_b_re, v_ssm_b_im, v_ssm_c_re, v_ssm_c_im, v_ssm_d, v_ssm_glu_w, v_ssm_glu_b, v_ssm_norm_g, v_pool_w, v_pool_scale, v_pool_norm_g, v_w_out, v_norm2_g, v_w_gate, v_w_up, v_w_down, v_final_norm_g):
    given = dict(locals())
    weights = {n: given[n] for n in WEIGHT_NAMES}
    n_meta = meta_tokens.shape[0]
    me = 4 * lax.axis_index("x") + 2 * lax.axis_index("y") + lax.axis_index("c")

    shard_rows = w_in.shape[1]
    first = [w_in[0].astype(BF16), w_out[0].astype(BF16), meta_tokens]
    first_make = _push_copies((SIBLING,) + CHIP_PEERS, [False] * 3)
    first_x, first_token = _split_start(first + [_landing(s, False) for s in first], first_make,
                                        3 * (1 + len(CHIP_PEERS)), "gather_w_in_start")

    xs = x[0]
    tgt = loss_target[0]
    first_row = HEAD - n_meta
    g1, g2, gf = norm1_g, norm2_g, final_norm_g.reshape(1, D_MODEL)
    gs, gp = ssm_norm_g, pool_norm_g

    lam_re, lam_im = ssm_lambda_re[0] + first_token[:1, :1], ssm_lambda_im[0]
    log_step = ssm_log_step[0].reshape(SSM_GROUPS, 1)
    b_re = ssm_b_re[0].transpose(0, 2, 1)
    b_im = ssm_b_im[0].transpose(0, 2, 1)
    abr, abi, zr, zi = _s5_disc_a(lam_re, lam_im, log_step)
    zr_col, zi_col = zr.reshape(SSM_GROUPS, 1, SSM_STATE), zi.reshape(SSM_GROUPS, 1, SSM_STATE)
    bbr, bbi = _s5_disc_b(zr_col, zi_col, b_re, b_im)
    s5_consts = (abr.reshape(1, -1), abi.reshape(1, -1), bbr, bbi, ssm_c_re[0], ssm_c_im[0],
                 ssm_d[0].reshape(1, D_SSM), ssm_glu_w[0], ssm_glu_b[0].reshape(1, D_SSM))
    pool_sc = pool_scale[0].reshape(POOL_GROUPS, 1, POOL_DIM)

    first_landed, _ = _split_wait(first_x, first_make, bbr, "gather_w_in_wait")
    first_fwd_make = _forward_copies(3)
    first_fwd, _ = _split_start(list(first_landed[3:]), first_fwd_make, 3 * len(CHIP_PEERS), "gather_w_in_forward_start")
    (w_in_all, w_out_all, meta_all), first_done = _split_wait(first_fwd, first_fwd_make, bbr, "gather_w_in_forward_wait")
    w_in_all = w_in_all.reshape(D_MODEL, D_MODEL)
    w_out_all = w_out_all.reshape(D_MODEL, D_MODEL)
    meta_full = meta_all.transpose(1, 0, 2).reshape(n_meta, D_MODEL)
    head = jnp.concatenate([jnp.zeros((HEAD - n_meta, D_MODEL), F32), meta_full], axis=0)
    shards = [(w_gate[0].T + first_done[:1, :1]).astype(BF16), w_up[0].T.astype(BF16), w_down[0].astype(BF16)]
    n_big = len(shards)
    gather_make = _push_copies((SIBLING,) + CHIP_PEERS, [False] * n_big)
    gather, gather_token = _split_start(shards + [_landing(s, False) for s in shards], gather_make,
                                        n_big * (1 + len(CHIP_PEERS)), "gather_start")

    u, v = _in_proj(head, xs, g1 + gather_token[:1, :1], w_in_all)
    u_p = _permute_rows(u)
    ys_p = _s5_forward(u_p, *s5_consts)
    ys = _unpermute_rows(ys_p)
    yp = _pool_forward(v, pool_w[0], pool_sc, first_row)
    landed, _ = _split_wait(gather, gather_make, yp, "gather_wait")
    forward_make = _forward_copies(n_big)
    forward, forward_token = _split_start(list(landed[n_big:]), forward_make, n_big * len(CHIP_PEERS),
                                          "gather_forward_start")
    h1 = _out_proj(head, xs, ys, yp, gs + forward_token[:1, :1], gp, w_out_all)
    (wg_t, wu_t, wd_all), _ = _split_wait(forward, forward_make, h1, "gather_forward_wait")
    ffn_weights = [w.reshape(D_FF, D_MODEL) for w in (wg_t, wu_t, wd_all)]
    ab, n2, h2 = _ffn_forward(h1, g2, *ffn_weights)

    dh1, xt, dh2, loss_part, d_gf, d_g2 = _ffn_backward(h2, tgt, h1, ab, gf, g2, *ffn_weights)
    d_ffn = _ffn_wgrad(xt, n2, dh2)
    ffn_make = _push_copies(ALL_PEERS, [True])
    ffn_x, ffn_token = _split_start([d_ffn, _landing(d_ffn, True)], ffn_make, len(ALL_PEERS), "ffn_grad_start")
    dys, dyp, d_gs, d_gp, d_wout = _out_proj_backward(dh1, ys, yp, gs + ffn_token[:1, :1], gp, w_out_all)
    dv, d_pool_w, d_pool_sc = _pool_backward(v, dyp, pool_w[0], pool_sc, first_row)
    (du_p, d_ar, d_ai, d_bbr, d_bbi, d_c_re, d_c_im, d_d, d_glu, d_glub) = _s5_backward(
        u_p, _permute_rows(dys), *s5_consts)
    du = _unpermute_rows(du_p)

    d_zr, d_zi, d_b_re, d_b_im = _s5_disc_b_bwd(zr_col, zi_col, b_re, b_im, d_bbr, d_bbi)
    d_lam_re, d_lam_im, d_log_step = _s5_disc_a_bwd(
        lam_re, lam_im, log_step,
        (d_ar.reshape(SSM_GROUPS, SSM_STATE), d_ai.reshape(SSM_GROUPS, SSM_STATE),
         d_zr.reshape(SSM_GROUPS, SSM_STATE), d_zi.reshape(SSM_GROUPS, SSM_STATE)))
    groups_last = lambda row: row.reshape(SSM_GROUPS, SSM_GROUP).T
    small_grads = {
        "ssm_lambda_re": d_lam_re, "ssm_lambda_im": d_lam_im, "ssm_log_step": d_log_step.reshape(1, SSM_GROUPS),
        "ssm_b_re": d_b_re.reshape(-1, SSM_STATE), "ssm_b_im": d_b_im.reshape(-1, SSM_STATE),
        "ssm_c_re": d_c_re.reshape(-1, SSM_STATE), "ssm_c_im": d_c_im.reshape(-1, SSM_STATE),
        "ssm_d": groups_last(d_d), "ssm_glu_w": d_glu.transpose(1, 2, 0).reshape(-1, SSM_GROUPS),
        "ssm_glu_b": groups_last(d_glub),
        "ssm_norm_g": d_gs, "pool_w": d_pool_w.reshape(-1, POOL_DIM), "pool_scale": d_pool_sc.reshape(-1, POOL_DIM),
        "pool_norm_g": d_gp, "norm2_g": d_g2,
    }

    early_names = SMALL_NAMES[1:-1]
    early_pack = _pack([small_grads[n] for n in early_names], BF16)
    d_wout = d_wout.reshape(N_DEV, shard_rows, D_MODEL)
    early_make = _push_copies(ALL_PEERS, [True, False])
    early_x, early_token = _split_start([d_wout, early_pack, _landing(d_wout, True), _landing(early_pack, False)],
                                        early_make, 2 * len(ALL_PEERS), "early_grad_start")
    d_x, d_head, d_g1, d_win = _in_proj_backward(head, xs, du, dv, dh1, g1 + early_token[:1, :1], w_in_all)
    (_, r_ffn), _ = _split_wait(ffn_x, ffn_make, d_g1, "ffn_grad_wait")
    (_, _, r_wout, r_early), _ = _split_wait(early_x, early_make, d_g1, "early_grad_wait")
    d_win = d_win.reshape(N_DEV, shard_rows, D_MODEL)
    late_pack = _pack([d_g1, d_gf, d_head[first_row:], loss_part], F32)
    late_make = _push_copies(ALL_PEERS, [True, False])
    late_x, late_token = _split_start([d_win, late_pack, _landing(d_win, True), _landing(late_pack, False)],
                                      late_make, 2 * len(ALL_PEERS), "late_grad_start")

    results = {}
    res_gate = _adamw_part(r_ffn, 0, w_gate[0].T + late_token[:1, :1], m_w_gate[0].T, v_w_gate[0].T, "adamw_w_gate")
    res_up = _adamw_part(r_ffn, 1, w_up[0].T, m_w_up[0].T, v_w_up[0].T, "adamw_w_up")
    results["w_gate"] = [r.T for r in res_gate]
    results["w_up"] = [r.T for r in res_up]
    results["w_down"] = _adamw_part(r_ffn, 2, w_down[0], m_w_down[0], v_w_down[0], "adamw_w_down")
    results["w_out"] = _adamw(r_wout, w_out[0], m_w_out[0], v_w_out[0], shard_rows, "adamw_w_out")
    done = sum(res[1][:1, :1] for res in (res_gate, res_up, results["w_down"], results["w_out"]))
    (_, _, r_win, r_late), _ = _split_wait(late_x, late_make, done, "late_grad_wait")
    results["w_in"] = _adamw(r_win, w_in[0], m_w_in[0], v_w_in[0], shard_rows, "adamw_w_in")

    sum_early, sum_late = _reduce_slots([r_early, r_late], "small_grad_sums")
    views = lambda prefix: [_to_view(n, given[prefix + n]) for n in SMALL_NAMES]
    w_views = views("")
    g_views = _unpack(sum_early, [w.shape for w in w_views[1:-1]])
    g_norm1, g_final, g_meta_all, loss_row = _unpack(
        sum_late, [norm1_g.shape, (1, D_MODEL), (n_meta, D_MODEL), (1, PACK_LANES)])
    g_views = [g_norm1] + g_views + [g_final]
    res_small = _adamw_many(g_views, w_views, views("m_"), views("v_"), "adamw_small")
    for idx, n in enumerate(SMALL_NAMES):
        results[n] = [_from_view(n, part[idx], weights[n].shape) for part in (g_views,) + tuple(res_small)]
    shard_cols = meta_tokens.shape[1]
    g_meta = lax.dynamic_slice_in_dim(g_meta_all, me * shard_cols, shard_cols, axis=1)
    results["meta_tokens"] = _adamw(g_meta[None], meta_tokens, m_meta_tokens, v_meta_tokens, n_meta, "adamw_meta")

    out = [loss_row[0, 0], d_x[None]]
    for part in range(4):
        for n in WEIGHT_NAMES:
            out.append(results[n][part].reshape(weights[n].shape))
    return tuple(out)
```

```python
import jax
import jax.numpy as jnp
from jax import lax
from jax.experimental import pallas as pl
from jax.experimental.pallas import tpu as pltpu

F32 = jnp.float32
BF16 = jnp.bfloat16

N_DEV = 8
D_MODEL = 1024
D_SSM = 512
SSM_GROUP = 16
SSM_STATE = 64
SSM_GROUPS = 32
POOL_GROUPS = 4
POOL_DIM = 128
COL_U = 128
COL_S = 512
N_COL = D_SSM // COL_U
GROUPS_PER_COL = COL_U // SSM_GROUP
D_FF = 2816
FF_SHARD = D_FF // N_DEV
FF_TILE = D_FF // 2
TM = 256
WGRAD_STEPS = 2
HEAD = TM
SUBLANES = 8
SCAN_UNROLL = 8
POOL_HALO = 16
EPS = 1e-6
STEP_FLOOR = -1e-4
VMEM_LIMIT = 60 * 1024 * 1024

ADAM_LR = 0.001
ADAM_B1 = 0.9
ADAM_B2 = 0.999
ADAM_EPS = 1e-08
ADAM_WD = 0.01
ADAM_STEP = 10

MESH_ID = pl.DeviceIdType.MESH
ANY = pl.BlockSpec(memory_space=pl.ANY)

SMALL_NAMES = ("norm1_g", "ssm_lambda_re", "ssm_lambda_im", "ssm_log_step", "ssm_b_re", "ssm_b_im",
               "ssm_c_re", "ssm_c_im", "ssm_d", "ssm_glu_w", "ssm_glu_b", "ssm_norm_g", "pool_w",
               "pool_scale", "pool_norm_g", "norm2_g", "final_norm_g")
WEIGHT_NAMES = ("meta_tokens", "norm1_g", "w_in", "ssm_lambda_re", "ssm_lambda_im", "ssm_log_step",
                "ssm_b_re", "ssm_b_im", "ssm_c_re", "ssm_c_im", "ssm_d", "ssm_glu_w", "ssm_glu_b",
                "ssm_norm_g", "pool_w", "pool_scale", "pool_norm_g", "w_out", "norm2_g", "w_gate",
                "w_up", "w_down", "final_norm_g")
LANES = 128
PACK_LANES = LANES
PACK_UNIT = 8 * PACK_LANES


def _dot(a, b):
    return jnp.dot(a.astype(BF16), b.astype(BF16), preferred_element_type=F32)


def _dot_nt(a, b):
    return lax.dot_general(a.astype(BF16), b.astype(BF16), (((1,), (1,)), ((), ())), preferred_element_type=F32)


def _dot_tn(a, b):
    return lax.dot_general(a.astype(BF16), b.astype(BF16), (((0,), (0,)), ((), ())), preferred_element_type=F32)


def _sigmoid(x):
    return 1.0 / (1.0 + jnp.exp(-x))


def _rstd(x):
    return lax.rsqrt(jnp.mean(x * x, axis=-1, keepdims=True) + EPS)


def _rms_bwd(dy, xhat, r, g):
    dxh = dy * g
    dx = r * (dxh - xhat * jnp.mean(dxh * xhat, axis=-1, keepdims=True))
    return dx, jnp.sum(dy * xhat, axis=0, keepdims=True)


def _params(sem, vmem=None):
    return pltpu.CompilerParams(dimension_semantics=sem, vmem_limit_bytes=vmem)


def _const(shape):
    return pl.BlockSpec(shape, lambda *_: (0,) * len(shape))


def _xrow(i):
    return (jnp.maximum(i - 1, 0), 0)


HBM = pl.BlockSpec(memory_space=pltpu.HBM)
SEM = pl.BlockSpec(memory_space=pltpu.SEMAPHORE)
EFFECT = pltpu.SideEffectType.DATAFLOW_SIDE_EFFECTING
ALL_PEERS = tuple(range(1, N_DEV))
SIBLING = 1
CHIP_PEERS = (2, 4, 6)


def _me():
    return 4 * lax.axis_index("x") + 2 * lax.axis_index("y") + lax.axis_index("c")


def _peer(k):
    x, y, c = lax.axis_index("x"), lax.axis_index("y"), lax.axis_index("c")
    px = 1 - x if k & 4 else x
    py = 1 - y if k & 2 else y
    pc = 1 - c if k & 1 else c
    return (px, py, pc), 4 * px + 2 * py + pc


def _landing(arr, scatter):
    if scatter:
        own = lax.dynamic_index_in_dim(arr, _me(), 0, keepdims=False)
    else:
        own = arr
    return lax.dynamic_update_index_in_dim(lax.empty((N_DEV,) + own.shape, arr.dtype), own, _me(), 0)


def _push_copies(peers, scatter):
    n_arr = len(scatter)

    def make(refs, send_sems, recv_sems):
        copies = []
        for a in range(n_arr):
            for i, k in enumerate(peers):
                peer_id, peer = _peer(k)
                sem = a * len(peers) + i
                copies.append(pltpu.make_async_remote_copy(
                    src_ref=refs[a].at[peer] if scatter[a] else refs[a], dst_ref=refs[n_arr + a].at[_me()],
                    send_sem=send_sems.at[sem], recv_sem=recv_sems.at[sem],
                    device_id=peer_id, device_id_type=MESH_ID))
        return copies
    return make


def _forward_copies(n_arr):
    def make(refs, send_sems, recv_sems):
        copies = []
        sibling_id, _ = _peer(SIBLING)
        for a in range(n_arr):
            for i, k in enumerate(CHIP_PEERS):
                slot = refs[a].at[_peer(k)[1]]
                sem = a * len(CHIP_PEERS) + i
                copies.append(pltpu.make_async_remote_copy(
                    src_ref=slot, dst_ref=slot, send_sem=send_sems.at[sem], recv_sem=recv_sems.at[sem],
                    device_id=sibling_id, device_id_type=MESH_ID))
        return copies
    return make


def _split_start(operands, make, n_sem, name):
    n_op = len(operands)

    def body(*refs):
        for cp in make(refs[:n_op], refs[n_op], refs[n_op + 1]):
            cp.start()
        refs[-1][...] = jnp.zeros(refs[-1].shape, F32)

    out = pl.pallas_call(
        body, name=name,
        out_shape=(pltpu.SemaphoreType.DMA((n_sem,)), pltpu.SemaphoreType.DMA((n_sem,)),
                   *[pltpu.HBM(t.shape, t.dtype) for t in operands], jax.ShapeDtypeStruct((8, PACK_LANES), F32)),
        in_specs=[HBM] * n_op, out_specs=(SEM, SEM, *[HBM] * n_op, pl.BlockSpec(memory_space=pltpu.VMEM)),
        input_output_aliases={i: 2 + i for i in range(n_op)},
        compiler_params=pltpu.CompilerParams(has_side_effects=EFFECT),
    )(*[pltpu.with_memory_space_constraint(t, pltpu.HBM) for t in operands])
    return out[:-1], out[-1]


def _split_wait(started, make, after, name):
    send_sems, recv_sems, thru = started[0], started[1], started[2:]
    n_op = len(thru)

    def body(*refs):
        for cp in make(refs[:n_op], refs[n_op], refs[n_op + 1]):
            cp.wait_send()
            cp.wait_recv()
        refs[-1][...] = jnp.zeros(refs[-1].shape, F32)

    out = pl.pallas_call(
        body, name=name,
        out_shape=(*[pltpu.HBM(t.shape, t.dtype) for t in thru], jax.ShapeDtypeStruct((8, PACK_LANES), F32)),
        in_specs=[HBM] * n_op + [SEM, SEM, ANY], out_specs=(*[HBM] * n_op, pl.BlockSpec(memory_space=pltpu.VMEM)),
        input_output_aliases={i: i for i in range(n_op)},
        compiler_params=pltpu.CompilerParams(has_side_effects=EFFECT),
    )(*thru, send_sems, recv_sems, after)
    return out[:-1], out[-1]


def _adamw_math(g, w, m, v):
    nm = ADAM_B1 * m + (1.0 - ADAM_B1) * g
    nv = ADAM_B2 * v + (1.0 - ADAM_B2) * (g * g)
    m_hat = nm / (1.0 - ADAM_B1 ** ADAM_STEP)
    v_hat = nv / (1.0 - ADAM_B2 ** ADAM_STEP)
    return -ADAM_LR * (m_hat / (jnp.sqrt(v_hat) + ADAM_EPS) + ADAM_WD * w), nm, nv


def _sum_slots(s_ref):
    g = s_ref[0].astype(F32)
    for s in range(1, s_ref.shape[0]):
        g = g + s_ref[s].astype(F32)
    return g


def _adamw(slots, w, m, v, tile_rows, name):
    n, rows, cols = slots.shape

    def body(s_ref, w_ref, m_ref, v_ref, g_ref, d_ref, nm_ref, nv_ref):
        g = _sum_slots(s_ref)
        g_ref[...] = g
        d_ref[...], nm_ref[...], nv_ref[...] = _adamw_math(g, w_ref[...], m_ref[...], v_ref[...])

    tile = pl.BlockSpec((tile_rows, cols), lambda i: (i, 0))
    return pl.pallas_call(
        body, name=name, grid=(rows // tile_rows,),
        in_specs=[pl.BlockSpec((n, tile_rows, cols), lambda i: (0, i, 0)), tile, tile, tile],
        out_specs=[tile] * 4, out_shape=[jax.ShapeDtypeStruct((rows, cols), F32)] * 4,
        compiler_params=_params(("parallel",), VMEM_LIMIT),
    )(slots, w, m, v)


def _adamw_part(slots, part, w, m, v, name):
    n, _, rows, cols = slots.shape
    tile_cols = 256

    def body(s_ref, w_ref, m_ref, v_ref, g_ref, d_ref, nm_ref, nv_ref):
        g = _sum_slots(s_ref)
        g_ref[...] = g
        d_ref[...], nm_ref[...], nv_ref[...] = _adamw_math(g, w_ref[...], m_ref[...], v_ref[...])

    tile = pl.BlockSpec((rows, tile_cols), lambda i: (0, i))
    return pl.pallas_call(
        body, name=name, grid=(cols // tile_cols,),
        in_specs=[pl.BlockSpec((n, None, rows, tile_cols), lambda i: (0, part, 0, i)), tile, tile, tile],
        out_specs=[tile] * 4, out_shape=[jax.ShapeDtypeStruct((rows, cols), F32)] * 4,
        compiler_params=_params(("parallel",), VMEM_LIMIT),
    )(slots, w, m, v)


def _reduce_slots(slot_arrays, name):
    def body(*refs):
        n_arr = len(refs) // 2
        for s_ref, o_ref in zip(refs[:n_arr], refs[n_arr:]):
            o_ref[...] = _sum_slots(s_ref)
    return pl.pallas_call(
        body, name=name, out_shape=[jax.ShapeDtypeStruct(s.shape[1:], F32) for s in slot_arrays],
        compiler_params=_params(None, VMEM_LIMIT))(*slot_arrays)


def _adamw_many(grads, ws, ms, vs, name):
    n = len(grads)

    def body(*refs):
        ins, outs = refs[:4 * n], refs[4 * n:]
        for i in range(n):
            g, w, m, v = (ins[j * n + i][...] for j in range(4))
            outs[i][...], outs[n + i][...], outs[2 * n + i][...] = _adamw_math(g, w, m, v)

    out = pl.pallas_call(
        body, name=name, out_shape=[jax.ShapeDtypeStruct(w.shape, F32) for w in ws] * 3,
        compiler_params=_params(None, VMEM_LIMIT))(*grads, *ws, *ms, *vs)
    return out[:n], out[n:2 * n], out[2 * n:]


def _disc_a(lam_re, lam_im, log_step):
    lr = jnp.minimum(lam_re, STEP_FLOOR)
    step = jnp.exp(log_step)
    mag = jnp.exp(lr * step)
    ang = lam_im * step
    abr = mag * jnp.cos(ang)
    abi = mag * jnp.sin(ang)
    nr = abr - 1.0
    den = lr * lr + lam_im * lam_im
    cr = (nr * lr + abi * lam_im) / den
    ci = (abi * lr - nr * lam_im) / den
    return abr, abi, cr, ci


def _disc_b(cr, ci, b_re, b_im):
    return cr * b_re - ci * b_im, cr * b_im + ci * b_re


def _s5_disc_a(lam_re, lam_im, log_step):
    def body(lr_ref, li_ref, ls_ref, *outs):
        for o, val in zip(outs, _disc_a(lr_ref[...], li_ref[...], ls_ref[...])):
            o[...] = val
    return pl.pallas_call(body, name="s5_disc_a", out_shape=[jax.ShapeDtypeStruct(lam_re.shape, F32)] * 4)(
        lam_re, lam_im, log_step)


def _s5_disc_a_bwd(lam_re, lam_im, log_step, cts):
    def body(lr_ref, li_ref, ls_ref, c0, c1, c2, c3, dlr_ref, dli_ref, dls_ref):
        _, vjp = jax.vjp(_disc_a, lr_ref[...], li_ref[...], ls_ref[...])
        dlr, dli, dls = vjp((c0[...], c1[...], c2[...], c3[...]))
        dlr_ref[...] = dlr
        dli_ref[...] = dli
        dls_ref[...] = dls
    return pl.pallas_call(
        body, name="s5_disc_a_bwd",
        out_shape=[jax.ShapeDtypeStruct(lam_re.shape, F32), jax.ShapeDtypeStruct(lam_re.shape, F32),
                   jax.ShapeDtypeStruct(log_step.shape, F32)])(lam_re, lam_im, log_step, *cts)


def _s5_disc_b(cr, ci, b_re, b_im):
    def body(cr_ref, ci_ref, br_ref, bi_ref, o_re, o_im):
        o_re[...], o_im[...] = _disc_b(cr_ref[...], ci_ref[...], br_ref[...], bi_ref[...])
    return pl.pallas_call(body, name="s5_disc_b", out_shape=[jax.ShapeDtypeStruct(b_re.shape, F32)] * 2)(
        cr, ci, b_re, b_im)


def _s5_disc_b_bwd(cr, ci, b_re, b_im, d_re, d_im):
    def body(cr_ref, ci_ref, br_ref, bi_ref, dr_ref, di_ref, dcr_ref, dci_ref, dbr_ref, dbi_ref):
        _, vjp = jax.vjp(_disc_b, cr_ref[...], ci_ref[...], br_ref[...], bi_ref[...])
        dcr_ref[...], dci_ref[...], dbr_ref[...], dbi_ref[...] = vjp((dr_ref[...], di_ref[...]))
    return pl.pallas_call(
        body, name="s5_disc_b_bwd",
        out_shape=[jax.ShapeDtypeStruct(cr.shape, F32)] * 2 + [jax.ShapeDtypeStruct(b_re.shape, F32)] * 2)(
            cr, ci, b_re, b_im, d_re, d_im)


def _cmul(ar, ai, br, bi):
    return ar * br - ai * bi, ar * bi + ai * br


def _cpow(ar, ai, n):
    rr, ri = jnp.ones_like(ar), jnp.zeros_like(ai)
    while n:
        if n & 1:
            rr, ri = _cmul(rr, ri, ar, ai)
        n >>= 1
        if n:
            ar, ai = _cmul(ar, ai, ar, ai)
    return rr, ri


def _tile_rows(i):
    if isinstance(i, int):
        return pl.ds(i * SUBLANES, SUBLANES)
    return pl.ds(pl.multiple_of(i * SUBLANES, SUBLANES), SUBLANES)


def _segment_scan(z_re, z_im, ar, ai, lseg, reverse, visit=None):
    shape = (SUBLANES, z_re.shape[1])
    half = lseg // 2
    arb = jnp.broadcast_to(ar, shape)
    aib = jnp.broadcast_to(ai, shape)
    zero = jnp.zeros(shape, F32)
    row = lax.broadcasted_iota(jnp.int32, shape, 0)

    def tiles(k):
        return (lseg - 1 - k, half - 1 - k) if reverse else (k, half + k)

    def advance(tile, sr, si):
        rows = _tile_rows(tile)
        nr, ni = _cmul(arb, aib, sr, si)
        return rows, nr + z_re[rows, :], ni + z_im[rows, :]

    def first_pass(k, carry):
        ta, tb = tiles(k)
        return advance(ta, carry[0], carry[1])[1:] + advance(tb, carry[2], carry[3])[1:]

    def unrolled(step):
        def body(it, carry):
            for j in range(SCAN_UNROLL):
                carry = step(it * SCAN_UNROLL + j, carry)
            return carry
        return body

    n_iter = half // SCAN_UNROLL
    fa_r, fa_i, fb_r, fb_i = lax.fori_loop(0, n_iter, unrolled(first_pass), (zero,) * 4)
    hr, hi = _cpow(arb, aib, half)
    pr, pi = _cmul(hr, hi, hr, hi)
    fr, fi = _cmul(hr, hi, fa_r, fa_i)
    fr, fi = fr + fb_r, fi + fb_i
    cr, ci = zero, zero
    for _ in range(SUBLANES - 1):
        tr, ti = _cmul(pr, pi, cr, ci)
        tr, ti = tr + fr, ti + fi
        if reverse:
            cr = jnp.where(row == SUBLANES - 1, 0.0, pltpu.roll(tr, SUBLANES - 1, 0))
            ci = jnp.where(row == SUBLANES - 1, 0.0, pltpu.roll(ti, SUBLANES - 1, 0))
        else:
            cr = jnp.where(row == 0, 0.0, pltpu.roll(tr, 1, 0))
            ci = jnp.where(row == 0, 0.0, pltpu.roll(ti, 1, 0))

    br, bi = _cmul(hr, hi, cr, ci)
    br, bi = br + fa_r, bi + fa_i

    def second_pass(k, carry, b_is_tile0=False):
        states, acc = list(carry[:4]), carry[4]
        for chain, tile in enumerate(tiles(k)):
            rows, nr, ni = advance(tile, states[2 * chain], states[2 * chain + 1])
            z_re[rows, :] = nr
            z_im[rows, :] = ni
            states[2 * chain], states[2 * chain + 1] = nr, ni
            if visit is not None:
                acc = visit(tile, nr, ni, acc, chain == 1 and b_is_tile0)
        return (*states, acc)

    acc0 = (zero, zero) if visit is not None else 0
    carry = lax.fori_loop(0, n_iter - 1, unrolled(second_pass), (cr, ci, br, bi, acc0))
    for k in range(half - SCAN_UNROLL, half - 1):
        carry = second_pass(k, carry)
    return second_pass(half - 1, carry, b_is_tile0=reverse)[4]


def _gelu(y):
    c = 0.7978845608028654
    return 0.5 * y * (1.0 + jnp.tanh(c * (y + 0.044715 * y * y * y)))


def _gelu_grad(y):
    c = 0.7978845608028654
    th = jnp.tanh(c * (y + 0.044715 * y * y * y))
    return 0.5 * (1.0 + th) + 0.5 * y * (1.0 - th * th) * c * (1.0 + 3.0 * 0.044715 * y * y)


def _s5_specs(lp):
    col_u = pl.BlockSpec((lp, COL_U), lambda j: (0, j))
    row_u = pl.BlockSpec((1, COL_U), lambda j: (0, j))
    row_s = pl.BlockSpec((1, COL_S), lambda j: (0, j))
    bc_blk = pl.BlockSpec((GROUPS_PER_COL, SSM_GROUP, SSM_STATE), lambda j: (j, 0, 0))
    glu_blk = pl.BlockSpec((GROUPS_PER_COL, SSM_GROUP, SSM_GROUP), lambda j: (j, 0, 0))
    return col_u, row_u, row_s, bc_blk, glu_blk


def _s5_block_diag_scratch():
    return ([pltpu.VMEM((COL_U, COL_S), BF16)] * 4 + [pltpu.VMEM((COL_U, COL_U), BF16)]
            + [pltpu.VMEM((COL_U, COL_S), F32)])


def _fill_block_diag(bd_ref, blocks_ref, stage):
    r, c = blocks_ref.shape[1:]
    stage[...] = jnp.zeros(stage.shape, F32)
    for gl in range(GROUPS_PER_COL):
        stage[pl.ds(gl * r, r), pl.ds(gl * c, c)] = blocks_ref[gl]
    bd_ref[...] = stage[:, :GROUPS_PER_COL * c].astype(BF16)


def _take_block_diag(out_ref, mat):
    r, c = out_ref.shape[1:]
    for gl in range(GROUPS_PER_COL):
        out_ref[gl] = mat[gl * r:(gl + 1) * r, gl * c:(gl + 1) * c]


def _s5_fill_states(u_ref, bre_ref, bim_ref, ar_ref, ai_ref, s_re, s_im, lseg, n_chunks, chunk):
    def fill(cidx, carry):
        rows = pl.ds(pl.multiple_of(cidx * chunk, SUBLANES), chunk)
        ub = u_ref[rows, :].astype(BF16)
        s_re[rows, :] = jnp.dot(ub, bre_ref[...], preferred_element_type=F32)
        s_im[rows, :] = jnp.dot(ub, bim_ref[...], preferred_element_type=F32)
        return carry
    lax.fori_loop(0, n_chunks, fill, 0)
    _segment_scan(s_re, s_im, ar_ref[...], ai_ref[...], lseg, reverse=False)


def _s5_forward(u_p, ar, ai, bbr, bbi, c_re, c_im, d_row, glu_w, glub_row):
    lp = u_p.shape[0]
    lseg = lp // SUBLANES
    chunk, n_chunks = 4 * lseg, SUBLANES // 4

    def body(u_ref, ar_ref, ai_ref, bbr_ref, bbi_ref, cr_ref, ci_ref, d_ref, gw_ref, glub_ref,
             ys_ref, s_re, s_im, bre_ref, bim_ref, cre_ref, cim_ref, glu_ref, stage):
        for bd, blocks in ((bre_ref, bbr_ref), (bim_ref, bbi_ref), (cre_ref, cr_ref), (cim_ref, ci_ref),
                           (glu_ref, gw_ref)):
            _fill_block_diag(bd, blocks, stage)
        _s5_fill_states(u_ref, bre_ref, bim_ref, ar_ref, ai_ref, s_re, s_im, lseg, n_chunks, chunk)

        def emit(cidx, carry):
            rows = pl.ds(pl.multiple_of(cidx * chunk, SUBLANES), chunk)
            y = (_dot_nt(s_re[rows, :], cre_ref[...]) - _dot_nt(s_im[rows, :], cim_ref[...])
                 + d_ref[...] * u_ref[rows, :])
            g = _gelu(y)
            gate = _dot(g, glu_ref[...]) + glub_ref[...]
            ys_ref[rows, :] = g * _sigmoid(gate)
            return carry
        lax.fori_loop(0, n_chunks, emit, 0)

    col_u, row_u, row_s, bc_blk, glu_blk = _s5_specs(lp)
    return pl.pallas_call(
        body, name="s5_forward", grid=(N_COL,),
        in_specs=[col_u, row_s, row_s, bc_blk, bc_blk, bc_blk, bc_blk, row_u, glu_blk, row_u],
        out_specs=col_u, out_shape=jax.ShapeDtypeStruct((lp, D_SSM), F32),
        scratch_shapes=[pltpu.VMEM((lp, COL_S), F32), pltpu.VMEM((lp, COL_S), F32)] + _s5_block_diag_scratch(),
        compiler_params=_params(("arbitrary",), VMEM_LIMIT),
    )(u_p, ar, ai, bbr, bbi, c_re, c_im, d_row, glu_w, glub_row)


def _s5_backward(u_p, dys_p, ar, ai, bbr, bbi, c_re, c_im, d_row, glu_w, glub_row):
    lp = u_p.shape[0]
    lseg = lp // SUBLANES
    chunk, n_chunks = 4 * lseg, SUBLANES // 4

    def body(u_ref, dys_ref, ar_ref, ai_ref, bbr_ref, bbi_ref, cr_ref, ci_ref, d_ref, gw_ref, glub_ref,
             du_ref, dar_ref, dai_ref, dbbr_ref, dbbi_ref, dcr_ref, dci_ref, dd_ref, dgw_ref, dglub_ref,
             s_re, s_im, q_re, q_im, bre_ref, bim_ref, cre_ref, cim_ref, glu_ref, stage,
             dbre_ref, dbim_ref, dcre_ref, dcim_ref, dglu_ref):
        for bd, blocks in ((bre_ref, bbr_ref), (bim_ref, bbi_ref), (cre_ref, cr_ref), (cim_ref, ci_ref),
                           (glu_ref, gw_ref)):
            _fill_block_diag(bd, blocks, stage)
        _s5_fill_states(u_ref, bre_ref, bim_ref, ar_ref, ai_ref, s_re, s_im, lseg, n_chunks, chunk)
        for ref in (dcre_ref, dcim_ref, dd_ref, dglu_ref, dglub_ref, dbre_ref, dbim_ref):
            ref[...] = jnp.zeros(ref.shape, F32)

        def mixer_bwd(cidx, carry):
            rows = pl.ds(pl.multiple_of(cidx * chunk, SUBLANES), chunk)
            u = u_ref[rows, :]
            sr, si = s_re[rows, :], s_im[rows, :]
            y = _dot_nt(sr, cre_ref[...]) - _dot_nt(si, cim_ref[...]) + d_ref[...] * u
            g = _gelu(y)
            sg = _sigmoid(_dot(g, glu_ref[...]) + glub_ref[...])
            dout = dys_ref[rows, :]
            dgate = dout * g * sg * (1.0 - sg)
            dy = (dout * sg + _dot_nt(dgate, glu_ref[...])) * _gelu_grad(y)
            dglu_ref[...] += _dot_tn(g, dgate)
            dglub_ref[...] += jnp.sum(dgate, axis=0, keepdims=True)
            dd_ref[...] += jnp.sum(dy * u, axis=0, keepdims=True)
            dcre_ref[...] += _dot_tn(dy, sr)
            dcim_ref[...] -= _dot_tn(dy, si)
            q_re[rows, :] = _dot(dy, cre_ref[...])
            q_im[rows, :] = -_dot(dy, cim_ref[...])
            du_ref[rows, :] = d_ref[...] * dy
            return carry
        lax.fori_loop(0, n_chunks, mixer_bwd, 0)

        row = lax.broadcasted_iota(jnp.int32, (SUBLANES, COL_S), 0)

        def visit(i, qr, qi, acc, is_tile0):
            if is_tile0:
                prev = _tile_rows(lseg - 1)
                pr = jnp.where(row == 0, 0.0, pltpu.roll(s_re[prev, :], 1, 0))
                pi = jnp.where(row == 0, 0.0, pltpu.roll(s_im[prev, :], 1, 0))
            else:
                prev = _tile_rows(i - 1)
                pr, pi = s_re[prev, :], s_im[prev, :]
            return acc[0] + qr * pr + qi * pi, acc[1] + qi * pr - qr * pi

        dar, dai = _segment_scan(q_re, q_im, ar_ref[...], -ai_ref[...], lseg, reverse=True, visit=visit)
        dar_ref[...] = jnp.sum(dar, axis=0, keepdims=True)
        dai_ref[...] = jnp.sum(dai, axis=0, keepdims=True)

        def input_bwd(cidx, carry):
            rows = pl.ds(pl.multiple_of(cidx * chunk, SUBLANES), chunk)
            qr, qi = q_re[rows, :], q_im[rows, :]
            u = u_ref[rows, :]
            du_ref[rows, :] += _dot_nt(qr, bre_ref[...]) + _dot_nt(qi, bim_ref[...])
            dbre_ref[...] += _dot_tn(u, qr)
            dbim_ref[...] += _dot_tn(u, qi)
            return carry
        lax.fori_loop(0, n_chunks, input_bwd, 0)
        for out, acc in ((dbbr_ref, dbre_ref), (dbbi_ref, dbim_ref), (dcr_ref, dcre_ref), (dci_ref, dcim_ref),
                         (dgw_ref, dglu_ref)):
            _take_block_diag(out, acc[...])

    col_u, row_u, row_s, bc_blk, glu_blk = _s5_specs(lp)
    group_mats = jax.ShapeDtypeStruct((SSM_GROUPS, SSM_GROUP, SSM_STATE), F32)
    return pl.pallas_call(
        body, name="s5_backward", grid=(N_COL,),
        in_specs=[col_u, col_u, row_s, row_s, bc_blk, bc_blk, bc_blk, bc_blk, row_u, glu_blk, row_u],
        out_specs=[col_u, row_s, row_s, bc_blk, bc_blk, bc_blk, bc_blk, row_u, glu_blk, row_u],
        out_shape=[jax.ShapeDtypeStruct((lp, D_SSM), F32),
                   jax.ShapeDtypeStruct((1, N_COL * COL_S), F32), jax.ShapeDtypeStruct((1, N_COL * COL_S), F32),
                   group_mats, group_mats, group_mats, group_mats, jax.ShapeDtypeStruct((1, D_SSM), F32),
                   jax.ShapeDtypeStruct((SSM_GROUPS, SSM_GROUP, SSM_GROUP), F32), jax.ShapeDtypeStruct((1, D_SSM), F32)],
        scratch_shapes=([pltpu.VMEM((lp, COL_S), F32)] * 4 + _s5_block_diag_scratch()
                        + [pltpu.VMEM((COL_U, COL_S), F32)] * 4 + [pltpu.VMEM((COL_U, COL_U), F32)]),
        compiler_params=_params(("arbitrary",), VMEM_LIMIT),
    )(u_p, dys_p, ar, ai, bbr, bbi, c_re, c_im, d_row, glu_w, glub_row)


def _window_sum(ext, group, leading):
    n = ext.shape[0]
    s = ext
    for j in range(POOL_GROUPS):
        shift = n - (1 << j) if leading else 1 << j
        s = jnp.where(j <= group, s + pltpu.roll(s, shift, 0), s)
    return s


def _pool_inv_count(tile, window, first_row):
    t = tile * TM + lax.broadcasted_iota(jnp.int32, (TM, 1), 0) - first_row
    return 1.0 / jnp.clip(t + 1, 1, window).astype(F32)


def _pool_specs(lp):
    col = pl.BlockSpec((lp, POOL_DIM), lambda k: (0, k))
    mat = pl.BlockSpec((None, POOL_DIM, POOL_DIM), lambda k: (k, 0, 0))
    row = pl.BlockSpec((None, 1, POOL_DIM), lambda k: (k, 0, 0))
    return col, mat, row


def _pool_forward(v, pool_w, pool_scale, first_row):
    lp = v.shape[0]
    n_tiles = lp // TM

    def body(v_ref, w_ref, sc_ref, yp_ref, vpad):
        group = pl.program_id(0)
        window = jnp.left_shift(2, group)
        vpad[pl.ds(0, POOL_HALO), :] = jnp.zeros((POOL_HALO, POOL_DIM), F32)
        vpad[pl.ds(POOL_HALO, lp), :] = v_ref[...]

        def tile(j, carry):
            start = pl.multiple_of(j * TM, TM)
            ext = vpad[pl.ds(start, TM + POOL_HALO), :]
            sums = _window_sum(ext, group, leading=False)[POOL_HALO:, :]
            p = sums * _pool_inv_count(j, window, first_row) - ext[POOL_HALO:, :]
            yp_ref[pl.ds(start, TM), :] = _dot(p, w_ref[...]) * sc_ref[...]
            return carry
        lax.fori_loop(0, n_tiles, tile, 0)

    col, mat, row = _pool_specs(lp)
    return pl.pallas_call(
        body, name="pool_forward", grid=(POOL_GROUPS,),
        in_specs=[col, mat, row], out_specs=col, out_shape=jax.ShapeDtypeStruct((lp, D_SSM), F32),
        scratch_shapes=[pltpu.VMEM((lp + POOL_HALO, POOL_DIM), F32)],
        compiler_params=_params(("arbitrary",), VMEM_LIMIT),
    )(v, pool_w, pool_scale)


def _pool_backward(v, dyp, pool_w, pool_scale, first_row):
    lp = v.shape[0]
    n_tiles = lp // TM

    def body(v_ref, dyp_ref, w_ref, sc_ref, dv_ref, dw_ref, dsc_ref, vpad, gpad):
        group = pl.program_id(0)
        window = jnp.left_shift(2, group)
        vpad[pl.ds(0, POOL_HALO), :] = jnp.zeros((POOL_HALO, POOL_DIM), F32)
        vpad[pl.ds(POOL_HALO, lp), :] = v_ref[...]
        gpad[pl.ds(lp, POOL_HALO), :] = jnp.zeros((POOL_HALO, POOL_DIM), F32)
        dw_ref[...] = jnp.zeros(dw_ref.shape, F32)
        dsc_ref[...] = jnp.zeros(dsc_ref.shape, F32)

        def linear_bwd(j, carry):
            start = pl.multiple_of(j * TM, TM)
            ext = vpad[pl.ds(start, TM + POOL_HALO), :]
            inv = _pool_inv_count(j, window, first_row)
            p = _window_sum(ext, group, leading=False)[POOL_HALO:, :] * inv - ext[POOL_HALO:, :]
            z = _dot(p, w_ref[...])
            dyp_t = dyp_ref[pl.ds(start, TM), :]
            dz = dyp_t * sc_ref[...]
            dsc_ref[...] += jnp.sum(dyp_t * z, axis=0, keepdims=True)
            dw_ref[...] += _dot_tn(p, dz)
            dp = _dot_nt(dz, w_ref[...])
            gpad[pl.ds(start, TM), :] = dp * inv
            dv_ref[pl.ds(start, TM), :] = -dp
            return carry
        lax.fori_loop(0, n_tiles, linear_bwd, 0)

        def window_bwd(j, carry):
            start = pl.multiple_of(j * TM, TM)
            ext = gpad[pl.ds(start, TM + POOL_HALO), :]
            dv_ref[pl.ds(start, TM), :] += _window_sum(ext, group, leading=True)[:TM, :]
            return carry
        lax.fori_loop(0, n_tiles, window_bwd, 0)

    col, mat, row = _pool_specs(lp)
    return pl.pallas_call(
        body, name="pool_backward", grid=(POOL_GROUPS,),
        in_specs=[col, col, mat, row], out_specs=[col, mat, row],
        out_shape=[jax.ShapeDtypeStruct((lp, D_SSM), F32),
                   jax.ShapeDtypeStruct((POOL_GROUPS, POOL_DIM, POOL_DIM), F32),
                   jax.ShapeDtypeStruct((POOL_GROUPS, 1, POOL_DIM), F32)],
        scratch_shapes=[pltpu.VMEM((lp + POOL_HALO, POOL_DIM), F32)] * 2,
        compiler_params=_params(("arbitrary",), VMEM_LIMIT),
    )(v, dyp, pool_w, pool_scale)


def _row_specs():
    head = _const((HEAD, D_MODEL))
    xrow = pl.BlockSpec((TM, D_MODEL), _xrow)
    full = pl.BlockSpec((TM, D_MODEL), lambda i: (i, 0))
    half = pl.BlockSpec((TM, D_SSM), lambda i: (i, 0))
    return head, xrow, full, half


def _in_proj(head, x, g1, w_in):
    n_tiles = (HEAD + x.shape[0]) // TM
    lp = n_tiles * TM

    def body(head_ref, x_ref, g_ref, w_ref, u_ref, v_ref):
        h0 = jnp.where(pl.program_id(0) == 0, head_ref[...], x_ref[...])
        proj = _dot(h0 * _rstd(h0) * g_ref[...], w_ref[...])
        u_ref[...] = proj[:, :D_SSM]
        v_ref[...] = proj[:, D_SSM:]

    head_s, xrow, _, half = _row_specs()
    return pl.pallas_call(
        body, name="in_proj", grid=(n_tiles,),
        in_specs=[head_s, xrow, _const((1, D_MODEL)), _const((D_MODEL, D_MODEL))],
        out_specs=[half, half], out_shape=[jax.ShapeDtypeStruct((lp, D_SSM), F32)] * 2,
        compiler_params=_params(("parallel",), VMEM_LIMIT),
    )(head, x, g1, w_in)


def _out_proj(head, x, ys, yp, gs, gp, w_out):
    lp = ys.shape[0]

    def body(head_ref, x_ref, ys_ref, yp_ref, gs_ref, gp_ref, w_ref, h1_ref):
        h0 = jnp.where(pl.program_id(0) == 0, head_ref[...], x_ref[...])
        ys_t, yp_t = ys_ref[...], yp_ref[...]
        ms = ys_t * _rstd(ys_t) * gs_ref[...]
        mp = yp_t * _rstd(yp_t) * gp_ref[...]
        h1_ref[...] = h0 + _dot(ms, w_ref[pl.ds(0, D_SSM), :]) + _dot(mp, w_ref[pl.ds(D_SSM, D_SSM), :])

    head_s, xrow, full, half = _row_specs()
    return pl.pallas_call(
        body, name="out_proj", grid=(lp // TM,),
        in_specs=[head_s, xrow, half, half, _const((1, D_SSM)), _const((1, D_SSM)), _const((D_MODEL, D_MODEL))],
        out_specs=full, out_shape=jax.ShapeDtypeStruct((lp, D_MODEL), F32),
        compiler_params=_params(("parallel",), VMEM_LIMIT),
    )(head, x, ys, yp, gs, gp, w_out)


def _load_weights(hbm_refs, vmem_refs, sems):
    @pl.when(pl.program_id(0) == 0)
    def _():
        copies = [pltpu.make_async_copy(h, v, sems.at[n]) for n, (h, v) in enumerate(zip(hbm_refs, vmem_refs))]
        for cp in copies:
            cp.start()
        for cp in copies:
            cp.wait()


def _ffn_scratch():
    return [pltpu.VMEM((D_FF, D_MODEL), BF16)] * 3 + [pltpu.SemaphoreType.DMA((3,))]


def _ff_tile(t):
    return pl.ds(t * FF_TILE, FF_TILE)


def _ffn_forward(h1, g2, wg_t, wu_t, wd):
    lp = h1.shape[0]

    def body(h1_ref, g_ref, wg_hbm, wu_hbm, wd_hbm, ab_ref, n2_ref, h2_ref, wg, wu, wdn, sems):
        _load_weights((wg_hbm, wu_hbm, wd_hbm), (wg, wu, wdn), sems)
        h1_t = h1_ref[...]
        n2 = (h1_t * _rstd(h1_t) * g_ref[...]).astype(BF16)
        n2_ref[...] = n2
        acc = h1_t
        for t in range(D_FF // FF_TILE):
            a = _dot_nt(n2, wg[_ff_tile(t), :])
            b = _dot_nt(n2, wu[_ff_tile(t), :])
            ab_ref[:, _ff_tile(t)] = a.astype(BF16)
            ab_ref[:, pl.ds(D_FF + t * FF_TILE, FF_TILE)] = b.astype(BF16)
            acc = acc + _dot(a * _sigmoid(a) * b, wdn[_ff_tile(t), :])
        h2_ref[...] = acc

    _, _, full, _ = _row_specs()
    wide = pl.BlockSpec((TM, 2 * D_FF), lambda i: (i, 0))
    half_width = pl.BlockSpec((TM, D_MODEL), lambda i: (i, 0))
    return pl.pallas_call(
        body, name="ffn_forward", grid=(lp // TM,),
        in_specs=[full, _const((1, D_MODEL)), ANY, ANY, ANY], out_specs=[wide, half_width, full],
        out_shape=[jax.ShapeDtypeStruct((lp, 2 * D_FF), BF16), jax.ShapeDtypeStruct((lp, D_MODEL), BF16),
                   jax.ShapeDtypeStruct((lp, D_MODEL), F32)],
        scratch_shapes=_ffn_scratch(), compiler_params=_params(("arbitrary",), VMEM_LIMIT),
    )(h1, g2, wg_t, wu_t, wd)


def _ffn_backward(h2, target, h1, ab, gf, g2, wg_t, wu_t, wd):
    lp = h1.shape[0]

    def body(h2_ref, t_ref, h1_ref, ab_ref, gf_ref, g2_ref, wg_hbm, wu_hbm, wd_hbm,
             dh1_ref, xt_ref, dh2_ref, loss_ref, dgf_ref, dg2_ref, wg, wu, wdn, sems):
        i = pl.program_id(0)
        _load_weights((wg_hbm, wu_hbm, wd_hbm), (wg, wu, wdn), sems)

        @pl.when(i == 0)
        def _():
            loss_ref[...] = jnp.zeros(loss_ref.shape, F32)
            dgf_ref[...] = jnp.zeros(dgf_ref.shape, F32)
            dg2_ref[...] = jnp.zeros(dg2_ref.shape, F32)

        h2_t = h2_ref[...]
        rf = _rstd(h2_t)
        xf = h2_t * rf
        diff = jnp.where(i == 0, 0.0, xf * gf_ref[...] - t_ref[...])
        loss_ref[...] += 0.5 * jnp.sum(diff * diff) / D_MODEL
        dh2, dgf = _rms_bwd(diff / D_MODEL, xf, rf, gf_ref[...])
        dgf_ref[...] += dgf
        dh2_b = dh2.astype(BF16)
        dh2_ref[...] = dh2_b

        dn2 = jnp.zeros((TM, D_MODEL), F32)
        for t in range(D_FF // FF_TILE):
            dff = _dot_nt(dh2_b, wdn[_ff_tile(t), :])
            a = ab_ref[:, _ff_tile(t)].astype(F32)
            b = ab_ref[:, pl.ds(D_FF + t * FF_TILE, FF_TILE)].astype(F32)
            sg = _sigmoid(a)
            silu = a * sg
            da = dff * b * sg * (1.0 + a * (1.0 - sg))
            db = dff * silu
            for part, val in enumerate((da, db, silu * b)):
                xt_ref[pl.ds(part * D_FF + t * FF_TILE, FF_TILE), :] = val.T.astype(BF16)
            dn2 = dn2 + _dot(da, wg[_ff_tile(t), :]) + _dot(db, wu[_ff_tile(t), :])

        h1_t = h1_ref[...]
        r2 = _rstd(h1_t)
        dx, dg2 = _rms_bwd(dn2, h1_t * r2, r2, g2_ref[...])
        dg2_ref[...] += dg2
        dh1_ref[...] = dh2 + dx

    _, xrow, full, _ = _row_specs()
    wide = pl.BlockSpec((TM, 2 * D_FF), lambda i: (i, 0))
    half_width = pl.BlockSpec((TM, D_MODEL), lambda i: (i, 0))
    vec = _const((1, D_MODEL))
    return pl.pallas_call(
        body, name="ffn_backward", grid=(lp // TM,),
        in_specs=[full, xrow, full, wide, vec, vec, ANY, ANY, ANY],
        out_specs=[full, pl.BlockSpec((3 * D_FF, TM), lambda i: (0, i)), half_width, _const((1, PACK_LANES)), vec, vec],
        out_shape=[jax.ShapeDtypeStruct((lp, D_MODEL), F32),
                   jax.ShapeDtypeStruct((3 * D_FF, lp), BF16),
                   jax.ShapeDtypeStruct((lp, D_MODEL), BF16),
                   jax.ShapeDtypeStruct((1, PACK_LANES), F32),
                   jax.ShapeDtypeStruct((1, D_MODEL), F32), jax.ShapeDtypeStruct((1, D_MODEL), F32)],
        scratch_shapes=_ffn_scratch(), compiler_params=_params(("arbitrary",), VMEM_LIMIT),
    )(h2, target, h1, ab, gf, g2, wg_t, wu_t, wd)


def _ffn_wgrad(xt, n2, dh2):
    lp = n2.shape[0]
    rows = lp // WGRAD_STEPS
    n_tiles = 3 * D_FF // FF_TILE
    shards_per_tile = FF_TILE // FF_SHARD
    gate_up_tiles = 2 * D_FF // FF_TILE

    def body(xt_ref, n2_ref, dh2_ref, out_ref, acc):
        q, k = pl.program_id(0), pl.program_id(1)

        @pl.when(k == 0)
        def _():
            acc[...] = jnp.zeros(acc.shape, F32)

        @pl.when(q < gate_up_tiles)
        def _():
            acc[...] += jnp.dot(xt_ref[...], n2_ref[...], preferred_element_type=F32)

        @pl.when(q >= gate_up_tiles)
        def _():
            acc[...] += jnp.dot(xt_ref[...], dh2_ref[...], preferred_element_type=F32)

        @pl.when(k == pl.num_programs(1) - 1)
        def _():
            for s in range(shards_per_tile):
                out_ref[s] = acc[pl.ds(s * FF_SHARD, FF_SHARD), :].astype(BF16)

    tiles_per_matrix = D_FF // FF_TILE
    return pl.pallas_call(
        body, name="ffn_wgrad", grid=(n_tiles, WGRAD_STEPS),
        in_specs=[pl.BlockSpec((FF_TILE, rows), lambda q, k: (q, k)),
                  pl.BlockSpec((rows, D_MODEL), lambda q, k: (jnp.where(q < gate_up_tiles, k, 0), 0)),
                  pl.BlockSpec((rows, D_MODEL), lambda q, k: (jnp.where(q < gate_up_tiles, 0, k), 0))],
        out_specs=pl.BlockSpec((shards_per_tile, None, FF_SHARD, D_MODEL),
                               lambda q, k: (q % tiles_per_matrix, q // tiles_per_matrix, 0, 0)),
        out_shape=jax.ShapeDtypeStruct((N_DEV, 3, FF_SHARD, D_MODEL), BF16),
        scratch_shapes=[pltpu.VMEM((FF_TILE, D_MODEL), F32)],
        compiler_params=_params(("parallel", "arbitrary"), VMEM_LIMIT),
    )(xt, n2, dh2)


def _out_proj_backward(dh1, ys, yp, gs, gp, w_out):
    lp = ys.shape[0]

    def body(dh1_ref, ys_ref, yp_ref, gs_ref, gp_ref, w_ref, dys_ref, dyp_ref, dgs_ref, dgp_ref, dw_out, dw_ref):
        @pl.when(pl.program_id(0) == 0)
        def _():
            dgs_ref[...] = jnp.zeros(dgs_ref.shape, F32)
            dgp_ref[...] = jnp.zeros(dgp_ref.shape, F32)
            dw_ref[...] = jnp.zeros(dw_ref.shape, F32)

        dh1_b = dh1_ref[...].astype(BF16)
        dmix = _dot_nt(dh1_b, w_ref[...])
        for y_ref, g_ref, dy_ref, dg_ref, lo in ((ys_ref, gs_ref, dys_ref, dgs_ref, 0),
                                                 (yp_ref, gp_ref, dyp_ref, dgp_ref, D_SSM)):
            y_t = y_ref[...]
            r = _rstd(y_t)
            xhat = y_t * r
            dy, dg = _rms_bwd(dmix[:, lo:lo + D_SSM], xhat, r, g_ref[...])
            dy_ref[...] = dy
            dg_ref[...] += dg
            dw_ref[pl.ds(lo, D_SSM), :] += _dot_tn(xhat * g_ref[...], dh1_b)

        @pl.when(pl.program_id(0) == pl.num_programs(0) - 1)
        def _():
            dw_out[...] = dw_ref[...].astype(BF16)

    _, _, full, half = _row_specs()
    vec = _const((1, D_SSM))
    return pl.pallas_call(
        body, name="out_proj_backward", grid=(lp // TM,),
        in_specs=[full, half, half, vec, vec, _const((D_MODEL, D_MODEL))],
        out_specs=[half, half, vec, vec, _const((D_MODEL, D_MODEL))],
        out_shape=[jax.ShapeDtypeStruct((lp, D_SSM), F32)] * 2 + [jax.ShapeDtypeStruct((1, D_SSM), F32)] * 2
        + [jax.ShapeDtypeStruct((D_MODEL, D_MODEL), BF16)],
        scratch_shapes=[pltpu.VMEM((D_MODEL, D_MODEL), F32)],
        compiler_params=_params(("arbitrary",), VMEM_LIMIT),
    )(dh1, ys, yp, gs, gp, w_out)


def _in_proj_backward(head, x, du, dv, dh1, g1, w_in):
    lp = du.shape[0]

    def body(head_ref, x_ref, du_ref, dv_ref, dh1_ref, g_ref, w_ref, dx_ref, dhead_ref, dg_ref, dw_out, dw_ref):
        i = pl.program_id(0)

        @pl.when(i == 0)
        def _():
            dg_ref[...] = jnp.zeros(dg_ref.shape, F32)
            dw_ref[...] = jnp.zeros(dw_ref.shape, F32)

        h0 = jnp.where(i == 0, head_ref[...], x_ref[...])
        r = _rstd(h0)
        xhat = h0 * r
        n1 = (xhat * g_ref[...]).astype(BF16)
        du_b, dv_b = du_ref[...].astype(BF16), dv_ref[...].astype(BF16)
        dn1 = _dot_nt(du_b, w_ref[:, pl.ds(0, D_SSM)]) + _dot_nt(dv_b, w_ref[:, pl.ds(D_SSM, D_SSM)])
        dx, dg = _rms_bwd(dn1, xhat, r, g_ref[...])
        dg_ref[...] += dg
        dh0 = dh1_ref[...] + dx
        dx_ref[...] = dh0

        @pl.when(i == 0)
        def _():
            dhead_ref[...] = dh0

        dw_ref[:, pl.ds(0, D_SSM)] += _dot_tn(n1, du_b)
        dw_ref[:, pl.ds(D_SSM, D_SSM)] += _dot_tn(n1, dv_b)

        @pl.when(i == pl.num_programs(0) - 1)
        def _():
            dw_out[...] = dw_ref[...].astype(BF16)

    head_s, xrow, full, half = _row_specs()
    vec = _const((1, D_MODEL))
    mat = _const((D_MODEL, D_MODEL))
    return pl.pallas_call(
        body, name="in_proj_backward", grid=(lp // TM,),
        in_specs=[head_s, xrow, half, half, full, vec, mat],
        out_specs=[xrow, head_s, vec, mat],
        out_shape=[jax.ShapeDtypeStruct(x.shape, F32), jax.ShapeDtypeStruct((HEAD, D_MODEL), F32),
                   jax.ShapeDtypeStruct((1, D_MODEL), F32), jax.ShapeDtypeStruct((D_MODEL, D_MODEL), BF16)],
        scratch_shapes=[pltpu.VMEM((D_MODEL, D_MODEL), F32)],
        compiler_params=_params(("arbitrary",), VMEM_LIMIT),
    )(head, x, du, dv, dh1, g1, w_in)


def _permute_rows(a):
    lp, n = a.shape
    return a.reshape(SUBLANES, lp // SUBLANES, n).transpose(1, 0, 2).reshape(lp, n)


def _unpermute_rows(a):
    lp, n = a.shape
    return a.reshape(lp // SUBLANES, SUBLANES, n).transpose(1, 0, 2).reshape(lp, n)


def _pack(parts, dtype):
    rows = []
    for p in parts:
        flat = p.reshape(-1).astype(dtype)
        pad = (-flat.shape[0]) % PACK_UNIT
        rows.append(jnp.pad(flat, (0, pad)).reshape(-1, PACK_LANES))
    n_rows = sum(r.shape[0] for r in rows)
    if n_rows % 16:
        rows.append(jnp.zeros((8, PACK_LANES), dtype))
    return jnp.concatenate(rows, axis=0)


def _as2d(a):
    return a.reshape(-1, a.shape[-1])


def _unpack(packed, shapes):
    out, row = [], 0
    for shape in shapes:
        size = 1
        for s in shape:
            size *= s
        n_rows = -(-size // PACK_UNIT) * 8
        out.append(packed[row:row + n_rows].reshape(-1)[:size].reshape(shape))
        row += n_rows
    return out


def _to_view(name, a):
    if name in ("ssm_b_re", "ssm_b_im"):
        return a[0].transpose(0, 2, 1).reshape(-1, SSM_STATE)
    if name in ("ssm_d", "ssm_glu_b"):
        return a[0].T
    if name == "ssm_glu_w":
        return a[0].transpose(1, 2, 0).reshape(-1, SSM_GROUPS)
    return _as2d(a)


def _from_view(name, r, shape):
    if name in ("ssm_b_re", "ssm_b_im"):
        return r.reshape(SSM_GROUPS, SSM_GROUP, SSM_STATE).transpose(0, 2, 1).reshape(shape)
    if name in ("ssm_d", "ssm_glu_b"):
        return r.T.reshape(shape)
    if name == "ssm_glu_w":
        return r.reshape(SSM_GROUP, SSM_GROUP, SSM_GROUPS).transpose(2, 0, 1).reshape(shape)
    return r.reshape(shape)


def kernel(x, meta_tokens, norm1_g, w_in, ssm_lambda_re, ssm_lambda_im, ssm_log_step, ssm_b_re, ssm_b_im, ssm_c_re, ssm_c_im, ssm_d, ssm_glu_w, ssm_glu_b, ssm_norm_g, pool_w, pool_scale, pool_norm_g, w_out, norm2_g, w_gate, w_up, w_down, final_norm_g, loss_target, m_meta_tokens, m_norm1_g, m_w_in, m_ssm_lambda_re, m_ssm_lambda_im, m_ssm_log_step, m_ssm_b_re, m_ssm_b_im, m_ssm_c_re, m_ssm_c_im, m_ssm_d, m_ssm_glu_w, m_ssm_glu_b, m_ssm_norm_g, m_pool_w, m_pool_scale, m_pool_norm_g, m_w_out, m_norm2_g, m_w_gate, m_w_up, m_w_down, m_final_norm_g, v_meta_tokens, v_norm1_g, v_w_in, v_ssm_lambda_re, v_ssm_lambda_im, v_ssm_log_step, v_ssm_b_re, v_ssm_b_im, v_ssm_c_re, v_ssm_c_im, v_ssm_d, v_ssm_glu_w, v_ssm_glu_b, v_ssm_norm_g, v_pool_w, v_pool_scale, v_pool_norm_g, v_w_out, v_norm2_g, v_w_gate, v_w_up, v_w_down, v_final_norm_g):
    given = dict(locals())
    weights = {n: given[n] for n in WEIGHT_NAMES}
    n_meta = meta_tokens.shape[0]
    me = 4 * lax.axis_index("x") + 2 * lax.axis_index("y") + lax.axis_index("c")

    shard_rows = w_in.shape[1]
    first = [w_in[0].astype(BF16), w_out[0].astype(BF16), meta_tokens]
    first_make = _push_copies((SIBLING,) + CHIP_PEERS, [False] * 3)
    first_x, first_token = _split_start(first + [_landing(s, False) for s in first], first_make,
                                        3 * (1 + len(CHIP_PEERS)), "gather_w_in_start")

    xs = x[0]
    tgt = loss_target[0]
    first_row = HEAD - n_meta
    g1, g2, gf = norm1_g, norm2_g, final_norm_g.reshape(1, D_MODEL)
    gs, gp = ssm_norm_g, pool_norm_g

    lam_re, lam_im = ssm_lambda_re[0] + first_token[:1, :1], ssm_lambda_im[0]
    log_step = ssm_log_step[0].reshape(SSM_GROUPS, 1)
    b_re = ssm_b_re[0].transpose(0, 2, 1)
    b_im = ssm_b_im[0].transpose(0, 2, 1)
    abr, abi, zr, zi = _s5_disc_a(lam_re, lam_im, log_step)
    zr_col, zi_col = zr.reshape(SSM_GROUPS, 1, SSM_STATE), zi.reshape(SSM_GROUPS, 1, SSM_STATE)
    bbr, bbi = _s5_disc_b(zr_col, zi_col, b_re, b_im)
    s5_consts = (abr.reshape(1, -1), abi.reshape(1, -1), bbr, bbi, ssm_c_re[0], ssm_c_im[0],
                 ssm_d[0].reshape(1, D_SSM), ssm_glu_w[0], ssm_glu_b[0].reshape(1, D_SSM))
    pool_sc = pool_scale[0].reshape(POOL_GROUPS, 1, POOL_DIM)

    first_landed, _ = _split_wait(first_x, first_make, bbr, "gather_w_in_wait")
    first_fwd_make = _forward_copies(3)
    first_fwd, _ = _split_start(list(first_landed[3:]), first_fwd_make, 3 * len(CHIP_PEERS), "gather_w_in_forward_start")
    (w_in_all, w_out_all, meta_all), first_done = _split_wait(first_fwd, first_fwd_make, bbr, "gather_w_in_forward_wait")
    w_in_all = w_in_all.reshape(D_MODEL, D_MODEL)
    w_out_all = w_out_all.reshape(D_MODEL, D_MODEL)
    meta_full = meta_all.transpose(1, 0, 2).reshape(n_meta, D_MODEL)
    head = jnp.concatenate([jnp.zeros((HEAD - n_meta, D_MODEL), F32), meta_full], axis=0)
    shards = [(w_gate[0].T + first_done[:1, :1]).astype(BF16), w_up[0].T.astype(BF16), w_down[0].astype(BF16)]
    n_big = len(shards)
    gather_make = _push_copies((SIBLING,) + CHIP_PEERS, [False] * n_big)
    gather, gather_token = _split_start(shards + [_landing(s, False) for s in shards], gather_make,
                                        n_big * (1 + len(CHIP_PEERS)), "gather_start")

    u, v = _in_proj(head, xs, g1 + gather_token[:1, :1], w_in_all)
    u_p = _permute_rows(u)
    ys_p = _s5_forward(u_p, *s5_consts)
    ys = _unpermute_rows(ys_p)
    yp = _pool_forward(v, pool_w[0], pool_sc, first_row)
    landed, _ = _split_wait(gather, gather_make, yp, "gather_wait")
    forward_make = _forward_copies(n_big)
    forward, forward_token = _split_start(list(landed[n_big:]), forward_make, n_big * len(CHIP_PEERS),
                                          "gather_forward_start")
    h1 = _out_proj(head, xs, ys, yp, gs + forward_token[:1, :1], gp, w_out_all)
    (wg_t, wu_t, wd_all), _ = _split_wait(forward, forward_make, h1, "gather_forward_wait")
    ffn_weights = [w.reshape(D_FF, D_MODEL) for w in (wg_t, wu_t, wd_all)]
    ab, n2, h2 = _ffn_forward(h1, g2, *ffn_weights)

    dh1, xt, dh2, loss_part, d_gf, d_g2 = _ffn_backward(h2, tgt, h1, ab, gf, g2, *ffn_weights)
    d_ffn = _ffn_wgrad(xt, n2, dh2)
    ffn_make = _push_copies(ALL_PEERS, [True])
    ffn_x, ffn_token = _split_start([d_ffn, _landing(d_ffn, True)], ffn_make, len(ALL_PEERS), "ffn_grad_start")
    dys, dyp, d_gs, d_gp, d_wout = _out_proj_backward(dh1, ys, yp, gs + ffn_token[:1, :1], gp, w_out_all)
    dv, d_pool_w, d_pool_sc = _pool_backward(v, dyp, pool_w[0], pool_sc, first_row)
    (du_p, d_ar, d_ai, d_bbr, d_bbi, d_c_re, d_c_im, d_d, d_glu, d_glub) = _s5_backward(
        u_p, _permute_rows(dys), *s5_consts)
    du = _unpermute_rows(du_p)

    d_zr, d_zi, d_b_re, d_b_im = _s5_disc_b_bwd(zr_col, zi_col, b_re, b_im, d_bbr, d_bbi)
    d_lam_re, d_lam_im, d_log_step = _s5_disc_a_bwd(
        lam_re, lam_im, log_step,
        (d_ar.reshape(SSM_GROUPS, SSM_STATE), d_ai.reshape(SSM_GROUPS, SSM_STATE),
         d_zr.reshape(SSM_GROUPS, SSM_STATE), d_zi.reshape(SSM_GROUPS, SSM_STATE)))
    groups_last = lambda row: row.reshape(SSM_GROUPS, SSM_GROUP).T
    small_grads = {
        "ssm_lambda_re": d_lam_re, "ssm_lambda_im": d_lam_im, "ssm_log_step": d_log_step.reshape(1, SSM_GROUPS),
        "ssm_b_re": d_b_re.reshape(-1, SSM_STATE), "ssm_b_im": d_b_im.reshape(-1, SSM_STATE),
        "ssm_c_re": d_c_re.reshape(-1, SSM_STATE), "ssm_c_im": d_c_im.reshape(-1, SSM_STATE),
        "ssm_d": groups_last(d_d), "ssm_glu_w": d_glu.transpose(1, 2, 0).reshape(-1, SSM_GROUPS),
        "ssm_glu_b": groups_last(d_glub),
        "ssm_norm_g": d_gs, "pool_w": d_pool_w.reshape(-1, POOL_DIM), "pool_scale": d_pool_sc.reshape(-1, POOL_DIM),
        "pool_norm_g": d_gp, "norm2_g": d_g2,
    }

    early_names = SMALL_NAMES[1:-1]
    early_pack = _pack([small_grads[n] for n in early_names], BF16)
    d_wout = d_wout.reshape(N_DEV, shard_rows, D_MODEL)
    early_make = _push_copies(ALL_PEERS, [True, False])
    early_x, early_token = _split_start([d_wout, early_pack, _landing(d_wout, True), _landing(early_pack, False)],
                                        early_make, 2 * len(ALL_PEERS), "early_grad_start")
    d_x, d_head, d_g1, d_win = _in_proj_backward(head, xs, du, dv, dh1, g1 + early_token[:1, :1], w_in_all)
    (_, r_ffn), _ = _split_wait(ffn_x, ffn_make, d_g1, "ffn_grad_wait")
    (_, _, r_wout, r_early), _ = _split_wait(early_x, early_make, d_g1, "early_grad_wait")
    d_win = d_win.reshape(N_DEV, shard_rows, D_MODEL)
    late_pack = _pack([d_g1, d_gf, d_head[first_row:], loss_part], F32)
    late_make = _push_copies(ALL_PEERS, [True, False])
    late_x, late_token = _split_start([d_win, late_pack, _landing(d_win, True), _landing(late_pack, False)],
                                      late_make, 2 * len(ALL_PEERS), "late_grad_start")

    results = {}
    res_gate = _adamw_part(r_ffn, 0, w_gate[0].T + late_token[:1, :1], m_w_gate[0].T, v_w_gate[0].T, "adamw_w_gate")
    res_up = _adamw_part(r_ffn, 1, w_up[0].T, m_w_up[0].T, v_w_up[0].T, "adamw_w_up")
    results["w_gate"] = [r.T for r in res_gate]
    results["w_up"] = [r.T for r in res_up]
    results["w_down"] = _adamw_part(r_ffn, 2, w_down[0], m_w_down[0], v_w_down[0], "adamw_w_down")
    results["w_out"] = _adamw(r_wout, w_out[0], m_w_out[0], v_w_out[0], shard_rows, "adamw_w_out")
    done = sum(res[1][:1, :1] for res in (res_gate, res_up, results["w_down"], results["w_out"]))
    (_, _, r_win, r_late), _ = _split_wait(late_x, late_make, done, "late_grad_wait")
    results["w_in"] = _adamw(r_win, w_in[0], m_w_in[0], v_w_in[0], shard_rows, "adamw_w_in")

    sum_early, sum_late = _reduce_slots([r_early, r_late], "small_grad_sums")
    views = lambda prefix: [_to_view(n, given[prefix + n]) for n in SMALL_NAMES]
    w_views = views("")
    g_views = _unpack(sum_early, [w.shape for w in w_views[1:-1]])
    g_norm1, g_final, g_meta_all, loss_row = _unpack(
        sum_late, [norm1_g.shape, (1, D_MODEL), (n_meta, D_MODEL), (1, PACK_LANES)])
    g_views = [g_norm1] + g_views + [g_final]
    res_small = _adamw_many(g_views, w_views, views("m_"), views("v_"), "adamw_small")
    for idx, n in enumerate(SMALL_NAMES):
        results[n] = [_from_view(n, part[idx], weights[n].shape) for part in (g_views,) + tuple(res_small)]
    shard_cols = meta_tokens.shape[1]
    g_meta = lax.dynamic_slice_in_dim(g_meta_all, me * shard_cols, shard_cols, axis=1)
    results["meta_tokens"] = _adamw(g_meta[None], meta_tokens, m_meta_tokens, v_meta_tokens, n_meta, "adamw_meta")

    out = [loss_row[0, 0], d_x[None]]
    for part in range(4):
        for n in WEIGHT_NAMES:
            out.append(results[n][part].reshape(weights[n].shape))
    return tuple(out)
```
